```python
import math
import jax, jax.numpy as jnp
from jax import lax
import numpy as np

D_MODEL = 1024
BATCH = 16
SEQ = 2048
DEPTH = 4

HEAD_DIM = 64
N_BRANCHES = 3
LRU_W = D_MODEL
LRU_BLOCKS = D_MODEL // HEAD_DIM
LRU_BLOCK_W = LRU_W // LRU_BLOCKS
CONV_WIDTH = 4
LRU_C = 8.0
SWA_HEADS = D_MODEL // HEAD_DIM
SWA_KV_HEADS = SWA_HEADS // 4
SWA_GROUP = SWA_HEADS // SWA_KV_HEADS
SWA_WINDOW = 128
DIL_HEADS = D_MODEL // HEAD_DIM
DIL_CONFIGS = ((128, 1), (512, 4), (2048, 16))
FF_HIDDEN = int(math.ceil(8 * D_MODEL / 3 / 256) * 256)
DEEPNORM_ALPHA = (2.0 * DEPTH) ** 0.25
DEEPNORM_BETA = (8.0 * DEPTH) ** -0.25
LN_EPS = 1e-5
NEG_INF = -1e30
IN_WIDTHS = (LRU_W, LRU_W,
             SWA_HEADS * HEAD_DIM, SWA_KV_HEADS * HEAD_DIM, SWA_KV_HEADS * HEAD_DIM,
             DIL_HEADS * HEAD_DIM, DIL_HEADS * HEAD_DIM, DIL_HEADS * HEAD_DIM,
             N_BRANCHES * D_MODEL)
IN_WIDTH = sum(IN_WIDTHS)
BRANCH_W = D_MODEL

kernel_name = "hybrid_rglru_swa_sink_dilated_deepnorm"


def layer_norm(x, g, b):
    xf = x.astype(jnp.float32)
    mu = jnp.mean(xf, axis=-1, keepdims=True)
    var = jnp.mean(jnp.square(xf - mu), axis=-1, keepdims=True)
    y = (xf - mu) * lax.rsqrt(var + LN_EPS)
    return (y * g.astype(jnp.float32) + b.astype(jnp.float32)).astype(x.dtype)


def banded_window_attention(q, k, v, window, sink=None):
    bsz, L, hk, g, dh = q.shape
    blk = window
    nb = -(-L // blk)
    pad = nb * blk - L
    if pad:
        q = jnp.pad(q, ((0, 0), (0, pad), (0, 0), (0, 0), (0, 0)))
        k = jnp.pad(k, ((0, 0), (0, pad), (0, 0), (0, 0)))
        v = jnp.pad(v, ((0, 0), (0, pad), (0, 0), (0, 0)))
    qb = q.reshape(bsz, nb, blk, hk, g, dh)
    kb = k.reshape(bsz, nb, blk, hk, dh)
    vb = v.reshape(bsz, nb, blk, hk, dh)

    def with_prev(t):
        prev = jnp.pad(t[:, :-1], ((0, 0), (1, 0), (0, 0), (0, 0), (0, 0)))
        return jnp.concatenate([prev, t], axis=2)

    kk = with_prev(kb)
    vv = with_prev(vb)
    s = jnp.einsum('bnqhgd,bnkhd->bhgnqk', qb, kk,
                   preferred_element_type=jnp.float32) * (dh ** -0.5)
    qi = jnp.arange(blk)[:, None]
    kj = jnp.arange(2 * blk)[None, :]
    rel = qi + blk - kj
    key_exists = (jnp.arange(nb)[:, None, None] > 0) | (kj[None] >= blk)
    mask = (rel >= 0)[None] & (rel <= window)[None] & key_exists
    s = jnp.where(mask, s, NEG_INF)
    m = jnp.max(s, axis=-1)
    if sink is not None:
        sink_b = sink.astype(jnp.float32).reshape(hk, g, 1, 1)
        m = jnp.maximum(m, sink_b)
    p = jnp.exp(s - m[..., None])
    denom = jnp.sum(p, axis=-1)
    if sink is not None:
        denom = denom + jnp.exp(sink_b - m)
    o = jnp.einsum('bhgnqk,bnkhd->bnqhgd', p.astype(vv.dtype), vv,
                   preferred_element_type=jnp.float32)
    den_t = jnp.transpose(denom, (0, 3, 4, 1, 2))
    lse_t = jnp.transpose(m + jnp.log(denom), (0, 3, 4, 1, 2))
    o = (o / den_t[..., None]).reshape(bsz, nb * blk, hk, g, dh)[:, :L]
    lse = lse_t.reshape(bsz, nb * blk, hk, g)[:, :L]
    return o.astype(q.dtype), lse


def dilated_attention(q, k, v):
    bsz, S, H, dh = q.shape
    outs, lses = [], []
    for window, dil in DIL_CONFIGS:
        Ls = S // dil

        def to_sub(t):
            return t.reshape(bsz, Ls, dil, H, dh).transpose(0, 2, 1, 3, 4).reshape(bsz * dil, Ls, H, dh)

        o, lse = banded_window_attention(to_sub(q)[:, :, :, None], to_sub(k), to_sub(v), window // dil)
        o = o[:, :, :, 0].reshape(bsz, dil, Ls, H, dh).transpose(0, 2, 1, 3, 4).reshape(bsz, S, H, dh)
        lse = lse[:, :, :, 0].reshape(bsz, dil, Ls, H).transpose(0, 2, 1, 3).reshape(bsz, S, H)
        outs.append(o)
        lses.append(lse)
    w = jax.nn.softmax(jnp.stack(lses, axis=0), axis=0)
    o = jnp.sum(w[..., None] * jnp.stack(outs, axis=0).astype(jnp.float32), axis=0)
    return o.astype(q.dtype)


def rglru_branch(xr, gate_in, conv_w, conv_b, w_rg, b_rg, w_ig, b_ig, lru_lambda):
    bsz, S, W = xr.shape
    xp = jnp.pad(xr, ((0, 0), (CONV_WIDTH - 1, 0), (0, 0)))
    xc = sum(xp[:, j:j + S] * conv_w[j] for j in range(CONV_WIDTH)) + conv_b
    xh = xc.reshape(bsz, S, LRU_BLOCKS, LRU_BLOCK_W)
    r = jax.nn.sigmoid(jnp.einsum('bshi,hij->bshj', xh, w_rg).reshape(bsz, S, W) + b_rg)
    i = jax.nn.sigmoid(jnp.einsum('bshi,hij->bshj', xh, w_ig).reshape(bsz, S, W) + b_ig)
    log_a = -LRU_C * r.astype(jnp.float32) * jax.nn.softplus(-lru_lambda.astype(jnp.float32))
    a = jnp.exp(log_a)
    mult = jnp.sqrt(-jnp.expm1(2.0 * log_a))
    b = mult * (i * xc).astype(jnp.float32)

    def combine(e1, e2):
        a1, b1 = e1
        a2, b2 = e2
        return a1 * a2, a2 * b1 + b2

    _, h = lax.associative_scan(combine, (a, b), axis=1)
    return (h.astype(xr.dtype) * jax.nn.gelu(gate_in))


def hybrid_mixer(x, w_in, conv_w, conv_b, w_rg, b_rg, w_ig, b_ig, lru_lambda, sinks, w_branch, w_out):
    bsz, S, D = x.shape
    proj = x @ w_in
    split_at = list(np.cumsum(IN_WIDTHS)[:-1])
    lru_x, lru_gate, qb, kb, vb, qc, kc, vc, gates = jnp.split(proj, split_at, axis=-1)
    y_a = rglru_branch(lru_x, lru_gate, conv_w, conv_b, w_rg, b_rg, w_ig, b_ig, lru_lambda)
    o_b, _ = banded_window_attention(qb.reshape(bsz, S, SWA_KV_HEADS, SWA_GROUP, HEAD_DIM),
                                     kb.reshape(bsz, S, SWA_KV_HEADS, HEAD_DIM),
                                     vb.reshape(bsz, S, SWA_KV_HEADS, HEAD_DIM),
                                     SWA_WINDOW, sinks)
    y_b = o_b.reshape(bsz, S, SWA_HEADS * HEAD_DIM)
    y_c = dilated_attention(qc.reshape(bsz, S, DIL_HEADS, HEAD_DIM),
                            kc.reshape(bsz, S, DIL_HEADS, HEAD_DIM),
                            vc.reshape(bsz, S, DIL_HEADS, HEAD_DIM)).reshape(bsz, S, DIL_HEADS * HEAD_DIM)
    ys = jnp.stack([y_a, y_b, y_c], axis=2)
    branch = jnp.einsum('bsnc,ncd->bsnd', ys, w_branch)
    merged = jnp.sum(jax.nn.sigmoid(gates.reshape(bsz, S, N_BRANCHES, D)) * branch, axis=2)
    return merged @ w_out


def swiglu(x, w_ffn_in, w_ffn_out):
    h1, h3 = jnp.split(x @ w_ffn_in, 2, axis=-1)
    return (jax.nn.silu(h1) * h3) @ w_ffn_out


def _fwd_setup_inputs(seed: int = 0) -> dict:
    key = jax.random.key(seed)
    ks = jax.random.split(key, 20)
    f32 = jnp.float32

    def nrm(k, shape, scale):
        return jax.random.normal(k, shape, f32) * scale

    u = jax.random.uniform(ks[7], (DEPTH, LRU_W), f32, 0.9, 0.999)
    return {
        "x": jax.random.normal(ks[0], (BATCH, SEQ, D_MODEL), f32),
        "w_in": nrm(ks[1], (DEPTH, D_MODEL, IN_WIDTH), D_MODEL ** -0.5),
        "conv_w": nrm(ks[2], (DEPTH, CONV_WIDTH, LRU_W), CONV_WIDTH ** -0.5),
        "conv_b": nrm(ks[3], (DEPTH, LRU_W), 0.02),
        "w_rg": nrm(ks[4], (DEPTH, LRU_BLOCKS, LRU_BLOCK_W, LRU_BLOCK_W), LRU_BLOCK_W ** -0.5),
        "b_rg": nrm(ks[5], (DEPTH, LRU_W), 0.02),
        "w_ig": nrm(ks[6], (DEPTH, LRU_BLOCKS, LRU_BLOCK_W, LRU_BLOCK_W), LRU_BLOCK_W ** -0.5),
        "b_ig": nrm(ks[8], (DEPTH, LRU_W), 0.02),
        "lru_lambda": jnp.log(u / (1.0 - u)),
        "sinks": nrm(ks[9], (DEPTH, SWA_HEADS), 0.5),
        "w_branch": nrm(ks[10], (DEPTH, N_BRANCHES, BRANCH_W, D_MODEL), DEEPNORM_BETA * BRANCH_W ** -0.5),
        "w_out": nrm(ks[11], (DEPTH, D_MODEL, D_MODEL), DEEPNORM_BETA * D_MODEL ** -0.5),
        "ln1_g": 1.0 + nrm(ks[12], (DEPTH, D_MODEL), 0.02),
        "ln1_b": nrm(ks[13], (DEPTH, D_MODEL), 0.02),
        "w_ffn_in": nrm(ks[14], (DEPTH, D_MODEL, 2 * FF_HIDDEN), DEEPNORM_BETA * D_MODEL ** -0.5),
        "w_ffn_out": nrm(ks[15], (DEPTH, FF_HIDDEN, D_MODEL), DEEPNORM_BETA * FF_HIDDEN ** -0.5),
        "ln2_g": 1.0 + nrm(ks[16], (DEPTH, D_MODEL), 0.02),
        "ln2_b": nrm(ks[17], (DEPTH, D_MODEL), 0.02),
    }


def _fwd_reference(x, w_in, conv_w, conv_b, w_rg, b_rg, w_ig, b_ig, lru_lambda, sinks, w_branch, w_out,
              ln1_g, ln1_b, w_ffn_in, w_ffn_out, ln2_g, ln2_b):
    for l in range(DEPTH):
        mix = hybrid_mixer(x, w_in[l], conv_w[l], conv_b[l], w_rg[l], b_rg[l], w_ig[l], b_ig[l],
                           lru_lambda[l], sinks[l], w_branch[l], w_out[l])
        x = layer_norm(DEEPNORM_ALPHA * x + mix, ln1_g[l], ln1_b[l])
        ffn = swiglu(x, w_ffn_in[l], w_ffn_out[l])
        x = layer_norm(DEEPNORM_ALPHA * x + ffn, ln2_g[l], ln2_b[l])
    return x


import jax as _jax
import jax.numpy as _jnp

TWIN_FORMAT = 'train_step'
FWD_PARAMS = ['x', 'w_in', 'conv_w', 'conv_b', 'w_rg', 'b_rg', 'w_ig', 'b_ig', 'lru_lambda', 'sinks', 'w_branch', 'w_out', 'ln1_g', 'ln1_b', 'w_ffn_in', 'w_ffn_out', 'ln2_g', 'ln2_b']
TWIN_WEIGHTS = ['w_in', 'conv_w', 'conv_b', 'w_rg', 'b_rg', 'w_ig', 'b_ig', 'lru_lambda', 'sinks', 'w_branch', 'w_out', 'ln1_g', 'ln1_b', 'w_ffn_in', 'w_ffn_out', 'ln2_g', 'ln2_b']
TWIN_DIFF_INPUT = 'x'
TWIN_INPUTS = ['x', 'w_in', 'conv_w', 'conv_b', 'w_rg', 'b_rg', 'w_ig', 'b_ig', 'lru_lambda', 'sinks', 'w_branch', 'w_out', 'ln1_g', 'ln1_b', 'w_ffn_in', 'w_ffn_out', 'ln2_g', 'ln2_b', 'loss_target', 'm_w_in', 'm_conv_w', 'm_conv_b', 'm_w_rg', 'm_b_rg', 'm_w_ig', 'm_b_ig', 'm_lru_lambda', 'm_sinks', 'm_w_branch', 'm_w_out', 'm_ln1_g', 'm_ln1_b', 'm_w_ffn_in', 'm_w_ffn_out', 'm_ln2_g', 'm_ln2_b', 'v_w_in', 'v_conv_w', 'v_conv_b', 'v_w_rg', 'v_b_rg', 'v_w_ig', 'v_b_ig', 'v_lru_lambda', 'v_sinks', 'v_w_branch', 'v_w_out', 'v_ln1_g', 'v_ln1_b', 'v_w_ffn_in', 'v_w_ffn_out', 'v_ln2_g', 'v_ln2_b']
TWIN_OUTPUTS = ['loss', 'grad_x', 'grad_w_in', 'grad_conv_w', 'grad_conv_b', 'grad_w_rg', 'grad_b_rg', 'grad_w_ig', 'grad_b_ig', 'grad_lru_lambda', 'grad_sinks', 'grad_w_branch', 'grad_w_out', 'grad_ln1_g', 'grad_ln1_b', 'grad_w_ffn_in', 'grad_w_ffn_out', 'grad_ln2_g', 'grad_ln2_b', 'delta_w_in', 'delta_conv_w', 'delta_conv_b', 'delta_w_rg', 'delta_b_rg', 'delta_w_ig', 'delta_b_ig', 'delta_lru_lambda', 'delta_sinks', 'delta_w_branch', 'delta_w_out', 'delta_ln1_g', 'delta_ln1_b', 'delta_w_ffn_in', 'delta_w_ffn_out', 'delta_ln2_g', 'delta_ln2_b', 'new_m_w_in', 'new_m_conv_w', 'new_m_conv_b', 'new_m_w_rg', 'new_m_b_rg', 'new_m_w_ig', 'new_m_b_ig', 'new_m_lru_lambda', 'new_m_sinks', 'new_m_w_branch', 'new_m_w_out', 'new_m_ln1_g', 'new_m_ln1_b', 'new_m_w_ffn_in', 'new_m_w_ffn_out', 'new_m_ln2_g', 'new_m_ln2_b', 'new_v_w_in', 'new_v_conv_w', 'new_v_conv_b', 'new_v_w_rg', 'new_v_b_rg', 'new_v_w_ig', 'new_v_b_ig', 'new_v_lru_lambda', 'new_v_sinks', 'new_v_w_branch', 'new_v_w_out', 'new_v_ln1_g', 'new_v_ln1_b', 'new_v_w_ffn_in', 'new_v_w_ffn_out', 'new_v_ln2_g', 'new_v_ln2_b']
TWIN_LEAF_KINDS = {'loss': 'loss', 'grad_x': 'grad_x', 'grad_w_in': 'grad_w', 'grad_conv_w': 'grad_w', 'grad_conv_b': 'grad_w', 'grad_w_rg': 'grad_w', 'grad_b_rg': 'grad_w', 'grad_w_ig': 'grad_w', 'grad_b_ig': 'grad_w', 'grad_lru_lambda': 'grad_w', 'grad_sinks': 'grad_w', 'grad_w_branch': 'grad_w', 'grad_w_out': 'grad_w', 'grad_ln1_g': 'grad_w', 'grad_ln1_b': 'grad_w', 'grad_w_ffn_in': 'grad_w', 'grad_w_ffn_out': 'grad_w', 'grad_ln2_g': 'grad_w', 'grad_ln2_b': 'grad_w', 'delta_w_in': 'delta_w', 'delta_conv_w': 'delta_w', 'delta_conv_b': 'delta_w', 'delta_w_rg': 'delta_w', 'delta_b_rg': 'delta_w', 'delta_w_ig': 'delta_w', 'delta_b_ig': 'delta_w', 'delta_lru_lambda': 'delta_w', 'delta_sinks': 'delta_w', 'delta_w_branch': 'delta_w', 'delta_w_out': 'delta_w', 'delta_ln1_g': 'delta_w', 'delta_ln1_b': 'delta_w', 'delta_w_ffn_in': 'delta_w', 'delta_w_ffn_out': 'delta_w', 'delta_ln2_g': 'delta_w', 'delta_ln2_b': 'delta_w', 'new_m_w_in': 'new_m', 'new_m_conv_w': 'new_m', 'new_m_conv_b': 'new_m', 'new_m_w_rg': 'new_m', 'new_m_b_rg': 'new_m', 'new_m_w_ig': 'new_m', 'new_m_b_ig': 'new_m', 'new_m_lru_lambda': 'new_m', 'new_m_sinks': 'new_m', 'new_m_w_branch': 'new_m', 'new_m_w_out': 'new_m', 'new_m_ln1_g': 'new_m', 'new_m_ln1_b': 'new_m', 'new_m_w_ffn_in': 'new_m', 'new_m_w_ffn_out': 'new_m', 'new_m_ln2_g': 'new_m', 'new_m_ln2_b': 'new_m', 'new_v_w_in': 'new_v', 'new_v_conv_w': 'new_v', 'new_v_conv_b': 'new_v', 'new_v_w_rg': 'new_v', 'new_v_b_rg': 'new_v', 'new_v_w_ig': 'new_v', 'new_v_b_ig': 'new_v', 'new_v_lru_lambda': 'new_v', 'new_v_sinks': 'new_v', 'new_v_w_branch': 'new_v', 'new_v_w_out': 'new_v', 'new_v_ln1_g': 'new_v', 'new_v_ln1_b': 'new_v', 'new_v_w_ffn_in': 'new_v', 'new_v_w_ffn_out': 'new_v', 'new_v_ln2_g': 'new_v', 'new_v_ln2_b': 'new_v'}


def _forward(args):
    return _fwd_reference(*[args[k] for k in FWD_PARAMS])


def _output_shape():
    out = _jax.eval_shape(lambda: _forward(_fwd_setup_inputs(0)))
    return out.shape, out.dtype

N_MICROBATCH = 1
ADAM_LR = 0.001
ADAM_B1 = 0.9
ADAM_B2 = 0.999
ADAM_EPS = 1e-08
ADAM_WD = 0.01
ADAM_STEP = 10
PER_EXAMPLE_BATCH_AXIS = {'x': 0, 'loss_target': 0}
SHARED_INPUTS = []
_WEIGHT_DTYPES = {'w_in': _jnp.float32, 'conv_w': _jnp.float32, 'conv_b': _jnp.float32, 'w_rg': _jnp.float32, 'b_rg': _jnp.float32, 'w_ig': _jnp.float32, 'b_ig': _jnp.float32, 'lru_lambda': _jnp.float32, 'sinks': _jnp.float32, 'w_branch': _jnp.float32, 'w_out': _jnp.float32, 'ln1_g': _jnp.float32, 'ln1_b': _jnp.float32, 'w_ffn_in': _jnp.float32, 'w_ffn_out': _jnp.float32, 'ln2_g': _jnp.float32, 'ln2_b': _jnp.float32}
MOMENT_SCALE = {'w_in': 2.467131e-03, 'conv_w': 4.094725e-03, 'conv_b': 2.209250e-02, 'w_rg': 1.075002e-03, 'b_rg': 9.842157e-04, 'w_ig': 1.902388e-03, 'b_ig': 1.439342e-03, 'lru_lambda': 1.973978e-03, 'sinks': 1.575415e-03, 'w_branch': 6.737515e-03, 'w_out': 1.165080e-02, 'ln1_g': 1.115419e+00, 'ln1_b': 4.696654e-01, 'w_ffn_in': 5.977357e-03, 'w_ffn_out': 9.833446e-03, 'ln2_g': 1.607714e+01, 'ln2_b': 8.135037e-01}


def _to_microbatches(a, axis):
    t = _jnp.moveaxis(a, axis, 0)
    t = t.reshape((N_MICROBATCH, t.shape[0] // N_MICROBATCH) + t.shape[1:])
    return _jnp.moveaxis(t, 1, axis + 1)


def setup_inputs(seed: int = 0) -> dict:
    inp = _fwd_setup_inputs(seed)
    key = _jax.random.fold_in(_jax.random.key(seed), 7919)
    shape, _ = _output_shape()
    out = dict(inp)
    out["loss_target"] = _jax.random.normal(_jax.random.fold_in(key, 0), shape, _jnp.float32)
    for i, name in enumerate(TWIN_WEIGHTS):
        w = inp[name].astype(_jnp.float32)
        if MOMENT_SCALE is None:
            s = _jnp.sqrt(_jnp.mean(_jnp.square(w)) + 1e-30)
        else:
            s = MOMENT_SCALE[name]
        km, kv = _jax.random.split(_jax.random.fold_in(key, i + 1))
        out[name] = w
        out["m_" + name] = s * _jax.random.normal(km, w.shape, _jnp.float32)
        out["v_" + name] = (s * s) * _jax.random.uniform(kv, w.shape, _jnp.float32, 0.5, 1.5)
    if N_MICROBATCH > 1:
        for name, axis in PER_EXAMPLE_BATCH_AXIS.items():
            out[name] = _to_microbatches(out[name], axis)
    return {'x': out['x'], 'w_in': out['w_in'], 'conv_w': out['conv_w'], 'conv_b': out['conv_b'], 'w_rg': out['w_rg'], 'b_rg': out['b_rg'], 'w_ig': out['w_ig'], 'b_ig': out['b_ig'], 'lru_lambda': out['lru_lambda'], 'sinks': out['sinks'], 'w_branch': out['w_branch'], 'w_out': out['w_out'], 'ln1_g': out['ln1_g'], 'ln1_b': out['ln1_b'], 'w_ffn_in': out['w_ffn_in'], 'w_ffn_out': out['w_ffn_out'], 'ln2_g': out['ln2_g'], 'ln2_b': out['ln2_b'], 'loss_target': out['loss_target'], 'm_w_in': out['m_w_in'], 'm_conv_w': out['m_conv_w'], 'm_conv_b': out['m_conv_b'], 'm_w_rg': out['m_w_rg'], 'm_b_rg': out['m_b_rg'], 'm_w_ig': out['m_w_ig'], 'm_b_ig': out['m_b_ig'], 'm_lru_lambda': out['m_lru_lambda'], 'm_sinks': out['m_sinks'], 'm_w_branch': out['m_w_branch'], 'm_w_out': out['m_w_out'], 'm_ln1_g': out['m_ln1_g'], 'm_ln1_b': out['m_ln1_b'], 'm_w_ffn_in': out['m_w_ffn_in'], 'm_w_ffn_out': out['m_w_ffn_out'], 'm_ln2_g': out['m_ln2_g'], 'm_ln2_b': out['m_ln2_b'], 'v_w_in': out['v_w_in'], 'v_conv_w': out['v_conv_w'], 'v_conv_b': out['v_conv_b'], 'v_w_rg': out['v_w_rg'], 'v_b_rg': out['v_b_rg'], 'v_w_ig': out['v_w_ig'], 'v_b_ig': out['v_b_ig'], 'v_lru_lambda': out['v_lru_lambda'], 'v_sinks': out['v_sinks'], 'v_w_branch': out['v_w_branch'], 'v_w_out': out['v_w_out'], 'v_ln1_g': out['v_ln1_g'], 'v_ln1_b': out['v_ln1_b'], 'v_w_ffn_in': out['v_w_ffn_in'], 'v_w_ffn_out': out['v_w_ffn_out'], 'v_ln2_g': out['v_ln2_g'], 'v_ln2_b': out['v_ln2_b']}


def _loss(weights, diff, rest, loss_target):
    with _jax.named_scope("forward"):
        args = {**rest, TWIN_DIFF_INPUT: diff, **{k: w.astype(_WEIGHT_DTYPES[k]) for k, w in weights.items()}}
        y = _forward(args)
    with _jax.named_scope("loss_head"):
        err = _jnp.square(y.astype(_jnp.float32) - loss_target)
        return 0.5 * _jnp.sum(_jnp.mean(err, axis=-1)) if err.ndim else 0.5 * err


def _adamw(w, g, m, v):
    m = ADAM_B1 * m + (1.0 - ADAM_B1) * g
    v = ADAM_B2 * v + (1.0 - ADAM_B2) * _jnp.square(g)
    m_hat = m / (1.0 - ADAM_B1 ** ADAM_STEP)
    v_hat = v / (1.0 - ADAM_B2 ** ADAM_STEP)
    delta = -ADAM_LR * (m_hat / (_jnp.sqrt(v_hat) + ADAM_EPS) + ADAM_WD * w)
    return delta, m, v


def reference(x, w_in, conv_w, conv_b, w_rg, b_rg, w_ig, b_ig, lru_lambda, sinks, w_branch, w_out, ln1_g, ln1_b, w_ffn_in, w_ffn_out, ln2_g, ln2_b, loss_target, m_w_in, m_conv_w, m_conv_b, m_w_rg, m_b_rg, m_w_ig, m_b_ig, m_lru_lambda, m_sinks, m_w_branch, m_w_out, m_ln1_g, m_ln1_b, m_w_ffn_in, m_w_ffn_out, m_ln2_g, m_ln2_b, v_w_in, v_conv_w, v_conv_b, v_w_rg, v_b_rg, v_w_ig, v_b_ig, v_lru_lambda, v_sinks, v_w_branch, v_w_out, v_ln1_g, v_ln1_b, v_w_ffn_in, v_w_ffn_out, v_ln2_g, v_ln2_b):
    given = dict(x=x, w_in=w_in, conv_w=conv_w, conv_b=conv_b, w_rg=w_rg, b_rg=b_rg, w_ig=w_ig, b_ig=b_ig, lru_lambda=lru_lambda, sinks=sinks, w_branch=w_branch, w_out=w_out, ln1_g=ln1_g, ln1_b=ln1_b, w_ffn_in=w_ffn_in, w_ffn_out=w_ffn_out, ln2_g=ln2_g, ln2_b=ln2_b, loss_target=loss_target, m_w_in=m_w_in, m_conv_w=m_conv_w, m_conv_b=m_conv_b, m_w_rg=m_w_rg, m_b_rg=m_b_rg, m_w_ig=m_w_ig, m_b_ig=m_b_ig, m_lru_lambda=m_lru_lambda, m_sinks=m_sinks, m_w_branch=m_w_branch, m_w_out=m_w_out, m_ln1_g=m_ln1_g, m_ln1_b=m_ln1_b, m_w_ffn_in=m_w_ffn_in, m_w_ffn_out=m_w_ffn_out, m_ln2_g=m_ln2_g, m_ln2_b=m_ln2_b, v_w_in=v_w_in, v_conv_w=v_conv_w, v_conv_b=v_conv_b, v_w_rg=v_w_rg, v_b_rg=v_b_rg, v_w_ig=v_w_ig, v_b_ig=v_b_ig, v_lru_lambda=v_lru_lambda, v_sinks=v_sinks, v_w_branch=v_w_branch, v_w_out=v_w_out, v_ln1_g=v_ln1_g, v_ln1_b=v_ln1_b, v_w_ffn_in=v_w_ffn_in, v_w_ffn_out=v_w_ffn_out, v_ln2_g=v_ln2_g, v_ln2_b=v_ln2_b)
    weights = {n: given[n] for n in TWIN_WEIGHTS}
    shared = {n: given[n] for n in SHARED_INPUTS}
    per_example = {n: given[n] for n in ['x']}
    grad_fn = _jax.value_and_grad(_loss, argnums=(0, 1))

    def one_microbatch(ex, loss_target):
        ex = dict(ex)
        diff = ex.pop(TWIN_DIFF_INPUT)
        return grad_fn(weights, diff, {**shared, **ex}, loss_target)

    if N_MICROBATCH == 1:
        loss, (grad_w, grad_x) = one_microbatch(per_example, given["loss_target"])
    else:
        def body(carry, xs):
            loss_sum, grad_sum = carry
            l_k, (gw_k, gx_k) = one_microbatch(xs[0], xs[1])
            with _jax.named_scope("update"):
                return (loss_sum + l_k, _jax.tree.map(_jnp.add, grad_sum, gw_k)), gx_k

        init = (_jnp.zeros((), _jnp.float32), _jax.tree.map(_jnp.zeros_like, weights))
        (loss, grad_w), grad_x = _jax.lax.scan(body, init, (per_example, given["loss_target"]))
    with _jax.named_scope("update"):
        delta_w, new_m, new_v = {}, {}, {}
        for n in TWIN_WEIGHTS:
            delta_w[n], new_m[n], new_v[n] = _adamw(weights[n], grad_w[n], given["m_" + n], given["v_" + n])
    return (loss, grad_x, *[grad_w[n] for n in TWIN_WEIGHTS], *[delta_w[n] for n in TWIN_WEIGHTS],
            *[new_m[n] for n in TWIN_WEIGHTS], *[new_v[n] for n in TWIN_WEIGHTS])
```

```python
import functools
import math

import jax
import jax.numpy as jnp
from jax import lax
from jax.experimental import pallas as pl
from jax.experimental.pallas import tpu as pltpu

F32 = jnp.float32
BF16 = jnp.bfloat16

N_DEV = 8
DEPTH = 4
D_MODEL = 1024
HEAD_DIM = 64
LANES = 128
N_HEADS = D_MODEL // HEAD_DIM
SWA_KV_HEADS = 4
ATT_BLOCK = 128
DILATIONS = (1, 4, 16)
CONV_WIDTH = 4
LRU_C = 8.0
FF_HIDDEN = 2816
ALPHA = (2.0 * DEPTH) ** 0.25
LN_EPS = 1e-5
NEG_INF = -1e30
W_F = 5 * D_MODEL
W_QS = D_MODEL + 2 * SWA_KV_HEADS * HEAD_DIM
W_QD = 3 * D_MODEL

ADAM_LR = 0.001
ADAM_B1 = 0.9
ADAM_B2 = 0.999
ADAM_EPS = 1e-08
ADAM_WD = 0.01
ADAM_STEP = 10

VMEM_LIMIT = 56 * 1024 * 1024
MESH = pl.DeviceIdType.MESH


def _pick(n, cands):
    for c in cands:
        if n % c == 0:
            return c
    raise ValueError(f"no tile for {n} among {cands}")


def _params(sem):
    return pltpu.CompilerParams(dimension_semantics=sem, vmem_limit_bytes=VMEM_LIMIT)


def matmul(a, b, *, name, trans_a=False, out_dtype=F32, add=None, add_scale=1.0):
    if trans_a:
        k_dim, m_dim = a.shape
    else:
        m_dim, k_dim = a.shape
    n_dim = b.shape[1]
    assert b.shape[0] == k_dim
    tm = _pick(m_dim, (512, 256, 128))
    tn = _pick(n_dim, (512, 256, 128))
    tk = _pick(k_dim, (512, 256, 128))
    nk = k_dim // tk

    def body(*refs):
        if add is None:
            a_ref, b_ref, o_ref, acc_ref = refs
            add_ref = None
        else:
            a_ref, b_ref, add_ref, o_ref, acc_ref = refs
        k = pl.program_id(2)

        @pl.when(k == 0)
        def _():
            acc_ref[...] = jnp.zeros_like(acc_ref)

        av = a_ref[...].astype(BF16)
        bv = b_ref[...].astype(BF16)
        if trans_a:
            acc_ref[...] += lax.dot_general(av, bv, (((0,), (0,)), ((), ())), preferred_element_type=F32)
        else:
            acc_ref[...] += jnp.dot(av, bv, preferred_element_type=F32)

        @pl.when(k == nk - 1)
        def _():
            r = acc_ref[...]
            if add_ref is not None:
                r = r + add_scale * add_ref[...].astype(F32)
            o_ref[...] = r.astype(out_dtype)

    a_spec = pl.BlockSpec((tk, tm), lambda i, j, k: (k, i)) if trans_a else pl.BlockSpec((tm, tk), lambda i, j, k: (i, k))
    in_specs = [a_spec, pl.BlockSpec((tk, tn), lambda i, j, k: (k, j))]
    args = [a, b]
    if add is not None:
        in_specs.append(pl.BlockSpec((tm, tn), lambda i, j, k: (i, j)))
        args.append(add)
    return pl.pallas_call(
        body,
        out_shape=jax.ShapeDtypeStruct((m_dim, n_dim), out_dtype),
        grid=(m_dim // tm, n_dim // tn, nk),
        in_specs=in_specs,
        out_specs=pl.BlockSpec((tm, tn), lambda i, j, k: (i, j)),
        scratch_shapes=[pltpu.VMEM((tm, tn), F32)],
        compiler_params=_params(("parallel", "parallel", "arbitrary")),
        name=name,
    )(*args)


def ln_fwd(x, r, g, b, *, name):
    t_dim, d = x.shape
    tr = _pick(t_dim, (256, 128, 8))

    def body(x_ref, r_ref, g_ref, b_ref, y_ref, z_ref):
        z = ALPHA * x_ref[...] + r_ref[...]
        mu = jnp.mean(z, axis=-1, keepdims=True)
        zc = z - mu
        var = jnp.mean(zc * zc, axis=-1, keepdims=True)
        y_ref[...] = zc * lax.rsqrt(var + LN_EPS) * g_ref[...] + b_ref[...]
        z_ref[...] = z

    row = pl.BlockSpec((tr, d), lambda i: (i, 0))
    vec = pl.BlockSpec((1, d), lambda i: (0, 0))
    return pl.pallas_call(
        body,
        out_shape=(jax.ShapeDtypeStruct((t_dim, d), F32), jax.ShapeDtypeStruct((t_dim, d), F32)),
        grid=(t_dim // tr,),
        in_specs=[row, row, vec, vec],
        out_specs=(row, row),
        compiler_params=_params(("parallel",)),
        name=name,
    )(x, r, g.reshape(1, d), b.reshape(1, d))


def ln_bwd(dy, z, g, *, name):
    t_dim, d = dy.shape
    tr = _pick(t_dim, (256, 128, 8))

    def body(dy_ref, z_ref, g_ref, dz_ref, dg_ref, db_ref):
        @pl.when(pl.program_id(0) == 0)
        def _():
            dg_ref[...] = jnp.zeros_like(dg_ref)
            db_ref[...] = jnp.zeros_like(db_ref)

        z = z_ref[...]
        dyv = dy_ref[...]
        mu = jnp.mean(z, axis=-1, keepdims=True)
        zc = z - mu
        var = jnp.mean(zc * zc, axis=-1, keepdims=True)
        rstd = lax.rsqrt(var + LN_EPS)
        xhat = zc * rstd
        dxhat = dyv * g_ref[...]
        m1 = jnp.mean(dxhat, axis=-1, keepdims=True)
        m2 = jnp.mean(dxhat * xhat, axis=-1, keepdims=True)
        dz_ref[...] = rstd * (dxhat - m1 - xhat * m2)
        dg_ref[...] += jnp.sum(dyv * xhat, axis=0, keepdims=True)
        db_ref[...] += jnp.sum(dyv, axis=0, keepdims=True)

    row = pl.BlockSpec((tr, d), lambda i: (i, 0))
    vec = pl.BlockSpec((1, d), lambda i: (0, 0))
    return pl.pallas_call(
        body,
        out_shape=(jax.ShapeDtypeStruct((t_dim, d), F32), jax.ShapeDtypeStruct((1, d), F32), jax.ShapeDtypeStruct((1, d), F32)),
        grid=(t_dim // tr,),
        in_specs=[row, row, vec],
        out_specs=(row, vec, vec),
        compiler_params=_params(("arbitrary",)),
        name=name,
    )(dy, z, g.reshape(1, d))


def loss_head(y, target, *, name):
    t_dim, d = y.shape
    tr = _pick(t_dim, (256, 128, 8))

    def body(y_ref, t_ref, dy_ref, sq_ref):
        @pl.when(pl.program_id(0) == 0)
        def _():
            sq_ref[...] = jnp.zeros_like(sq_ref)

        diff = y_ref[...] - t_ref[...]
        dy_ref[...] = diff / d
        sq_ref[...] += jnp.sum(diff * diff, axis=0, keepdims=True)

    row = pl.BlockSpec((tr, d), lambda i: (i, 0))
    vec = pl.BlockSpec((1, d), lambda i: (0, 0))
    return pl.pallas_call(
        body,
        out_shape=(jax.ShapeDtypeStruct((t_dim, d), F32), jax.ShapeDtypeStruct((1, d), F32)),
        grid=(t_dim // tr,),
        in_specs=[row, row],
        out_specs=(row, vec),
        compiler_params=_params(("arbitrary",)),
        name=name,
    )(y, target)


def _sigmoid(x):
    return 1.0 / (1.0 + jnp.exp(-x))


def swiglu_fwd(h13, *, name):
    t_dim = h13.shape[0]
    f = h13.shape[1] // 2
    tr = _pick(t_dim, (256, 128, 8))

    def body(h1_ref, h3_ref, act_ref):
        h1 = h1_ref[...]
        act_ref[...] = (h1 * _sigmoid(h1) * h3_ref[...]).astype(BF16)

    return pl.pallas_call(
        body,
        out_shape=jax.ShapeDtypeStruct((t_dim, f), BF16),
        grid=(t_dim // tr,),
        in_specs=[pl.BlockSpec((tr, f), lambda i: (i, 0)), pl.BlockSpec((tr, f), lambda i: (i, 1))],
        out_specs=pl.BlockSpec((tr, f), lambda i: (i, 0)),
        compiler_params=_params(("parallel",)),
        name=name,
    )(h13, h13)


def swiglu_bwd(dact, h13, *, name):
    t_dim = h13.shape[0]
    f = h13.shape[1] // 2
    tr = _pick(t_dim, (256, 128, 8))

    def body(da_ref, h1_ref, h3_ref, dh_ref):
        h1 = h1_ref[...]
        da = da_ref[...]
        sg = _sigmoid(h1)
        dh_ref[:, :f] = (da * h3_ref[...] * sg * (1.0 + h1 * (1.0 - sg))).astype(BF16)
        dh_ref[:, f:] = (da * h1 * sg).astype(BF16)

    return pl.pallas_call(
        body,
        out_shape=jax.ShapeDtypeStruct((t_dim, 2 * f), BF16),
        grid=(t_dim // tr,),
        in_specs=[pl.BlockSpec((tr, f), lambda i: (i, 0)), pl.BlockSpec((tr, f), lambda i: (i, 0)),
                  pl.BlockSpec((tr, f), lambda i: (i, 1))],
        out_specs=pl.BlockSpec((tr, 2 * f), lambda i: (i, 0)),
        compiler_params=_params(("parallel",)),
        name=name,
    )(dact, h13, h13)


def merge_fwd(proj_f, br, *, name):
    t_dim, d = br[0].shape
    tr = _pick(t_dim, (256, 128, 8))

    def body(g0, g1, g2, b0, b1, b2, o_ref):
        o_ref[...] = (_sigmoid(g0[...]) * b0[...] + _sigmoid(g1[...]) * b1[...] + _sigmoid(g2[...]) * b2[...])

    row = pl.BlockSpec((tr, d), lambda i: (i, 0))
    gate = [pl.BlockSpec((tr, d), functools.partial(lambda n, i: (i, 2 + n), n)) for n in range(3)]
    return pl.pallas_call(
        body,
        out_shape=jax.ShapeDtypeStruct((t_dim, d), F32),
        grid=(t_dim // tr,),
        in_specs=gate + [row, row, row],
        out_specs=row,
        compiler_params=_params(("parallel",)),
        name=name,
    )(proj_f, proj_f, proj_f, *br)


def merge_bwd(dmerged, proj_f, br, *, name):
    t_dim, d = dmerged.shape
    tr = _pick(t_dim, (256, 128, 8))

    def body(dm_ref, g0, g1, g2, b0, b1, b2, d0, d1, d2, dg_ref):
        dm = dm_ref[...]
        for n, (g, b, o) in enumerate(((g0, b0, d0), (g1, b1, d1), (g2, b2, d2))):
            sg = _sigmoid(g[...])
            o[...] = (dm * sg).astype(BF16)
            dg_ref[:, n * d:(n + 1) * d] = (dm * b[...] * sg * (1.0 - sg)).astype(BF16)

    row = pl.BlockSpec((tr, d), lambda i: (i, 0))
    gate = [pl.BlockSpec((tr, d), functools.partial(lambda n, i: (i, 2 + n), n)) for n in range(3)]
    return pl.pallas_call(
        body,
        out_shape=(jax.ShapeDtypeStruct((t_dim, d), BF16),) * 3 + (jax.ShapeDtypeStruct((t_dim, 3 * d), BF16),),
        grid=(t_dim // tr,),
        in_specs=[row] + gate + [row, row, row],
        out_specs=(row, row, row, pl.BlockSpec((tr, 3 * d), lambda i: (i, 0))),
        compiler_params=_params(("parallel",)),
        name=name,
    )(dmerged, proj_f, proj_f, proj_f, *br)


GELU_C = math.sqrt(2.0 / math.pi)
PAD = 8


def _gelu(x):
    return 0.5 * x * (1.0 + jnp.tanh(GELU_C * (x + 0.044715 * x * x * x)))


def _gelu_grad(x):
    t = jnp.tanh(GELU_C * (x + 0.044715 * x * x * x))
    return 0.5 * (1.0 + t) + 0.5 * x * (1.0 - t * t) * GELU_C * (1.0 + 3.0 * 0.044715 * x * x)


def _neg_expm1(x):
    series = -x * (1.0 + x * (0.5 + x * (1.0 / 6.0 + x * (1.0 / 24.0 + x * (1.0 / 120.0)))))
    return jnp.where(x > -0.1, series, 1.0 - jnp.exp(x))


def _lru_gates(xv, cw_ref, cb_ref, wr_ref, wi_ref, br_ref, bi_ref, lam_ref, pad_ref, s_len):
    pad_ref[pl.ds(0, PAD), :] = jnp.zeros((PAD, LANES), F32)
    pad_ref[pl.ds(PAD, s_len), :] = xv
    xc = cb_ref[...] + jnp.zeros((s_len, LANES), F32)
    for j in range(CONV_WIDTH):
        xc = xc + pad_ref[pl.ds(PAD - (CONV_WIDTH - 1) + j, s_len), :] * cw_ref[pl.ds(j, 1), :]
    xcb = xc.astype(BF16)
    r = _sigmoid(jnp.dot(xcb, wr_ref[0].astype(BF16), preferred_element_type=F32) + br_ref[...])
    i = _sigmoid(jnp.dot(xcb, wi_ref[0].astype(BF16), preferred_element_type=F32) + bi_ref[...])
    nl = -lam_ref[...]
    sp = jnp.maximum(nl, 0.0) + jnp.log(1.0 + jnp.exp(-jnp.abs(nl)))
    log_a = -LRU_C * r * sp
    a = jnp.exp(log_a)
    mult = jnp.sqrt(_neg_expm1(2.0 * log_a))
    return xc, r, i, sp, a, mult


def _tile_scan(a, b, row, reverse):
    for s in (1, 2, 4):
        if reverse:
            a_sh = pltpu.roll(a, 8 - s, 0)
            b_sh = pltpu.roll(b, 8 - s, 0)
            m = row + s <= 7
        else:
            a_sh = pltpu.roll(a, s, 0)
            b_sh = pltpu.roll(b, s, 0)
            m = row >= s
        b = jnp.where(m, a * b_sh + b, b)
        a = jnp.where(m, a * a_sh, a)
    return a, b


def lru_fwd(proj_f, conv_w, conv_b, wr_bd, wi_bd, b_rg, b_ig, lam, *, name):
    bsz, s_len, _ = proj_f.shape
    d = D_MODEL
    ncb = d // LANES
    n_tiles = s_len // 8

    def body(x_ref, g_ref, cw_ref, cb_ref, wr_ref, wi_ref, br_ref, bi_ref, lam_ref, y_ref, h_ref, pad_ref, a_s, b_s):
        xc, r, i, sp, a, mult = _lru_gates(x_ref[0], cw_ref, cb_ref, wr_ref, wi_ref, br_ref, bi_ref, lam_ref, pad_ref, s_len)
        a_s[...] = a
        b_s[...] = mult * (i * xc)
        row = lax.broadcasted_iota(jnp.int32, (8, LANES), 0)

        def tile(t, carry):
            i0 = pl.multiple_of(t * 8, 8)
            ac, hl = _tile_scan(a_s[pl.ds(i0, 8), :], b_s[pl.ds(i0, 8), :], row, False)
            h = hl + ac * carry
            h_ref[0, pl.ds(i0, 8), :] = h
            return jnp.broadcast_to(h[7:8, :], (8, LANES))

        lax.fori_loop(0, n_tiles, tile, jnp.zeros((8, LANES), F32))
        y_ref[0] = h_ref[0] * _gelu(g_ref[0])

    slab = lambda off: pl.BlockSpec((1, s_len, LANES), functools.partial(lambda o, c, b: (b, 0, o + c), off))
    vec = pl.BlockSpec((1, LANES), lambda c, b: (0, c))
    mat = pl.BlockSpec((1, LANES, LANES), lambda c, b: (c, 0, 0))
    out = pl.BlockSpec((1, s_len, LANES), lambda c, b: (b, 0, c))
    return pl.pallas_call(
        body,
        out_shape=(jax.ShapeDtypeStruct((bsz, s_len, d), F32),) * 2,
        grid=(ncb, bsz),
        in_specs=[slab(0), slab(ncb), pl.BlockSpec((CONV_WIDTH, LANES), lambda c, b: (0, c)), vec, mat, mat, vec, vec, vec],
        out_specs=(out, out),
        scratch_shapes=[pltpu.VMEM((s_len + 2 * PAD, LANES), F32), pltpu.VMEM((s_len, LANES), F32), pltpu.VMEM((s_len, LANES), F32)],
        compiler_params=_params(("parallel", "parallel")),
        name=name,
    )(proj_f, proj_f, conv_w, conv_b.reshape(1, d), wr_bd, wi_bd, b_rg.reshape(1, d), b_ig.reshape(1, d), lam.reshape(1, d))


def lru_bwd(dy, proj_f, h, conv_w, conv_b, wr_bd, wi_bd, wr_bd_t, wi_bd_t, b_rg, b_ig, lam, *, name):
    bsz, s_len, _ = proj_f.shape
    d = D_MODEL
    ncb = d // LANES
    n_tiles = s_len // 8

    def body(dy_ref, x_ref, g_ref, h_ref, cw_ref, cb_ref, wr_ref, wi_ref, wrt_ref, wit_ref, br_ref, bi_ref, lam_ref,
             dx_ref, dg_ref, dcw_ref, dcb_ref, dbr_ref, dbi_ref, dlam_ref, dwr_ref, dwi_ref, pad_ref, a_s, b_s, l_s):
        @pl.when(pl.program_id(1) == 0)
        def _():
            for ref in (dcw_ref, dcb_ref, dbr_ref, dbi_ref, dlam_ref, dwr_ref, dwi_ref):
                ref[...] = jnp.zeros_like(ref)

        xc, r, i, sp, a, mult = _lru_gates(x_ref[0], cw_ref, cb_ref, wr_ref, wi_ref, br_ref, bi_ref, lam_ref, pad_ref, s_len)
        gate = g_ref[0]
        hv = h_ref[0]
        dyv = dy_ref[0]
        dg_ref[0] = (dyv * hv * _gelu_grad(gate)).astype(BF16)
        b_s[...] = dyv * _gelu(gate)
        l_s[pl.ds(0, s_len), :] = a
        l_s[pl.ds(s_len, PAD), :] = jnp.zeros((PAD, LANES), F32)
        a_s[...] = l_s[pl.ds(1, s_len), :]
        row = lax.broadcasted_iota(jnp.int32, (8, LANES), 0)

        def tile(t, carry):
            i0 = pl.multiple_of((n_tiles - 1 - t) * 8, 8)
            ac, ll = _tile_scan(a_s[pl.ds(i0, 8), :], b_s[pl.ds(i0, 8), :], row, True)
            lmb = ll + ac * carry
            b_s[pl.ds(i0, 8), :] = lmb
            return jnp.broadcast_to(lmb[0:1, :], (8, LANES))

        lax.fori_loop(0, n_tiles, tile, jnp.zeros((8, LANES), F32))
        lmb = b_s[...]
        l_s[pl.ds(0, PAD), :] = jnp.zeros((PAD, LANES), F32)
        l_s[pl.ds(PAD, s_len), :] = hv
        h_prev = l_s[pl.ds(PAD - 1, s_len), :]
        da = lmb * h_prev
        dmult = lmb * (i * xc)
        di = lmb * mult * xc
        dxc = lmb * mult * i
        dlog_a = da * a - dmult * a * a / mult
        dr = -LRU_C * sp * dlog_a
        dsp = jnp.sum(-LRU_C * r * dlog_a, axis=0, keepdims=True)
        dlam_ref[...] += dsp * (-_sigmoid(-lam_ref[...]))
        dpr = dr * r * (1.0 - r)
        dpi = di * i * (1.0 - i)
        dprb = dpr.astype(BF16)
        dpib = dpi.astype(BF16)
        xcb = xc.astype(BF16)
        dbr_ref[...] += jnp.sum(dpr, axis=0, keepdims=True)
        dbi_ref[...] += jnp.sum(dpi, axis=0, keepdims=True)
        tn = (((0,), (0,)), ((), ()))
        dwr_ref[0] += lax.dot_general(xcb, dprb, tn, preferred_element_type=F32)
        dwi_ref[0] += lax.dot_general(xcb, dpib, tn, preferred_element_type=F32)
        dxc = (dxc + jnp.dot(dprb, wrt_ref[0].astype(BF16), preferred_element_type=F32)
               + jnp.dot(dpib, wit_ref[0].astype(BF16), preferred_element_type=F32))
        dcb_ref[...] += jnp.sum(dxc, axis=0, keepdims=True)
        for j in range(CONV_WIDTH):
            dcw_ref[pl.ds(j, 1), :] += jnp.sum(dxc * pad_ref[pl.ds(PAD - (CONV_WIDTH - 1) + j, s_len), :], axis=0, keepdims=True)
        l_s[pl.ds(0, s_len), :] = dxc
        l_s[pl.ds(s_len, PAD), :] = jnp.zeros((PAD, LANES), F32)
        dx = jnp.zeros((s_len, LANES), F32)
        for j in range(CONV_WIDTH):
            dx = dx + l_s[pl.ds(CONV_WIDTH - 1 - j, s_len), :] * cw_ref[pl.ds(j, 1), :]
        dx_ref[0] = dx.astype(BF16)

    slab = lambda off: pl.BlockSpec((1, s_len, LANES), functools.partial(lambda o, c, b: (b, 0, o + c), off))
    vec = pl.BlockSpec((1, LANES), lambda c, b: (0, c))
    mat = pl.BlockSpec((1, LANES, LANES), lambda c, b: (c, 0, 0))
    cw = pl.BlockSpec((CONV_WIDTH, LANES), lambda c, b: (0, c))
    out = pl.BlockSpec((1, s_len, LANES), lambda c, b: (b, 0, c))
    vshape = jax.ShapeDtypeStruct((1, d), F32)
    mshape = jax.ShapeDtypeStruct((ncb, LANES, LANES), F32)
    return pl.pallas_call(
        body,
        out_shape=(jax.ShapeDtypeStruct((bsz, s_len, d), BF16),) * 2
        + (jax.ShapeDtypeStruct((CONV_WIDTH, d), F32), vshape, vshape, vshape, vshape, mshape, mshape),
        grid=(ncb, bsz),
        in_specs=[out, slab(0), slab(ncb), out, cw, vec, mat, mat, mat, mat, vec, vec, vec],
        out_specs=(out, out, cw, vec, vec, vec, vec, mat, mat),
        scratch_shapes=[pltpu.VMEM((s_len + 2 * PAD, LANES), F32), pltpu.VMEM((s_len, LANES), F32), pltpu.VMEM((s_len, LANES), F32),
                        pltpu.VMEM((s_len + 2 * PAD, LANES), F32)],
        compiler_params=_params(("parallel", "arbitrary")),
        name=name,
    )(dy, proj_f, proj_f, h, conv_w, conv_b.reshape(1, d), wr_bd, wi_bd, wr_bd_t, wi_bd_t,
      b_rg.reshape(1, d), b_ig.reshape(1, d), lam.reshape(1, d))


def _kv_place(head, n_kv_heads):
    kv = head // (N_HEADS // n_kv_heads)
    return kv // 2, kv % 2


def _band_mask(n):
    qi = lax.broadcasted_iota(jnp.int32, (ATT_BLOCK, 2 * ATT_BLOCK), 0)
    kj = lax.broadcasted_iota(jnp.int32, (ATT_BLOCK, 2 * ATT_BLOCK), 1)
    rel = qi + ATT_BLOCK - kj
    return (rel >= 0) & (rel <= ATT_BLOCK) & ((n > 0) | (kj >= ATT_BLOCK))


def _half_masks(dtype):
    lane = lax.broadcasted_iota(jnp.int32, (1, LANES), 1)
    return [(lane < HEAD_DIM).astype(dtype), (lane >= HEAD_DIM).astype(dtype)]


NT = (((1,), (1,)), ((), ()))
TN = (((0,), (0,)), ((), ()))


def _qkv_specs(dil, q_blk, k_blk, v_blk, ckv, clamp):
    qw = D_MODEL // LANES * LANES
    return [
        pl.BlockSpec((1, ATT_BLOCK, qw), lambda b, j, n: (b, clamp(n), j * (q_blk[1]) + q_blk[0])),
        pl.BlockSpec((1, ATT_BLOCK, ckv), lambda b, j, n: (b, jnp.maximum(clamp(n) - 1, 0), j * k_blk[1] + k_blk[0])),
        pl.BlockSpec((1, ATT_BLOCK, ckv), lambda b, j, n: (b, clamp(n), j * k_blk[1] + k_blk[0])),
        pl.BlockSpec((1, ATT_BLOCK, ckv), lambda b, j, n: (b, jnp.maximum(clamp(n) - 1, 0), j * v_blk[1] + v_blk[0])),
        pl.BlockSpec((1, ATT_BLOCK, ckv), lambda b, j, n: (b, clamp(n), j * v_blk[1] + v_blk[0])),
    ]


def attn_fwd(qkv, *, dil, n_kv_heads, sinks, name):
    bsz, s_len, width = qkv.shape
    ckv = n_kv_heads * HEAD_DIM
    l_sub = s_len // dil
    nb = l_sub // ATT_BLOCK
    view = qkv.reshape(bsz, l_sub, dil * width)
    scale = HEAD_DIM ** -0.5
    q_blk = (0, width // D_MODEL)
    k_blk = (D_MODEL // ckv, width // ckv)
    v_blk = (D_MODEL // ckv + 1, width // ckv)
    assert (dil == 1 or width % D_MODEL == 0) and width % ckv == 0 and D_MODEL % ckv == 0

    def body(*refs):
        if sinks is None:
            q_ref, kp_ref, kc_ref, vp_ref, vc_ref, o_ref, lse_ref = refs
            sink_ref = None
        else:
            sink_ref, q_ref, kp_ref, kc_ref, vp_ref, vc_ref, o_ref, lse_ref = refs
        n = pl.program_id(2)
        mask = _band_mask(n)
        hm = _half_masks(BF16)
        hmf = _half_masks(F32)
        kk = jnp.concatenate([kp_ref[0], kc_ref[0]], axis=0)
        vv = jnp.concatenate([vp_ref[0], vc_ref[0]], axis=0)
        for hp in range(N_HEADS // 2):
            q2 = q_ref[0, :, hp * LANES:(hp + 1) * LANES]
            o2 = jnp.zeros((ATT_BLOCK, LANES), F32)
            l2 = jnp.zeros((ATT_BLOCK, LANES), F32)
            for a in range(2):
                kb, kh = _kv_place(2 * hp + a, n_kv_heads)
                k2 = kk[:, kb * LANES:(kb + 1) * LANES]
                v2 = vv[:, kb * LANES:(kb + 1) * LANES]
                if kh != a:
                    k2 = pltpu.roll(k2, HEAD_DIM, 1)
                    v2 = pltpu.roll(v2, HEAD_DIM, 1)
                s = lax.dot_general(q2 * hm[a], k2, NT, preferred_element_type=F32) * scale
                s = jnp.where(mask, s, NEG_INF)
                m = jnp.max(s, axis=-1, keepdims=True)
                if sink_ref is not None:
                    sk = sink_ref[2 * hp + a]
                    m = jnp.maximum(m, sk)
                p = jnp.exp(s - m)
                den = jnp.sum(p, axis=-1, keepdims=True)
                if sink_ref is not None:
                    den = den + jnp.exp(sk - m)
                o2 = o2 + jnp.dot(p.astype(BF16), v2 * hm[a], preferred_element_type=F32) / den
                l2 = l2 + (m + jnp.log(den)) * hmf[a]
            o_ref[0, :, hp * LANES:(hp + 1) * LANES] = o2
            lse_ref[0, :, hp * LANES:(hp + 1) * LANES] = l2

    in_specs = _qkv_specs(dil, q_blk, k_blk, v_blk, ckv, lambda n: n)
    args = [view] * 5
    if sinks is not None:
        in_specs = [pl.BlockSpec(memory_space=pltpu.SMEM)] + in_specs
        args = [sinks] + args
    out = pl.BlockSpec((1, ATT_BLOCK, D_MODEL), lambda b, j, n: (b, n, j))
    o, lse = pl.pallas_call(
        body,
        out_shape=(jax.ShapeDtypeStruct((bsz, l_sub, dil * D_MODEL), F32),) * 2,
        grid=(bsz, dil, nb),
        in_specs=in_specs,
        out_specs=(out, out),
        compiler_params=_params(("parallel", "parallel", "arbitrary")),
        name=name,
    )(*args)
    return o.reshape(bsz, s_len, D_MODEL), lse.reshape(bsz, s_len, D_MODEL)


def attn_bwd(qkv, o, lse, do, acc, *, dil, n_kv_heads, name):
    bsz, s_len, width = qkv.shape
    ckv = n_kv_heads * HEAD_DIM
    l_sub = s_len // dil
    nb = l_sub // ATT_BLOCK
    view = qkv.reshape(bsz, l_sub, dil * width)
    scale = HEAD_DIM ** -0.5
    q_blk = (0, width // D_MODEL)
    k_blk = (D_MODEL // ckv, width // ckv)
    v_blk = (D_MODEL // ckv + 1, width // ckv)

    def body(*refs):
        if acc is None:
            q_ref, kp_ref, kc_ref, vp_ref, vc_ref, o_ref, lse_ref, do_ref, dq_ref, dk_ref, dv_ref, dkk, dvv, ck, cv = refs
            aq_ref = ak_ref = av_ref = None
        else:
            (q_ref, kp_ref, kc_ref, vp_ref, vc_ref, o_ref, lse_ref, do_ref, aq_ref, ak_ref, av_ref,
             dq_ref, dk_ref, dv_ref, dkk, dvv, ck, cv) = refs
        n = pl.program_id(2)

        @pl.when(n < nb)
        def _():
            mask = _band_mask(n)
            hm = _half_masks(BF16)
            hmf = _half_masks(F32)
            kk = jnp.concatenate([kp_ref[0], kc_ref[0]], axis=0)
            vv = jnp.concatenate([vp_ref[0], vc_ref[0]], axis=0)
            dkk[...] = jnp.zeros_like(dkk)
            dvv[...] = jnp.zeros_like(dvv)
            for hp in range(N_HEADS // 2):
                cols = slice(hp * LANES, (hp + 1) * LANES)
                q2 = q_ref[0, :, cols]
                do2f = do_ref[0, :, cols]
                do2 = do2f.astype(BF16)
                dd2 = do2f * o_ref[0, :, cols]
                l2 = lse_ref[0, :, cols]
                dq2 = jnp.zeros((ATT_BLOCK, LANES), F32)
                for a in range(2):
                    kb, kh = _kv_place(2 * hp + a, n_kv_heads)
                    kcols = slice(kb * LANES, (kb + 1) * LANES)
                    k2 = kk[:, kcols]
                    v2 = vv[:, kcols]
                    if kh != a:
                        k2 = pltpu.roll(k2, HEAD_DIM, 1)
                        v2 = pltpu.roll(v2, HEAD_DIM, 1)
                    qm = q2 * hm[a]
                    dom = do2 * hm[a]
                    dsum = jnp.sum(dd2 * hmf[a], axis=-1, keepdims=True)
                    lse_h = jnp.max(jnp.where(hmf[a] > 0.5, l2, NEG_INF), axis=-1, keepdims=True)
                    s = lax.dot_general(qm, k2, NT, preferred_element_type=F32) * scale
                    s = jnp.where(mask, s, NEG_INF)
                    p = jnp.exp(s - lse_h)
                    dp = lax.dot_general(dom, v2, NT, preferred_element_type=F32)
                    ds = (p * (dp - dsum) * scale).astype(BF16)
                    dq2 = dq2 + jnp.dot(ds, k2 * hm[a], preferred_element_type=F32)
                    dk_c = lax.dot_general(ds, qm, TN, preferred_element_type=F32)
                    dv_c = lax.dot_general(p.astype(BF16), dom, TN, preferred_element_type=F32)
                    if kh != a:
                        dk_c = pltpu.roll(dk_c, HEAD_DIM, 1)
                        dv_c = pltpu.roll(dv_c, HEAD_DIM, 1)
                    dkk[:, kcols] += dk_c
                    dvv[:, kcols] += dv_c
                if aq_ref is not None:
                    dq2 = dq2 + aq_ref[0, :, cols]
                dq_ref[0, :, cols] = dq2

        @pl.when((n >= 1) & (n < nb))
        def _():
            dk_ref[0] = ck[...] + dkk[pl.ds(0, ATT_BLOCK), :] + (0.0 if ak_ref is None else ak_ref[0])
            dv_ref[0] = cv[...] + dvv[pl.ds(0, ATT_BLOCK), :] + (0.0 if av_ref is None else av_ref[0])

        @pl.when(n == nb)
        def _():
            dk_ref[0] = ck[...] + (0.0 if ak_ref is None else ak_ref[0])
            dv_ref[0] = cv[...] + (0.0 if av_ref is None else av_ref[0])

        @pl.when(n < nb)
        def _():
            ck[...] = dkk[pl.ds(ATT_BLOCK, ATT_BLOCK), :]
            cv[...] = dvv[pl.ds(ATT_BLOCK, ATT_BLOCK), :]

    clamp = lambda n: jnp.minimum(n, nb - 1)
    prev = lambda n: jnp.maximum(n - 1, 0)
    row = pl.BlockSpec((1, ATT_BLOCK, D_MODEL), lambda b, j, n: (b, clamp(n), j))
    kv_out = pl.BlockSpec((1, ATT_BLOCK, ckv), lambda b, j, n: (b, prev(n), j))
    in_specs = _qkv_specs(dil, q_blk, k_blk, v_blk, ckv, clamp) + [row, row, row]
    rs = lambda t: t.reshape(bsz, l_sub, dil * t.shape[-1])
    args = [view] * 5 + [rs(o), rs(lse), rs(do)]
    if acc is not None:
        in_specs += [row, kv_out, kv_out]
        args += [rs(t) for t in acc]
    dq, dk, dv = pl.pallas_call(
        body,
        out_shape=(jax.ShapeDtypeStruct((bsz, l_sub, dil * D_MODEL), F32),
                   jax.ShapeDtypeStruct((bsz, l_sub, dil * ckv), F32), jax.ShapeDtypeStruct((bsz, l_sub, dil * ckv), F32)),
        grid=(bsz, dil, nb + 1),
        in_specs=in_specs,
        out_specs=(row, kv_out, kv_out),
        scratch_shapes=[pltpu.VMEM((2 * ATT_BLOCK, ckv), F32), pltpu.VMEM((2 * ATT_BLOCK, ckv), F32),
                        pltpu.VMEM((ATT_BLOCK, ckv), F32), pltpu.VMEM((ATT_BLOCK, ckv), F32)],
        compiler_params=_params(("parallel", "parallel", "arbitrary")),
        name=name,
    )(*args)
    return dq.reshape(bsz, s_len, D_MODEL), dk.reshape(bsz, s_len, ckv), dv.reshape(bsz, s_len, ckv)


def dil_combine(os_, lses, *, name):
    t_dim, d = os_[0].shape
    tr = _pick(t_dim, (256, 128, 8))

    def body(o0, o1, o2, l0, l1, l2, y_ref, lt_ref):
        la, lb, lc = l0[...], l1[...], l2[...]
        m = jnp.maximum(jnp.maximum(la, lb), lc)
        ea, eb, ec = jnp.exp(la - m), jnp.exp(lb - m), jnp.exp(lc - m)
        tot = ea + eb + ec
        y_ref[...] = (ea / tot) * o0[...] + (eb / tot) * o1[...] + (ec / tot) * o2[...]
        lt_ref[...] = m + jnp.log(tot)

    row = pl.BlockSpec((tr, d), lambda i: (i, 0))
    return pl.pallas_call(
        body,
        out_shape=(jax.ShapeDtypeStruct((t_dim, d), F32),) * 2,
        grid=(t_dim // tr,),
        in_specs=[row] * 6,
        out_specs=(row, row),
        compiler_params=_params(("parallel",)),
        name=name,
    )(*os_, *lses)


def sink_grad(do, o, lse, sink_lanes, *, name):
    t_dim, d = do.shape
    tr = _pick(t_dim, (256, 128, 8))

    def body(do_ref, o_ref, l_ref, s_ref, out_ref):
        @pl.when(pl.program_id(0) == 0)
        def _():
            out_ref[...] = jnp.zeros_like(out_ref)

        out_ref[...] += jnp.sum(-jnp.exp(s_ref[...] - l_ref[...]) * do_ref[...] * o_ref[...], axis=0, keepdims=True)

    row = pl.BlockSpec((tr, d), lambda i: (i, 0))
    vec = pl.BlockSpec((1, d), lambda i: (0, 0))
    return pl.pallas_call(
        body,
        out_shape=jax.ShapeDtypeStruct((1, d), F32),
        grid=(t_dim // tr,),
        in_specs=[row, row, row, vec],
        out_specs=vec,
        compiler_params=_params(("arbitrary",)),
        name=name,
    )(do, o, lse, sink_lanes)


def adamw(w, g, m, v, *, name):
    rows, cols = w.shape
    tr = _pick(rows, (256, 128, 64, 32, 16, 8))

    def body(w_ref, g_ref, m_ref, v_ref, d_ref, nm_ref, nv_ref):
        gv = g_ref[...]
        nm = ADAM_B1 * m_ref[...] + (1.0 - ADAM_B1) * gv
        nv = ADAM_B2 * v_ref[...] + (1.0 - ADAM_B2) * (gv * gv)
        m_hat = nm / (1.0 - ADAM_B1 ** ADAM_STEP)
        v_hat = nv / (1.0 - ADAM_B2 ** ADAM_STEP)
        d_ref[...] = -ADAM_LR * (m_hat / (jnp.sqrt(v_hat) + ADAM_EPS) + ADAM_WD * w_ref[...])
        nm_ref[...] = nm
        nv_ref[...] = nv

    row = pl.BlockSpec((tr, cols), lambda i: (i, 0))
    return pl.pallas_call(
        body,
        out_shape=(jax.ShapeDtypeStruct((rows, cols), F32),) * 3,
        grid=(rows // tr,),
        in_specs=[row] * 4,
        out_specs=(row, row, row),
        compiler_params=_params(("parallel",)),
        name=name,
    )(w, g, m, v)


def _place():
    return lax.axis_index("x"), lax.axis_index("y"), lax.axis_index("c")


def all_gather(x, *, name):
    def body(x_ref, out_ref, send_sems, recv_sems, local_sem):
        x, y, c = _place()
        me, sibling = (x, y, c), (x, y, 1 - c)
        chips = [(1 - x, y), (x, 1 - y), (1 - x, 1 - y)]

        def slot(px, py, pc):
            return out_ref.at[4 * px + 2 * py + pc]

        def copy(k, block, to, src=None):
            return pltpu.make_async_remote_copy(
                src_ref=slot(*block) if src is None else src, dst_ref=slot(*block),
                send_sem=send_sems.at[k], recv_sem=recv_sems.at[k], device_id=to, device_id_type=MESH)

        mine = pltpu.make_async_copy(x_ref, slot(*me), local_sem)
        mine.start()
        first = [copy(0, me, sibling, src=x_ref)]
        first += [copy(1 + j, me, (*chip, c), src=x_ref) for j, chip in enumerate(chips)]
        for cp in first:
            cp.start()
        passed = [copy(4 + j, (*chip, c), sibling) for j, chip in enumerate(chips)]
        for j, chip in enumerate(chips):
            copy(1 + j, (*chip, c), me).wait_recv()
            passed[j].start()
        copy(0, sibling, me).wait_recv()
        for j, chip in enumerate(chips):
            copy(4 + j, (*chip, 1 - c), me).wait_recv()
        for cp in first + passed:
            cp.wait_send()
        mine.wait()

    return pl.pallas_call(
        body,
        out_shape=jax.ShapeDtypeStruct((N_DEV,) + x.shape, x.dtype),
        in_specs=[pl.BlockSpec(memory_space=pl.ANY)],
        out_specs=pl.BlockSpec(memory_space=pl.ANY),
        scratch_shapes=[pltpu.SemaphoreType.DMA((7,)), pltpu.SemaphoreType.DMA((7,)), pltpu.SemaphoreType.DMA(())],
        name=name,
    )(x)


def all_to_all(x, *, name):
    def body(x_ref, out_ref, send_sems, recv_sems, local_sem):
        x, y, c = _place()
        me = 4 * x + 2 * y + c
        mine = pltpu.make_async_copy(x_ref.at[me], out_ref.at[me], local_sem)
        mine.start()
        copies = []
        for k in range(1, N_DEV):
            px = 1 - x if k & 4 else x
            py = 1 - y if k & 2 else y
            pc = 1 - c if k & 1 else c
            peer = 4 * px + 2 * py + pc
            copies.append(pltpu.make_async_remote_copy(
                src_ref=x_ref.at[peer], dst_ref=out_ref.at[me], send_sem=send_sems.at[k - 1], recv_sem=recv_sems.at[k - 1],
                device_id=(px, py, pc), device_id_type=MESH))
        for cp in copies:
            cp.start()
        for cp in copies:
            cp.wait_recv()
        for cp in copies:
            cp.wait_send()
        mine.wait()

    return pl.pallas_call(
        body,
        out_shape=jax.ShapeDtypeStruct(x.shape, x.dtype),
        in_specs=[pl.BlockSpec(memory_space=pl.ANY)],
        out_specs=pl.BlockSpec(memory_space=pl.ANY),
        scratch_shapes=[pltpu.SemaphoreType.DMA((7,)), pltpu.SemaphoreType.DMA((7,)), pltpu.SemaphoreType.DMA(())],
        name=name,
    )(x)


def sum_slots(x, *, name):
    _, rows, cols = x.shape
    tr = _pick(rows, (512, 256, 128, 64, 32, 16))

    def body(x_ref, o_ref):
        acc = x_ref[0].astype(F32)
        for k in range(1, N_DEV):
            acc = acc + x_ref[k].astype(F32)
        o_ref[...] = acc

    return pl.pallas_call(
        body,
        out_shape=jax.ShapeDtypeStruct((rows, cols), F32),
        grid=(rows // tr,),
        in_specs=[pl.BlockSpec((N_DEV, tr, cols), lambda i: (0, i, 0))],
        out_specs=pl.BlockSpec((tr, cols), lambda i: (i, 0)),
        compiler_params=_params(("parallel",)),
        name=name,
    )(x)


BIG = ("w_in", "w_branch", "w_out", "w_ffn_in", "w_ffn_out")
SMALL = ("conv_b", "w_rg", "b_rg", "w_ig", "b_ig", "lru_lambda", "sinks", "ln1_g", "ln1_b", "ln2_g", "ln2_b")
N_LRU_BLOCKS = D_MODEL // HEAD_DIM
SMALL_ROWS_TILE = 512


def _block_diag(w):
    z = jnp.zeros((N_LRU_BLOCKS // 2, HEAD_DIM, HEAD_DIM), w.dtype)
    top = jnp.concatenate([w[0::2], z], axis=2)
    bot = jnp.concatenate([z, w[1::2]], axis=2)
    return jnp.concatenate([top, bot], axis=1)


def _block_diag_grad(g):
    return jnp.stack([g[:, :HEAD_DIM, :HEAD_DIM], g[:, HEAD_DIM:, HEAD_DIM:]], axis=1).reshape(N_LRU_BLOCKS, HEAD_DIM, HEAD_DIM)


def _split_w_in(w):
    a, b = 2 * D_MODEL, 2 * D_MODEL + W_QS
    return jnp.concatenate([w[:, :a], w[:, b + W_QD:]], axis=1), w[:, a:b], w[:, b:b + W_QD]


def _merge_w_in(gf, gqs, gqd):
    return jnp.concatenate([gf[:, :2 * D_MODEL], gqs, gqd, gf[:, 2 * D_MODEL:]], axis=1)


def layer_fwd(x, p, bsz):
    t_dim = x.shape[0]
    s_len = t_dim // bsz
    w_f, w_qs, w_qd = _split_w_in(p["w_in"])
    proj_f = matmul(x, w_f, name="proj_f")
    qs = matmul(x, w_qs, out_dtype=BF16, name="proj_qs").reshape(bsz, s_len, W_QS)
    qd = matmul(x, w_qd, out_dtype=BF16, name="proj_qd").reshape(bsz, s_len, W_QD)
    proj_f3 = proj_f.reshape(bsz, s_len, W_F)
    wr_bd, wi_bd = _block_diag(p["w_rg"]), _block_diag(p["w_ig"])
    y_a, h = lru_fwd(proj_f3, p["conv_w"], p["conv_b"], wr_bd, wi_bd, p["b_rg"], p["b_ig"], p["lru_lambda"], name="lru_fwd")
    y_b, lse_b = attn_fwd(qs, dil=1, n_kv_heads=SWA_KV_HEADS, sinks=p["sinks"], name="swa_fwd")
    os_, lses = [], []
    for dil in DILATIONS:
        o, lse = attn_fwd(qd, dil=dil, n_kv_heads=N_HEADS, sinks=None, name=f"dil{dil}_fwd")
        os_.append(o.reshape(t_dim, D_MODEL))
        lses.append(lse.reshape(t_dim, D_MODEL))
    y_c, lse_c = dil_combine(os_, lses, name="dil_combine")
    ys = [y_a.reshape(t_dim, D_MODEL), y_b.reshape(t_dim, D_MODEL), y_c]
    br = [matmul(ys[n], p["w_branch"][n], name="branch") for n in range(3)]
    merged = merge_fwd(proj_f, br, name="merge_fwd")
    mix = matmul(merged, p["w_out"], name="w_out")
    x1, z1 = ln_fwd(x, mix, p["ln1_g"], p["ln1_b"], name="ln_fwd")
    h13 = matmul(x1, p["w_ffn_in"], name="ffn_in")
    act = swiglu_fwd(h13, name="swiglu_fwd")
    ffn = matmul(act, p["w_ffn_out"], name="ffn_out")
    x2, z2 = ln_fwd(x1, ffn, p["ln2_g"], p["ln2_b"], name="ln_fwd")
    saved = dict(x=x, proj_f=proj_f, qs=qs, qd=qd, h=h, ys=ys, lse_b=lse_b, lse_c=lse_c, br=br, merged=merged,
                 z1=z1, x1=x1, h13=h13, act=act, z2=z2, wr_bd=wr_bd, wi_bd=wi_bd, w_split=(w_f, w_qs, w_qd))
    return x2, saved


def layer_bwd(dx2, p, s, bsz):
    t_dim = dx2.shape[0]
    s_len = t_dim // bsz
    g = {}
    dz2, g["ln2_g"], g["ln2_b"] = ln_bwd(dx2, s["z2"], p["ln2_g"], name="ln_bwd")
    dact = matmul(dz2, p["w_ffn_out"].T, name="d_act")
    dh13 = swiglu_bwd(dact, s["h13"], name="swiglu_bwd")
    g["w_ffn_out"] = matmul(s["act"], dz2, trans_a=True, name="dw_ffn_out")
    g["w_ffn_in"] = matmul(s["x1"], dh13, trans_a=True, name="dw_ffn_in")
    dx1 = matmul(dh13, p["w_ffn_in"].T, add=dz2, add_scale=ALPHA, name="dx_ffn")
    dz1, g["ln1_g"], g["ln1_b"] = ln_bwd(dx1, s["z1"], p["ln1_g"], name="ln_bwd")
    dmerged = matmul(dz1, p["w_out"].T, name="d_merged")
    g["w_out"] = matmul(s["merged"], dz1, trans_a=True, name="dw_out")
    *dbr, dgates = merge_bwd(dmerged, s["proj_f"], s["br"], name="merge_bwd")
    dys = [matmul(dbr[n], p["w_branch"][n].T, name="d_branch") for n in range(3)]
    g["w_branch"] = jnp.stack([matmul(s["ys"][n], dbr[n], trans_a=True, name="dw_branch") for n in range(3)])
    shape3 = (bsz, s_len, D_MODEL)
    (dlx, dlg, g["conv_w"], g["conv_b"], g["b_rg"], g["b_ig"], g["lru_lambda"], dwr, dwi) = lru_bwd(
        dys[0].reshape(shape3), s["proj_f"].reshape(bsz, s_len, W_F), s["h"], p["conv_w"], p["conv_b"], s["wr_bd"], s["wi_bd"],
        jnp.swapaxes(s["wr_bd"], 1, 2), jnp.swapaxes(s["wi_bd"], 1, 2), p["b_rg"], p["b_ig"], p["lru_lambda"], name="lru_bwd")
    g["w_rg"], g["w_ig"] = _block_diag_grad(dwr), _block_diag_grad(dwi)
    y_b3, dy_b3 = s["ys"][1].reshape(shape3), dys[1].reshape(shape3)
    dqs = attn_bwd(s["qs"], y_b3, s["lse_b"], dy_b3, None, dil=1, n_kv_heads=SWA_KV_HEADS, name="swa_bwd")
    sink_lanes = jnp.repeat(p["sinks"], HEAD_DIM).reshape(1, D_MODEL)
    g["sinks"] = sink_grad(dys[1], s["ys"][1], s["lse_b"].reshape(t_dim, D_MODEL), sink_lanes, name="sink_grad").reshape(
        N_HEADS, HEAD_DIM).sum(axis=1)
    y_c3, dy_c3, lse_c3 = s["ys"][2].reshape(shape3), dys[2].reshape(shape3), s["lse_c"].reshape(shape3)
    dqd = None
    for dil in DILATIONS:
        dqd = attn_bwd(s["qd"], y_c3, lse_c3, dy_c3, dqd, dil=dil, n_kv_heads=N_HEADS, name=f"dil{dil}_bwd")
    flat = lambda t: t.reshape(t_dim, t.shape[-1])
    dproj_f = jnp.concatenate([flat(dlx), flat(dlg), dgates], axis=1)
    dproj_qs = jnp.concatenate([flat(t) for t in dqs], axis=1).astype(BF16)
    dproj_qd = jnp.concatenate([flat(t) for t in dqd], axis=1).astype(BF16)
    w_f, w_qs, w_qd = s["w_split"]
    g["w_in"] = _merge_w_in(matmul(s["x"], dproj_f, trans_a=True, name="dw_in_f"),
                            matmul(s["x"], dproj_qs, trans_a=True, name="dw_in_qs"),
                            matmul(s["x"], dproj_qd, trans_a=True, name="dw_in_qd"))
    dx = matmul(dproj_f, w_f.T, add=dz1, add_scale=ALPHA, name="dx_f")
    dx = matmul(dproj_qs, w_qs.T, add=dx, name="dx_qs")
    dx = matmul(dproj_qd, w_qd.T, add=dx, name="dx_qd")
    g = {k: (v.reshape(p[k].shape) if k in p else v) for k, v in g.items()}
    return dx, g


def local_step(x, target, params):
    bsz, s_len, d = x.shape
    t_dim = bsz * s_len
    xf = x.reshape(t_dim, d)
    saved = []
    for l in range(DEPTH):
        xf, s = layer_fwd(xf, {k: v[l] for k, v in params.items()}, bsz)
        saved.append(s)
    dy, sq = loss_head(xf, target.reshape(t_dim, d), name="loss_head")
    loss = 0.5 * jnp.sum(sq) / d
    grads = [None] * DEPTH
    for l in reversed(range(DEPTH)):
        dy, grads[l] = layer_bwd(dy, {k: v[l] for k, v in params.items()}, saved[l], bsz)
    grads = {k: jnp.stack([grads[l][k] for l in range(DEPTH)]) for k in grads[0]}
    return loss, dy.reshape(bsz, s_len, d), grads


def _gathered_to_full(name, t):
    if name in ("w_in", "w_ffn_in"):
        return jnp.moveaxis(t, 0, 2).reshape(t.shape[1], t.shape[2], -1)
    if name == "w_branch":
        return jnp.moveaxis(t, 0, 2).reshape(t.shape[1], t.shape[2], -1, t.shape[4])
    return jnp.moveaxis(t, 0, 1).reshape(t.shape[1], -1, t.shape[3])


def _full_to_slots(name, t):
    n_l = t.shape[0]
    if name in ("w_in", "w_ffn_in"):
        return jnp.moveaxis(t.reshape(n_l, t.shape[1], N_DEV, -1), 2, 0)
    if name == "w_branch":
        return jnp.moveaxis(t.reshape(n_l, 3, N_DEV, -1, t.shape[3]), 2, 0)
    return jnp.moveaxis(t.reshape(n_l, N_DEV, -1, t.shape[2]), 1, 0)


def _pad_rows(flat, tile_rows):
    n = flat.shape[0]
    per = tile_rows * LANES
    total = -(-n // per) * per
    return jnp.pad(flat, (0, total - n)).reshape(-1, LANES)


def kernel(x, w_in, conv_w, conv_b, w_rg, b_rg, w_ig, b_ig, lru_lambda, sinks, w_branch, w_out, ln1_g, ln1_b, w_ffn_in, w_ffn_out, ln2_g, ln2_b, loss_target, m_w_in, m_conv_w, m_conv_b, m_w_rg, m_b_rg, m_w_ig, m_b_ig, m_lru_lambda, m_sinks, m_w_branch, m_w_out, m_ln1_g, m_ln1_b, m_w_ffn_in, m_w_ffn_out, m_ln2_g, m_ln2_b, v_w_in, v_conv_w, v_conv_b, v_w_rg, v_b_rg, v_w_ig, v_b_ig, v_lru_lambda, v_sinks, v_w_branch, v_w_out, v_ln1_g, v_ln1_b, v_w_ffn_in, v_w_ffn_out, v_ln2_g, v_ln2_b):
    w = dict(w_in=w_in, conv_w=conv_w, conv_b=conv_b, w_rg=w_rg, b_rg=b_rg, w_ig=w_ig, b_ig=b_ig, lru_lambda=lru_lambda, sinks=sinks,
             w_branch=w_branch, w_out=w_out, ln1_g=ln1_g, ln1_b=ln1_b, w_ffn_in=w_ffn_in, w_ffn_out=w_ffn_out, ln2_g=ln2_g, ln2_b=ln2_b)
    m = dict(w_in=m_w_in, conv_w=m_conv_w, conv_b=m_conv_b, w_rg=m_w_rg, b_rg=m_b_rg, w_ig=m_w_ig, b_ig=m_b_ig, lru_lambda=m_lru_lambda,
             sinks=m_sinks, w_branch=m_w_branch, w_out=m_w_out, ln1_g=m_ln1_g, ln1_b=m_ln1_b, w_ffn_in=m_w_ffn_in, w_ffn_out=m_w_ffn_out,
             ln2_g=m_ln2_g, ln2_b=m_ln2_b)
    v = dict(w_in=v_w_in, conv_w=v_conv_w, conv_b=v_conv_b, w_rg=v_w_rg, b_rg=v_b_rg, w_ig=v_w_ig, b_ig=v_b_ig, lru_lambda=v_lru_lambda,
             sinks=v_sinks, w_branch=v_w_branch, w_out=v_w_out, ln1_g=v_ln1_g, ln1_b=v_ln1_b, w_ffn_in=v_w_ffn_in, w_ffn_out=v_w_ffn_out,
             ln2_g=v_ln2_g, ln2_b=v_ln2_b)
    order = ["w_in", "conv_w", "conv_b", "w_rg", "b_rg", "w_ig", "b_ig", "lru_lambda", "sinks", "w_branch", "w_out", "ln1_g", "ln1_b",
             "w_ffn_in", "w_ffn_out", "ln2_g", "ln2_b"]
    me = 4 * lax.axis_index("x") + 2 * lax.axis_index("y") + lax.axis_index("c")

    sizes = [w[k].size for k in BIG]
    packed = jnp.concatenate([w[k].astype(BF16).reshape(-1) for k in BIG]).reshape(-1, LANES)
    gathered = all_gather(packed, name="gather_weights").reshape(N_DEV, -1)
    params, off = {}, 0
    for k, n in zip(BIG, sizes):
        params[k] = _gathered_to_full(k, gathered[:, off:off + n].reshape((N_DEV,) + w[k].shape))
        off += n
    cw = all_gather(conv_w.reshape(-1, LANES), name="gather_conv_w")
    params["conv_w"] = jnp.moveaxis(cw.reshape(N_DEV, DEPTH, CONV_WIDTH, LANES), 0, 2).reshape(DEPTH, CONV_WIDTH, D_MODEL)
    for k in SMALL:
        params[k] = w[k]

    loss_local, grad_x, grads = local_step(x, loss_target, params)
    loss = lax.psum(loss_local, ("x", "y", "c"))

    slots = jnp.concatenate([_full_to_slots(k, grads[k]).reshape(N_DEV, -1) for k in BIG], axis=1)
    recv = all_to_all(slots.astype(BF16).reshape(N_DEV, -1, LANES), name="exchange_grads")
    gsum = sum_slots(recv, name="sum_grads").reshape(-1)
    g_final, off = {}, 0
    for k, n in zip(BIG, sizes):
        g_final[k] = gsum[off:off + n].reshape(w[k].shape)
        off += n

    small_names = list(SMALL) + ["conv_w"]
    small_sizes = [grads[k].size for k in small_names]
    svec = _pad_rows(jnp.concatenate([grads[k].reshape(-1) for k in small_names]), SMALL_ROWS_TILE)
    ssum = sum_slots(all_gather(svec, name="gather_small_grads"), name="sum_small_grads")
    sflat, off = ssum.reshape(-1), 0
    for k, n in zip(small_names, small_sizes):
        g_final[k] = sflat[off:off + n].reshape(grads[k].shape)
        off += n
    g_final["conv_w"] = lax.dynamic_slice_in_dim(g_final["conv_w"], me * LANES, LANES, axis=2)

    delta, new_m, new_v = {}, {}, {}
    for k in list(BIG) + ["conv_w"]:
        cols = w[k].shape[-1]
        two_d = lambda t: t.reshape(-1, cols)
        d_, m_, v_ = adamw(two_d(w[k]), two_d(g_final[k]), two_d(m[k]), two_d(v[k]), name=f"adamw_{k}")
        delta[k], new_m[k], new_v[k] = d_.reshape(w[k].shape), m_.reshape(w[k].shape), v_.reshape(w[k].shape)
    pack_small = lambda dct: _pad_rows(jnp.concatenate([dct[k].reshape(-1) for k in SMALL]), SMALL_ROWS_TILE)
    d_, m_, v_ = adamw(pack_small(w), pack_small(g_final), pack_small(m), pack_small(v), name="adamw_small")
    off = 0
    for k in SMALL:
        n = w[k].size
        for dst, src in ((delta, d_), (new_m, m_), (new_v, v_)):
            dst[k] = src.reshape(-1)[off:off + n].reshape(w[k].shape)
        off += n
    return (loss, grad_x, *[g_final[k] for k in order], *[delta[k] for k in order], *[new_m[k] for k in order], *[new_v[k] for k in order])
```

```python
import functools
import math

import jax
import jax.numpy as jnp
from jax import lax
from jax.experimental import pallas as pl
from jax.experimental.pallas import tpu as pltpu

F32 = jnp.float32
BF16 = jnp.bfloat16

N_DEV = 8
DEPTH = 4
D_MODEL = 1024
HEAD_DIM = 64
LANES = 128
N_HEADS = D_MODEL // HEAD_DIM
SWA_KV_HEADS = 4
ATT_BLOCK = 128
DILATIONS = (1, 4, 16)
CONV_WIDTH = 4
LRU_C = 8.0
FF_HIDDEN = 2816
ALPHA = (2.0 * DEPTH) ** 0.25
LN_EPS = 1e-5
NEG_INF = -1e30
W_F = 5 * D_MODEL
W_QS = D_MODEL + 2 * SWA_KV_HEADS * HEAD_DIM
W_QD = 3 * D_MODEL

ADAM_LR = 0.001
ADAM_B1 = 0.9
ADAM_B2 = 0.999
ADAM_EPS = 1e-08
ADAM_WD = 0.01
ADAM_STEP = 10

VMEM_LIMIT = 56 * 1024 * 1024
MESH = pl.DeviceIdType.MESH


def _pick(n, cands):
    for c in cands:
        if n % c == 0:
            return c
    raise ValueError(f"no tile for {n} among {cands}")


def _params(sem):
    return pltpu.CompilerParams(dimension_semantics=sem, vmem_limit_bytes=VMEM_LIMIT)


def _tile(n, cap):
    best = None
    for t in range(LANES, cap + 1, LANES):
        if n % t == 0:
            best = t
    assert best is not None, (n, cap)
    return best


def matmul(a, b, *, name, trans_a=False, trans_b=False, out_dtype=F32, add=None, add_scale=1.0):
    if trans_a:
        k_dim, m_dim = a.shape
    else:
        m_dim, k_dim = a.shape
    n_dim = b.shape[0] if trans_b else b.shape[1]
    assert (b.shape[1] if trans_b else b.shape[0]) == k_dim
    tm = _tile(m_dim, 1024)
    tn = _tile(n_dim, 1408)
    tk = _tile(k_dim, 1408)
    nk = k_dim // tk
    dims = (((0 if trans_a else 1,), (1 if trans_b else 0,)), ((), ()))

    def body(*refs):
        if add is None:
            a_ref, b_ref, o_ref, acc_ref = refs
            add_ref = None
        else:
            a_ref, b_ref, add_ref, o_ref, acc_ref = refs
        k = pl.program_id(2)
        part = lax.dot_general(a_ref[...].astype(BF16), b_ref[...].astype(BF16), dims, preferred_element_type=F32)

        def finish(r):
            if add_ref is not None:
                r = r + add_scale * add_ref[...].astype(F32)
            o_ref[...] = r.astype(out_dtype)

        if nk == 1:
            finish(part)
        else:
            @pl.when(k == 0)
            def _():
                acc_ref[...] = part

            @pl.when((k > 0) & (k < nk - 1))
            def _():
                acc_ref[...] += part

            @pl.when(k == nk - 1)
            def _():
                finish(acc_ref[...] + part)

    a_spec = pl.BlockSpec((tk, tm), lambda i, j, k: (k, i)) if trans_a else pl.BlockSpec((tm, tk), lambda i, j, k: (i, k))
    b_spec = pl.BlockSpec((tn, tk), lambda i, j, k: (j, k)) if trans_b else pl.BlockSpec((tk, tn), lambda i, j, k: (k, j))
    in_specs = [a_spec, b_spec]
    args = [a, b]
    if add is not None:
        in_specs.append(pl.BlockSpec((tm, tn), lambda i, j, k: (i, j)))
        args.append(add)
    return pl.pallas_call(
        body,
        out_shape=jax.ShapeDtypeStruct((m_dim, n_dim), out_dtype),
        grid=(m_dim // tm, n_dim // tn, nk),
        in_specs=in_specs,
        out_specs=pl.BlockSpec((tm, tn), lambda i, j, k: (i, j)),
        scratch_shapes=[pltpu.VMEM((tm, tn) if nk > 1 else (8, LANES), F32)],
        compiler_params=_params(("parallel", "parallel", "arbitrary")),
        name=name,
    )(*args)


def ln_fwd(x, r, g, b, *, name):
    t_dim, d = x.shape
    tr = _pick(t_dim, (256, 128, 8))

    def body(x_ref, r_ref, g_ref, b_ref, y_ref, yb_ref, z_ref):
        z = ALPHA * x_ref[...] + r_ref[...]
        mu = jnp.mean(z, axis=-1, keepdims=True)
        zc = z - mu
        var = jnp.mean(zc * zc, axis=-1, keepdims=True)
        y = zc * lax.rsqrt(var + LN_EPS) * g_ref[...] + b_ref[...]
        y_ref[...] = y
        yb_ref[...] = y.astype(BF16)
        z_ref[...] = z

    row = pl.BlockSpec((tr, d), lambda i: (i, 0))
    vec = pl.BlockSpec((1, d), lambda i: (0, 0))
    return pl.pallas_call(
        body,
        out_shape=(jax.ShapeDtypeStruct((t_dim, d), F32), jax.ShapeDtypeStruct((t_dim, d), BF16), jax.ShapeDtypeStruct((t_dim, d), F32)),
        grid=(t_dim // tr,),
        in_specs=[row, row, vec, vec],
        out_specs=(row, row, row),
        compiler_params=_params(("parallel",)),
        name=name,
    )(x, r, g.reshape(1, d), b.reshape(1, d))


def ln_bwd(dy, z, g, *, name):
    t_dim, d = dy.shape
    tr = _pick(t_dim, (256, 128, 8))

    def body(dy_ref, z_ref, g_ref, dz_ref, dzb_ref, dg_ref, db_ref):
        @pl.when(pl.program_id(0) == 0)
        def _():
            dg_ref[...] = jnp.zeros_like(dg_ref)
            db_ref[...] = jnp.zeros_like(db_ref)

        z = z_ref[...]
        dyv = dy_ref[...]
        mu = jnp.mean(z, axis=-1, keepdims=True)
        zc = z - mu
        var = jnp.mean(zc * zc, axis=-1, keepdims=True)
        rstd = lax.rsqrt(var + LN_EPS)
        xhat = zc * rstd
        dxhat = dyv * g_ref[...]
        m1 = jnp.mean(dxhat, axis=-1, keepdims=True)
        m2 = jnp.mean(dxhat * xhat, axis=-1, keepdims=True)
        dz = rstd * (dxhat - m1 - xhat * m2)
        dz_ref[...] = dz
        dzb_ref[...] = dz.astype(BF16)
        dg_ref[...] += jnp.sum(dyv * xhat, axis=0, keepdims=True)
        db_ref[...] += jnp.sum(dyv, axis=0, keepdims=True)

    row = pl.BlockSpec((tr, d), lambda i: (i, 0))
    vec = pl.BlockSpec((1, d), lambda i: (0, 0))
    return pl.pallas_call(
        body,
        out_shape=(jax.ShapeDtypeStruct((t_dim, d), F32), jax.ShapeDtypeStruct((t_dim, d), BF16),
                   jax.ShapeDtypeStruct((1, d), F32), jax.ShapeDtypeStruct((1, d), F32)),
        grid=(t_dim // tr,),
        in_specs=[row, row, vec],
        out_specs=(row, row, vec, vec),
        compiler_params=_params(("arbitrary",)),
        name=name,
    )(dy, z, g.reshape(1, d))


def loss_head(y, target, *, name):
    t_dim, d = y.shape
    tr = _pick(t_dim, (256, 128, 8))

    def body(y_ref, t_ref, dy_ref, sq_ref):
        @pl.when(pl.program_id(0) == 0)
        def _():
            sq_ref[...] = jnp.zeros_like(sq_ref)

        diff = y_ref[...] - t_ref[...]
        dy_ref[...] = diff / d
        sq_ref[...] += jnp.sum(diff * diff, axis=0, keepdims=True)

    row = pl.BlockSpec((tr, d), lambda i: (i, 0))
    vec = pl.BlockSpec((1, d), lambda i: (0, 0))
    return pl.pallas_call(
        body,
        out_shape=(jax.ShapeDtypeStruct((t_dim, d), F32), jax.ShapeDtypeStruct((1, d), F32)),
        grid=(t_dim // tr,),
        in_specs=[row, row],
        out_specs=(row, vec),
        compiler_params=_params(("arbitrary",)),
        name=name,
    )(y, target)


def _sigmoid(x):
    return 1.0 / (1.0 + jnp.exp(-x))


def swiglu_fwd(h13, *, name):
    t_dim = h13.shape[0]
    f = h13.shape[1] // 2
    tr = _pick(t_dim, (256, 128, 8))

    def body(h1_ref, h3_ref, act_ref):
        h1 = h1_ref[...]
        act_ref[...] = (h1 * _sigmoid(h1) * h3_ref[...]).astype(BF16)

    return pl.pallas_call(
        body,
        out_shape=jax.ShapeDtypeStruct((t_dim, f), BF16),
        grid=(t_dim // tr,),
        in_specs=[pl.BlockSpec((tr, f), lambda i: (i, 0)), pl.BlockSpec((tr, f), lambda i: (i, 1))],
        out_specs=pl.BlockSpec((tr, f), lambda i: (i, 0)),
        compiler_params=_params(("parallel",)),
        name=name,
    )(h13, h13)


def swiglu_bwd(dact, h13, *, name):
    t_dim = h13.shape[0]
    f = h13.shape[1] // 2
    tr = _pick(t_dim, (256, 128, 8))

    def body(da_ref, h1_ref, h3_ref, dh_ref):
        h1 = h1_ref[...]
        da = da_ref[...]
        sg = _sigmoid(h1)
        dh_ref[:, :f] = (da * h3_ref[...] * sg * (1.0 + h1 * (1.0 - sg))).astype(BF16)
        dh_ref[:, f:] = (da * h1 * sg).astype(BF16)

    return pl.pallas_call(
        body,
        out_shape=jax.ShapeDtypeStruct((t_dim, 2 * f), BF16),
        grid=(t_dim // tr,),
        in_specs=[pl.BlockSpec((tr, f), lambda i: (i, 0)), pl.BlockSpec((tr, f), lambda i: (i, 0)),
                  pl.BlockSpec((tr, f), lambda i: (i, 1))],
        out_specs=pl.BlockSpec((tr, 2 * f), lambda i: (i, 0)),
        compiler_params=_params(("parallel",)),
        name=name,
    )(dact, h13, h13)


def merge_fwd(proj_f, br, *, name):
    t_dim, d = br[0].shape
    tr = _pick(t_dim, (256, 128, 8))

    def body(g0, g1, g2, b0, b1, b2, o_ref):
        o_ref[...] = (_sigmoid(g0[...]) * b0[...] + _sigmoid(g1[...]) * b1[...] + _sigmoid(g2[...]) * b2[...]).astype(BF16)

    row = pl.BlockSpec((tr, d), lambda i: (i, 0))
    gate = [pl.BlockSpec((tr, d), functools.partial(lambda n, i: (i, 2 + n), n)) for n in range(3)]
    return pl.pallas_call(
        body,
        out_shape=jax.ShapeDtypeStruct((t_dim, d), BF16),
        grid=(t_dim // tr,),
        in_specs=gate + [row, row, row],
        out_specs=row,
        compiler_params=_params(("parallel",)),
        name=name,
    )(proj_f, proj_f, proj_f, *br)


def merge_bwd(dmerged, proj_f, br, *, name):
    t_dim, d = dmerged.shape
    tr = _pick(t_dim, (256, 128, 8))

    def body(dm_ref, g0, g1, g2, b0, b1, b2, d0, d1, d2, dg_ref):
        dm = dm_ref[...]
        for n, (g, b, o) in enumerate(((g0, b0, d0), (g1, b1, d1), (g2, b2, d2))):
            sg = _sigmoid(g[...])
            o[...] = (dm * sg).astype(BF16)
            dg_ref[:, n * d:(n + 1) * d] = (dm * b[...] * sg * (1.0 - sg)).astype(BF16)

    row = pl.BlockSpec((tr, d), lambda i: (i, 0))
    gate = [pl.BlockSpec((tr, d), functools.partial(lambda n, i: (i, 2 + n), n)) for n in range(3)]
    return pl.pallas_call(
        body,
        out_shape=(jax.ShapeDtypeStruct((t_dim, d), BF16),) * 3 + (jax.ShapeDtypeStruct((t_dim, 3 * d), BF16),),
        grid=(t_dim // tr,),
        in_specs=[row] + gate + [row, row, row],
        out_specs=(row, row, row, pl.BlockSpec((tr, 3 * d), lambda i: (i, 0))),
        compiler_params=_params(("parallel",)),
        name=name,
    )(dmerged, proj_f, proj_f, proj_f, *br)


GELU_C = math.sqrt(2.0 / math.pi)
PAD = 8


def _gelu(x):
    return 0.5 * x * (1.0 + jnp.tanh(GELU_C * (x + 0.044715 * x * x * x)))


def _gelu_grad(x):
    t = jnp.tanh(GELU_C * (x + 0.044715 * x * x * x))
    return 0.5 * (1.0 + t) + 0.5 * x * (1.0 - t * t) * GELU_C * (1.0 + 3.0 * 0.044715 * x * x)


def _neg_expm1(x):
    series = -x * (1.0 + x * (0.5 + x * (1.0 / 6.0 + x * (1.0 / 24.0 + x * (1.0 / 120.0)))))
    return jnp.where(x > -0.1, series, 1.0 - jnp.exp(x))


def _lru_gates(xv, cw_ref, cb_ref, wr_ref, wi_ref, br_ref, bi_ref, lam_ref, pad_ref, s_len):
    pad_ref[pl.ds(0, PAD), :] = jnp.zeros((PAD, LANES), F32)
    pad_ref[pl.ds(PAD, s_len), :] = xv
    xc = cb_ref[...] + jnp.zeros((s_len, LANES), F32)
    for j in range(CONV_WIDTH):
        xc = xc + pad_ref[pl.ds(PAD - (CONV_WIDTH - 1) + j, s_len), :] * cw_ref[pl.ds(j, 1), :]
    xcb = xc.astype(BF16)
    r = _sigmoid(jnp.dot(xcb, wr_ref[0].astype(BF16), preferred_element_type=F32) + br_ref[...])
    i = _sigmoid(jnp.dot(xcb, wi_ref[0].astype(BF16), preferred_element_type=F32) + bi_ref[...])
    nl = -lam_ref[...]
    sp = jnp.maximum(nl, 0.0) + jnp.log(1.0 + jnp.exp(-jnp.abs(nl)))
    log_a = -LRU_C * r * sp
    a = jnp.exp(log_a)
    mult = jnp.sqrt(_neg_expm1(2.0 * log_a))
    return xc, r, i, sp, a, mult


def _tile_scan(a, b, row, reverse):
    for s in (1, 2, 4):
        if reverse:
            a_sh = pltpu.roll(a, 8 - s, 0)
            b_sh = pltpu.roll(b, 8 - s, 0)
            m = row + s <= 7
        else:
            a_sh = pltpu.roll(a, s, 0)
            b_sh = pltpu.roll(b, s, 0)
            m = row >= s
        b = jnp.where(m, a * b_sh + b, b)
        a = jnp.where(m, a * a_sh, a)
    return a, b


def lru_fwd(proj_f, conv_w, conv_b, wr_bd, wi_bd, b_rg, b_ig, lam, *, name):
    bsz, s_len, _ = proj_f.shape
    d = D_MODEL
    ncb = d // LANES
    n_tiles = s_len // 8

    def body(x_ref, g_ref, cw_ref, cb_ref, wr_ref, wi_ref, br_ref, bi_ref, lam_ref, y_ref, h_ref, pad_ref, a_s, b_s):
        xc, r, i, sp, a, mult = _lru_gates(x_ref[0], cw_ref, cb_ref, wr_ref, wi_ref, br_ref, bi_ref, lam_ref, pad_ref, s_len)
        a_s[...] = a
        b_s[...] = mult * (i * xc)
        row = lax.broadcasted_iota(jnp.int32, (8, LANES), 0)

        def tile(t, carry):
            i0 = pl.multiple_of(t * 8, 8)
            ac, hl = _tile_scan(a_s[pl.ds(i0, 8), :], b_s[pl.ds(i0, 8), :], row, False)
            h = hl + ac * carry
            h_ref[0, pl.ds(i0, 8), :] = h
            return jnp.broadcast_to(h[7:8, :], (8, LANES))

        lax.fori_loop(0, n_tiles, tile, jnp.zeros((8, LANES), F32))
        y_ref[0] = (h_ref[0] * _gelu(g_ref[0])).astype(BF16)

    slab = lambda off: pl.BlockSpec((1, s_len, LANES), functools.partial(lambda o, c, b: (b, 0, o + c), off))
    vec = pl.BlockSpec((1, LANES), lambda c, b: (0, c))
    mat = pl.BlockSpec((1, LANES, LANES), lambda c, b: (c, 0, 0))
    out = pl.BlockSpec((1, s_len, LANES), lambda c, b: (b, 0, c))
    return pl.pallas_call(
        body,
        out_shape=(jax.ShapeDtypeStruct((bsz, s_len, d), BF16), jax.ShapeDtypeStruct((bsz, s_len, d), F32)),
        grid=(ncb, bsz),
        in_specs=[slab(0), slab(ncb), pl.BlockSpec((CONV_WIDTH, LANES), lambda c, b: (0, c)), vec, mat, mat, vec, vec, vec],
        out_specs=(out, out),
        scratch_shapes=[pltpu.VMEM((s_len + 2 * PAD, LANES), F32), pltpu.VMEM((s_len, LANES), F32), pltpu.VMEM((s_len, LANES), F32)],
        compiler_params=_params(("parallel", "parallel")),
        name=name,
    )(proj_f, proj_f, conv_w, conv_b.reshape(1, d), wr_bd, wi_bd, b_rg.reshape(1, d), b_ig.reshape(1, d), lam.reshape(1, d))


def lru_bwd(dy, proj_f, h, conv_w, conv_b, wr_bd, wi_bd, wr_bd_t, wi_bd_t, b_rg, b_ig, lam, *, name):
    bsz, s_len, _ = proj_f.shape
    d = D_MODEL
    ncb = d // LANES
    n_tiles = s_len // 8

    def body(dy_ref, x_ref, g_ref, h_ref, cw_ref, cb_ref, wr_ref, wi_ref, wrt_ref, wit_ref, br_ref, bi_ref, lam_ref,
             dx_ref, dg_ref, dcw_ref, dcb_ref, dbr_ref, dbi_ref, dlam_ref, dwr_ref, dwi_ref, pad_ref, a_s, b_s, l_s):
        @pl.when(pl.program_id(1) == 0)
        def _():
            for ref in (dcw_ref, dcb_ref, dbr_ref, dbi_ref, dlam_ref, dwr_ref, dwi_ref):
                ref[...] = jnp.zeros_like(ref)

        xc, r, i, sp, a, mult = _lru_gates(x_ref[0], cw_ref, cb_ref, wr_ref, wi_ref, br_ref, bi_ref, lam_ref, pad_ref, s_len)
        gate = g_ref[0]
        hv = h_ref[0]
        dyv = dy_ref[0]
        dg_ref[0] = (dyv * hv * _gelu_grad(gate)).astype(BF16)
        b_s[...] = dyv * _gelu(gate)
        l_s[pl.ds(0, s_len), :] = a
        l_s[pl.ds(s_len, PAD), :] = jnp.zeros((PAD, LANES), F32)
        a_s[...] = l_s[pl.ds(1, s_len), :]
        row = lax.broadcasted_iota(jnp.int32, (8, LANES), 0)

        def tile(t, carry):
            i0 = pl.multiple_of((n_tiles - 1 - t) * 8, 8)
            ac, ll = _tile_scan(a_s[pl.ds(i0, 8), :], b_s[pl.ds(i0, 8), :], row, True)
            lmb = ll + ac * carry
            b_s[pl.ds(i0, 8), :] = lmb
            return jnp.broadcast_to(lmb[0:1, :], (8, LANES))

        lax.fori_loop(0, n_tiles, tile, jnp.zeros((8, LANES), F32))
        lmb = b_s[...]
        l_s[pl.ds(0, PAD), :] = jnp.zeros((PAD, LANES), F32)
        l_s[pl.ds(PAD, s_len), :] = hv
        h_prev = l_s[pl.ds(PAD - 1, s_len), :]
        da = lmb * h_prev
        dmult = lmb * (i * xc)
        di = lmb * mult * xc
        dxc = lmb * mult * i
        dlog_a = da * a - dmult * a * a / mult
        dr = -LRU_C * sp * dlog_a
        dsp = jnp.sum(-LRU_C * r * dlog_a, axis=0, keepdims=True)
        dlam_ref[...] += dsp * (-_sigmoid(-lam_ref[...]))
        dpr = dr * r * (1.0 - r)
        dpi = di * i * (1.0 - i)
        dprb = dpr.astype(BF16)
        dpib = dpi.astype(BF16)
        xcb = xc.astype(BF16)
        dbr_ref[...] += jnp.sum(dpr, axis=0, keepdims=True)
        dbi_ref[...] += jnp.sum(dpi, axis=0, keepdims=True)
        tn = (((0,), (0,)), ((), ()))
        dwr_ref[0] += lax.dot_general(xcb, dprb, tn, preferred_element_type=F32)
        dwi_ref[0] += lax.dot_general(xcb, dpib, tn, preferred_element_type=F32)
        dxc = (dxc + jnp.dot(dprb, wrt_ref[0].astype(BF16), preferred_element_type=F32)
               + jnp.dot(dpib, wit_ref[0].astype(BF16), preferred_element_type=F32))
        dcb_ref[...] += jnp.sum(dxc, axis=0, keepdims=True)
        for j in range(CONV_WIDTH):
            dcw_ref[pl.ds(j, 1), :] += jnp.sum(dxc * pad_ref[pl.ds(PAD - (CONV_WIDTH - 1) + j, s_len), :], axis=0, keepdims=True)
        l_s[pl.ds(0, s_len), :] = dxc
        l_s[pl.ds(s_len, PAD), :] = jnp.zeros((PAD, LANES), F32)
        dx = jnp.zeros((s_len, LANES), F32)
        for j in range(CONV_WIDTH):
            dx = dx + l_s[pl.ds(CONV_WIDTH - 1 - j, s_len), :] * cw_ref[pl.ds(j, 1), :]
        dx_ref[0] = dx.astype(BF16)

    slab = lambda off: pl.BlockSpec((1, s_len, LANES), functools.partial(lambda o, c, b: (b, 0, o + c), off))
    vec = pl.BlockSpec((1, LANES), lambda c, b: (0, c))
    mat = pl.BlockSpec((1, LANES, LANES), lambda c, b: (c, 0, 0))
    cw = pl.BlockSpec((CONV_WIDTH, LANES), lambda c, b: (0, c))
    out = pl.BlockSpec((1, s_len, LANES), lambda c, b: (b, 0, c))
    vshape = jax.ShapeDtypeStruct((1, d), F32)
    mshape = jax.ShapeDtypeStruct((ncb, LANES, LANES), F32)
    return pl.pallas_call(
        body,
        out_shape=(jax.ShapeDtypeStruct((bsz, s_len, d), BF16),) * 2
        + (jax.ShapeDtypeStruct((CONV_WIDTH, d), F32), vshape, vshape, vshape, vshape, mshape, mshape),
        grid=(ncb, bsz),
        in_specs=[out, slab(0), slab(ncb), out, cw, vec, mat, mat, mat, mat, vec, vec, vec],
        out_specs=(out, out, cw, vec, vec, vec, vec, mat, mat),
        scratch_shapes=[pltpu.VMEM((s_len + 2 * PAD, LANES), F32), pltpu.VMEM((s_len, LANES), F32), pltpu.VMEM((s_len, LANES), F32),
                        pltpu.VMEM((s_len + 2 * PAD, LANES), F32)],
        compiler_params=_params(("parallel", "arbitrary")),
        name=name,
    )(dy, proj_f, proj_f, h, conv_w, conv_b.reshape(1, d), wr_bd, wi_bd, wr_bd_t, wi_bd_t,
      b_rg.reshape(1, d), b_ig.reshape(1, d), lam.reshape(1, d))


def _kv_place(head, n_kv_heads):
    kv = head // (N_HEADS // n_kv_heads)
    return kv // 2, kv % 2


def _band_mask(n, single):
    if single:
        qi = lax.broadcasted_iota(jnp.int32, (ATT_BLOCK, ATT_BLOCK), 0)
        return qi >= lax.broadcasted_iota(jnp.int32, (ATT_BLOCK, ATT_BLOCK), 1)
    qi = lax.broadcasted_iota(jnp.int32, (ATT_BLOCK, 2 * ATT_BLOCK), 0)
    kj = lax.broadcasted_iota(jnp.int32, (ATT_BLOCK, 2 * ATT_BLOCK), 1)
    rel = qi + ATT_BLOCK - kj
    return (rel >= 0) & (rel <= ATT_BLOCK) & ((n > 0) | (kj >= ATT_BLOCK))


def _half_masks(dtype):
    lane = lax.broadcasted_iota(jnp.int32, (1, LANES), 1)
    return [(lane < HEAD_DIM).astype(dtype), (lane >= HEAD_DIM).astype(dtype)]


NT = (((1,), (1,)), ((), ()))
TN = (((0,), (0,)), ((), ()))


def _qkv_specs(dil, q_blk, k_blk, v_blk, ckv, clamp):
    qw = D_MODEL // LANES * LANES
    return [
        pl.BlockSpec((1, ATT_BLOCK, qw), lambda b, j, n: (b, clamp(n), j * (q_blk[1]) + q_blk[0])),
        pl.BlockSpec((1, ATT_BLOCK, ckv), lambda b, j, n: (b, jnp.maximum(clamp(n) - 1, 0), j * k_blk[1] + k_blk[0])),
        pl.BlockSpec((1, ATT_BLOCK, ckv), lambda b, j, n: (b, clamp(n), j * k_blk[1] + k_blk[0])),
        pl.BlockSpec((1, ATT_BLOCK, ckv), lambda b, j, n: (b, jnp.maximum(clamp(n) - 1, 0), j * v_blk[1] + v_blk[0])),
        pl.BlockSpec((1, ATT_BLOCK, ckv), lambda b, j, n: (b, clamp(n), j * v_blk[1] + v_blk[0])),
    ]


def attn_fwd(qkv, *, dil, n_kv_heads, sinks, name, emit_bf16=False):
    bsz, s_len, width = qkv.shape
    ckv = n_kv_heads * HEAD_DIM
    l_sub = s_len // dil
    nb = l_sub // ATT_BLOCK
    view = qkv.reshape(bsz, l_sub, dil * width)
    scale = HEAD_DIM ** -0.5
    q_blk = (0, width // D_MODEL)
    k_blk = (D_MODEL // ckv, width // ckv)
    v_blk = (D_MODEL // ckv + 1, width // ckv)
    assert (dil == 1 or width % D_MODEL == 0) and width % ckv == 0 and D_MODEL % ckv == 0

    single = nb == 1

    def body(*refs):
        refs = list(refs)
        sink_ref = refs.pop(0) if sinks is not None else None
        ob_ref = refs.pop() if emit_bf16 else None
        q_ref, kp_ref, kc_ref, vp_ref, vc_ref, o_ref, lse_ref = refs
        n = pl.program_id(2)
        mask = _band_mask(n, single)
        hm = _half_masks(BF16)
        hmf = _half_masks(F32)
        kk = kc_ref[0] if single else jnp.concatenate([kp_ref[0], kc_ref[0]], axis=0)
        vv = vc_ref[0] if single else jnp.concatenate([vp_ref[0], vc_ref[0]], axis=0)
        for hp in range(N_HEADS // 2):
            q2 = q_ref[0, :, hp * LANES:(hp + 1) * LANES]
            o2 = jnp.zeros((ATT_BLOCK, LANES), F32)
            l2 = jnp.zeros((ATT_BLOCK, LANES), F32)
            for a in range(2):
                kb, kh = _kv_place(2 * hp + a, n_kv_heads)
                k2 = kk[:, kb * LANES:(kb + 1) * LANES]
                v2 = vv[:, kb * LANES:(kb + 1) * LANES]
                if kh != a:
                    k2 = pltpu.roll(k2, HEAD_DIM, 1)
                    v2 = pltpu.roll(v2, HEAD_DIM, 1)
                s = lax.dot_general(q2 * hm[a], k2, NT, preferred_element_type=F32) * scale
                s = jnp.where(mask, s, NEG_INF)
                m = jnp.max(s, axis=-1, keepdims=True)
                if sink_ref is not None:
                    sk = sink_ref[2 * hp + a]
                    m = jnp.maximum(m, sk)
                p = jnp.exp(s - m)
                den = jnp.sum(p, axis=-1, keepdims=True)
                if sink_ref is not None:
                    den = den + jnp.exp(sk - m)
                o2 = o2 + jnp.dot(p.astype(BF16), v2 * hm[a], preferred_element_type=F32) / den
                l2 = l2 + (m + jnp.log(den)) * hmf[a]
            o_ref[0, :, hp * LANES:(hp + 1) * LANES] = o2
            lse_ref[0, :, hp * LANES:(hp + 1) * LANES] = l2
            if ob_ref is not None:
                ob_ref[0, :, hp * LANES:(hp + 1) * LANES] = o2.astype(BF16)

    in_specs = _qkv_specs(dil, q_blk, k_blk, v_blk, ckv, lambda n: n)
    args = [view] * 5
    if sinks is not None:
        in_specs = [pl.BlockSpec(memory_space=pltpu.SMEM)] + in_specs
        args = [sinks] + args
    out = pl.BlockSpec((1, ATT_BLOCK, D_MODEL), lambda b, j, n: (b, n, j))
    res = pl.pallas_call(
        body,
        out_shape=(jax.ShapeDtypeStruct((bsz, l_sub, dil * D_MODEL), F32),) * 2
        + ((jax.ShapeDtypeStruct((bsz, l_sub, dil * D_MODEL), BF16),) if emit_bf16 else ()),
        grid=(bsz, dil, nb),
        in_specs=in_specs,
        out_specs=(out,) * (3 if emit_bf16 else 2),
        compiler_params=_params(("parallel", "parallel", "arbitrary")),
        name=name,
    )(*args)
    return tuple(t.reshape(bsz, s_len, D_MODEL) for t in res)


def attn_bwd(qkv, o, lse, do, acc, *, dil, n_kv_heads, name):
    bsz, s_len, width = qkv.shape
    ckv = n_kv_heads * HEAD_DIM
    l_sub = s_len // dil
    nb = l_sub // ATT_BLOCK
    view = qkv.reshape(bsz, l_sub, dil * width)
    scale = HEAD_DIM ** -0.5
    q_blk = (0, width // D_MODEL)
    k_blk = (D_MODEL // ckv, width // ckv)
    v_blk = (D_MODEL // ckv + 1, width // ckv)
    single = nb == 1

    def body(*refs):
        if acc is None:
            q_ref, kp_ref, kc_ref, vp_ref, vc_ref, o_ref, lse_ref, do_ref, dq_ref, dk_ref, dv_ref, dkk, dvv, ck, cv = refs
            aq_ref = ak_ref = av_ref = None
        else:
            (q_ref, kp_ref, kc_ref, vp_ref, vc_ref, o_ref, lse_ref, do_ref, aq_ref, ak_ref, av_ref,
             dq_ref, dk_ref, dv_ref, dkk, dvv, ck, cv) = refs
        n = pl.program_id(2)

        @pl.when(n < nb)
        def _():
            mask = _band_mask(n, single)
            hm = _half_masks(BF16)
            hmf = _half_masks(F32)
            kk = kc_ref[0] if single else jnp.concatenate([kp_ref[0], kc_ref[0]], axis=0)
            vv = vc_ref[0] if single else jnp.concatenate([vp_ref[0], vc_ref[0]], axis=0)
            krows = pl.ds(ATT_BLOCK, ATT_BLOCK) if single else pl.ds(0, 2 * ATT_BLOCK)
            dkk[...] = jnp.zeros_like(dkk)
            dvv[...] = jnp.zeros_like(dvv)
            for hp in range(N_HEADS // 2):
                cols = slice(hp * LANES, (hp + 1) * LANES)
                q2 = q_ref[0, :, cols]
                do2f = do_ref[0, :, cols]
                do2 = do2f.astype(BF16)
                dd2 = do2f * o_ref[0, :, cols]
                l2 = lse_ref[0, :, cols]
                dq2 = jnp.zeros((ATT_BLOCK, LANES), F32)
                for a in range(2):
                    kb, kh = _kv_place(2 * hp + a, n_kv_heads)
                    kcols = slice(kb * LANES, (kb + 1) * LANES)
                    k2 = kk[:, kcols]
                    v2 = vv[:, kcols]
                    if kh != a:
                        k2 = pltpu.roll(k2, HEAD_DIM, 1)
                        v2 = pltpu.roll(v2, HEAD_DIM, 1)
                    qm = q2 * hm[a]
                    dom = do2 * hm[a]
                    dsum = jnp.sum(dd2 * hmf[a], axis=-1, keepdims=True)
                    lse_h = jnp.max(jnp.where(hmf[a] > 0.5, l2, NEG_INF), axis=-1, keepdims=True)
                    s = lax.dot_general(qm, k2, NT, preferred_element_type=F32) * scale
                    s = jnp.where(mask, s, NEG_INF)
                    p = jnp.exp(s - lse_h)
                    dp = lax.dot_general(dom, v2, NT, preferred_element_type=F32)
                    ds = (p * (dp - dsum) * scale).astype(BF16)
                    dq2 = dq2 + jnp.dot(ds, k2 * hm[a], preferred_element_type=F32)
                    dk_c = lax.dot_general(ds, qm, TN, preferred_element_type=F32)
                    dv_c = lax.dot_general(p.astype(BF16), dom, TN, preferred_element_type=F32)
                    if kh != a:
                        dk_c = pltpu.roll(dk_c, HEAD_DIM, 1)
                        dv_c = pltpu.roll(dv_c, HEAD_DIM, 1)
                    dkk[krows, kcols] += dk_c
                    dvv[krows, kcols] += dv_c
                if aq_ref is not None:
                    dq2 = dq2 + aq_ref[0, :, cols]
                dq_ref[0, :, cols] = dq2

        @pl.when((n >= 1) & (n < nb))
        def _():
            dk_ref[0] = ck[...] + dkk[pl.ds(0, ATT_BLOCK), :] + (0.0 if ak_ref is None else ak_ref[0])
            dv_ref[0] = cv[...] + dvv[pl.ds(0, ATT_BLOCK), :] + (0.0 if av_ref is None else av_ref[0])

        @pl.when(n == nb)
        def _():
            dk_ref[0] = ck[...] + (0.0 if ak_ref is None else ak_ref[0])
            dv_ref[0] = cv[...] + (0.0 if av_ref is None else av_ref[0])

        @pl.when(n < nb)
        def _():
            ck[...] = dkk[pl.ds(ATT_BLOCK, ATT_BLOCK), :]
            cv[...] = dvv[pl.ds(ATT_BLOCK, ATT_BLOCK), :]

    clamp = lambda n: jnp.minimum(n, nb - 1)
    prev = lambda n: jnp.maximum(n - 1, 0)
    row = pl.BlockSpec((1, ATT_BLOCK, D_MODEL), lambda b, j, n: (b, clamp(n), j))
    kv_out = pl.BlockSpec((1, ATT_BLOCK, ckv), lambda b, j, n: (b, prev(n), j))
    in_specs = _qkv_specs(dil, q_blk, k_blk, v_blk, ckv, clamp) + [row, row, row]
    rs = lambda t: t.reshape(bsz, l_sub, dil * t.shape[-1])
    args = [view] * 5 + [rs(o), rs(lse), rs(do)]
    if acc is not None:
        in_specs += [row, kv_out, kv_out]
        args += [rs(t) for t in acc]
    dq, dk, dv = pl.pallas_call(
        body,
        out_shape=(jax.ShapeDtypeStruct((bsz, l_sub, dil * D_MODEL), F32),
                   jax.ShapeDtypeStruct((bsz, l_sub, dil * ckv), F32), jax.ShapeDtypeStruct((bsz, l_sub, dil * ckv), F32)),
        grid=(bsz, dil, nb + 1),
        in_specs=in_specs,
        out_specs=(row, kv_out, kv_out),
        scratch_shapes=[pltpu.VMEM((2 * ATT_BLOCK, ckv), F32), pltpu.VMEM((2 * ATT_BLOCK, ckv), F32),
                        pltpu.VMEM((ATT_BLOCK, ckv), F32), pltpu.VMEM((ATT_BLOCK, ckv), F32)],
        compiler_params=_params(("parallel", "parallel", "arbitrary")),
        name=name,
    )(*args)
    return dq.reshape(bsz, s_len, D_MODEL), dk.reshape(bsz, s_len, ckv), dv.reshape(bsz, s_len, ckv)


def dil_combine(os_, lses, *, name):
    t_dim, d = os_[0].shape
    tr = _pick(t_dim, (256, 128, 8))

    def body(o0, o1, o2, l0, l1, l2, y_ref, lt_ref, yb_ref):
        la, lb, lc = l0[...], l1[...], l2[...]
        m = jnp.maximum(jnp.maximum(la, lb), lc)
        ea, eb, ec = jnp.exp(la - m), jnp.exp(lb - m), jnp.exp(lc - m)
        tot = ea + eb + ec
        y = (ea / tot) * o0[...] + (eb / tot) * o1[...] + (ec / tot) * o2[...]
        y_ref[...] = y
        yb_ref[...] = y.astype(BF16)
        lt_ref[...] = m + jnp.log(tot)

    row = pl.BlockSpec((tr, d), lambda i: (i, 0))
    return pl.pallas_call(
        body,
        out_shape=(jax.ShapeDtypeStruct((t_dim, d), F32),) * 2 + (jax.ShapeDtypeStruct((t_dim, d), BF16),),
        grid=(t_dim // tr,),
        in_specs=[row] * 6,
        out_specs=(row, row, row),
        compiler_params=_params(("parallel",)),
        name=name,
    )(*os_, *lses)


def sink_grad(do, o, lse, sink_lanes, *, name):
    t_dim, d = do.shape
    tr = _pick(t_dim, (256, 128, 8))

    def body(do_ref, o_ref, l_ref, s_ref, out_ref):
        @pl.when(pl.program_id(0) == 0)
        def _():
            out_ref[...] = jnp.zeros_like(out_ref)

        out_ref[...] += jnp.sum(-jnp.exp(s_ref[...] - l_ref[...]) * do_ref[...] * o_ref[...], axis=0, keepdims=True)

    row = pl.BlockSpec((tr, d), lambda i: (i, 0))
    vec = pl.BlockSpec((1, d), lambda i: (0, 0))
    return pl.pallas_call(
        body,
        out_shape=jax.ShapeDtypeStruct((1, d), F32),
        grid=(t_dim // tr,),
        in_specs=[row, row, row, vec],
        out_specs=vec,
        compiler_params=_params(("arbitrary",)),
        name=name,
    )(do, o, lse, sink_lanes)


def adamw(w, g, m, v, *, name):
    rows, cols = w.shape
    tr = _pick(rows, (256, 128, 64, 32, 16, 8))

    def body(w_ref, g_ref, m_ref, v_ref, d_ref, nm_ref, nv_ref):
        gv = g_ref[...]
        nm = ADAM_B1 * m_ref[...] + (1.0 - ADAM_B1) * gv
        nv = ADAM_B2 * v_ref[...] + (1.0 - ADAM_B2) * (gv * gv)
        m_hat = nm / (1.0 - ADAM_B1 ** ADAM_STEP)
        v_hat = nv / (1.0 - ADAM_B2 ** ADAM_STEP)
        d_ref[...] = -ADAM_LR * (m_hat / (jnp.sqrt(v_hat) + ADAM_EPS) + ADAM_WD * w_ref[...])
        nm_ref[...] = nm
        nv_ref[...] = nv

    row = pl.BlockSpec((tr, cols), lambda i: (i, 0))
    return pl.pallas_call(
        body,
        out_shape=(jax.ShapeDtypeStruct((rows, cols), F32),) * 3,
        grid=(rows // tr,),
        in_specs=[row] * 4,
        out_specs=(row, row, row),
        compiler_params=_params(("parallel",)),
        name=name,
    )(w, g, m, v)


def _place():
    return lax.axis_index("x"), lax.axis_index("y"), lax.axis_index("c")


def all_gather(x, *, name):
    def body(x_ref, out_ref, send_sems, recv_sems, local_sem):
        x, y, c = _place()
        me, sibling = (x, y, c), (x, y, 1 - c)
        chips = [(1 - x, y), (x, 1 - y), (1 - x, 1 - y)]

        def slot(px, py, pc):
            return out_ref.at[4 * px + 2 * py + pc]

        def copy(k, block, to, src=None):
            return pltpu.make_async_remote_copy(
                src_ref=slot(*block) if src is None else src, dst_ref=slot(*block),
                send_sem=send_sems.at[k], recv_sem=recv_sems.at[k], device_id=to, device_id_type=MESH)

        mine = pltpu.make_async_copy(x_ref, slot(*me), local_sem)
        mine.start()
        first = [copy(0, me, sibling, src=x_ref)]
        first += [copy(1 + j, me, (*chip, c), src=x_ref) for j, chip in enumerate(chips)]
        for cp in first:
            cp.start()
        passed = [copy(4 + j, (*chip, c), sibling) for j, chip in enumerate(chips)]
        for j, chip in enumerate(chips):
            copy(1 + j, (*chip, c), me).wait_recv()
            passed[j].start()
        copy(0, sibling, me).wait_recv()
        for j, chip in enumerate(chips):
            copy(4 + j, (*chip, 1 - c), me).wait_recv()
        for cp in first + passed:
            cp.wait_send()
        mine.wait()

    return pl.pallas_call(
        body,
        out_shape=jax.ShapeDtypeStruct((N_DEV,) + x.shape, x.dtype),
        in_specs=[pl.BlockSpec(memory_space=pl.ANY)],
        out_specs=pl.BlockSpec(memory_space=pl.ANY),
        scratch_shapes=[pltpu.SemaphoreType.DMA((7,)), pltpu.SemaphoreType.DMA((7,)), pltpu.SemaphoreType.DMA(())],
        name=name,
    )(x)


def all_to_all(x, *, name):
    def body(x_ref, out_ref, send_sems, recv_sems, local_sem):
        x, y, c = _place()
        me = 4 * x + 2 * y + c
        mine = pltpu.make_async_copy(x_ref.at[me], out_ref.at[me], local_sem)
        mine.start()
        copies = []
        for k in range(1, N_DEV):
            px = 1 - x if k & 4 else x
            py = 1 - y if k & 2 else y
            pc = 1 - c if k & 1 else c
            peer = 4 * px + 2 * py + pc
            copies.append(pltpu.make_async_remote_copy(
                src_ref=x_ref.at[peer], dst_ref=out_ref.at[me], send_sem=send_sems.at[k - 1], recv_sem=recv_sems.at[k - 1],
                device_id=(px, py, pc), device_id_type=MESH))
        for cp in copies:
            cp.start()
        for cp in copies:
            cp.wait_recv()
        for cp in copies:
            cp.wait_send()
        mine.wait()

    return pl.pallas_call(
        body,
        out_shape=jax.ShapeDtypeStruct(x.shape, x.dtype),
        in_specs=[pl.BlockSpec(memory_space=pl.ANY)],
        out_specs=pl.BlockSpec(memory_space=pl.ANY),
        scratch_shapes=[pltpu.SemaphoreType.DMA((7,)), pltpu.SemaphoreType.DMA((7,)), pltpu.SemaphoreType.DMA(())],
        name=name,
    )(x)


def sum_slots(x, *, name):
    _, rows, cols = x.shape
    tr = _pick(rows, (512, 256, 128, 64, 32, 16))

    def body(x_ref, o_ref):
        acc = x_ref[0].astype(F32)
        for k in range(1, N_DEV):
            acc = acc + x_ref[k].astype(F32)
        o_ref[...] = acc

    return pl.pallas_call(
        body,
        out_shape=jax.ShapeDtypeStruct((rows, cols), F32),
        grid=(rows // tr,),
        in_specs=[pl.BlockSpec((N_DEV, tr, cols), lambda i: (0, i, 0))],
        out_specs=pl.BlockSpec((tr, cols), lambda i: (i, 0)),
        compiler_params=_params(("parallel",)),
        name=name,
    )(x)


BIG = ("w_in", "w_branch", "w_out", "w_ffn_in", "w_ffn_out")
SMALL = ("conv_b", "w_rg", "b_rg", "w_ig", "b_ig", "lru_lambda", "sinks", "ln1_g", "ln1_b", "ln2_g", "ln2_b")
N_LRU_BLOCKS = D_MODEL // HEAD_DIM
SMALL_ROWS_TILE = 512


def _block_diag(w):
    z = jnp.zeros((N_LRU_BLOCKS // 2, HEAD_DIM, HEAD_DIM), w.dtype)
    top = jnp.concatenate([w[0::2], z], axis=2)
    bot = jnp.concatenate([z, w[1::2]], axis=2)
    return jnp.concatenate([top, bot], axis=1)


def _block_diag_grad(g):
    return jnp.stack([g[:, :HEAD_DIM, :HEAD_DIM], g[:, HEAD_DIM:, HEAD_DIM:]], axis=1).reshape(N_LRU_BLOCKS, HEAD_DIM, HEAD_DIM)


def layer_fwd(x, xb, p, bsz):
    t_dim = x.shape[0]
    s_len = t_dim // bsz
    w_f, w_qs, w_qd = p["w_in_f"], p["w_in_qs"], p["w_in_qd"]
    proj_f = matmul(xb, w_f, name="proj_f")
    qs = matmul(xb, w_qs, out_dtype=BF16, name="proj_qs").reshape(bsz, s_len, W_QS)
    qd = matmul(xb, w_qd, out_dtype=BF16, name="proj_qd").reshape(bsz, s_len, W_QD)
    proj_f3 = proj_f.reshape(bsz, s_len, W_F)
    wr_bd, wi_bd = _block_diag(p["w_rg"]), _block_diag(p["w_ig"])
    y_a, h = lru_fwd(proj_f3, p["conv_w"], p["conv_b"], wr_bd, wi_bd, p["b_rg"], p["b_ig"], p["lru_lambda"], name="lru_fwd")
    y_b, lse_b, y_bb = attn_fwd(qs, dil=1, n_kv_heads=SWA_KV_HEADS, sinks=p["sinks"], emit_bf16=True, name="swa_fwd")
    os_, lses = [], []
    for dil in DILATIONS:
        o, lse = attn_fwd(qd, dil=dil, n_kv_heads=N_HEADS, sinks=None, name=f"dil{dil}_fwd")
        os_.append(o.reshape(t_dim, D_MODEL))
        lses.append(lse.reshape(t_dim, D_MODEL))
    y_c, lse_c, y_cb = dil_combine(os_, lses, name="dil_combine")
    ys = [y_a.reshape(t_dim, D_MODEL), y_bb.reshape(t_dim, D_MODEL), y_cb]
    br = [matmul(ys[n], p["w_branch"][n], name="branch") for n in range(3)]
    merged = merge_fwd(proj_f, br, name="merge_fwd")
    mix = matmul(merged, p["w_out"], name="w_out")
    x1, x1b, z1 = ln_fwd(x, mix, p["ln1_g"], p["ln1_b"], name="ln_fwd")
    h13 = matmul(x1b, p["w_ffn_in"], name="ffn_in")
    act = swiglu_fwd(h13, name="swiglu_fwd")
    ffn = matmul(act, p["w_ffn_out"], name="ffn_out")
    x2, x2b, z2 = ln_fwd(x1, ffn, p["ln2_g"], p["ln2_b"], name="ln_fwd")
    saved = dict(xb=xb, proj_f=proj_f, qs=qs, qd=qd, h=h, ys=ys, y_b=y_b, y_c=y_c, lse_b=lse_b, lse_c=lse_c, br=br, merged=merged,
                 z1=z1, x1b=x1b, h13=h13, act=act, z2=z2, wr_bd=wr_bd, wi_bd=wi_bd)
    return x2, x2b, saved


def layer_bwd(dx2, p, s, bsz):
    t_dim = dx2.shape[0]
    s_len = t_dim // bsz
    g = {}
    dz2, dz2b, g["ln2_g"], g["ln2_b"] = ln_bwd(dx2, s["z2"], p["ln2_g"], name="ln_bwd")
    dact = matmul(dz2b, p["w_ffn_out"], trans_b=True, name="d_act")
    dh13 = swiglu_bwd(dact, s["h13"], name="swiglu_bwd")
    g["w_ffn_out"] = matmul(s["act"], dz2b, trans_a=True, name="dw_ffn_out")
    g["w_ffn_in"] = matmul(s["x1b"], dh13, trans_a=True, name="dw_ffn_in")
    dx1 = matmul(dh13, p["w_ffn_in"], trans_b=True, add=dz2, add_scale=ALPHA, name="dx_ffn")
    dz1, dz1b, g["ln1_g"], g["ln1_b"] = ln_bwd(dx1, s["z1"], p["ln1_g"], name="ln_bwd")
    dmerged = matmul(dz1b, p["w_out"], trans_b=True, name="d_merged")
    g["w_out"] = matmul(s["merged"], dz1b, trans_a=True, name="dw_out")
    *dbr, dgates = merge_bwd(dmerged, s["proj_f"], s["br"], name="merge_bwd")
    dys = [matmul(dbr[n], p["w_branch"][n], trans_b=True, name="d_branch") for n in range(3)]
    g["w_branch"] = jnp.stack([matmul(s["ys"][n], dbr[n], trans_a=True, name="dw_branch") for n in range(3)])
    shape3 = (bsz, s_len, D_MODEL)
    (dlx, dlg, g["conv_w"], g["conv_b"], g["b_rg"], g["b_ig"], g["lru_lambda"], dwr, dwi) = lru_bwd(
        dys[0].reshape(shape3), s["proj_f"].reshape(bsz, s_len, W_F), s["h"], p["conv_w"], p["conv_b"], s["wr_bd"], s["wi_bd"],
        jnp.swapaxes(s["wr_bd"], 1, 2), jnp.swapaxes(s["wi_bd"], 1, 2), p["b_rg"], p["b_ig"], p["lru_lambda"], name="lru_bwd")
    g["w_rg"], g["w_ig"] = _block_diag_grad(dwr), _block_diag_grad(dwi)
    dy_b3 = dys[1].reshape(shape3)
    dqs = attn_bwd(s["qs"], s["y_b"], s["lse_b"], dy_b3, None, dil=1, n_kv_heads=SWA_KV_HEADS, name="swa_bwd")
    sink_lanes = jnp.repeat(p["sinks"], HEAD_DIM).reshape(1, D_MODEL)
    g["sinks"] = sink_grad(dys[1], s["y_b"].reshape(t_dim, D_MODEL), s["lse_b"].reshape(t_dim, D_MODEL), sink_lanes,
                           name="sink_grad").reshape(N_HEADS, HEAD_DIM).sum(axis=1)
    y_c3, dy_c3, lse_c3 = s["y_c"].reshape(shape3), dys[2].reshape(shape3), s["lse_c"].reshape(shape3)
    dqd = None
    for dil in DILATIONS:
        dqd = attn_bwd(s["qd"], y_c3, lse_c3, dy_c3, dqd, dil=dil, n_kv_heads=N_HEADS, name=f"dil{dil}_bwd")
    flat = lambda t: t.reshape(t_dim, t.shape[-1])
    dproj_f = jnp.concatenate([flat(dlx), flat(dlg), dgates], axis=1)
    dproj_qs = jnp.concatenate([flat(t) for t in dqs], axis=1).astype(BF16)
    dproj_qd = jnp.concatenate([flat(t) for t in dqd], axis=1).astype(BF16)
    g["w_in_f"] = matmul(s["xb"], dproj_f, trans_a=True, name="dw_in_f")
    g["w_in_qs"] = matmul(s["xb"], dproj_qs, trans_a=True, name="dw_in_qs")
    g["w_in_qd"] = matmul(s["xb"], dproj_qd, trans_a=True, name="dw_in_qd")
    dx = matmul(dproj_f, p["w_in_f"], trans_b=True, add=dz1, add_scale=ALPHA, name="dx_f")
    dx = matmul(dproj_qs, p["w_in_qs"], trans_b=True, add=dx, name="dx_qs")
    dx = matmul(dproj_qd, p["w_in_qd"], trans_b=True, add=dx, name="dx_qd")
    g = {k: (v.reshape(p[k].shape) if k in p else v) for k, v in g.items()}
    return dx, g


def local_step(x, target, params):
    bsz, s_len, d = x.shape
    t_dim = bsz * s_len
    xf = x.reshape(t_dim, d)
    xb = xf.astype(BF16)
    saved = []
    for l in range(DEPTH):
        xf, xb, s = layer_fwd(xf, xb, {k: v[l] for k, v in params.items()}, bsz)
        saved.append(s)
    dy, sq = loss_head(xf, target.reshape(t_dim, d), name="loss_head")
    loss = 0.5 * jnp.sum(sq) / d
    grads = [None] * DEPTH
    for l in reversed(range(DEPTH)):
        dy, grads[l] = layer_bwd(dy, {k: v[l] for k, v in params.items()}, saved[l], bsz)
    grads = {k: jnp.stack([grads[l][k] for l in range(DEPTH)]) for k in grads[0]}
    return loss, dy.reshape(bsz, s_len, d), grads


W_IN_SEGMENTS = (("w_in_f", 0, 0, 2 * D_MODEL), ("w_in_qs", 0, 2 * D_MODEL, W_QS), ("w_in_qd", 0, 2 * D_MODEL + W_QS, W_QD),
                 ("w_in_f", 2 * D_MODEL, 2 * D_MODEL + W_QS + W_QD, 3 * D_MODEL))
ROW_SHARDED = ("w_branch", "w_out", "w_ffn_out")


def _cols_of_shards(shards, lo, hi):
    width = shards[0].shape[-1]
    parts = []
    for k, sh in enumerate(shards):
        a, b = max(lo, k * width), min(hi, (k + 1) * width)
        if a < b:
            parts.append(sh[..., a - k * width:b - k * width])
    return parts[0] if len(parts) == 1 else jnp.concatenate(parts, axis=-1)


def _cols_of_w_in(pieces, lo, hi):
    parts = []
    for name, p0, l0, width in W_IN_SEGMENTS:
        a, b = max(lo, l0), min(hi, l0 + width)
        if a < b:
            parts.append(pieces[name][..., p0 + a - l0:p0 + b - l0])
    return parts[0] if len(parts) == 1 else jnp.concatenate(parts, axis=-1)


def _rows_to_full(name, t, shard_shape):
    t = t.reshape((N_DEV,) + shard_shape)
    if name == "w_branch":
        return jnp.transpose(t, (1, 2, 0, 3, 4)).reshape(shard_shape[0], 3, -1, D_MODEL)
    return jnp.transpose(t, (1, 0, 2, 3)).reshape(shard_shape[0], -1, D_MODEL)


def _full_to_rows(name, t):
    if name == "w_branch":
        t = jnp.transpose(t.reshape(DEPTH, 3, N_DEV, -1, D_MODEL), (2, 0, 1, 3, 4))
    else:
        t = jnp.transpose(t.reshape(DEPTH, N_DEV, -1, D_MODEL), (1, 0, 2, 3))
    return t.reshape(N_DEV, -1, D_MODEL)


def _pad_rows(flat, tile_rows):
    n = flat.shape[0]
    per = tile_rows * LANES
    total = -(-n // per) * per
    return jnp.pad(flat, (0, total - n)).reshape(-1, LANES)


def kernel(x, w_in, conv_w, conv_b, w_rg, b_rg, w_ig, b_ig, lru_lambda, sinks, w_branch, w_out, ln1_g, ln1_b, w_ffn_in, w_ffn_out, ln2_g, ln2_b, loss_target, m_w_in, m_conv_w, m_conv_b, m_w_rg, m_b_rg, m_w_ig, m_b_ig, m_lru_lambda, m_sinks, m_w_branch, m_w_out, m_ln1_g, m_ln1_b, m_w_ffn_in, m_w_ffn_out, m_ln2_g, m_ln2_b, v_w_in, v_conv_w, v_conv_b, v_w_rg, v_b_rg, v_w_ig, v_b_ig, v_lru_lambda, v_sinks, v_w_branch, v_w_out, v_ln1_g, v_ln1_b, v_w_ffn_in, v_w_ffn_out, v_ln2_g, v_ln2_b):
    w = dict(w_in=w_in, conv_w=conv_w, conv_b=conv_b, w_rg=w_rg, b_rg=b_rg, w_ig=w_ig, b_ig=b_ig, lru_lambda=lru_lambda, sinks=sinks,
             w_branch=w_branch, w_out=w_out, ln1_g=ln1_g, ln1_b=ln1_b, w_ffn_in=w_ffn_in, w_ffn_out=w_ffn_out, ln2_g=ln2_g, ln2_b=ln2_b)
    m = dict(w_in=m_w_in, conv_w=m_conv_w, conv_b=m_conv_b, w_rg=m_w_rg, b_rg=m_b_rg, w_ig=m_w_ig, b_ig=m_b_ig, lru_lambda=m_lru_lambda,
             sinks=m_sinks, w_branch=m_w_branch, w_out=m_w_out, ln1_g=m_ln1_g, ln1_b=m_ln1_b, w_ffn_in=m_w_ffn_in, w_ffn_out=m_w_ffn_out,
             ln2_g=m_ln2_g, ln2_b=m_ln2_b)
    v = dict(w_in=v_w_in, conv_w=v_conv_w, conv_b=v_conv_b, w_rg=v_w_rg, b_rg=v_b_rg, w_ig=v_w_ig, b_ig=v_b_ig, lru_lambda=v_lru_lambda,
             sinks=v_sinks, w_branch=v_w_branch, w_out=v_w_out, ln1_g=v_ln1_g, ln1_b=v_ln1_b, w_ffn_in=v_w_ffn_in, w_ffn_out=v_w_ffn_out,
             ln2_g=v_ln2_g, ln2_b=v_ln2_b)
    order = ["w_in", "conv_w", "conv_b", "w_rg", "b_rg", "w_ig", "b_ig", "lru_lambda", "sinks", "w_branch", "w_out", "ln1_g", "ln1_b",
             "w_ffn_in", "w_ffn_out", "ln2_g", "ln2_b"]
    me = 4 * lax.axis_index("x") + 2 * lax.axis_index("y") + lax.axis_index("c")

    params = {}
    g_in = all_gather(w_in.astype(BF16).reshape(DEPTH * D_MODEL, -1), name="gather_w_in")
    sh_in = [g_in[k].reshape(DEPTH, D_MODEL, -1) for k in range(N_DEV)]
    params["w_in_f"] = jnp.concatenate([_cols_of_shards(sh_in, 0, 2 * D_MODEL),
                                        _cols_of_shards(sh_in, 2 * D_MODEL + W_QS + W_QD, W_F + W_QS + W_QD)], axis=-1)
    params["w_in_qs"] = _cols_of_shards(sh_in, 2 * D_MODEL, 2 * D_MODEL + W_QS)
    params["w_in_qd"] = _cols_of_shards(sh_in, 2 * D_MODEL + W_QS, 2 * D_MODEL + W_QS + W_QD)
    g_fi = all_gather(w_ffn_in.astype(BF16).reshape(DEPTH * D_MODEL, -1), name="gather_w_ffn_in")
    params["w_ffn_in"] = jnp.concatenate([g_fi[k].reshape(DEPTH, D_MODEL, -1) for k in range(N_DEV)], axis=-1)
    row_counts = [w[k].size // D_MODEL for k in ROW_SHARDED]
    g_rows = all_gather(jnp.concatenate([w[k].astype(BF16).reshape(-1, D_MODEL) for k in ROW_SHARDED]), name="gather_w_rows")
    off = 0
    for k, n in zip(ROW_SHARDED, row_counts):
        params[k] = _rows_to_full(k, g_rows[:, off:off + n], w[k].shape)
        off += n
    cw = all_gather(conv_w.reshape(-1, LANES), name="gather_conv_w")
    params["conv_w"] = jnp.moveaxis(cw.reshape(N_DEV, DEPTH, CONV_WIDTH, LANES), 0, 2).reshape(DEPTH, CONV_WIDTH, D_MODEL)
    for k in SMALL:
        params[k] = w[k]

    loss_local, grad_x, grads = local_step(x, loss_target, params)
    loss = lax.psum(loss_local, ("x", "y", "c"))

    g_final = {}
    shard = w_in.shape[-1]
    slots = jnp.stack([_cols_of_w_in(grads, k * shard, (k + 1) * shard).astype(BF16) for k in range(N_DEV)])
    recv = all_to_all(slots.reshape(N_DEV, DEPTH * D_MODEL, shard), name="exchange_g_w_in")
    g_final["w_in"] = sum_slots(recv, name="sum_g_w_in").reshape(w_in.shape)
    shard = w_ffn_in.shape[-1]
    slots = jnp.stack([grads["w_ffn_in"][..., k * shard:(k + 1) * shard].astype(BF16) for k in range(N_DEV)])
    recv = all_to_all(slots.reshape(N_DEV, DEPTH * D_MODEL, shard), name="exchange_g_w_ffn_in")
    g_final["w_ffn_in"] = sum_slots(recv, name="sum_g_w_ffn_in").reshape(w_ffn_in.shape)
    slots = jnp.concatenate([_full_to_rows(k, grads[k]).astype(BF16) for k in ROW_SHARDED], axis=1)
    g_rows = sum_slots(all_to_all(slots, name="exchange_g_rows"), name="sum_g_rows")
    off = 0
    for k, n in zip(ROW_SHARDED, row_counts):
        g_final[k] = g_rows[off:off + n].reshape(w[k].shape)
        off += n

    small_names = list(SMALL) + ["conv_w"]
    small_sizes = [grads[k].size for k in small_names]
    svec = _pad_rows(jnp.concatenate([grads[k].reshape(-1) for k in small_names]), SMALL_ROWS_TILE)
    ssum = sum_slots(all_gather(svec, name="gather_small_grads"), name="sum_small_grads")
    sflat, off = ssum.reshape(-1), 0
    for k, n in zip(small_names, small_sizes):
        g_final[k] = sflat[off:off + n].reshape(grads[k].shape)
        off += n
    g_final["conv_w"] = lax.dynamic_slice_in_dim(g_final["conv_w"], me * LANES, LANES, axis=2)

    delta, new_m, new_v = {}, {}, {}
    for k in list(BIG) + ["conv_w"]:
        cols = w[k].shape[-1]
        two_d = lambda t: t.reshape(-1, cols)
        d_, m_, v_ = adamw(two_d(w[k]), two_d(g_final[k]), two_d(m[k]), two_d(v[k]), name=f"adamw_{k}")
        delta[k], new_m[k], new_v[k] = d_.reshape(w[k].shape), m_.reshape(w[k].shape), v_.reshape(w[k].shape)
    pack_small = lambda dct: _pad_rows(jnp.concatenate([dct[k].reshape(-1) for k in SMALL]), SMALL_ROWS_TILE)
    d_, m_, v_ = adamw(pack_small(w), pack_small(g_final), pack_small(m), pack_small(v), name="adamw_small")
    off = 0
    for k in SMALL:
        n = w[k].size
        for dst, src in ((delta, d_), (new_m, m_), (new_v, v_)):
            dst[k] = src.reshape(-1)[off:off + n].reshape(w[k].shape)
        off += n
    return (loss, grad_x, *[g_final[k] for k in order], *[delta[k] for k in order], *[new_m[k] for k in order], *[new_v[k] for k in order])
```

```python
import functools
import math

import jax
import jax.numpy as jnp
from jax import lax
from jax.experimental import pallas as pl
from jax.experimental.pallas import tpu as pltpu

F32 = jnp.float32
BF16 = jnp.bfloat16

N_DEV = 8
DEPTH = 4
D_MODEL = 1024
HEAD_DIM = 64
LANES = 128
N_HEADS = D_MODEL // HEAD_DIM
SWA_KV_HEADS = 4
ATT_BLOCK = 128
DILATIONS = (1, 4, 16)
CONV_WIDTH = 4
LRU_C = 8.0
FF_HIDDEN = 2816
ALPHA = (2.0 * DEPTH) ** 0.25
LN_EPS = 1e-5
NEG_INF = -1e30
W_F = 5 * D_MODEL
W_QS = D_MODEL + 2 * SWA_KV_HEADS * HEAD_DIM
W_QD = 3 * D_MODEL

ADAM_LR = 0.001
ADAM_B1 = 0.9
ADAM_B2 = 0.999
ADAM_EPS = 1e-08
ADAM_WD = 0.01
ADAM_STEP = 10

VMEM_LIMIT = 56 * 1024 * 1024
MESH = pl.DeviceIdType.MESH


def _pick(n, cands):
    for c in cands:
        if n % c == 0:
            return c
    raise ValueError(f"no tile for {n} among {cands}")


def _params(sem):
    return pltpu.CompilerParams(dimension_semantics=sem, vmem_limit_bytes=VMEM_LIMIT)


def _tile(n, cap):
    best = None
    for t in range(LANES, cap + 1, LANES):
        if n % t == 0:
            best = t
    assert best is not None, (n, cap)
    return best


def matmul(a, b, *, name, trans_a=False, trans_b=False, out_dtype=F32, add=None, add_scale=1.0):
    if trans_a:
        k_dim, m_dim = a.shape
    else:
        m_dim, k_dim = a.shape
    n_dim = b.shape[0] if trans_b else b.shape[1]
    assert (b.shape[1] if trans_b else b.shape[0]) == k_dim
    tm = _tile(m_dim, 1024)
    tn = _tile(n_dim, 1408)
    tk = _tile(k_dim, 1408)
    nk = k_dim // tk
    dims = (((0 if trans_a else 1,), (1 if trans_b else 0,)), ((), ()))

    def body(*refs):
        if add is None:
            a_ref, b_ref, o_ref, acc_ref = refs
            add_ref = None
        else:
            a_ref, b_ref, add_ref, o_ref, acc_ref = refs
        k = pl.program_id(2)
        part = lax.dot_general(a_ref[...].astype(BF16), b_ref[...].astype(BF16), dims, preferred_element_type=F32)

        def finish(r):
            if add_ref is not None:
                r = r + add_scale * add_ref[...].astype(F32)
            o_ref[...] = r.astype(out_dtype)

        if nk == 1:
            finish(part)
        else:
            @pl.when(k == 0)
            def _():
                acc_ref[...] = part

            @pl.when((k > 0) & (k < nk - 1))
            def _():
                acc_ref[...] += part

            @pl.when(k == nk - 1)
            def _():
                finish(acc_ref[...] + part)

    a_spec = pl.BlockSpec((tk, tm), lambda i, j, k: (k, i)) if trans_a else pl.BlockSpec((tm, tk), lambda i, j, k: (i, k))
    b_spec = pl.BlockSpec((tn, tk), lambda i, j, k: (j, k)) if trans_b else pl.BlockSpec((tk, tn), lambda i, j, k: (k, j))
    in_specs = [a_spec, b_spec]
    args = [a, b]
    if add is not None:
        in_specs.append(pl.BlockSpec((tm, tn), lambda i, j, k: (i, j)))
        args.append(add)
    return pl.pallas_call(
        body,
        out_shape=jax.ShapeDtypeStruct((m_dim, n_dim), out_dtype),
        grid=(m_dim // tm, n_dim // tn, nk),
        in_specs=in_specs,
        out_specs=pl.BlockSpec((tm, tn), lambda i, j, k: (i, j)),
        scratch_shapes=[pltpu.VMEM((tm, tn) if nk > 1 else (8, LANES), F32)],
        compiler_params=_params(("parallel", "parallel", "arbitrary")),
        name=name,
    )(*args)


def ln_fwd(x, r, g, b, *, name):
    t_dim, d = x.shape
    tr = _pick(t_dim, (256, 128, 8))

    def body(x_ref, r_ref, g_ref, b_ref, y_ref, yb_ref, z_ref):
        z = ALPHA * x_ref[...] + r_ref[...]
        mu = jnp.mean(z, axis=-1, keepdims=True)
        zc = z - mu
        var = jnp.mean(zc * zc, axis=-1, keepdims=True)
        y = zc * lax.rsqrt(var + LN_EPS) * g_ref[...] + b_ref[...]
        y_ref[...] = y
        yb_ref[...] = y.astype(BF16)
        z_ref[...] = z

    row = pl.BlockSpec((tr, d), lambda i: (i, 0))
    vec = pl.BlockSpec((1, d), lambda i: (0, 0))
    return pl.pallas_call(
        body,
        out_shape=(jax.ShapeDtypeStruct((t_dim, d), F32), jax.ShapeDtypeStruct((t_dim, d), BF16), jax.ShapeDtypeStruct((t_dim, d), F32)),
        grid=(t_dim // tr,),
        in_specs=[row, row, vec, vec],
        out_specs=(row, row, row),
        compiler_params=_params(("parallel",)),
        name=name,
    )(x, r, g.reshape(1, d), b.reshape(1, d))


def ln_bwd(dy, z, g, *, name):
    t_dim, d = dy.shape
    tr = _pick(t_dim, (256, 128, 8))

    def body(dy_ref, z_ref, g_ref, dz_ref, dzb_ref, dg_ref, db_ref):
        @pl.when(pl.program_id(0) == 0)
        def _():
            dg_ref[...] = jnp.zeros_like(dg_ref)
            db_ref[...] = jnp.zeros_like(db_ref)

        z = z_ref[...]
        dyv = dy_ref[...]
        mu = jnp.mean(z, axis=-1, keepdims=True)
        zc = z - mu
        var = jnp.mean(zc * zc, axis=-1, keepdims=True)
        rstd = lax.rsqrt(var + LN_EPS)
        xhat = zc * rstd
        dxhat = dyv * g_ref[...]
        m1 = jnp.mean(dxhat, axis=-1, keepdims=True)
        m2 = jnp.mean(dxhat * xhat, axis=-1, keepdims=True)
        dz = rstd * (dxhat - m1 - xhat * m2)
        dz_ref[...] = dz
        dzb_ref[...] = dz.astype(BF16)
        dg_ref[...] += jnp.sum(dyv * xhat, axis=0, keepdims=True)
        db_ref[...] += jnp.sum(dyv, axis=0, keepdims=True)

    row = pl.BlockSpec((tr, d), lambda i: (i, 0))
    vec = pl.BlockSpec((1, d), lambda i: (0, 0))
    return pl.pallas_call(
        body,
        out_shape=(jax.ShapeDtypeStruct((t_dim, d), F32), jax.ShapeDtypeStruct((t_dim, d), BF16),
                   jax.ShapeDtypeStruct((1, d), F32), jax.ShapeDtypeStruct((1, d), F32)),
        grid=(t_dim // tr,),
        in_specs=[row, row, vec],
        out_specs=(row, row, vec, vec),
        compiler_params=_params(("arbitrary",)),
        name=name,
    )(dy, z, g.reshape(1, d))


def loss_head(y, target, *, name):
    t_dim, d = y.shape
    tr = _pick(t_dim, (256, 128, 8))

    def body(y_ref, t_ref, dy_ref, sq_ref):
        @pl.when(pl.program_id(0) == 0)
        def _():
            sq_ref[...] = jnp.zeros_like(sq_ref)

        diff = y_ref[...] - t_ref[...]
        dy_ref[...] = diff / d
        sq_ref[...] += jnp.sum(diff * diff, axis=0, keepdims=True)

    row = pl.BlockSpec((tr, d), lambda i: (i, 0))
    vec = pl.BlockSpec((1, d), lambda i: (0, 0))
    return pl.pallas_call(
        body,
        out_shape=(jax.ShapeDtypeStruct((t_dim, d), F32), jax.ShapeDtypeStruct((1, d), F32)),
        grid=(t_dim // tr,),
        in_specs=[row, row],
        out_specs=(row, vec),
        compiler_params=_params(("arbitrary",)),
        name=name,
    )(y, target)


def _sigmoid(x):
    return 1.0 / (1.0 + jnp.exp(-x))


def swiglu_fwd(h13, *, name):
    t_dim = h13.shape[0]
    f = h13.shape[1] // 2
    tr = _pick(t_dim, (256, 128, 8))

    def body(h1_ref, h3_ref, act_ref):
        h1 = h1_ref[...]
        act_ref[...] = (h1 * _sigmoid(h1) * h3_ref[...]).astype(BF16)

    return pl.pallas_call(
        body,
        out_shape=jax.ShapeDtypeStruct((t_dim, f), BF16),
        grid=(t_dim // tr,),
        in_specs=[pl.BlockSpec((tr, f), lambda i: (i, 0)), pl.BlockSpec((tr, f), lambda i: (i, 1))],
        out_specs=pl.BlockSpec((tr, f), lambda i: (i, 0)),
        compiler_params=_params(("parallel",)),
        name=name,
    )(h13, h13)


def swiglu_bwd(dact, h13, *, name):
    t_dim = h13.shape[0]
    f = h13.shape[1] // 2
    tr = _pick(t_dim, (256, 128, 8))

    def body(da_ref, h1_ref, h3_ref, dh_ref):
        h1 = h1_ref[...]
        da = da_ref[...]
        sg = _sigmoid(h1)
        dh_ref[:, :f] = (da * h3_ref[...] * sg * (1.0 + h1 * (1.0 - sg))).astype(BF16)
        dh_ref[:, f:] = (da * h1 * sg).astype(BF16)

    return pl.pallas_call(
        body,
        out_shape=jax.ShapeDtypeStruct((t_dim, 2 * f), BF16),
        grid=(t_dim // tr,),
        in_specs=[pl.BlockSpec((tr, f), lambda i: (i, 0)), pl.BlockSpec((tr, f), lambda i: (i, 0)),
                  pl.BlockSpec((tr, f), lambda i: (i, 1))],
        out_specs=pl.BlockSpec((tr, 2 * f), lambda i: (i, 0)),
        compiler_params=_params(("parallel",)),
        name=name,
    )(dact, h13, h13)


def merge_fwd(proj_f, br, *, name):
    t_dim, d = br[0].shape
    tr = _pick(t_dim, (256, 128, 8))

    def body(g0, g1, g2, b0, b1, b2, o_ref):
        o_ref[...] = (_sigmoid(g0[...]) * b0[...] + _sigmoid(g1[...]) * b1[...] + _sigmoid(g2[...]) * b2[...]).astype(BF16)

    row = pl.BlockSpec((tr, d), lambda i: (i, 0))
    gate = [pl.BlockSpec((tr, d), functools.partial(lambda n, i: (i, 2 + n), n)) for n in range(3)]
    return pl.pallas_call(
        body,
        out_shape=jax.ShapeDtypeStruct((t_dim, d), BF16),
        grid=(t_dim // tr,),
        in_specs=gate + [row, row, row],
        out_specs=row,
        compiler_params=_params(("parallel",)),
        name=name,
    )(proj_f, proj_f, proj_f, *br)


def merge_bwd(dmerged, proj_f, br, *, name):
    t_dim, d = dmerged.shape
    tr = _pick(t_dim, (256, 128, 8))

    def body(dm_ref, g0, g1, g2, b0, b1, b2, d0, d1, d2, dg_ref):
        dm = dm_ref[...]
        for n, (g, b, o) in enumerate(((g0, b0, d0), (g1, b1, d1), (g2, b2, d2))):
            sg = _sigmoid(g[...])
            o[...] = (dm * sg).astype(BF16)
            dg_ref[:, n * d:(n + 1) * d] = (dm * b[...] * sg * (1.0 - sg)).astype(BF16)

    row = pl.BlockSpec((tr, d), lambda i: (i, 0))
    gate = [pl.BlockSpec((tr, d), functools.partial(lambda n, i: (i, 2 + n), n)) for n in range(3)]
    return pl.pallas_call(
        body,
        out_shape=(jax.ShapeDtypeStruct((t_dim, d), BF16),) * 3 + (jax.ShapeDtypeStruct((t_dim, 3 * d), BF16),),
        grid=(t_dim // tr,),
        in_specs=[row] + gate + [row, row, row],
        out_specs=(row, row, row, pl.BlockSpec((tr, 3 * d), lambda i: (i, 0))),
        compiler_params=_params(("parallel",)),
        name=name,
    )(dmerged, proj_f, proj_f, proj_f, *br)


GELU_C = math.sqrt(2.0 / math.pi)
PAD = 8


def _gelu(x):
    return 0.5 * x * (1.0 + jnp.tanh(GELU_C * (x + 0.044715 * x * x * x)))


def _gelu_grad(x):
    t = jnp.tanh(GELU_C * (x + 0.044715 * x * x * x))
    return 0.5 * (1.0 + t) + 0.5 * x * (1.0 - t * t) * GELU_C * (1.0 + 3.0 * 0.044715 * x * x)


def _neg_expm1(x):
    series = -x * (1.0 + x * (0.5 + x * (1.0 / 6.0 + x * (1.0 / 24.0 + x * (1.0 / 120.0)))))
    return jnp.where(x > -0.1, series, 1.0 - jnp.exp(x))


def _lru_gates(xv, cw_ref, cb_ref, wr_ref, wi_ref, br_ref, bi_ref, lam_ref, pad_ref, s_len):
    pad_ref[pl.ds(0, PAD), :] = jnp.zeros((PAD, LANES), F32)
    pad_ref[pl.ds(PAD, s_len), :] = xv
    xc = cb_ref[...] + jnp.zeros((s_len, LANES), F32)
    for j in range(CONV_WIDTH):
        xc = xc + pad_ref[pl.ds(PAD - (CONV_WIDTH - 1) + j, s_len), :] * cw_ref[pl.ds(j, 1), :]
    xcb = xc.astype(BF16)
    r = _sigmoid(jnp.dot(xcb, wr_ref[0].astype(BF16), preferred_element_type=F32) + br_ref[...])
    i = _sigmoid(jnp.dot(xcb, wi_ref[0].astype(BF16), preferred_element_type=F32) + bi_ref[...])
    nl = -lam_ref[...]
    sp = jnp.maximum(nl, 0.0) + jnp.log(1.0 + jnp.exp(-jnp.abs(nl)))
    log_a = -LRU_C * r * sp
    a = jnp.exp(log_a)
    mult = jnp.sqrt(_neg_expm1(2.0 * log_a))
    return xc, r, i, sp, a, mult


def _tile_scan(a, b, row, reverse):
    for s in (1, 2, 4):
        if reverse:
            a_sh = pltpu.roll(a, 8 - s, 0)
            b_sh = pltpu.roll(b, 8 - s, 0)
            m = row + s <= 7
        else:
            a_sh = pltpu.roll(a, s, 0)
            b_sh = pltpu.roll(b, s, 0)
            m = row >= s
        b = jnp.where(m, a * b_sh + b, b)
        a = jnp.where(m, a * a_sh, a)
    return a, b


def lru_fwd(proj_f, conv_w, conv_b, wr_bd, wi_bd, b_rg, b_ig, lam, *, name):
    bsz, s_len, _ = proj_f.shape
    d = D_MODEL
    ncb = d // LANES
    n_tiles = s_len // 8

    def body(x_ref, g_ref, cw_ref, cb_ref, wr_ref, wi_ref, br_ref, bi_ref, lam_ref, y_ref, h_ref, pad_ref, a_s, b_s):
        xc, r, i, sp, a, mult = _lru_gates(x_ref[0], cw_ref, cb_ref, wr_ref, wi_ref, br_ref, bi_ref, lam_ref, pad_ref, s_len)
        a_s[...] = a
        b_s[...] = mult * (i * xc)
        row = lax.broadcasted_iota(jnp.int32, (8, LANES), 0)

        def tile(t, carry):
            i0 = pl.multiple_of(t * 8, 8)
            ac, hl = _tile_scan(a_s[pl.ds(i0, 8), :], b_s[pl.ds(i0, 8), :], row, False)
            h = hl + ac * carry
            h_ref[0, pl.ds(i0, 8), :] = h
            return jnp.broadcast_to(h[7:8, :], (8, LANES))

        lax.fori_loop(0, n_tiles, tile, jnp.zeros((8, LANES), F32))
        y_ref[0] = (h_ref[0] * _gelu(g_ref[0])).astype(BF16)

    slab = lambda off: pl.BlockSpec((1, s_len, LANES), functools.partial(lambda o, c, b: (b, 0, o + c), off))
    vec = pl.BlockSpec((1, LANES), lambda c, b: (0, c))
    mat = pl.BlockSpec((1, LANES, LANES), lambda c, b: (c, 0, 0))
    out = pl.BlockSpec((1, s_len, LANES), lambda c, b: (b, 0, c))
    return pl.pallas_call(
        body,
        out_shape=(jax.ShapeDtypeStruct((bsz, s_len, d), BF16), jax.ShapeDtypeStruct((bsz, s_len, d), F32)),
        grid=(ncb, bsz),
        in_specs=[slab(0), slab(ncb), pl.BlockSpec((CONV_WIDTH, LANES), lambda c, b: (0, c)), vec, mat, mat, vec, vec, vec],
        out_specs=(out, out),
        scratch_shapes=[pltpu.VMEM((s_len + 2 * PAD, LANES), F32), pltpu.VMEM((s_len, LANES), F32), pltpu.VMEM((s_len, LANES), F32)],
        compiler_params=_params(("parallel", "parallel")),
        name=name,
    )(proj_f, proj_f, conv_w, conv_b.reshape(1, d), wr_bd, wi_bd, b_rg.reshape(1, d), b_ig.reshape(1, d), lam.reshape(1, d))


def lru_bwd(dy, proj_f, h, conv_w, conv_b, wr_bd, wi_bd, wr_bd_t, wi_bd_t, b_rg, b_ig, lam, *, name):
    bsz, s_len, _ = proj_f.shape
    d = D_MODEL
    ncb = d // LANES
    n_tiles = s_len // 8

    def body(dy_ref, x_ref, g_ref, h_ref, cw_ref, cb_ref, wr_ref, wi_ref, wrt_ref, wit_ref, br_ref, bi_ref, lam_ref,
             dx_ref, dg_ref, dcw_ref, dcb_ref, dbr_ref, dbi_ref, dlam_ref, dwr_ref, dwi_ref, pad_ref, a_s, b_s, l_s):
        @pl.when(pl.program_id(1) == 0)
        def _():
            for ref in (dcw_ref, dcb_ref, dbr_ref, dbi_ref, dlam_ref, dwr_ref, dwi_ref):
                ref[...] = jnp.zeros_like(ref)

        xc, r, i, sp, a, mult = _lru_gates(x_ref[0], cw_ref, cb_ref, wr_ref, wi_ref, br_ref, bi_ref, lam_ref, pad_ref, s_len)
        gate = g_ref[0]
        hv = h_ref[0]
        dyv = dy_ref[0]
        dg_ref[0] = (dyv * hv * _gelu_grad(gate)).astype(BF16)
        b_s[...] = dyv * _gelu(gate)
        l_s[pl.ds(0, s_len), :] = a
        l_s[pl.ds(s_len, PAD), :] = jnp.zeros((PAD, LANES), F32)
        a_s[...] = l_s[pl.ds(1, s_len), :]
        row = lax.broadcasted_iota(jnp.int32, (8, LANES), 0)

        def tile(t, carry):
            i0 = pl.multiple_of((n_tiles - 1 - t) * 8, 8)
            ac, ll = _tile_scan(a_s[pl.ds(i0, 8), :], b_s[pl.ds(i0, 8), :], row, True)
            lmb = ll + ac * carry
            b_s[pl.ds(i0, 8), :] = lmb
            return jnp.broadcast_to(lmb[0:1, :], (8, LANES))

        lax.fori_loop(0, n_tiles, tile, jnp.zeros((8, LANES), F32))
        lmb = b_s[...]
        l_s[pl.ds(0, PAD), :] = jnp.zeros((PAD, LANES), F32)
        l_s[pl.ds(PAD, s_len), :] = hv
        h_prev = l_s[pl.ds(PAD - 1, s_len), :]
        da = lmb * h_prev
        dmult = lmb * (i * xc)
        di = lmb * mult * xc
        dxc = lmb * mult * i
        dlog_a = da * a - dmult * a * a / mult
        dr = -LRU_C * sp * dlog_a
        dsp = jnp.sum(-LRU_C * r * dlog_a, axis=0, keepdims=True)
        dlam_ref[...] += dsp * (-_sigmoid(-lam_ref[...]))
        dpr = dr * r * (1.0 - r)
        dpi = di * i * (1.0 - i)
        dprb = dpr.astype(BF16)
        dpib = dpi.astype(BF16)
        xcb = xc.astype(BF16)
        dbr_ref[...] += jnp.sum(dpr, axis=0, keepdims=True)
        dbi_ref[...] += jnp.sum(dpi, axis=0, keepdims=True)
        tn = (((0,), (0,)), ((), ()))
        dwr_ref[0] += lax.dot_general(xcb, dprb, tn, preferred_element_type=F32)
        dwi_ref[0] += lax.dot_general(xcb, dpib, tn, preferred_element_type=F32)
        dxc = (dxc + jnp.dot(dprb, wrt_ref[0].astype(BF16), preferred_element_type=F32)
               + jnp.dot(dpib, wit_ref[0].astype(BF16), preferred_element_type=F32))
        dcb_ref[...] += jnp.sum(dxc, axis=0, keepdims=True)
        for j in range(CONV_WIDTH):
            dcw_ref[pl.ds(j, 1), :] += jnp.sum(dxc * pad_ref[pl.ds(PAD - (CONV_WIDTH - 1) + j, s_len), :], axis=0, keepdims=True)
        l_s[pl.ds(0, s_len), :] = dxc
        l_s[pl.ds(s_len, PAD), :] = jnp.zeros((PAD, LANES), F32)
        dx = jnp.zeros((s_len, LANES), F32)
        for j in range(CONV_WIDTH):
            dx = dx + l_s[pl.ds(CONV_WIDTH - 1 - j, s_len), :] * cw_ref[pl.ds(j, 1), :]
        dx_ref[0] = dx.astype(BF16)

    slab = lambda off: pl.BlockSpec((1, s_len, LANES), functools.partial(lambda o, c, b: (b, 0, o + c), off))
    vec = pl.BlockSpec((1, LANES), lambda c, b: (0, c))
    mat = pl.BlockSpec((1, LANES, LANES), lambda c, b: (c, 0, 0))
    cw = pl.BlockSpec((CONV_WIDTH, LANES), lambda c, b: (0, c))
    out = pl.BlockSpec((1, s_len, LANES), lambda c, b: (b, 0, c))
    vshape = jax.ShapeDtypeStruct((1, d), F32)
    mshape = jax.ShapeDtypeStruct((ncb, LANES, LANES), F32)
    return pl.pallas_call(
        body,
        out_shape=(jax.ShapeDtypeStruct((bsz, s_len, d), BF16),) * 2
        + (jax.ShapeDtypeStruct((CONV_WIDTH, d), F32), vshape, vshape, vshape, vshape, mshape, mshape),
        grid=(ncb, bsz),
        in_specs=[out, slab(0), slab(ncb), out, cw, vec, mat, mat, mat, mat, vec, vec, vec],
        out_specs=(out, out, cw, vec, vec, vec, vec, mat, mat),
        scratch_shapes=[pltpu.VMEM((s_len + 2 * PAD, LANES), F32), pltpu.VMEM((s_len, LANES), F32), pltpu.VMEM((s_len, LANES), F32),
                        pltpu.VMEM((s_len + 2 * PAD, LANES), F32)],
        compiler_params=_params(("parallel", "arbitrary")),
        name=name,
    )(dy, proj_f, proj_f, h, conv_w, conv_b.reshape(1, d), wr_bd, wi_bd, wr_bd_t, wi_bd_t,
      b_rg.reshape(1, d), b_ig.reshape(1, d), lam.reshape(1, d))


def _kv_place(head, n_kv_heads):
    kv = head // (N_HEADS // n_kv_heads)
    return kv // 2, kv % 2


def _band_mask(n, single):
    if single:
        qi = lax.broadcasted_iota(jnp.int32, (ATT_BLOCK, ATT_BLOCK), 0)
        return qi >= lax.broadcasted_iota(jnp.int32, (ATT_BLOCK, ATT_BLOCK), 1)
    qi = lax.broadcasted_iota(jnp.int32, (ATT_BLOCK, 2 * ATT_BLOCK), 0)
    kj = lax.broadcasted_iota(jnp.int32, (ATT_BLOCK, 2 * ATT_BLOCK), 1)
    rel = qi + ATT_BLOCK - kj
    return (rel >= 0) & (rel <= ATT_BLOCK) & ((n > 0) | (kj >= ATT_BLOCK))


def _half_masks(dtype):
    lane = lax.broadcasted_iota(jnp.int32, (1, LANES), 1)
    return [(lane < HEAD_DIM).astype(dtype), (lane >= HEAD_DIM).astype(dtype)]


NT = (((1,), (1,)), ((), ()))
TN = (((0,), (0,)), ((), ()))


def _qkv_specs(dil, q_blk, k_blk, v_blk, ckv, clamp):
    qw = D_MODEL // LANES * LANES
    return [
        pl.BlockSpec((1, ATT_BLOCK, qw), lambda b, j, n: (b, clamp(n), j * (q_blk[1]) + q_blk[0])),
        pl.BlockSpec((1, ATT_BLOCK, ckv), lambda b, j, n: (b, jnp.maximum(clamp(n) - 1, 0), j * k_blk[1] + k_blk[0])),
        pl.BlockSpec((1, ATT_BLOCK, ckv), lambda b, j, n: (b, clamp(n), j * k_blk[1] + k_blk[0])),
        pl.BlockSpec((1, ATT_BLOCK, ckv), lambda b, j, n: (b, jnp.maximum(clamp(n) - 1, 0), j * v_blk[1] + v_blk[0])),
        pl.BlockSpec((1, ATT_BLOCK, ckv), lambda b, j, n: (b, clamp(n), j * v_blk[1] + v_blk[0])),
    ]


def attn_fwd(qkv, *, dil, n_kv_heads, sinks, name, emit_bf16=False):
    bsz, s_len, width = qkv.shape
    ckv = n_kv_heads * HEAD_DIM
    l_sub = s_len // dil
    nb = l_sub // ATT_BLOCK
    view = qkv.reshape(bsz, l_sub, dil * width)
    scale = HEAD_DIM ** -0.5
    q_blk = (0, width // D_MODEL)
    k_blk = (D_MODEL // ckv, width // ckv)
    v_blk = (D_MODEL // ckv + 1, width // ckv)
    assert (dil == 1 or width % D_MODEL == 0) and width % ckv == 0 and D_MODEL % ckv == 0

    single = nb == 1

    def body(*refs):
        refs = list(refs)
        sink_ref = refs.pop(0) if sinks is not None else None
        ob_ref = refs.pop() if emit_bf16 else None
        q_ref, kp_ref, kc_ref, vp_ref, vc_ref, o_ref, lse_ref = refs
        n = pl.program_id(2)
        mask = _band_mask(n, single)
        hm = _half_masks(BF16)
        hmf = _half_masks(F32)
        kk = kc_ref[0] if single else jnp.concatenate([kp_ref[0], kc_ref[0]], axis=0)
        vv = vc_ref[0] if single else jnp.concatenate([vp_ref[0], vc_ref[0]], axis=0)
        for hp in range(N_HEADS // 2):
            q2 = q_ref[0, :, hp * LANES:(hp + 1) * LANES]
            o2 = jnp.zeros((ATT_BLOCK, LANES), F32)
            l2 = jnp.zeros((ATT_BLOCK, LANES), F32)
            for a in range(2):
                kb, kh = _kv_place(2 * hp + a, n_kv_heads)
                k2 = kk[:, kb * LANES:(kb + 1) * LANES]
                v2 = vv[:, kb * LANES:(kb + 1) * LANES]
                if kh != a:
                    k2 = pltpu.roll(k2, HEAD_DIM, 1)
                    v2 = pltpu.roll(v2, HEAD_DIM, 1)
                s = lax.dot_general(q2 * hm[a], k2, NT, preferred_element_type=F32) * scale
                s = jnp.where(mask, s, NEG_INF)
                m = jnp.max(s, axis=-1, keepdims=True)
                if sink_ref is not None:
                    sk = sink_ref[2 * hp + a]
                    m = jnp.maximum(m, sk)
                p = jnp.exp(s - m)
                den = jnp.sum(p, axis=-1, keepdims=True)
                if sink_ref is not None:
                    den = den + jnp.exp(sk - m)
                o2 = o2 + jnp.dot(p.astype(BF16), v2 * hm[a], preferred_element_type=F32) / den
                l2 = l2 + (m + jnp.log(den)) * hmf[a]
            o_ref[0, :, hp * LANES:(hp + 1) * LANES] = o2
            lse_ref[0, :, hp * LANES:(hp + 1) * LANES] = l2
            if ob_ref is not None:
                ob_ref[0, :, hp * LANES:(hp + 1) * LANES] = o2.astype(BF16)

    in_specs = _qkv_specs(dil, q_blk, k_blk, v_blk, ckv, lambda n: n)
    args = [view] * 5
    if sinks is not None:
        in_specs = [pl.BlockSpec(memory_space=pltpu.SMEM)] + in_specs
        args = [sinks] + args
    out = pl.BlockSpec((1, ATT_BLOCK, D_MODEL), lambda b, j, n: (b, n, j))
    res = pl.pallas_call(
        body,
        out_shape=(jax.ShapeDtypeStruct((bsz, l_sub, dil * D_MODEL), F32),) * 2
        + ((jax.ShapeDtypeStruct((bsz, l_sub, dil * D_MODEL), BF16),) if emit_bf16 else ()),
        grid=(bsz, dil, nb),
        in_specs=in_specs,
        out_specs=(out,) * (3 if emit_bf16 else 2),
        compiler_params=_params(("parallel", "parallel", "arbitrary")),
        name=name,
    )(*args)
    return tuple(t.reshape(bsz, s_len, D_MODEL) for t in res)


def attn_bwd(qkv, o, lse, do, acc, *, dil, n_kv_heads, name):
    bsz, s_len, width = qkv.shape
    ckv = n_kv_heads * HEAD_DIM
    l_sub = s_len // dil
    nb = l_sub // ATT_BLOCK
    view = qkv.reshape(bsz, l_sub, dil * width)
    scale = HEAD_DIM ** -0.5
    q_blk = (0, width // D_MODEL)
    k_blk = (D_MODEL // ckv, width // ckv)
    v_blk = (D_MODEL // ckv + 1, width // ckv)
    single = nb == 1

    def body(*refs):
        if acc is None:
            q_ref, kp_ref, kc_ref, vp_ref, vc_ref, o_ref, lse_ref, do_ref, dq_ref, dk_ref, dv_ref, dkk, dvv, ck, cv = refs
            aq_ref = ak_ref = av_ref = None
        else:
            (q_ref, kp_ref, kc_ref, vp_ref, vc_ref, o_ref, lse_ref, do_ref, aq_ref, ak_ref, av_ref,
             dq_ref, dk_ref, dv_ref, dkk, dvv, ck, cv) = refs
        n = pl.program_id(2)

        @pl.when(n < nb)
        def _():
            mask = _band_mask(n, single)
            hm = _half_masks(BF16)
            hmf = _half_masks(F32)
            kk = kc_ref[0] if single else jnp.concatenate([kp_ref[0], kc_ref[0]], axis=0)
            vv = vc_ref[0] if single else jnp.concatenate([vp_ref[0], vc_ref[0]], axis=0)
            krows = pl.ds(ATT_BLOCK, ATT_BLOCK) if single else pl.ds(0, 2 * ATT_BLOCK)
            dkk[...] = jnp.zeros_like(dkk)
            dvv[...] = jnp.zeros_like(dvv)
            for hp in range(N_HEADS // 2):
                cols = slice(hp * LANES, (hp + 1) * LANES)
                q2 = q_ref[0, :, cols]
                do2f = do_ref[0, :, cols]
                do2 = do2f.astype(BF16)
                dd2 = do2f * o_ref[0, :, cols]
                l2 = lse_ref[0, :, cols]
                dq2 = jnp.zeros((ATT_BLOCK, LANES), F32)
                for a in range(2):
                    kb, kh = _kv_place(2 * hp + a, n_kv_heads)
                    kcols = slice(kb * LANES, (kb + 1) * LANES)
                    k2 = kk[:, kcols]
                    v2 = vv[:, kcols]
                    if kh != a:
                        k2 = pltpu.roll(k2, HEAD_DIM, 1)
                        v2 = pltpu.roll(v2, HEAD_DIM, 1)
                    qm = q2 * hm[a]
                    dom = do2 * hm[a]
                    dsum = jnp.sum(dd2 * hmf[a], axis=-1, keepdims=True)
                    lse_h = jnp.max(jnp.where(hmf[a] > 0.5, l2, NEG_INF), axis=-1, keepdims=True)
                    s = lax.dot_general(qm, k2, NT, preferred_element_type=F32) * scale
                    s = jnp.where(mask, s, NEG_INF)
                    p = jnp.exp(s - lse_h)
                    dp = lax.dot_general(dom, v2, NT, preferred_element_type=F32)
                    ds = (p * (dp - dsum) * scale).astype(BF16)
                    dq2 = dq2 + jnp.dot(ds, k2 * hm[a], preferred_element_type=F32)
                    dk_c = lax.dot_general(ds, qm, TN, preferred_element_type=F32)
                    dv_c = lax.dot_general(p.astype(BF16), dom, TN, preferred_element_type=F32)
                    if kh != a:
                        dk_c = pltpu.roll(dk_c, HEAD_DIM, 1)
                        dv_c = pltpu.roll(dv_c, HEAD_DIM, 1)
                    dkk[krows, kcols] += dk_c
                    dvv[krows, kcols] += dv_c
                if aq_ref is not None:
                    dq2 = dq2 + aq_ref[0, :, cols]
                dq_ref[0, :, cols] = dq2

        @pl.when((n >= 1) & (n < nb))
        def _():
            dk_ref[0] = ck[...] + dkk[pl.ds(0, ATT_BLOCK), :] + (0.0 if ak_ref is None else ak_ref[0])
            dv_ref[0] = cv[...] + dvv[pl.ds(0, ATT_BLOCK), :] + (0.0 if av_ref is None else av_ref[0])

        @pl.when(n == nb)
        def _():
            dk_ref[0] = ck[...] + (0.0 if ak_ref is None else ak_ref[0])
            dv_ref[0] = cv[...] + (0.0 if av_ref is None else av_ref[0])

        @pl.when(n < nb)
        def _():
            ck[...] = dkk[pl.ds(ATT_BLOCK, ATT_BLOCK), :]
            cv[...] = dvv[pl.ds(ATT_BLOCK, ATT_BLOCK), :]

    clamp = lambda n: jnp.minimum(n, nb - 1)
    prev = lambda n: jnp.maximum(n - 1, 0)
    row = pl.BlockSpec((1, ATT_BLOCK, D_MODEL), lambda b, j, n: (b, clamp(n), j))
    kv_out = pl.BlockSpec((1, ATT_BLOCK, ckv), lambda b, j, n: (b, prev(n), j))
    in_specs = _qkv_specs(dil, q_blk, k_blk, v_blk, ckv, clamp) + [row, row, row]
    rs = lambda t: t.reshape(bsz, l_sub, dil * t.shape[-1])
    args = [view] * 5 + [rs(o), rs(lse), rs(do)]
    if acc is not None:
        in_specs += [row, kv_out, kv_out]
        args += [rs(t) for t in acc]
    dq, dk, dv = pl.pallas_call(
        body,
        out_shape=(jax.ShapeDtypeStruct((bsz, l_sub, dil * D_MODEL), F32),
                   jax.ShapeDtypeStruct((bsz, l_sub, dil * ckv), F32), jax.ShapeDtypeStruct((bsz, l_sub, dil * ckv), F32)),
        grid=(bsz, dil, nb + 1),
        in_specs=in_specs,
        out_specs=(row, kv_out, kv_out),
        scratch_shapes=[pltpu.VMEM((2 * ATT_BLOCK, ckv), F32), pltpu.VMEM((2 * ATT_BLOCK, ckv), F32),
                        pltpu.VMEM((ATT_BLOCK, ckv), F32), pltpu.VMEM((ATT_BLOCK, ckv), F32)],
        compiler_params=_params(("parallel", "parallel", "arbitrary")),
        name=name,
    )(*args)
    return dq.reshape(bsz, s_len, D_MODEL), dk.reshape(bsz, s_len, ckv), dv.reshape(bsz, s_len, ckv)


def dil_combine(os_, lses, *, name):
    t_dim, d = os_[0].shape
    tr = _pick(t_dim, (256, 128, 8))

    def body(o0, o1, o2, l0, l1, l2, y_ref, lt_ref, yb_ref):
        la, lb, lc = l0[...], l1[...], l2[...]
        m = jnp.maximum(jnp.maximum(la, lb), lc)
        ea, eb, ec = jnp.exp(la - m), jnp.exp(lb - m), jnp.exp(lc - m)
        tot = ea + eb + ec
        y = (ea / tot) * o0[...] + (eb / tot) * o1[...] + (ec / tot) * o2[...]
        y_ref[...] = y
        yb_ref[...] = y.astype(BF16)
        lt_ref[...] = m + jnp.log(tot)

    row = pl.BlockSpec((tr, d), lambda i: (i, 0))
    return pl.pallas_call(
        body,
        out_shape=(jax.ShapeDtypeStruct((t_dim, d), F32),) * 2 + (jax.ShapeDtypeStruct((t_dim, d), BF16),),
        grid=(t_dim // tr,),
        in_specs=[row] * 6,
        out_specs=(row, row, row),
        compiler_params=_params(("parallel",)),
        name=name,
    )(*os_, *lses)


ATT_SCALE = HEAD_DIM ** -0.5


def _band_mask(n, single):
    nk = ATT_BLOCK if single else 2 * ATT_BLOCK
    qi = lax.broadcasted_iota(jnp.int32, (2 * ATT_BLOCK, nk), 0) % ATT_BLOCK
    kj = lax.broadcasted_iota(jnp.int32, (2 * ATT_BLOCK, nk), 1)
    if single:
        return qi >= kj
    rel = qi + ATT_BLOCK - kj
    return (rel >= 0) & (rel <= ATT_BLOCK) & ((n > 0) | (kj >= ATT_BLOCK))


def _lane_halves():
    lane = lax.broadcasted_iota(jnp.int32, (1, LANES), 1)
    return lane < HEAD_DIM


def _stack_heads(t2, kh):
    first = _lane_halves()
    parts = []
    for a in range(2):
        ta = jnp.where(first if a == 0 else ~first, t2, jnp.zeros_like(t2))
        if a != kh[a]:
            ta = pltpu.roll(ta, HEAD_DIM, 1)
        parts.append(ta)
    return jnp.concatenate(parts, axis=0)


def _fold_heads(t, kh):
    t0, t1 = t[:ATT_BLOCK], t[ATT_BLOCK:]
    if t.shape[1] == LANES:
        if kh[0] != 0:
            t0 = pltpu.roll(t0, HEAD_DIM, 1)
        if kh[1] != 1:
            t1 = pltpu.roll(t1, HEAD_DIM, 1)
    return jnp.where(_lane_halves(), t0, t1)


def _rows_of_heads(t2):
    return jnp.concatenate([t2[:, 0:1], t2[:, HEAD_DIM:HEAD_DIM + 1]], axis=0)


def _pair_fwd(q2, kk, vv, mask, kh, sink_col):
    qs = _stack_heads(q2, kh)
    s = lax.dot_general(qs, kk, NT, preferred_element_type=F32) * ATT_SCALE
    s = jnp.where(mask, s, NEG_INF)
    m = jnp.max(s, axis=-1, keepdims=True)
    if sink_col is not None:
        m = jnp.maximum(m, sink_col)
    p = jnp.exp(s - m)
    l = jnp.sum(p, axis=-1, keepdims=True)
    if sink_col is not None:
        l = l + jnp.exp(sink_col - m)
    return jnp.dot(p.astype(BF16), vv, preferred_element_type=F32), m, l


def _pair_bwd(q2, kk, vv, do2, o2, lse2, mask, kh):
    first = _lane_halves()
    dd = do2 * o2
    dsum = jnp.concatenate([jnp.sum(jnp.where(first, dd, 0.0), axis=-1, keepdims=True),
                            jnp.sum(jnp.where(first, 0.0, dd), axis=-1, keepdims=True)], axis=0)
    qs = _stack_heads(q2, kh)
    dos = _stack_heads(do2.astype(BF16), kh)
    s = lax.dot_general(qs, kk, NT, preferred_element_type=F32) * ATT_SCALE
    s = jnp.where(mask, s, NEG_INF)
    p = jnp.exp(s - _rows_of_heads(lse2))
    dp = lax.dot_general(dos, vv, NT, preferred_element_type=F32)
    ds = (p * (dp - dsum) * ATT_SCALE).astype(BF16)
    dq = _fold_heads(jnp.dot(ds, kk, preferred_element_type=F32), kh)
    dk = lax.dot_general(ds, qs, TN, preferred_element_type=F32)
    dv = lax.dot_general(p.astype(BF16), dos, TN, preferred_element_type=F32)
    return dq, dk, dv


def swa_fwd(qkv, sinks, *, name):
    bsz, s_len, width = qkv.shape
    ckv = SWA_KV_HEADS * HEAD_DIM
    nb = s_len // ATT_BLOCK
    kblk = D_MODEL // ckv

    def body(sink_ref, q_ref, kp_ref, kc_ref, vp_ref, vc_ref, o_ref, lse_ref, ob_ref):
        n = pl.program_id(1)
        mask = _band_mask(n, False)
        kk = jnp.concatenate([kp_ref[0], kc_ref[0]], axis=0)
        vv = jnp.concatenate([vp_ref[0], vc_ref[0]], axis=0)
        top = lax.broadcasted_iota(jnp.int32, (2 * ATT_BLOCK, 1), 0) < ATT_BLOCK
        for hp in range(N_HEADS // 2):
            cols = slice(hp * LANES, (hp + 1) * LANES)
            kb, kh = _kv_place(2 * hp, SWA_KV_HEADS)
            kcols = slice(kb * LANES, (kb + 1) * LANES)
            sink_col = jnp.where(top, sink_ref[2 * hp], sink_ref[2 * hp + 1])
            pv, m, l = _pair_fwd(q_ref[0, :, cols], kk[:, kcols], vv[:, kcols], mask, (kh, kh), sink_col)
            o2 = _fold_heads(pv / l, (kh, kh))
            o_ref[0, :, cols] = o2
            ob_ref[0, :, cols] = o2.astype(BF16)
            lse_ref[0, :, cols] = _fold_heads(m + jnp.log(l), (kh, kh))

    prev = lambda n: jnp.maximum(n - 1, 0)
    out = pl.BlockSpec((1, ATT_BLOCK, D_MODEL), lambda b, n: (b, n, 0))
    sd = lambda dt: jax.ShapeDtypeStruct((bsz, s_len, D_MODEL), dt)
    return pl.pallas_call(
        body,
        out_shape=(sd(F32), sd(F32), sd(BF16)),
        grid=(bsz, nb),
        in_specs=[pl.BlockSpec(memory_space=pltpu.SMEM), out,
                  pl.BlockSpec((1, ATT_BLOCK, ckv), lambda b, n: (b, prev(n), kblk)),
                  pl.BlockSpec((1, ATT_BLOCK, ckv), lambda b, n: (b, n, kblk)),
                  pl.BlockSpec((1, ATT_BLOCK, ckv), lambda b, n: (b, prev(n), kblk + 1)),
                  pl.BlockSpec((1, ATT_BLOCK, ckv), lambda b, n: (b, n, kblk + 1))],
        out_specs=(out, out, out),
        compiler_params=_params(("parallel", "arbitrary")),
        name=name,
    )(sinks, qkv, qkv, qkv, qkv, qkv)


def swa_bwd(qkv, o, lse, do, *, name):
    bsz, s_len, width = qkv.shape
    ckv = SWA_KV_HEADS * HEAD_DIM
    nb = s_len // ATT_BLOCK
    kblk = D_MODEL // ckv

    def body(q_ref, kp_ref, kc_ref, vp_ref, vc_ref, o_ref, lse_ref, do_ref, dq_ref, dk_ref, dv_ref, dkk, dvv, ck, cv):
        n = pl.program_id(1)

        @pl.when(n < nb)
        def _():
            mask = _band_mask(n, False)
            kk = jnp.concatenate([kp_ref[0], kc_ref[0]], axis=0)
            vv = jnp.concatenate([vp_ref[0], vc_ref[0]], axis=0)
            dkk[...] = jnp.zeros_like(dkk)
            dvv[...] = jnp.zeros_like(dvv)
            for hp in range(N_HEADS // 2):
                cols = slice(hp * LANES, (hp + 1) * LANES)
                kb, kh = _kv_place(2 * hp, SWA_KV_HEADS)
                kcols = slice(kb * LANES, (kb + 1) * LANES)
                dq, dk, dv = _pair_bwd(q_ref[0, :, cols], kk[:, kcols], vv[:, kcols], do_ref[0, :, cols], o_ref[0, :, cols],
                                       lse_ref[0, :, cols], mask, (kh, kh))
                dq_ref[0, :, cols] = dq
                dkk[:, kcols] += dk
                dvv[:, kcols] += dv

        @pl.when((n >= 1) & (n < nb))
        def _():
            dk_ref[0] = ck[...] + dkk[pl.ds(0, ATT_BLOCK), :]
            dv_ref[0] = cv[...] + dvv[pl.ds(0, ATT_BLOCK), :]

        @pl.when(n == nb)
        def _():
            dk_ref[0] = ck[...]
            dv_ref[0] = cv[...]

        @pl.when(n < nb)
        def _():
            ck[...] = dkk[pl.ds(ATT_BLOCK, ATT_BLOCK), :]
            cv[...] = dvv[pl.ds(ATT_BLOCK, ATT_BLOCK), :]

    clamp = lambda n: jnp.minimum(n, nb - 1)
    prev = lambda n: jnp.maximum(n - 1, 0)
    row = pl.BlockSpec((1, ATT_BLOCK, D_MODEL), lambda b, n: (b, clamp(n), 0))
    kv_out = pl.BlockSpec((1, ATT_BLOCK, ckv), lambda b, n: (b, prev(n), 0))
    return pl.pallas_call(
        body,
        out_shape=(jax.ShapeDtypeStruct((bsz, s_len, D_MODEL), F32), jax.ShapeDtypeStruct((bsz, s_len, ckv), F32),
                   jax.ShapeDtypeStruct((bsz, s_len, ckv), F32)),
        grid=(bsz, nb + 1),
        in_specs=[row,
                  pl.BlockSpec((1, ATT_BLOCK, ckv), lambda b, n: (b, prev(clamp(n)), kblk)),
                  pl.BlockSpec((1, ATT_BLOCK, ckv), lambda b, n: (b, clamp(n), kblk)),
                  pl.BlockSpec((1, ATT_BLOCK, ckv), lambda b, n: (b, prev(clamp(n)), kblk + 1)),
                  pl.BlockSpec((1, ATT_BLOCK, ckv), lambda b, n: (b, clamp(n), kblk + 1)),
                  row, row, row],
        out_specs=(row, kv_out, kv_out),
        scratch_shapes=[pltpu.VMEM((2 * ATT_BLOCK, ckv), F32), pltpu.VMEM((2 * ATT_BLOCK, ckv), F32),
                        pltpu.VMEM((ATT_BLOCK, ckv), F32), pltpu.VMEM((ATT_BLOCK, ckv), F32)],
        compiler_params=_params(("parallel", "arbitrary")),
        name=name,
    )(qkv, qkv, qkv, qkv, qkv, o, lse, do)


DIL_PATTERNS = tuple((d, 2048 // d // ATT_BLOCK) for d in DILATIONS)
MHA = (0, 1)


def _dil_rows(idx, d, nb):
    j = idx // nb
    n = idx % nb
    base = j + n * (ATT_BLOCK * d)
    prev = jnp.maximum(base - ATT_BLOCK * d, j)
    if d == 1:
        return n, pl.ds(pl.multiple_of(base, ATT_BLOCK), ATT_BLOCK), pl.ds(pl.multiple_of(prev, ATT_BLOCK), ATT_BLOCK)
    return n, pl.ds(base, ATT_BLOCK, stride=d), pl.ds(prev, ATT_BLOCK, stride=d)


def dil_fwd(qkv, *, name):
    bsz, s_len, _ = qkv.shape
    assert s_len == DIL_PATTERNS[0][0] * DIL_PATTERNS[0][1] * ATT_BLOCK
    npair = N_HEADS // 2

    def body(q_ref, k_ref, v_ref, y_ref, lse_ref, yb_ref, m_acc, l_acc):
        for ci, (d, nb) in enumerate(DIL_PATTERNS):
            single = nb == 1

            def block(idx, carry):
                n, rows, prows = _dil_rows(idx, d, nb)
                kc = k_ref[rows, :].astype(BF16)
                vc = v_ref[rows, :].astype(BF16)
                if single:
                    kk, vv = kc, vc
                else:
                    kk = jnp.concatenate([k_ref[prows, :].astype(BF16), kc], axis=0)
                    vv = jnp.concatenate([v_ref[prows, :].astype(BF16), vc], axis=0)
                pv, m, l = _pair_fwd(q_ref[rows, :].astype(BF16), kk, vv, _band_mask(n, single), MHA, None)
                o2, m2, l2 = _fold_heads(pv, MHA), _fold_heads(m, MHA), _fold_heads(l, MHA)
                if ci == 0:
                    y_ref[rows, :] = o2
                    m_acc[rows, :] = m2
                    l_acc[rows, :] = l2
                else:
                    m_old = m_acc[rows, :]
                    m_new = jnp.maximum(m_old, m2)
                    w_old = jnp.exp(m_old - m_new)
                    w_new = jnp.exp(m2 - m_new)
                    y_ref[rows, :] = y_ref[rows, :] * w_old + o2 * w_new
                    l_acc[rows, :] = l_acc[rows, :] * w_old + l2 * w_new
                    m_acc[rows, :] = m_new
                return carry

            lax.fori_loop(0, d * nb, block, 0)
        y = y_ref[...] / l_acc[...]
        y_ref[...] = y
        yb_ref[...] = y.astype(BF16)
        lse_ref[...] = m_acc[...] + jnp.log(l_acc[...])

    slab = lambda off: pl.BlockSpec((None, s_len, LANES), functools.partial(lambda o, b, h: (b, 0, o + h), off))
    sd = lambda dt: jax.ShapeDtypeStruct((bsz, s_len, D_MODEL), dt)
    return pl.pallas_call(
        body,
        out_shape=(sd(F32), sd(F32), sd(BF16)),
        grid=(bsz, npair),
        in_specs=[slab(0), slab(npair), slab(2 * npair)],
        out_specs=(slab(0), slab(0), slab(0)),
        scratch_shapes=[pltpu.VMEM((s_len, LANES), F32), pltpu.VMEM((s_len, LANES), F32)],
        compiler_params=_params(("parallel", "parallel")),
        name=name,
    )(qkv, qkv, qkv)


def dil_bwd(qkv, y, lse, dy, *, name):
    bsz, s_len, _ = qkv.shape
    npair = N_HEADS // 2

    def body(q_ref, k_ref, v_ref, y_ref, lse_ref, dy_ref, dq_ref, dk_ref, dv_ref):
        dq_ref[...] = jnp.zeros_like(dq_ref)
        dk_ref[...] = jnp.zeros_like(dk_ref)
        dv_ref[...] = jnp.zeros_like(dv_ref)
        for d, nb in DIL_PATTERNS:
            single = nb == 1

            def block(idx, carry):
                n, rows, prows = _dil_rows(idx, d, nb)
                kc = k_ref[rows, :].astype(BF16)
                vc = v_ref[rows, :].astype(BF16)
                if single:
                    kk, vv = kc, vc
                else:
                    kk = jnp.concatenate([k_ref[prows, :].astype(BF16), kc], axis=0)
                    vv = jnp.concatenate([v_ref[prows, :].astype(BF16), vc], axis=0)
                dq, dk, dv = _pair_bwd(q_ref[rows, :].astype(BF16), kk, vv, dy_ref[rows, :], y_ref[rows, :],
                                       lse_ref[rows, :], _band_mask(n, single), MHA)
                dq_ref[rows, :] += dq
                if single:
                    dk_ref[rows, :] += dk
                    dv_ref[rows, :] += dv
                else:
                    dk_ref[prows, :] += dk[:ATT_BLOCK]
                    dv_ref[prows, :] += dv[:ATT_BLOCK]
                    dk_ref[rows, :] += dk[ATT_BLOCK:]
                    dv_ref[rows, :] += dv[ATT_BLOCK:]
                return carry

            lax.fori_loop(0, d * nb, block, 0)

    slab = lambda off: pl.BlockSpec((None, s_len, LANES), functools.partial(lambda o, b, h: (b, 0, o + h), off))
    sd = jax.ShapeDtypeStruct((bsz, s_len, D_MODEL), F32)
    return pl.pallas_call(
        body,
        out_shape=(sd, sd, sd),
        grid=(bsz, npair),
        in_specs=[slab(0), slab(npair), slab(2 * npair), slab(0), slab(0), slab(0)],
        out_specs=(slab(0), slab(0), slab(0)),
        compiler_params=_params(("parallel", "parallel")),
        name=name,
    )(qkv, qkv, qkv, y, lse, dy)


def sink_grad(do, o, lse, sink_lanes, *, name):
    t_dim, d = do.shape
    tr = _pick(t_dim, (256, 128, 8))

    def body(do_ref, o_ref, l_ref, s_ref, out_ref):
        @pl.when(pl.program_id(0) == 0)
        def _():
            out_ref[...] = jnp.zeros_like(out_ref)

        out_ref[...] += jnp.sum(-jnp.exp(s_ref[...] - l_ref[...]) * do_ref[...] * o_ref[...], axis=0, keepdims=True)

    row = pl.BlockSpec((tr, d), lambda i: (i, 0))
    vec = pl.BlockSpec((1, d), lambda i: (0, 0))
    return pl.pallas_call(
        body,
        out_shape=jax.ShapeDtypeStruct((1, d), F32),
        grid=(t_dim // tr,),
        in_specs=[row, row, row, vec],
        out_specs=vec,
        compiler_params=_params(("arbitrary",)),
        name=name,
    )(do, o, lse, sink_lanes)


def adamw(w, g, m, v, *, name):
    rows, cols = w.shape
    tr = _pick(rows, (256, 128, 64, 32, 16, 8))

    def body(w_ref, g_ref, m_ref, v_ref, d_ref, nm_ref, nv_ref):
        gv = g_ref[...]
        nm = ADAM_B1 * m_ref[...] + (1.0 - ADAM_B1) * gv
        nv = ADAM_B2 * v_ref[...] + (1.0 - ADAM_B2) * (gv * gv)
        m_hat = nm / (1.0 - ADAM_B1 ** ADAM_STEP)
        v_hat = nv / (1.0 - ADAM_B2 ** ADAM_STEP)
        d_ref[...] = -ADAM_LR * (m_hat / (jnp.sqrt(v_hat) + ADAM_EPS) + ADAM_WD * w_ref[...])
        nm_ref[...] = nm
        nv_ref[...] = nv

    row = pl.BlockSpec((tr, cols), lambda i: (i, 0))
    return pl.pallas_call(
        body,
        out_shape=(jax.ShapeDtypeStruct((rows, cols), F32),) * 3,
        grid=(rows // tr,),
        in_specs=[row] * 4,
        out_specs=(row, row, row),
        compiler_params=_params(("parallel",)),
        name=name,
    )(w, g, m, v)


def _place():
    return lax.axis_index("x"), lax.axis_index("y"), lax.axis_index("c")


def all_gather(x, *, name):
    def body(x_ref, out_ref, send_sems, recv_sems, local_sem):
        x, y, c = _place()
        me, sibling = (x, y, c), (x, y, 1 - c)
        chips = [(1 - x, y), (x, 1 - y), (1 - x, 1 - y)]

        def slot(px, py, pc):
            return out_ref.at[4 * px + 2 * py + pc]

        def copy(k, block, to, src=None):
            return pltpu.make_async_remote_copy(
                src_ref=slot(*block) if src is None else src, dst_ref=slot(*block),
                send_sem=send_sems.at[k], recv_sem=recv_sems.at[k], device_id=to, device_id_type=MESH)

        mine = pltpu.make_async_copy(x_ref, slot(*me), local_sem)
        mine.start()
        first = [copy(0, me, sibling, src=x_ref)]
        first += [copy(1 + j, me, (*chip, c), src=x_ref) for j, chip in enumerate(chips)]
        for cp in first:
            cp.start()
        passed = [copy(4 + j, (*chip, c), sibling) for j, chip in enumerate(chips)]
        for j, chip in enumerate(chips):
            copy(1 + j, (*chip, c), me).wait_recv()
            passed[j].start()
        copy(0, sibling, me).wait_recv()
        for j, chip in enumerate(chips):
            copy(4 + j, (*chip, 1 - c), me).wait_recv()
        for cp in first + passed:
            cp.wait_send()
        mine.wait()

    return pl.pallas_call(
        body,
        out_shape=jax.ShapeDtypeStruct((N_DEV,) + x.shape, x.dtype),
        in_specs=[pl.BlockSpec(memory_space=pl.ANY)],
        out_specs=pl.BlockSpec(memory_space=pl.ANY),
        scratch_shapes=[pltpu.SemaphoreType.DMA((7,)), pltpu.SemaphoreType.DMA((7,)), pltpu.SemaphoreType.DMA(())],
        name=name,
    )(x)


def all_to_all(x, *, name):
    def body(x_ref, out_ref, send_sems, recv_sems, local_sem):
        x, y, c = _place()
        me = 4 * x + 2 * y + c
        mine = pltpu.make_async_copy(x_ref.at[me], out_ref.at[me], local_sem)
        mine.start()
        copies = []
        for k in range(1, N_DEV):
            px = 1 - x if k & 4 else x
            py = 1 - y if k & 2 else y
            pc = 1 - c if k & 1 else c
            peer = 4 * px + 2 * py + pc
            copies.append(pltpu.make_async_remote_copy(
                src_ref=x_ref.at[peer], dst_ref=out_ref.at[me], send_sem=send_sems.at[k - 1], recv_sem=recv_sems.at[k - 1],
                device_id=(px, py, pc), device_id_type=MESH))
        for cp in copies:
            cp.start()
        for cp in copies:
            cp.wait_recv()
        for cp in copies:
            cp.wait_send()
        mine.wait()

    return pl.pallas_call(
        body,
        out_shape=jax.ShapeDtypeStruct(x.shape, x.dtype),
        in_specs=[pl.BlockSpec(memory_space=pl.ANY)],
        out_specs=pl.BlockSpec(memory_space=pl.ANY),
        scratch_shapes=[pltpu.SemaphoreType.DMA((7,)), pltpu.SemaphoreType.DMA((7,)), pltpu.SemaphoreType.DMA(())],
        name=name,
    )(x)


def sum_slots(x, *, name):
    _, rows, cols = x.shape
    tr = _pick(rows, (512, 256, 128, 64, 32, 16))

    def body(x_ref, o_ref):
        acc = x_ref[0].astype(F32)
        for k in range(1, N_DEV):
            acc = acc + x_ref[k].astype(F32)
        o_ref[...] = acc

    return pl.pallas_call(
        body,
        out_shape=jax.ShapeDtypeStruct((rows, cols), F32),
        grid=(rows // tr,),
        in_specs=[pl.BlockSpec((N_DEV, tr, cols), lambda i: (0, i, 0))],
        out_specs=pl.BlockSpec((tr, cols), lambda i: (i, 0)),
        compiler_params=_params(("parallel",)),
        name=name,
    )(x)


BIG = ("w_in", "w_branch", "w_out", "w_ffn_in", "w_ffn_out")
SMALL = ("conv_b", "w_rg", "b_rg", "w_ig", "b_ig", "lru_lambda", "sinks", "ln1_g", "ln1_b", "ln2_g", "ln2_b")
N_LRU_BLOCKS = D_MODEL // HEAD_DIM
SMALL_ROWS_TILE = 512


def _block_diag(w):
    z = jnp.zeros((N_LRU_BLOCKS // 2, HEAD_DIM, HEAD_DIM), w.dtype)
    top = jnp.concatenate([w[0::2], z], axis=2)
    bot = jnp.concatenate([z, w[1::2]], axis=2)
    return jnp.concatenate([top, bot], axis=1)


def _block_diag_grad(g):
    return jnp.stack([g[:, :HEAD_DIM, :HEAD_DIM], g[:, HEAD_DIM:, HEAD_DIM:]], axis=1).reshape(N_LRU_BLOCKS, HEAD_DIM, HEAD_DIM)


def layer_fwd(x, xb, p, bsz):
    t_dim = x.shape[0]
    s_len = t_dim // bsz
    w_f, w_qs, w_qd = p["w_in_f"], p["w_in_qs"], p["w_in_qd"]
    proj_f = matmul(xb, w_f, name="proj_f")
    qs = matmul(xb, w_qs, out_dtype=BF16, name="proj_qs").reshape(bsz, s_len, W_QS)
    qd = matmul(xb, w_qd, name="proj_qd").reshape(bsz, s_len, W_QD)
    proj_f3 = proj_f.reshape(bsz, s_len, W_F)
    wr_bd, wi_bd = _block_diag(p["w_rg"]), _block_diag(p["w_ig"])
    y_a, h = lru_fwd(proj_f3, p["conv_w"], p["conv_b"], wr_bd, wi_bd, p["b_rg"], p["b_ig"], p["lru_lambda"], name="lru_fwd")
    y_b, lse_b, y_bb = swa_fwd(qs, p["sinks"], name="swa_fwd")
    y_c, lse_c, y_cb = dil_fwd(qd, name="dil_fwd")
    ys = [t.reshape(t_dim, D_MODEL) for t in (y_a, y_bb, y_cb)]
    br = [matmul(ys[n], p["w_branch"][n], name="branch") for n in range(3)]
    merged = merge_fwd(proj_f, br, name="merge_fwd")
    mix = matmul(merged, p["w_out"], name="w_out")
    x1, x1b, z1 = ln_fwd(x, mix, p["ln1_g"], p["ln1_b"], name="ln_fwd")
    h13 = matmul(x1b, p["w_ffn_in"], name="ffn_in")
    act = swiglu_fwd(h13, name="swiglu_fwd")
    ffn = matmul(act, p["w_ffn_out"], name="ffn_out")
    x2, x2b, z2 = ln_fwd(x1, ffn, p["ln2_g"], p["ln2_b"], name="ln_fwd")
    saved = dict(xb=xb, proj_f=proj_f, qs=qs, qd=qd, h=h, ys=ys, y_b=y_b, y_c=y_c, lse_b=lse_b, lse_c=lse_c, br=br, merged=merged,
                 z1=z1, x1b=x1b, h13=h13, act=act, z2=z2, wr_bd=wr_bd, wi_bd=wi_bd)
    return x2, x2b, saved


def layer_bwd(dx2, p, s, bsz):
    t_dim = dx2.shape[0]
    s_len = t_dim // bsz
    g = {}
    dz2, dz2b, g["ln2_g"], g["ln2_b"] = ln_bwd(dx2, s["z2"], p["ln2_g"], name="ln_bwd")
    dact = matmul(dz2b, p["w_ffn_out"], trans_b=True, name="d_act")
    dh13 = swiglu_bwd(dact, s["h13"], name="swiglu_bwd")
    g["w_ffn_out"] = matmul(s["act"], dz2b, trans_a=True, name="dw_ffn_out")
    g["w_ffn_in"] = matmul(s["x1b"], dh13, trans_a=True, name="dw_ffn_in")
    dx1 = matmul(dh13, p["w_ffn_in"], trans_b=True, add=dz2, add_scale=ALPHA, name="dx_ffn")
    dz1, dz1b, g["ln1_g"], g["ln1_b"] = ln_bwd(dx1, s["z1"], p["ln1_g"], name="ln_bwd")
    dmerged = matmul(dz1b, p["w_out"], trans_b=True, name="d_merged")
    g["w_out"] = matmul(s["merged"], dz1b, trans_a=True, name="dw_out")
    *dbr, dgates = merge_bwd(dmerged, s["proj_f"], s["br"], name="merge_bwd")
    dys = [matmul(dbr[n], p["w_branch"][n], trans_b=True, name="d_branch") for n in range(3)]
    g["w_branch"] = jnp.stack([matmul(s["ys"][n], dbr[n], trans_a=True, name="dw_branch") for n in range(3)])
    shape3 = (bsz, s_len, D_MODEL)
    (dlx, dlg, g["conv_w"], g["conv_b"], g["b_rg"], g["b_ig"], g["lru_lambda"], dwr, dwi) = lru_bwd(
        dys[0].reshape(shape3), s["proj_f"].reshape(bsz, s_len, W_F), s["h"], p["conv_w"], p["conv_b"], s["wr_bd"], s["wi_bd"],
        jnp.swapaxes(s["wr_bd"], 1, 2), jnp.swapaxes(s["wi_bd"], 1, 2), p["b_rg"], p["b_ig"], p["lru_lambda"], name="lru_bwd")
    g["w_rg"], g["w_ig"] = _block_diag_grad(dwr), _block_diag_grad(dwi)
    dy_b3 = dys[1].reshape(shape3)
    dqs = swa_bwd(s["qs"], s["y_b"], s["lse_b"], dy_b3, name="swa_bwd")
    sink_lanes = jnp.repeat(p["sinks"], HEAD_DIM).reshape(1, D_MODEL)
    g["sinks"] = sink_grad(dys[1], s["y_b"].reshape(t_dim, D_MODEL), s["lse_b"].reshape(t_dim, D_MODEL), sink_lanes,
                           name="sink_grad").reshape(N_HEADS, HEAD_DIM).sum(axis=1)
    dqd = dil_bwd(s["qd"], s["y_c"], s["lse_c"], dys[2].reshape(shape3), name="dil_bwd")
    flat = lambda t: t.reshape(t_dim, t.shape[-1])
    dproj_f = jnp.concatenate([flat(dlx), flat(dlg), dgates], axis=1)
    dproj_qs = jnp.concatenate([flat(t) for t in dqs], axis=1).astype(BF16)
    dproj_qd = jnp.concatenate([flat(t) for t in dqd], axis=1).astype(BF16)
    g["w_in_f"] = matmul(s["xb"], dproj_f, trans_a=True, name="dw_in_f")
    g["w_in_qs"] = matmul(s["xb"], dproj_qs, trans_a=True, name="dw_in_qs")
    g["w_in_qd"] = matmul(s["xb"], dproj_qd, trans_a=True, name="dw_in_qd")
    dx = matmul(dproj_f, p["w_in_f"], trans_b=True, add=dz1, add_scale=ALPHA, name="dx_f")
    dx = matmul(dproj_qs, p["w_in_qs"], trans_b=True, add=dx, name="dx_qs")
    dx = matmul(dproj_qd, p["w_in_qd"], trans_b=True, add=dx, name="dx_qd")
    g = {k: (v.reshape(p[k].shape) if k in p else v) for k, v in g.items()}
    return dx, g


def local_step(x, target, params):
    bsz, s_len, d = x.shape
    t_dim = bsz * s_len
    xf = x.reshape(t_dim, d)
    xb = xf.astype(BF16)
    saved = []
    for l in range(DEPTH):
        xf, xb, s = layer_fwd(xf, xb, {k: v[l] for k, v in params.items()}, bsz)
        saved.append(s)
    dy, sq = loss_head(xf, target.reshape(t_dim, d), name="loss_head")
    loss = 0.5 * jnp.sum(sq) / d
    grads = [None] * DEPTH
    for l in reversed(range(DEPTH)):
        dy, grads[l] = layer_bwd(dy, {k: v[l] for k, v in params.items()}, saved[l], bsz)
    grads = {k: jnp.stack([grads[l][k] for l in range(DEPTH)]) for k in grads[0]}
    return loss, dy.reshape(bsz, s_len, d), grads


W_IN_SEGMENTS = (("w_in_f", 0, 0, 2 * D_MODEL), ("w_in_qs", 0, 2 * D_MODEL, W_QS), ("w_in_qd", 0, 2 * D_MODEL + W_QS, W_QD),
                 ("w_in_f", 2 * D_MODEL, 2 * D_MODEL + W_QS + W_QD, 3 * D_MODEL))
ROW_SHARDED = ("w_branch", "w_out", "w_ffn_out")


def _cols_of_shards(shards, lo, hi):
    width = shards[0].shape[-1]
    parts = []
    for k, sh in enumerate(shards):
        a, b = max(lo, k * width), min(hi, (k + 1) * width)
        if a < b:
            parts.append(sh[..., a - k * width:b - k * width])
    return parts[0] if len(parts) == 1 else jnp.concatenate(parts, axis=-1)


def _cols_of_w_in(pieces, lo, hi):
    parts = []
    for name, p0, l0, width in W_IN_SEGMENTS:
        a, b = max(lo, l0), min(hi, l0 + width)
        if a < b:
            parts.append(pieces[name][..., p0 + a - l0:p0 + b - l0])
    return parts[0] if len(parts) == 1 else jnp.concatenate(parts, axis=-1)


def _rows_to_full(name, t, shard_shape):
    t = t.reshape((N_DEV,) + shard_shape)
    if name == "w_branch":
        return jnp.transpose(t, (1, 2, 0, 3, 4)).reshape(shard_shape[0], 3, -1, D_MODEL)
    return jnp.transpose(t, (1, 0, 2, 3)).reshape(shard_shape[0], -1, D_MODEL)


def _full_to_rows(name, t):
    if name == "w_branch":
        t = jnp.transpose(t.reshape(DEPTH, 3, N_DEV, -1, D_MODEL), (2, 0, 1, 3, 4))
    else:
        t = jnp.transpose(t.reshape(DEPTH, N_DEV, -1, D_MODEL), (1, 0, 2, 3))
    return t.reshape(N_DEV, -1, D_MODEL)


def _pad_rows(flat, tile_rows):
    n = flat.shape[0]
    per = tile_rows * LANES
    total = -(-n // per) * per
    return jnp.pad(flat, (0, total - n)).reshape(-1, LANES)


def kernel(x, w_in, conv_w, conv_b, w_rg, b_rg, w_ig, b_ig, lru_lambda, sinks, w_branch, w_out, ln1_g, ln1_b, w_ffn_in, w_ffn_out, ln2_g, ln2_b, loss_target, m_w_in, m_conv_w, m_conv_b, m_w_rg, m_b_rg, m_w_ig, m_b_ig, m_lru_lambda, m_sinks, m_w_branch, m_w_out, m_ln1_g, m_ln1_b, m_w_ffn_in, m_w_ffn_out, m_ln2_g, m_ln2_b, v_w_in, v_conv_w, v_conv_b, v_w_rg, v_b_rg, v_w_ig, v_b_ig, v_lru_lambda, v_sinks, v_w_branch, v_w_out, v_ln1_g, v_ln1_b, v_w_ffn_in, v_w_ffn_out, v_ln2_g, v_ln2_b):
    w = dict(w_in=w_in, conv_w=conv_w, conv_b=conv_b, w_rg=w_rg, b_rg=b_rg, w_ig=w_ig, b_ig=b_ig, lru_lambda=lru_lambda, sinks=sinks,
             w_branch=w_branch, w_out=w_out, ln1_g=ln1_g, ln1_b=ln1_b, w_ffn_in=w_ffn_in, w_ffn_out=w_ffn_out, ln2_g=ln2_g, ln2_b=ln2_b)
    m = dict(w_in=m_w_in, conv_w=m_conv_w, conv_b=m_conv_b, w_rg=m_w_rg, b_rg=m_b_rg, w_ig=m_w_ig, b_ig=m_b_ig, lru_lambda=m_lru_lambda,
             sinks=m_sinks, w_branch=m_w_branch, w_out=m_w_out, ln1_g=m_ln1_g, ln1_b=m_ln1_b, w_ffn_in=m_w_ffn_in, w_ffn_out=m_w_ffn_out,
             ln2_g=m_ln2_g, ln2_b=m_ln2_b)
    v = dict(w_in=v_w_in, conv_w=v_conv_w, conv_b=v_conv_b, w_rg=v_w_rg, b_rg=v_b_rg, w_ig=v_w_ig, b_ig=v_b_ig, lru_lambda=v_lru_lambda,
             sinks=v_sinks, w_branch=v_w_branch, w_out=v_w_out, ln1_g=v_ln1_g, ln1_b=v_ln1_b, w_ffn_in=v_w_ffn_in, w_ffn_out=v_w_ffn_out,
             ln2_g=v_ln2_g, ln2_b=v_ln2_b)
    order = ["w_in", "conv_w", "conv_b", "w_rg", "b_rg", "w_ig", "b_ig", "lru_lambda", "sinks", "w_branch", "w_out", "ln1_g", "ln1_b",
             "w_ffn_in", "w_ffn_out", "ln2_g", "ln2_b"]
    me = 4 * lax.axis_index("x") + 2 * lax.axis_index("y") + lax.axis_index("c")

    params = {}
    g_in = all_gather(w_in.astype(BF16).reshape(DEPTH * D_MODEL, -1), name="gather_w_in")
    sh_in = [g_in[k].reshape(DEPTH, D_MODEL, -1) for k in range(N_DEV)]
    params["w_in_f"] = jnp.concatenate([_cols_of_shards(sh_in, 0, 2 * D_MODEL),
                                        _cols_of_shards(sh_in, 2 * D_MODEL + W_QS + W_QD, W_F + W_QS + W_QD)], axis=-1)
    params["w_in_qs"] = _cols_of_shards(sh_in, 2 * D_MODEL, 2 * D_MODEL + W_QS)
    params["w_in_qd"] = _cols_of_shards(sh_in, 2 * D_MODEL + W_QS, 2 * D_MODEL + W_QS + W_QD)
    g_fi = all_gather(w_ffn_in.astype(BF16).reshape(DEPTH * D_MODEL, -1), name="gather_w_ffn_in")
    params["w_ffn_in"] = jnp.concatenate([g_fi[k].reshape(DEPTH, D_MODEL, -1) for k in range(N_DEV)], axis=-1)
    row_counts = [w[k].size // D_MODEL for k in ROW_SHARDED]
    g_rows = all_gather(jnp.concatenate([w[k].astype(BF16).reshape(-1, D_MODEL) for k in ROW_SHARDED]), name="gather_w_rows")
    off = 0
    for k, n in zip(ROW_SHARDED, row_counts):
        params[k] = _rows_to_full(k, g_rows[:, off:off + n], w[k].shape)
        off += n
    cw = all_gather(conv_w.reshape(-1, LANES), name="gather_conv_w")
    params["conv_w"] = jnp.moveaxis(cw.reshape(N_DEV, DEPTH, CONV_WIDTH, LANES), 0, 2).reshape(DEPTH, CONV_WIDTH, D_MODEL)
    for k in SMALL:
        params[k] = w[k]

    loss_local, grad_x, grads = local_step(x, loss_target, params)
    loss = lax.psum(loss_local, ("x", "y", "c"))

    g_final = {}
    shard = w_in.shape[-1]
    slots = jnp.stack([_cols_of_w_in(grads, k * shard, (k + 1) * shard).astype(BF16) for k in range(N_DEV)])
    recv = all_to_all(slots.reshape(N_DEV, DEPTH * D_MODEL, shard), name="exchange_g_w_in")
    g_final["w_in"] = sum_slots(recv, name="sum_g_w_in").reshape(w_in.shape)
    shard = w_ffn_in.shape[-1]
    slots = jnp.stack([grads["w_ffn_in"][..., k * shard:(k + 1) * shard].astype(BF16) for k in range(N_DEV)])
    recv = all_to_all(slots.reshape(N_DEV, DEPTH * D_MODEL, shard), name="exchange_g_w_ffn_in")
    g_final["w_ffn_in"] = sum_slots(recv, name="sum_g_w_ffn_in").reshape(w_ffn_in.shape)
    slots = jnp.concatenate([_full_to_rows(k, grads[k]).astype(BF16) for k in ROW_SHARDED], axis=1)
    g_rows = sum_slots(all_to_all(slots, name="exchange_g_rows"), name="sum_g_rows")
    off = 0
    for k, n in zip(ROW_SHARDED, row_counts):
        g_final[k] = g_rows[off:off + n].reshape(w[k].shape)
        off += n

    small_names = list(SMALL) + ["conv_w"]
    small_sizes = [grads[k].size for k in small_names]
    svec = _pad_rows(jnp.concatenate([grads[k].reshape(-1) for k in small_names]), SMALL_ROWS_TILE)
    ssum = sum_slots(all_gather(svec, name="gather_small_grads"), name="sum_small_grads")
    sflat, off = ssum.reshape(-1), 0
    for k, n in zip(small_names, small_sizes):
        g_final[k] = sflat[off:off + n].reshape(grads[k].shape)
        off += n
    g_final["conv_w"] = lax.dynamic_slice_in_dim(g_final["conv_w"], me * LANES, LANES, axis=2)

    delta, new_m, new_v = {}, {}, {}
    for k in list(BIG) + ["conv_w"]:
        cols = w[k].shape[-1]
        two_d = lambda t: t.reshape(-1, cols)
        d_, m_, v_ = adamw(two_d(w[k]), two_d(g_final[k]), two_d(m[k]), two_d(v[k]), name=f"adamw_{k}")
        delta[k], new_m[k], new_v[k] = d_.reshape(w[k].shape), m_.reshape(w[k].shape), v_.reshape(w[k].shape)
    pack_small = lambda dct: _pad_rows(jnp.concatenate([dct[k].reshape(-1) for k in SMALL]), SMALL_ROWS_TILE)
    d_, m_, v_ = adamw(pack_small(w), pack_small(g_final), pack_small(m), pack_small(v), name="adamw_small")
    off = 0
    for k in SMALL:
        n = w[k].size
        for dst, src in ((delta, d_), (new_m, m_), (new_v, v_)):
            dst[k] = src.reshape(-1)[off:off + n].reshape(w[k].shape)
        off += n
    return (loss, grad_x, *[g_final[k] for k in order], *[delta[k] for k in order], *[new_m[k] for k in order], *[new_v[k] for k in order])
```

```python
import functools
import math

import jax
import jax.numpy as jnp
from jax import lax
from jax.experimental import pallas as pl
from jax.experimental.pallas import tpu as pltpu

F32 = jnp.float32
BF16 = jnp.bfloat16

N_DEV = 8
DEPTH = 4
D_MODEL = 1024
HEAD_DIM = 64
LANES = 128
N_HEADS = D_MODEL // HEAD_DIM
SWA_KV_HEADS = 4
ATT_BLOCK = 128
DILATIONS = (1, 4, 16)
CONV_WIDTH = 4
LRU_C = 8.0
FF_HIDDEN = 2816
ALPHA = (2.0 * DEPTH) ** 0.25
LN_EPS = 1e-5
NEG_INF = -1e30
W_F = 5 * D_MODEL
W_QS = D_MODEL + 2 * SWA_KV_HEADS * HEAD_DIM
W_QD = 3 * D_MODEL

ADAM_LR = 0.001
ADAM_B1 = 0.9
ADAM_B2 = 0.999
ADAM_EPS = 1e-08
ADAM_WD = 0.01
ADAM_STEP = 10

VMEM_LIMIT = 56 * 1024 * 1024
MESH = pl.DeviceIdType.MESH


def _pick(n, cands):
    for c in cands:
        if n % c == 0:
            return c
    raise ValueError(f"no tile for {n} among {cands}")


def _params(sem):
    return pltpu.CompilerParams(dimension_semantics=sem, vmem_limit_bytes=VMEM_LIMIT)


def _tile(n, cap):
    best = None
    for t in range(LANES, cap + 1, LANES):
        if n % t == 0:
            best = t
    assert best is not None, (n, cap)
    return best


def matmul(a, b, *, name, trans_a=False, trans_b=False, out_dtype=F32, add=None, add_scale=1.0):
    if trans_a:
        k_dim, m_dim = a.shape
    else:
        m_dim, k_dim = a.shape
    n_dim = b.shape[0] if trans_b else b.shape[1]
    assert (b.shape[1] if trans_b else b.shape[0]) == k_dim
    tm = _tile(m_dim, 1024)
    tn = _tile(n_dim, 1408)
    tk = _tile(k_dim, 1408)
    nk = k_dim // tk
    dims = (((0 if trans_a else 1,), (1 if trans_b else 0,)), ((), ()))

    def body(*refs):
        if add is None:
            a_ref, b_ref, o_ref, acc_ref = refs
            add_ref = None
        else:
            a_ref, b_ref, add_ref, o_ref, acc_ref = refs
        k = pl.program_id(2)
        part = lax.dot_general(a_ref[...].astype(BF16), b_ref[...].astype(BF16), dims, preferred_element_type=F32)

        def finish(r):
            if add_ref is not None:
                r = r + add_scale * add_ref[...].astype(F32)
            o_ref[...] = r.astype(out_dtype)

        if nk == 1:
            finish(part)
        else:
            @pl.when(k == 0)
            def _():
                acc_ref[...] = part

            @pl.when((k > 0) & (k < nk - 1))
            def _():
                acc_ref[...] += part

            @pl.when(k == nk - 1)
            def _():
                finish(acc_ref[...] + part)

    a_spec = pl.BlockSpec((tk, tm), lambda i, j, k: (k, i)) if trans_a else pl.BlockSpec((tm, tk), lambda i, j, k: (i, k))
    b_spec = pl.BlockSpec((tn, tk), lambda i, j, k: (j, k)) if trans_b else pl.BlockSpec((tk, tn), lambda i, j, k: (k, j))
    in_specs = [a_spec, b_spec]
    args = [a, b]
    if add is not None:
        in_specs.append(pl.BlockSpec((tm, tn), lambda i, j, k: (i, j)))
        args.append(add)
    return pl.pallas_call(
        body,
        out_shape=jax.ShapeDtypeStruct((m_dim, n_dim), out_dtype),
        grid=(m_dim // tm, n_dim // tn, nk),
        in_specs=in_specs,
        out_specs=pl.BlockSpec((tm, tn), lambda i, j, k: (i, j)),
        scratch_shapes=[pltpu.VMEM((tm, tn) if nk > 1 else (8, LANES), F32)],
        compiler_params=_params(("parallel", "parallel", "arbitrary")),
        name=name,
    )(*args)


def ln_fwd(x, r, g, b, *, name):
    t_dim, d = x.shape
    tr = _pick(t_dim, (256, 128, 8))

    def body(x_ref, r_ref, g_ref, b_ref, y_ref, yb_ref, z_ref):
        z = ALPHA * x_ref[...] + r_ref[...]
        mu = jnp.mean(z, axis=-1, keepdims=True)
        zc = z - mu
        var = jnp.mean(zc * zc, axis=-1, keepdims=True)
        y = zc * lax.rsqrt(var + LN_EPS) * g_ref[...] + b_ref[...]
        y_ref[...] = y
        yb_ref[...] = y.astype(BF16)
        z_ref[...] = z

    row = pl.BlockSpec((tr, d), lambda i: (i, 0))
    vec = pl.BlockSpec((1, d), lambda i: (0, 0))
    return pl.pallas_call(
        body,
        out_shape=(jax.ShapeDtypeStruct((t_dim, d), F32), jax.ShapeDtypeStruct((t_dim, d), BF16), jax.ShapeDtypeStruct((t_dim, d), F32)),
        grid=(t_dim // tr,),
        in_specs=[row, row, vec, vec],
        out_specs=(row, row, row),
        compiler_params=_params(("parallel",)),
        name=name,
    )(x, r, g.reshape(1, d), b.reshape(1, d))


def ln_bwd(dy, z, g, *, name):
    t_dim, d = dy.shape
    tr = _pick(t_dim, (256, 128, 8))

    def body(dy_ref, z_ref, g_ref, dz_ref, dzb_ref, dg_ref, db_ref):
        @pl.when(pl.program_id(0) == 0)
        def _():
            dg_ref[...] = jnp.zeros_like(dg_ref)
            db_ref[...] = jnp.zeros_like(db_ref)

        z = z_ref[...]
        dyv = dy_ref[...]
        mu = jnp.mean(z, axis=-1, keepdims=True)
        zc = z - mu
        var = jnp.mean(zc * zc, axis=-1, keepdims=True)
        rstd = lax.rsqrt(var + LN_EPS)
        xhat = zc * rstd
        dxhat = dyv * g_ref[...]
        m1 = jnp.mean(dxhat, axis=-1, keepdims=True)
        m2 = jnp.mean(dxhat * xhat, axis=-1, keepdims=True)
        dz = rstd * (dxhat - m1 - xhat * m2)
        dz_ref[...] = dz
        dzb_ref[...] = dz.astype(BF16)
        dg_ref[...] += jnp.sum(dyv * xhat, axis=0, keepdims=True)
        db_ref[...] += jnp.sum(dyv, axis=0, keepdims=True)

    row = pl.BlockSpec((tr, d), lambda i: (i, 0))
    vec = pl.BlockSpec((1, d), lambda i: (0, 0))
    return pl.pallas_call(
        body,
        out_shape=(jax.ShapeDtypeStruct((t_dim, d), F32), jax.ShapeDtypeStruct((t_dim, d), BF16),
                   jax.ShapeDtypeStruct((1, d), F32), jax.ShapeDtypeStruct((1, d), F32)),
        grid=(t_dim // tr,),
        in_specs=[row, row, vec],
        out_specs=(row, row, vec, vec),
        compiler_params=_params(("arbitrary",)),
        name=name,
    )(dy, z, g.reshape(1, d))


def loss_head(y, target, *, name):
    t_dim, d = y.shape
    tr = _pick(t_dim, (256, 128, 8))

    def body(y_ref, t_ref, dy_ref, sq_ref):
        @pl.when(pl.program_id(0) == 0)
        def _():
            sq_ref[...] = jnp.zeros_like(sq_ref)

        diff = y_ref[...] - t_ref[...]
        dy_ref[...] = diff / d
        sq_ref[...] += jnp.sum(diff * diff, axis=0, keepdims=True)

    row = pl.BlockSpec((tr, d), lambda i: (i, 0))
    vec = pl.BlockSpec((1, d), lambda i: (0, 0))
    return pl.pallas_call(
        body,
        out_shape=(jax.ShapeDtypeStruct((t_dim, d), F32), jax.ShapeDtypeStruct((1, d), F32)),
        grid=(t_dim // tr,),
        in_specs=[row, row],
        out_specs=(row, vec),
        compiler_params=_params(("arbitrary",)),
        name=name,
    )(y, target)


def _sigmoid(x):
    return 1.0 / (1.0 + jnp.exp(-x))


def swiglu_fwd(h13, *, name):
    t_dim = h13.shape[0]
    f = h13.shape[1] // 2
    tr = _pick(t_dim, (256, 128, 8))

    def body(h1_ref, h3_ref, act_ref):
        h1 = h1_ref[...]
        act_ref[...] = (h1 * _sigmoid(h1) * h3_ref[...]).astype(BF16)

    return pl.pallas_call(
        body,
        out_shape=jax.ShapeDtypeStruct((t_dim, f), BF16),
        grid=(t_dim // tr,),
        in_specs=[pl.BlockSpec((tr, f), lambda i: (i, 0)), pl.BlockSpec((tr, f), lambda i: (i, 1))],
        out_specs=pl.BlockSpec((tr, f), lambda i: (i, 0)),
        compiler_params=_params(("parallel",)),
        name=name,
    )(h13, h13)


def swiglu_bwd(dact, h13, *, name):
    t_dim = h13.shape[0]
    f = h13.shape[1] // 2
    tr = _pick(t_dim, (256, 128, 8))

    def body(da_ref, h1_ref, h3_ref, dh_ref):
        h1 = h1_ref[...]
        da = da_ref[...]
        sg = _sigmoid(h1)
        dh_ref[:, :f] = (da * h3_ref[...] * sg * (1.0 + h1 * (1.0 - sg))).astype(BF16)
        dh_ref[:, f:] = (da * h1 * sg).astype(BF16)

    return pl.pallas_call(
        body,
        out_shape=jax.ShapeDtypeStruct((t_dim, 2 * f), BF16),
        grid=(t_dim // tr,),
        in_specs=[pl.BlockSpec((tr, f), lambda i: (i, 0)), pl.BlockSpec((tr, f), lambda i: (i, 0)),
                  pl.BlockSpec((tr, f), lambda i: (i, 1))],
        out_specs=pl.BlockSpec((tr, 2 * f), lambda i: (i, 0)),
        compiler_params=_params(("parallel",)),
        name=name,
    )(dact, h13, h13)


def merge_fwd(proj_f, br, *, name):
    t_dim, d = br[0].shape
    tr = _pick(t_dim, (256, 128, 8))

    def body(g0, g1, g2, b0, b1, b2, o_ref):
        o_ref[...] = (_sigmoid(g0[...]) * b0[...] + _sigmoid(g1[...]) * b1[...] + _sigmoid(g2[...]) * b2[...]).astype(BF16)

    row = pl.BlockSpec((tr, d), lambda i: (i, 0))
    gate = [pl.BlockSpec((tr, d), functools.partial(lambda n, i: (i, 2 + n), n)) for n in range(3)]
    return pl.pallas_call(
        body,
        out_shape=jax.ShapeDtypeStruct((t_dim, d), BF16),
        grid=(t_dim // tr,),
        in_specs=gate + [row, row, row],
        out_specs=row,
        compiler_params=_params(("parallel",)),
        name=name,
    )(proj_f, proj_f, proj_f, *br)


def merge_bwd(dmerged, proj_f, br, *, name):
    t_dim, d = dmerged.shape
    tr = _pick(t_dim, (256, 128, 8))

    def body(dm_ref, g0, g1, g2, b0, b1, b2, d0, d1, d2, dg_ref):
        dm = dm_ref[...]
        for n, (g, b, o) in enumerate(((g0, b0, d0), (g1, b1, d1), (g2, b2, d2))):
            sg = _sigmoid(g[...])
            o[...] = (dm * sg).astype(BF16)
            dg_ref[:, n * d:(n + 1) * d] = (dm * b[...] * sg * (1.0 - sg)).astype(BF16)

    row = pl.BlockSpec((tr, d), lambda i: (i, 0))
    gate = [pl.BlockSpec((tr, d), functools.partial(lambda n, i: (i, 2 + n), n)) for n in range(3)]
    return pl.pallas_call(
        body,
        out_shape=(jax.ShapeDtypeStruct((t_dim, d), BF16),) * 3 + (jax.ShapeDtypeStruct((t_dim, 3 * d), BF16),),
        grid=(t_dim // tr,),
        in_specs=[row] + gate + [row, row, row],
        out_specs=(row, row, row, pl.BlockSpec((tr, 3 * d), lambda i: (i, 0))),
        compiler_params=_params(("parallel",)),
        name=name,
    )(dmerged, proj_f, proj_f, proj_f, *br)


GELU_C = math.sqrt(2.0 / math.pi)
PAD = 8


def _gelu(x):
    return 0.5 * x * (1.0 + jnp.tanh(GELU_C * (x + 0.044715 * x * x * x)))


def _gelu_grad(x):
    t = jnp.tanh(GELU_C * (x + 0.044715 * x * x * x))
    return 0.5 * (1.0 + t) + 0.5 * x * (1.0 - t * t) * GELU_C * (1.0 + 3.0 * 0.044715 * x * x)


def _neg_expm1(x):
    series = -x * (1.0 + x * (0.5 + x * (1.0 / 6.0 + x * (1.0 / 24.0 + x * (1.0 / 120.0)))))
    return jnp.where(x > -0.1, series, 1.0 - jnp.exp(x))


def _lru_gates(xv, cw_ref, cb_ref, wr_ref, wi_ref, br_ref, bi_ref, lam_ref, pad_ref, s_len):
    pad_ref[pl.ds(0, PAD), :] = jnp.zeros((PAD, LANES), F32)
    pad_ref[pl.ds(PAD, s_len), :] = xv
    xc = cb_ref[...] + jnp.zeros((s_len, LANES), F32)
    for j in range(CONV_WIDTH):
        xc = xc + pad_ref[pl.ds(PAD - (CONV_WIDTH - 1) + j, s_len), :] * cw_ref[pl.ds(j, 1), :]
    xcb = xc.astype(BF16)
    r = _sigmoid(jnp.dot(xcb, wr_ref[0].astype(BF16), preferred_element_type=F32) + br_ref[...])
    i = _sigmoid(jnp.dot(xcb, wi_ref[0].astype(BF16), preferred_element_type=F32) + bi_ref[...])
    nl = -lam_ref[...]
    sp = jnp.maximum(nl, 0.0) + jnp.log(1.0 + jnp.exp(-jnp.abs(nl)))
    log_a = -LRU_C * r * sp
    a = jnp.exp(log_a)
    mult = jnp.sqrt(_neg_expm1(2.0 * log_a))
    return xc, r, i, sp, a, mult


def _tile_scan(a, b, row, reverse):
    for s in (1, 2, 4):
        if reverse:
            a_sh = pltpu.roll(a, 8 - s, 0)
            b_sh = pltpu.roll(b, 8 - s, 0)
            m = row + s <= 7
        else:
            a_sh = pltpu.roll(a, s, 0)
            b_sh = pltpu.roll(b, s, 0)
            m = row >= s
        b = jnp.where(m, a * b_sh + b, b)
        a = jnp.where(m, a * a_sh, a)
    return a, b


def lru_fwd(proj_f, conv_w, conv_b, wr_bd, wi_bd, b_rg, b_ig, lam, *, name):
    bsz, s_len, _ = proj_f.shape
    d = D_MODEL
    ncb = d // LANES
    n_tiles = s_len // 8

    def body(x_ref, g_ref, cw_ref, cb_ref, wr_ref, wi_ref, br_ref, bi_ref, lam_ref, y_ref, h_ref, pad_ref, a_s, b_s):
        xc, r, i, sp, a, mult = _lru_gates(x_ref[0], cw_ref, cb_ref, wr_ref, wi_ref, br_ref, bi_ref, lam_ref, pad_ref, s_len)
        a_s[...] = a
        b_s[...] = mult * (i * xc)
        row = lax.broadcasted_iota(jnp.int32, (8, LANES), 0)

        def tile(t, carry):
            i0 = pl.multiple_of(t * 8, 8)
            ac, hl = _tile_scan(a_s[pl.ds(i0, 8), :], b_s[pl.ds(i0, 8), :], row, False)
            h = hl + ac * carry
            h_ref[0, pl.ds(i0, 8), :] = h
            return jnp.broadcast_to(h[7:8, :], (8, LANES))

        lax.fori_loop(0, n_tiles, tile, jnp.zeros((8, LANES), F32))
        y_ref[0] = (h_ref[0] * _gelu(g_ref[0])).astype(BF16)

    slab = lambda off: pl.BlockSpec((1, s_len, LANES), functools.partial(lambda o, c, b: (b, 0, o + c), off))
    vec = pl.BlockSpec((1, LANES), lambda c, b: (0, c))
    mat = pl.BlockSpec((1, LANES, LANES), lambda c, b: (c, 0, 0))
    out = pl.BlockSpec((1, s_len, LANES), lambda c, b: (b, 0, c))
    return pl.pallas_call(
        body,
        out_shape=(jax.ShapeDtypeStruct((bsz, s_len, d), BF16), jax.ShapeDtypeStruct((bsz, s_len, d), F32)),
        grid=(ncb, bsz),
        in_specs=[slab(0), slab(ncb), pl.BlockSpec((CONV_WIDTH, LANES), lambda c, b: (0, c)), vec, mat, mat, vec, vec, vec],
        out_specs=(out, out),
        scratch_shapes=[pltpu.VMEM((s_len + 2 * PAD, LANES), F32), pltpu.VMEM((s_len, LANES), F32), pltpu.VMEM((s_len, LANES), F32)],
        compiler_params=_params(("parallel", "parallel")),
        name=name,
    )(proj_f, proj_f, conv_w, conv_b.reshape(1, d), wr_bd, wi_bd, b_rg.reshape(1, d), b_ig.reshape(1, d), lam.reshape(1, d))


def lru_bwd(dy, proj_f, h, conv_w, conv_b, wr_bd, wi_bd, wr_bd_t, wi_bd_t, b_rg, b_ig, lam, *, name):
    bsz, s_len, _ = proj_f.shape
    d = D_MODEL
    ncb = d // LANES
    n_tiles = s_len // 8

    def body(dy_ref, x_ref, g_ref, h_ref, cw_ref, cb_ref, wr_ref, wi_ref, wrt_ref, wit_ref, br_ref, bi_ref, lam_ref,
             dx_ref, dg_ref, dcw_ref, dcb_ref, dbr_ref, dbi_ref, dlam_ref, dwr_ref, dwi_ref, pad_ref, a_s, b_s, l_s):
        @pl.when(pl.program_id(1) == 0)
        def _():
            for ref in (dcw_ref, dcb_ref, dbr_ref, dbi_ref, dlam_ref, dwr_ref, dwi_ref):
                ref[...] = jnp.zeros_like(ref)

        xc, r, i, sp, a, mult = _lru_gates(x_ref[0], cw_ref, cb_ref, wr_ref, wi_ref, br_ref, bi_ref, lam_ref, pad_ref, s_len)
        gate = g_ref[0]
        hv = h_ref[0]
        dyv = dy_ref[0]
        dg_ref[0] = (dyv * hv * _gelu_grad(gate)).astype(BF16)
        b_s[...] = dyv * _gelu(gate)
        l_s[pl.ds(0, s_len), :] = a
        l_s[pl.ds(s_len, PAD), :] = jnp.zeros((PAD, LANES), F32)
        a_s[...] = l_s[pl.ds(1, s_len), :]
        row = lax.broadcasted_iota(jnp.int32, (8, LANES), 0)

        def tile(t, carry):
            i0 = pl.multiple_of((n_tiles - 1 - t) * 8, 8)
            ac, ll = _tile_scan(a_s[pl.ds(i0, 8), :], b_s[pl.ds(i0, 8), :], row, True)
            lmb = ll + ac * carry
            b_s[pl.ds(i0, 8), :] = lmb
            return jnp.broadcast_to(lmb[0:1, :], (8, LANES))

        lax.fori_loop(0, n_tiles, tile, jnp.zeros((8, LANES), F32))
        lmb = b_s[...]
        l_s[pl.ds(0, PAD), :] = jnp.zeros((PAD, LANES), F32)
        l_s[pl.ds(PAD, s_len), :] = hv
        h_prev = l_s[pl.ds(PAD - 1, s_len), :]
        da = lmb * h_prev
        dmult = lmb * (i * xc)
        di = lmb * mult * xc
        dxc = lmb * mult * i
        dlog_a = da * a - dmult * a * a / mult
        dr = -LRU_C * sp * dlog_a
        dsp = jnp.sum(-LRU_C * r * dlog_a, axis=0, keepdims=True)
        dlam_ref[...] += dsp * (-_sigmoid(-lam_ref[...]))
        dpr = dr * r * (1.0 - r)
        dpi = di * i * (1.0 - i)
        dprb = dpr.astype(BF16)
        dpib = dpi.astype(BF16)
        xcb = xc.astype(BF16)
        dbr_ref[...] += jnp.sum(dpr, axis=0, keepdims=True)
        dbi_ref[...] += jnp.sum(dpi, axis=0, keepdims=True)
        tn = (((0,), (0,)), ((), ()))
        dwr_ref[0] += lax.dot_general(xcb, dprb, tn, preferred_element_type=F32)
        dwi_ref[0] += lax.dot_general(xcb, dpib, tn, preferred_element_type=F32)
        dxc = (dxc + jnp.dot(dprb, wrt_ref[0].astype(BF16), preferred_element_type=F32)
               + jnp.dot(dpib, wit_ref[0].astype(BF16), preferred_element_type=F32))
        dcb_ref[...] += jnp.sum(dxc, axis=0, keepdims=True)
        for j in range(CONV_WIDTH):
            dcw_ref[pl.ds(j, 1), :] += jnp.sum(dxc * pad_ref[pl.ds(PAD - (CONV_WIDTH - 1) + j, s_len), :], axis=0, keepdims=True)
        l_s[pl.ds(0, s_len), :] = dxc
        l_s[pl.ds(s_len, PAD), :] = jnp.zeros((PAD, LANES), F32)
        dx = jnp.zeros((s_len, LANES), F32)
        for j in range(CONV_WIDTH):
            dx = dx + l_s[pl.ds(CONV_WIDTH - 1 - j, s_len), :] * cw_ref[pl.ds(j, 1), :]
        dx_ref[0] = dx.astype(BF16)

    slab = lambda off: pl.BlockSpec((1, s_len, LANES), functools.partial(lambda o, c, b: (b, 0, o + c), off))
    vec = pl.BlockSpec((1, LANES), lambda c, b: (0, c))
    mat = pl.BlockSpec((1, LANES, LANES), lambda c, b: (c, 0, 0))
    cw = pl.BlockSpec((CONV_WIDTH, LANES), lambda c, b: (0, c))
    out = pl.BlockSpec((1, s_len, LANES), lambda c, b: (b, 0, c))
    vshape = jax.ShapeDtypeStruct((1, d), F32)
    mshape = jax.ShapeDtypeStruct((ncb, LANES, LANES), F32)
    return pl.pallas_call(
        body,
        out_shape=(jax.ShapeDtypeStruct((bsz, s_len, d), BF16),) * 2
        + (jax.ShapeDtypeStruct((CONV_WIDTH, d), F32), vshape, vshape, vshape, vshape, mshape, mshape),
        grid=(ncb, bsz),
        in_specs=[out, slab(0), slab(ncb), out, cw, vec, mat, mat, mat, mat, vec, vec, vec],
        out_specs=(out, out, cw, vec, vec, vec, vec, mat, mat),
        scratch_shapes=[pltpu.VMEM((s_len + 2 * PAD, LANES), F32), pltpu.VMEM((s_len, LANES), F32), pltpu.VMEM((s_len, LANES), F32),
                        pltpu.VMEM((s_len + 2 * PAD, LANES), F32)],
        compiler_params=_params(("parallel", "arbitrary")),
        name=name,
    )(dy, proj_f, proj_f, h, conv_w, conv_b.reshape(1, d), wr_bd, wi_bd, wr_bd_t, wi_bd_t,
      b_rg.reshape(1, d), b_ig.reshape(1, d), lam.reshape(1, d))


def _kv_place(head, n_kv_heads):
    kv = head // (N_HEADS // n_kv_heads)
    return kv // 2, kv % 2


def _band_mask(n, single):
    if single:
        qi = lax.broadcasted_iota(jnp.int32, (ATT_BLOCK, ATT_BLOCK), 0)
        return qi >= lax.broadcasted_iota(jnp.int32, (ATT_BLOCK, ATT_BLOCK), 1)
    qi = lax.broadcasted_iota(jnp.int32, (ATT_BLOCK, 2 * ATT_BLOCK), 0)
    kj = lax.broadcasted_iota(jnp.int32, (ATT_BLOCK, 2 * ATT_BLOCK), 1)
    rel = qi + ATT_BLOCK - kj
    return (rel >= 0) & (rel <= ATT_BLOCK) & ((n > 0) | (kj >= ATT_BLOCK))


def _half_masks(dtype):
    lane = lax.broadcasted_iota(jnp.int32, (1, LANES), 1)
    return [(lane < HEAD_DIM).astype(dtype), (lane >= HEAD_DIM).astype(dtype)]


NT = (((1,), (1,)), ((), ()))
TN = (((0,), (0,)), ((), ()))


def _qkv_specs(dil, q_blk, k_blk, v_blk, ckv, clamp):
    qw = D_MODEL // LANES * LANES
    return [
        pl.BlockSpec((1, ATT_BLOCK, qw), lambda b, j, n: (b, clamp(n), j * (q_blk[1]) + q_blk[0])),
        pl.BlockSpec((1, ATT_BLOCK, ckv), lambda b, j, n: (b, jnp.maximum(clamp(n) - 1, 0), j * k_blk[1] + k_blk[0])),
        pl.BlockSpec((1, ATT_BLOCK, ckv), lambda b, j, n: (b, clamp(n), j * k_blk[1] + k_blk[0])),
        pl.BlockSpec((1, ATT_BLOCK, ckv), lambda b, j, n: (b, jnp.maximum(clamp(n) - 1, 0), j * v_blk[1] + v_blk[0])),
        pl.BlockSpec((1, ATT_BLOCK, ckv), lambda b, j, n: (b, clamp(n), j * v_blk[1] + v_blk[0])),
    ]


def attn_fwd(qkv, *, dil, n_kv_heads, sinks, name, emit_bf16=False):
    bsz, s_len, width = qkv.shape
    ckv = n_kv_heads * HEAD_DIM
    l_sub = s_len // dil
    nb = l_sub // ATT_BLOCK
    view = qkv.reshape(bsz, l_sub, dil * width)
    scale = HEAD_DIM ** -0.5
    q_blk = (0, width // D_MODEL)
    k_blk = (D_MODEL // ckv, width // ckv)
    v_blk = (D_MODEL // ckv + 1, width // ckv)
    assert (dil == 1 or width % D_MODEL == 0) and width % ckv == 0 and D_MODEL % ckv == 0

    single = nb == 1

    def body(*refs):
        refs = list(refs)
        sink_ref = refs.pop(0) if sinks is not None else None
        ob_ref = refs.pop() if emit_bf16 else None
        q_ref, kp_ref, kc_ref, vp_ref, vc_ref, o_ref, lse_ref = refs
        n = pl.program_id(2)
        mask = _band_mask(n, single)
        hm = _half_masks(BF16)
        hmf = _half_masks(F32)
        kk = kc_ref[0] if single else jnp.concatenate([kp_ref[0], kc_ref[0]], axis=0)
        vv = vc_ref[0] if single else jnp.concatenate([vp_ref[0], vc_ref[0]], axis=0)
        for hp in range(N_HEADS // 2):
            q2 = q_ref[0, :, hp * LANES:(hp + 1) * LANES]
            o2 = jnp.zeros((ATT_BLOCK, LANES), F32)
            l2 = jnp.zeros((ATT_BLOCK, LANES), F32)
            for a in range(2):
                kb, kh = _kv_place(2 * hp + a, n_kv_heads)
                k2 = kk[:, kb * LANES:(kb + 1) * LANES]
                v2 = vv[:, kb * LANES:(kb + 1) * LANES]
                if kh != a:
                    k2 = pltpu.roll(k2, HEAD_DIM, 1)
                    v2 = pltpu.roll(v2, HEAD_DIM, 1)
                s = lax.dot_general(q2 * hm[a], k2, NT, preferred_element_type=F32) * scale
                s = jnp.where(mask, s, NEG_INF)
                m = jnp.max(s, axis=-1, keepdims=True)
                if sink_ref is not None:
                    sk = sink_ref[2 * hp + a]
                    m = jnp.maximum(m, sk)
                p = jnp.exp(s - m)
                den = jnp.sum(p, axis=-1, keepdims=True)
                if sink_ref is not None:
                    den = den + jnp.exp(sk - m)
                o2 = o2 + jnp.dot(p.astype(BF16), v2 * hm[a], preferred_element_type=F32) / den
                l2 = l2 + (m + jnp.log(den)) * hmf[a]
            o_ref[0, :, hp * LANES:(hp + 1) * LANES] = o2
            lse_ref[0, :, hp * LANES:(hp + 1) * LANES] = l2
            if ob_ref is not None:
                ob_ref[0, :, hp * LANES:(hp + 1) * LANES] = o2.astype(BF16)

    in_specs = _qkv_specs(dil, q_blk, k_blk, v_blk, ckv, lambda n: n)
    args = [view] * 5
    if sinks is not None:
        in_specs = [pl.BlockSpec(memory_space=pltpu.SMEM)] + in_specs
        args = [sinks] + args
    out = pl.BlockSpec((1, ATT_BLOCK, D_MODEL), lambda b, j, n: (b, n, j))
    res = pl.pallas_call(
        body,
        out_shape=(jax.ShapeDtypeStruct((bsz, l_sub, dil * D_MODEL), F32),) * 2
        + ((jax.ShapeDtypeStruct((bsz, l_sub, dil * D_MODEL), BF16),) if emit_bf16 else ()),
        grid=(bsz, dil, nb),
        in_specs=in_specs,
        out_specs=(out,) * (3 if emit_bf16 else 2),
        compiler_params=_params(("parallel", "parallel", "arbitrary")),
        name=name,
    )(*args)
    return tuple(t.reshape(bsz, s_len, D_MODEL) for t in res)


def attn_bwd(qkv, o, lse, do, acc, *, dil, n_kv_heads, name):
    bsz, s_len, width = qkv.shape
    ckv = n_kv_heads * HEAD_DIM
    l_sub = s_len // dil
    nb = l_sub // ATT_BLOCK
    view = qkv.reshape(bsz, l_sub, dil * width)
    scale = HEAD_DIM ** -0.5
    q_blk = (0, width // D_MODEL)
    k_blk = (D_MODEL // ckv, width // ckv)
    v_blk = (D_MODEL // ckv + 1, width // ckv)
    single = nb == 1

    def body(*refs):
        if acc is None:
            q_ref, kp_ref, kc_ref, vp_ref, vc_ref, o_ref, lse_ref, do_ref, dq_ref, dk_ref, dv_ref, dkk, dvv, ck, cv = refs
            aq_ref = ak_ref = av_ref = None
        else:
            (q_ref, kp_ref, kc_ref, vp_ref, vc_ref, o_ref, lse_ref, do_ref, aq_ref, ak_ref, av_ref,
             dq_ref, dk_ref, dv_ref, dkk, dvv, ck, cv) = refs
        n = pl.program_id(2)

        @pl.when(n < nb)
        def _():
            mask = _band_mask(n, single)
            hm = _half_masks(BF16)
            hmf = _half_masks(F32)
            kk = kc_ref[0] if single else jnp.concatenate([kp_ref[0], kc_ref[0]], axis=0)
            vv = vc_ref[0] if single else jnp.concatenate([vp_ref[0], vc_ref[0]], axis=0)
            krows = pl.ds(ATT_BLOCK, ATT_BLOCK) if single else pl.ds(0, 2 * ATT_BLOCK)
            dkk[...] = jnp.zeros_like(dkk)
            dvv[...] = jnp.zeros_like(dvv)
            for hp in range(N_HEADS // 2):
                cols = slice(hp * LANES, (hp + 1) * LANES)
                q2 = q_ref[0, :, cols]
                do2f = do_ref[0, :, cols]
                do2 = do2f.astype(BF16)
                dd2 = do2f * o_ref[0, :, cols]
                l2 = lse_ref[0, :, cols]
                dq2 = jnp.zeros((ATT_BLOCK, LANES), F32)
                for a in range(2):
                    kb, kh = _kv_place(2 * hp + a, n_kv_heads)
                    kcols = slice(kb * LANES, (kb + 1) * LANES)
                    k2 = kk[:, kcols]
                    v2 = vv[:, kcols]
                    if kh != a:
                        k2 = pltpu.roll(k2, HEAD_DIM, 1)
                        v2 = pltpu.roll(v2, HEAD_DIM, 1)
                    qm = q2 * hm[a]
                    dom = do2 * hm[a]
                    dsum = jnp.sum(dd2 * hmf[a], axis=-1, keepdims=True)
                    lse_h = jnp.max(jnp.where(hmf[a] > 0.5, l2, NEG_INF), axis=-1, keepdims=True)
                    s = lax.dot_general(qm, k2, NT, preferred_element_type=F32) * scale
                    s = jnp.where(mask, s, NEG_INF)
                    p = jnp.exp(s - lse_h)
                    dp = lax.dot_general(dom, v2, NT, preferred_element_type=F32)
                    ds = (p * (dp - dsum) * scale).astype(BF16)
                    dq2 = dq2 + jnp.dot(ds, k2 * hm[a], preferred_element_type=F32)
                    dk_c = lax.dot_general(ds, qm, TN, preferred_element_type=F32)
                    dv_c = lax.dot_general(p.astype(BF16), dom, TN, preferred_element_type=F32)
                    if kh != a:
                        dk_c = pltpu.roll(dk_c, HEAD_DIM, 1)
                        dv_c = pltpu.roll(dv_c, HEAD_DIM, 1)
                    dkk[krows, kcols] += dk_c
                    dvv[krows, kcols] += dv_c
                if aq_ref is not None:
                    dq2 = dq2 + aq_ref[0, :, cols]
                dq_ref[0, :, cols] = dq2

        @pl.when((n >= 1) & (n < nb))
        def _():
            dk_ref[0] = ck[...] + dkk[pl.ds(0, ATT_BLOCK), :] + (0.0 if ak_ref is None else ak_ref[0])
            dv_ref[0] = cv[...] + dvv[pl.ds(0, ATT_BLOCK), :] + (0.0 if av_ref is None else av_ref[0])

        @pl.when(n == nb)
        def _():
            dk_ref[0] = ck[...] + (0.0 if ak_ref is None else ak_ref[0])
            dv_ref[0] = cv[...] + (0.0 if av_ref is None else av_ref[0])

        @pl.when(n < nb)
        def _():
            ck[...] = dkk[pl.ds(ATT_BLOCK, ATT_BLOCK), :]
            cv[...] = dvv[pl.ds(ATT_BLOCK, ATT_BLOCK), :]

    clamp = lambda n: jnp.minimum(n, nb - 1)
    prev = lambda n: jnp.maximum(n - 1, 0)
    row = pl.BlockSpec((1, ATT_BLOCK, D_MODEL), lambda b, j, n: (b, clamp(n), j))
    kv_out = pl.BlockSpec((1, ATT_BLOCK, ckv), lambda b, j, n: (b, prev(n), j))
    in_specs = _qkv_specs(dil, q_blk, k_blk, v_blk, ckv, clamp) + [row, row, row]
    rs = lambda t: t.reshape(bsz, l_sub, dil * t.shape[-1])
    args = [view] * 5 + [rs(o), rs(lse), rs(do)]
    if acc is not None:
        in_specs += [row, kv_out, kv_out]
        args += [rs(t) for t in acc]
    dq, dk, dv = pl.pallas_call(
        body,
        out_shape=(jax.ShapeDtypeStruct((bsz, l_sub, dil * D_MODEL), F32),
                   jax.ShapeDtypeStruct((bsz, l_sub, dil * ckv), F32), jax.ShapeDtypeStruct((bsz, l_sub, dil * ckv), F32)),
        grid=(bsz, dil, nb + 1),
        in_specs=in_specs,
        out_specs=(row, kv_out, kv_out),
        scratch_shapes=[pltpu.VMEM((2 * ATT_BLOCK, ckv), F32), pltpu.VMEM((2 * ATT_BLOCK, ckv), F32),
                        pltpu.VMEM((ATT_BLOCK, ckv), F32), pltpu.VMEM((ATT_BLOCK, ckv), F32)],
        compiler_params=_params(("parallel", "parallel", "arbitrary")),
        name=name,
    )(*args)
    return dq.reshape(bsz, s_len, D_MODEL), dk.reshape(bsz, s_len, ckv), dv.reshape(bsz, s_len, ckv)


def dil_combine(os_, lses, *, name):
    t_dim, d = os_[0].shape
    tr = _pick(t_dim, (256, 128, 8))

    def body(o0, o1, o2, l0, l1, l2, y_ref, lt_ref, yb_ref):
        la, lb, lc = l0[...], l1[...], l2[...]
        m = jnp.maximum(jnp.maximum(la, lb), lc)
        ea, eb, ec = jnp.exp(la - m), jnp.exp(lb - m), jnp.exp(lc - m)
        tot = ea + eb + ec
        y = (ea / tot) * o0[...] + (eb / tot) * o1[...] + (ec / tot) * o2[...]
        y_ref[...] = y
        yb_ref[...] = y.astype(BF16)
        lt_ref[...] = m + jnp.log(tot)

    row = pl.BlockSpec((tr, d), lambda i: (i, 0))
    return pl.pallas_call(
        body,
        out_shape=(jax.ShapeDtypeStruct((t_dim, d), F32),) * 2 + (jax.ShapeDtypeStruct((t_dim, d), BF16),),
        grid=(t_dim // tr,),
        in_specs=[row] * 6,
        out_specs=(row, row, row),
        compiler_params=_params(("parallel",)),
        name=name,
    )(*os_, *lses)


ATT_SCALE = HEAD_DIM ** -0.5


def _band_mask(n, single):
    nk = ATT_BLOCK if single else 2 * ATT_BLOCK
    qi = lax.broadcasted_iota(jnp.int32, (2 * ATT_BLOCK, nk), 0) % ATT_BLOCK
    kj = lax.broadcasted_iota(jnp.int32, (2 * ATT_BLOCK, nk), 1)
    if single:
        return qi >= kj
    rel = qi + ATT_BLOCK - kj
    return (rel >= 0) & (rel <= ATT_BLOCK) & ((n > 0) | (kj >= ATT_BLOCK))


def _lane_halves():
    lane = lax.broadcasted_iota(jnp.int32, (1, LANES), 1)
    return lane < HEAD_DIM


def _stack_heads(t2, kh):
    first = _lane_halves()
    parts = []
    for a in range(2):
        ta = jnp.where(first if a == 0 else ~first, t2, jnp.zeros_like(t2))
        if a != kh[a]:
            ta = pltpu.roll(ta, HEAD_DIM, 1)
        parts.append(ta)
    return jnp.concatenate(parts, axis=0)


def _fold_heads(t, kh):
    t0, t1 = t[:ATT_BLOCK], t[ATT_BLOCK:]
    if t.shape[1] == LANES:
        if kh[0] != 0:
            t0 = pltpu.roll(t0, HEAD_DIM, 1)
        if kh[1] != 1:
            t1 = pltpu.roll(t1, HEAD_DIM, 1)
    return jnp.where(_lane_halves(), t0, t1)


def _rows_of_heads(t2):
    return jnp.concatenate([t2[:, 0:1], t2[:, HEAD_DIM:HEAD_DIM + 1]], axis=0)


PAIRS_AT_ONCE = 4


def _fill_bias(bias2_ref, bias1_ref=None):
    for i in range(2):
        bias2_ref[i] = jnp.where(_band_mask(i, False), 0.0, NEG_INF)
    if bias1_ref is not None:
        bias1_ref[...] = jnp.where(_band_mask(0, True), 0.0, NEG_INF)


def _pairs_fwd(items):
    ss = [lax.dot_general(_stack_heads(q2 * ATT_SCALE, kh), kk, NT, preferred_element_type=F32) + bias
          for q2, kk, _, bias, kh, _ in items]
    ps, ms, ls = [], [], []
    for s, (_, _, _, _, _, sink_col) in zip(ss, items):
        m = jnp.max(s, axis=-1, keepdims=True)
        if sink_col is not None:
            m = jnp.maximum(m, sink_col)
        p = jnp.exp(s - m)
        l = jnp.sum(p, axis=-1, keepdims=True)
        if sink_col is not None:
            l = l + jnp.exp(sink_col - m)
        ps.append(p.astype(BF16))
        ms.append(m)
        ls.append(l)
    pvs = [jnp.dot(p, it[2], preferred_element_type=F32) for p, it in zip(ps, items)]
    return list(zip(pvs, ms, ls))


def _pairs_bwd(items):
    first = _lane_halves()
    pre = []
    for q2, kk, vv, do2, o2, lse2, bias, kh in items:
        dd = do2 * o2
        dsum = jnp.concatenate([jnp.sum(jnp.where(first, dd, 0.0), axis=-1, keepdims=True),
                                jnp.sum(jnp.where(first, 0.0, dd), axis=-1, keepdims=True)], axis=0)
        qs = _stack_heads(q2 * ATT_SCALE, kh)
        dos = _stack_heads(do2.astype(BF16), kh)
        s = lax.dot_general(qs, kk, NT, preferred_element_type=F32) + bias
        dp = lax.dot_general(dos, vv, NT, preferred_element_type=F32)
        pre.append((qs, dos, s, dp, dsum))
    mid = []
    for (qs, dos, s, dp, dsum), it in zip(pre, items):
        p = jnp.exp(s - _rows_of_heads(it[5]))
        mid.append((p.astype(BF16), (p * (dp - dsum)).astype(BF16)))
    out = []
    for (pb, ds), (qs, dos, _, _, _), it in zip(mid, pre, items):
        dq = _fold_heads(jnp.dot(ds, it[1], preferred_element_type=F32), it[7]) * ATT_SCALE
        dk = lax.dot_general(ds, qs, TN, preferred_element_type=F32)
        dv = lax.dot_general(pb, dos, TN, preferred_element_type=F32)
        out.append((dq, dk, dv))
    return out


def swa_fwd(qkv, sinks, *, name):
    bsz, s_len, width = qkv.shape
    ckv = SWA_KV_HEADS * HEAD_DIM
    nb = s_len // ATT_BLOCK
    kblk = D_MODEL // ckv

    def body(sink_ref, q_ref, kp_ref, kc_ref, vp_ref, vc_ref, o_ref, lse_ref, ob_ref, bias2):
        n = pl.program_id(1)
        _fill_bias(bias2)
        bias = bias2[jnp.minimum(n, 1)]
        kk = jnp.concatenate([kp_ref[0], kc_ref[0]], axis=0)
        vv = jnp.concatenate([vp_ref[0], vc_ref[0]], axis=0)
        top = lax.broadcasted_iota(jnp.int32, (2 * ATT_BLOCK, 1), 0) < ATT_BLOCK
        for hp0 in range(0, N_HEADS // 2, PAIRS_AT_ONCE):
            items, places = [], []
            for hp in range(hp0, hp0 + PAIRS_AT_ONCE):
                cols = slice(hp * LANES, (hp + 1) * LANES)
                kb, kh = _kv_place(2 * hp, SWA_KV_HEADS)
                kcols = slice(kb * LANES, (kb + 1) * LANES)
                sink_col = jnp.where(top, sink_ref[2 * hp], sink_ref[2 * hp + 1])
                items.append((q_ref[0, :, cols], kk[:, kcols], vv[:, kcols], bias, (kh, kh), sink_col))
                places.append((cols, (kh, kh)))
            for (pv, m, l), (cols, kh2) in zip(_pairs_fwd(items), places):
                o2 = _fold_heads(pv / l, kh2)
                o_ref[0, :, cols] = o2
                ob_ref[0, :, cols] = o2.astype(BF16)
                lse_ref[0, :, cols] = _fold_heads(m + jnp.log(l), kh2)

    prev = lambda n: jnp.maximum(n - 1, 0)
    out = pl.BlockSpec((1, ATT_BLOCK, D_MODEL), lambda b, n: (b, n, 0))
    sd = lambda dt: jax.ShapeDtypeStruct((bsz, s_len, D_MODEL), dt)
    return pl.pallas_call(
        body,
        out_shape=(sd(F32), sd(F32), sd(BF16)),
        grid=(bsz, nb),
        in_specs=[pl.BlockSpec(memory_space=pltpu.SMEM), out,
                  pl.BlockSpec((1, ATT_BLOCK, ckv), lambda b, n: (b, prev(n), kblk)),
                  pl.BlockSpec((1, ATT_BLOCK, ckv), lambda b, n: (b, n, kblk)),
                  pl.BlockSpec((1, ATT_BLOCK, ckv), lambda b, n: (b, prev(n), kblk + 1)),
                  pl.BlockSpec((1, ATT_BLOCK, ckv), lambda b, n: (b, n, kblk + 1))],
        out_specs=(out, out, out),
        scratch_shapes=[pltpu.VMEM((2, 2 * ATT_BLOCK, 2 * ATT_BLOCK), F32)],
        compiler_params=_params(("parallel", "arbitrary")),
        name=name,
    )(sinks, qkv, qkv, qkv, qkv, qkv)


def swa_bwd(qkv, o, lse, do, *, name):
    bsz, s_len, width = qkv.shape
    ckv = SWA_KV_HEADS * HEAD_DIM
    nb = s_len // ATT_BLOCK
    kblk = D_MODEL // ckv

    def body(q_ref, kp_ref, kc_ref, vp_ref, vc_ref, o_ref, lse_ref, do_ref, dq_ref, dk_ref, dv_ref, dkk, dvv, ck, cv,
             bias2):
        n = pl.program_id(1)

        @pl.when(n < nb)
        def _():
            _fill_bias(bias2)
            bias = bias2[jnp.minimum(n, 1)]
            kk = jnp.concatenate([kp_ref[0], kc_ref[0]], axis=0)
            vv = jnp.concatenate([vp_ref[0], vc_ref[0]], axis=0)
            dkk[...] = jnp.zeros_like(dkk)
            dvv[...] = jnp.zeros_like(dvv)
            for hp0 in range(0, N_HEADS // 2, PAIRS_AT_ONCE):
                items, places = [], []
                for hp in range(hp0, hp0 + PAIRS_AT_ONCE):
                    cols = slice(hp * LANES, (hp + 1) * LANES)
                    kb, kh = _kv_place(2 * hp, SWA_KV_HEADS)
                    kcols = slice(kb * LANES, (kb + 1) * LANES)
                    items.append((q_ref[0, :, cols], kk[:, kcols], vv[:, kcols], do_ref[0, :, cols], o_ref[0, :, cols],
                                  lse_ref[0, :, cols], bias, (kh, kh)))
                    places.append((cols, kcols))
                for (dq, dk, dv), (cols, kcols) in zip(_pairs_bwd(items), places):
                    dq_ref[0, :, cols] = dq
                    dkk[:, kcols] += dk
                    dvv[:, kcols] += dv

        @pl.when((n >= 1) & (n < nb))
        def _():
            dk_ref[0] = ck[...] + dkk[pl.ds(0, ATT_BLOCK), :]
            dv_ref[0] = cv[...] + dvv[pl.ds(0, ATT_BLOCK), :]

        @pl.when(n == nb)
        def _():
            dk_ref[0] = ck[...]
            dv_ref[0] = cv[...]

        @pl.when(n < nb)
        def _():
            ck[...] = dkk[pl.ds(ATT_BLOCK, ATT_BLOCK), :]
            cv[...] = dvv[pl.ds(ATT_BLOCK, ATT_BLOCK), :]

    clamp = lambda n: jnp.minimum(n, nb - 1)
    prev = lambda n: jnp.maximum(n - 1, 0)
    row = pl.BlockSpec((1, ATT_BLOCK, D_MODEL), lambda b, n: (b, clamp(n), 0))
    kv_out = pl.BlockSpec((1, ATT_BLOCK, ckv), lambda b, n: (b, prev(n), 0))
    return pl.pallas_call(
        body,
        out_shape=(jax.ShapeDtypeStruct((bsz, s_len, D_MODEL), F32), jax.ShapeDtypeStruct((bsz, s_len, ckv), F32),
                   jax.ShapeDtypeStruct((bsz, s_len, ckv), F32)),
        grid=(bsz, nb + 1),
        in_specs=[row,
                  pl.BlockSpec((1, ATT_BLOCK, ckv), lambda b, n: (b, prev(clamp(n)), kblk)),
                  pl.BlockSpec((1, ATT_BLOCK, ckv), lambda b, n: (b, clamp(n), kblk)),
                  pl.BlockSpec((1, ATT_BLOCK, ckv), lambda b, n: (b, prev(clamp(n)), kblk + 1)),
                  pl.BlockSpec((1, ATT_BLOCK, ckv), lambda b, n: (b, clamp(n), kblk + 1)),
                  row, row, row],
        out_specs=(row, kv_out, kv_out),
        scratch_shapes=[pltpu.VMEM((2 * ATT_BLOCK, ckv), F32), pltpu.VMEM((2 * ATT_BLOCK, ckv), F32),
                        pltpu.VMEM((ATT_BLOCK, ckv), F32), pltpu.VMEM((ATT_BLOCK, ckv), F32),
                        pltpu.VMEM((2, 2 * ATT_BLOCK, 2 * ATT_BLOCK), F32)],
        compiler_params=_params(("parallel", "arbitrary")),
        name=name,
    )(qkv, qkv, qkv, qkv, qkv, o, lse, do)


DIL_PATTERNS = tuple((d, 2048 // d // ATT_BLOCK) for d in DILATIONS)
MHA = (0, 1)


def _dil_rows(idx, d, nb):
    j = idx // nb
    n = idx % nb
    base = j + n * (ATT_BLOCK * d)
    prev = jnp.maximum(base - ATT_BLOCK * d, j)
    if d == 1:
        return n, pl.ds(pl.multiple_of(base, ATT_BLOCK), ATT_BLOCK), pl.ds(pl.multiple_of(prev, ATT_BLOCK), ATT_BLOCK)
    return n, pl.ds(base, ATT_BLOCK, stride=d), pl.ds(prev, ATT_BLOCK, stride=d)


def dil_fwd(qkv, *, name):
    bsz, s_len, _ = qkv.shape
    assert s_len == DIL_PATTERNS[0][0] * DIL_PATTERNS[0][1] * ATT_BLOCK
    npair = N_HEADS // 2

    def body(q_ref, k_ref, v_ref, y_ref, lse_ref, yb_ref, m_acc, l_acc, bias2, bias1):
        _fill_bias(bias2, bias1)
        for ci, (d, nb) in enumerate(DIL_PATTERNS):
            single = nb == 1

            def blocks(it, carry):
                items, places = [], []
                for u in range(PAIRS_AT_ONCE):
                    n, rows, prows = _dil_rows(it * PAIRS_AT_ONCE + u, d, nb)
                    kc = k_ref[rows, :].astype(BF16)
                    vc = v_ref[rows, :].astype(BF16)
                    if single:
                        kk, vv, bias = kc, vc, bias1[...]
                    else:
                        kk = jnp.concatenate([k_ref[prows, :].astype(BF16), kc], axis=0)
                        vv = jnp.concatenate([v_ref[prows, :].astype(BF16), vc], axis=0)
                        bias = bias2[jnp.minimum(n, 1)]
                    items.append((q_ref[rows, :].astype(BF16), kk, vv, bias, MHA, None))
                    places.append(rows)
                for (pv, m, l), rows in zip(_pairs_fwd(items), places):
                    o2, m2, l2 = _fold_heads(pv, MHA), _fold_heads(m, MHA), _fold_heads(l, MHA)
                    if ci == 0:
                        y_ref[rows, :] = o2
                        m_acc[rows, :] = m2
                        l_acc[rows, :] = l2
                    else:
                        m_old = m_acc[rows, :]
                        m_new = jnp.maximum(m_old, m2)
                        w_old = jnp.exp(m_old - m_new)
                        w_new = jnp.exp(m2 - m_new)
                        y_ref[rows, :] = y_ref[rows, :] * w_old + o2 * w_new
                        l_acc[rows, :] = l_acc[rows, :] * w_old + l2 * w_new
                        m_acc[rows, :] = m_new
                return carry

            lax.fori_loop(0, d * nb // PAIRS_AT_ONCE, blocks, 0)
        y = y_ref[...] / l_acc[...]
        y_ref[...] = y
        yb_ref[...] = y.astype(BF16)
        lse_ref[...] = m_acc[...] + jnp.log(l_acc[...])

    slab = lambda off: pl.BlockSpec((None, s_len, LANES), functools.partial(lambda o, b, h: (b, 0, o + h), off))
    sd = lambda dt: jax.ShapeDtypeStruct((bsz, s_len, D_MODEL), dt)
    return pl.pallas_call(
        body,
        out_shape=(sd(F32), sd(F32), sd(BF16)),
        grid=(bsz, npair),
        in_specs=[slab(0), slab(npair), slab(2 * npair)],
        out_specs=(slab(0), slab(0), slab(0)),
        scratch_shapes=[pltpu.VMEM((s_len, LANES), F32), pltpu.VMEM((s_len, LANES), F32),
                        pltpu.VMEM((2, 2 * ATT_BLOCK, 2 * ATT_BLOCK), F32), pltpu.VMEM((2 * ATT_BLOCK, ATT_BLOCK), F32)],
        compiler_params=_params(("parallel", "parallel")),
        name=name,
    )(qkv, qkv, qkv)


def dil_bwd(qkv, y, lse, dy, *, name):
    bsz, s_len, _ = qkv.shape
    npair = N_HEADS // 2

    def body(q_ref, k_ref, v_ref, y_ref, lse_ref, dy_ref, dq_ref, dk_ref, dv_ref, bias2, bias1):
        _fill_bias(bias2, bias1)
        dq_ref[...] = jnp.zeros_like(dq_ref)
        dk_ref[...] = jnp.zeros_like(dk_ref)
        dv_ref[...] = jnp.zeros_like(dv_ref)
        for d, nb in DIL_PATTERNS:
            single = nb == 1

            def blocks(it, carry):
                items, places = [], []
                for u in range(PAIRS_AT_ONCE):
                    n, rows, prows = _dil_rows(it * PAIRS_AT_ONCE + u, d, nb)
                    kc = k_ref[rows, :].astype(BF16)
                    vc = v_ref[rows, :].astype(BF16)
                    if single:
                        kk, vv, bias = kc, vc, bias1[...]
                    else:
                        kk = jnp.concatenate([k_ref[prows, :].astype(BF16), kc], axis=0)
                        vv = jnp.concatenate([v_ref[prows, :].astype(BF16), vc], axis=0)
                        bias = bias2[jnp.minimum(n, 1)]
                    items.append((q_ref[rows, :].astype(BF16), kk, vv, dy_ref[rows, :], y_ref[rows, :], lse_ref[rows, :], bias, MHA))
                    places.append((rows, prows))
                for (dq, dk, dv), (rows, prows) in zip(_pairs_bwd(items), places):
                    dq_ref[rows, :] += dq
                    if single:
                        dk_ref[rows, :] += dk
                        dv_ref[rows, :] += dv
                    else:
                        dk_ref[prows, :] += dk[:ATT_BLOCK]
                        dv_ref[prows, :] += dv[:ATT_BLOCK]
                        dk_ref[rows, :] += dk[ATT_BLOCK:]
                        dv_ref[rows, :] += dv[ATT_BLOCK:]
                return carry

            lax.fori_loop(0, d * nb // PAIRS_AT_ONCE, blocks, 0)

    slab = lambda off: pl.BlockSpec((None, s_len, LANES), functools.partial(lambda o, b, h: (b, 0, o + h), off))
    sd = jax.ShapeDtypeStruct((bsz, s_len, D_MODEL), F32)
    return pl.pallas_call(
        body,
        out_shape=(sd, sd, sd),
        grid=(bsz, npair),
        in_specs=[slab(0), slab(npair), slab(2 * npair), slab(0), slab(0), slab(0)],
        out_specs=(slab(0), slab(0), slab(0)),
        scratch_shapes=[pltpu.VMEM((2, 2 * ATT_BLOCK, 2 * ATT_BLOCK), F32), pltpu.VMEM((2 * ATT_BLOCK, ATT_BLOCK), F32)],
        compiler_params=_params(("parallel", "parallel")),
        name=name,
    )(qkv, qkv, qkv, y, lse, dy)


def sink_grad(do, o, lse, sink_lanes, *, name):
    t_dim, d = do.shape
    tr = _pick(t_dim, (256, 128, 8))

    def body(do_ref, o_ref, l_ref, s_ref, out_ref):
        @pl.when(pl.program_id(0) == 0)
        def _():
            out_ref[...] = jnp.zeros_like(out_ref)

        out_ref[...] += jnp.sum(-jnp.exp(s_ref[...] - l_ref[...]) * do_ref[...] * o_ref[...], axis=0, keepdims=True)

    row = pl.BlockSpec((tr, d), lambda i: (i, 0))
    vec = pl.BlockSpec((1, d), lambda i: (0, 0))
    return pl.pallas_call(
        body,
        out_shape=jax.ShapeDtypeStruct((1, d), F32),
        grid=(t_dim // tr,),
        in_specs=[row, row, row, vec],
        out_specs=vec,
        compiler_params=_params(("arbitrary",)),
        name=name,
    )(do, o, lse, sink_lanes)


def adamw(w, g, m, v, *, name):
    rows, cols = w.shape
    tr = _pick(rows, (256, 128, 64, 32, 16, 8))

    def body(w_ref, g_ref, m_ref, v_ref, d_ref, nm_ref, nv_ref):
        gv = g_ref[...]
        nm = ADAM_B1 * m_ref[...] + (1.0 - ADAM_B1) * gv
        nv = ADAM_B2 * v_ref[...] + (1.0 - ADAM_B2) * (gv * gv)
        m_hat = nm / (1.0 - ADAM_B1 ** ADAM_STEP)
        v_hat = nv / (1.0 - ADAM_B2 ** ADAM_STEP)
        d_ref[...] = -ADAM_LR * (m_hat / (jnp.sqrt(v_hat) + ADAM_EPS) + ADAM_WD * w_ref[...])
        nm_ref[...] = nm
        nv_ref[...] = nv

    row = pl.BlockSpec((tr, cols), lambda i: (i, 0))
    return pl.pallas_call(
        body,
        out_shape=(jax.ShapeDtypeStruct((rows, cols), F32),) * 3,
        grid=(rows // tr,),
        in_specs=[row] * 4,
        out_specs=(row, row, row),
        compiler_params=_params(("parallel",)),
        name=name,
    )(w, g, m, v)


def _place():
    return lax.axis_index("x"), lax.axis_index("y"), lax.axis_index("c")


def all_gather(x, *, name):
    def body(x_ref, out_ref, send_sems, recv_sems, local_sem):
        x, y, c = _place()
        me, sibling = (x, y, c), (x, y, 1 - c)
        chips = [(1 - x, y), (x, 1 - y), (1 - x, 1 - y)]

        def slot(px, py, pc):
            return out_ref.at[4 * px + 2 * py + pc]

        def copy(k, block, to, src=None):
            return pltpu.make_async_remote_copy(
                src_ref=slot(*block) if src is None else src, dst_ref=slot(*block),
                send_sem=send_sems.at[k], recv_sem=recv_sems.at[k], device_id=to, device_id_type=MESH)

        mine = pltpu.make_async_copy(x_ref, slot(*me), local_sem)
        mine.start()
        first = [copy(0, me, sibling, src=x_ref)]
        first += [copy(1 + j, me, (*chip, c), src=x_ref) for j, chip in enumerate(chips)]
        for cp in first:
            cp.start()
        passed = [copy(4 + j, (*chip, c), sibling) for j, chip in enumerate(chips)]
        for j, chip in enumerate(chips):
            copy(1 + j, (*chip, c), me).wait_recv()
            passed[j].start()
        copy(0, sibling, me).wait_recv()
        for j, chip in enumerate(chips):
            copy(4 + j, (*chip, 1 - c), me).wait_recv()
        for cp in first + passed:
            cp.wait_send()
        mine.wait()

    return pl.pallas_call(
        body,
        out_shape=jax.ShapeDtypeStruct((N_DEV,) + x.shape, x.dtype),
        in_specs=[pl.BlockSpec(memory_space=pl.ANY)],
        out_specs=pl.BlockSpec(memory_space=pl.ANY),
        scratch_shapes=[pltpu.SemaphoreType.DMA((7,)), pltpu.SemaphoreType.DMA((7,)), pltpu.SemaphoreType.DMA(())],
        name=name,
    )(x)


def all_to_all(x, *, name):
    def body(x_ref, out_ref, send_sems, recv_sems, local_sem):
        x, y, c = _place()
        me = 4 * x + 2 * y + c
        mine = pltpu.make_async_copy(x_ref.at[me], out_ref.at[me], local_sem)
        mine.start()
        copies = []
        for k in range(1, N_DEV):
            px = 1 - x if k & 4 else x
            py = 1 - y if k & 2 else y
            pc = 1 - c if k & 1 else c
            peer = 4 * px + 2 * py + pc
            copies.append(pltpu.make_async_remote_copy(
                src_ref=x_ref.at[peer], dst_ref=out_ref.at[me], send_sem=send_sems.at[k - 1], recv_sem=recv_sems.at[k - 1],
                device_id=(px, py, pc), device_id_type=MESH))
        for cp in copies:
            cp.start()
        for cp in copies:
            cp.wait_recv()
        for cp in copies:
            cp.wait_send()
        mine.wait()

    return pl.pallas_call(
        body,
        out_shape=jax.ShapeDtypeStruct(x.shape, x.dtype),
        in_specs=[pl.BlockSpec(memory_space=pl.ANY)],
        out_specs=pl.BlockSpec(memory_space=pl.ANY),
        scratch_shapes=[pltpu.SemaphoreType.DMA((7,)), pltpu.SemaphoreType.DMA((7,)), pltpu.SemaphoreType.DMA(())],
        name=name,
    )(x)


def sum_slots(x, *, name):
    _, rows, cols = x.shape
    tr = _pick(rows, (512, 256, 128, 64, 32, 16))

    def body(x_ref, o_ref):
        acc = x_ref[0].astype(F32)
        for k in range(1, N_DEV):
            acc = acc + x_ref[k].astype(F32)
        o_ref[...] = acc

    return pl.pallas_call(
        body,
        out_shape=jax.ShapeDtypeStruct((rows, cols), F32),
        grid=(rows // tr,),
        in_specs=[pl.BlockSpec((N_DEV, tr, cols), lambda i: (0, i, 0))],
        out_specs=pl.BlockSpec((tr, cols), lambda i: (i, 0)),
        compiler_params=_params(("parallel",)),
        name=name,
    )(x)


BIG = ("w_in", "w_branch", "w_out", "w_ffn_in", "w_ffn_out")
SMALL = ("conv_b", "w_rg", "b_rg", "w_ig", "b_ig", "lru_lambda", "sinks", "ln1_g", "ln1_b", "ln2_g", "ln2_b")
N_LRU_BLOCKS = D_MODEL // HEAD_DIM
SMALL_ROWS_TILE = 512


def _block_diag(w):
    z = jnp.zeros((N_LRU_BLOCKS // 2, HEAD_DIM, HEAD_DIM), w.dtype)
    top = jnp.concatenate([w[0::2], z], axis=2)
    bot = jnp.concatenate([z, w[1::2]], axis=2)
    return jnp.concatenate([top, bot], axis=1)


def _block_diag_grad(g):
    return jnp.stack([g[:, :HEAD_DIM, :HEAD_DIM], g[:, HEAD_DIM:, HEAD_DIM:]], axis=1).reshape(N_LRU_BLOCKS, HEAD_DIM, HEAD_DIM)


def layer_fwd(x, xb, p, bsz):
    t_dim = x.shape[0]
    s_len = t_dim // bsz
    w_f, w_qs, w_qd = p["w_in_f"], p["w_in_qs"], p["w_in_qd"]
    proj_f = matmul(xb, w_f, name="proj_f")
    qs = matmul(xb, w_qs, out_dtype=BF16, name="proj_qs").reshape(bsz, s_len, W_QS)
    qd = matmul(xb, w_qd, name="proj_qd").reshape(bsz, s_len, W_QD)
    proj_f3 = proj_f.reshape(bsz, s_len, W_F)
    wr_bd, wi_bd = _block_diag(p["w_rg"]), _block_diag(p["w_ig"])
    y_a, h = lru_fwd(proj_f3, p["conv_w"], p["conv_b"], wr_bd, wi_bd, p["b_rg"], p["b_ig"], p["lru_lambda"], name="lru_fwd")
    y_b, lse_b, y_bb = swa_fwd(qs, p["sinks"], name="swa_fwd")
    y_c, lse_c, y_cb = dil_fwd(qd, name="dil_fwd")
    ys = [t.reshape(t_dim, D_MODEL) for t in (y_a, y_bb, y_cb)]
    br = [matmul(ys[n], p["w_branch"][n], name="branch") for n in range(3)]
    merged = merge_fwd(proj_f, br, name="merge_fwd")
    mix = matmul(merged, p["w_out"], name="w_out")
    x1, x1b, z1 = ln_fwd(x, mix, p["ln1_g"], p["ln1_b"], name="ln_fwd")
    h13 = matmul(x1b, p["w_ffn_in"], name="ffn_in")
    act = swiglu_fwd(h13, name="swiglu_fwd")
    ffn = matmul(act, p["w_ffn_out"], name="ffn_out")
    x2, x2b, z2 = ln_fwd(x1, ffn, p["ln2_g"], p["ln2_b"], name="ln_fwd")
    saved = dict(xb=xb, proj_f=proj_f, qs=qs, qd=qd, h=h, ys=ys, y_b=y_b, y_c=y_c, lse_b=lse_b, lse_c=lse_c, br=br, merged=merged,
                 z1=z1, x1b=x1b, h13=h13, act=act, z2=z2, wr_bd=wr_bd, wi_bd=wi_bd)
    return x2, x2b, saved


def layer_bwd(dx2, p, s, bsz):
    t_dim = dx2.shape[0]
    s_len = t_dim // bsz
    g = {}
    dz2, dz2b, g["ln2_g"], g["ln2_b"] = ln_bwd(dx2, s["z2"], p["ln2_g"], name="ln_bwd")
    dact = matmul(dz2b, p["w_ffn_out"], trans_b=True, name="d_act")
    dh13 = swiglu_bwd(dact, s["h13"], name="swiglu_bwd")
    g["w_ffn_out"] = matmul(s["act"], dz2b, trans_a=True, name="dw_ffn_out")
    g["w_ffn_in"] = matmul(s["x1b"], dh13, trans_a=True, name="dw_ffn_in")
    dx1 = matmul(dh13, p["w_ffn_in"], trans_b=True, add=dz2, add_scale=ALPHA, name="dx_ffn")
    dz1, dz1b, g["ln1_g"], g["ln1_b"] = ln_bwd(dx1, s["z1"], p["ln1_g"], name="ln_bwd")
    dmerged = matmul(dz1b, p["w_out"], trans_b=True, name="d_merged")
    g["w_out"] = matmul(s["merged"], dz1b, trans_a=True, name="dw_out")
    *dbr, dgates = merge_bwd(dmerged, s["proj_f"], s["br"], name="merge_bwd")
    dys = [matmul(dbr[n], p["w_branch"][n], trans_b=True, name="d_branch") for n in range(3)]
    g["w_branch"] = jnp.stack([matmul(s["ys"][n], dbr[n], trans_a=True, name="dw_branch") for n in range(3)])
    shape3 = (bsz, s_len, D_MODEL)
    (dlx, dlg, g["conv_w"], g["conv_b"], g["b_rg"], g["b_ig"], g["lru_lambda"], dwr, dwi) = lru_bwd(
        dys[0].reshape(shape3), s["proj_f"].reshape(bsz, s_len, W_F), s["h"], p["conv_w"], p["conv_b"], s["wr_bd"], s["wi_bd"],
        jnp.swapaxes(s["wr_bd"], 1, 2), jnp.swapaxes(s["wi_bd"], 1, 2), p["b_rg"], p["b_ig"], p["lru_lambda"], name="lru_bwd")
    g["w_rg"], g["w_ig"] = _block_diag_grad(dwr), _block_diag_grad(dwi)
    dy_b3 = dys[1].reshape(shape3)
    dqs = swa_bwd(s["qs"], s["y_b"], s["lse_b"], dy_b3, name="swa_bwd")
    sink_lanes = jnp.repeat(p["sinks"], HEAD_DIM).reshape(1, D_MODEL)
    g["sinks"] = sink_grad(dys[1], s["y_b"].reshape(t_dim, D_MODEL), s["lse_b"].reshape(t_dim, D_MODEL), sink_lanes,
                           name="sink_grad").reshape(N_HEADS, HEAD_DIM).sum(axis=1)
    dqd = dil_bwd(s["qd"], s["y_c"], s["lse_c"], dys[2].reshape(shape3), name="dil_bwd")
    flat = lambda t: t.reshape(t_dim, t.shape[-1])
    dproj_f = jnp.concatenate([flat(dlx), flat(dlg), dgates], axis=1)
    dproj_qs = jnp.concatenate([flat(t) for t in dqs], axis=1).astype(BF16)
    dproj_qd = jnp.concatenate([flat(t) for t in dqd], axis=1).astype(BF16)
    g["w_in_f"] = matmul(s["xb"], dproj_f, trans_a=True, name="dw_in_f")
    g["w_in_qs"] = matmul(s["xb"], dproj_qs, trans_a=True, name="dw_in_qs")
    g["w_in_qd"] = matmul(s["xb"], dproj_qd, trans_a=True, name="dw_in_qd")
    dx = matmul(dproj_f, p["w_in_f"], trans_b=True, add=dz1, add_scale=ALPHA, name="dx_f")
    dx = matmul(dproj_qs, p["w_in_qs"], trans_b=True, add=dx, name="dx_qs")
    dx = matmul(dproj_qd, p["w_in_qd"], trans_b=True, add=dx, name="dx_qd")
    g = {k: (v.reshape(p[k].shape) if k in p else v) for k, v in g.items()}
    return dx, g


def local_step(x, target, params):
    bsz, s_len, d = x.shape
    t_dim = bsz * s_len
    xf = x.reshape(t_dim, d)
    xb = xf.astype(BF16)
    saved = []
    for l in range(DEPTH):
        xf, xb, s = layer_fwd(xf, xb, {k: v[l] for k, v in params.items()}, bsz)
        saved.append(s)
    dy, sq = loss_head(xf, target.reshape(t_dim, d), name="loss_head")
    loss = 0.5 * jnp.sum(sq) / d
    grads = [None] * DEPTH
    for l in reversed(range(DEPTH)):
        dy, grads[l] = layer_bwd(dy, {k: v[l] for k, v in params.items()}, saved[l], bsz)
    grads = {k: jnp.stack([grads[l][k] for l in range(DEPTH)]) for k in grads[0]}
    return loss, dy.reshape(bsz, s_len, d), grads


W_IN_SEGMENTS = (("w_in_f", 0, 0, 2 * D_MODEL), ("w_in_qs", 0, 2 * D_MODEL, W_QS), ("w_in_qd", 0, 2 * D_MODEL + W_QS, W_QD),
                 ("w_in_f", 2 * D_MODEL, 2 * D_MODEL + W_QS + W_QD, 3 * D_MODEL))
ROW_SHARDED = ("w_branch", "w_out", "w_ffn_out")


def _cols_of_shards(shards, lo, hi):
    width = shards[0].shape[-1]
    parts = []
    for k, sh in enumerate(shards):
        a, b = max(lo, k * width), min(hi, (k + 1) * width)
        if a < b:
            parts.append(sh[..., a - k * width:b - k * width])
    return parts[0] if len(parts) == 1 else jnp.concatenate(parts, axis=-1)


def _cols_of_w_in(pieces, lo, hi):
    parts = []
    for name, p0, l0, width in W_IN_SEGMENTS:
        a, b = max(lo, l0), min(hi, l0 + width)
        if a < b:
            parts.append(pieces[name][..., p0 + a - l0:p0 + b - l0])
    return parts[0] if len(parts) == 1 else jnp.concatenate(parts, axis=-1)


def _rows_to_full(name, t, shard_shape):
    t = t.reshape((N_DEV,) + shard_shape)
    if name == "w_branch":
        return jnp.transpose(t, (1, 2, 0, 3, 4)).reshape(shard_shape[0], 3, -1, D_MODEL)
    return jnp.transpose(t, (1, 0, 2, 3)).reshape(shard_shape[0], -1, D_MODEL)


def _full_to_rows(name, t):
    if name == "w_branch":
        t = jnp.transpose(t.reshape(DEPTH, 3, N_DEV, -1, D_MODEL), (2, 0, 1, 3, 4))
    else:
        t = jnp.transpose(t.reshape(DEPTH, N_DEV, -1, D_MODEL), (1, 0, 2, 3))
    return t.reshape(N_DEV, -1, D_MODEL)


def _pad_rows(flat, tile_rows):
    n = flat.shape[0]
    per = tile_rows * LANES
    total = -(-n // per) * per
    return jnp.pad(flat, (0, total - n)).reshape(-1, LANES)


def kernel(x, w_in, conv_w, conv_b, w_rg, b_rg, w_ig, b_ig, lru_lambda, sinks, w_branch, w_out, ln1_g, ln1_b, w_ffn_in, w_ffn_out, ln2_g, ln2_b, loss_target, m_w_in, m_conv_w, m_conv_b, m_w_rg, m_b_rg, m_w_ig, m_b_ig, m_lru_lambda, m_sinks, m_w_branch, m_w_out, m_ln1_g, m_ln1_b, m_w_ffn_in, m_w_ffn_out, m_ln2_g, m_ln2_b, v_w_in, v_conv_w, v_conv_b, v_w_rg, v_b_rg, v_w_ig, v_b_ig, v_lru_lambda, v_sinks, v_w_branch, v_w_out, v_ln1_g, v_ln1_b, v_w_ffn_in, v_w_ffn_out, v_ln2_g, v_ln2_b):
    w = dict(w_in=w_in, conv_w=conv_w, conv_b=conv_b, w_rg=w_rg, b_rg=b_rg, w_ig=w_ig, b_ig=b_ig, lru_lambda=lru_lambda, sinks=sinks,
             w_branch=w_branch, w_out=w_out, ln1_g=ln1_g, ln1_b=ln1_b, w_ffn_in=w_ffn_in, w_ffn_out=w_ffn_out, ln2_g=ln2_g, ln2_b=ln2_b)
    m = dict(w_in=m_w_in, conv_w=m_conv_w, conv_b=m_conv_b, w_rg=m_w_rg, b_rg=m_b_rg, w_ig=m_w_ig, b_ig=m_b_ig, lru_lambda=m_lru_lambda,
             sinks=m_sinks, w_branch=m_w_branch, w_out=m_w_out, ln1_g=m_ln1_g, ln1_b=m_ln1_b, w_ffn_in=m_w_ffn_in, w_ffn_out=m_w_ffn_out,
             ln2_g=m_ln2_g, ln2_b=m_ln2_b)
    v = dict(w_in=v_w_in, conv_w=v_conv_w, conv_b=v_conv_b, w_rg=v_w_rg, b_rg=v_b_rg, w_ig=v_w_ig, b_ig=v_b_ig, lru_lambda=v_lru_lambda,
             sinks=v_sinks, w_branch=v_w_branch, w_out=v_w_out, ln1_g=v_ln1_g, ln1_b=v_ln1_b, w_ffn_in=v_w_ffn_in, w_ffn_out=v_w_ffn_out,
             ln2_g=v_ln2_g, ln2_b=v_ln2_b)
    order = ["w_in", "conv_w", "conv_b", "w_rg", "b_rg", "w_ig", "b_ig", "lru_lambda", "sinks", "w_branch", "w_out", "ln1_g", "ln1_b",
             "w_ffn_in", "w_ffn_out", "ln2_g", "ln2_b"]
    me = 4 * lax.axis_index("x") + 2 * lax.axis_index("y") + lax.axis_index("c")

    params = {}
    g_in = all_gather(w_in.astype(BF16).reshape(DEPTH * D_MODEL, -1), name="gather_w_in")
    sh_in = [g_in[k].reshape(DEPTH, D_MODEL, -1) for k in range(N_DEV)]
    params["w_in_f"] = jnp.concatenate([_cols_of_shards(sh_in, 0, 2 * D_MODEL),
                                        _cols_of_shards(sh_in, 2 * D_MODEL + W_QS + W_QD, W_F + W_QS + W_QD)], axis=-1)
    params["w_in_qs"] = _cols_of_shards(sh_in, 2 * D_MODEL, 2 * D_MODEL + W_QS)
    params["w_in_qd"] = _cols_of_shards(sh_in, 2 * D_MODEL + W_QS, 2 * D_MODEL + W_QS + W_QD)
    g_fi = all_gather(w_ffn_in.astype(BF16).reshape(DEPTH * D_MODEL, -1), name="gather_w_ffn_in")
    params["w_ffn_in"] = jnp.concatenate([g_fi[k].reshape(DEPTH, D_MODEL, -1) for k in range(N_DEV)], axis=-1)
    row_counts = [w[k].size // D_MODEL for k in ROW_SHARDED]
    g_rows = all_gather(jnp.concatenate([w[k].astype(BF16).reshape(-1, D_MODEL) for k in ROW_SHARDED]), name="gather_w_rows")
    off = 0
    for k, n in zip(ROW_SHARDED, row_counts):
        params[k] = _rows_to_full(k, g_rows[:, off:off + n], w[k].shape)
        off += n
    cw = all_gather(conv_w.reshape(-1, LANES), name="gather_conv_w")
    params["conv_w"] = jnp.moveaxis(cw.reshape(N_DEV, DEPTH, CONV_WIDTH, LANES), 0, 2).reshape(DEPTH, CONV_WIDTH, D_MODEL)
    for k in SMALL:
        params[k] = w[k]

    loss_local, grad_x, grads = local_step(x, loss_target, params)
    loss = lax.psum(loss_local, ("x", "y", "c"))

    g_final = {}
    shard = w_in.shape[-1]
    slots = jnp.stack([_cols_of_w_in(grads, k * shard, (k + 1) * shard).astype(BF16) for k in range(N_DEV)])
    recv = all_to_all(slots.reshape(N_DEV, DEPTH * D_MODEL, shard), name="exchange_g_w_in")
    g_final["w_in"] = sum_slots(recv, name="sum_g_w_in").reshape(w_in.shape)
    shard = w_ffn_in.shape[-1]
    slots = jnp.stack([grads["w_ffn_in"][..., k * shard:(k + 1) * shard].astype(BF16) for k in range(N_DEV)])
    recv = all_to_all(slots.reshape(N_DEV, DEPTH * D_MODEL, shard), name="exchange_g_w_ffn_in")
    g_final["w_ffn_in"] = sum_slots(recv, name="sum_g_w_ffn_in").reshape(w_ffn_in.shape)
    slots = jnp.concatenate([_full_to_rows(k, grads[k]).astype(BF16) for k in ROW_SHARDED], axis=1)
    g_rows = sum_slots(all_to_all(slots, name="exchange_g_rows"), name="sum_g_rows")
    off = 0
    for k, n in zip(ROW_SHARDED, row_counts):
        g_final[k] = g_rows[off:off + n].reshape(w[k].shape)
        off += n

    small_names = list(SMALL) + ["conv_w"]
    small_sizes = [grads[k].size for k in small_names]
    svec = _pad_rows(jnp.concatenate([grads[k].reshape(-1) for k in small_names]), SMALL_ROWS_TILE)
    ssum = sum_slots(all_gather(svec, name="gather_small_grads"), name="sum_small_grads")
    sflat, off = ssum.reshape(-1), 0
    for k, n in zip(small_names, small_sizes):
        g_final[k] = sflat[off:off + n].reshape(grads[k].shape)
        off += n
    g_final["conv_w"] = lax.dynamic_slice_in_dim(g_final["conv_w"], me * LANES, LANES, axis=2)

    delta, new_m, new_v = {}, {}, {}
    for k in list(BIG) + ["conv_w"]:
        cols = w[k].shape[-1]
        two_d = lambda t: t.reshape(-1, cols)
        d_, m_, v_ = adamw(two_d(w[k]), two_d(g_final[k]), two_d(m[k]), two_d(v[k]), name=f"adamw_{k}")
        delta[k], new_m[k], new_v[k] = d_.reshape(w[k].shape), m_.reshape(w[k].shape), v_.reshape(w[k].shape)
    pack_small = lambda dct: _pad_rows(jnp.concatenate([dct[k].reshape(-1) for k in SMALL]), SMALL_ROWS_TILE)
    d_, m_, v_ = adamw(pack_small(w), pack_small(g_final), pack_small(m), pack_small(v), name="adamw_small")
    off = 0
    for k in SMALL:
        n = w[k].size
        for dst, src in ((delta, d_), (new_m, m_), (new_v, v_)):
            dst[k] = src.reshape(-1)[off:off + n].reshape(w[k].shape)
        off += n
    return (loss, grad_x, *[g_final[k] for k in order], *[delta[k] for k in order], *[new_m[k] for k in order], *[new_v[k] for k in order])
```

```python
import functools
import math

import jax
import jax.numpy as jnp
from jax import lax
from jax.experimental import pallas as pl
from jax.experimental.pallas import tpu as pltpu

F32 = jnp.float32
BF16 = jnp.bfloat16

N_DEV = 8
DEPTH = 4
D_MODEL = 1024
HEAD_DIM = 64
LANES = 128
N_HEADS = D_MODEL // HEAD_DIM
SWA_KV_HEADS = 4
ATT_BLOCK = 128
DILATIONS = (1, 4, 16)
CONV_WIDTH = 4
LRU_C = 8.0
FF_HIDDEN = 2816
ALPHA = (2.0 * DEPTH) ** 0.25
LN_EPS = 1e-5
NEG_INF = -1e30
W_F = 5 * D_MODEL
W_QS = D_MODEL + 2 * SWA_KV_HEADS * HEAD_DIM
W_QD = 3 * D_MODEL

ADAM_LR = 0.001
ADAM_B1 = 0.9
ADAM_B2 = 0.999
ADAM_EPS = 1e-08
ADAM_WD = 0.01
ADAM_STEP = 10

VMEM_LIMIT = 56 * 1024 * 1024
MESH = pl.DeviceIdType.MESH


def _pick(n, cands):
    for c in cands:
        if n % c == 0:
            return c
    raise ValueError(f"no tile for {n} among {cands}")


def _params(sem):
    return pltpu.CompilerParams(dimension_semantics=sem, vmem_limit_bytes=VMEM_LIMIT)


def _tile(n, cap):
    best = None
    for t in range(LANES, cap + 1, LANES):
        if n % t == 0:
            best = t
    assert best is not None, (n, cap)
    return best


def matmul(a, b, *, name, trans_a=False, trans_b=False, out_dtype=F32, add=None, add_scale=1.0):
    if trans_a:
        k_dim, m_dim = a.shape
    else:
        m_dim, k_dim = a.shape
    n_dim = b.shape[0] if trans_b else b.shape[1]
    assert (b.shape[1] if trans_b else b.shape[0]) == k_dim
    tm = _tile(m_dim, 1024)
    tn = _tile(n_dim, 1408)
    tk = _tile(k_dim, 1408)
    nk = k_dim // tk
    dims = (((0 if trans_a else 1,), (1 if trans_b else 0,)), ((), ()))

    def body(*refs):
        if add is None:
            a_ref, b_ref, o_ref, acc_ref = refs
            add_ref = None
        else:
            a_ref, b_ref, add_ref, o_ref, acc_ref = refs
        k = pl.program_id(2)
        part = lax.dot_general(a_ref[...].astype(BF16), b_ref[...].astype(BF16), dims, preferred_element_type=F32)

        def finish(r):
            if add_ref is not None:
                r = r + add_scale * add_ref[...].astype(F32)
            o_ref[...] = r.astype(out_dtype)

        if nk == 1:
            finish(part)
        else:
            @pl.when(k == 0)
            def _():
                acc_ref[...] = part

            @pl.when((k > 0) & (k < nk - 1))
            def _():
                acc_ref[...] += part

            @pl.when(k == nk - 1)
            def _():
                finish(acc_ref[...] + part)

    a_spec = pl.BlockSpec((tk, tm), lambda i, j, k: (k, i)) if trans_a else pl.BlockSpec((tm, tk), lambda i, j, k: (i, k))
    b_spec = pl.BlockSpec((tn, tk), lambda i, j, k: (j, k)) if trans_b else pl.BlockSpec((tk, tn), lambda i, j, k: (k, j))
    in_specs = [a_spec, b_spec]
    args = [a, b]
    if add is not None:
        in_specs.append(pl.BlockSpec((tm, tn), lambda i, j, k: (i, j)))
        args.append(add)
    return pl.pallas_call(
        body,
        out_shape=jax.ShapeDtypeStruct((m_dim, n_dim), out_dtype),
        grid=(m_dim // tm, n_dim // tn, nk),
        in_specs=in_specs,
        out_specs=pl.BlockSpec((tm, tn), lambda i, j, k: (i, j)),
        scratch_shapes=[pltpu.VMEM((tm, tn) if nk > 1 else (8, LANES), F32)],
        compiler_params=_params(("parallel", "parallel", "arbitrary")),
        name=name,
    )(*args)


def ln_fwd(x, r, g, b, *, name):
    t_dim, d = x.shape
    tr = _pick(t_dim, (256, 128, 8))

    def body(x_ref, r_ref, g_ref, b_ref, y_ref, yb_ref, z_ref):
        z = ALPHA * x_ref[...] + r_ref[...]
        mu = jnp.mean(z, axis=-1, keepdims=True)
        zc = z - mu
        var = jnp.mean(zc * zc, axis=-1, keepdims=True)
        y = zc * lax.rsqrt(var + LN_EPS) * g_ref[...] + b_ref[...]
        y_ref[...] = y
        yb_ref[...] = y.astype(BF16)
        z_ref[...] = z

    row = pl.BlockSpec((tr, d), lambda i: (i, 0))
    vec = pl.BlockSpec((1, d), lambda i: (0, 0))
    return pl.pallas_call(
        body,
        out_shape=(jax.ShapeDtypeStruct((t_dim, d), F32), jax.ShapeDtypeStruct((t_dim, d), BF16), jax.ShapeDtypeStruct((t_dim, d), F32)),
        grid=(t_dim // tr,),
        in_specs=[row, row, vec, vec],
        out_specs=(row, row, row),
        compiler_params=_params(("parallel",)),
        name=name,
    )(x, r, g.reshape(1, d), b.reshape(1, d))


def ln_bwd(dy, z, g, *, name):
    t_dim, d = dy.shape
    tr = _pick(t_dim, (256, 128, 8))

    def body(dy_ref, z_ref, g_ref, dz_ref, dzb_ref, dg_ref, db_ref):
        @pl.when(pl.program_id(0) == 0)
        def _():
            dg_ref[...] = jnp.zeros_like(dg_ref)
            db_ref[...] = jnp.zeros_like(db_ref)

        z = z_ref[...]
        dyv = dy_ref[...]
        mu = jnp.mean(z, axis=-1, keepdims=True)
        zc = z - mu
        var = jnp.mean(zc * zc, axis=-1, keepdims=True)
        rstd = lax.rsqrt(var + LN_EPS)
        xhat = zc * rstd
        dxhat = dyv * g_ref[...]
        m1 = jnp.mean(dxhat, axis=-1, keepdims=True)
        m2 = jnp.mean(dxhat * xhat, axis=-1, keepdims=True)
        dz = rstd * (dxhat - m1 - xhat * m2)
        dz_ref[...] = dz
        dzb_ref[...] = dz.astype(BF16)
        dg_ref[...] += jnp.sum(dyv * xhat, axis=0, keepdims=True)
        db_ref[...] += jnp.sum(dyv, axis=0, keepdims=True)

    row = pl.BlockSpec((tr, d), lambda i: (i, 0))
    vec = pl.BlockSpec((1, d), lambda i: (0, 0))
    return pl.pallas_call(
        body,
        out_shape=(jax.ShapeDtypeStruct((t_dim, d), F32), jax.ShapeDtypeStruct((t_dim, d), BF16),
                   jax.ShapeDtypeStruct((1, d), F32), jax.ShapeDtypeStruct((1, d), F32)),
        grid=(t_dim // tr,),
        in_specs=[row, row, vec],
        out_specs=(row, row, vec, vec),
        compiler_params=_params(("arbitrary",)),
        name=name,
    )(dy, z, g.reshape(1, d))


def loss_head(y, target, *, name):
    t_dim, d = y.shape
    tr = _pick(t_dim, (256, 128, 8))

    def body(y_ref, t_ref, dy_ref, sq_ref):
        @pl.when(pl.program_id(0) == 0)
        def _():
            sq_ref[...] = jnp.zeros_like(sq_ref)

        diff = y_ref[...] - t_ref[...]
        dy_ref[...] = diff / d
        sq_ref[...] += jnp.sum(diff * diff, axis=0, keepdims=True)

    row = pl.BlockSpec((tr, d), lambda i: (i, 0))
    vec = pl.BlockSpec((1, d), lambda i: (0, 0))
    return pl.pallas_call(
        body,
        out_shape=(jax.ShapeDtypeStruct((t_dim, d), F32), jax.ShapeDtypeStruct((1, d), F32)),
        grid=(t_dim // tr,),
        in_specs=[row, row],
        out_specs=(row, vec),
        compiler_params=_params(("arbitrary",)),
        name=name,
    )(y, target)


def _sigmoid(x):
    return 1.0 / (1.0 + jnp.exp(-x))


def swiglu_fwd(h13, *, name):
    t_dim = h13.shape[0]
    f = h13.shape[1] // 2
    tr = _pick(t_dim, (256, 128, 8))

    def body(h1_ref, h3_ref, act_ref):
        h1 = h1_ref[...]
        act_ref[...] = (h1 * _sigmoid(h1) * h3_ref[...]).astype(BF16)

    return pl.pallas_call(
        body,
        out_shape=jax.ShapeDtypeStruct((t_dim, f), BF16),
        grid=(t_dim // tr,),
        in_specs=[pl.BlockSpec((tr, f), lambda i: (i, 0)), pl.BlockSpec((tr, f), lambda i: (i, 1))],
        out_specs=pl.BlockSpec((tr, f), lambda i: (i, 0)),
        compiler_params=_params(("parallel",)),
        name=name,
    )(h13, h13)


def swiglu_bwd(dact, h13, *, name):
    t_dim = h13.shape[0]
    f = h13.shape[1] // 2
    tr = _pick(t_dim, (256, 128, 8))

    def body(da_ref, h1_ref, h3_ref, dh_ref):
        h1 = h1_ref[...]
        da = da_ref[...]
        sg = _sigmoid(h1)
        dh_ref[:, :f] = (da * h3_ref[...] * sg * (1.0 + h1 * (1.0 - sg))).astype(BF16)
        dh_ref[:, f:] = (da * h1 * sg).astype(BF16)

    return pl.pallas_call(
        body,
        out_shape=jax.ShapeDtypeStruct((t_dim, 2 * f), BF16),
        grid=(t_dim // tr,),
        in_specs=[pl.BlockSpec((tr, f), lambda i: (i, 0)), pl.BlockSpec((tr, f), lambda i: (i, 0)),
                  pl.BlockSpec((tr, f), lambda i: (i, 1))],
        out_specs=pl.BlockSpec((tr, 2 * f), lambda i: (i, 0)),
        compiler_params=_params(("parallel",)),
        name=name,
    )(dact, h13, h13)


def merge_fwd(proj_f, br, *, name):
    t_dim, d = br[0].shape
    tr = _pick(t_dim, (256, 128, 8))

    def body(g0, g1, g2, b0, b1, b2, o_ref):
        o_ref[...] = (_sigmoid(g0[...]) * b0[...] + _sigmoid(g1[...]) * b1[...] + _sigmoid(g2[...]) * b2[...]).astype(BF16)

    row = pl.BlockSpec((tr, d), lambda i: (i, 0))
    gate = [pl.BlockSpec((tr, d), functools.partial(lambda n, i: (i, 2 + n), n)) for n in range(3)]
    return pl.pallas_call(
        body,
        out_shape=jax.ShapeDtypeStruct((t_dim, d), BF16),
        grid=(t_dim // tr,),
        in_specs=gate + [row, row, row],
        out_specs=row,
        compiler_params=_params(("parallel",)),
        name=name,
    )(proj_f, proj_f, proj_f, *br)


def merge_bwd(dmerged, proj_f, br, *, name):
    t_dim, d = dmerged.shape
    tr = _pick(t_dim, (256, 128, 8))

    def body(dm_ref, g0, g1, g2, b0, b1, b2, d0, d1, d2, dg_ref):
        dm = dm_ref[...]
        for n, (g, b, o) in enumerate(((g0, b0, d0), (g1, b1, d1), (g2, b2, d2))):
            sg = _sigmoid(g[...])
            o[...] = (dm * sg).astype(BF16)
            dg_ref[:, n * d:(n + 1) * d] = (dm * b[...] * sg * (1.0 - sg)).astype(BF16)

    row = pl.BlockSpec((tr, d), lambda i: (i, 0))
    gate = [pl.BlockSpec((tr, d), functools.partial(lambda n, i: (i, 2 + n), n)) for n in range(3)]
    return pl.pallas_call(
        body,
        out_shape=(jax.ShapeDtypeStruct((t_dim, d), BF16),) * 3 + (jax.ShapeDtypeStruct((t_dim, 3 * d), BF16),),
        grid=(t_dim // tr,),
        in_specs=[row] + gate + [row, row, row],
        out_specs=(row, row, row, pl.BlockSpec((tr, 3 * d), lambda i: (i, 0))),
        compiler_params=_params(("parallel",)),
        name=name,
    )(dmerged, proj_f, proj_f, proj_f, *br)


GELU_C = math.sqrt(2.0 / math.pi)
PAD = 8


def _gelu(x):
    return 0.5 * x * (1.0 + jnp.tanh(GELU_C * (x + 0.044715 * x * x * x)))


def _gelu_grad(x):
    t = jnp.tanh(GELU_C * (x + 0.044715 * x * x * x))
    return 0.5 * (1.0 + t) + 0.5 * x * (1.0 - t * t) * GELU_C * (1.0 + 3.0 * 0.044715 * x * x)


def _neg_expm1(x):
    series = -x * (1.0 + x * (0.5 + x * (1.0 / 6.0 + x * (1.0 / 24.0 + x * (1.0 / 120.0)))))
    return jnp.where(x > -0.1, series, 1.0 - jnp.exp(x))


def _lru_gates(xv, cw_ref, cb_ref, wr_ref, wi_ref, br_ref, bi_ref, lam_ref, pad_ref, s_len):
    pad_ref[pl.ds(0, PAD), :] = jnp.zeros((PAD, LANES), F32)
    pad_ref[pl.ds(PAD, s_len), :] = xv
    xc = cb_ref[...] + jnp.zeros((s_len, LANES), F32)
    for j in range(CONV_WIDTH):
        xc = xc + pad_ref[pl.ds(PAD - (CONV_WIDTH - 1) + j, s_len), :] * cw_ref[pl.ds(j, 1), :]
    xcb = xc.astype(BF16)
    r = _sigmoid(jnp.dot(xcb, wr_ref[0].astype(BF16), preferred_element_type=F32) + br_ref[...])
    i = _sigmoid(jnp.dot(xcb, wi_ref[0].astype(BF16), preferred_element_type=F32) + bi_ref[...])
    nl = -lam_ref[...]
    sp = jnp.maximum(nl, 0.0) + jnp.log(1.0 + jnp.exp(-jnp.abs(nl)))
    log_a = -LRU_C * r * sp
    a = jnp.exp(log_a)
    mult = jnp.sqrt(_neg_expm1(2.0 * log_a))
    return xc, r, i, sp, a, mult


def _tile_scan(a, b, row, reverse):
    for s in (1, 2, 4):
        if reverse:
            a_sh = pltpu.roll(a, 8 - s, 0)
            b_sh = pltpu.roll(b, 8 - s, 0)
            m = row + s <= 7
        else:
            a_sh = pltpu.roll(a, s, 0)
            b_sh = pltpu.roll(b, s, 0)
            m = row >= s
        b = jnp.where(m, a * b_sh + b, b)
        a = jnp.where(m, a * a_sh, a)
    return a, b


def lru_fwd(proj_f, conv_w, conv_b, wr_bd, wi_bd, b_rg, b_ig, lam, *, name, carried=None):
    bsz, s_len, _ = proj_f.shape
    d = D_MODEL
    ncb = d // LANES
    n_tiles = s_len // 8

    def body(x_ref, g_ref, cw_ref, cb_ref, wr_ref, wi_ref, br_ref, bi_ref, lam_ref, y_ref, h_ref, pad_ref, a_s, b_s):
        xc, r, i, sp, a, mult = _lru_gates(x_ref[0], cw_ref, cb_ref, wr_ref, wi_ref, br_ref, bi_ref, lam_ref, pad_ref, s_len)
        a_s[...] = a
        b_s[...] = mult * (i * xc)
        row = lax.broadcasted_iota(jnp.int32, (8, LANES), 0)

        def tile(t, carry):
            i0 = pl.multiple_of(t * 8, 8)
            ac, hl = _tile_scan(a_s[pl.ds(i0, 8), :], b_s[pl.ds(i0, 8), :], row, False)
            h = hl + ac * carry
            h_ref[0, pl.ds(i0, 8), :] = h
            return jnp.broadcast_to(h[7:8, :], (8, LANES))

        lax.fori_loop(0, n_tiles, tile, jnp.zeros((8, LANES), F32))
        y_ref[0] = (h_ref[0] * _gelu(g_ref[0])).astype(BF16)

    slab = lambda off: pl.BlockSpec((1, s_len, LANES), functools.partial(lambda o, c, b: (b, 0, o + c), off))
    vec = pl.BlockSpec((1, LANES), lambda c, b: (0, c))
    mat = pl.BlockSpec((1, LANES, LANES), lambda c, b: (c, 0, 0))
    out = pl.BlockSpec((1, s_len, LANES), lambda c, b: (b, 0, c))
    return call_with_exchange(
        body, carried,
        out_shape=(jax.ShapeDtypeStruct((bsz, s_len, d), BF16), jax.ShapeDtypeStruct((bsz, s_len, d), F32)),
        grid=(ncb, bsz),
        in_specs=[slab(0), slab(ncb), pl.BlockSpec((CONV_WIDTH, LANES), lambda c, b: (0, c)), vec, mat, mat, vec, vec, vec],
        out_specs=(out, out),
        scratch_shapes=[pltpu.VMEM((s_len + 2 * PAD, LANES), F32), pltpu.VMEM((s_len, LANES), F32), pltpu.VMEM((s_len, LANES), F32)],
        name=name,
        args=(proj_f, proj_f, conv_w, conv_b.reshape(1, d), wr_bd, wi_bd, b_rg.reshape(1, d), b_ig.reshape(1, d), lam.reshape(1, d)))


def lru_bwd(dy, proj_f, h, conv_w, conv_b, wr_bd, wi_bd, wr_bd_t, wi_bd_t, b_rg, b_ig, lam, *, name, carried=None):
    bsz, s_len, _ = proj_f.shape
    d = D_MODEL
    ncb = d // LANES
    n_tiles = s_len // 8

    def body(dy_ref, x_ref, g_ref, h_ref, cw_ref, cb_ref, wr_ref, wi_ref, wrt_ref, wit_ref, br_ref, bi_ref, lam_ref,
             dx_ref, dg_ref, dcw_ref, dcb_ref, dbr_ref, dbi_ref, dlam_ref, dwr_ref, dwi_ref, pad_ref, a_s, b_s, l_s):
        @pl.when(pl.program_id(1) == 0)
        def _():
            for ref in (dcw_ref, dcb_ref, dbr_ref, dbi_ref, dlam_ref, dwr_ref, dwi_ref):
                ref[...] = jnp.zeros_like(ref)

        xc, r, i, sp, a, mult = _lru_gates(x_ref[0], cw_ref, cb_ref, wr_ref, wi_ref, br_ref, bi_ref, lam_ref, pad_ref, s_len)
        gate = g_ref[0]
        hv = h_ref[0]
        dyv = dy_ref[0]
        dg_ref[0] = (dyv * hv * _gelu_grad(gate)).astype(BF16)
        b_s[...] = dyv * _gelu(gate)
        l_s[pl.ds(0, s_len), :] = a
        l_s[pl.ds(s_len, PAD), :] = jnp.zeros((PAD, LANES), F32)
        a_s[...] = l_s[pl.ds(1, s_len), :]
        row = lax.broadcasted_iota(jnp.int32, (8, LANES), 0)

        def tile(t, carry):
            i0 = pl.multiple_of((n_tiles - 1 - t) * 8, 8)
            ac, ll = _tile_scan(a_s[pl.ds(i0, 8), :], b_s[pl.ds(i0, 8), :], row, True)
            lmb = ll + ac * carry
            b_s[pl.ds(i0, 8), :] = lmb
            return jnp.broadcast_to(lmb[0:1, :], (8, LANES))

        lax.fori_loop(0, n_tiles, tile, jnp.zeros((8, LANES), F32))
        lmb = b_s[...]
        l_s[pl.ds(0, PAD), :] = jnp.zeros((PAD, LANES), F32)
        l_s[pl.ds(PAD, s_len), :] = hv
        h_prev = l_s[pl.ds(PAD - 1, s_len), :]
        da = lmb * h_prev
        dmult = lmb * (i * xc)
        di = lmb * mult * xc
        dxc = lmb * mult * i
        dlog_a = da * a - dmult * a * a / mult
        dr = -LRU_C * sp * dlog_a
        dsp = jnp.sum(-LRU_C * r * dlog_a, axis=0, keepdims=True)
        dlam_ref[...] += dsp * (-_sigmoid(-lam_ref[...]))
        dpr = dr * r * (1.0 - r)
        dpi = di * i * (1.0 - i)
        dprb = dpr.astype(BF16)
        dpib = dpi.astype(BF16)
        xcb = xc.astype(BF16)
        dbr_ref[...] += jnp.sum(dpr, axis=0, keepdims=True)
        dbi_ref[...] += jnp.sum(dpi, axis=0, keepdims=True)
        tn = (((0,), (0,)), ((), ()))
        dwr_ref[0] += lax.dot_general(xcb, dprb, tn, preferred_element_type=F32)
        dwi_ref[0] += lax.dot_general(xcb, dpib, tn, preferred_element_type=F32)
        dxc = (dxc + jnp.dot(dprb, wrt_ref[0].astype(BF16), preferred_element_type=F32)
               + jnp.dot(dpib, wit_ref[0].astype(BF16), preferred_element_type=F32))
        dcb_ref[...] += jnp.sum(dxc, axis=0, keepdims=True)
        for j in range(CONV_WIDTH):
            dcw_ref[pl.ds(j, 1), :] += jnp.sum(dxc * pad_ref[pl.ds(PAD - (CONV_WIDTH - 1) + j, s_len), :], axis=0, keepdims=True)
        l_s[pl.ds(0, s_len), :] = dxc
        l_s[pl.ds(s_len, PAD), :] = jnp.zeros((PAD, LANES), F32)
        dx = jnp.zeros((s_len, LANES), F32)
        for j in range(CONV_WIDTH):
            dx = dx + l_s[pl.ds(CONV_WIDTH - 1 - j, s_len), :] * cw_ref[pl.ds(j, 1), :]
        dx_ref[0] = dx.astype(BF16)

    slab = lambda off: pl.BlockSpec((1, s_len, LANES), functools.partial(lambda o, c, b: (b, 0, o + c), off))
    vec = pl.BlockSpec((1, LANES), lambda c, b: (0, c))
    mat = pl.BlockSpec((1, LANES, LANES), lambda c, b: (c, 0, 0))
    cw = pl.BlockSpec((CONV_WIDTH, LANES), lambda c, b: (0, c))
    out = pl.BlockSpec((1, s_len, LANES), lambda c, b: (b, 0, c))
    vshape = jax.ShapeDtypeStruct((1, d), F32)
    mshape = jax.ShapeDtypeStruct((ncb, LANES, LANES), F32)
    return call_with_exchange(
        body, carried,
        out_shape=(jax.ShapeDtypeStruct((bsz, s_len, d), BF16),) * 2
        + (jax.ShapeDtypeStruct((CONV_WIDTH, d), F32), vshape, vshape, vshape, vshape, mshape, mshape),
        grid=(ncb, bsz),
        in_specs=[out, slab(0), slab(ncb), out, cw, vec, mat, mat, mat, mat, vec, vec, vec],
        out_specs=(out, out, cw, vec, vec, vec, vec, mat, mat),
        scratch_shapes=[pltpu.VMEM((s_len + 2 * PAD, LANES), F32), pltpu.VMEM((s_len, LANES), F32), pltpu.VMEM((s_len, LANES), F32),
                        pltpu.VMEM((s_len + 2 * PAD, LANES), F32)],
        name=name,
        args=(dy, proj_f, proj_f, h, conv_w, conv_b.reshape(1, d), wr_bd, wi_bd, wr_bd_t, wi_bd_t,
              b_rg.reshape(1, d), b_ig.reshape(1, d), lam.reshape(1, d)))


def _kv_place(head, n_kv_heads):
    kv = head // (N_HEADS // n_kv_heads)
    return kv // 2, kv % 2


def _band_mask(n, single):
    if single:
        qi = lax.broadcasted_iota(jnp.int32, (ATT_BLOCK, ATT_BLOCK), 0)
        return qi >= lax.broadcasted_iota(jnp.int32, (ATT_BLOCK, ATT_BLOCK), 1)
    qi = lax.broadcasted_iota(jnp.int32, (ATT_BLOCK, 2 * ATT_BLOCK), 0)
    kj = lax.broadcasted_iota(jnp.int32, (ATT_BLOCK, 2 * ATT_BLOCK), 1)
    rel = qi + ATT_BLOCK - kj
    return (rel >= 0) & (rel <= ATT_BLOCK) & ((n > 0) | (kj >= ATT_BLOCK))


def _half_masks(dtype):
    lane = lax.broadcasted_iota(jnp.int32, (1, LANES), 1)
    return [(lane < HEAD_DIM).astype(dtype), (lane >= HEAD_DIM).astype(dtype)]


NT = (((1,), (1,)), ((), ()))
TN = (((0,), (0,)), ((), ()))


def _qkv_specs(dil, q_blk, k_blk, v_blk, ckv, clamp):
    qw = D_MODEL // LANES * LANES
    return [
        pl.BlockSpec((1, ATT_BLOCK, qw), lambda b, j, n: (b, clamp(n), j * (q_blk[1]) + q_blk[0])),
        pl.BlockSpec((1, ATT_BLOCK, ckv), lambda b, j, n: (b, jnp.maximum(clamp(n) - 1, 0), j * k_blk[1] + k_blk[0])),
        pl.BlockSpec((1, ATT_BLOCK, ckv), lambda b, j, n: (b, clamp(n), j * k_blk[1] + k_blk[0])),
        pl.BlockSpec((1, ATT_BLOCK, ckv), lambda b, j, n: (b, jnp.maximum(clamp(n) - 1, 0), j * v_blk[1] + v_blk[0])),
        pl.BlockSpec((1, ATT_BLOCK, ckv), lambda b, j, n: (b, clamp(n), j * v_blk[1] + v_blk[0])),
    ]


def attn_fwd(qkv, *, dil, n_kv_heads, sinks, name, emit_bf16=False):
    bsz, s_len, width = qkv.shape
    ckv = n_kv_heads * HEAD_DIM
    l_sub = s_len // dil
    nb = l_sub // ATT_BLOCK
    view = qkv.reshape(bsz, l_sub, dil * width)
    scale = HEAD_DIM ** -0.5
    q_blk = (0, width // D_MODEL)
    k_blk = (D_MODEL // ckv, width // ckv)
    v_blk = (D_MODEL // ckv + 1, width // ckv)
    assert (dil == 1 or width % D_MODEL == 0) and width % ckv == 0 and D_MODEL % ckv == 0

    single = nb == 1

    def body(*refs):
        refs = list(refs)
        sink_ref = refs.pop(0) if sinks is not None else None
        ob_ref = refs.pop() if emit_bf16 else None
        q_ref, kp_ref, kc_ref, vp_ref, vc_ref, o_ref, lse_ref = refs
        n = pl.program_id(2)
        mask = _band_mask(n, single)
        hm = _half_masks(BF16)
        hmf = _half_masks(F32)
        kk = kc_ref[0] if single else jnp.concatenate([kp_ref[0], kc_ref[0]], axis=0)
        vv = vc_ref[0] if single else jnp.concatenate([vp_ref[0], vc_ref[0]], axis=0)
        for hp in range(N_HEADS // 2):
            q2 = q_ref[0, :, hp * LANES:(hp + 1) * LANES]
            o2 = jnp.zeros((ATT_BLOCK, LANES), F32)
            l2 = jnp.zeros((ATT_BLOCK, LANES), F32)
            for a in range(2):
                kb, kh = _kv_place(2 * hp + a, n_kv_heads)
                k2 = kk[:, kb * LANES:(kb + 1) * LANES]
                v2 = vv[:, kb * LANES:(kb + 1) * LANES]
                if kh != a:
                    k2 = pltpu.roll(k2, HEAD_DIM, 1)
                    v2 = pltpu.roll(v2, HEAD_DIM, 1)
                s = lax.dot_general(q2 * hm[a], k2, NT, preferred_element_type=F32) * scale
                s = jnp.where(mask, s, NEG_INF)
                m = jnp.max(s, axis=-1, keepdims=True)
                if sink_ref is not None:
                    sk = sink_ref[2 * hp + a]
                    m = jnp.maximum(m, sk)
                p = jnp.exp(s - m)
                den = jnp.sum(p, axis=-1, keepdims=True)
                if sink_ref is not None:
                    den = den + jnp.exp(sk - m)
                o2 = o2 + jnp.dot(p.astype(BF16), v2 * hm[a], preferred_element_type=F32) / den
                l2 = l2 + (m + jnp.log(den)) * hmf[a]
            o_ref[0, :, hp * LANES:(hp + 1) * LANES] = o2
            lse_ref[0, :, hp * LANES:(hp + 1) * LANES] = l2
            if ob_ref is not None:
                ob_ref[0, :, hp * LANES:(hp + 1) * LANES] = o2.astype(BF16)

    in_specs = _qkv_specs(dil, q_blk, k_blk, v_blk, ckv, lambda n: n)
    args = [view] * 5
    if sinks is not None:
        in_specs = [pl.BlockSpec(memory_space=pltpu.SMEM)] + in_specs
        args = [sinks] + args
    out = pl.BlockSpec((1, ATT_BLOCK, D_MODEL), lambda b, j, n: (b, n, j))
    res = pl.pallas_call(
        body,
        out_shape=(jax.ShapeDtypeStruct((bsz, l_sub, dil * D_MODEL), F32),) * 2
        + ((jax.ShapeDtypeStruct((bsz, l_sub, dil * D_MODEL), BF16),) if emit_bf16 else ()),
        grid=(bsz, dil, nb),
        in_specs=in_specs,
        out_specs=(out,) * (3 if emit_bf16 else 2),
        compiler_params=_params(("parallel", "parallel", "arbitrary")),
        name=name,
    )(*args)
    return tuple(t.reshape(bsz, s_len, D_MODEL) for t in res)


def attn_bwd(qkv, o, lse, do, acc, *, dil, n_kv_heads, name):
    bsz, s_len, width = qkv.shape
    ckv = n_kv_heads * HEAD_DIM
    l_sub = s_len // dil
    nb = l_sub // ATT_BLOCK
    view = qkv.reshape(bsz, l_sub, dil * width)
    scale = HEAD_DIM ** -0.5
    q_blk = (0, width // D_MODEL)
    k_blk = (D_MODEL // ckv, width // ckv)
    v_blk = (D_MODEL // ckv + 1, width // ckv)
    single = nb == 1

    def body(*refs):
        if acc is None:
            q_ref, kp_ref, kc_ref, vp_ref, vc_ref, o_ref, lse_ref, do_ref, dq_ref, dk_ref, dv_ref, dkk, dvv, ck, cv = refs
            aq_ref = ak_ref = av_ref = None
        else:
            (q_ref, kp_ref, kc_ref, vp_ref, vc_ref, o_ref, lse_ref, do_ref, aq_ref, ak_ref, av_ref,
             dq_ref, dk_ref, dv_ref, dkk, dvv, ck, cv) = refs
        n = pl.program_id(2)

        @pl.when(n < nb)
        def _():
            mask = _band_mask(n, single)
            hm = _half_masks(BF16)
            hmf = _half_masks(F32)
            kk = kc_ref[0] if single else jnp.concatenate([kp_ref[0], kc_ref[0]], axis=0)
            vv = vc_ref[0] if single else jnp.concatenate([vp_ref[0], vc_ref[0]], axis=0)
            krows = pl.ds(ATT_BLOCK, ATT_BLOCK) if single else pl.ds(0, 2 * ATT_BLOCK)
            dkk[...] = jnp.zeros_like(dkk)
            dvv[...] = jnp.zeros_like(dvv)
            for hp in range(N_HEADS // 2):
                cols = slice(hp * LANES, (hp + 1) * LANES)
                q2 = q_ref[0, :, cols]
                do2f = do_ref[0, :, cols]
                do2 = do2f.astype(BF16)
                dd2 = do2f * o_ref[0, :, cols]
                l2 = lse_ref[0, :, cols]
                dq2 = jnp.zeros((ATT_BLOCK, LANES), F32)
                for a in range(2):
                    kb, kh = _kv_place(2 * hp + a, n_kv_heads)
                    kcols = slice(kb * LANES, (kb + 1) * LANES)
                    k2 = kk[:, kcols]
                    v2 = vv[:, kcols]
                    if kh != a:
                        k2 = pltpu.roll(k2, HEAD_DIM, 1)
                        v2 = pltpu.roll(v2, HEAD_DIM, 1)
                    qm = q2 * hm[a]
                    dom = do2 * hm[a]
                    dsum = jnp.sum(dd2 * hmf[a], axis=-1, keepdims=True)
                    lse_h = jnp.max(jnp.where(hmf[a] > 0.5, l2, NEG_INF), axis=-1, keepdims=True)
                    s = lax.dot_general(qm, k2, NT, preferred_element_type=F32) * scale
                    s = jnp.where(mask, s, NEG_INF)
                    p = jnp.exp(s - lse_h)
                    dp = lax.dot_general(dom, v2, NT, preferred_element_type=F32)
                    ds = (p * (dp - dsum) * scale).astype(BF16)
                    dq2 = dq2 + jnp.dot(ds, k2 * hm[a], preferred_element_type=F32)
                    dk_c = lax.dot_general(ds, qm, TN, preferred_element_type=F32)
                    dv_c = lax.dot_general(p.astype(BF16), dom, TN, preferred_element_type=F32)
                    if kh != a:
                        dk_c = pltpu.roll(dk_c, HEAD_DIM, 1)
                        dv_c = pltpu.roll(dv_c, HEAD_DIM, 1)
                    dkk[krows, kcols] += dk_c
                    dvv[krows, kcols] += dv_c
                if aq_ref is not None:
                    dq2 = dq2 + aq_ref[0, :, cols]
                dq_ref[0, :, cols] = dq2

        @pl.when((n >= 1) & (n < nb))
        def _():
            dk_ref[0] = ck[...] + dkk[pl.ds(0, ATT_BLOCK), :] + (0.0 if ak_ref is None else ak_ref[0])
            dv_ref[0] = cv[...] + dvv[pl.ds(0, ATT_BLOCK), :] + (0.0 if av_ref is None else av_ref[0])

        @pl.when(n == nb)
        def _():
            dk_ref[0] = ck[...] + (0.0 if ak_ref is None else ak_ref[0])
            dv_ref[0] = cv[...] + (0.0 if av_ref is None else av_ref[0])

        @pl.when(n < nb)
        def _():
            ck[...] = dkk[pl.ds(ATT_BLOCK, ATT_BLOCK), :]
            cv[...] = dvv[pl.ds(ATT_BLOCK, ATT_BLOCK), :]

    clamp = lambda n: jnp.minimum(n, nb - 1)
    prev = lambda n: jnp.maximum(n - 1, 0)
    row = pl.BlockSpec((1, ATT_BLOCK, D_MODEL), lambda b, j, n: (b, clamp(n), j))
    kv_out = pl.BlockSpec((1, ATT_BLOCK, ckv), lambda b, j, n: (b, prev(n), j))
    in_specs = _qkv_specs(dil, q_blk, k_blk, v_blk, ckv, clamp) + [row, row, row]
    rs = lambda t: t.reshape(bsz, l_sub, dil * t.shape[-1])
    args = [view] * 5 + [rs(o), rs(lse), rs(do)]
    if acc is not None:
        in_specs += [row, kv_out, kv_out]
        args += [rs(t) for t in acc]
    dq, dk, dv = pl.pallas_call(
        body,
        out_shape=(jax.ShapeDtypeStruct((bsz, l_sub, dil * D_MODEL), F32),
                   jax.ShapeDtypeStruct((bsz, l_sub, dil * ckv), F32), jax.ShapeDtypeStruct((bsz, l_sub, dil * ckv), F32)),
        grid=(bsz, dil, nb + 1),
        in_specs=in_specs,
        out_specs=(row, kv_out, kv_out),
        scratch_shapes=[pltpu.VMEM((2 * ATT_BLOCK, ckv), F32), pltpu.VMEM((2 * ATT_BLOCK, ckv), F32),
                        pltpu.VMEM((ATT_BLOCK, ckv), F32), pltpu.VMEM((ATT_BLOCK, ckv), F32)],
        compiler_params=_params(("parallel", "parallel", "arbitrary")),
        name=name,
    )(*args)
    return dq.reshape(bsz, s_len, D_MODEL), dk.reshape(bsz, s_len, ckv), dv.reshape(bsz, s_len, ckv)


def dil_combine(os_, lses, *, name):
    t_dim, d = os_[0].shape
    tr = _pick(t_dim, (256, 128, 8))

    def body(o0, o1, o2, l0, l1, l2, y_ref, lt_ref, yb_ref):
        la, lb, lc = l0[...], l1[...], l2[...]
        m = jnp.maximum(jnp.maximum(la, lb), lc)
        ea, eb, ec = jnp.exp(la - m), jnp.exp(lb - m), jnp.exp(lc - m)
        tot = ea + eb + ec
        y = (ea / tot) * o0[...] + (eb / tot) * o1[...] + (ec / tot) * o2[...]
        y_ref[...] = y
        yb_ref[...] = y.astype(BF16)
        lt_ref[...] = m + jnp.log(tot)

    row = pl.BlockSpec((tr, d), lambda i: (i, 0))
    return pl.pallas_call(
        body,
        out_shape=(jax.ShapeDtypeStruct((t_dim, d), F32),) * 2 + (jax.ShapeDtypeStruct((t_dim, d), BF16),),
        grid=(t_dim // tr,),
        in_specs=[row] * 6,
        out_specs=(row, row, row),
        compiler_params=_params(("parallel",)),
        name=name,
    )(*os_, *lses)


ATT_SCALE = HEAD_DIM ** -0.5


def _band_mask(n, single):
    nk = ATT_BLOCK if single else 2 * ATT_BLOCK
    qi = lax.broadcasted_iota(jnp.int32, (2 * ATT_BLOCK, nk), 0) % ATT_BLOCK
    kj = lax.broadcasted_iota(jnp.int32, (2 * ATT_BLOCK, nk), 1)
    if single:
        return qi >= kj
    rel = qi + ATT_BLOCK - kj
    return (rel >= 0) & (rel <= ATT_BLOCK) & ((n > 0) | (kj >= ATT_BLOCK))


def _lane_halves():
    lane = lax.broadcasted_iota(jnp.int32, (1, LANES), 1)
    return lane < HEAD_DIM


def _stack_heads(t2, kh):
    first = _lane_halves()
    parts = []
    for a in range(2):
        ta = jnp.where(first if a == 0 else ~first, t2, jnp.zeros_like(t2))
        if a != kh[a]:
            ta = pltpu.roll(ta, HEAD_DIM, 1)
        parts.append(ta)
    return jnp.concatenate(parts, axis=0)


def _fold_heads(t, kh):
    t0, t1 = t[:ATT_BLOCK], t[ATT_BLOCK:]
    if t.shape[1] == LANES:
        if kh[0] != 0:
            t0 = pltpu.roll(t0, HEAD_DIM, 1)
        if kh[1] != 1:
            t1 = pltpu.roll(t1, HEAD_DIM, 1)
    return jnp.where(_lane_halves(), t0, t1)


def _rows_of_heads(t2):
    return jnp.concatenate([t2[:, 0:1], t2[:, HEAD_DIM:HEAD_DIM + 1]], axis=0)


PAIRS_AT_ONCE = 4


def _fill_bias(bias2_ref, bias1_ref=None):
    for i in range(2):
        bias2_ref[i] = jnp.where(_band_mask(i, False), 0.0, NEG_INF)
    if bias1_ref is not None:
        bias1_ref[...] = jnp.where(_band_mask(0, True), 0.0, NEG_INF)


def _pairs_fwd(items):
    ss = [lax.dot_general(_stack_heads(q2 * ATT_SCALE, kh), kk, NT, preferred_element_type=F32) + bias
          for q2, kk, _, bias, kh, _ in items]
    ps, ms, ls = [], [], []
    for s, (_, _, _, _, _, sink_col) in zip(ss, items):
        m = jnp.max(s, axis=-1, keepdims=True)
        if sink_col is not None:
            m = jnp.maximum(m, sink_col)
        p = jnp.exp(s - m)
        l = jnp.sum(p, axis=-1, keepdims=True)
        if sink_col is not None:
            l = l + jnp.exp(sink_col - m)
        ps.append(p.astype(BF16))
        ms.append(m)
        ls.append(l)
    pvs = [jnp.dot(p, it[2], preferred_element_type=F32) for p, it in zip(ps, items)]
    return list(zip(pvs, ms, ls))


def _pairs_bwd(items):
    first = _lane_halves()
    pre = []
    for q2, kk, vv, do2, o2, lse2, bias, kh in items:
        dd = do2 * o2
        dsum = jnp.concatenate([jnp.sum(jnp.where(first, dd, 0.0), axis=-1, keepdims=True),
                                jnp.sum(jnp.where(first, 0.0, dd), axis=-1, keepdims=True)], axis=0)
        qs = _stack_heads(q2 * ATT_SCALE, kh)
        dos = _stack_heads(do2.astype(BF16), kh)
        s = lax.dot_general(qs, kk, NT, preferred_element_type=F32) + bias
        dp = lax.dot_general(dos, vv, NT, preferred_element_type=F32)
        pre.append((qs, dos, s, dp, dsum))
    mid = []
    for (qs, dos, s, dp, dsum), it in zip(pre, items):
        p = jnp.exp(s - _rows_of_heads(it[5]))
        mid.append((p.astype(BF16), (p * (dp - dsum)).astype(BF16)))
    out = []
    for (pb, ds), (qs, dos, _, _, _), it in zip(mid, pre, items):
        dq = _fold_heads(jnp.dot(ds, it[1], preferred_element_type=F32), it[7]) * ATT_SCALE
        dk = lax.dot_general(ds, qs, TN, preferred_element_type=F32)
        dv = lax.dot_general(pb, dos, TN, preferred_element_type=F32)
        out.append((dq, dk, dv))
    return out


def swa_fwd(qkv, sinks, *, name, carried=None):
    bsz, s_len, width = qkv.shape
    ckv = SWA_KV_HEADS * HEAD_DIM
    nb = s_len // ATT_BLOCK
    kblk = D_MODEL // ckv

    def body(sink_ref, q_ref, kp_ref, kc_ref, vp_ref, vc_ref, o_ref, lse_ref, ob_ref, bias2):
        n = pl.program_id(1)
        _fill_bias(bias2)
        bias = bias2[jnp.minimum(n, 1)]
        kk = jnp.concatenate([kp_ref[0], kc_ref[0]], axis=0)
        vv = jnp.concatenate([vp_ref[0], vc_ref[0]], axis=0)
        top = lax.broadcasted_iota(jnp.int32, (2 * ATT_BLOCK, 1), 0) < ATT_BLOCK
        for hp0 in range(0, N_HEADS // 2, PAIRS_AT_ONCE):
            items, places = [], []
            for hp in range(hp0, hp0 + PAIRS_AT_ONCE):
                cols = slice(hp * LANES, (hp + 1) * LANES)
                kb, kh = _kv_place(2 * hp, SWA_KV_HEADS)
                kcols = slice(kb * LANES, (kb + 1) * LANES)
                sink_col = jnp.where(top, sink_ref[2 * hp], sink_ref[2 * hp + 1])
                items.append((q_ref[0, :, cols], kk[:, kcols], vv[:, kcols], bias, (kh, kh), sink_col))
                places.append((cols, (kh, kh)))
            for (pv, m, l), (cols, kh2) in zip(_pairs_fwd(items), places):
                o2 = _fold_heads(pv / l, kh2)
                o_ref[0, :, cols] = o2
                ob_ref[0, :, cols] = o2.astype(BF16)
                lse_ref[0, :, cols] = _fold_heads(m + jnp.log(l), kh2)

    prev = lambda n: jnp.maximum(n - 1, 0)
    out = pl.BlockSpec((1, ATT_BLOCK, D_MODEL), lambda b, n: (b, n, 0))
    sd = lambda dt: jax.ShapeDtypeStruct((bsz, s_len, D_MODEL), dt)
    return call_with_exchange(
        body, carried,
        out_shape=(sd(F32), sd(F32), sd(BF16)),
        grid=(bsz, nb),
        in_specs=[pl.BlockSpec(memory_space=pltpu.SMEM), out,
                  pl.BlockSpec((1, ATT_BLOCK, ckv), lambda b, n: (b, prev(n), kblk)),
                  pl.BlockSpec((1, ATT_BLOCK, ckv), lambda b, n: (b, n, kblk)),
                  pl.BlockSpec((1, ATT_BLOCK, ckv), lambda b, n: (b, prev(n), kblk + 1)),
                  pl.BlockSpec((1, ATT_BLOCK, ckv), lambda b, n: (b, n, kblk + 1))],
        out_specs=(out, out, out),
        scratch_shapes=[pltpu.VMEM((2, 2 * ATT_BLOCK, 2 * ATT_BLOCK), F32)],
        name=name,
        args=(sinks, qkv, qkv, qkv, qkv, qkv))


def swa_bwd(qkv, o, lse, do, *, name, carried=None):
    bsz, s_len, width = qkv.shape
    ckv = SWA_KV_HEADS * HEAD_DIM
    nb = s_len // ATT_BLOCK
    kblk = D_MODEL // ckv

    def body(q_ref, kp_ref, kc_ref, vp_ref, vc_ref, o_ref, lse_ref, do_ref, dq_ref, dk_ref, dv_ref, dkk, dvv, ck, cv,
             bias2):
        n = pl.program_id(1)

        @pl.when(n < nb)
        def _():
            _fill_bias(bias2)
            bias = bias2[jnp.minimum(n, 1)]
            kk = jnp.concatenate([kp_ref[0], kc_ref[0]], axis=0)
            vv = jnp.concatenate([vp_ref[0], vc_ref[0]], axis=0)
            dkk[...] = jnp.zeros_like(dkk)
            dvv[...] = jnp.zeros_like(dvv)
            for hp0 in range(0, N_HEADS // 2, PAIRS_AT_ONCE):
                items, places = [], []
                for hp in range(hp0, hp0 + PAIRS_AT_ONCE):
                    cols = slice(hp * LANES, (hp + 1) * LANES)
                    kb, kh = _kv_place(2 * hp, SWA_KV_HEADS)
                    kcols = slice(kb * LANES, (kb + 1) * LANES)
                    items.append((q_ref[0, :, cols], kk[:, kcols], vv[:, kcols], do_ref[0, :, cols], o_ref[0, :, cols],
                                  lse_ref[0, :, cols], bias, (kh, kh)))
                    places.append((cols, kcols))
                for (dq, dk, dv), (cols, kcols) in zip(_pairs_bwd(items), places):
                    dq_ref[0, :, cols] = dq
                    dkk[:, kcols] += dk
                    dvv[:, kcols] += dv

        @pl.when((n >= 1) & (n < nb))
        def _():
            dk_ref[0] = ck[...] + dkk[pl.ds(0, ATT_BLOCK), :]
            dv_ref[0] = cv[...] + dvv[pl.ds(0, ATT_BLOCK), :]

        @pl.when(n == nb)
        def _():
            dk_ref[0] = ck[...]
            dv_ref[0] = cv[...]

        @pl.when(n < nb)
        def _():
            ck[...] = dkk[pl.ds(ATT_BLOCK, ATT_BLOCK), :]
            cv[...] = dvv[pl.ds(ATT_BLOCK, ATT_BLOCK), :]

    clamp = lambda n: jnp.minimum(n, nb - 1)
    prev = lambda n: jnp.maximum(n - 1, 0)
    row = pl.BlockSpec((1, ATT_BLOCK, D_MODEL), lambda b, n: (b, clamp(n), 0))
    kv_out = pl.BlockSpec((1, ATT_BLOCK, ckv), lambda b, n: (b, prev(n), 0))
    return call_with_exchange(
        body, carried,
        out_shape=(jax.ShapeDtypeStruct((bsz, s_len, D_MODEL), F32), jax.ShapeDtypeStruct((bsz, s_len, ckv), F32),
                   jax.ShapeDtypeStruct((bsz, s_len, ckv), F32)),
        grid=(bsz, nb + 1),
        in_specs=[row,
                  pl.BlockSpec((1, ATT_BLOCK, ckv), lambda b, n: (b, prev(clamp(n)), kblk)),
                  pl.BlockSpec((1, ATT_BLOCK, ckv), lambda b, n: (b, clamp(n), kblk)),
                  pl.BlockSpec((1, ATT_BLOCK, ckv), lambda b, n: (b, prev(clamp(n)), kblk + 1)),
                  pl.BlockSpec((1, ATT_BLOCK, ckv), lambda b, n: (b, clamp(n), kblk + 1)),
                  row, row, row],
        out_specs=(row, kv_out, kv_out),
        scratch_shapes=[pltpu.VMEM((2 * ATT_BLOCK, ckv), F32), pltpu.VMEM((2 * ATT_BLOCK, ckv), F32),
                        pltpu.VMEM((ATT_BLOCK, ckv), F32), pltpu.VMEM((ATT_BLOCK, ckv), F32),
                        pltpu.VMEM((2, 2 * ATT_BLOCK, 2 * ATT_BLOCK), F32)],
        name=name,
        args=(qkv, qkv, qkv, qkv, qkv, o, lse, do))


DIL_PATTERNS = tuple((d, 2048 // d // ATT_BLOCK) for d in DILATIONS)
MHA = (0, 1)


def _dil_rows(idx, d, nb):
    j = idx // nb
    n = idx % nb
    base = j + n * (ATT_BLOCK * d)
    prev = jnp.maximum(base - ATT_BLOCK * d, j)
    if d == 1:
        return n, pl.ds(pl.multiple_of(base, ATT_BLOCK), ATT_BLOCK), pl.ds(pl.multiple_of(prev, ATT_BLOCK), ATT_BLOCK)
    return n, pl.ds(base, ATT_BLOCK, stride=d), pl.ds(prev, ATT_BLOCK, stride=d)


def dil_fwd(qkv, *, name, carried=None):
    bsz, s_len, _ = qkv.shape
    assert s_len == DIL_PATTERNS[0][0] * DIL_PATTERNS[0][1] * ATT_BLOCK
    npair = N_HEADS // 2

    def body(q_ref, k_ref, v_ref, y_ref, lse_ref, yb_ref, m_acc, l_acc, bias2, bias1):
        _fill_bias(bias2, bias1)
        for ci, (d, nb) in enumerate(DIL_PATTERNS):
            single = nb == 1

            def blocks(it, carry):
                items, places = [], []
                for u in range(PAIRS_AT_ONCE):
                    n, rows, prows = _dil_rows(it * PAIRS_AT_ONCE + u, d, nb)
                    kc = k_ref[rows, :].astype(BF16)
                    vc = v_ref[rows, :].astype(BF16)
                    if single:
                        kk, vv, bias = kc, vc, bias1[...]
                    else:
                        kk = jnp.concatenate([k_ref[prows, :].astype(BF16), kc], axis=0)
                        vv = jnp.concatenate([v_ref[prows, :].astype(BF16), vc], axis=0)
                        bias = bias2[jnp.minimum(n, 1)]
                    items.append((q_ref[rows, :].astype(BF16), kk, vv, bias, MHA, None))
                    places.append(rows)
                for (pv, m, l), rows in zip(_pairs_fwd(items), places):
                    o2, m2, l2 = _fold_heads(pv, MHA), _fold_heads(m, MHA), _fold_heads(l, MHA)
                    if ci == 0:
                        y_ref[rows, :] = o2
                        m_acc[rows, :] = m2
                        l_acc[rows, :] = l2
                    else:
                        m_old = m_acc[rows, :]
                        m_new = jnp.maximum(m_old, m2)
                        w_old = jnp.exp(m_old - m_new)
                        w_new = jnp.exp(m2 - m_new)
                        y_ref[rows, :] = y_ref[rows, :] * w_old + o2 * w_new
                        l_acc[rows, :] = l_acc[rows, :] * w_old + l2 * w_new
                        m_acc[rows, :] = m_new
                return carry

            lax.fori_loop(0, d * nb // PAIRS_AT_ONCE, blocks, 0)
        y = y_ref[...] / l_acc[...]
        y_ref[...] = y
        yb_ref[...] = y.astype(BF16)
        lse_ref[...] = m_acc[...] + jnp.log(l_acc[...])

    slab = lambda off: pl.BlockSpec((None, s_len, LANES), functools.partial(lambda o, b, h: (b, 0, o + h), off))
    sd = lambda dt: jax.ShapeDtypeStruct((bsz, s_len, D_MODEL), dt)
    return call_with_exchange(
        body, carried,
        out_shape=(sd(F32), sd(F32), sd(BF16)),
        grid=(bsz, npair),
        in_specs=[slab(0), slab(npair), slab(2 * npair)],
        out_specs=(slab(0), slab(0), slab(0)),
        scratch_shapes=[pltpu.VMEM((s_len, LANES), F32), pltpu.VMEM((s_len, LANES), F32),
                        pltpu.VMEM((2, 2 * ATT_BLOCK, 2 * ATT_BLOCK), F32), pltpu.VMEM((2 * ATT_BLOCK, ATT_BLOCK), F32)],
        name=name,
        args=(qkv, qkv, qkv))


def dil_bwd(qkv, y, lse, dy, *, name, carried=None):
    bsz, s_len, _ = qkv.shape
    npair = N_HEADS // 2

    def body(q_ref, k_ref, v_ref, y_ref, lse_ref, dy_ref, dq_ref, dk_ref, dv_ref, bias2, bias1):
        _fill_bias(bias2, bias1)
        dq_ref[...] = jnp.zeros_like(dq_ref)
        dk_ref[...] = jnp.zeros_like(dk_ref)
        dv_ref[...] = jnp.zeros_like(dv_ref)
        for d, nb in DIL_PATTERNS:
            single = nb == 1

            def blocks(it, carry):
                items, places = [], []
                for u in range(PAIRS_AT_ONCE):
                    n, rows, prows = _dil_rows(it * PAIRS_AT_ONCE + u, d, nb)
                    kc = k_ref[rows, :].astype(BF16)
                    vc = v_ref[rows, :].astype(BF16)
                    if single:
                        kk, vv, bias = kc, vc, bias1[...]
                    else:
                        kk = jnp.concatenate([k_ref[prows, :].astype(BF16), kc], axis=0)
                        vv = jnp.concatenate([v_ref[prows, :].astype(BF16), vc], axis=0)
                        bias = bias2[jnp.minimum(n, 1)]
                    items.append((q_ref[rows, :].astype(BF16), kk, vv, dy_ref[rows, :], y_ref[rows, :], lse_ref[rows, :], bias, MHA))
                    places.append((rows, prows))
                for (dq, dk, dv), (rows, prows) in zip(_pairs_bwd(items), places):
                    dq_ref[rows, :] += dq
                    if single:
                        dk_ref[rows, :] += dk
                        dv_ref[rows, :] += dv
                    else:
                        dk_ref[prows, :] += dk[:ATT_BLOCK]
                        dv_ref[prows, :] += dv[:ATT_BLOCK]
                        dk_ref[rows, :] += dk[ATT_BLOCK:]
                        dv_ref[rows, :] += dv[ATT_BLOCK:]
                return carry

            lax.fori_loop(0, d * nb // PAIRS_AT_ONCE, blocks, 0)

    slab = lambda off: pl.BlockSpec((None, s_len, LANES), functools.partial(lambda o, b, h: (b, 0, o + h), off))
    sd = jax.ShapeDtypeStruct((bsz, s_len, D_MODEL), F32)
    return call_with_exchange(
        body, carried,
        out_shape=(sd, sd, sd),
        grid=(bsz, npair),
        in_specs=[slab(0), slab(npair), slab(2 * npair), slab(0), slab(0), slab(0)],
        out_specs=(slab(0), slab(0), slab(0)),
        scratch_shapes=[pltpu.VMEM((2, 2 * ATT_BLOCK, 2 * ATT_BLOCK), F32), pltpu.VMEM((2 * ATT_BLOCK, ATT_BLOCK), F32)],
        name=name,
        args=(qkv, qkv, qkv, y, lse, dy))


def sink_grad(do, o, lse, sink_lanes, *, name):
    t_dim, d = do.shape
    tr = _pick(t_dim, (256, 128, 8))

    def body(do_ref, o_ref, l_ref, s_ref, out_ref):
        @pl.when(pl.program_id(0) == 0)
        def _():
            out_ref[...] = jnp.zeros_like(out_ref)

        out_ref[...] += jnp.sum(-jnp.exp(s_ref[...] - l_ref[...]) * do_ref[...] * o_ref[...], axis=0, keepdims=True)

    row = pl.BlockSpec((tr, d), lambda i: (i, 0))
    vec = pl.BlockSpec((1, d), lambda i: (0, 0))
    return pl.pallas_call(
        body,
        out_shape=jax.ShapeDtypeStruct((1, d), F32),
        grid=(t_dim // tr,),
        in_specs=[row, row, row, vec],
        out_specs=vec,
        compiler_params=_params(("arbitrary",)),
        name=name,
    )(do, o, lse, sink_lanes)


def adamw(w, g, m, v, *, name):
    rows, cols = w.shape
    tr = _pick(rows, (256, 128, 64, 32, 16, 8))

    def body(w_ref, g_ref, m_ref, v_ref, d_ref, nm_ref, nv_ref):
        gv = g_ref[...]
        nm = ADAM_B1 * m_ref[...] + (1.0 - ADAM_B1) * gv
        nv = ADAM_B2 * v_ref[...] + (1.0 - ADAM_B2) * (gv * gv)
        m_hat = nm / (1.0 - ADAM_B1 ** ADAM_STEP)
        v_hat = nv / (1.0 - ADAM_B2 ** ADAM_STEP)
        d_ref[...] = -ADAM_LR * (m_hat / (jnp.sqrt(v_hat) + ADAM_EPS) + ADAM_WD * w_ref[...])
        nm_ref[...] = nm
        nv_ref[...] = nv

    row = pl.BlockSpec((tr, cols), lambda i: (i, 0))
    return pl.pallas_call(
        body,
        out_shape=(jax.ShapeDtypeStruct((rows, cols), F32),) * 3,
        grid=(rows // tr,),
        in_specs=[row] * 4,
        out_specs=(row, row, row),
        compiler_params=_params(("parallel",)),
        name=name,
    )(w, g, m, v)


def _place():
    return lax.axis_index("x"), lax.axis_index("y"), lax.axis_index("c")


def _gather_copies(x_ref, out_ref, send_sems, recv_sems):
    x, y, c = _place()
    me, sibling = (x, y, c), (x, y, 1 - c)
    chips = [(1 - x, y), (x, 1 - y), (1 - x, 1 - y)]

    def slot(px, py, pc):
        return out_ref.at[4 * px + 2 * py + pc]

    def copy(k, block, to, src=None):
        return pltpu.make_async_remote_copy(
            src_ref=slot(*block) if src is None else src, dst_ref=slot(*block),
            send_sem=send_sems.at[k], recv_sem=recv_sems.at[k], device_id=to, device_id_type=MESH)

    first = [lambda: copy(0, me, sibling, src=x_ref)] + [functools.partial(copy, 1 + j, me, (*chip, c), src=x_ref)
                                                         for j, chip in enumerate(chips)]
    passed = [functools.partial(copy, 4 + j, (*chip, c), sibling) for j, chip in enumerate(chips)]
    landing = [functools.partial(copy, 1 + j, (*chip, c), me) for j, chip in enumerate(chips)]
    from_sibling = [lambda: copy(0, sibling, me)] + [functools.partial(copy, 4 + j, (*chip, 1 - c), me) for j, chip in enumerate(chips)]
    return slot(*me), first, passed, landing, from_sibling


def _gather_start(x_ref, out_ref, send_sems, recv_sems, local_sem):
    mine, first, _, _, _ = _gather_copies(x_ref, out_ref, send_sems, recv_sems)
    pltpu.make_async_copy(x_ref, mine, local_sem).start()
    for cp in first:
        cp().start()


def _gather_finish(x_ref, out_ref, send_sems, recv_sems, local_sem):
    mine, first, passed, landing, from_sibling = _gather_copies(x_ref, out_ref, send_sems, recv_sems)
    for cp, fwd in zip(landing, passed):
        cp().wait_recv()
        fwd().start()
    for cp in from_sibling:
        cp().wait_recv()
    for cp in first + passed:
        cp().wait_send()
    pltpu.make_async_copy(x_ref, mine, local_sem).wait()


def _a2a_copies(x_ref, out_ref, send_sems, recv_sems):
    x, y, c = _place()
    me = 4 * x + 2 * y + c
    copies = []
    for k in range(1, N_DEV):
        px = 1 - x if k & 4 else x
        py = 1 - y if k & 2 else y
        pc = 1 - c if k & 1 else c
        copies.append(pltpu.make_async_remote_copy(
            src_ref=x_ref.at[4 * px + 2 * py + pc], dst_ref=out_ref.at[me], send_sem=send_sems.at[k - 1],
            recv_sem=recv_sems.at[k - 1], device_id=(px, py, pc), device_id_type=MESH))
    return me, copies


def _a2a_start(x_ref, out_ref, send_sems, recv_sems, local_sem):
    me, copies = _a2a_copies(x_ref, out_ref, send_sems, recv_sems)
    pltpu.make_async_copy(x_ref.at[me], out_ref.at[me], local_sem).start()
    for cp in copies:
        cp.start()


def _a2a_finish(x_ref, out_ref, send_sems, recv_sems, local_sem):
    me, copies = _a2a_copies(x_ref, out_ref, send_sems, recv_sems)
    for cp in copies:
        cp.wait_recv()
    for cp in copies:
        cp.wait_send()
    pltpu.make_async_copy(x_ref.at[me], out_ref.at[me], local_sem).wait()


EXCHANGES = {"gather": (_gather_start, _gather_finish, lambda x: (N_DEV,) + x.shape),
             "a2a": (_a2a_start, _a2a_finish, lambda x: x.shape)}
EXCHANGE_SEMS = [pltpu.SemaphoreType.DMA((7,)), pltpu.SemaphoreType.DMA((7,)), pltpu.SemaphoreType.DMA(())]


def exchange(kind, x, *, name):
    start, finish, shape = EXCHANGES[kind]

    def body(x_ref, out_ref, *sems):
        start(x_ref, out_ref, *sems)
        finish(x_ref, out_ref, *sems)

    return pl.pallas_call(
        body,
        out_shape=jax.ShapeDtypeStruct(shape(x), x.dtype),
        in_specs=[pl.BlockSpec(memory_space=pl.ANY)],
        out_specs=pl.BlockSpec(memory_space=pl.ANY),
        scratch_shapes=EXCHANGE_SEMS,
        name=name,
    )(x)


def call_with_exchange(body, carried, *, out_shape, grid, in_specs, out_specs, scratch_shapes, name, args):
    sem = ("arbitrary",) * len(grid)
    if carried is None:
        res = pl.pallas_call(body, out_shape=out_shape, grid=grid, in_specs=in_specs, out_specs=out_specs,
                             scratch_shapes=scratch_shapes, compiler_params=_params(sem), name=name)(*args)
        return res, None
    kind, x = carried
    start, finish, shape = EXCHANGES[kind]
    n_in, n_out, n_scr = len(in_specs), len(out_shape), len(scratch_shapes)

    def wrapped(*refs):
        ins, x_ref = refs[:n_in], refs[n_in]
        outs, out_ref = refs[n_in + 1:n_in + 1 + n_out], refs[n_in + 1 + n_out]
        scratch, sems = refs[n_in + 2 + n_out:n_in + 2 + n_out + n_scr], refs[n_in + 2 + n_out + n_scr:]
        ids = [pl.program_id(i) for i in range(len(grid))]
        is_first = functools.reduce(lambda a, b: a & b, [i == 0 for i in ids])
        is_last = functools.reduce(lambda a, b: a & b, [i == g - 1 for i, g in zip(ids, grid)])

        @pl.when(is_first)
        def _():
            start(x_ref, out_ref, *sems)

        body(*ins, *outs, *scratch)

        @pl.when(is_last)
        def _():
            finish(x_ref, out_ref, *sems)

    any_spec = pl.BlockSpec(memory_space=pl.ANY)
    res = pl.pallas_call(
        wrapped,
        out_shape=tuple(out_shape) + (jax.ShapeDtypeStruct(shape(x), x.dtype),),
        grid=grid,
        in_specs=list(in_specs) + [any_spec],
        out_specs=tuple(out_specs) + (any_spec,),
        scratch_shapes=list(scratch_shapes) + EXCHANGE_SEMS,
        compiler_params=_params(sem),
        name=name + "_" + kind,
    )(*args, x)
    return res[:-1], res[-1]


def sum_slots(x, *, name):
    _, rows, cols = x.shape
    tr = _pick(rows, (512, 256, 128, 64, 32, 16))

    def body(x_ref, o_ref):
        acc = x_ref[0].astype(F32)
        for k in range(1, N_DEV):
            acc = acc + x_ref[k].astype(F32)
        o_ref[...] = acc

    return pl.pallas_call(
        body,
        out_shape=jax.ShapeDtypeStruct((rows, cols), F32),
        grid=(rows // tr,),
        in_specs=[pl.BlockSpec((N_DEV, tr, cols), lambda i: (0, i, 0))],
        out_specs=pl.BlockSpec((tr, cols), lambda i: (i, 0)),
        compiler_params=_params(("parallel",)),
        name=name,
    )(x)


BIG = ("w_in", "w_branch", "w_out", "w_ffn_in", "w_ffn_out")
SMALL = ("conv_b", "w_rg", "b_rg", "w_ig", "b_ig", "lru_lambda", "sinks", "ln1_g", "ln1_b", "ln2_g", "ln2_b")
N_LRU_BLOCKS = D_MODEL // HEAD_DIM
SMALL_ROWS_TILE = 512


def _block_diag(w):
    z = jnp.zeros((N_LRU_BLOCKS // 2, HEAD_DIM, HEAD_DIM), w.dtype)
    top = jnp.concatenate([w[0::2], z], axis=2)
    bot = jnp.concatenate([z, w[1::2]], axis=2)
    return jnp.concatenate([top, bot], axis=1)


def _block_diag_grad(g):
    return jnp.stack([g[:, :HEAD_DIM, :HEAD_DIM], g[:, HEAD_DIM:, HEAD_DIM:]], axis=1).reshape(N_LRU_BLOCKS, HEAD_DIM, HEAD_DIM)


def _carry(kind, arrays, i):
    return None if arrays is None else (kind, arrays[i])


def layer_fwd(x, xb, p, bsz, next_shards):
    t_dim = x.shape[0]
    s_len = t_dim // bsz
    w_f, w_qs, w_qd = p["w_in_f"], p["w_in_qs"], p["w_in_qd"]
    proj_f = matmul(xb, w_f, name="proj_f")
    qs = matmul(xb, w_qs, out_dtype=BF16, name="proj_qs").reshape(bsz, s_len, W_QS)
    qd = matmul(xb, w_qd, name="proj_qd").reshape(bsz, s_len, W_QD)
    proj_f3 = proj_f.reshape(bsz, s_len, W_F)
    wr_bd, wi_bd = _block_diag(p["w_rg"]), _block_diag(p["w_ig"])
    (y_a, h), got_in = lru_fwd(proj_f3, p["conv_w"], p["conv_b"], wr_bd, wi_bd, p["b_rg"], p["b_ig"], p["lru_lambda"],
                               name="lru_fwd", carried=_carry("gather", next_shards, 0))
    (y_b, lse_b, y_bb), got_fi = swa_fwd(qs, p["sinks"], name="swa_fwd", carried=_carry("gather", next_shards, 1))
    (y_c, lse_c, y_cb), got_rows = dil_fwd(qd, name="dil_fwd", carried=_carry("gather", next_shards, 2))
    ys = [t.reshape(t_dim, D_MODEL) for t in (y_a, y_bb, y_cb)]
    br = [matmul(ys[n], p["w_branch"][n], name="branch") for n in range(3)]
    merged = merge_fwd(proj_f, br, name="merge_fwd")
    mix = matmul(merged, p["w_out"], name="w_out")
    x1, x1b, z1 = ln_fwd(x, mix, p["ln1_g"], p["ln1_b"], name="ln_fwd")
    h13 = matmul(x1b, p["w_ffn_in"], name="ffn_in")
    act = swiglu_fwd(h13, name="swiglu_fwd")
    ffn = matmul(act, p["w_ffn_out"], name="ffn_out")
    x2, x2b, z2 = ln_fwd(x1, ffn, p["ln2_g"], p["ln2_b"], name="ln_fwd")
    saved = dict(xb=xb, proj_f=proj_f, qs=qs, qd=qd, h=h, ys=ys, y_b=y_b, y_c=y_c, lse_b=lse_b, lse_c=lse_c, br=br, merged=merged,
                 z1=z1, x1b=x1b, h13=h13, act=act, z2=z2, wr_bd=wr_bd, wi_bd=wi_bd)
    return x2, x2b, saved, (None if next_shards is None else (got_in, got_fi, got_rows))


def layer_bwd(dx2, p, s, bsz, prev_slots):
    t_dim = dx2.shape[0]
    s_len = t_dim // bsz
    g = {}
    dz2, dz2b, g["ln2_g"], g["ln2_b"] = ln_bwd(dx2, s["z2"], p["ln2_g"], name="ln_bwd")
    dact = matmul(dz2b, p["w_ffn_out"], trans_b=True, name="d_act")
    dh13 = swiglu_bwd(dact, s["h13"], name="swiglu_bwd")
    g["w_ffn_out"] = matmul(s["act"], dz2b, trans_a=True, name="dw_ffn_out")
    g["w_ffn_in"] = matmul(s["x1b"], dh13, trans_a=True, name="dw_ffn_in")
    dx1 = matmul(dh13, p["w_ffn_in"], trans_b=True, add=dz2, add_scale=ALPHA, name="dx_ffn")
    dz1, dz1b, g["ln1_g"], g["ln1_b"] = ln_bwd(dx1, s["z1"], p["ln1_g"], name="ln_bwd")
    dmerged = matmul(dz1b, p["w_out"], trans_b=True, name="d_merged")
    g["w_out"] = matmul(s["merged"], dz1b, trans_a=True, name="dw_out")
    *dbr, dgates = merge_bwd(dmerged, s["proj_f"], s["br"], name="merge_bwd")
    dys = [matmul(dbr[n], p["w_branch"][n], trans_b=True, name="d_branch") for n in range(3)]
    g["w_branch"] = jnp.stack([matmul(s["ys"][n], dbr[n], trans_a=True, name="dw_branch") for n in range(3)])
    shape3 = (bsz, s_len, D_MODEL)
    (dlx, dlg, g["conv_w"], g["conv_b"], g["b_rg"], g["b_ig"], g["lru_lambda"], dwr, dwi), got_rows = lru_bwd(
        dys[0].reshape(shape3), s["proj_f"].reshape(bsz, s_len, W_F), s["h"], p["conv_w"], p["conv_b"], s["wr_bd"], s["wi_bd"],
        jnp.swapaxes(s["wr_bd"], 1, 2), jnp.swapaxes(s["wi_bd"], 1, 2), p["b_rg"], p["b_ig"], p["lru_lambda"], name="lru_bwd",
        carried=_carry("a2a", prev_slots, 2))
    g["w_rg"], g["w_ig"] = _block_diag_grad(dwr), _block_diag_grad(dwi)
    dy_b3 = dys[1].reshape(shape3)
    dqs, got_fi = swa_bwd(s["qs"], s["y_b"], s["lse_b"], dy_b3, name="swa_bwd", carried=_carry("a2a", prev_slots, 1))
    sink_lanes = jnp.repeat(p["sinks"], HEAD_DIM).reshape(1, D_MODEL)
    g["sinks"] = sink_grad(dys[1], s["y_b"].reshape(t_dim, D_MODEL), s["lse_b"].reshape(t_dim, D_MODEL), sink_lanes,
                           name="sink_grad").reshape(N_HEADS, HEAD_DIM).sum(axis=1)
    dqd, got_in = dil_bwd(s["qd"], s["y_c"], s["lse_c"], dys[2].reshape(shape3), name="dil_bwd", carried=_carry("a2a", prev_slots, 0))
    flat = lambda t: t.reshape(t_dim, t.shape[-1])
    dproj_f = jnp.concatenate([flat(dlx), flat(dlg), dgates], axis=1)
    dproj_qs = jnp.concatenate([flat(t) for t in dqs], axis=1).astype(BF16)
    dproj_qd = jnp.concatenate([flat(t) for t in dqd], axis=1).astype(BF16)
    g["w_in_f"] = matmul(s["xb"], dproj_f, trans_a=True, name="dw_in_f")
    g["w_in_qs"] = matmul(s["xb"], dproj_qs, trans_a=True, name="dw_in_qs")
    g["w_in_qd"] = matmul(s["xb"], dproj_qd, trans_a=True, name="dw_in_qd")
    dx = matmul(dproj_f, p["w_in_f"], trans_b=True, add=dz1, add_scale=ALPHA, name="dx_f")
    dx = matmul(dproj_qs, p["w_in_qs"], trans_b=True, add=dx, name="dx_qs")
    dx = matmul(dproj_qd, p["w_in_qd"], trans_b=True, add=dx, name="dx_qd")
    g = {k: (v.reshape(p[k].shape) if k in p else v) for k, v in g.items()}
    return dx, g, (None if prev_slots is None else (got_in, got_fi, got_rows))


def local_step(x, target, layer_params, layer_shards=None, first_gathered=None, slots_of=None):
    bsz, s_len, d = x.shape
    t_dim = bsz * s_len
    xf = x.reshape(t_dim, d)
    xb = xf.astype(BF16)
    saved, gathered = [], first_gathered
    for l in range(DEPTH):
        p = layer_params(l, gathered)
        nxt = layer_shards[l + 1] if layer_shards is not None and l + 1 < DEPTH else None
        xf, xb, s, gathered = layer_fwd(xf, xb, p, bsz, nxt)
        saved.append((p, s))
    dy, sq = loss_head(xf, target.reshape(t_dim, d), name="loss_head")
    loss = 0.5 * jnp.sum(sq) / d
    grads, received, slots = [None] * DEPTH, [None] * DEPTH, None
    for l in reversed(range(DEPTH)):
        dy, grads[l], got = layer_bwd(dy, saved[l][0], saved[l][1], bsz, slots)
        if got is not None:
            received[l + 1] = got
        slots = slots_of(grads[l]) if slots_of is not None else None
    return loss, dy.reshape(bsz, s_len, d), grads, received, slots


W_IN_SEGMENTS = (("w_in_f", 0, 0, 2 * D_MODEL), ("w_in_qs", 0, 2 * D_MODEL, W_QS), ("w_in_qd", 0, 2 * D_MODEL + W_QS, W_QD),
                 ("w_in_f", 2 * D_MODEL, 2 * D_MODEL + W_QS + W_QD, 3 * D_MODEL))
ROW_SHARDED = ("w_branch", "w_out", "w_ffn_out")


def _cols_of_shards(shards, lo, hi):
    width = shards[0].shape[-1]
    parts = []
    for k, sh in enumerate(shards):
        a, b = max(lo, k * width), min(hi, (k + 1) * width)
        if a < b:
            parts.append(sh[..., a - k * width:b - k * width])
    return parts[0] if len(parts) == 1 else jnp.concatenate(parts, axis=-1)


def _cols_of_w_in(pieces, lo, hi):
    parts = []
    for name, p0, l0, width in W_IN_SEGMENTS:
        a, b = max(lo, l0), min(hi, l0 + width)
        if a < b:
            parts.append(pieces[name][..., p0 + a - l0:p0 + b - l0])
    return parts[0] if len(parts) == 1 else jnp.concatenate(parts, axis=-1)


W_IN_COLS = W_F + W_QS + W_QD


def _layer_shards(w, l):
    rows = jnp.concatenate([w[k][l].reshape(-1, D_MODEL) for k in ROW_SHARDED]).astype(BF16)
    return w["w_in"][l].astype(BF16), w["w_ffn_in"][l].astype(BF16), rows


def _layer_weights(gathered, row_counts):
    g_in, g_fi, g_rows = gathered
    sh = [g_in[k] for k in range(N_DEV)]
    p = dict(w_in_f=jnp.concatenate([_cols_of_shards(sh, 0, 2 * D_MODEL), _cols_of_shards(sh, W_IN_COLS - 3 * D_MODEL, W_IN_COLS)], axis=-1),
             w_in_qs=_cols_of_shards(sh, 2 * D_MODEL, 2 * D_MODEL + W_QS),
             w_in_qd=_cols_of_shards(sh, 2 * D_MODEL + W_QS, 2 * D_MODEL + W_QS + W_QD),
             w_ffn_in=jnp.concatenate([g_fi[k] for k in range(N_DEV)], axis=-1))
    off = 0
    for k, n in zip(ROW_SHARDED, row_counts):
        t = g_rows[:, off:off + n]
        if k == "w_branch":
            p[k] = jnp.transpose(t.reshape(N_DEV, 3, n // 3, D_MODEL), (1, 0, 2, 3)).reshape(3, -1, D_MODEL)
        else:
            p[k] = t.reshape(-1, D_MODEL)
        off += n
    return p


def _grad_slots(g):
    shard = W_IN_COLS // N_DEV
    s_in = jnp.stack([_cols_of_w_in(g, k * shard, (k + 1) * shard) for k in range(N_DEV)]).astype(BF16)
    shard = g["w_ffn_in"].shape[-1] // N_DEV
    s_fi = jnp.stack([g["w_ffn_in"][:, k * shard:(k + 1) * shard] for k in range(N_DEV)]).astype(BF16)
    rows = jnp.concatenate([jnp.transpose(g["w_branch"].reshape(3, N_DEV, -1, D_MODEL), (1, 0, 2, 3)).reshape(N_DEV, -1, D_MODEL),
                            g["w_out"].reshape(N_DEV, -1, D_MODEL), g["w_ffn_out"].reshape(N_DEV, -1, D_MODEL)], axis=1).astype(BF16)
    return s_in, s_fi, rows


def _pad_rows(flat, tile_rows):
    n = flat.shape[0]
    per = tile_rows * LANES
    total = -(-n // per) * per
    return jnp.pad(flat, (0, total - n)).reshape(-1, LANES)


def kernel(x, w_in, conv_w, conv_b, w_rg, b_rg, w_ig, b_ig, lru_lambda, sinks, w_branch, w_out, ln1_g, ln1_b, w_ffn_in, w_ffn_out, ln2_g, ln2_b, loss_target, m_w_in, m_conv_w, m_conv_b, m_w_rg, m_b_rg, m_w_ig, m_b_ig, m_lru_lambda, m_sinks, m_w_branch, m_w_out, m_ln1_g, m_ln1_b, m_w_ffn_in, m_w_ffn_out, m_ln2_g, m_ln2_b, v_w_in, v_conv_w, v_conv_b, v_w_rg, v_b_rg, v_w_ig, v_b_ig, v_lru_lambda, v_sinks, v_w_branch, v_w_out, v_ln1_g, v_ln1_b, v_w_ffn_in, v_w_ffn_out, v_ln2_g, v_ln2_b):
    w = dict(w_in=w_in, conv_w=conv_w, conv_b=conv_b, w_rg=w_rg, b_rg=b_rg, w_ig=w_ig, b_ig=b_ig, lru_lambda=lru_lambda, sinks=sinks,
             w_branch=w_branch, w_out=w_out, ln1_g=ln1_g, ln1_b=ln1_b, w_ffn_in=w_ffn_in, w_ffn_out=w_ffn_out, ln2_g=ln2_g, ln2_b=ln2_b)
    m = dict(w_in=m_w_in, conv_w=m_conv_w, conv_b=m_conv_b, w_rg=m_w_rg, b_rg=m_b_rg, w_ig=m_w_ig, b_ig=m_b_ig, lru_lambda=m_lru_lambda,
             sinks=m_sinks, w_branch=m_w_branch, w_out=m_w_out, ln1_g=m_ln1_g, ln1_b=m_ln1_b, w_ffn_in=m_w_ffn_in, w_ffn_out=m_w_ffn_out,
             ln2_g=m_ln2_g, ln2_b=m_ln2_b)
    v = dict(w_in=v_w_in, conv_w=v_conv_w, conv_b=v_conv_b, w_rg=v_w_rg, b_rg=v_b_rg, w_ig=v_w_ig, b_ig=v_b_ig, lru_lambda=v_lru_lambda,
             sinks=v_sinks, w_branch=v_w_branch, w_out=v_w_out, ln1_g=v_ln1_g, ln1_b=v_ln1_b, w_ffn_in=v_w_ffn_in, w_ffn_out=v_w_ffn_out,
             ln2_g=v_ln2_g, ln2_b=v_ln2_b)
    order = ["w_in", "conv_w", "conv_b", "w_rg", "b_rg", "w_ig", "b_ig", "lru_lambda", "sinks", "w_branch", "w_out", "ln1_g", "ln1_b",
             "w_ffn_in", "w_ffn_out", "ln2_g", "ln2_b"]
    me = 4 * lax.axis_index("x") + 2 * lax.axis_index("y") + lax.axis_index("c")

    names = ("w_in", "w_ffn_in", "w_rows")
    shards = [_layer_shards(w, l) for l in range(DEPTH)]
    row_counts = [w[k][0].size // D_MODEL for k in ROW_SHARDED]
    first_gathered = tuple(exchange("gather", t, name=f"gather_{n}") for t, n in zip(shards[0], names))
    cw = exchange("gather", conv_w.reshape(-1, LANES), name="gather_conv_w")
    conv_w_full = jnp.moveaxis(cw.reshape(N_DEV, DEPTH, CONV_WIDTH, LANES), 0, 2).reshape(DEPTH, CONV_WIDTH, D_MODEL)

    def layer_params(l, gathered):
        return {**_layer_weights(gathered, row_counts), **{k: w[k][l] for k in SMALL}, "conv_w": conv_w_full[l]}

    loss_local, grad_x, grads, received, slots = local_step(x, loss_target, layer_params, shards, first_gathered, _grad_slots)
    loss = lax.psum(loss_local, ("x", "y", "c"))
    received[0] = tuple(exchange("a2a", t, name=f"exchange_g_{n}") for t, n in zip(slots, names))

    sums = [[sum_slots(t, name=f"sum_g_{n}") for t, n in zip(received[l], names)] for l in range(DEPTH)]
    g_final = {"w_in": jnp.stack([sums[l][0] for l in range(DEPTH)]), "w_ffn_in": jnp.stack([sums[l][1] for l in range(DEPTH)])}
    off = 0
    for k, n in zip(ROW_SHARDED, row_counts):
        g_final[k] = jnp.stack([sums[l][2][off:off + n] for l in range(DEPTH)]).reshape(w[k].shape)
        off += n
    grads = {k: jnp.stack([grads[l][k] for l in range(DEPTH)]) for k in list(SMALL) + ["conv_w"]}

    small_names = list(SMALL) + ["conv_w"]
    small_sizes = [grads[k].size for k in small_names]
    svec = _pad_rows(jnp.concatenate([grads[k].reshape(-1) for k in small_names]), SMALL_ROWS_TILE)
    ssum = sum_slots(exchange("gather", svec, name="gather_small_grads"), name="sum_small_grads")
    sflat, off = ssum.reshape(-1), 0
    for k, n in zip(small_names, small_sizes):
        g_final[k] = sflat[off:off + n].reshape(grads[k].shape)
        off += n
    g_final["conv_w"] = lax.dynamic_slice_in_dim(g_final["conv_w"], me * LANES, LANES, axis=2)

    delta, new_m, new_v = {}, {}, {}
    for k in list(BIG) + ["conv_w"]:
        cols = w[k].shape[-1]
        two_d = lambda t: t.reshape(-1, cols)
        d_, m_, v_ = adamw(two_d(w[k]), two_d(g_final[k]), two_d(m[k]), two_d(v[k]), name=f"adamw_{k}")
        delta[k], new_m[k], new_v[k] = d_.reshape(w[k].shape), m_.reshape(w[k].shape), v_.reshape(w[k].shape)
    pack_small = lambda dct: _pad_rows(jnp.concatenate([dct[k].reshape(-1) for k in SMALL]), SMALL_ROWS_TILE)
    d_, m_, v_ = adamw(pack_small(w), pack_small(g_final), pack_small(m), pack_small(v), name="adamw_small")
    off = 0
    for k in SMALL:
        n = w[k].size
        for dst, src in ((delta, d_), (new_m, m_), (new_v, v_)):
            dst[k] = src.reshape(-1)[off:off + n].reshape(w[k].shape)
        off += n
    return (loss, grad_x, *[g_final[k] for k in order], *[delta[k] for k in order], *[new_m[k] for k in order], *[new_v[k] for k in order])
```

```python
import functools
import math

import jax
import jax.numpy as jnp
from jax import lax
from jax.experimental import pallas as pl
from jax.experimental.pallas import tpu as pltpu

F32 = jnp.float32
BF16 = jnp.bfloat16

N_DEV = 8
DEPTH = 4
D_MODEL = 1024
HEAD_DIM = 64
LANES = 128
N_HEADS = D_MODEL // HEAD_DIM
SWA_KV_HEADS = 4
ATT_BLOCK = 128
DILATIONS = (1, 4, 16)
CONV_WIDTH = 4
LRU_C = 8.0
FF_HIDDEN = 2816
ALPHA = (2.0 * DEPTH) ** 0.25
LN_EPS = 1e-5
NEG_INF = -1e30
W_F = 5 * D_MODEL
W_QS = D_MODEL + 2 * SWA_KV_HEADS * HEAD_DIM
W_QD = 3 * D_MODEL

ADAM_LR = 0.001
ADAM_B1 = 0.9
ADAM_B2 = 0.999
ADAM_EPS = 1e-08
ADAM_WD = 0.01
ADAM_STEP = 10

VMEM_LIMIT = 56 * 1024 * 1024
MESH = pl.DeviceIdType.MESH


def _pick(n, cands):
    for c in cands:
        if n % c == 0:
            return c
    raise ValueError(f"no tile for {n} among {cands}")


def _params(sem):
    return pltpu.CompilerParams(dimension_semantics=sem, vmem_limit_bytes=VMEM_LIMIT)


def _tile(n, cap):
    best = None
    for t in range(LANES, cap + 1, LANES):
        if n % t == 0:
            best = t
    assert best is not None, (n, cap)
    return best


def matmul(a, b, *, name, trans_a=False, trans_b=False, out_dtype=F32, add=None, add_scale=1.0):
    if trans_a:
        k_dim, m_dim = a.shape
    else:
        m_dim, k_dim = a.shape
    n_dim = b.shape[0] if trans_b else b.shape[1]
    assert (b.shape[1] if trans_b else b.shape[0]) == k_dim
    tm = _tile(m_dim, 1024)
    tn = _tile(n_dim, 1408)
    tk = _tile(k_dim, 1408)
    nk = k_dim // tk
    dims = (((0 if trans_a else 1,), (1 if trans_b else 0,)), ((), ()))

    def body(*refs):
        if add is None:
            a_ref, b_ref, o_ref, acc_ref = refs
            add_ref = None
        else:
            a_ref, b_ref, add_ref, o_ref, acc_ref = refs
        k = pl.program_id(2)
        part = lax.dot_general(a_ref[...].astype(BF16), b_ref[...].astype(BF16), dims, preferred_element_type=F32)

        def finish(r):
            if add_ref is not None:
                r = r + add_scale * add_ref[...].astype(F32)
            o_ref[...] = r.astype(out_dtype)

        if nk == 1:
            finish(part)
        else:
            @pl.when(k == 0)
            def _():
                acc_ref[...] = part

            @pl.when((k > 0) & (k < nk - 1))
            def _():
                acc_ref[...] += part

            @pl.when(k == nk - 1)
            def _():
                finish(acc_ref[...] + part)

    a_spec = pl.BlockSpec((tk, tm), lambda i, j, k: (k, i)) if trans_a else pl.BlockSpec((tm, tk), lambda i, j, k: (i, k))
    b_spec = pl.BlockSpec((tn, tk), lambda i, j, k: (j, k)) if trans_b else pl.BlockSpec((tk, tn), lambda i, j, k: (k, j))
    in_specs = [a_spec, b_spec]
    args = [a, b]
    if add is not None:
        in_specs.append(pl.BlockSpec((tm, tn), lambda i, j, k: (i, j)))
        args.append(add)
    return pl.pallas_call(
        body,
        out_shape=jax.ShapeDtypeStruct((m_dim, n_dim), out_dtype),
        grid=(m_dim // tm, n_dim // tn, nk),
        in_specs=in_specs,
        out_specs=pl.BlockSpec((tm, tn), lambda i, j, k: (i, j)),
        scratch_shapes=[pltpu.VMEM((tm, tn) if nk > 1 else (8, LANES), F32)],
        compiler_params=_params(("parallel", "parallel", "arbitrary")),
        name=name,
    )(*args)


def ln_fwd(x, r, g, b, *, name):
    t_dim, d = x.shape
    tr = _pick(t_dim, (256, 128, 8))

    def body(x_ref, r_ref, g_ref, b_ref, y_ref, yb_ref, z_ref):
        z = ALPHA * x_ref[...] + r_ref[...]
        mu = jnp.mean(z, axis=-1, keepdims=True)
        zc = z - mu
        var = jnp.mean(zc * zc, axis=-1, keepdims=True)
        y = zc * lax.rsqrt(var + LN_EPS) * g_ref[...] + b_ref[...]
        y_ref[...] = y
        yb_ref[...] = y.astype(BF16)
        z_ref[...] = z

    row = pl.BlockSpec((tr, d), lambda i: (i, 0))
    vec = pl.BlockSpec((1, d), lambda i: (0, 0))
    return pl.pallas_call(
        body,
        out_shape=(jax.ShapeDtypeStruct((t_dim, d), F32), jax.ShapeDtypeStruct((t_dim, d), BF16), jax.ShapeDtypeStruct((t_dim, d), F32)),
        grid=(t_dim // tr,),
        in_specs=[row, row, vec, vec],
        out_specs=(row, row, row),
        compiler_params=_params(("parallel",)),
        name=name,
    )(x, r, g.reshape(1, d), b.reshape(1, d))


def ln_bwd(dy, z, g, *, name):
    t_dim, d = dy.shape
    tr = _pick(t_dim, (256, 128, 8))

    def body(dy_ref, z_ref, g_ref, dz_ref, dzb_ref, dg_ref, db_ref):
        @pl.when(pl.program_id(0) == 0)
        def _():
            dg_ref[...] = jnp.zeros_like(dg_ref)
            db_ref[...] = jnp.zeros_like(db_ref)

        z = z_ref[...]
        dyv = dy_ref[...]
        mu = jnp.mean(z, axis=-1, keepdims=True)
        zc = z - mu
        var = jnp.mean(zc * zc, axis=-1, keepdims=True)
        rstd = lax.rsqrt(var + LN_EPS)
        xhat = zc * rstd
        dxhat = dyv * g_ref[...]
        m1 = jnp.mean(dxhat, axis=-1, keepdims=True)
        m2 = jnp.mean(dxhat * xhat, axis=-1, keepdims=True)
        dz = rstd * (dxhat - m1 - xhat * m2)
        dz_ref[...] = dz
        dzb_ref[...] = dz.astype(BF16)
        dg_ref[...] += jnp.sum(dyv * xhat, axis=0, keepdims=True)
        db_ref[...] += jnp.sum(dyv, axis=0, keepdims=True)

    row = pl.BlockSpec((tr, d), lambda i: (i, 0))
    vec = pl.BlockSpec((1, d), lambda i: (0, 0))
    return pl.pallas_call(
        body,
        out_shape=(jax.ShapeDtypeStruct((t_dim, d), F32), jax.ShapeDtypeStruct((t_dim, d), BF16),
                   jax.ShapeDtypeStruct((1, d), F32), jax.ShapeDtypeStruct((1, d), F32)),
        grid=(t_dim // tr,),
        in_specs=[row, row, vec],
        out_specs=(row, row, vec, vec),
        compiler_params=_params(("arbitrary",)),
        name=name,
    )(dy, z, g.reshape(1, d))


def loss_head(y, target, *, name):
    t_dim, d = y.shape
    tr = _pick(t_dim, (256, 128, 8))

    def body(y_ref, t_ref, dy_ref, sq_ref):
        @pl.when(pl.program_id(0) == 0)
        def _():
            sq_ref[...] = jnp.zeros_like(sq_ref)

        diff = y_ref[...] - t_ref[...]
        dy_ref[...] = diff / d
        sq_ref[...] += jnp.sum(diff * diff, axis=0, keepdims=True)

    row = pl.BlockSpec((tr, d), lambda i: (i, 0))
    vec = pl.BlockSpec((1, d), lambda i: (0, 0))
    return pl.pallas_call(
        body,
        out_shape=(jax.ShapeDtypeStruct((t_dim, d), F32), jax.ShapeDtypeStruct((1, d), F32)),
        grid=(t_dim // tr,),
        in_specs=[row, row],
        out_specs=(row, vec),
        compiler_params=_params(("arbitrary",)),
        name=name,
    )(y, target)


def _sigmoid(x):
    return 1.0 / (1.0 + jnp.exp(-x))


def swiglu_fwd(h13, *, name):
    t_dim = h13.shape[0]
    f = h13.shape[1] // 2
    tr = _pick(t_dim, (256, 128, 8))

    def body(h1_ref, h3_ref, act_ref):
        h1 = h1_ref[...]
        act_ref[...] = (h1 * _sigmoid(h1) * h3_ref[...]).astype(BF16)

    return pl.pallas_call(
        body,
        out_shape=jax.ShapeDtypeStruct((t_dim, f), BF16),
        grid=(t_dim // tr,),
        in_specs=[pl.BlockSpec((tr, f), lambda i: (i, 0)), pl.BlockSpec((tr, f), lambda i: (i, 1))],
        out_specs=pl.BlockSpec((tr, f), lambda i: (i, 0)),
        compiler_params=_params(("parallel",)),
        name=name,
    )(h13, h13)


def swiglu_bwd(dact, h13, *, name):
    t_dim = h13.shape[0]
    f = h13.shape[1] // 2
    tr = _pick(t_dim, (256, 128, 8))

    def body(da_ref, h1_ref, h3_ref, dh_ref):
        h1 = h1_ref[...]
        da = da_ref[...]
        sg = _sigmoid(h1)
        dh_ref[:, :f] = (da * h3_ref[...] * sg * (1.0 + h1 * (1.0 - sg))).astype(BF16)
        dh_ref[:, f:] = (da * h1 * sg).astype(BF16)

    return pl.pallas_call(
        body,
        out_shape=jax.ShapeDtypeStruct((t_dim, 2 * f), BF16),
        grid=(t_dim // tr,),
        in_specs=[pl.BlockSpec((tr, f), lambda i: (i, 0)), pl.BlockSpec((tr, f), lambda i: (i, 0)),
                  pl.BlockSpec((tr, f), lambda i: (i, 1))],
        out_specs=pl.BlockSpec((tr, 2 * f), lambda i: (i, 0)),
        compiler_params=_params(("parallel",)),
        name=name,
    )(dact, h13, h13)


def merge_fwd(proj_f, br, *, name):
    t_dim, d = br[0].shape
    tr = _pick(t_dim, (256, 128, 8))

    def body(g0, g1, g2, b0, b1, b2, o_ref):
        o_ref[...] = (_sigmoid(g0[...]) * b0[...] + _sigmoid(g1[...]) * b1[...] + _sigmoid(g2[...]) * b2[...]).astype(BF16)

    row = pl.BlockSpec((tr, d), lambda i: (i, 0))
    gate = [pl.BlockSpec((tr, d), functools.partial(lambda n, i: (i, 2 + n), n)) for n in range(3)]
    return pl.pallas_call(
        body,
        out_shape=jax.ShapeDtypeStruct((t_dim, d), BF16),
        grid=(t_dim // tr,),
        in_specs=gate + [row, row, row],
        out_specs=row,
        compiler_params=_params(("parallel",)),
        name=name,
    )(proj_f, proj_f, proj_f, *br)


def merge_bwd(dmerged, proj_f, br, *, name):
    t_dim, d = dmerged.shape
    tr = _pick(t_dim, (256, 128, 8))

    def body(dm_ref, g0, g1, g2, b0, b1, b2, d0, d1, d2, dg_ref):
        dm = dm_ref[...]
        for n, (g, b, o) in enumerate(((g0, b0, d0), (g1, b1, d1), (g2, b2, d2))):
            sg = _sigmoid(g[...])
            o[...] = (dm * sg).astype(BF16)
            dg_ref[:, n * d:(n + 1) * d] = (dm * b[...] * sg * (1.0 - sg)).astype(BF16)

    row = pl.BlockSpec((tr, d), lambda i: (i, 0))
    gate = [pl.BlockSpec((tr, d), functools.partial(lambda n, i: (i, 2 + n), n)) for n in range(3)]
    return pl.pallas_call(
        body,
        out_shape=(jax.ShapeDtypeStruct((t_dim, d), BF16),) * 3 + (jax.ShapeDtypeStruct((t_dim, 3 * d), BF16),),
        grid=(t_dim // tr,),
        in_specs=[row] + gate + [row, row, row],
        out_specs=(row, row, row, pl.BlockSpec((tr, 3 * d), lambda i: (i, 0))),
        compiler_params=_params(("parallel",)),
        name=name,
    )(dmerged, proj_f, proj_f, proj_f, *br)


GELU_C = math.sqrt(2.0 / math.pi)
PAD = 8
SCAN_TILES = 8


def _gelu(x):
    return 0.5 * x * (1.0 + jnp.tanh(GELU_C * (x + 0.044715 * x * x * x)))


def _gelu_grad(x):
    t = jnp.tanh(GELU_C * (x + 0.044715 * x * x * x))
    return 0.5 * (1.0 + t) + 0.5 * x * (1.0 - t * t) * GELU_C * (1.0 + 3.0 * 0.044715 * x * x)


def _neg_expm1(x):
    series = -x * (1.0 + x * (0.5 + x * (1.0 / 6.0 + x * (1.0 / 24.0 + x * (1.0 / 120.0)))))
    return jnp.where(x > -0.1, series, 1.0 - jnp.exp(x))


def _lru_gates(xv, cw_ref, cb_ref, wr_ref, wi_ref, br_ref, bi_ref, lam_ref, pad_ref, s_len):
    pad_ref[pl.ds(0, PAD), :] = jnp.zeros((PAD, LANES), F32)
    pad_ref[pl.ds(PAD, s_len), :] = xv
    xc = cb_ref[...] + jnp.zeros((s_len, LANES), F32)
    for j in range(CONV_WIDTH):
        xc = xc + pad_ref[pl.ds(PAD - (CONV_WIDTH - 1) + j, s_len), :] * cw_ref[pl.ds(j, 1), :]
    xcb = xc.astype(BF16)
    r = _sigmoid(jnp.dot(xcb, wr_ref[0].astype(BF16), preferred_element_type=F32) + br_ref[...])
    i = _sigmoid(jnp.dot(xcb, wi_ref[0].astype(BF16), preferred_element_type=F32) + bi_ref[...])
    nl = -lam_ref[...]
    sp = jnp.maximum(nl, 0.0) + jnp.log(1.0 + jnp.exp(-jnp.abs(nl)))
    log_a = -LRU_C * r * sp
    a = jnp.exp(log_a)
    mult = jnp.sqrt(_neg_expm1(2.0 * log_a))
    return xc, r, i, sp, a, mult


def _tile_scan(a, b, row, reverse):
    for s in (1, 2, 4):
        if reverse:
            a_sh = pltpu.roll(a, 8 - s, 0)
            b_sh = pltpu.roll(b, 8 - s, 0)
            m = row + s <= 7
        else:
            a_sh = pltpu.roll(a, s, 0)
            b_sh = pltpu.roll(b, s, 0)
            m = row >= s
        b = jnp.where(m, a * b_sh + b, b)
        a = jnp.where(m, a * a_sh, a)
    return a, b


def lru_fwd(proj_f, conv_w, conv_b, wr_bd, wi_bd, b_rg, b_ig, lam, *, name, carried=None):
    bsz, s_len, _ = proj_f.shape
    d = D_MODEL
    ncb = d // LANES
    n_tiles = s_len // 8

    def body(x_ref, g_ref, cw_ref, cb_ref, wr_ref, wi_ref, br_ref, bi_ref, lam_ref, y_ref, h_ref, pad_ref, a_s, b_s):
        xc, r, i, sp, a, mult = _lru_gates(x_ref[0], cw_ref, cb_ref, wr_ref, wi_ref, br_ref, bi_ref, lam_ref, pad_ref, s_len)
        a_s[...] = a
        b_s[...] = mult * (i * xc)
        row = lax.broadcasted_iota(jnp.int32, (8, LANES), 0)

        def tiles(t, carry):
            starts = [pl.multiple_of((t * SCAN_TILES + u) * 8, 8) for u in range(SCAN_TILES)]
            local = [_tile_scan(a_s[pl.ds(i0, 8), :], b_s[pl.ds(i0, 8), :], row, False) for i0 in starts]
            for i0, (ac, hl) in zip(starts, local):
                h = hl + ac * carry
                h_ref[0, pl.ds(i0, 8), :] = h
                carry = jnp.broadcast_to(h[7:8, :], (8, LANES))
            return carry

        lax.fori_loop(0, n_tiles // SCAN_TILES, tiles, jnp.zeros((8, LANES), F32))
        y_ref[0] = (h_ref[0] * _gelu(g_ref[0])).astype(BF16)

    slab = lambda off: pl.BlockSpec((1, s_len, LANES), functools.partial(lambda o, c, b: (b, 0, o + c), off))
    vec = pl.BlockSpec((1, LANES), lambda c, b: (0, c))
    mat = pl.BlockSpec((1, LANES, LANES), lambda c, b: (c, 0, 0))
    out = pl.BlockSpec((1, s_len, LANES), lambda c, b: (b, 0, c))
    return call_with_exchange(
        body, carried,
        out_shape=(jax.ShapeDtypeStruct((bsz, s_len, d), BF16), jax.ShapeDtypeStruct((bsz, s_len, d), F32)),
        grid=(ncb, bsz),
        in_specs=[slab(0), slab(ncb), pl.BlockSpec((CONV_WIDTH, LANES), lambda c, b: (0, c)), vec, mat, mat, vec, vec, vec],
        out_specs=(out, out),
        scratch_shapes=[pltpu.VMEM((s_len + 2 * PAD, LANES), F32), pltpu.VMEM((s_len, LANES), F32), pltpu.VMEM((s_len, LANES), F32)],
        name=name,
        args=(proj_f, proj_f, conv_w, conv_b.reshape(1, d), wr_bd, wi_bd, b_rg.reshape(1, d), b_ig.reshape(1, d), lam.reshape(1, d)))


def lru_bwd(dy, proj_f, h, conv_w, conv_b, wr_bd, wi_bd, wr_bd_t, wi_bd_t, b_rg, b_ig, lam, *, name, carried=None):
    bsz, s_len, _ = proj_f.shape
    d = D_MODEL
    ncb = d // LANES
    n_tiles = s_len // 8

    def body(dy_ref, x_ref, g_ref, h_ref, cw_ref, cb_ref, wr_ref, wi_ref, wrt_ref, wit_ref, br_ref, bi_ref, lam_ref,
             dx_ref, dg_ref, dcw_ref, dcb_ref, dbr_ref, dbi_ref, dlam_ref, dwr_ref, dwi_ref, pad_ref, a_s, b_s, l_s):
        @pl.when(pl.program_id(1) == 0)
        def _():
            for ref in (dcw_ref, dcb_ref, dbr_ref, dbi_ref, dlam_ref, dwr_ref, dwi_ref):
                ref[...] = jnp.zeros_like(ref)

        xc, r, i, sp, a, mult = _lru_gates(x_ref[0], cw_ref, cb_ref, wr_ref, wi_ref, br_ref, bi_ref, lam_ref, pad_ref, s_len)
        gate = g_ref[0]
        hv = h_ref[0]
        dyv = dy_ref[0]
        dg_ref[0] = (dyv * hv * _gelu_grad(gate)).astype(BF16)
        b_s[...] = dyv * _gelu(gate)
        l_s[pl.ds(0, s_len), :] = a
        l_s[pl.ds(s_len, PAD), :] = jnp.zeros((PAD, LANES), F32)
        a_s[...] = l_s[pl.ds(1, s_len), :]
        row = lax.broadcasted_iota(jnp.int32, (8, LANES), 0)

        def tiles(t, carry):
            starts = [pl.multiple_of((n_tiles - 1 - (t * SCAN_TILES + u)) * 8, 8) for u in range(SCAN_TILES)]
            local = [_tile_scan(a_s[pl.ds(i0, 8), :], b_s[pl.ds(i0, 8), :], row, True) for i0 in starts]
            for i0, (ac, ll) in zip(starts, local):
                lmb = ll + ac * carry
                b_s[pl.ds(i0, 8), :] = lmb
                carry = jnp.broadcast_to(lmb[0:1, :], (8, LANES))
            return carry

        lax.fori_loop(0, n_tiles // SCAN_TILES, tiles, jnp.zeros((8, LANES), F32))
        lmb = b_s[...]
        l_s[pl.ds(0, PAD), :] = jnp.zeros((PAD, LANES), F32)
        l_s[pl.ds(PAD, s_len), :] = hv
        h_prev = l_s[pl.ds(PAD - 1, s_len), :]
        da = lmb * h_prev
        dmult = lmb * (i * xc)
        di = lmb * mult * xc
        dxc = lmb * mult * i
        dlog_a = da * a - dmult * a * a / mult
        dr = -LRU_C * sp * dlog_a
        dsp = jnp.sum(-LRU_C * r * dlog_a, axis=0, keepdims=True)
        dlam_ref[...] += dsp * (-_sigmoid(-lam_ref[...]))
        dpr = dr * r * (1.0 - r)
        dpi = di * i * (1.0 - i)
        dprb = dpr.astype(BF16)
        dpib = dpi.astype(BF16)
        xcb = xc.astype(BF16)
        dbr_ref[...] += jnp.sum(dpr, axis=0, keepdims=True)
        dbi_ref[...] += jnp.sum(dpi, axis=0, keepdims=True)
        tn = (((0,), (0,)), ((), ()))
        dwr_ref[0] += lax.dot_general(xcb, dprb, tn, preferred_element_type=F32)
        dwi_ref[0] += lax.dot_general(xcb, dpib, tn, preferred_element_type=F32)
        dxc = (dxc + jnp.dot(dprb, wrt_ref[0].astype(BF16), preferred_element_type=F32)
               + jnp.dot(dpib, wit_ref[0].astype(BF16), preferred_element_type=F32))
        dcb_ref[...] += jnp.sum(dxc, axis=0, keepdims=True)
        for j in range(CONV_WIDTH):
            dcw_ref[pl.ds(j, 1), :] += jnp.sum(dxc * pad_ref[pl.ds(PAD - (CONV_WIDTH - 1) + j, s_len), :], axis=0, keepdims=True)
        l_s[pl.ds(0, s_len), :] = dxc
        l_s[pl.ds(s_len, PAD), :] = jnp.zeros((PAD, LANES), F32)
        dx = jnp.zeros((s_len, LANES), F32)
        for j in range(CONV_WIDTH):
            dx = dx + l_s[pl.ds(CONV_WIDTH - 1 - j, s_len), :] * cw_ref[pl.ds(j, 1), :]
        dx_ref[0] = dx.astype(BF16)

    slab = lambda off: pl.BlockSpec((1, s_len, LANES), functools.partial(lambda o, c, b: (b, 0, o + c), off))
    vec = pl.BlockSpec((1, LANES), lambda c, b: (0, c))
    mat = pl.BlockSpec((1, LANES, LANES), lambda c, b: (c, 0, 0))
    cw = pl.BlockSpec((CONV_WIDTH, LANES), lambda c, b: (0, c))
    out = pl.BlockSpec((1, s_len, LANES), lambda c, b: (b, 0, c))
    vshape = jax.ShapeDtypeStruct((1, d), F32)
    mshape = jax.ShapeDtypeStruct((ncb, LANES, LANES), F32)
    return call_with_exchange(
        body, carried,
        out_shape=(jax.ShapeDtypeStruct((bsz, s_len, d), BF16),) * 2
        + (jax.ShapeDtypeStruct((CONV_WIDTH, d), F32), vshape, vshape, vshape, vshape, mshape, mshape),
        grid=(ncb, bsz),
        in_specs=[out, slab(0), slab(ncb), out, cw, vec, mat, mat, mat, mat, vec, vec, vec],
        out_specs=(out, out, cw, vec, vec, vec, vec, mat, mat),
        scratch_shapes=[pltpu.VMEM((s_len + 2 * PAD, LANES), F32), pltpu.VMEM((s_len, LANES), F32), pltpu.VMEM((s_len, LANES), F32),
                        pltpu.VMEM((s_len + 2 * PAD, LANES), F32)],
        name=name,
        args=(dy, proj_f, proj_f, h, conv_w, conv_b.reshape(1, d), wr_bd, wi_bd, wr_bd_t, wi_bd_t,
              b_rg.reshape(1, d), b_ig.reshape(1, d), lam.reshape(1, d)))


def _kv_place(head, n_kv_heads):
    kv = head // (N_HEADS // n_kv_heads)
    return kv // 2, kv % 2


def _band_mask(n, single):
    if single:
        qi = lax.broadcasted_iota(jnp.int32, (ATT_BLOCK, ATT_BLOCK), 0)
        return qi >= lax.broadcasted_iota(jnp.int32, (ATT_BLOCK, ATT_BLOCK), 1)
    qi = lax.broadcasted_iota(jnp.int32, (ATT_BLOCK, 2 * ATT_BLOCK), 0)
    kj = lax.broadcasted_iota(jnp.int32, (ATT_BLOCK, 2 * ATT_BLOCK), 1)
    rel = qi + ATT_BLOCK - kj
    return (rel >= 0) & (rel <= ATT_BLOCK) & ((n > 0) | (kj >= ATT_BLOCK))


def _half_masks(dtype):
    lane = lax.broadcasted_iota(jnp.int32, (1, LANES), 1)
    return [(lane < HEAD_DIM).astype(dtype), (lane >= HEAD_DIM).astype(dtype)]


NT = (((1,), (1,)), ((), ()))
TN = (((0,), (0,)), ((), ()))


def _qkv_specs(dil, q_blk, k_blk, v_blk, ckv, clamp):
    qw = D_MODEL // LANES * LANES
    return [
        pl.BlockSpec((1, ATT_BLOCK, qw), lambda b, j, n: (b, clamp(n), j * (q_blk[1]) + q_blk[0])),
        pl.BlockSpec((1, ATT_BLOCK, ckv), lambda b, j, n: (b, jnp.maximum(clamp(n) - 1, 0), j * k_blk[1] + k_blk[0])),
        pl.BlockSpec((1, ATT_BLOCK, ckv), lambda b, j, n: (b, clamp(n), j * k_blk[1] + k_blk[0])),
        pl.BlockSpec((1, ATT_BLOCK, ckv), lambda b, j, n: (b, jnp.maximum(clamp(n) - 1, 0), j * v_blk[1] + v_blk[0])),
        pl.BlockSpec((1, ATT_BLOCK, ckv), lambda b, j, n: (b, clamp(n), j * v_blk[1] + v_blk[0])),
    ]


def attn_fwd(qkv, *, dil, n_kv_heads, sinks, name, emit_bf16=False):
    bsz, s_len, width = qkv.shape
    ckv = n_kv_heads * HEAD_DIM
    l_sub = s_len // dil
    nb = l_sub // ATT_BLOCK
    view = qkv.reshape(bsz, l_sub, dil * width)
    scale = HEAD_DIM ** -0.5
    q_blk = (0, width // D_MODEL)
    k_blk = (D_MODEL // ckv, width // ckv)
    v_blk = (D_MODEL // ckv + 1, width // ckv)
    assert (dil == 1 or width % D_MODEL == 0) and width % ckv == 0 and D_MODEL % ckv == 0

    single = nb == 1

    def body(*refs):
        refs = list(refs)
        sink_ref = refs.pop(0) if sinks is not None else None
        ob_ref = refs.pop() if emit_bf16 else None
        q_ref, kp_ref, kc_ref, vp_ref, vc_ref, o_ref, lse_ref = refs
        n = pl.program_id(2)
        mask = _band_mask(n, single)
        hm = _half_masks(BF16)
        hmf = _half_masks(F32)
        kk = kc_ref[0] if single else jnp.concatenate([kp_ref[0], kc_ref[0]], axis=0)
        vv = vc_ref[0] if single else jnp.concatenate([vp_ref[0], vc_ref[0]], axis=0)
        for hp in range(N_HEADS // 2):
            q2 = q_ref[0, :, hp * LANES:(hp + 1) * LANES]
            o2 = jnp.zeros((ATT_BLOCK, LANES), F32)
            l2 = jnp.zeros((ATT_BLOCK, LANES), F32)
            for a in range(2):
                kb, kh = _kv_place(2 * hp + a, n_kv_heads)
                k2 = kk[:, kb * LANES:(kb + 1) * LANES]
                v2 = vv[:, kb * LANES:(kb + 1) * LANES]
                if kh != a:
                    k2 = pltpu.roll(k2, HEAD_DIM, 1)
                    v2 = pltpu.roll(v2, HEAD_DIM, 1)
                s = lax.dot_general(q2 * hm[a], k2, NT, preferred_element_type=F32) * scale
                s = jnp.where(mask, s, NEG_INF)
                m = jnp.max(s, axis=-1, keepdims=True)
                if sink_ref is not None:
                    sk = sink_ref[2 * hp + a]
                    m = jnp.maximum(m, sk)
                p = jnp.exp(s - m)
                den = jnp.sum(p, axis=-1, keepdims=True)
                if sink_ref is not None:
                    den = den + jnp.exp(sk - m)
                o2 = o2 + jnp.dot(p.astype(BF16), v2 * hm[a], preferred_element_type=F32) / den
                l2 = l2 + (m + jnp.log(den)) * hmf[a]
            o_ref[0, :, hp * LANES:(hp + 1) * LANES] = o2
            lse_ref[0, :, hp * LANES:(hp + 1) * LANES] = l2
            if ob_ref is not None:
                ob_ref[0, :, hp * LANES:(hp + 1) * LANES] = o2.astype(BF16)

    in_specs = _qkv_specs(dil, q_blk, k_blk, v_blk, ckv, lambda n: n)
    args = [view] * 5
    if sinks is not None:
        in_specs = [pl.BlockSpec(memory_space=pltpu.SMEM)] + in_specs
        args = [sinks] + args
    out = pl.BlockSpec((1, ATT_BLOCK, D_MODEL), lambda b, j, n: (b, n, j))
    res = pl.pallas_call(
        body,
        out_shape=(jax.ShapeDtypeStruct((bsz, l_sub, dil * D_MODEL), F32),) * 2
        + ((jax.ShapeDtypeStruct((bsz, l_sub, dil * D_MODEL), BF16),) if emit_bf16 else ()),
        grid=(bsz, dil, nb),
        in_specs=in_specs,
        out_specs=(out,) * (3 if emit_bf16 else 2),
        compiler_params=_params(("parallel", "parallel", "arbitrary")),
        name=name,
    )(*args)
    return tuple(t.reshape(bsz, s_len, D_MODEL) for t in res)


def attn_bwd(qkv, o, lse, do, acc, *, dil, n_kv_heads, name):
    bsz, s_len, width = qkv.shape
    ckv = n_kv_heads * HEAD_DIM
    l_sub = s_len // dil
    nb = l_sub // ATT_BLOCK
    view = qkv.reshape(bsz, l_sub, dil * width)
    scale = HEAD_DIM ** -0.5
    q_blk = (0, width // D_MODEL)
    k_blk = (D_MODEL // ckv, width // ckv)
    v_blk = (D_MODEL // ckv + 1, width // ckv)
    single = nb == 1

    def body(*refs):
        if acc is None:
            q_ref, kp_ref, kc_ref, vp_ref, vc_ref, o_ref, lse_ref, do_ref, dq_ref, dk_ref, dv_ref, dkk, dvv, ck, cv = refs
            aq_ref = ak_ref = av_ref = None
        else:
            (q_ref, kp_ref, kc_ref, vp_ref, vc_ref, o_ref, lse_ref, do_ref, aq_ref, ak_ref, av_ref,
             dq_ref, dk_ref, dv_ref, dkk, dvv, ck, cv) = refs
        n = pl.program_id(2)

        @pl.when(n < nb)
        def _():
            mask = _band_mask(n, single)
            hm = _half_masks(BF16)
            hmf = _half_masks(F32)
            kk = kc_ref[0] if single else jnp.concatenate([kp_ref[0], kc_ref[0]], axis=0)
            vv = vc_ref[0] if single else jnp.concatenate([vp_ref[0], vc_ref[0]], axis=0)
            krows = pl.ds(ATT_BLOCK, ATT_BLOCK) if single else pl.ds(0, 2 * ATT_BLOCK)
            dkk[...] = jnp.zeros_like(dkk)
            dvv[...] = jnp.zeros_like(dvv)
            for hp in range(N_HEADS // 2):
                cols = slice(hp * LANES, (hp + 1) * LANES)
                q2 = q_ref[0, :, cols]
                do2f = do_ref[0, :, cols]
                do2 = do2f.astype(BF16)
                dd2 = do2f * o_ref[0, :, cols]
                l2 = lse_ref[0, :, cols]
                dq2 = jnp.zeros((ATT_BLOCK, LANES), F32)
                for a in range(2):
                    kb, kh = _kv_place(2 * hp + a, n_kv_heads)
                    kcols = slice(kb * LANES, (kb + 1) * LANES)
                    k2 = kk[:, kcols]
                    v2 = vv[:, kcols]
                    if kh != a:
                        k2 = pltpu.roll(k2, HEAD_DIM, 1)
                        v2 = pltpu.roll(v2, HEAD_DIM, 1)
                    qm = q2 * hm[a]
                    dom = do2 * hm[a]
                    dsum = jnp.sum(dd2 * hmf[a], axis=-1, keepdims=True)
                    lse_h = jnp.max(jnp.where(hmf[a] > 0.5, l2, NEG_INF), axis=-1, keepdims=True)
                    s = lax.dot_general(qm, k2, NT, preferred_element_type=F32) * scale
                    s = jnp.where(mask, s, NEG_INF)
                    p = jnp.exp(s - lse_h)
                    dp = lax.dot_general(dom, v2, NT, preferred_element_type=F32)
                    ds = (p * (dp - dsum) * scale).astype(BF16)
                    dq2 = dq2 + jnp.dot(ds, k2 * hm[a], preferred_element_type=F32)
                    dk_c = lax.dot_general(ds, qm, TN, preferred_element_type=F32)
                    dv_c = lax.dot_general(p.astype(BF16), dom, TN, preferred_element_type=F32)
                    if kh != a:
                        dk_c = pltpu.roll(dk_c, HEAD_DIM, 1)
                        dv_c = pltpu.roll(dv_c, HEAD_DIM, 1)
                    dkk[krows, kcols] += dk_c
                    dvv[krows, kcols] += dv_c
                if aq_ref is not None:
                    dq2 = dq2 + aq_ref[0, :, cols]
                dq_ref[0, :, cols] = dq2

        @pl.when((n >= 1) & (n < nb))
        def _():
            dk_ref[0] = ck[...] + dkk[pl.ds(0, ATT_BLOCK), :] + (0.0 if ak_ref is None else ak_ref[0])
            dv_ref[0] = cv[...] + dvv[pl.ds(0, ATT_BLOCK), :] + (0.0 if av_ref is None else av_ref[0])

        @pl.when(n == nb)
        def _():
            dk_ref[0] = ck[...] + (0.0 if ak_ref is None else ak_ref[0])
            dv_ref[0] = cv[...] + (0.0 if av_ref is None else av_ref[0])

        @pl.when(n < nb)
        def _():
            ck[...] = dkk[pl.ds(ATT_BLOCK, ATT_BLOCK), :]
            cv[...] = dvv[pl.ds(ATT_BLOCK, ATT_BLOCK), :]

    clamp = lambda n: jnp.minimum(n, nb - 1)
    prev = lambda n: jnp.maximum(n - 1, 0)
    row = pl.BlockSpec((1, ATT_BLOCK, D_MODEL), lambda b, j, n: (b, clamp(n), j))
    kv_out = pl.BlockSpec((1, ATT_BLOCK, ckv), lambda b, j, n: (b, prev(n), j))
    in_specs = _qkv_specs(dil, q_blk, k_blk, v_blk, ckv, clamp) + [row, row, row]
    rs = lambda t: t.reshape(bsz, l_sub, dil * t.shape[-1])
    args = [view] * 5 + [rs(o), rs(lse), rs(do)]
    if acc is not None:
        in_specs += [row, kv_out, kv_out]
        args += [rs(t) for t in acc]
    dq, dk, dv = pl.pallas_call(
        body,
        out_shape=(jax.ShapeDtypeStruct((bsz, l_sub, dil * D_MODEL), F32),
                   jax.ShapeDtypeStruct((bsz, l_sub, dil * ckv), F32), jax.ShapeDtypeStruct((bsz, l_sub, dil * ckv), F32)),
        grid=(bsz, dil, nb + 1),
        in_specs=in_specs,
        out_specs=(row, kv_out, kv_out),
        scratch_shapes=[pltpu.VMEM((2 * ATT_BLOCK, ckv), F32), pltpu.VMEM((2 * ATT_BLOCK, ckv), F32),
                        pltpu.VMEM((ATT_BLOCK, ckv), F32), pltpu.VMEM((ATT_BLOCK, ckv), F32)],
        compiler_params=_params(("parallel", "parallel", "arbitrary")),
        name=name,
    )(*args)
    return dq.reshape(bsz, s_len, D_MODEL), dk.reshape(bsz, s_len, ckv), dv.reshape(bsz, s_len, ckv)


def dil_combine(os_, lses, *, name):
    t_dim, d = os_[0].shape
    tr = _pick(t_dim, (256, 128, 8))

    def body(o0, o1, o2, l0, l1, l2, y_ref, lt_ref, yb_ref):
        la, lb, lc = l0[...], l1[...], l2[...]
        m = jnp.maximum(jnp.maximum(la, lb), lc)
        ea, eb, ec = jnp.exp(la - m), jnp.exp(lb - m), jnp.exp(lc - m)
        tot = ea + eb + ec
        y = (ea / tot) * o0[...] + (eb / tot) * o1[...] + (ec / tot) * o2[...]
        y_ref[...] = y
        yb_ref[...] = y.astype(BF16)
        lt_ref[...] = m + jnp.log(tot)

    row = pl.BlockSpec((tr, d), lambda i: (i, 0))
    return pl.pallas_call(
        body,
        out_shape=(jax.ShapeDtypeStruct((t_dim, d), F32),) * 2 + (jax.ShapeDtypeStruct((t_dim, d), BF16),),
        grid=(t_dim // tr,),
        in_specs=[row] * 6,
        out_specs=(row, row, row),
        compiler_params=_params(("parallel",)),
        name=name,
    )(*os_, *lses)


ATT_SCALE = HEAD_DIM ** -0.5


def _band_mask(n, single):
    nk = ATT_BLOCK if single else 2 * ATT_BLOCK
    qi = lax.broadcasted_iota(jnp.int32, (2 * ATT_BLOCK, nk), 0) % ATT_BLOCK
    kj = lax.broadcasted_iota(jnp.int32, (2 * ATT_BLOCK, nk), 1)
    if single:
        return qi >= kj
    rel = qi + ATT_BLOCK - kj
    return (rel >= 0) & (rel <= ATT_BLOCK) & ((n > 0) | (kj >= ATT_BLOCK))


def _lane_halves():
    lane = lax.broadcasted_iota(jnp.int32, (1, LANES), 1)
    return lane < HEAD_DIM


def _stack_heads(t2, kh):
    first = _lane_halves()
    parts = []
    for a in range(2):
        ta = jnp.where(first if a == 0 else ~first, t2, jnp.zeros_like(t2))
        if a != kh[a]:
            ta = pltpu.roll(ta, HEAD_DIM, 1)
        parts.append(ta)
    return jnp.concatenate(parts, axis=0)


def _fold_heads(t, kh):
    t0, t1 = t[:ATT_BLOCK], t[ATT_BLOCK:]
    if t.shape[1] == LANES:
        if kh[0] != 0:
            t0 = pltpu.roll(t0, HEAD_DIM, 1)
        if kh[1] != 1:
            t1 = pltpu.roll(t1, HEAD_DIM, 1)
    return jnp.where(_lane_halves(), t0, t1)


def _rows_of_heads(t2):
    return jnp.concatenate([t2[:, 0:1], t2[:, HEAD_DIM:HEAD_DIM + 1]], axis=0)


PAIRS_AT_ONCE = 4


def _fill_bias(bias2_ref, bias1_ref=None):
    for i in range(2):
        bias2_ref[i] = jnp.where(_band_mask(i, False), 0.0, NEG_INF)
    if bias1_ref is not None:
        bias1_ref[...] = jnp.where(_band_mask(0, True), 0.0, NEG_INF)


def _pairs_fwd(items):
    ss = [lax.dot_general(_stack_heads(q2 * ATT_SCALE, kh), kk, NT, preferred_element_type=F32) + bias
          for q2, kk, _, bias, kh, _ in items]
    ps, ms, ls = [], [], []
    for s, (_, _, _, _, _, sink_col) in zip(ss, items):
        m = jnp.max(s, axis=-1, keepdims=True)
        if sink_col is not None:
            m = jnp.maximum(m, sink_col)
        p = jnp.exp(s - m)
        l = jnp.sum(p, axis=-1, keepdims=True)
        if sink_col is not None:
            l = l + jnp.exp(sink_col - m)
        ps.append(p.astype(BF16))
        ms.append(m)
        ls.append(l)
    pvs = [jnp.dot(p, it[2], preferred_element_type=F32) for p, it in zip(ps, items)]
    return list(zip(pvs, ms, ls))


def _pairs_bwd(items):
    first = _lane_halves()
    pre = []
    for q2, kk, vv, do2, o2, lse2, bias, kh in items:
        dd = do2 * o2
        dsum = jnp.concatenate([jnp.sum(jnp.where(first, dd, 0.0), axis=-1, keepdims=True),
                                jnp.sum(jnp.where(first, 0.0, dd), axis=-1, keepdims=True)], axis=0)
        qs = _stack_heads(q2 * ATT_SCALE, kh)
        dos = _stack_heads(do2.astype(BF16), kh)
        s = lax.dot_general(qs, kk, NT, preferred_element_type=F32) + bias
        dp = lax.dot_general(dos, vv, NT, preferred_element_type=F32)
        pre.append((qs, dos, s, dp, dsum))
    mid = []
    for (qs, dos, s, dp, dsum), it in zip(pre, items):
        p = jnp.exp(s - _rows_of_heads(it[5]))
        mid.append((p.astype(BF16), (p * (dp - dsum)).astype(BF16)))
    out = []
    for (pb, ds), (qs, dos, _, _, _), it in zip(mid, pre, items):
        dq = _fold_heads(jnp.dot(ds, it[1], preferred_element_type=F32), it[7]) * ATT_SCALE
        dk = lax.dot_general(ds, qs, TN, preferred_element_type=F32)
        dv = lax.dot_general(pb, dos, TN, preferred_element_type=F32)
        out.append((dq, dk, dv))
    return out


def swa_fwd(qkv, sinks, *, name, carried=None):
    bsz, s_len, width = qkv.shape
    ckv = SWA_KV_HEADS * HEAD_DIM
    nb = s_len // ATT_BLOCK
    kblk = D_MODEL // ckv

    def body(sink_ref, q_ref, kp_ref, kc_ref, vp_ref, vc_ref, o_ref, lse_ref, ob_ref, bias2):
        n = pl.program_id(1)
        _fill_bias(bias2)
        bias = bias2[jnp.minimum(n, 1)]
        kk = jnp.concatenate([kp_ref[0], kc_ref[0]], axis=0)
        vv = jnp.concatenate([vp_ref[0], vc_ref[0]], axis=0)
        top = lax.broadcasted_iota(jnp.int32, (2 * ATT_BLOCK, 1), 0) < ATT_BLOCK
        for hp0 in range(0, N_HEADS // 2, PAIRS_AT_ONCE):
            items, places = [], []
            for hp in range(hp0, hp0 + PAIRS_AT_ONCE):
                cols = slice(hp * LANES, (hp + 1) * LANES)
                kb, kh = _kv_place(2 * hp, SWA_KV_HEADS)
                kcols = slice(kb * LANES, (kb + 1) * LANES)
                sink_col = jnp.where(top, sink_ref[2 * hp], sink_ref[2 * hp + 1])
                items.append((q_ref[0, :, cols], kk[:, kcols], vv[:, kcols], bias, (kh, kh), sink_col))
                places.append((cols, (kh, kh)))
            for (pv, m, l), (cols, kh2) in zip(_pairs_fwd(items), places):
                o2 = _fold_heads(pv / l, kh2)
                o_ref[0, :, cols] = o2
                ob_ref[0, :, cols] = o2.astype(BF16)
                lse_ref[0, :, cols] = _fold_heads(m + jnp.log(l), kh2)

    prev = lambda n: jnp.maximum(n - 1, 0)
    out = pl.BlockSpec((1, ATT_BLOCK, D_MODEL), lambda b, n: (b, n, 0))
    sd = lambda dt: jax.ShapeDtypeStruct((bsz, s_len, D_MODEL), dt)
    return call_with_exchange(
        body, carried,
        out_shape=(sd(F32), sd(F32), sd(BF16)),
        grid=(bsz, nb),
        in_specs=[pl.BlockSpec(memory_space=pltpu.SMEM), out,
                  pl.BlockSpec((1, ATT_BLOCK, ckv), lambda b, n: (b, prev(n), kblk)),
                  pl.BlockSpec((1, ATT_BLOCK, ckv), lambda b, n: (b, n, kblk)),
                  pl.BlockSpec((1, ATT_BLOCK, ckv), lambda b, n: (b, prev(n), kblk + 1)),
                  pl.BlockSpec((1, ATT_BLOCK, ckv), lambda b, n: (b, n, kblk + 1))],
        out_specs=(out, out, out),
        scratch_shapes=[pltpu.VMEM((2, 2 * ATT_BLOCK, 2 * ATT_BLOCK), F32)],
        name=name,
        args=(sinks, qkv, qkv, qkv, qkv, qkv))


def swa_bwd(qkv, o, lse, do, *, name, carried=None):
    bsz, s_len, width = qkv.shape
    ckv = SWA_KV_HEADS * HEAD_DIM
    nb = s_len // ATT_BLOCK
    kblk = D_MODEL // ckv

    def body(q_ref, kp_ref, kc_ref, vp_ref, vc_ref, o_ref, lse_ref, do_ref, dq_ref, dk_ref, dv_ref, dkk, dvv, ck, cv,
             bias2):
        n = pl.program_id(1)

        @pl.when(n < nb)
        def _():
            _fill_bias(bias2)
            bias = bias2[jnp.minimum(n, 1)]
            kk = jnp.concatenate([kp_ref[0], kc_ref[0]], axis=0)
            vv = jnp.concatenate([vp_ref[0], vc_ref[0]], axis=0)
            dkk[...] = jnp.zeros_like(dkk)
            dvv[...] = jnp.zeros_like(dvv)
            for hp0 in range(0, N_HEADS // 2, PAIRS_AT_ONCE):
                items, places = [], []
                for hp in range(hp0, hp0 + PAIRS_AT_ONCE):
                    cols = slice(hp * LANES, (hp + 1) * LANES)
                    kb, kh = _kv_place(2 * hp, SWA_KV_HEADS)
                    kcols = slice(kb * LANES, (kb + 1) * LANES)
                    items.append((q_ref[0, :, cols], kk[:, kcols], vv[:, kcols], do_ref[0, :, cols], o_ref[0, :, cols],
                                  lse_ref[0, :, cols], bias, (kh, kh)))
                    places.append((cols, kcols))
                for (dq, dk, dv), (cols, kcols) in zip(_pairs_bwd(items), places):
                    dq_ref[0, :, cols] = dq
                    dkk[:, kcols] += dk
                    dvv[:, kcols] += dv

        @pl.when((n >= 1) & (n < nb))
        def _():
            dk_ref[0] = ck[...] + dkk[pl.ds(0, ATT_BLOCK), :]
            dv_ref[0] = cv[...] + dvv[pl.ds(0, ATT_BLOCK), :]

        @pl.when(n == nb)
        def _():
            dk_ref[0] = ck[...]
            dv_ref[0] = cv[...]

        @pl.when(n < nb)
        def _():
            ck[...] = dkk[pl.ds(ATT_BLOCK, ATT_BLOCK), :]
            cv[...] = dvv[pl.ds(ATT_BLOCK, ATT_BLOCK), :]

    clamp = lambda n: jnp.minimum(n, nb - 1)
    prev = lambda n: jnp.maximum(n - 1, 0)
    row = pl.BlockSpec((1, ATT_BLOCK, D_MODEL), lambda b, n: (b, clamp(n), 0))
    kv_out = pl.BlockSpec((1, ATT_BLOCK, ckv), lambda b, n: (b, prev(n), 0))
    return call_with_exchange(
        body, carried,
        out_shape=(jax.ShapeDtypeStruct((bsz, s_len, D_MODEL), F32), jax.ShapeDtypeStruct((bsz, s_len, ckv), F32),
                   jax.ShapeDtypeStruct((bsz, s_len, ckv), F32)),
        grid=(bsz, nb + 1),
        in_specs=[row,
                  pl.BlockSpec((1, ATT_BLOCK, ckv), lambda b, n: (b, prev(clamp(n)), kblk)),
                  pl.BlockSpec((1, ATT_BLOCK, ckv), lambda b, n: (b, clamp(n), kblk)),
                  pl.BlockSpec((1, ATT_BLOCK, ckv), lambda b, n: (b, prev(clamp(n)), kblk + 1)),
                  pl.BlockSpec((1, ATT_BLOCK, ckv), lambda b, n: (b, clamp(n), kblk + 1)),
                  row, row, row],
        out_specs=(row, kv_out, kv_out),
        scratch_shapes=[pltpu.VMEM((2 * ATT_BLOCK, ckv), F32), pltpu.VMEM((2 * ATT_BLOCK, ckv), F32),
                        pltpu.VMEM((ATT_BLOCK, ckv), F32), pltpu.VMEM((ATT_BLOCK, ckv), F32),
                        pltpu.VMEM((2, 2 * ATT_BLOCK, 2 * ATT_BLOCK), F32)],
        name=name,
        args=(qkv, qkv, qkv, qkv, qkv, o, lse, do))


DIL_PATTERNS = tuple((d, 2048 // d // ATT_BLOCK) for d in DILATIONS)
MHA = (0, 1)


def _dil_rows(idx, d, nb):
    j = idx // nb
    n = idx % nb
    base = j + n * (ATT_BLOCK * d)
    prev = jnp.maximum(base - ATT_BLOCK * d, j)
    if d == 1:
        return n, pl.ds(pl.multiple_of(base, ATT_BLOCK), ATT_BLOCK), pl.ds(pl.multiple_of(prev, ATT_BLOCK), ATT_BLOCK)
    return n, pl.ds(base, ATT_BLOCK, stride=d), pl.ds(prev, ATT_BLOCK, stride=d)


def dil_fwd(qkv, *, name, carried=None):
    bsz, s_len, _ = qkv.shape
    assert s_len == DIL_PATTERNS[0][0] * DIL_PATTERNS[0][1] * ATT_BLOCK
    npair = N_HEADS // 2

    def body(q_ref, k_ref, v_ref, y_ref, lse_ref, yb_ref, m_acc, l_acc, bias2, bias1):
        _fill_bias(bias2, bias1)
        for ci, (d, nb) in enumerate(DIL_PATTERNS):
            single = nb == 1

            def blocks(it, carry):
                items, places = [], []
                for u in range(PAIRS_AT_ONCE):
                    n, rows, prows = _dil_rows(it * PAIRS_AT_ONCE + u, d, nb)
                    kc = k_ref[rows, :].astype(BF16)
                    vc = v_ref[rows, :].astype(BF16)
                    if single:
                        kk, vv, bias = kc, vc, bias1[...]
                    else:
                        kk = jnp.concatenate([k_ref[prows, :].astype(BF16), kc], axis=0)
                        vv = jnp.concatenate([v_ref[prows, :].astype(BF16), vc], axis=0)
                        bias = bias2[jnp.minimum(n, 1)]
                    items.append((q_ref[rows, :].astype(BF16), kk, vv, bias, MHA, None))
                    places.append(rows)
                for (pv, m, l), rows in zip(_pairs_fwd(items), places):
                    o2, m2, l2 = _fold_heads(pv, MHA), _fold_heads(m, MHA), _fold_heads(l, MHA)
                    if ci == 0:
                        y_ref[rows, :] = o2
                        m_acc[rows, :] = m2
                        l_acc[rows, :] = l2
                    else:
                        m_old = m_acc[rows, :]
                        m_new = jnp.maximum(m_old, m2)
                        w_old = jnp.exp(m_old - m_new)
                        w_new = jnp.exp(m2 - m_new)
                        y_ref[rows, :] = y_ref[rows, :] * w_old + o2 * w_new
                        l_acc[rows, :] = l_acc[rows, :] * w_old + l2 * w_new
                        m_acc[rows, :] = m_new
                return carry

            lax.fori_loop(0, d * nb // PAIRS_AT_ONCE, blocks, 0)
        y = y_ref[...] / l_acc[...]
        y_ref[...] = y
        yb_ref[...] = y.astype(BF16)
        lse_ref[...] = m_acc[...] + jnp.log(l_acc[...])

    slab = lambda off: pl.BlockSpec((None, s_len, LANES), functools.partial(lambda o, b, h: (b, 0, o + h), off))
    sd = lambda dt: jax.ShapeDtypeStruct((bsz, s_len, D_MODEL), dt)
    return call_with_exchange(
        body, carried,
        out_shape=(sd(F32), sd(F32), sd(BF16)),
        grid=(bsz, npair),
        in_specs=[slab(0), slab(npair), slab(2 * npair)],
        out_specs=(slab(0), slab(0), slab(0)),
        scratch_shapes=[pltpu.VMEM((s_len, LANES), F32), pltpu.VMEM((s_len, LANES), F32),
                        pltpu.VMEM((2, 2 * ATT_BLOCK, 2 * ATT_BLOCK), F32), pltpu.VMEM((2 * ATT_BLOCK, ATT_BLOCK), F32)],
        name=name,
        args=(qkv, qkv, qkv))


def dil_bwd(qkv, y, lse, dy, *, name, carried=None):
    bsz, s_len, _ = qkv.shape
    npair = N_HEADS // 2

    def body(q_ref, k_ref, v_ref, y_ref, lse_ref, dy_ref, dq_ref, dk_ref, dv_ref, bias2, bias1):
        _fill_bias(bias2, bias1)
        dq_ref[...] = jnp.zeros_like(dq_ref)
        dk_ref[...] = jnp.zeros_like(dk_ref)
        dv_ref[...] = jnp.zeros_like(dv_ref)
        for d, nb in DIL_PATTERNS:
            single = nb == 1

            def blocks(it, carry):
                items, places = [], []
                for u in range(PAIRS_AT_ONCE):
                    n, rows, prows = _dil_rows(it * PAIRS_AT_ONCE + u, d, nb)
                    kc = k_ref[rows, :].astype(BF16)
                    vc = v_ref[rows, :].astype(BF16)
                    if single:
                        kk, vv, bias = kc, vc, bias1[...]
                    else:
                        kk = jnp.concatenate([k_ref[prows, :].astype(BF16), kc], axis=0)
                        vv = jnp.concatenate([v_ref[prows, :].astype(BF16), vc], axis=0)
                        bias = bias2[jnp.minimum(n, 1)]
                    items.append((q_ref[rows, :].astype(BF16), kk, vv, dy_ref[rows, :], y_ref[rows, :], lse_ref[rows, :], bias, MHA))
                    places.append((rows, prows))
                for (dq, dk, dv), (rows, prows) in zip(_pairs_bwd(items), places):
                    dq_ref[rows, :] += dq
                    if single:
                        dk_ref[rows, :] += dk
                        dv_ref[rows, :] += dv
                    else:
                        dk_ref[prows, :] += dk[:ATT_BLOCK]
                        dv_ref[prows, :] += dv[:ATT_BLOCK]
                        dk_ref[rows, :] += dk[ATT_BLOCK:]
                        dv_ref[rows, :] += dv[ATT_BLOCK:]
                return carry

            lax.fori_loop(0, d * nb // PAIRS_AT_ONCE, blocks, 0)

    slab = lambda off: pl.BlockSpec((None, s_len, LANES), functools.partial(lambda o, b, h: (b, 0, o + h), off))
    sd = jax.ShapeDtypeStruct((bsz, s_len, D_MODEL), F32)
    return call_with_exchange(
        body, carried,
        out_shape=(sd, sd, sd),
        grid=(bsz, npair),
        in_specs=[slab(0), slab(npair), slab(2 * npair), slab(0), slab(0), slab(0)],
        out_specs=(slab(0), slab(0), slab(0)),
        scratch_shapes=[pltpu.VMEM((2, 2 * ATT_BLOCK, 2 * ATT_BLOCK), F32), pltpu.VMEM((2 * ATT_BLOCK, ATT_BLOCK), F32)],
        name=name,
        args=(qkv, qkv, qkv, y, lse, dy))


def sink_grad(do, o, lse, sink_lanes, *, name):
    t_dim, d = do.shape
    tr = _pick(t_dim, (256, 128, 8))

    def body(do_ref, o_ref, l_ref, s_ref, out_ref):
        @pl.when(pl.program_id(0) == 0)
        def _():
            out_ref[...] = jnp.zeros_like(out_ref)

        out_ref[...] += jnp.sum(-jnp.exp(s_ref[...] - l_ref[...]) * do_ref[...] * o_ref[...], axis=0, keepdims=True)

    row = pl.BlockSpec((tr, d), lambda i: (i, 0))
    vec = pl.BlockSpec((1, d), lambda i: (0, 0))
    return pl.pallas_call(
        body,
        out_shape=jax.ShapeDtypeStruct((1, d), F32),
        grid=(t_dim // tr,),
        in_specs=[row, row, row, vec],
        out_specs=vec,
        compiler_params=_params(("arbitrary",)),
        name=name,
    )(do, o, lse, sink_lanes)


def adamw(w, g, m, v, *, name):
    rows, cols = w.shape
    tr = _pick(rows, (256, 128, 64, 32, 16, 8))

    def body(w_ref, g_ref, m_ref, v_ref, d_ref, nm_ref, nv_ref):
        gv = g_ref[...]
        nm = ADAM_B1 * m_ref[...] + (1.0 - ADAM_B1) * gv
        nv = ADAM_B2 * v_ref[...] + (1.0 - ADAM_B2) * (gv * gv)
        m_hat = nm / (1.0 - ADAM_B1 ** ADAM_STEP)
        v_hat = nv / (1.0 - ADAM_B2 ** ADAM_STEP)
        d_ref[...] = -ADAM_LR * (m_hat / (jnp.sqrt(v_hat) + ADAM_EPS) + ADAM_WD * w_ref[...])
        nm_ref[...] = nm
        nv_ref[...] = nv

    row = pl.BlockSpec((tr, cols), lambda i: (i, 0))
    return pl.pallas_call(
        body,
        out_shape=(jax.ShapeDtypeStruct((rows, cols), F32),) * 3,
        grid=(rows // tr,),
        in_specs=[row] * 4,
        out_specs=(row, row, row),
        compiler_params=_params(("parallel",)),
        name=name,
    )(w, g, m, v)


def _place():
    return lax.axis_index("x"), lax.axis_index("y"), lax.axis_index("c")


def _gather_copies(x_ref, out_ref, send_sems, recv_sems):
    x, y, c = _place()
    me, sibling = (x, y, c), (x, y, 1 - c)
    chips = [(1 - x, y), (x, 1 - y), (1 - x, 1 - y)]

    def slot(px, py, pc):
        return out_ref.at[4 * px + 2 * py + pc]

    def copy(k, block, to, src=None):
        return pltpu.make_async_remote_copy(
            src_ref=slot(*block) if src is None else src, dst_ref=slot(*block),
            send_sem=send_sems.at[k], recv_sem=recv_sems.at[k], device_id=to, device_id_type=MESH)

    first = [lambda: copy(0, me, sibling, src=x_ref)] + [functools.partial(copy, 1 + j, me, (*chip, c), src=x_ref)
                                                         for j, chip in enumerate(chips)]
    passed = [functools.partial(copy, 4 + j, (*chip, c), sibling) for j, chip in enumerate(chips)]
    landing = [functools.partial(copy, 1 + j, (*chip, c), me) for j, chip in enumerate(chips)]
    from_sibling = [lambda: copy(0, sibling, me)] + [functools.partial(copy, 4 + j, (*chip, 1 - c), me) for j, chip in enumerate(chips)]
    return slot(*me), first, passed, landing, from_sibling


def _gather_start(x_ref, out_ref, send_sems, recv_sems, local_sem):
    mine, first, _, _, _ = _gather_copies(x_ref, out_ref, send_sems, recv_sems)
    pltpu.make_async_copy(x_ref, mine, local_sem).start()
    for cp in first:
        cp().start()


def _gather_finish(x_ref, out_ref, send_sems, recv_sems, local_sem):
    mine, first, passed, landing, from_sibling = _gather_copies(x_ref, out_ref, send_sems, recv_sems)
    for cp, fwd in zip(landing, passed):
        cp().wait_recv()
        fwd().start()
    for cp in from_sibling:
        cp().wait_recv()
    for cp in first + passed:
        cp().wait_send()
    pltpu.make_async_copy(x_ref, mine, local_sem).wait()


def _a2a_copies(x_ref, out_ref, send_sems, recv_sems):
    x, y, c = _place()
    me = 4 * x + 2 * y + c
    copies = []
    for k in range(1, N_DEV):
        px = 1 - x if k & 4 else x
        py = 1 - y if k & 2 else y
        pc = 1 - c if k & 1 else c
        copies.append(pltpu.make_async_remote_copy(
            src_ref=x_ref.at[4 * px + 2 * py + pc], dst_ref=out_ref.at[me], send_sem=send_sems.at[k - 1],
            recv_sem=recv_sems.at[k - 1], device_id=(px, py, pc), device_id_type=MESH))
    return me, copies


def _a2a_start(x_ref, out_ref, send_sems, recv_sems, local_sem):
    me, copies = _a2a_copies(x_ref, out_ref, send_sems, recv_sems)
    pltpu.make_async_copy(x_ref.at[me], out_ref.at[me], local_sem).start()
    for cp in copies:
        cp.start()


def _a2a_finish(x_ref, out_ref, send_sems, recv_sems, local_sem):
    me, copies = _a2a_copies(x_ref, out_ref, send_sems, recv_sems)
    for cp in copies:
        cp.wait_recv()
    for cp in copies:
        cp.wait_send()
    pltpu.make_async_copy(x_ref.at[me], out_ref.at[me], local_sem).wait()


EXCHANGES = {"gather": (_gather_start, _gather_finish, lambda x: (N_DEV,) + x.shape),
             "a2a": (_a2a_start, _a2a_finish, lambda x: x.shape)}
EXCHANGE_SEMS = [pltpu.SemaphoreType.DMA((7,)), pltpu.SemaphoreType.DMA((7,)), pltpu.SemaphoreType.DMA(())]


def exchange(kind, x, *, name):
    start, finish, shape = EXCHANGES[kind]

    def body(x_ref, out_ref, *sems):
        start(x_ref, out_ref, *sems)
        finish(x_ref, out_ref, *sems)

    return pl.pallas_call(
        body,
        out_shape=jax.ShapeDtypeStruct(shape(x), x.dtype),
        in_specs=[pl.BlockSpec(memory_space=pl.ANY)],
        out_specs=pl.BlockSpec(memory_space=pl.ANY),
        scratch_shapes=EXCHANGE_SEMS,
        name=name,
    )(x)


def call_with_exchange(body, carried, *, out_shape, grid, in_specs, out_specs, scratch_shapes, name, args):
    sem = ("arbitrary",) * len(grid)
    carried = list(carried or ())
    if not carried:
        res = pl.pallas_call(body, out_shape=out_shape, grid=grid, in_specs=in_specs, out_specs=out_specs,
                             scratch_shapes=scratch_shapes, compiler_params=_params(sem), name=name)(*args)
        return res, []
    n_in, n_out, n_scr, n_x = len(in_specs), len(out_shape), len(scratch_shapes), len(carried)
    n_sems = len(EXCHANGE_SEMS)

    def wrapped(*refs):
        ins, x_refs = refs[:n_in], refs[n_in:n_in + n_x]
        outs = refs[n_in + n_x:n_in + n_x + n_out]
        out_refs = refs[n_in + n_x + n_out:n_in + 2 * n_x + n_out]
        rest = refs[n_in + 2 * n_x + n_out:]
        scratch, sems = rest[:n_scr], rest[n_scr:]
        ids = [pl.program_id(i) for i in range(len(grid))]
        is_first = functools.reduce(lambda a, b: a & b, [i == 0 for i in ids])
        is_last = functools.reduce(lambda a, b: a & b, [i == g - 1 for i, g in zip(ids, grid)])

        @pl.when(is_first)
        def _():
            for e, (kind, _) in enumerate(carried):
                EXCHANGES[kind][0](x_refs[e], out_refs[e], *sems[e * n_sems:(e + 1) * n_sems])

        body(*ins, *outs, *scratch)

        @pl.when(is_last)
        def _():
            for e, (kind, _) in enumerate(carried):
                EXCHANGES[kind][1](x_refs[e], out_refs[e], *sems[e * n_sems:(e + 1) * n_sems])

    any_spec = pl.BlockSpec(memory_space=pl.ANY)
    res = pl.pallas_call(
        wrapped,
        out_shape=tuple(out_shape) + tuple(jax.ShapeDtypeStruct(EXCHANGES[kind][2](x), x.dtype) for kind, x in carried),
        grid=grid,
        in_specs=list(in_specs) + [any_spec] * n_x,
        out_specs=tuple(out_specs) + (any_spec,) * n_x,
        scratch_shapes=list(scratch_shapes) + EXCHANGE_SEMS * n_x,
        compiler_params=_params(sem),
        name=name + "".join("_" + kind for kind, _ in carried),
    )(*args, *[x for _, x in carried])
    return res[:n_out], list(res[n_out:])


def sum_slots(x, *, name):
    _, rows, cols = x.shape
    tr = _pick(rows, (512, 256, 128, 64, 32, 16))

    def body(x_ref, o_ref):
        acc = x_ref[0].astype(F32)
        for k in range(1, N_DEV):
            acc = acc + x_ref[k].astype(F32)
        o_ref[...] = acc

    return pl.pallas_call(
        body,
        out_shape=jax.ShapeDtypeStruct((rows, cols), F32),
        grid=(rows // tr,),
        in_specs=[pl.BlockSpec((N_DEV, tr, cols), lambda i: (0, i, 0))],
        out_specs=pl.BlockSpec((tr, cols), lambda i: (i, 0)),
        compiler_params=_params(("parallel",)),
        name=name,
    )(x)


BIG = ("w_in", "w_branch", "w_out", "w_ffn_in", "w_ffn_out")
SMALL = ("conv_b", "w_rg", "b_rg", "w_ig", "b_ig", "lru_lambda", "sinks", "ln1_g", "ln1_b", "ln2_g", "ln2_b")
N_LRU_BLOCKS = D_MODEL // HEAD_DIM
SMALL_ROWS_TILE = 512


def _block_diag(w):
    z = jnp.zeros((N_LRU_BLOCKS // 2, HEAD_DIM, HEAD_DIM), w.dtype)
    top = jnp.concatenate([w[0::2], z], axis=2)
    bot = jnp.concatenate([z, w[1::2]], axis=2)
    return jnp.concatenate([top, bot], axis=1)


def _block_diag_grad(g):
    return jnp.stack([g[:, :HEAD_DIM, :HEAD_DIM], g[:, HEAD_DIM:, HEAD_DIM:]], axis=1).reshape(N_LRU_BLOCKS, HEAD_DIM, HEAD_DIM)


def layer_fwd(x, xb, p, bsz, own_late=None, next_w_in=None):
    t_dim = x.shape[0]
    s_len = t_dim // bsz
    w_f, w_qs, w_qd = p["w_in_f"], p["w_in_qs"], p["w_in_qd"]
    proj_f = matmul(xb, w_f, name="proj_f")
    qs = matmul(xb, w_qs, out_dtype=BF16, name="proj_qs").reshape(bsz, s_len, W_QS)
    qd = matmul(xb, w_qd, name="proj_qd").reshape(bsz, s_len, W_QD)
    proj_f3 = proj_f.reshape(bsz, s_len, W_F)
    wr_bd, wi_bd = _block_diag(p["w_rg"]), _block_diag(p["w_ig"])
    (y_a, h), got_in = lru_fwd(proj_f3, p["conv_w"], p["conv_b"], wr_bd, wi_bd, p["b_rg"], p["b_ig"], p["lru_lambda"],
                               name="lru_fwd", carried=[("gather", next_w_in)] if next_w_in is not None else [])
    (y_b, lse_b, y_bb), _ = swa_fwd(qs, p["sinks"], name="swa_fwd")
    (y_c, lse_c, y_cb), got_late = dil_fwd(qd, name="dil_fwd", carried=[("gather", t) for t in own_late or ()])
    if own_late is not None:
        p = {**p, **_late_weights(*got_late)}
    ys =[t.reshape(t_dim, D_MODEL) for t in (y_a, y_bb, y_cb)]
    br = [matmul(ys[n], p["w_branch"][n], name="branch") for n in range(3)]
    merged = merge_fwd(proj_f, br, name="merge_fwd")
    mix = matmul(merged, p["w_out"], name="w_out")
    x1, x1b, z1 = ln_fwd(x, mix, p["ln1_g"], p["ln1_b"], name="ln_fwd")
    h13 = matmul(x1b, p["w_ffn_in"], name="ffn_in")
    act = swiglu_fwd(h13, name="swiglu_fwd")
    ffn = matmul(act, p["w_ffn_out"], name="ffn_out")
    x2, x2b, z2 = ln_fwd(x1, ffn, p["ln2_g"], p["ln2_b"], name="ln_fwd")
    saved = dict(xb=xb, proj_f=proj_f, qs=qs, qd=qd, h=h, ys=ys, y_b=y_b, y_c=y_c, lse_b=lse_b, lse_c=lse_c, br=br, merged=merged,
                 z1=z1, x1b=x1b, h13=h13, act=act, z2=z2, wr_bd=wr_bd, wi_bd=wi_bd, p=p)
    return x2, x2b, saved, (got_in[0] if got_in else None)


def layer_bwd(dx2, s, bsz, exchange_own=False, above_w_in=None):
    p = s["p"]
    t_dim = dx2.shape[0]
    s_len = t_dim // bsz
    g = {}
    dz2, dz2b, g["ln2_g"], g["ln2_b"] = ln_bwd(dx2, s["z2"], p["ln2_g"], name="ln_bwd")
    dact = matmul(dz2b, p["w_ffn_out"], trans_b=True, name="d_act")
    dh13 = swiglu_bwd(dact, s["h13"], name="swiglu_bwd")
    g["w_ffn_out"] = matmul(s["act"], dz2b, trans_a=True, name="dw_ffn_out")
    g["w_ffn_in"] = matmul(s["x1b"], dh13, trans_a=True, name="dw_ffn_in")
    dx1 = matmul(dh13, p["w_ffn_in"], trans_b=True, add=dz2, add_scale=ALPHA, name="dx_ffn")
    dz1, dz1b, g["ln1_g"], g["ln1_b"] = ln_bwd(dx1, s["z1"], p["ln1_g"], name="ln_bwd")
    dmerged = matmul(dz1b, p["w_out"], trans_b=True, name="d_merged")
    g["w_out"] = matmul(s["merged"], dz1b, trans_a=True, name="dw_out")
    *dbr, dgates = merge_bwd(dmerged, s["proj_f"], s["br"], name="merge_bwd")
    dys = [matmul(dbr[n], p["w_branch"][n], trans_b=True, name="d_branch") for n in range(3)]
    g["w_branch"] = jnp.stack([matmul(s["ys"][n], dbr[n], trans_a=True, name="dw_branch") for n in range(3)])
    fi_slots, rows_slots = _late_slots(g) if exchange_own else (None, None)
    shape3 = (bsz, s_len, D_MODEL)
    (dlx, dlg, g["conv_w"], g["conv_b"], g["b_rg"], g["b_ig"], g["lru_lambda"], dwr, dwi), got_rows = lru_bwd(
        dys[0].reshape(shape3), s["proj_f"].reshape(bsz, s_len, W_F), s["h"], p["conv_w"], p["conv_b"], s["wr_bd"], s["wi_bd"],
        jnp.swapaxes(s["wr_bd"], 1, 2), jnp.swapaxes(s["wi_bd"], 1, 2), p["b_rg"], p["b_ig"], p["lru_lambda"], name="lru_bwd",
        carried=[("a2a", rows_slots)] if exchange_own else [])
    g["w_rg"], g["w_ig"] = _block_diag_grad(dwr), _block_diag_grad(dwi)
    dy_b3 = dys[1].reshape(shape3)
    dqs, got_fi = swa_bwd(s["qs"], s["y_b"], s["lse_b"], dy_b3, name="swa_bwd", carried=[("a2a", fi_slots)] if exchange_own else [])
    sink_lanes = jnp.repeat(p["sinks"], HEAD_DIM).reshape(1, D_MODEL)
    g["sinks"] = sink_grad(dys[1], s["y_b"].reshape(t_dim, D_MODEL), s["lse_b"].reshape(t_dim, D_MODEL), sink_lanes,
                           name="sink_grad").reshape(N_HEADS, HEAD_DIM).sum(axis=1)
    dqd, got_in = dil_bwd(s["qd"], s["y_c"], s["lse_c"], dys[2].reshape(shape3), name="dil_bwd",
                          carried=[("a2a", above_w_in)] if above_w_in is not None else [])
    flat = lambda t: t.reshape(t_dim, t.shape[-1])
    dproj_f = jnp.concatenate([flat(dlx), flat(dlg), dgates], axis=1)
    dproj_qs = jnp.concatenate([flat(t) for t in dqs], axis=1).astype(BF16)
    dproj_qd = jnp.concatenate([flat(t) for t in dqd], axis=1).astype(BF16)
    g["w_in_f"] = matmul(s["xb"], dproj_f, trans_a=True, name="dw_in_f")
    g["w_in_qs"] = matmul(s["xb"], dproj_qs, trans_a=True, name="dw_in_qs")
    g["w_in_qd"] = matmul(s["xb"], dproj_qd, trans_a=True, name="dw_in_qd")
    dx = matmul(dproj_f, p["w_in_f"], trans_b=True, add=dz1, add_scale=ALPHA, name="dx_f")
    dx = matmul(dproj_qs, p["w_in_qs"], trans_b=True, add=dx, name="dx_qs")
    dx = matmul(dproj_qd, p["w_in_qd"], trans_b=True, add=dx, name="dx_qd")
    g = {k: (v.reshape(p[k].shape) if k in p else v) for k, v in g.items()}
    return dx, g, dict(late=(got_fi[0], got_rows[0]) if exchange_own else None, w_in=got_in[0] if got_in else None)


def local_step(x, target, layer_params, layer_shards=None, first_w_in=None):
    bsz, s_len, d = x.shape
    t_dim = bsz * s_len
    xf = x.reshape(t_dim, d)
    xb = xf.astype(BF16)
    exchanging = layer_shards is not None
    saved, gathered = [], first_w_in
    for l in range(DEPTH):
        p = layer_params(l, gathered)
        xf, xb, s, gathered = layer_fwd(xf, xb, p, bsz, own_late=layer_shards[l][1:] if exchanging else None,
                                        next_w_in=layer_shards[l + 1][0] if exchanging and l + 1 < DEPTH else None)
        saved.append(s)
    dy, sq = loss_head(xf, target.reshape(t_dim, d), name="loss_head")
    loss = 0.5 * jnp.sum(sq) / d
    grads, received, w_in_slots = [None] * DEPTH, [[None] * 3 for _ in range(DEPTH)], None
    for l in reversed(range(DEPTH)):
        dy, grads[l], got = layer_bwd(dy, saved[l], bsz, exchange_own=exchanging, above_w_in=w_in_slots)
        if got["w_in"] is not None:
            received[l + 1][0] = got["w_in"]
        if exchanging:
            received[l][1:] = got["late"]
            w_in_slots = _w_in_slots(grads[l])
    return loss, dy.reshape(bsz, s_len, d), grads, received, w_in_slots


W_IN_SEGMENTS = (("w_in_f", 0, 0, 2 * D_MODEL), ("w_in_qs", 0, 2 * D_MODEL, W_QS), ("w_in_qd", 0, 2 * D_MODEL + W_QS, W_QD),
                 ("w_in_f", 2 * D_MODEL, 2 * D_MODEL + W_QS + W_QD, 3 * D_MODEL))
ROW_SHARDED = ("w_branch", "w_out", "w_ffn_out")


def _cols_of_shards(shards, lo, hi):
    width = shards[0].shape[-1]
    parts = []
    for k, sh in enumerate(shards):
        a, b = max(lo, k * width), min(hi, (k + 1) * width)
        if a < b:
            parts.append(sh[..., a - k * width:b - k * width])
    return parts[0] if len(parts) == 1 else jnp.concatenate(parts, axis=-1)


def _cols_of_w_in(pieces, lo, hi):
    parts = []
    for name, p0, l0, width in W_IN_SEGMENTS:
        a, b = max(lo, l0), min(hi, l0 + width)
        if a < b:
            parts.append(pieces[name][..., p0 + a - l0:p0 + b - l0])
    return parts[0] if len(parts) == 1 else jnp.concatenate(parts, axis=-1)


W_IN_COLS = W_F + W_QS + W_QD


def _layer_shards(w, l):
    rows = jnp.concatenate([w[k][l].reshape(-1, D_MODEL) for k in ROW_SHARDED]).astype(BF16)
    return w["w_in"][l].astype(BF16), w["w_ffn_in"][l].astype(BF16), rows


ROW_COUNTS = (3 * D_MODEL // N_DEV, D_MODEL // N_DEV, FF_HIDDEN // N_DEV)


def _w_in_weights(g_in):
    sh = [g_in[k] for k in range(N_DEV)]
    return dict(w_in_f=jnp.concatenate([_cols_of_shards(sh, 0, 2 * D_MODEL), _cols_of_shards(sh, W_IN_COLS - 3 * D_MODEL, W_IN_COLS)], axis=-1),
                w_in_qs=_cols_of_shards(sh, 2 * D_MODEL, 2 * D_MODEL + W_QS),
                w_in_qd=_cols_of_shards(sh, 2 * D_MODEL + W_QS, 2 * D_MODEL + W_QS + W_QD))


def _late_weights(g_fi, g_rows):
    p = dict(w_ffn_in=jnp.concatenate([g_fi[k] for k in range(N_DEV)], axis=-1))
    off = 0
    for k, n in zip(ROW_SHARDED, ROW_COUNTS):
        t = g_rows[:, off:off + n]
        if k == "w_branch":
            p[k] = jnp.transpose(t.reshape(N_DEV, 3, n // 3, D_MODEL), (1, 0, 2, 3)).reshape(3, -1, D_MODEL)
        else:
            p[k] = t.reshape(-1, D_MODEL)
        off += n
    return p


def _w_in_slots(g):
    shard = W_IN_COLS // N_DEV
    return jnp.stack([_cols_of_w_in(g, k * shard, (k + 1) * shard) for k in range(N_DEV)]).astype(BF16)


def _late_slots(g):
    shard = g["w_ffn_in"].shape[-1] // N_DEV
    s_fi = jnp.stack([g["w_ffn_in"][:, k * shard:(k + 1) * shard] for k in range(N_DEV)]).astype(BF16)
    rows = jnp.concatenate([jnp.transpose(g["w_branch"].reshape(3, N_DEV, -1, D_MODEL), (1, 0, 2, 3)).reshape(N_DEV, -1, D_MODEL),
                            g["w_out"].reshape(N_DEV, -1, D_MODEL), g["w_ffn_out"].reshape(N_DEV, -1, D_MODEL)], axis=1).astype(BF16)
    return s_fi, rows


def _pad_rows(flat, tile_rows):
    n = flat.shape[0]
    per = tile_rows * LANES
    total = -(-n // per) * per
    return jnp.pad(flat, (0, total - n)).reshape(-1, LANES)


def kernel(x, w_in, conv_w, conv_b, w_rg, b_rg, w_ig, b_ig, lru_lambda, sinks, w_branch, w_out, ln1_g, ln1_b, w_ffn_in, w_ffn_out, ln2_g, ln2_b, loss_target, m_w_in, m_conv_w, m_conv_b, m_w_rg, m_b_rg, m_w_ig, m_b_ig, m_lru_lambda, m_sinks, m_w_branch, m_w_out, m_ln1_g, m_ln1_b, m_w_ffn_in, m_w_ffn_out, m_ln2_g, m_ln2_b, v_w_in, v_conv_w, v_conv_b, v_w_rg, v_b_rg, v_w_ig, v_b_ig, v_lru_lambda, v_sinks, v_w_branch, v_w_out, v_ln1_g, v_ln1_b, v_w_ffn_in, v_w_ffn_out, v_ln2_g, v_ln2_b):
    w = dict(w_in=w_in, conv_w=conv_w, conv_b=conv_b, w_rg=w_rg, b_rg=b_rg, w_ig=w_ig, b_ig=b_ig, lru_lambda=lru_lambda, sinks=sinks,
             w_branch=w_branch, w_out=w_out, ln1_g=ln1_g, ln1_b=ln1_b, w_ffn_in=w_ffn_in, w_ffn_out=w_ffn_out, ln2_g=ln2_g, ln2_b=ln2_b)
    m = dict(w_in=m_w_in, conv_w=m_conv_w, conv_b=m_conv_b, w_rg=m_w_rg, b_rg=m_b_rg, w_ig=m_w_ig, b_ig=m_b_ig, lru_lambda=m_lru_lambda,
             sinks=m_sinks, w_branch=m_w_branch, w_out=m_w_out, ln1_g=m_ln1_g, ln1_b=m_ln1_b, w_ffn_in=m_w_ffn_in, w_ffn_out=m_w_ffn_out,
             ln2_g=m_ln2_g, ln2_b=m_ln2_b)
    v = dict(w_in=v_w_in, conv_w=v_conv_w, conv_b=v_conv_b, w_rg=v_w_rg, b_rg=v_b_rg, w_ig=v_w_ig, b_ig=v_b_ig, lru_lambda=v_lru_lambda,
             sinks=v_sinks, w_branch=v_w_branch, w_out=v_w_out, ln1_g=v_ln1_g, ln1_b=v_ln1_b, w_ffn_in=v_w_ffn_in, w_ffn_out=v_w_ffn_out,
             ln2_g=v_ln2_g, ln2_b=v_ln2_b)
    order = ["w_in", "conv_w", "conv_b", "w_rg", "b_rg", "w_ig", "b_ig", "lru_lambda", "sinks", "w_branch", "w_out", "ln1_g", "ln1_b",
             "w_ffn_in", "w_ffn_out", "ln2_g", "ln2_b"]
    me = 4 * lax.axis_index("x") + 2 * lax.axis_index("y") + lax.axis_index("c")

    names = ("w_in", "w_ffn_in", "w_rows")
    shards = [_layer_shards(w, l) for l in range(DEPTH)]
    first_w_in = exchange("gather", shards[0][0], name="gather_w_in")
    cw = exchange("gather", conv_w.reshape(-1, LANES), name="gather_conv_w")
    conv_w_full = jnp.moveaxis(cw.reshape(N_DEV, DEPTH, CONV_WIDTH, LANES), 0, 2).reshape(DEPTH, CONV_WIDTH, D_MODEL)

    def layer_params(l, gathered_w_in):
        return {**_w_in_weights(gathered_w_in), **{k: w[k][l] for k in SMALL}, "conv_w": conv_w_full[l]}

    loss_local, grad_x, grads, received, w_in_slots = local_step(x, loss_target, layer_params, shards, first_w_in)
    loss = lax.psum(loss_local, ("x", "y", "c"))
    received[0][0] = exchange("a2a", w_in_slots, name="exchange_g_w_in")

    sums = [[sum_slots(t, name=f"sum_g_{n}") for t, n in zip(received[l], names)] for l in range(DEPTH)]
    g_final = {"w_in": jnp.stack([sums[l][0] for l in range(DEPTH)]), "w_ffn_in": jnp.stack([sums[l][1] for l in range(DEPTH)])}
    off = 0
    for k, n in zip(ROW_SHARDED, ROW_COUNTS):
        g_final[k] = jnp.stack([sums[l][2][off:off + n] for l in range(DEPTH)]).reshape(w[k].shape)
        off += n
    grads = {k: jnp.stack([grads[l][k] for l in range(DEPTH)]) for k in list(SMALL) + ["conv_w"]}

    small_names = list(SMALL) + ["conv_w"]
    small_sizes = [grads[k].size for k in small_names]
    svec = _pad_rows(jnp.concatenate([grads[k].reshape(-1) for k in small_names]), SMALL_ROWS_TILE)
    ssum = sum_slots(exchange("gather", svec, name="gather_small_grads"), name="sum_small_grads")
    sflat, off = ssum.reshape(-1), 0
    for k, n in zip(small_names, small_sizes):
        g_final[k] = sflat[off:off + n].reshape(grads[k].shape)
        off += n
    g_final["conv_w"] = lax.dynamic_slice_in_dim(g_final["conv_w"], me * LANES, LANES, axis=2)

    delta, new_m, new_v = {}, {}, {}
    for k in list(BIG) + ["conv_w"]:
        cols = w[k].shape[-1]
        two_d = lambda t: t.reshape(-1, cols)
        d_, m_, v_ = adamw(two_d(w[k]), two_d(g_final[k]), two_d(m[k]), two_d(v[k]), name=f"adamw_{k}")
        delta[k], new_m[k], new_v[k] = d_.reshape(w[k].shape), m_.reshape(w[k].shape), v_.reshape(w[k].shape)
    pack_small = lambda dct: _pad_rows(jnp.concatenate([dct[k].reshape(-1) for k in SMALL]), SMALL_ROWS_TILE)
    d_, m_, v_ = adamw(pack_small(w), pack_small(g_final), pack_small(m), pack_small(v), name="adamw_small")
    off = 0
    for k in SMALL:
        n = w[k].size
        for dst, src in ((delta, d_), (new_m, m_), (new_v, v_)):
            dst[k] = src.reshape(-1)[off:off + n].reshape(w[k].shape)
        off += n
    return (loss, grad_x, *[g_final[k] for k in order], *[delta[k] for k in order], *[new_m[k] for k in order], *[new_v[k] for k in order])
```

```python
import functools
import math

import jax
import jax.numpy as jnp
from jax import lax
from jax.experimental import pallas as pl
from jax.experimental.pallas import tpu as pltpu

F32 = jnp.float32
BF16 = jnp.bfloat16

N_DEV = 8
DEPTH = 4
D_MODEL = 1024
HEAD_DIM = 64
LANES = 128
N_HEADS = D_MODEL // HEAD_DIM
SWA_KV_HEADS = 4
ATT_BLOCK = 128
DILATIONS = (1, 4, 16)
CONV_WIDTH = 4
LRU_C = 8.0
FF_HIDDEN = 2816
ALPHA = (2.0 * DEPTH) ** 0.25
LN_EPS = 1e-5
NEG_INF = -1e30
W_F = 5 * D_MODEL
W_QS = D_MODEL + 2 * SWA_KV_HEADS * HEAD_DIM
W_QD = 3 * D_MODEL

ADAM_LR = 0.001
ADAM_B1 = 0.9
ADAM_B2 = 0.999
ADAM_EPS = 1e-08
ADAM_WD = 0.01
ADAM_STEP = 10

VMEM_LIMIT = 56 * 1024 * 1024
MATMUL_BLOCK_BYTES = 40 * 1024 * 1024
MESH = pl.DeviceIdType.MESH


def _pick(n, cands):
    for c in cands:
        if n % c == 0:
            return c
    raise ValueError(f"no tile for {n} among {cands}")


def _params(sem):
    return pltpu.CompilerParams(dimension_semantics=sem, vmem_limit_bytes=VMEM_LIMIT)


def _tile(n, cap):
    best = None
    for t in range(LANES, cap + 1, LANES):
        if n % t == 0:
            best = t
    assert best is not None, (n, cap)
    return best


def matmul(a, b, *, name, trans_a=False, trans_b=False, out_dtype=F32, add=None, add_scale=1.0):
    if trans_a:
        k_dim, m_dim = a.shape
    else:
        m_dim, k_dim = a.shape
    n_dim = b.shape[0] if trans_b else b.shape[1]
    assert (b.shape[1] if trans_b else b.shape[0]) == k_dim
    tn = _tile(n_dim, 1408)
    tm, tk = _tile(m_dim, 1024), _tile(k_dim, 1408)
    for cand in (1024, 512, 256):
        ctm = _tile(m_dim, cand)
        blocks = 2 * (ctm * k_dim * a.dtype.itemsize + tn * k_dim * b.dtype.itemsize + ctm * tn * jnp.dtype(out_dtype).itemsize
                      + (ctm * tn * add.dtype.itemsize if add is not None else 0))
        if blocks <= MATMUL_BLOCK_BYTES:
            tm, tk = ctm, k_dim
            break
    nk = k_dim // tk
    dims = (((0 if trans_a else 1,), (1 if trans_b else 0,)), ((), ()))

    def body(*refs):
        if add is None:
            a_ref, b_ref, o_ref, acc_ref = refs
            add_ref = None
        else:
            a_ref, b_ref, add_ref, o_ref, acc_ref = refs
        k = pl.program_id(2)
        part = lax.dot_general(a_ref[...].astype(BF16), b_ref[...].astype(BF16), dims, preferred_element_type=F32)

        def finish(r):
            if add_ref is not None:
                r = r + add_scale * add_ref[...].astype(F32)
            o_ref[...] = r.astype(out_dtype)

        if nk == 1:
            finish(part)
        else:
            @pl.when(k == 0)
            def _():
                acc_ref[...] = part

            @pl.when((k > 0) & (k < nk - 1))
            def _():
                acc_ref[...] += part

            @pl.when(k == nk - 1)
            def _():
                finish(acc_ref[...] + part)

    a_spec = pl.BlockSpec((tk, tm), lambda i, j, k: (k, i)) if trans_a else pl.BlockSpec((tm, tk), lambda i, j, k: (i, k))
    b_spec = pl.BlockSpec((tn, tk), lambda i, j, k: (j, k)) if trans_b else pl.BlockSpec((tk, tn), lambda i, j, k: (k, j))
    in_specs = [a_spec, b_spec]
    args = [a, b]
    if add is not None:
        in_specs.append(pl.BlockSpec((tm, tn), lambda i, j, k: (i, j)))
        args.append(add)
    return pl.pallas_call(
        body,
        out_shape=jax.ShapeDtypeStruct((m_dim, n_dim), out_dtype),
        grid=(m_dim // tm, n_dim // tn, nk),
        in_specs=in_specs,
        out_specs=pl.BlockSpec((tm, tn), lambda i, j, k: (i, j)),
        scratch_shapes=[pltpu.VMEM((tm, tn) if nk > 1 else (8, LANES), F32)],
        compiler_params=_params(("parallel", "parallel", "arbitrary")),
        name=name,
    )(*args)


def ln_fwd(x, r, g, b, *, name):
    t_dim, d = x.shape
    tr = _pick(t_dim, (256, 128, 8))

    def body(x_ref, r_ref, g_ref, b_ref, y_ref, yb_ref, z_ref):
        z = ALPHA * x_ref[...] + r_ref[...]
        mu = jnp.mean(z, axis=-1, keepdims=True)
        zc = z - mu
        var = jnp.mean(zc * zc, axis=-1, keepdims=True)
        y = zc * lax.rsqrt(var + LN_EPS) * g_ref[...] + b_ref[...]
        y_ref[...] = y
        yb_ref[...] = y.astype(BF16)
        z_ref[...] = z

    row = pl.BlockSpec((tr, d), lambda i: (i, 0))
    vec = pl.BlockSpec((1, d), lambda i: (0, 0))
    return pl.pallas_call(
        body,
        out_shape=(jax.ShapeDtypeStruct((t_dim, d), F32), jax.ShapeDtypeStruct((t_dim, d), BF16), jax.ShapeDtypeStruct((t_dim, d), F32)),
        grid=(t_dim // tr,),
        in_specs=[row, row, vec, vec],
        out_specs=(row, row, row),
        compiler_params=_params(("parallel",)),
        name=name,
    )(x, r, g.reshape(1, d), b.reshape(1, d))


def ln_bwd(dy, z, g, *, name):
    t_dim, d = dy.shape
    tr = _pick(t_dim, (256, 128, 8))

    def body(dy_ref, z_ref, g_ref, dz_ref, dzb_ref, dg_ref, db_ref):
        @pl.when(pl.program_id(0) == 0)
        def _():
            dg_ref[...] = jnp.zeros_like(dg_ref)
            db_ref[...] = jnp.zeros_like(db_ref)

        z = z_ref[...]
        dyv = dy_ref[...]
        mu = jnp.mean(z, axis=-1, keepdims=True)
        zc = z - mu
        var = jnp.mean(zc * zc, axis=-1, keepdims=True)
        rstd = lax.rsqrt(var + LN_EPS)
        xhat = zc * rstd
        dxhat = dyv * g_ref[...]
        m1 = jnp.mean(dxhat, axis=-1, keepdims=True)
        m2 = jnp.mean(dxhat * xhat, axis=-1, keepdims=True)
        dz = rstd * (dxhat - m1 - xhat * m2)
        dz_ref[...] = dz
        dzb_ref[...] = dz.astype(BF16)
        dg_ref[...] += jnp.sum(dyv * xhat, axis=0, keepdims=True)
        db_ref[...] += jnp.sum(dyv, axis=0, keepdims=True)

    row = pl.BlockSpec((tr, d), lambda i: (i, 0))
    vec = pl.BlockSpec((1, d), lambda i: (0, 0))
    return pl.pallas_call(
        body,
        out_shape=(jax.ShapeDtypeStruct((t_dim, d), F32), jax.ShapeDtypeStruct((t_dim, d), BF16),
                   jax.ShapeDtypeStruct((1, d), F32), jax.ShapeDtypeStruct((1, d), F32)),
        grid=(t_dim // tr,),
        in_specs=[row, row, vec],
        out_specs=(row, row, vec, vec),
        compiler_params=_params(("arbitrary",)),
        name=name,
    )(dy, z, g.reshape(1, d))


def loss_head(y, target, *, name):
    t_dim, d = y.shape
    tr = _pick(t_dim, (256, 128, 8))

    def body(y_ref, t_ref, dy_ref, sq_ref):
        @pl.when(pl.program_id(0) == 0)
        def _():
            sq_ref[...] = jnp.zeros_like(sq_ref)

        diff = y_ref[...] - t_ref[...]
        dy_ref[...] = diff / d
        sq_ref[...] += jnp.sum(diff * diff, axis=0, keepdims=True)

    row = pl.BlockSpec((tr, d), lambda i: (i, 0))
    vec = pl.BlockSpec((1, d), lambda i: (0, 0))
    return pl.pallas_call(
        body,
        out_shape=(jax.ShapeDtypeStruct((t_dim, d), F32), jax.ShapeDtypeStruct((1, d), F32)),
        grid=(t_dim // tr,),
        in_specs=[row, row],
        out_specs=(row, vec),
        compiler_params=_params(("arbitrary",)),
        name=name,
    )(y, target)


def _sigmoid(x):
    return 1.0 / (1.0 + jnp.exp(-x))


def swiglu_fwd(h13, *, name):
    t_dim = h13.shape[0]
    f = h13.shape[1] // 2
    tr = _pick(t_dim, (256, 128, 8))

    def body(h1_ref, h3_ref, act_ref):
        h1 = h1_ref[...]
        act_ref[...] = (h1 * _sigmoid(h1) * h3_ref[...]).astype(BF16)

    return pl.pallas_call(
        body,
        out_shape=jax.ShapeDtypeStruct((t_dim, f), BF16),
        grid=(t_dim // tr,),
        in_specs=[pl.BlockSpec((tr, f), lambda i: (i, 0)), pl.BlockSpec((tr, f), lambda i: (i, 1))],
        out_specs=pl.BlockSpec((tr, f), lambda i: (i, 0)),
        compiler_params=_params(("parallel",)),
        name=name,
    )(h13, h13)


def swiglu_bwd(dact, h13, *, name):
    t_dim = h13.shape[0]
    f = h13.shape[1] // 2
    tr = _pick(t_dim, (256, 128, 8))

    def body(da_ref, h1_ref, h3_ref, dh_ref):
        h1 = h1_ref[...]
        da = da_ref[...]
        sg = _sigmoid(h1)
        dh_ref[:, :f] = (da * h3_ref[...] * sg * (1.0 + h1 * (1.0 - sg))).astype(BF16)
        dh_ref[:, f:] = (da * h1 * sg).astype(BF16)

    return pl.pallas_call(
        body,
        out_shape=jax.ShapeDtypeStruct((t_dim, 2 * f), BF16),
        grid=(t_dim // tr,),
        in_specs=[pl.BlockSpec((tr, f), lambda i: (i, 0)), pl.BlockSpec((tr, f), lambda i: (i, 0)),
                  pl.BlockSpec((tr, f), lambda i: (i, 1))],
        out_specs=pl.BlockSpec((tr, 2 * f), lambda i: (i, 0)),
        compiler_params=_params(("parallel",)),
        name=name,
    )(dact, h13, h13)


def merge_fwd(proj_f, br, *, name):
    t_dim, d = br[0].shape
    tr = _pick(t_dim, (256, 128, 8))

    def body(g0, g1, g2, b0, b1, b2, o_ref):
        o_ref[...] = (_sigmoid(g0[...]) * b0[...] + _sigmoid(g1[...]) * b1[...] + _sigmoid(g2[...]) * b2[...]).astype(BF16)

    row = pl.BlockSpec((tr, d), lambda i: (i, 0))
    gate = [pl.BlockSpec((tr, d), functools.partial(lambda n, i: (i, 2 + n), n)) for n in range(3)]
    return pl.pallas_call(
        body,
        out_shape=jax.ShapeDtypeStruct((t_dim, d), BF16),
        grid=(t_dim // tr,),
        in_specs=gate + [row, row, row],
        out_specs=row,
        compiler_params=_params(("parallel",)),
        name=name,
    )(proj_f, proj_f, proj_f, *br)


def merge_bwd(dmerged, proj_f, br, *, name):
    t_dim, d = dmerged.shape
    tr = _pick(t_dim, (256, 128, 8))

    def body(dm_ref, g0, g1, g2, b0, b1, b2, d0, d1, d2, dg_ref):
        dm = dm_ref[...]
        for n, (g, b, o) in enumerate(((g0, b0, d0), (g1, b1, d1), (g2, b2, d2))):
            sg = _sigmoid(g[...])
            o[...] = (dm * sg).astype(BF16)
            dg_ref[:, n * d:(n + 1) * d] = (dm * b[...] * sg * (1.0 - sg)).astype(BF16)

    row = pl.BlockSpec((tr, d), lambda i: (i, 0))
    gate = [pl.BlockSpec((tr, d), functools.partial(lambda n, i: (i, 2 + n), n)) for n in range(3)]
    return pl.pallas_call(
        body,
        out_shape=(jax.ShapeDtypeStruct((t_dim, d), BF16),) * 3 + (jax.ShapeDtypeStruct((t_dim, 3 * d), BF16),),
        grid=(t_dim // tr,),
        in_specs=[row] + gate + [row, row, row],
        out_specs=(row, row, row, pl.BlockSpec((tr, 3 * d), lambda i: (i, 0))),
        compiler_params=_params(("parallel",)),
        name=name,
    )(dmerged, proj_f, proj_f, proj_f, *br)


GELU_C = math.sqrt(2.0 / math.pi)
PAD = 8
SCAN_TILES = 8


def _gelu(x):
    return 0.5 * x * (1.0 + jnp.tanh(GELU_C * (x + 0.044715 * x * x * x)))


def _gelu_grad(x):
    t = jnp.tanh(GELU_C * (x + 0.044715 * x * x * x))
    return 0.5 * (1.0 + t) + 0.5 * x * (1.0 - t * t) * GELU_C * (1.0 + 3.0 * 0.044715 * x * x)


def _neg_expm1(x):
    series = -x * (1.0 + x * (0.5 + x * (1.0 / 6.0 + x * (1.0 / 24.0 + x * (1.0 / 120.0)))))
    return jnp.where(x > -0.1, series, 1.0 - jnp.exp(x))


def _lru_gates(xv, cw_ref, cb_ref, wr_ref, wi_ref, br_ref, bi_ref, lam_ref, pad_ref, s_len):
    pad_ref[pl.ds(0, PAD), :] = jnp.zeros((PAD, LANES), F32)
    pad_ref[pl.ds(PAD, s_len), :] = xv
    xc = cb_ref[...] + jnp.zeros((s_len, LANES), F32)
    for j in range(CONV_WIDTH):
        xc = xc + pad_ref[pl.ds(PAD - (CONV_WIDTH - 1) + j, s_len), :] * cw_ref[pl.ds(j, 1), :]
    xcb = xc.astype(BF16)
    r = _sigmoid(jnp.dot(xcb, wr_ref[0].astype(BF16), preferred_element_type=F32) + br_ref[...])
    i = _sigmoid(jnp.dot(xcb, wi_ref[0].astype(BF16), preferred_element_type=F32) + bi_ref[...])
    nl = -lam_ref[...]
    sp = jnp.maximum(nl, 0.0) + jnp.log(1.0 + jnp.exp(-jnp.abs(nl)))
    log_a = -LRU_C * r * sp
    a = jnp.exp(log_a)
    mult = jnp.sqrt(_neg_expm1(2.0 * log_a))
    return xc, r, i, sp, a, mult


def _tile_scan(a, b, row, reverse):
    for s in (1, 2, 4):
        if reverse:
            a_sh = pltpu.roll(a, 8 - s, 0)
            b_sh = pltpu.roll(b, 8 - s, 0)
            m = row + s <= 7
        else:
            a_sh = pltpu.roll(a, s, 0)
            b_sh = pltpu.roll(b, s, 0)
            m = row >= s
        b = jnp.where(m, a * b_sh + b, b)
        a = jnp.where(m, a * a_sh, a)
    return a, b


def lru_fwd(proj_f, conv_w, conv_b, wr_bd, wi_bd, b_rg, b_ig, lam, *, name, carried=None):
    bsz, s_len, _ = proj_f.shape
    d = D_MODEL
    ncb = d // LANES
    n_tiles = s_len // 8

    def body(x_ref, g_ref, cw_ref, cb_ref, wr_ref, wi_ref, br_ref, bi_ref, lam_ref, y_ref, h_ref, pad_ref, a_s, b_s):
        xc, r, i, sp, a, mult = _lru_gates(x_ref[0], cw_ref, cb_ref, wr_ref, wi_ref, br_ref, bi_ref, lam_ref, pad_ref, s_len)
        a_s[...] = a
        b_s[...] = mult * (i * xc)
        row = lax.broadcasted_iota(jnp.int32, (8, LANES), 0)

        def tiles(t, carry):
            starts = [pl.multiple_of((t * SCAN_TILES + u) * 8, 8) for u in range(SCAN_TILES)]
            local = [_tile_scan(a_s[pl.ds(i0, 8), :], b_s[pl.ds(i0, 8), :], row, False) for i0 in starts]
            for i0, (ac, hl) in zip(starts, local):
                h = hl + ac * carry
                h_ref[0, pl.ds(i0, 8), :] = h
                carry = jnp.broadcast_to(h[7:8, :], (8, LANES))
            return carry

        lax.fori_loop(0, n_tiles // SCAN_TILES, tiles, jnp.zeros((8, LANES), F32))
        y_ref[0] = (h_ref[0] * _gelu(g_ref[0])).astype(BF16)

    slab = lambda off: pl.BlockSpec((1, s_len, LANES), functools.partial(lambda o, c, b: (b, 0, o + c), off))
    vec = pl.BlockSpec((1, LANES), lambda c, b: (0, c))
    mat = pl.BlockSpec((1, LANES, LANES), lambda c, b: (c, 0, 0))
    out = pl.BlockSpec((1, s_len, LANES), lambda c, b: (b, 0, c))
    return call_with_exchange(
        body, carried,
        out_shape=(jax.ShapeDtypeStruct((bsz, s_len, d), BF16), jax.ShapeDtypeStruct((bsz, s_len, d), F32)),
        grid=(ncb, bsz),
        in_specs=[slab(0), slab(ncb), pl.BlockSpec((CONV_WIDTH, LANES), lambda c, b: (0, c)), vec, mat, mat, vec, vec, vec],
        out_specs=(out, out),
        scratch_shapes=[pltpu.VMEM((s_len + 2 * PAD, LANES), F32), pltpu.VMEM((s_len, LANES), F32), pltpu.VMEM((s_len, LANES), F32)],
        name=name,
        args=(proj_f, proj_f, conv_w, conv_b.reshape(1, d), wr_bd, wi_bd, b_rg.reshape(1, d), b_ig.reshape(1, d), lam.reshape(1, d)))


def lru_bwd(dy, proj_f, h, conv_w, conv_b, wr_bd, wi_bd, wr_bd_t, wi_bd_t, b_rg, b_ig, lam, *, name, carried=None):
    bsz, s_len, _ = proj_f.shape
    d = D_MODEL
    ncb = d // LANES
    n_tiles = s_len // 8

    def body(dy_ref, x_ref, g_ref, h_ref, cw_ref, cb_ref, wr_ref, wi_ref, wrt_ref, wit_ref, br_ref, bi_ref, lam_ref,
             dx_ref, dg_ref, dcw_ref, dcb_ref, dbr_ref, dbi_ref, dlam_ref, dwr_ref, dwi_ref, pad_ref, a_s, b_s, l_s):
        @pl.when(pl.program_id(1) == 0)
        def _():
            for ref in (dcw_ref, dcb_ref, dbr_ref, dbi_ref, dlam_ref, dwr_ref, dwi_ref):
                ref[...] = jnp.zeros_like(ref)

        xc, r, i, sp, a, mult = _lru_gates(x_ref[0], cw_ref, cb_ref, wr_ref, wi_ref, br_ref, bi_ref, lam_ref, pad_ref, s_len)
        gate = g_ref[0]
        hv = h_ref[0]
        dyv = dy_ref[0]
        dg_ref[0] = (dyv * hv * _gelu_grad(gate)).astype(BF16)
        b_s[...] = dyv * _gelu(gate)
        l_s[pl.ds(0, s_len), :] = a
        l_s[pl.ds(s_len, PAD), :] = jnp.zeros((PAD, LANES), F32)
        a_s[...] = l_s[pl.ds(1, s_len), :]
        row = lax.broadcasted_iota(jnp.int32, (8, LANES), 0)

        def tiles(t, carry):
            starts = [pl.multiple_of((n_tiles - 1 - (t * SCAN_TILES + u)) * 8, 8) for u in range(SCAN_TILES)]
            local = [_tile_scan(a_s[pl.ds(i0, 8), :], b_s[pl.ds(i0, 8), :], row, True) for i0 in starts]
            for i0, (ac, ll) in zip(starts, local):
                lmb = ll + ac * carry
                b_s[pl.ds(i0, 8), :] = lmb
                carry = jnp.broadcast_to(lmb[0:1, :], (8, LANES))
            return carry

        lax.fori_loop(0, n_tiles // SCAN_TILES, tiles, jnp.zeros((8, LANES), F32))
        lmb = b_s[...]
        l_s[pl.ds(0, PAD), :] = jnp.zeros((PAD, LANES), F32)
        l_s[pl.ds(PAD, s_len), :] = hv
        h_prev = l_s[pl.ds(PAD - 1, s_len), :]
        da = lmb * h_prev
        dmult = lmb * (i * xc)
        di = lmb * mult * xc
        dxc = lmb * mult * i
        dlog_a = da * a - dmult * a * a / mult
        dr = -LRU_C * sp * dlog_a
        dsp = jnp.sum(-LRU_C * r * dlog_a, axis=0, keepdims=True)
        dlam_ref[...] += dsp * (-_sigmoid(-lam_ref[...]))
        dpr = dr * r * (1.0 - r)
        dpi = di * i * (1.0 - i)
        dprb = dpr.astype(BF16)
        dpib = dpi.astype(BF16)
        xcb = xc.astype(BF16)
        dbr_ref[...] += jnp.sum(dpr, axis=0, keepdims=True)
        dbi_ref[...] += jnp.sum(dpi, axis=0, keepdims=True)
        tn = (((0,), (0,)), ((), ()))
        dwr_ref[0] += lax.dot_general(xcb, dprb, tn, preferred_element_type=F32)
        dwi_ref[0] += lax.dot_general(xcb, dpib, tn, preferred_element_type=F32)
        dxc = (dxc + jnp.dot(dprb, wrt_ref[0].astype(BF16), preferred_element_type=F32)
               + jnp.dot(dpib, wit_ref[0].astype(BF16), preferred_element_type=F32))
        dcb_ref[...] += jnp.sum(dxc, axis=0, keepdims=True)
        for j in range(CONV_WIDTH):
            dcw_ref[pl.ds(j, 1), :] += jnp.sum(dxc * pad_ref[pl.ds(PAD - (CONV_WIDTH - 1) + j, s_len), :], axis=0, keepdims=True)
        l_s[pl.ds(0, s_len), :] = dxc
        l_s[pl.ds(s_len, PAD), :] = jnp.zeros((PAD, LANES), F32)
        dx = jnp.zeros((s_len, LANES), F32)
        for j in range(CONV_WIDTH):
            dx = dx + l_s[pl.ds(CONV_WIDTH - 1 - j, s_len), :] * cw_ref[pl.ds(j, 1), :]
        dx_ref[0] = dx.astype(BF16)

    slab = lambda off: pl.BlockSpec((1, s_len, LANES), functools.partial(lambda o, c, b: (b, 0, o + c), off))
    vec = pl.BlockSpec((1, LANES), lambda c, b: (0, c))
    mat = pl.BlockSpec((1, LANES, LANES), lambda c, b: (c, 0, 0))
    cw = pl.BlockSpec((CONV_WIDTH, LANES), lambda c, b: (0, c))
    out = pl.BlockSpec((1, s_len, LANES), lambda c, b: (b, 0, c))
    vshape = jax.ShapeDtypeStruct((1, d), F32)
    mshape = jax.ShapeDtypeStruct((ncb, LANES, LANES), F32)
    return call_with_exchange(
        body, carried,
        out_shape=(jax.ShapeDtypeStruct((bsz, s_len, d), BF16),) * 2
        + (jax.ShapeDtypeStruct((CONV_WIDTH, d), F32), vshape, vshape, vshape, vshape, mshape, mshape),
        grid=(ncb, bsz),
        in_specs=[out, slab(0), slab(ncb), out, cw, vec, mat, mat, mat, mat, vec, vec, vec],
        out_specs=(out, out, cw, vec, vec, vec, vec, mat, mat),
        scratch_shapes=[pltpu.VMEM((s_len + 2 * PAD, LANES), F32), pltpu.VMEM((s_len, LANES), F32), pltpu.VMEM((s_len, LANES), F32),
                        pltpu.VMEM((s_len + 2 * PAD, LANES), F32)],
        name=name,
        args=(dy, proj_f, proj_f, h, conv_w, conv_b.reshape(1, d), wr_bd, wi_bd, wr_bd_t, wi_bd_t,
              b_rg.reshape(1, d), b_ig.reshape(1, d), lam.reshape(1, d)))


def _kv_place(head, n_kv_heads):
    kv = head // (N_HEADS // n_kv_heads)
    return kv // 2, kv % 2


def _band_mask(n, single):
    if single:
        qi = lax.broadcasted_iota(jnp.int32, (ATT_BLOCK, ATT_BLOCK), 0)
        return qi >= lax.broadcasted_iota(jnp.int32, (ATT_BLOCK, ATT_BLOCK), 1)
    qi = lax.broadcasted_iota(jnp.int32, (ATT_BLOCK, 2 * ATT_BLOCK), 0)
    kj = lax.broadcasted_iota(jnp.int32, (ATT_BLOCK, 2 * ATT_BLOCK), 1)
    rel = qi + ATT_BLOCK - kj
    return (rel >= 0) & (rel <= ATT_BLOCK) & ((n > 0) | (kj >= ATT_BLOCK))


def _half_masks(dtype):
    lane = lax.broadcasted_iota(jnp.int32, (1, LANES), 1)
    return [(lane < HEAD_DIM).astype(dtype), (lane >= HEAD_DIM).astype(dtype)]


NT = (((1,), (1,)), ((), ()))
TN = (((0,), (0,)), ((), ()))


def _qkv_specs(dil, q_blk, k_blk, v_blk, ckv, clamp):
    qw = D_MODEL // LANES * LANES
    return [
        pl.BlockSpec((1, ATT_BLOCK, qw), lambda b, j, n: (b, clamp(n), j * (q_blk[1]) + q_blk[0])),
        pl.BlockSpec((1, ATT_BLOCK, ckv), lambda b, j, n: (b, jnp.maximum(clamp(n) - 1, 0), j * k_blk[1] + k_blk[0])),
        pl.BlockSpec((1, ATT_BLOCK, ckv), lambda b, j, n: (b, clamp(n), j * k_blk[1] + k_blk[0])),
        pl.BlockSpec((1, ATT_BLOCK, ckv), lambda b, j, n: (b, jnp.maximum(clamp(n) - 1, 0), j * v_blk[1] + v_blk[0])),
        pl.BlockSpec((1, ATT_BLOCK, ckv), lambda b, j, n: (b, clamp(n), j * v_blk[1] + v_blk[0])),
    ]


def attn_fwd(qkv, *, dil, n_kv_heads, sinks, name, emit_bf16=False):
    bsz, s_len, width = qkv.shape
    ckv = n_kv_heads * HEAD_DIM
    l_sub = s_len // dil
    nb = l_sub // ATT_BLOCK
    view = qkv.reshape(bsz, l_sub, dil * width)
    scale = HEAD_DIM ** -0.5
    q_blk = (0, width // D_MODEL)
    k_blk = (D_MODEL // ckv, width // ckv)
    v_blk = (D_MODEL // ckv + 1, width // ckv)
    assert (dil == 1 or width % D_MODEL == 0) and width % ckv == 0 and D_MODEL % ckv == 0

    single = nb == 1

    def body(*refs):
        refs = list(refs)
        sink_ref = refs.pop(0) if sinks is not None else None
        ob_ref = refs.pop() if emit_bf16 else None
        q_ref, kp_ref, kc_ref, vp_ref, vc_ref, o_ref, lse_ref = refs
        n = pl.program_id(2)
        mask = _band_mask(n, single)
        hm = _half_masks(BF16)
        hmf = _half_masks(F32)
        kk = kc_ref[0] if single else jnp.concatenate([kp_ref[0], kc_ref[0]], axis=0)
        vv = vc_ref[0] if single else jnp.concatenate([vp_ref[0], vc_ref[0]], axis=0)
        for hp in range(N_HEADS // 2):
            q2 = q_ref[0, :, hp * LANES:(hp + 1) * LANES]
            o2 = jnp.zeros((ATT_BLOCK, LANES), F32)
            l2 = jnp.zeros((ATT_BLOCK, LANES), F32)
            for a in range(2):
                kb, kh = _kv_place(2 * hp + a, n_kv_heads)
                k2 = kk[:, kb * LANES:(kb + 1) * LANES]
                v2 = vv[:, kb * LANES:(kb + 1) * LANES]
                if kh != a:
                    k2 = pltpu.roll(k2, HEAD_DIM, 1)
                    v2 = pltpu.roll(v2, HEAD_DIM, 1)
                s = lax.dot_general(q2 * hm[a], k2, NT, preferred_element_type=F32) * scale
                s = jnp.where(mask, s, NEG_INF)
                m = jnp.max(s, axis=-1, keepdims=True)
                if sink_ref is not None:
                    sk = sink_ref[2 * hp + a]
                    m = jnp.maximum(m, sk)
                p = jnp.exp(s - m)
                den = jnp.sum(p, axis=-1, keepdims=True)
                if sink_ref is not None:
                    den = den + jnp.exp(sk - m)
                o2 = o2 + jnp.dot(p.astype(BF16), v2 * hm[a], preferred_element_type=F32) / den
                l2 = l2 + (m + jnp.log(den)) * hmf[a]
            o_ref[0, :, hp * LANES:(hp + 1) * LANES] = o2
            lse_ref[0, :, hp * LANES:(hp + 1) * LANES] = l2
            if ob_ref is not None:
                ob_ref[0, :, hp * LANES:(hp + 1) * LANES] = o2.astype(BF16)

    in_specs = _qkv_specs(dil, q_blk, k_blk, v_blk, ckv, lambda n: n)
    args = [view] * 5
    if sinks is not None:
        in_specs = [pl.BlockSpec(memory_space=pltpu.SMEM)] + in_specs
        args = [sinks] + args
    out = pl.BlockSpec((1, ATT_BLOCK, D_MODEL), lambda b, j, n: (b, n, j))
    res = pl.pallas_call(
        body,
        out_shape=(jax.ShapeDtypeStruct((bsz, l_sub, dil * D_MODEL), F32),) * 2
        + ((jax.ShapeDtypeStruct((bsz, l_sub, dil * D_MODEL), BF16),) if emit_bf16 else ()),
        grid=(bsz, dil, nb),
        in_specs=in_specs,
        out_specs=(out,) * (3 if emit_bf16 else 2),
        compiler_params=_params(("parallel", "parallel", "arbitrary")),
        name=name,
    )(*args)
    return tuple(t.reshape(bsz, s_len, D_MODEL) for t in res)


def attn_bwd(qkv, o, lse, do, acc, *, dil, n_kv_heads, name):
    bsz, s_len, width = qkv.shape
    ckv = n_kv_heads * HEAD_DIM
    l_sub = s_len // dil
    nb = l_sub // ATT_BLOCK
    view = qkv.reshape(bsz, l_sub, dil * width)
    scale = HEAD_DIM ** -0.5
    q_blk = (0, width // D_MODEL)
    k_blk = (D_MODEL // ckv, width // ckv)
    v_blk = (D_MODEL // ckv + 1, width // ckv)
    single = nb == 1

    def body(*refs):
        if acc is None:
            q_ref, kp_ref, kc_ref, vp_ref, vc_ref, o_ref, lse_ref, do_ref, dq_ref, dk_ref, dv_ref, dkk, dvv, ck, cv = refs
            aq_ref = ak_ref = av_ref = None
        else:
            (q_ref, kp_ref, kc_ref, vp_ref, vc_ref, o_ref, lse_ref, do_ref, aq_ref, ak_ref, av_ref,
             dq_ref, dk_ref, dv_ref, dkk, dvv, ck, cv) = refs
        n = pl.program_id(2)

        @pl.when(n < nb)
        def _():
            mask = _band_mask(n, single)
            hm = _half_masks(BF16)
            hmf = _half_masks(F32)
            kk = kc_ref[0] if single else jnp.concatenate([kp_ref[0], kc_ref[0]], axis=0)
            vv = vc_ref[0] if single else jnp.concatenate([vp_ref[0], vc_ref[0]], axis=0)
            krows = pl.ds(ATT_BLOCK, ATT_BLOCK) if single else pl.ds(0, 2 * ATT_BLOCK)
            dkk[...] = jnp.zeros_like(dkk)
            dvv[...] = jnp.zeros_like(dvv)
            for hp in range(N_HEADS // 2):
                cols = slice(hp * LANES, (hp + 1) * LANES)
                q2 = q_ref[0, :, cols]
                do2f = do_ref[0, :, cols]
                do2 = do2f.astype(BF16)
                dd2 = do2f * o_ref[0, :, cols]
                l2 = lse_ref[0, :, cols]
                dq2 = jnp.zeros((ATT_BLOCK, LANES), F32)
                for a in range(2):
                    kb, kh = _kv_place(2 * hp + a, n_kv_heads)
                    kcols = slice(kb * LANES, (kb + 1) * LANES)
                    k2 = kk[:, kcols]
                    v2 = vv[:, kcols]
                    if kh != a:
                        k2 = pltpu.roll(k2, HEAD_DIM, 1)
                        v2 = pltpu.roll(v2, HEAD_DIM, 1)
                    qm = q2 * hm[a]
                    dom = do2 * hm[a]
                    dsum = jnp.sum(dd2 * hmf[a], axis=-1, keepdims=True)
                    lse_h = jnp.max(jnp.where(hmf[a] > 0.5, l2, NEG_INF), axis=-1, keepdims=True)
                    s = lax.dot_general(qm, k2, NT, preferred_element_type=F32) * scale
                    s = jnp.where(mask, s, NEG_INF)
                    p = jnp.exp(s - lse_h)
                    dp = lax.dot_general(dom, v2, NT, preferred_element_type=F32)
                    ds = (p * (dp - dsum) * scale).astype(BF16)
                    dq2 = dq2 + jnp.dot(ds, k2 * hm[a], preferred_element_type=F32)
                    dk_c = lax.dot_general(ds, qm, TN, preferred_element_type=F32)
                    dv_c = lax.dot_general(p.astype(BF16), dom, TN, preferred_element_type=F32)
                    if kh != a:
                        dk_c = pltpu.roll(dk_c, HEAD_DIM, 1)
                        dv_c = pltpu.roll(dv_c, HEAD_DIM, 1)
                    dkk[krows, kcols] += dk_c
                    dvv[krows, kcols] += dv_c
                if aq_ref is not None:
                    dq2 = dq2 + aq_ref[0, :, cols]
                dq_ref[0, :, cols] = dq2

        @pl.when((n >= 1) & (n < nb))
        def _():
            dk_ref[0] = ck[...] + dkk[pl.ds(0, ATT_BLOCK), :] + (0.0 if ak_ref is None else ak_ref[0])
            dv_ref[0] = cv[...] + dvv[pl.ds(0, ATT_BLOCK), :] + (0.0 if av_ref is None else av_ref[0])

        @pl.when(n == nb)
        def _():
            dk_ref[0] = ck[...] + (0.0 if ak_ref is None else ak_ref[0])
            dv_ref[0] = cv[...] + (0.0 if av_ref is None else av_ref[0])

        @pl.when(n < nb)
        def _():
            ck[...] = dkk[pl.ds(ATT_BLOCK, ATT_BLOCK), :]
            cv[...] = dvv[pl.ds(ATT_BLOCK, ATT_BLOCK), :]

    clamp = lambda n: jnp.minimum(n, nb - 1)
    prev = lambda n: jnp.maximum(n - 1, 0)
    row = pl.BlockSpec((1, ATT_BLOCK, D_MODEL), lambda b, j, n: (b, clamp(n), j))
    kv_out = pl.BlockSpec((1, ATT_BLOCK, ckv), lambda b, j, n: (b, prev(n), j))
    in_specs = _qkv_specs(dil, q_blk, k_blk, v_blk, ckv, clamp) + [row, row, row]
    rs = lambda t: t.reshape(bsz, l_sub, dil * t.shape[-1])
    args = [view] * 5 + [rs(o), rs(lse), rs(do)]
    if acc is not None:
        in_specs += [row, kv_out, kv_out]
        args += [rs(t) for t in acc]
    dq, dk, dv = pl.pallas_call(
        body,
        out_shape=(jax.ShapeDtypeStruct((bsz, l_sub, dil * D_MODEL), F32),
                   jax.ShapeDtypeStruct((bsz, l_sub, dil * ckv), F32), jax.ShapeDtypeStruct((bsz, l_sub, dil * ckv), F32)),
        grid=(bsz, dil, nb + 1),
        in_specs=in_specs,
        out_specs=(row, kv_out, kv_out),
        scratch_shapes=[pltpu.VMEM((2 * ATT_BLOCK, ckv), F32), pltpu.VMEM((2 * ATT_BLOCK, ckv), F32),
                        pltpu.VMEM((ATT_BLOCK, ckv), F32), pltpu.VMEM((ATT_BLOCK, ckv), F32)],
        compiler_params=_params(("parallel", "parallel", "arbitrary")),
        name=name,
    )(*args)
    return dq.reshape(bsz, s_len, D_MODEL), dk.reshape(bsz, s_len, ckv), dv.reshape(bsz, s_len, ckv)


def dil_combine(os_, lses, *, name):
    t_dim, d = os_[0].shape
    tr = _pick(t_dim, (256, 128, 8))

    def body(o0, o1, o2, l0, l1, l2, y_ref, lt_ref, yb_ref):
        la, lb, lc = l0[...], l1[...], l2[...]
        m = jnp.maximum(jnp.maximum(la, lb), lc)
        ea, eb, ec = jnp.exp(la - m), jnp.exp(lb - m), jnp.exp(lc - m)
        tot = ea + eb + ec
        y = (ea / tot) * o0[...] + (eb / tot) * o1[...] + (ec / tot) * o2[...]
        y_ref[...] = y
        yb_ref[...] = y.astype(BF16)
        lt_ref[...] = m + jnp.log(tot)

    row = pl.BlockSpec((tr, d), lambda i: (i, 0))
    return pl.pallas_call(
        body,
        out_shape=(jax.ShapeDtypeStruct((t_dim, d), F32),) * 2 + (jax.ShapeDtypeStruct((t_dim, d), BF16),),
        grid=(t_dim // tr,),
        in_specs=[row] * 6,
        out_specs=(row, row, row),
        compiler_params=_params(("parallel",)),
        name=name,
    )(*os_, *lses)


ATT_SCALE = HEAD_DIM ** -0.5


def _band_mask(n, single):
    nk = ATT_BLOCK if single else 2 * ATT_BLOCK
    qi = lax.broadcasted_iota(jnp.int32, (2 * ATT_BLOCK, nk), 0) % ATT_BLOCK
    kj = lax.broadcasted_iota(jnp.int32, (2 * ATT_BLOCK, nk), 1)
    if single:
        return qi >= kj
    rel = qi + ATT_BLOCK - kj
    return (rel >= 0) & (rel <= ATT_BLOCK) & ((n > 0) | (kj >= ATT_BLOCK))


def _lane_halves():
    lane = lax.broadcasted_iota(jnp.int32, (1, LANES), 1)
    return lane < HEAD_DIM


def _stack_heads(t2, kh):
    first = _lane_halves()
    parts = []
    for a in range(2):
        ta = jnp.where(first if a == 0 else ~first, t2, jnp.zeros_like(t2))
        if a != kh[a]:
            ta = pltpu.roll(ta, HEAD_DIM, 1)
        parts.append(ta)
    return jnp.concatenate(parts, axis=0)


def _fold_heads(t, kh):
    t0, t1 = t[:ATT_BLOCK], t[ATT_BLOCK:]
    if t.shape[1] == LANES:
        if kh[0] != 0:
            t0 = pltpu.roll(t0, HEAD_DIM, 1)
        if kh[1] != 1:
            t1 = pltpu.roll(t1, HEAD_DIM, 1)
    return jnp.where(_lane_halves(), t0, t1)


def _rows_of_heads(t2):
    return jnp.concatenate([t2[:, 0:1], t2[:, HEAD_DIM:HEAD_DIM + 1]], axis=0)


PAIRS_AT_ONCE = 4


def _fill_bias(bias2_ref, bias1_ref=None):
    for i in range(2):
        bias2_ref[i] = jnp.where(_band_mask(i, False), 0.0, NEG_INF)
    if bias1_ref is not None:
        bias1_ref[...] = jnp.where(_band_mask(0, True), 0.0, NEG_INF)


def _pairs_fwd(items):
    ss = [lax.dot_general(_stack_heads(q2 * ATT_SCALE, kh), kk, NT, preferred_element_type=F32) + bias
          for q2, kk, _, bias, kh, _ in items]
    ps, ms, ls = [], [], []
    for s, (_, _, _, _, _, sink_col) in zip(ss, items):
        m = jnp.max(s, axis=-1, keepdims=True)
        if sink_col is not None:
            m = jnp.maximum(m, sink_col)
        p = jnp.exp(s - m)
        l = jnp.sum(p, axis=-1, keepdims=True)
        if sink_col is not None:
            l = l + jnp.exp(sink_col - m)
        ps.append(p.astype(BF16))
        ms.append(m)
        ls.append(l)
    pvs = [jnp.dot(p, it[2], preferred_element_type=F32) for p, it in zip(ps, items)]
    return list(zip(pvs, ms, ls))


def _pairs_bwd(items):
    first = _lane_halves()
    pre = []
    for q2, kk, vv, do2, o2, lse2, bias, kh in items:
        dd = do2 * o2
        dsum = jnp.concatenate([jnp.sum(jnp.where(first, dd, 0.0), axis=-1, keepdims=True),
                                jnp.sum(jnp.where(first, 0.0, dd), axis=-1, keepdims=True)], axis=0)
        qs = _stack_heads(q2 * ATT_SCALE, kh)
        dos = _stack_heads(do2.astype(BF16), kh)
        s = lax.dot_general(qs, kk, NT, preferred_element_type=F32) + bias
        dp = lax.dot_general(dos, vv, NT, preferred_element_type=F32)
        pre.append((qs, dos, s, dp, dsum))
    mid = []
    for (qs, dos, s, dp, dsum), it in zip(pre, items):
        p = jnp.exp(s - _rows_of_heads(it[5]))
        mid.append((p.astype(BF16), (p * (dp - dsum)).astype(BF16)))
    out = []
    for (pb, ds), (qs, dos, _, _, _), it in zip(mid, pre, items):
        dq = _fold_heads(jnp.dot(ds, it[1], preferred_element_type=F32), it[7]) * ATT_SCALE
        dk = lax.dot_general(ds, qs, TN, preferred_element_type=F32)
        dv = lax.dot_general(pb, dos, TN, preferred_element_type=F32)
        out.append((dq, dk, dv))
    return out


def swa_fwd(qkv, sinks, *, name, carried=None):
    bsz, s_len, width = qkv.shape
    ckv = SWA_KV_HEADS * HEAD_DIM
    nb = s_len // ATT_BLOCK
    kblk = D_MODEL // ckv

    def body(sink_ref, q_ref, kp_ref, kc_ref, vp_ref, vc_ref, o_ref, lse_ref, ob_ref, bias2):
        n = pl.program_id(1)
        _fill_bias(bias2)
        bias = bias2[jnp.minimum(n, 1)]
        kk = jnp.concatenate([kp_ref[0], kc_ref[0]], axis=0)
        vv = jnp.concatenate([vp_ref[0], vc_ref[0]], axis=0)
        top = lax.broadcasted_iota(jnp.int32, (2 * ATT_BLOCK, 1), 0) < ATT_BLOCK
        for hp0 in range(0, N_HEADS // 2, PAIRS_AT_ONCE):
            items, places = [], []
            for hp in range(hp0, hp0 + PAIRS_AT_ONCE):
                cols = slice(hp * LANES, (hp + 1) * LANES)
                kb, kh = _kv_place(2 * hp, SWA_KV_HEADS)
                kcols = slice(kb * LANES, (kb + 1) * LANES)
                sink_col = jnp.where(top, sink_ref[2 * hp], sink_ref[2 * hp + 1])
                items.append((q_ref[0, :, cols], kk[:, kcols], vv[:, kcols], bias, (kh, kh), sink_col))
                places.append((cols, (kh, kh)))
            for (pv, m, l), (cols, kh2) in zip(_pairs_fwd(items), places):
                o2 = _fold_heads(pv / l, kh2)
                o_ref[0, :, cols] = o2
                ob_ref[0, :, cols] = o2.astype(BF16)
                lse_ref[0, :, cols] = _fold_heads(m + jnp.log(l), kh2)

    prev = lambda n: jnp.maximum(n - 1, 0)
    out = pl.BlockSpec((1, ATT_BLOCK, D_MODEL), lambda b, n: (b, n, 0))
    sd = lambda dt: jax.ShapeDtypeStruct((bsz, s_len, D_MODEL), dt)
    return call_with_exchange(
        body, carried,
        out_shape=(sd(F32), sd(F32), sd(BF16)),
        grid=(bsz, nb),
        in_specs=[pl.BlockSpec(memory_space=pltpu.SMEM), out,
                  pl.BlockSpec((1, ATT_BLOCK, ckv), lambda b, n: (b, prev(n), kblk)),
                  pl.BlockSpec((1, ATT_BLOCK, ckv), lambda b, n: (b, n, kblk)),
                  pl.BlockSpec((1, ATT_BLOCK, ckv), lambda b, n: (b, prev(n), kblk + 1)),
                  pl.BlockSpec((1, ATT_BLOCK, ckv), lambda b, n: (b, n, kblk + 1))],
        out_specs=(out, out, out),
        scratch_shapes=[pltpu.VMEM((2, 2 * ATT_BLOCK, 2 * ATT_BLOCK), F32)],
        name=name,
        args=(sinks, qkv, qkv, qkv, qkv, qkv))


def swa_bwd(qkv, o, lse, do, *, name, carried=None):
    bsz, s_len, width = qkv.shape
    ckv = SWA_KV_HEADS * HEAD_DIM
    nb = s_len // ATT_BLOCK
    kblk = D_MODEL // ckv

    def body(q_ref, kp_ref, kc_ref, vp_ref, vc_ref, o_ref, lse_ref, do_ref, dq_ref, dk_ref, dv_ref, dkk, dvv, ck, cv,
             bias2):
        n = pl.program_id(1)

        @pl.when(n < nb)
        def _():
            _fill_bias(bias2)
            bias = bias2[jnp.minimum(n, 1)]
            kk = jnp.concatenate([kp_ref[0], kc_ref[0]], axis=0)
            vv = jnp.concatenate([vp_ref[0], vc_ref[0]], axis=0)
            dkk[...] = jnp.zeros_like(dkk)
            dvv[...] = jnp.zeros_like(dvv)
            for hp0 in range(0, N_HEADS // 2, PAIRS_AT_ONCE):
                items, places = [], []
                for hp in range(hp0, hp0 + PAIRS_AT_ONCE):
                    cols = slice(hp * LANES, (hp + 1) * LANES)
                    kb, kh = _kv_place(2 * hp, SWA_KV_HEADS)
                    kcols = slice(kb * LANES, (kb + 1) * LANES)
                    items.append((q_ref[0, :, cols], kk[:, kcols], vv[:, kcols], do_ref[0, :, cols], o_ref[0, :, cols],
                                  lse_ref[0, :, cols], bias, (kh, kh)))
                    places.append((cols, kcols))
                for (dq, dk, dv), (cols, kcols) in zip(_pairs_bwd(items), places):
                    dq_ref[0, :, cols] = dq
                    dkk[:, kcols] += dk
                    dvv[:, kcols] += dv

        @pl.when((n >= 1) & (n < nb))
        def _():
            dk_ref[0] = ck[...] + dkk[pl.ds(0, ATT_BLOCK), :]
            dv_ref[0] = cv[...] + dvv[pl.ds(0, ATT_BLOCK), :]

        @pl.when(n == nb)
        def _():
            dk_ref[0] = ck[...]
            dv_ref[0] = cv[...]

        @pl.when(n < nb)
        def _():
            ck[...] = dkk[pl.ds(ATT_BLOCK, ATT_BLOCK), :]
            cv[...] = dvv[pl.ds(ATT_BLOCK, ATT_BLOCK), :]

    clamp = lambda n: jnp.minimum(n, nb - 1)
    prev = lambda n: jnp.maximum(n - 1, 0)
    row = pl.BlockSpec((1, ATT_BLOCK, D_MODEL), lambda b, n: (b, clamp(n), 0))
    kv_out = pl.BlockSpec((1, ATT_BLOCK, ckv), lambda b, n: (b, prev(n), 0))
    return call_with_exchange(
        body, carried,
        out_shape=(jax.ShapeDtypeStruct((bsz, s_len, D_MODEL), F32), jax.ShapeDtypeStruct((bsz, s_len, ckv), F32),
                   jax.ShapeDtypeStruct((bsz, s_len, ckv), F32)),
        grid=(bsz, nb + 1),
        in_specs=[row,
                  pl.BlockSpec((1, ATT_BLOCK, ckv), lambda b, n: (b, prev(clamp(n)), kblk)),
                  pl.BlockSpec((1, ATT_BLOCK, ckv), lambda b, n: (b, clamp(n), kblk)),
                  pl.BlockSpec((1, ATT_BLOCK, ckv), lambda b, n: (b, prev(clamp(n)), kblk + 1)),
                  pl.BlockSpec((1, ATT_BLOCK, ckv), lambda b, n: (b, clamp(n), kblk + 1)),
                  row, row, row],
        out_specs=(row, kv_out, kv_out),
        scratch_shapes=[pltpu.VMEM((2 * ATT_BLOCK, ckv), F32), pltpu.VMEM((2 * ATT_BLOCK, ckv), F32),
                        pltpu.VMEM((ATT_BLOCK, ckv), F32), pltpu.VMEM((ATT_BLOCK, ckv), F32),
                        pltpu.VMEM((2, 2 * ATT_BLOCK, 2 * ATT_BLOCK), F32)],
        name=name,
        args=(qkv, qkv, qkv, qkv, qkv, o, lse, do))


DIL_PATTERNS = tuple((d, 2048 // d // ATT_BLOCK) for d in DILATIONS)
MHA = (0, 1)


def _dil_rows(idx, d, nb):
    j = idx // nb
    n = idx % nb
    base = j + n * (ATT_BLOCK * d)
    prev = jnp.maximum(base - ATT_BLOCK * d, j)
    if d == 1:
        return n, pl.ds(pl.multiple_of(base, ATT_BLOCK), ATT_BLOCK), pl.ds(pl.multiple_of(prev, ATT_BLOCK), ATT_BLOCK)
    return n, pl.ds(base, ATT_BLOCK, stride=d), pl.ds(prev, ATT_BLOCK, stride=d)


def dil_fwd(qkv, *, name, carried=None):
    bsz, s_len, _ = qkv.shape
    assert s_len == DIL_PATTERNS[0][0] * DIL_PATTERNS[0][1] * ATT_BLOCK
    npair = N_HEADS // 2

    def body(q_ref, k_ref, v_ref, y_ref, lse_ref, yb_ref, m_acc, l_acc, bias2, bias1):
        _fill_bias(bias2, bias1)
        for ci, (d, nb) in enumerate(DIL_PATTERNS):
            single = nb == 1

            def blocks(it, carry):
                items, places = [], []
                for u in range(PAIRS_AT_ONCE):
                    n, rows, prows = _dil_rows(it * PAIRS_AT_ONCE + u, d, nb)
                    kc = k_ref[rows, :].astype(BF16)
                    vc = v_ref[rows, :].astype(BF16)
                    if single:
                        kk, vv, bias = kc, vc, bias1[...]
                    else:
                        kk = jnp.concatenate([k_ref[prows, :].astype(BF16), kc], axis=0)
                        vv = jnp.concatenate([v_ref[prows, :].astype(BF16), vc], axis=0)
                        bias = bias2[jnp.minimum(n, 1)]
                    items.append((q_ref[rows, :].astype(BF16), kk, vv, bias, MHA, None))
                    places.append(rows)
                for (pv, m, l), rows in zip(_pairs_fwd(items), places):
                    o2, m2, l2 = _fold_heads(pv, MHA), _fold_heads(m, MHA), _fold_heads(l, MHA)
                    if ci == 0:
                        y_ref[rows, :] = o2
                        m_acc[rows, :] = m2
                        l_acc[rows, :] = l2
                    else:
                        m_old = m_acc[rows, :]
                        m_new = jnp.maximum(m_old, m2)
                        w_old = jnp.exp(m_old - m_new)
                        w_new = jnp.exp(m2 - m_new)
                        y_ref[rows, :] = y_ref[rows, :] * w_old + o2 * w_new
                        l_acc[rows, :] = l_acc[rows, :] * w_old + l2 * w_new
                        m_acc[rows, :] = m_new
                return carry

            lax.fori_loop(0, d * nb // PAIRS_AT_ONCE, blocks, 0)
        y = y_ref[...] / l_acc[...]
        y_ref[...] = y
        yb_ref[...] = y.astype(BF16)
        lse_ref[...] = m_acc[...] + jnp.log(l_acc[...])

    slab = lambda off: pl.BlockSpec((None, s_len, LANES), functools.partial(lambda o, b, h: (b, 0, o + h), off))
    sd = lambda dt: jax.ShapeDtypeStruct((bsz, s_len, D_MODEL), dt)
    return call_with_exchange(
        body, carried,
        out_shape=(sd(F32), sd(F32), sd(BF16)),
        grid=(bsz, npair),
        in_specs=[slab(0), slab(npair), slab(2 * npair)],
        out_specs=(slab(0), slab(0), slab(0)),
        scratch_shapes=[pltpu.VMEM((s_len, LANES), F32), pltpu.VMEM((s_len, LANES), F32),
                        pltpu.VMEM((2, 2 * ATT_BLOCK, 2 * ATT_BLOCK), F32), pltpu.VMEM((2 * ATT_BLOCK, ATT_BLOCK), F32)],
        name=name,
        args=(qkv, qkv, qkv))


def dil_bwd(qkv, y, lse, dy, *, name, carried=None):
    bsz, s_len, _ = qkv.shape
    npair = N_HEADS // 2

    def body(q_ref, k_ref, v_ref, y_ref, lse_ref, dy_ref, dq_ref, dk_ref, dv_ref, bias2, bias1):
        _fill_bias(bias2, bias1)
        dq_ref[...] = jnp.zeros_like(dq_ref)
        dk_ref[...] = jnp.zeros_like(dk_ref)
        dv_ref[...] = jnp.zeros_like(dv_ref)
        for d, nb in DIL_PATTERNS:
            single = nb == 1

            def blocks(it, carry):
                items, places = [], []
                for u in range(PAIRS_AT_ONCE):
                    n, rows, prows = _dil_rows(it * PAIRS_AT_ONCE + u, d, nb)
                    kc = k_ref[rows, :].astype(BF16)
                    vc = v_ref[rows, :].astype(BF16)
                    if single:
                        kk, vv, bias = kc, vc, bias1[...]
                    else:
                        kk = jnp.concatenate([k_ref[prows, :].astype(BF16), kc], axis=0)
                        vv = jnp.concatenate([v_ref[prows, :].astype(BF16), vc], axis=0)
                        bias = bias2[jnp.minimum(n, 1)]
                    items.append((q_ref[rows, :].astype(BF16), kk, vv, dy_ref[rows, :], y_ref[rows, :], lse_ref[rows, :], bias, MHA))
                    places.append((rows, prows))
                for (dq, dk, dv), (rows, prows) in zip(_pairs_bwd(items), places):
                    dq_ref[rows, :] += dq
                    if single:
                        dk_ref[rows, :] += dk
                        dv_ref[rows, :] += dv
                    else:
                        dk_ref[prows, :] += dk[:ATT_BLOCK]
                        dv_ref[prows, :] += dv[:ATT_BLOCK]
                        dk_ref[rows, :] += dk[ATT_BLOCK:]
                        dv_ref[rows, :] += dv[ATT_BLOCK:]
                return carry

            lax.fori_loop(0, d * nb // PAIRS_AT_ONCE, blocks, 0)

    slab = lambda off: pl.BlockSpec((None, s_len, LANES), functools.partial(lambda o, b, h: (b, 0, o + h), off))
    sd = jax.ShapeDtypeStruct((bsz, s_len, D_MODEL), F32)
    return call_with_exchange(
        body, carried,
        out_shape=(sd, sd, sd),
        grid=(bsz, npair),
        in_specs=[slab(0), slab(npair), slab(2 * npair), slab(0), slab(0), slab(0)],
        out_specs=(slab(0), slab(0), slab(0)),
        scratch_shapes=[pltpu.VMEM((2, 2 * ATT_BLOCK, 2 * ATT_BLOCK), F32), pltpu.VMEM((2 * ATT_BLOCK, ATT_BLOCK), F32)],
        name=name,
        args=(qkv, qkv, qkv, y, lse, dy))


def sink_grad(do, o, lse, sink_lanes, *, name):
    t_dim, d = do.shape
    tr = _pick(t_dim, (256, 128, 8))

    def body(do_ref, o_ref, l_ref, s_ref, out_ref):
        @pl.when(pl.program_id(0) == 0)
        def _():
            out_ref[...] = jnp.zeros_like(out_ref)

        out_ref[...] += jnp.sum(-jnp.exp(s_ref[...] - l_ref[...]) * do_ref[...] * o_ref[...], axis=0, keepdims=True)

    row = pl.BlockSpec((tr, d), lambda i: (i, 0))
    vec = pl.BlockSpec((1, d), lambda i: (0, 0))
    return pl.pallas_call(
        body,
        out_shape=jax.ShapeDtypeStruct((1, d), F32),
        grid=(t_dim // tr,),
        in_specs=[row, row, row, vec],
        out_specs=vec,
        compiler_params=_params(("arbitrary",)),
        name=name,
    )(do, o, lse, sink_lanes)


def adamw(w, g, m, v, *, name):
    rows, cols = w.shape
    tr = _pick(rows, (256, 128, 64, 32, 16, 8))

    def body(w_ref, g_ref, m_ref, v_ref, d_ref, nm_ref, nv_ref):
        gv = g_ref[...]
        nm = ADAM_B1 * m_ref[...] + (1.0 - ADAM_B1) * gv
        nv = ADAM_B2 * v_ref[...] + (1.0 - ADAM_B2) * (gv * gv)
        m_hat = nm / (1.0 - ADAM_B1 ** ADAM_STEP)
        v_hat = nv / (1.0 - ADAM_B2 ** ADAM_STEP)
        d_ref[...] = -ADAM_LR * (m_hat / (jnp.sqrt(v_hat) + ADAM_EPS) + ADAM_WD * w_ref[...])
        nm_ref[...] = nm
        nv_ref[...] = nv

    row = pl.BlockSpec((tr, cols), lambda i: (i, 0))
    return pl.pallas_call(
        body,
        out_shape=(jax.ShapeDtypeStruct((rows, cols), F32),) * 3,
        grid=(rows // tr,),
        in_specs=[row] * 4,
        out_specs=(row, row, row),
        compiler_params=_params(("parallel",)),
        name=name,
    )(w, g, m, v)


def _place():
    return lax.axis_index("x"), lax.axis_index("y"), lax.axis_index("c")


def _gather_copies(x_ref, out_ref, send_sems, recv_sems):
    x, y, c = _place()
    me, sibling = (x, y, c), (x, y, 1 - c)
    chips = [(1 - x, y), (x, 1 - y), (1 - x, 1 - y)]

    def slot(px, py, pc):
        return out_ref.at[4 * px + 2 * py + pc]

    def copy(k, block, to, src=None):
        return pltpu.make_async_remote_copy(
            src_ref=slot(*block) if src is None else src, dst_ref=slot(*block),
            send_sem=send_sems.at[k], recv_sem=recv_sems.at[k], device_id=to, device_id_type=MESH)

    first = [lambda: copy(0, me, sibling, src=x_ref)] + [functools.partial(copy, 1 + j, me, (*chip, c), src=x_ref)
                                                         for j, chip in enumerate(chips)]
    passed = [functools.partial(copy, 4 + j, (*chip, c), sibling) for j, chip in enumerate(chips)]
    landing = [functools.partial(copy, 1 + j, (*chip, c), me) for j, chip in enumerate(chips)]
    from_sibling = [lambda: copy(0, sibling, me)] + [functools.partial(copy, 4 + j, (*chip, 1 - c), me) for j, chip in enumerate(chips)]
    return slot(*me), first, passed, landing, from_sibling


def _gather_start(x_ref, out_ref, send_sems, recv_sems, local_sem):
    mine, first, _, _, _ = _gather_copies(x_ref, out_ref, send_sems, recv_sems)
    pltpu.make_async_copy(x_ref, mine, local_sem).start()
    for cp in first:
        cp().start()


def _gather_finish(x_ref, out_ref, send_sems, recv_sems, local_sem):
    mine, first, passed, landing, from_sibling = _gather_copies(x_ref, out_ref, send_sems, recv_sems)
    for cp, fwd in zip(landing, passed):
        cp().wait_recv()
        fwd().start()
    for cp in from_sibling:
        cp().wait_recv()
    for cp in first + passed:
        cp().wait_send()
    pltpu.make_async_copy(x_ref, mine, local_sem).wait()


def _a2a_copies(x_ref, out_ref, send_sems, recv_sems):
    x, y, c = _place()
    me = 4 * x + 2 * y + c
    copies = []
    for k in range(1, N_DEV):
        px = 1 - x if k & 4 else x
        py = 1 - y if k & 2 else y
        pc = 1 - c if k & 1 else c
        copies.append(pltpu.make_async_remote_copy(
            src_ref=x_ref.at[4 * px + 2 * py + pc], dst_ref=out_ref.at[me], send_sem=send_sems.at[k - 1],
            recv_sem=recv_sems.at[k - 1], device_id=(px, py, pc), device_id_type=MESH))
    return me, copies


def _a2a_start(x_ref, out_ref, send_sems, recv_sems, local_sem):
    me, copies = _a2a_copies(x_ref, out_ref, send_sems, recv_sems)
    pltpu.make_async_copy(x_ref.at[me], out_ref.at[me], local_sem).start()
    for cp in copies:
        cp.start()


def _a2a_finish(x_ref, out_ref, send_sems, recv_sems, local_sem):
    me, copies = _a2a_copies(x_ref, out_ref, send_sems, recv_sems)
    for cp in copies:
        cp.wait_recv()
    for cp in copies:
        cp.wait_send()
    pltpu.make_async_copy(x_ref.at[me], out_ref.at[me], local_sem).wait()


EXCHANGES = {"gather": (_gather_start, _gather_finish, lambda x: (N_DEV,) + x.shape),
             "a2a": (_a2a_start, _a2a_finish, lambda x: x.shape)}
EXCHANGE_SEMS = [pltpu.SemaphoreType.DMA((7,)), pltpu.SemaphoreType.DMA((7,)), pltpu.SemaphoreType.DMA(())]


def exchange(kind, x, *, name):
    start, finish, shape = EXCHANGES[kind]

    def body(x_ref, out_ref, *sems):
        start(x_ref, out_ref, *sems)
        finish(x_ref, out_ref, *sems)

    return pl.pallas_call(
        body,
        out_shape=jax.ShapeDtypeStruct(shape(x), x.dtype),
        in_specs=[pl.BlockSpec(memory_space=pl.ANY)],
        out_specs=pl.BlockSpec(memory_space=pl.ANY),
        scratch_shapes=EXCHANGE_SEMS,
        name=name,
    )(x)


def call_with_exchange(body, carried, *, out_shape, grid, in_specs, out_specs, scratch_shapes, name, args):
    sem = ("arbitrary",) * len(grid)
    carried = list(carried or ())
    if not carried:
        res = pl.pallas_call(body, out_shape=out_shape, grid=grid, in_specs=in_specs, out_specs=out_specs,
                             scratch_shapes=scratch_shapes, compiler_params=_params(sem), name=name)(*args)
        return res, []
    n_in, n_out, n_scr, n_x = len(in_specs), len(out_shape), len(scratch_shapes), len(carried)
    n_sems = len(EXCHANGE_SEMS)

    def wrapped(*refs):
        ins, x_refs = refs[:n_in], refs[n_in:n_in + n_x]
        outs = refs[n_in + n_x:n_in + n_x + n_out]
        out_refs = refs[n_in + n_x + n_out:n_in + 2 * n_x + n_out]
        rest = refs[n_in + 2 * n_x + n_out:]
        scratch, sems = rest[:n_scr], rest[n_scr:]
        ids = [pl.program_id(i) for i in range(len(grid))]
        is_first = functools.reduce(lambda a, b: a & b, [i == 0 for i in ids])
        is_last = functools.reduce(lambda a, b: a & b, [i == g - 1 for i, g in zip(ids, grid)])

        @pl.when(is_first)
        def _():
            for e, (kind, _) in enumerate(carried):
                EXCHANGES[kind][0](x_refs[e], out_refs[e], *sems[e * n_sems:(e + 1) * n_sems])

        body(*ins, *outs, *scratch)

        @pl.when(is_last)
        def _():
            for e, (kind, _) in enumerate(carried):
                EXCHANGES[kind][1](x_refs[e], out_refs[e], *sems[e * n_sems:(e + 1) * n_sems])

    any_spec = pl.BlockSpec(memory_space=pl.ANY)
    res = pl.pallas_call(
        wrapped,
        out_shape=tuple(out_shape) + tuple(jax.ShapeDtypeStruct(EXCHANGES[kind][2](x), x.dtype) for kind, x in carried),
        grid=grid,
        in_specs=list(in_specs) + [any_spec] * n_x,
        out_specs=tuple(out_specs) + (any_spec,) * n_x,
        scratch_shapes=list(scratch_shapes) + EXCHANGE_SEMS * n_x,
        compiler_params=_params(sem),
        name=name + "".join("_" + kind for kind, _ in carried),
    )(*args, *[x for _, x in carried])
    return res[:n_out], list(res[n_out:])


def sum_slots(x, *, name):
    _, rows, cols = x.shape
    tr = _pick(rows, (512, 256, 128, 64, 32, 16))

    def body(x_ref, o_ref):
        acc = x_ref[0].astype(F32)
        for k in range(1, N_DEV):
            acc = acc + x_ref[k].astype(F32)
        o_ref[...] = acc

    return pl.pallas_call(
        body,
        out_shape=jax.ShapeDtypeStruct((rows, cols), F32),
        grid=(rows // tr,),
        in_specs=[pl.BlockSpec((N_DEV, tr, cols), lambda i: (0, i, 0))],
        out_specs=pl.BlockSpec((tr, cols), lambda i: (i, 0)),
        compiler_params=_params(("parallel",)),
        name=name,
    )(x)


BIG = ("w_in", "w_branch", "w_out", "w_ffn_in", "w_ffn_out")
SMALL = ("conv_b", "w_rg", "b_rg", "w_ig", "b_ig", "lru_lambda", "sinks", "ln1_g", "ln1_b", "ln2_g", "ln2_b")
N_LRU_BLOCKS = D_MODEL // HEAD_DIM
SMALL_ROWS_TILE = 512


def _block_diag(w):
    z = jnp.zeros((N_LRU_BLOCKS // 2, HEAD_DIM, HEAD_DIM), w.dtype)
    top = jnp.concatenate([w[0::2], z], axis=2)
    bot = jnp.concatenate([z, w[1::2]], axis=2)
    return jnp.concatenate([top, bot], axis=1)


def _block_diag_grad(g):
    return jnp.stack([g[:, :HEAD_DIM, :HEAD_DIM], g[:, HEAD_DIM:, HEAD_DIM:]], axis=1).reshape(N_LRU_BLOCKS, HEAD_DIM, HEAD_DIM)


def layer_fwd(x, xb, p, bsz, own_late=None, next_w_in=None):
    t_dim = x.shape[0]
    s_len = t_dim // bsz
    w_f, w_qs, w_qd = p["w_in_f"], p["w_in_qs"], p["w_in_qd"]
    proj_f = matmul(xb, w_f, name="proj_f")
    qs = matmul(xb, w_qs, out_dtype=BF16, name="proj_qs").reshape(bsz, s_len, W_QS)
    qd = matmul(xb, w_qd, name="proj_qd").reshape(bsz, s_len, W_QD)
    proj_f3 = proj_f.reshape(bsz, s_len, W_F)
    wr_bd, wi_bd = _block_diag(p["w_rg"]), _block_diag(p["w_ig"])
    (y_a, h), got_in = lru_fwd(proj_f3, p["conv_w"], p["conv_b"], wr_bd, wi_bd, p["b_rg"], p["b_ig"], p["lru_lambda"],
                               name="lru_fwd", carried=[("gather", next_w_in)] if next_w_in is not None else [])
    (y_b, lse_b, y_bb), _ = swa_fwd(qs, p["sinks"], name="swa_fwd")
    (y_c, lse_c, y_cb), got_late = dil_fwd(qd, name="dil_fwd", carried=[("gather", t) for t in own_late or ()])
    if own_late is not None:
        p = {**p, **_late_weights(*got_late)}
    ys =[t.reshape(t_dim, D_MODEL) for t in (y_a, y_bb, y_cb)]
    br = [matmul(ys[n], p["w_branch"][n], name="branch") for n in range(3)]
    merged = merge_fwd(proj_f, br, name="merge_fwd")
    mix = matmul(merged, p["w_out"], name="w_out")
    x1, x1b, z1 = ln_fwd(x, mix, p["ln1_g"], p["ln1_b"], name="ln_fwd")
    h13 = matmul(x1b, p["w_ffn_in"], name="ffn_in")
    act = swiglu_fwd(h13, name="swiglu_fwd")
    ffn = matmul(act, p["w_ffn_out"], name="ffn_out")
    x2, x2b, z2 = ln_fwd(x1, ffn, p["ln2_g"], p["ln2_b"], name="ln_fwd")
    saved = dict(xb=xb, proj_f=proj_f, qs=qs, qd=qd, h=h, ys=ys, y_b=y_b, y_c=y_c, lse_b=lse_b, lse_c=lse_c, br=br, merged=merged,
                 z1=z1, x1b=x1b, h13=h13, act=act, z2=z2, wr_bd=wr_bd, wi_bd=wi_bd, p=p)
    return x2, x2b, saved, (got_in[0] if got_in else None)


def layer_bwd(dx2, s, bsz, exchange_own=False, above_w_in=None):
    p = s["p"]
    t_dim = dx2.shape[0]
    s_len = t_dim // bsz
    g = {}
    dz2, dz2b, g["ln2_g"], g["ln2_b"] = ln_bwd(dx2, s["z2"], p["ln2_g"], name="ln_bwd")
    dact = matmul(dz2b, p["w_ffn_out"], trans_b=True, name="d_act")
    dh13 = swiglu_bwd(dact, s["h13"], name="swiglu_bwd")
    g["w_ffn_out"] = matmul(s["act"], dz2b, trans_a=True, out_dtype=BF16, name="dw_ffn_out")
    g["w_ffn_in"] = matmul(s["x1b"], dh13, trans_a=True, out_dtype=BF16, name="dw_ffn_in")
    dx1 = matmul(dh13, p["w_ffn_in"], trans_b=True, add=dz2, add_scale=ALPHA, name="dx_ffn")
    dz1, dz1b, g["ln1_g"], g["ln1_b"] = ln_bwd(dx1, s["z1"], p["ln1_g"], name="ln_bwd")
    dmerged = matmul(dz1b, p["w_out"], trans_b=True, name="d_merged")
    g["w_out"] = matmul(s["merged"], dz1b, trans_a=True, out_dtype=BF16, name="dw_out")
    *dbr, dgates = merge_bwd(dmerged, s["proj_f"], s["br"], name="merge_bwd")
    dys = [matmul(dbr[n], p["w_branch"][n], trans_b=True, name="d_branch") for n in range(3)]
    g["w_branch"] = jnp.stack([matmul(s["ys"][n], dbr[n], trans_a=True, out_dtype=BF16, name="dw_branch") for n in range(3)])
    fi_slots, rows_slots = _late_slots(g) if exchange_own else (None, None)
    shape3 = (bsz, s_len, D_MODEL)
    (dlx, dlg, g["conv_w"], g["conv_b"], g["b_rg"], g["b_ig"], g["lru_lambda"], dwr, dwi), got_rows = lru_bwd(
        dys[0].reshape(shape3), s["proj_f"].reshape(bsz, s_len, W_F), s["h"], p["conv_w"], p["conv_b"], s["wr_bd"], s["wi_bd"],
        jnp.swapaxes(s["wr_bd"], 1, 2), jnp.swapaxes(s["wi_bd"], 1, 2), p["b_rg"], p["b_ig"], p["lru_lambda"], name="lru_bwd",
        carried=[("a2a", rows_slots)] if exchange_own else [])
    g["w_rg"], g["w_ig"] = _block_diag_grad(dwr), _block_diag_grad(dwi)
    dy_b3 = dys[1].reshape(shape3)
    dqs, got_fi = swa_bwd(s["qs"], s["y_b"], s["lse_b"], dy_b3, name="swa_bwd", carried=[("a2a", fi_slots)] if exchange_own else [])
    sink_lanes = jnp.repeat(p["sinks"], HEAD_DIM).reshape(1, D_MODEL)
    g["sinks"] = sink_grad(dys[1], s["y_b"].reshape(t_dim, D_MODEL), s["lse_b"].reshape(t_dim, D_MODEL), sink_lanes,
                           name="sink_grad").reshape(N_HEADS, HEAD_DIM).sum(axis=1)
    dqd, got_in = dil_bwd(s["qd"], s["y_c"], s["lse_c"], dys[2].reshape(shape3), name="dil_bwd",
                          carried=[("a2a", above_w_in)] if above_w_in is not None else [])
    flat = lambda t: t.reshape(t_dim, t.shape[-1])
    dproj_f = jnp.concatenate([flat(dlx), flat(dlg), dgates], axis=1)
    dproj_qs = jnp.concatenate([flat(t) for t in dqs], axis=1).astype(BF16)
    dproj_qd = jnp.concatenate([flat(t) for t in dqd], axis=1).astype(BF16)
    g["w_in_f"] = matmul(s["xb"], dproj_f, trans_a=True, out_dtype=BF16, name="dw_in_f")
    g["w_in_qs"] = matmul(s["xb"], dproj_qs, trans_a=True, out_dtype=BF16, name="dw_in_qs")
    g["w_in_qd"] = matmul(s["xb"], dproj_qd, trans_a=True, out_dtype=BF16, name="dw_in_qd")
    dx = matmul(dproj_f, p["w_in_f"], trans_b=True, add=dz1, add_scale=ALPHA, name="dx_f")
    dx = matmul(dproj_qs, p["w_in_qs"], trans_b=True, add=dx, name="dx_qs")
    dx = matmul(dproj_qd, p["w_in_qd"], trans_b=True, add=dx, name="dx_qd")
    g = {k: (v.reshape(p[k].shape) if k in p else v) for k, v in g.items()}
    return dx, g, dict(late=(got_fi[0], got_rows[0]) if exchange_own else None, w_in=got_in[0] if got_in else None)


def local_step(x, target, layer_params, layer_shards=None, first_w_in=None):
    bsz, s_len, d = x.shape
    t_dim = bsz * s_len
    xf = x.reshape(t_dim, d)
    xb = xf.astype(BF16)
    exchanging = layer_shards is not None
    saved, gathered = [], first_w_in
    for l in range(DEPTH):
        p = layer_params(l, gathered)
        xf, xb, s, gathered = layer_fwd(xf, xb, p, bsz, own_late=layer_shards[l][1:] if exchanging else None,
                                        next_w_in=layer_shards[l + 1][0] if exchanging and l + 1 < DEPTH else None)
        saved.append(s)
    dy, sq = loss_head(xf, target.reshape(t_dim, d), name="loss_head")
    loss = 0.5 * jnp.sum(sq) / d
    grads, received, w_in_slots = [None] * DEPTH, [[None] * 3 for _ in range(DEPTH)], None
    for l in reversed(range(DEPTH)):
        dy, grads[l], got = layer_bwd(dy, saved[l], bsz, exchange_own=exchanging, above_w_in=w_in_slots)
        if got["w_in"] is not None:
            received[l + 1][0] = got["w_in"]
        if exchanging:
            received[l][1:] = got["late"]
            w_in_slots = _w_in_slots(grads[l])
    return loss, dy.reshape(bsz, s_len, d), grads, received, w_in_slots


W_IN_SEGMENTS = (("w_in_f", 0, 0, 2 * D_MODEL), ("w_in_qs", 0, 2 * D_MODEL, W_QS), ("w_in_qd", 0, 2 * D_MODEL + W_QS, W_QD),
                 ("w_in_f", 2 * D_MODEL, 2 * D_MODEL + W_QS + W_QD, 3 * D_MODEL))
ROW_SHARDED = ("w_branch", "w_out", "w_ffn_out")


def _cols_of_shards(shards, lo, hi):
    width = shards[0].shape[-1]
    parts = []
    for k, sh in enumerate(shards):
        a, b = max(lo, k * width), min(hi, (k + 1) * width)
        if a < b:
            parts.append(sh[..., a - k * width:b - k * width])
    return parts[0] if len(parts) == 1 else jnp.concatenate(parts, axis=-1)


def _cols_of_w_in(pieces, lo, hi):
    parts = []
    for name, p0, l0, width in W_IN_SEGMENTS:
        a, b = max(lo, l0), min(hi, l0 + width)
        if a < b:
            parts.append(pieces[name][..., p0 + a - l0:p0 + b - l0])
    return parts[0] if len(parts) == 1 else jnp.concatenate(parts, axis=-1)


W_IN_COLS = W_F + W_QS + W_QD


def _layer_shards(w, l):
    rows = jnp.concatenate([w[k][l].reshape(-1, D_MODEL) for k in ROW_SHARDED]).astype(BF16)
    return w["w_in"][l].astype(BF16), w["w_ffn_in"][l].astype(BF16), rows


ROW_COUNTS = (3 * D_MODEL // N_DEV, D_MODEL // N_DEV, FF_HIDDEN // N_DEV)


def _w_in_weights(g_in):
    sh = [g_in[k] for k in range(N_DEV)]
    return dict(w_in_f=jnp.concatenate([_cols_of_shards(sh, 0, 2 * D_MODEL), _cols_of_shards(sh, W_IN_COLS - 3 * D_MODEL, W_IN_COLS)], axis=-1),
                w_in_qs=_cols_of_shards(sh, 2 * D_MODEL, 2 * D_MODEL + W_QS),
                w_in_qd=_cols_of_shards(sh, 2 * D_MODEL + W_QS, 2 * D_MODEL + W_QS + W_QD))


def _late_weights(g_fi, g_rows):
    p = dict(w_ffn_in=jnp.concatenate([g_fi[k] for k in range(N_DEV)], axis=-1))
    off = 0
    for k, n in zip(ROW_SHARDED, ROW_COUNTS):
        t = g_rows[:, off:off + n]
        if k == "w_branch":
            p[k] = jnp.transpose(t.reshape(N_DEV, 3, n // 3, D_MODEL), (1, 0, 2, 3)).reshape(3, -1, D_MODEL)
        else:
            p[k] = t.reshape(-1, D_MODEL)
        off += n
    return p


def _w_in_slots(g):
    shard = W_IN_COLS // N_DEV
    return jnp.stack([_cols_of_w_in(g, k * shard, (k + 1) * shard) for k in range(N_DEV)]).astype(BF16)


def _late_slots(g):
    shard = g["w_ffn_in"].shape[-1] // N_DEV
    s_fi = jnp.stack([g["w_ffn_in"][:, k * shard:(k + 1) * shard] for k in range(N_DEV)]).astype(BF16)
    rows = jnp.concatenate([jnp.transpose(g["w_branch"].reshape(3, N_DEV, -1, D_MODEL), (1, 0, 2, 3)).reshape(N_DEV, -1, D_MODEL),
                            g["w_out"].reshape(N_DEV, -1, D_MODEL), g["w_ffn_out"].reshape(N_DEV, -1, D_MODEL)], axis=1).astype(BF16)
    return s_fi, rows


def _pad_rows(flat, tile_rows):
    n = flat.shape[0]
    per = tile_rows * LANES
    total = -(-n // per) * per
    return jnp.pad(flat, (0, total - n)).reshape(-1, LANES)


def kernel(x, w_in, conv_w, conv_b, w_rg, b_rg, w_ig, b_ig, lru_lambda, sinks, w_branch, w_out, ln1_g, ln1_b, w_ffn_in, w_ffn_out, ln2_g, ln2_b, loss_target, m_w_in, m_conv_w, m_conv_b, m_w_rg, m_b_rg, m_w_ig, m_b_ig, m_lru_lambda, m_sinks, m_w_branch, m_w_out, m_ln1_g, m_ln1_b, m_w_ffn_in, m_w_ffn_out, m_ln2_g, m_ln2_b, v_w_in, v_conv_w, v_conv_b, v_w_rg, v_b_rg, v_w_ig, v_b_ig, v_lru_lambda, v_sinks, v_w_branch, v_w_out, v_ln1_g, v_ln1_b, v_w_ffn_in, v_w_ffn_out, v_ln2_g, v_ln2_b):
    w = dict(w_in=w_in, conv_w=conv_w, conv_b=conv_b, w_rg=w_rg, b_rg=b_rg, w_ig=w_ig, b_ig=b_ig, lru_lambda=lru_lambda, sinks=sinks,
             w_branch=w_branch, w_out=w_out, ln1_g=ln1_g, ln1_b=ln1_b, w_ffn_in=w_ffn_in, w_ffn_out=w_ffn_out, ln2_g=ln2_g, ln2_b=ln2_b)
    m = dict(w_in=m_w_in, conv_w=m_conv_w, conv_b=m_conv_b, w_rg=m_w_rg, b_rg=m_b_rg, w_ig=m_w_ig, b_ig=m_b_ig, lru_lambda=m_lru_lambda,
             sinks=m_sinks, w_branch=m_w_branch, w_out=m_w_out, ln1_g=m_ln1_g, ln1_b=m_ln1_b, w_ffn_in=m_w_ffn_in, w_ffn_out=m_w_ffn_out,
             ln2_g=m_ln2_g, ln2_b=m_ln2_b)
    v = dict(w_in=v_w_in, conv_w=v_conv_w, conv_b=v_conv_b, w_rg=v_w_rg, b_rg=v_b_rg, w_ig=v_w_ig, b_ig=v_b_ig, lru_lambda=v_lru_lambda,
             sinks=v_sinks, w_branch=v_w_branch, w_out=v_w_out, ln1_g=v_ln1_g, ln1_b=v_ln1_b, w_ffn_in=v_w_ffn_in, w_ffn_out=v_w_ffn_out,
             ln2_g=v_ln2_g, ln2_b=v_ln2_b)
    order = ["w_in", "conv_w", "conv_b", "w_rg", "b_rg", "w_ig", "b_ig", "lru_lambda", "sinks", "w_branch", "w_out", "ln1_g", "ln1_b",
             "w_ffn_in", "w_ffn_out", "ln2_g", "ln2_b"]
    me = 4 * lax.axis_index("x") + 2 * lax.axis_index("y") + lax.axis_index("c")

    names = ("w_in", "w_ffn_in", "w_rows")
    shards = [_layer_shards(w, l) for l in range(DEPTH)]
    first_w_in = exchange("gather", shards[0][0], name="gather_w_in")
    cw = exchange("gather", conv_w.reshape(-1, LANES), name="gather_conv_w")
    conv_w_full = jnp.moveaxis(cw.reshape(N_DEV, DEPTH, CONV_WIDTH, LANES), 0, 2).reshape(DEPTH, CONV_WIDTH, D_MODEL)

    def layer_params(l, gathered_w_in):
        return {**_w_in_weights(gathered_w_in), **{k: w[k][l] for k in SMALL}, "conv_w": conv_w_full[l]}

    loss_local, grad_x, grads, received, w_in_slots = local_step(x, loss_target, layer_params, shards, first_w_in)
    loss = lax.psum(loss_local, ("x", "y", "c"))
    received[0][0] = exchange("a2a", w_in_slots, name="exchange_g_w_in")

    sums = [[sum_slots(t, name=f"sum_g_{n}") for t, n in zip(received[l], names)] for l in range(DEPTH)]
    g_final = {"w_in": jnp.stack([sums[l][0] for l in range(DEPTH)]), "w_ffn_in": jnp.stack([sums[l][1] for l in range(DEPTH)])}
    off = 0
    for k, n in zip(ROW_SHARDED, ROW_COUNTS):
        g_final[k] = jnp.stack([sums[l][2][off:off + n] for l in range(DEPTH)]).reshape(w[k].shape)
        off += n
    grads = {k: jnp.stack([grads[l][k] for l in range(DEPTH)]) for k in list(SMALL) + ["conv_w"]}

    small_names = list(SMALL) + ["conv_w"]
    small_sizes = [grads[k].size for k in small_names]
    svec = _pad_rows(jnp.concatenate([grads[k].reshape(-1) for k in small_names]), SMALL_ROWS_TILE)
    ssum = sum_slots(exchange("gather", svec, name="gather_small_grads"), name="sum_small_grads")
    sflat, off = ssum.reshape(-1), 0
    for k, n in zip(small_names, small_sizes):
        g_final[k] = sflat[off:off + n].reshape(grads[k].shape)
        off += n
    g_final["conv_w"] = lax.dynamic_slice_in_dim(g_final["conv_w"], me * LANES, LANES, axis=2)

    delta, new_m, new_v = {}, {}, {}
    for k in list(BIG) + ["conv_w"]:
        cols = w[k].shape[-1]
        two_d = lambda t: t.reshape(-1, cols)
        d_, m_, v_ = adamw(two_d(w[k]), two_d(g_final[k]), two_d(m[k]), two_d(v[k]), name=f"adamw_{k}")
        delta[k], new_m[k], new_v[k] = d_.reshape(w[k].shape), m_.reshape(w[k].shape), v_.reshape(w[k].shape)
    pack_small = lambda dct: _pad_rows(jnp.concatenate([dct[k].reshape(-1) for k in SMALL]), SMALL_ROWS_TILE)
    d_, m_, v_ = adamw(pack_small(w), pack_small(g_final), pack_small(m), pack_small(v), name="adamw_small")
    off = 0
    for k in SMALL:
        n = w[k].size
        for dst, src in ((delta, d_), (new_m, m_), (new_v, v_)):
            dst[k] = src.reshape(-1)[off:off + n].reshape(w[k].shape)
        off += n
    return (loss, grad_x, *[g_final[k] for k in order], *[delta[k] for k in order], *[new_m[k] for k in order], *[new_v[k] for k in order])
```

```python
import functools
import math

import jax
import jax.numpy as jnp
from jax import lax
from jax.experimental import pallas as pl
from jax.experimental.pallas import tpu as pltpu

F32 = jnp.float32
BF16 = jnp.bfloat16

N_DEV = 8
DEPTH = 4
D_MODEL = 1024
HEAD_DIM = 64
LANES = 128
N_HEADS = D_MODEL // HEAD_DIM
SWA_KV_HEADS = 4
ATT_BLOCK = 128
DILATIONS = (1, 4, 16)
CONV_WIDTH = 4
LRU_C = 8.0
FF_HIDDEN = 2816
ALPHA = (2.0 * DEPTH) ** 0.25
LN_EPS = 1e-5
NEG_INF = -1e30
W_F = 5 * D_MODEL
W_QS = D_MODEL + 2 * SWA_KV_HEADS * HEAD_DIM
W_QD = 3 * D_MODEL

ADAM_LR = 0.001
ADAM_B1 = 0.9
ADAM_B2 = 0.999
ADAM_EPS = 1e-08
ADAM_WD = 0.01
ADAM_STEP = 10

VMEM_LIMIT = 56 * 1024 * 1024
MATMUL_BLOCK_BYTES = 40 * 1024 * 1024
MESH = pl.DeviceIdType.MESH


def _pick(n, cands):
    for c in cands:
        if n % c == 0:
            return c
    raise ValueError(f"no tile for {n} among {cands}")


def _params(sem):
    return pltpu.CompilerParams(dimension_semantics=sem, vmem_limit_bytes=VMEM_LIMIT)


def _tile(n, cap):
    best = None
    for t in range(LANES, cap + 1, LANES):
        if n % t == 0:
            best = t
    assert best is not None, (n, cap)
    return best


def matmul(a, b, *, name, trans_a=False, trans_b=False, out_dtype=F32, add=None, add_scale=1.0):
    if trans_a:
        k_dim, m_dim = a.shape
    else:
        m_dim, k_dim = a.shape
    n_dim = b.shape[0] if trans_b else b.shape[1]
    assert (b.shape[1] if trans_b else b.shape[0]) == k_dim
    tn = _tile(n_dim, 1408)
    tm, tk = _tile(m_dim, 1024), _tile(k_dim, 1408)
    for cand in (1024, 512, 256):
        ctm = _tile(m_dim, cand)
        blocks = 2 * (ctm * k_dim * a.dtype.itemsize + tn * k_dim * b.dtype.itemsize + ctm * tn * jnp.dtype(out_dtype).itemsize
                      + (ctm * tn * add.dtype.itemsize if add is not None else 0))
        if blocks <= MATMUL_BLOCK_BYTES:
            tm, tk = ctm, k_dim
            break
    nk = k_dim // tk
    dims = (((0 if trans_a else 1,), (1 if trans_b else 0,)), ((), ()))

    def body(*refs):
        if add is None:
            a_ref, b_ref, o_ref, acc_ref = refs
            add_ref = None
        else:
            a_ref, b_ref, add_ref, o_ref, acc_ref = refs
        k = pl.program_id(2)
        part = lax.dot_general(a_ref[...].astype(BF16), b_ref[...].astype(BF16), dims, preferred_element_type=F32)

        def finish(r):
            if add_ref is not None:
                r = r + add_scale * add_ref[...].astype(F32)
            o_ref[...] = r.astype(out_dtype)

        if nk == 1:
            finish(part)
        else:
            @pl.when(k == 0)
            def _():
                acc_ref[...] = part

            @pl.when((k > 0) & (k < nk - 1))
            def _():
                acc_ref[...] += part

            @pl.when(k == nk - 1)
            def _():
                finish(acc_ref[...] + part)

    a_spec = pl.BlockSpec((tk, tm), lambda i, j, k: (k, i)) if trans_a else pl.BlockSpec((tm, tk), lambda i, j, k: (i, k))
    b_spec = pl.BlockSpec((tn, tk), lambda i, j, k: (j, k)) if trans_b else pl.BlockSpec((tk, tn), lambda i, j, k: (k, j))
    in_specs = [a_spec, b_spec]
    args = [a, b]
    if add is not None:
        in_specs.append(pl.BlockSpec((tm, tn), lambda i, j, k: (i, j)))
        args.append(add)
    return pl.pallas_call(
        body,
        out_shape=jax.ShapeDtypeStruct((m_dim, n_dim), out_dtype),
        grid=(m_dim // tm, n_dim // tn, nk),
        in_specs=in_specs,
        out_specs=pl.BlockSpec((tm, tn), lambda i, j, k: (i, j)),
        scratch_shapes=[pltpu.VMEM((tm, tn) if nk > 1 else (8, LANES), F32)],
        compiler_params=_params(("parallel", "parallel", "arbitrary")),
        name=name,
    )(*args)


def ln_fwd(x, r, g, b, *, name):
    t_dim, d = x.shape
    tr = _pick(t_dim, (256, 128, 8))

    def body(x_ref, r_ref, g_ref, b_ref, y_ref, yb_ref, z_ref):
        z = ALPHA * x_ref[...] + r_ref[...]
        mu = jnp.mean(z, axis=-1, keepdims=True)
        zc = z - mu
        var = jnp.mean(zc * zc, axis=-1, keepdims=True)
        y = zc * lax.rsqrt(var + LN_EPS) * g_ref[...] + b_ref[...]
        y_ref[...] = y
        yb_ref[...] = y.astype(BF16)
        z_ref[...] = z

    row = pl.BlockSpec((tr, d), lambda i: (i, 0))
    vec = pl.BlockSpec((1, d), lambda i: (0, 0))
    return pl.pallas_call(
        body,
        out_shape=(jax.ShapeDtypeStruct((t_dim, d), F32), jax.ShapeDtypeStruct((t_dim, d), BF16), jax.ShapeDtypeStruct((t_dim, d), F32)),
        grid=(t_dim // tr,),
        in_specs=[row, row, vec, vec],
        out_specs=(row, row, row),
        compiler_params=_params(("parallel",)),
        name=name,
    )(x, r, g.reshape(1, d), b.reshape(1, d))


def ln_bwd(dy, z, g, *, name):
    t_dim, d = dy.shape
    tr = _pick(t_dim, (256, 128, 8))

    def body(dy_ref, z_ref, g_ref, dz_ref, dzb_ref, dg_ref, db_ref):
        @pl.when(pl.program_id(0) == 0)
        def _():
            dg_ref[...] = jnp.zeros_like(dg_ref)
            db_ref[...] = jnp.zeros_like(db_ref)

        z = z_ref[...]
        dyv = dy_ref[...]
        mu = jnp.mean(z, axis=-1, keepdims=True)
        zc = z - mu
        var = jnp.mean(zc * zc, axis=-1, keepdims=True)
        rstd = lax.rsqrt(var + LN_EPS)
        xhat = zc * rstd
        dxhat = dyv * g_ref[...]
        m1 = jnp.mean(dxhat, axis=-1, keepdims=True)
        m2 = jnp.mean(dxhat * xhat, axis=-1, keepdims=True)
        dz = rstd * (dxhat - m1 - xhat * m2)
        dz_ref[...] = dz
        dzb_ref[...] = dz.astype(BF16)
        dg_ref[...] += jnp.sum(dyv * xhat, axis=0, keepdims=True)
        db_ref[...] += jnp.sum(dyv, axis=0, keepdims=True)

    row = pl.BlockSpec((tr, d), lambda i: (i, 0))
    vec = pl.BlockSpec((1, d), lambda i: (0, 0))
    return pl.pallas_call(
        body,
        out_shape=(jax.ShapeDtypeStruct((t_dim, d), F32), jax.ShapeDtypeStruct((t_dim, d), BF16),
                   jax.ShapeDtypeStruct((1, d), F32), jax.ShapeDtypeStruct((1, d), F32)),
        grid=(t_dim // tr,),
        in_specs=[row, row, vec],
        out_specs=(row, row, vec, vec),
        compiler_params=_params(("arbitrary",)),
        name=name,
    )(dy, z, g.reshape(1, d))


def loss_head(y, target, *, name):
    t_dim, d = y.shape
    tr = _pick(t_dim, (256, 128, 8))

    def body(y_ref, t_ref, dy_ref, sq_ref):
        @pl.when(pl.program_id(0) == 0)
        def _():
            sq_ref[...] = jnp.zeros_like(sq_ref)

        diff = y_ref[...] - t_ref[...]
        dy_ref[...] = diff / d
        sq_ref[...] += jnp.sum(diff * diff, axis=0, keepdims=True)

    row = pl.BlockSpec((tr, d), lambda i: (i, 0))
    vec = pl.BlockSpec((1, d), lambda i: (0, 0))
    return pl.pallas_call(
        body,
        out_shape=(jax.ShapeDtypeStruct((t_dim, d), F32), jax.ShapeDtypeStruct((1, d), F32)),
        grid=(t_dim // tr,),
        in_specs=[row, row],
        out_specs=(row, vec),
        compiler_params=_params(("arbitrary",)),
        name=name,
    )(y, target)


def _sigmoid(x):
    return 0.5 * jnp.tanh(0.5 * x) + 0.5


def swiglu_fwd(h13, *, name):
    t_dim = h13.shape[0]
    f = h13.shape[1] // 2
    tr = _pick(t_dim, (256, 128, 8))

    def body(h1_ref, h3_ref, act_ref):
        h1 = h1_ref[...].astype(F32)
        act_ref[...] = (h1 * _sigmoid(h1) * h3_ref[...].astype(F32)).astype(BF16)

    return pl.pallas_call(
        body,
        out_shape=jax.ShapeDtypeStruct((t_dim, f), BF16),
        grid=(t_dim // tr,),
        in_specs=[pl.BlockSpec((tr, f), lambda i: (i, 0)), pl.BlockSpec((tr, f), lambda i: (i, 1))],
        out_specs=pl.BlockSpec((tr, f), lambda i: (i, 0)),
        compiler_params=_params(("parallel",)),
        name=name,
    )(h13, h13)


def swiglu_bwd(dact, h13, *, name):
    t_dim = h13.shape[0]
    f = h13.shape[1] // 2
    tr = _pick(t_dim, (256, 128, 8))

    def body(da_ref, h1_ref, h3_ref, dh_ref):
        h1 = h1_ref[...].astype(F32)
        da = da_ref[...].astype(F32)
        sg = _sigmoid(h1)
        dh_ref[:, :f] = (da * h3_ref[...].astype(F32) * sg * (1.0 + h1 * (1.0 - sg))).astype(BF16)
        dh_ref[:, f:] = (da * h1 * sg).astype(BF16)

    return pl.pallas_call(
        body,
        out_shape=jax.ShapeDtypeStruct((t_dim, 2 * f), BF16),
        grid=(t_dim // tr,),
        in_specs=[pl.BlockSpec((tr, f), lambda i: (i, 0)), pl.BlockSpec((tr, f), lambda i: (i, 0)),
                  pl.BlockSpec((tr, f), lambda i: (i, 1))],
        out_specs=pl.BlockSpec((tr, 2 * f), lambda i: (i, 0)),
        compiler_params=_params(("parallel",)),
        name=name,
    )(dact, h13, h13)


def merge_fwd(proj_f, br, *, name):
    t_dim, d = br[0].shape
    tr = _pick(t_dim, (256, 128, 8))

    def body(g0, g1, g2, b0, b1, b2, o_ref):
        o_ref[...] = (_sigmoid(g0[...]) * b0[...].astype(F32) + _sigmoid(g1[...]) * b1[...].astype(F32)
                      + _sigmoid(g2[...]) * b2[...].astype(F32)).astype(BF16)

    row = pl.BlockSpec((tr, d), lambda i: (i, 0))
    gate = [pl.BlockSpec((tr, d), functools.partial(lambda n, i: (i, 2 + n), n)) for n in range(3)]
    return pl.pallas_call(
        body,
        out_shape=jax.ShapeDtypeStruct((t_dim, d), BF16),
        grid=(t_dim // tr,),
        in_specs=gate + [row, row, row],
        out_specs=row,
        compiler_params=_params(("parallel",)),
        name=name,
    )(proj_f, proj_f, proj_f, *br)


def merge_bwd(dmerged, proj_f, br, *, name):
    t_dim, d = dmerged.shape
    tr = _pick(t_dim, (256, 128, 8))

    def body(dm_ref, g0, g1, g2, b0, b1, b2, d0, d1, d2, dg_ref):
        dm = dm_ref[...]
        for n, (g, b, o) in enumerate(((g0, b0, d0), (g1, b1, d1), (g2, b2, d2))):
            sg = _sigmoid(g[...])
            o[...] = (dm * sg).astype(BF16)
            dg_ref[:, n * d:(n + 1) * d] = (dm * b[...].astype(F32) * sg * (1.0 - sg)).astype(BF16)

    row = pl.BlockSpec((tr, d), lambda i: (i, 0))
    gate = [pl.BlockSpec((tr, d), functools.partial(lambda n, i: (i, 2 + n), n)) for n in range(3)]
    return pl.pallas_call(
        body,
        out_shape=(jax.ShapeDtypeStruct((t_dim, d), BF16),) * 3 + (jax.ShapeDtypeStruct((t_dim, 3 * d), BF16),),
        grid=(t_dim // tr,),
        in_specs=[row] + gate + [row, row, row],
        out_specs=(row, row, row, pl.BlockSpec((tr, 3 * d), lambda i: (i, 0))),
        compiler_params=_params(("parallel",)),
        name=name,
    )(dmerged, proj_f, proj_f, proj_f, *br)


GELU_C = math.sqrt(2.0 / math.pi)
PAD = 8
SCAN_TILES = 8


def _gelu(x):
    return 0.5 * x * (1.0 + jnp.tanh(GELU_C * (x + 0.044715 * x * x * x)))


def _gelu_grad(x):
    t = jnp.tanh(GELU_C * (x + 0.044715 * x * x * x))
    return 0.5 * (1.0 + t) + 0.5 * x * (1.0 - t * t) * GELU_C * (1.0 + 3.0 * 0.044715 * x * x)


def _neg_expm1(x, exp_x):
    series = -x * (1.0 + x * (0.5 + x * (1.0 / 6.0)))
    return jnp.where(x > -0.02, series, 1.0 - exp_x)


def _lru_gates(xv, cw_ref, cb_ref, wr_ref, wi_ref, br_ref, bi_ref, lam_ref, pad_ref, s_len):
    pad_ref[pl.ds(0, PAD), :] = jnp.zeros((PAD, LANES), F32)
    pad_ref[pl.ds(PAD, s_len), :] = xv
    xc = cb_ref[...] + jnp.zeros((s_len, LANES), F32)
    for j in range(CONV_WIDTH):
        xc = xc + pad_ref[pl.ds(PAD - (CONV_WIDTH - 1) + j, s_len), :] * cw_ref[pl.ds(j, 1), :]
    xcb = xc.astype(BF16)
    r = _sigmoid(jnp.dot(xcb, wr_ref[0].astype(BF16), preferred_element_type=F32) + br_ref[...])
    i = _sigmoid(jnp.dot(xcb, wi_ref[0].astype(BF16), preferred_element_type=F32) + bi_ref[...])
    nl = -lam_ref[...]
    sp = jnp.maximum(nl, 0.0) + jnp.log(1.0 + jnp.exp(-jnp.abs(nl)))
    log_a = -LRU_C * r * sp
    a = jnp.exp(log_a)
    mult = jnp.sqrt(_neg_expm1(2.0 * log_a, a * a))
    return xc, r, i, sp, a, mult


def _tile_scan(a, b, row, reverse):
    for s in (1, 2, 4):
        if reverse:
            a_sh = pltpu.roll(a, 8 - s, 0)
            b_sh = pltpu.roll(b, 8 - s, 0)
            m = row + s <= 7
        else:
            a_sh = pltpu.roll(a, s, 0)
            b_sh = pltpu.roll(b, s, 0)
            m = row >= s
        b = jnp.where(m, a * b_sh + b, b)
        a = jnp.where(m, a * a_sh, a)
    return a, b


def lru_fwd(proj_f, conv_w, conv_b, wr_bd, wi_bd, b_rg, b_ig, lam, *, name, carried=None):
    bsz, s_len, _ = proj_f.shape
    d = D_MODEL
    ncb = d // LANES
    n_tiles = s_len // 8

    def body(x_ref, g_ref, cw_ref, cb_ref, wr_ref, wi_ref, br_ref, bi_ref, lam_ref, y_ref, h_ref, pad_ref, a_s, b_s):
        xc, r, i, sp, a, mult = _lru_gates(x_ref[0], cw_ref, cb_ref, wr_ref, wi_ref, br_ref, bi_ref, lam_ref, pad_ref, s_len)
        a_s[...] = a
        b_s[...] = mult * (i * xc)
        row = lax.broadcasted_iota(jnp.int32, (8, LANES), 0)

        def tiles(t, carry):
            starts = [pl.multiple_of((t * SCAN_TILES + u) * 8, 8) for u in range(SCAN_TILES)]
            local = [_tile_scan(a_s[pl.ds(i0, 8), :], b_s[pl.ds(i0, 8), :], row, False) for i0 in starts]
            for i0, (ac, hl) in zip(starts, local):
                h = hl + ac * carry
                h_ref[0, pl.ds(i0, 8), :] = h
                carry = jnp.broadcast_to(h[7:8, :], (8, LANES))
            return carry

        lax.fori_loop(0, n_tiles // SCAN_TILES, tiles, jnp.zeros((8, LANES), F32))
        y_ref[0] = (h_ref[0] * _gelu(g_ref[0])).astype(BF16)

    slab = lambda off: pl.BlockSpec((1, s_len, LANES), functools.partial(lambda o, c, b: (b, 0, o + c), off))
    vec = pl.BlockSpec((1, LANES), lambda c, b: (0, c))
    mat = pl.BlockSpec((1, LANES, LANES), lambda c, b: (c, 0, 0))
    out = pl.BlockSpec((1, s_len, LANES), lambda c, b: (b, 0, c))
    return call_with_exchange(
        body, carried,
        out_shape=(jax.ShapeDtypeStruct((bsz, s_len, d), BF16), jax.ShapeDtypeStruct((bsz, s_len, d), F32)),
        grid=(ncb, bsz),
        in_specs=[slab(0), slab(ncb), pl.BlockSpec((CONV_WIDTH, LANES), lambda c, b: (0, c)), vec, mat, mat, vec, vec, vec],
        out_specs=(out, out),
        scratch_shapes=[pltpu.VMEM((s_len + 2 * PAD, LANES), F32), pltpu.VMEM((s_len, LANES), F32), pltpu.VMEM((s_len, LANES), F32)],
        name=name,
        args=(proj_f, proj_f, conv_w, conv_b.reshape(1, d), wr_bd, wi_bd, b_rg.reshape(1, d), b_ig.reshape(1, d), lam.reshape(1, d)))


def lru_bwd(dy, proj_f, h, conv_w, conv_b, wr_bd, wi_bd, wr_bd_t, wi_bd_t, b_rg, b_ig, lam, *, name, carried=None):
    bsz, s_len, _ = proj_f.shape
    d = D_MODEL
    ncb = d // LANES
    n_tiles = s_len // 8

    def body(dy_ref, x_ref, g_ref, h_ref, cw_ref, cb_ref, wr_ref, wi_ref, wrt_ref, wit_ref, br_ref, bi_ref, lam_ref,
             dx_ref, dg_ref, dcw_ref, dcb_ref, dbr_ref, dbi_ref, dlam_ref, dwr_ref, dwi_ref, pad_ref, a_s, b_s, l_s):
        @pl.when(pl.program_id(1) == 0)
        def _():
            for ref in (dcw_ref, dcb_ref, dbr_ref, dbi_ref, dlam_ref, dwr_ref, dwi_ref):
                ref[...] = jnp.zeros_like(ref)

        xc, r, i, sp, a, mult = _lru_gates(x_ref[0], cw_ref, cb_ref, wr_ref, wi_ref, br_ref, bi_ref, lam_ref, pad_ref, s_len)
        gate = g_ref[0]
        hv = h_ref[0]
        dyv = dy_ref[0].astype(F32)
        dg_ref[0] = (dyv * hv * _gelu_grad(gate)).astype(BF16)
        b_s[...] = dyv * _gelu(gate)
        l_s[pl.ds(0, s_len), :] = a
        l_s[pl.ds(s_len, PAD), :] = jnp.zeros((PAD, LANES), F32)
        a_s[...] = l_s[pl.ds(1, s_len), :]
        row = lax.broadcasted_iota(jnp.int32, (8, LANES), 0)

        def tiles(t, carry):
            starts = [pl.multiple_of((n_tiles - 1 - (t * SCAN_TILES + u)) * 8, 8) for u in range(SCAN_TILES)]
            local = [_tile_scan(a_s[pl.ds(i0, 8), :], b_s[pl.ds(i0, 8), :], row, True) for i0 in starts]
            for i0, (ac, ll) in zip(starts, local):
                lmb = ll + ac * carry
                b_s[pl.ds(i0, 8), :] = lmb
                carry = jnp.broadcast_to(lmb[0:1, :], (8, LANES))
            return carry

        lax.fori_loop(0, n_tiles // SCAN_TILES, tiles, jnp.zeros((8, LANES), F32))
        lmb = b_s[...]
        l_s[pl.ds(0, PAD), :] = jnp.zeros((PAD, LANES), F32)
        l_s[pl.ds(PAD, s_len), :] = hv
        h_prev = l_s[pl.ds(PAD - 1, s_len), :]
        da = lmb * h_prev
        dmult = lmb * (i * xc)
        di = lmb * mult * xc
        dxc = lmb * mult * i
        dlog_a = da * a - dmult * a * a / mult
        dr = -LRU_C * sp * dlog_a
        dsp = jnp.sum(-LRU_C * r * dlog_a, axis=0, keepdims=True)
        dlam_ref[...] += dsp * (-_sigmoid(-lam_ref[...]))
        dpr = dr * r * (1.0 - r)
        dpi = di * i * (1.0 - i)
        dprb = dpr.astype(BF16)
        dpib = dpi.astype(BF16)
        xcb = xc.astype(BF16)
        dbr_ref[...] += jnp.sum(dpr, axis=0, keepdims=True)
        dbi_ref[...] += jnp.sum(dpi, axis=0, keepdims=True)
        tn = (((0,), (0,)), ((), ()))
        dwr_ref[0] += lax.dot_general(xcb, dprb, tn, preferred_element_type=F32)
        dwi_ref[0] += lax.dot_general(xcb, dpib, tn, preferred_element_type=F32)
        dxc = (dxc + jnp.dot(dprb, wrt_ref[0].astype(BF16), preferred_element_type=F32)
               + jnp.dot(dpib, wit_ref[0].astype(BF16), preferred_element_type=F32))
        dcb_ref[...] += jnp.sum(dxc, axis=0, keepdims=True)
        for j in range(CONV_WIDTH):
            dcw_ref[pl.ds(j, 1), :] += jnp.sum(dxc * pad_ref[pl.ds(PAD - (CONV_WIDTH - 1) + j, s_len), :], axis=0, keepdims=True)
        l_s[pl.ds(0, s_len), :] = dxc
        l_s[pl.ds(s_len, PAD), :] = jnp.zeros((PAD, LANES), F32)
        dx = jnp.zeros((s_len, LANES), F32)
        for j in range(CONV_WIDTH):
            dx = dx + l_s[pl.ds(CONV_WIDTH - 1 - j, s_len), :] * cw_ref[pl.ds(j, 1), :]
        dx_ref[0] = dx.astype(BF16)

    slab = lambda off: pl.BlockSpec((1, s_len, LANES), functools.partial(lambda o, c, b: (b, 0, o + c), off))
    vec = pl.BlockSpec((1, LANES), lambda c, b: (0, c))
    mat = pl.BlockSpec((1, LANES, LANES), lambda c, b: (c, 0, 0))
    cw = pl.BlockSpec((CONV_WIDTH, LANES), lambda c, b: (0, c))
    out = pl.BlockSpec((1, s_len, LANES), lambda c, b: (b, 0, c))
    vshape = jax.ShapeDtypeStruct((1, d), F32)
    mshape = jax.ShapeDtypeStruct((ncb, LANES, LANES), F32)
    return call_with_exchange(
        body, carried,
        out_shape=(jax.ShapeDtypeStruct((bsz, s_len, d), BF16),) * 2
        + (jax.ShapeDtypeStruct((CONV_WIDTH, d), F32), vshape, vshape, vshape, vshape, mshape, mshape),
        grid=(ncb, bsz),
        in_specs=[out, slab(0), slab(ncb), out, cw, vec, mat, mat, mat, mat, vec, vec, vec],
        out_specs=(out, out, cw, vec, vec, vec, vec, mat, mat),
        scratch_shapes=[pltpu.VMEM((s_len + 2 * PAD, LANES), F32), pltpu.VMEM((s_len, LANES), F32), pltpu.VMEM((s_len, LANES), F32),
                        pltpu.VMEM((s_len + 2 * PAD, LANES), F32)],
        name=name,
        args=(dy, proj_f, proj_f, h, conv_w, conv_b.reshape(1, d), wr_bd, wi_bd, wr_bd_t, wi_bd_t,
              b_rg.reshape(1, d), b_ig.reshape(1, d), lam.reshape(1, d)))


def _kv_place(head, n_kv_heads):
    kv = head // (N_HEADS // n_kv_heads)
    return kv // 2, kv % 2


def _band_mask(n, single):
    if single:
        qi = lax.broadcasted_iota(jnp.int32, (ATT_BLOCK, ATT_BLOCK), 0)
        return qi >= lax.broadcasted_iota(jnp.int32, (ATT_BLOCK, ATT_BLOCK), 1)
    qi = lax.broadcasted_iota(jnp.int32, (ATT_BLOCK, 2 * ATT_BLOCK), 0)
    kj = lax.broadcasted_iota(jnp.int32, (ATT_BLOCK, 2 * ATT_BLOCK), 1)
    rel = qi + ATT_BLOCK - kj
    return (rel >= 0) & (rel <= ATT_BLOCK) & ((n > 0) | (kj >= ATT_BLOCK))


def _half_masks(dtype):
    lane = lax.broadcasted_iota(jnp.int32, (1, LANES), 1)
    return [(lane < HEAD_DIM).astype(dtype), (lane >= HEAD_DIM).astype(dtype)]


NT = (((1,), (1,)), ((), ()))
TN = (((0,), (0,)), ((), ()))


def _qkv_specs(dil, q_blk, k_blk, v_blk, ckv, clamp):
    qw = D_MODEL // LANES * LANES
    return [
        pl.BlockSpec((1, ATT_BLOCK, qw), lambda b, j, n: (b, clamp(n), j * (q_blk[1]) + q_blk[0])),
        pl.BlockSpec((1, ATT_BLOCK, ckv), lambda b, j, n: (b, jnp.maximum(clamp(n) - 1, 0), j * k_blk[1] + k_blk[0])),
        pl.BlockSpec((1, ATT_BLOCK, ckv), lambda b, j, n: (b, clamp(n), j * k_blk[1] + k_blk[0])),
        pl.BlockSpec((1, ATT_BLOCK, ckv), lambda b, j, n: (b, jnp.maximum(clamp(n) - 1, 0), j * v_blk[1] + v_blk[0])),
        pl.BlockSpec((1, ATT_BLOCK, ckv), lambda b, j, n: (b, clamp(n), j * v_blk[1] + v_blk[0])),
    ]


def attn_fwd(qkv, *, dil, n_kv_heads, sinks, name, emit_bf16=False):
    bsz, s_len, width = qkv.shape
    ckv = n_kv_heads * HEAD_DIM
    l_sub = s_len // dil
    nb = l_sub // ATT_BLOCK
    view = qkv.reshape(bsz, l_sub, dil * width)
    scale = HEAD_DIM ** -0.5
    q_blk = (0, width // D_MODEL)
    k_blk = (D_MODEL // ckv, width // ckv)
    v_blk = (D_MODEL // ckv + 1, width // ckv)
    assert (dil == 1 or width % D_MODEL == 0) and width % ckv == 0 and D_MODEL % ckv == 0

    single = nb == 1

    def body(*refs):
        refs = list(refs)
        sink_ref = refs.pop(0) if sinks is not None else None
        ob_ref = refs.pop() if emit_bf16 else None
        q_ref, kp_ref, kc_ref, vp_ref, vc_ref, o_ref, lse_ref = refs
        n = pl.program_id(2)
        mask = _band_mask(n, single)
        hm = _half_masks(BF16)
        hmf = _half_masks(F32)
        kk = kc_ref[0] if single else jnp.concatenate([kp_ref[0], kc_ref[0]], axis=0)
        vv = vc_ref[0] if single else jnp.concatenate([vp_ref[0], vc_ref[0]], axis=0)
        for hp in range(N_HEADS // 2):
            q2 = q_ref[0, :, hp * LANES:(hp + 1) * LANES]
            o2 = jnp.zeros((ATT_BLOCK, LANES), F32)
            l2 = jnp.zeros((ATT_BLOCK, LANES), F32)
            for a in range(2):
                kb, kh = _kv_place(2 * hp + a, n_kv_heads)
                k2 = kk[:, kb * LANES:(kb + 1) * LANES]
                v2 = vv[:, kb * LANES:(kb + 1) * LANES]
                if kh != a:
                    k2 = pltpu.roll(k2, HEAD_DIM, 1)
                    v2 = pltpu.roll(v2, HEAD_DIM, 1)
                s = lax.dot_general(q2 * hm[a], k2, NT, preferred_element_type=F32) * scale
                s = jnp.where(mask, s, NEG_INF)
                m = jnp.max(s, axis=-1, keepdims=True)
                if sink_ref is not None:
                    sk = sink_ref[2 * hp + a]
                    m = jnp.maximum(m, sk)
                p = jnp.exp(s - m)
                den = jnp.sum(p, axis=-1, keepdims=True)
                if sink_ref is not None:
                    den = den + jnp.exp(sk - m)
                o2 = o2 + jnp.dot(p.astype(BF16), v2 * hm[a], preferred_element_type=F32) / den
                l2 = l2 + (m + jnp.log(den)) * hmf[a]
            o_ref[0, :, hp * LANES:(hp + 1) * LANES] = o2
            lse_ref[0, :, hp * LANES:(hp + 1) * LANES] = l2
            if ob_ref is not None:
                ob_ref[0, :, hp * LANES:(hp + 1) * LANES] = o2.astype(BF16)

    in_specs = _qkv_specs(dil, q_blk, k_blk, v_blk, ckv, lambda n: n)
    args = [view] * 5
    if sinks is not None:
        in_specs = [pl.BlockSpec(memory_space=pltpu.SMEM)] + in_specs
        args = [sinks] + args
    out = pl.BlockSpec((1, ATT_BLOCK, D_MODEL), lambda b, j, n: (b, n, j))
    res = pl.pallas_call(
        body,
        out_shape=(jax.ShapeDtypeStruct((bsz, l_sub, dil * D_MODEL), F32),) * 2
        + ((jax.ShapeDtypeStruct((bsz, l_sub, dil * D_MODEL), BF16),) if emit_bf16 else ()),
        grid=(bsz, dil, nb),
        in_specs=in_specs,
        out_specs=(out,) * (3 if emit_bf16 else 2),
        compiler_params=_params(("parallel", "parallel", "arbitrary")),
        name=name,
    )(*args)
    return tuple(t.reshape(bsz, s_len, D_MODEL) for t in res)


def attn_bwd(qkv, o, lse, do, acc, *, dil, n_kv_heads, name):
    bsz, s_len, width = qkv.shape
    ckv = n_kv_heads * HEAD_DIM
    l_sub = s_len // dil
    nb = l_sub // ATT_BLOCK
    view = qkv.reshape(bsz, l_sub, dil * width)
    scale = HEAD_DIM ** -0.5
    q_blk = (0, width // D_MODEL)
    k_blk = (D_MODEL // ckv, width // ckv)
    v_blk = (D_MODEL // ckv + 1, width // ckv)
    single = nb == 1

    def body(*refs):
        if acc is None:
            q_ref, kp_ref, kc_ref, vp_ref, vc_ref, o_ref, lse_ref, do_ref, dq_ref, dk_ref, dv_ref, dkk, dvv, ck, cv = refs
            aq_ref = ak_ref = av_ref = None
        else:
            (q_ref, kp_ref, kc_ref, vp_ref, vc_ref, o_ref, lse_ref, do_ref, aq_ref, ak_ref, av_ref,
             dq_ref, dk_ref, dv_ref, dkk, dvv, ck, cv) = refs
        n = pl.program_id(2)

        @pl.when(n < nb)
        def _():
            mask = _band_mask(n, single)
            hm = _half_masks(BF16)
            hmf = _half_masks(F32)
            kk = kc_ref[0] if single else jnp.concatenate([kp_ref[0], kc_ref[0]], axis=0)
            vv = vc_ref[0] if single else jnp.concatenate([vp_ref[0], vc_ref[0]], axis=0)
            krows = pl.ds(ATT_BLOCK, ATT_BLOCK) if single else pl.ds(0, 2 * ATT_BLOCK)
            dkk[...] = jnp.zeros_like(dkk)
            dvv[...] = jnp.zeros_like(dvv)
            for hp in range(N_HEADS // 2):
                cols = slice(hp * LANES, (hp + 1) * LANES)
                q2 = q_ref[0, :, cols]
                do2f = do_ref[0, :, cols]
                do2 = do2f.astype(BF16)
                dd2 = do2f * o_ref[0, :, cols]
                l2 = lse_ref[0, :, cols]
                dq2 = jnp.zeros((ATT_BLOCK, LANES), F32)
                for a in range(2):
                    kb, kh = _kv_place(2 * hp + a, n_kv_heads)
                    kcols = slice(kb * LANES, (kb + 1) * LANES)
                    k2 = kk[:, kcols]
                    v2 = vv[:, kcols]
                    if kh != a:
                        k2 = pltpu.roll(k2, HEAD_DIM, 1)
                        v2 = pltpu.roll(v2, HEAD_DIM, 1)
                    qm = q2 * hm[a]
                    dom = do2 * hm[a]
                    dsum = jnp.sum(dd2 * hmf[a], axis=-1, keepdims=True)
                    lse_h = jnp.max(jnp.where(hmf[a] > 0.5, l2, NEG_INF), axis=-1, keepdims=True)
                    s = lax.dot_general(qm, k2, NT, preferred_element_type=F32) * scale
                    s = jnp.where(mask, s, NEG_INF)
                    p = jnp.exp(s - lse_h)
                    dp = lax.dot_general(dom, v2, NT, preferred_element_type=F32)
                    ds = (p * (dp - dsum) * scale).astype(BF16)
                    dq2 = dq2 + jnp.dot(ds, k2 * hm[a], preferred_element_type=F32)
                    dk_c = lax.dot_general(ds, qm, TN, preferred_element_type=F32)
                    dv_c = lax.dot_general(p.astype(BF16), dom, TN, preferred_element_type=F32)
                    if kh != a:
                        dk_c = pltpu.roll(dk_c, HEAD_DIM, 1)
                        dv_c = pltpu.roll(dv_c, HEAD_DIM, 1)
                    dkk[krows, kcols] += dk_c
                    dvv[krows, kcols] += dv_c
                if aq_ref is not None:
                    dq2 = dq2 + aq_ref[0, :, cols]
                dq_ref[0, :, cols] = dq2

        @pl.when((n >= 1) & (n < nb))
        def _():
            dk_ref[0] = ck[...] + dkk[pl.ds(0, ATT_BLOCK), :] + (0.0 if ak_ref is None else ak_ref[0])
            dv_ref[0] = cv[...] + dvv[pl.ds(0, ATT_BLOCK), :] + (0.0 if av_ref is None else av_ref[0])

        @pl.when(n == nb)
        def _():
            dk_ref[0] = ck[...] + (0.0 if ak_ref is None else ak_ref[0])
            dv_ref[0] = cv[...] + (0.0 if av_ref is None else av_ref[0])

        @pl.when(n < nb)
        def _():
            ck[...] = dkk[pl.ds(ATT_BLOCK, ATT_BLOCK), :]
            cv[...] = dvv[pl.ds(ATT_BLOCK, ATT_BLOCK), :]

    clamp = lambda n: jnp.minimum(n, nb - 1)
    prev = lambda n: jnp.maximum(n - 1, 0)
    row = pl.BlockSpec((1, ATT_BLOCK, D_MODEL), lambda b, j, n: (b, clamp(n), j))
    kv_out = pl.BlockSpec((1, ATT_BLOCK, ckv), lambda b, j, n: (b, prev(n), j))
    in_specs = _qkv_specs(dil, q_blk, k_blk, v_blk, ckv, clamp) + [row, row, row]
    rs = lambda t: t.reshape(bsz, l_sub, dil * t.shape[-1])
    args = [view] * 5 + [rs(o), rs(lse), rs(do)]
    if acc is not None:
        in_specs += [row, kv_out, kv_out]
        args += [rs(t) for t in acc]
    dq, dk, dv = pl.pallas_call(
        body,
        out_shape=(jax.ShapeDtypeStruct((bsz, l_sub, dil * D_MODEL), F32),
                   jax.ShapeDtypeStruct((bsz, l_sub, dil * ckv), F32), jax.ShapeDtypeStruct((bsz, l_sub, dil * ckv), F32)),
        grid=(bsz, dil, nb + 1),
        in_specs=in_specs,
        out_specs=(row, kv_out, kv_out),
        scratch_shapes=[pltpu.VMEM((2 * ATT_BLOCK, ckv), F32), pltpu.VMEM((2 * ATT_BLOCK, ckv), F32),
                        pltpu.VMEM((ATT_BLOCK, ckv), F32), pltpu.VMEM((ATT_BLOCK, ckv), F32)],
        compiler_params=_params(("parallel", "parallel", "arbitrary")),
        name=name,
    )(*args)
    return dq.reshape(bsz, s_len, D_MODEL), dk.reshape(bsz, s_len, ckv), dv.reshape(bsz, s_len, ckv)


def dil_combine(os_, lses, *, name):
    t_dim, d = os_[0].shape
    tr = _pick(t_dim, (256, 128, 8))

    def body(o0, o1, o2, l0, l1, l2, y_ref, lt_ref, yb_ref):
        la, lb, lc = l0[...], l1[...], l2[...]
        m = jnp.maximum(jnp.maximum(la, lb), lc)
        ea, eb, ec = jnp.exp(la - m), jnp.exp(lb - m), jnp.exp(lc - m)
        tot = ea + eb + ec
        y = (ea / tot) * o0[...] + (eb / tot) * o1[...] + (ec / tot) * o2[...]
        y_ref[...] = y
        yb_ref[...] = y.astype(BF16)
        lt_ref[...] = m + jnp.log(tot)

    row = pl.BlockSpec((tr, d), lambda i: (i, 0))
    return pl.pallas_call(
        body,
        out_shape=(jax.ShapeDtypeStruct((t_dim, d), F32),) * 2 + (jax.ShapeDtypeStruct((t_dim, d), BF16),),
        grid=(t_dim // tr,),
        in_specs=[row] * 6,
        out_specs=(row, row, row),
        compiler_params=_params(("parallel",)),
        name=name,
    )(*os_, *lses)


ATT_SCALE = HEAD_DIM ** -0.5


def _band_mask(n, single):
    nk = ATT_BLOCK if single else 2 * ATT_BLOCK
    qi = lax.broadcasted_iota(jnp.int32, (2 * ATT_BLOCK, nk), 0) % ATT_BLOCK
    kj = lax.broadcasted_iota(jnp.int32, (2 * ATT_BLOCK, nk), 1)
    if single:
        return qi >= kj
    rel = qi + ATT_BLOCK - kj
    return (rel >= 0) & (rel <= ATT_BLOCK) & ((n > 0) | (kj >= ATT_BLOCK))


def _lane_halves():
    lane = lax.broadcasted_iota(jnp.int32, (1, LANES), 1)
    return lane < HEAD_DIM


def _stack_heads(t2, kh):
    first = _lane_halves()
    parts = []
    for a in range(2):
        ta = jnp.where(first if a == 0 else ~first, t2, jnp.zeros_like(t2))
        if a != kh[a]:
            ta = pltpu.roll(ta, HEAD_DIM, 1)
        parts.append(ta)
    return jnp.concatenate(parts, axis=0)


def _fold_heads(t, kh):
    t0, t1 = t[:ATT_BLOCK], t[ATT_BLOCK:]
    if t.shape[1] == LANES:
        if kh[0] != 0:
            t0 = pltpu.roll(t0, HEAD_DIM, 1)
        if kh[1] != 1:
            t1 = pltpu.roll(t1, HEAD_DIM, 1)
    return jnp.where(_lane_halves(), t0, t1)


def _rows_of_heads(t2):
    return jnp.concatenate([t2[:, 0:1], t2[:, HEAD_DIM:HEAD_DIM + 1]], axis=0)


PAIRS_AT_ONCE = 4


def _fill_bias(bias2_ref, bias1_ref=None):
    for i in range(2):
        bias2_ref[i] = jnp.where(_band_mask(i, False), 0.0, NEG_INF)
    if bias1_ref is not None:
        bias1_ref[...] = jnp.where(_band_mask(0, True), 0.0, NEG_INF)


def _pairs_fwd(items):
    ss = [lax.dot_general(_stack_heads(q2 * ATT_SCALE, kh), kk, NT, preferred_element_type=F32) + bias
          for q2, kk, _, bias, kh, _ in items]
    ps, ms, ls = [], [], []
    for s, (_, _, _, _, _, sink_col) in zip(ss, items):
        m = jnp.max(s, axis=-1, keepdims=True)
        if sink_col is not None:
            m = jnp.maximum(m, sink_col)
        p = jnp.exp(s - m)
        l = jnp.sum(p, axis=-1, keepdims=True)
        if sink_col is not None:
            l = l + jnp.exp(sink_col - m)
        ps.append(p.astype(BF16))
        ms.append(m)
        ls.append(l)
    pvs = [jnp.dot(p, it[2], preferred_element_type=F32) for p, it in zip(ps, items)]
    return list(zip(pvs, ms, ls))


def _pairs_bwd(items):
    first = _lane_halves()
    pre = []
    for q2, kk, vv, do2, o2, lse2, bias, kh in items:
        dd = do2 * o2
        dsum = jnp.concatenate([jnp.sum(jnp.where(first, dd, 0.0), axis=-1, keepdims=True),
                                jnp.sum(jnp.where(first, 0.0, dd), axis=-1, keepdims=True)], axis=0)
        qs = _stack_heads(q2 * ATT_SCALE, kh)
        dos = _stack_heads(do2.astype(BF16), kh)
        s = lax.dot_general(qs, kk, NT, preferred_element_type=F32) + bias
        dp = lax.dot_general(dos, vv, NT, preferred_element_type=F32)
        pre.append((qs, dos, s, dp, dsum))
    mid = []
    for (qs, dos, s, dp, dsum), it in zip(pre, items):
        p = jnp.exp(s - _rows_of_heads(it[5]))
        mid.append((p.astype(BF16), (p * (dp - dsum)).astype(BF16)))
    out = []
    for (pb, ds), (qs, dos, _, _, _), it in zip(mid, pre, items):
        dq = _fold_heads(jnp.dot(ds, it[1], preferred_element_type=F32), it[7]) * ATT_SCALE
        dk = lax.dot_general(ds, qs, TN, preferred_element_type=F32)
        dv = lax.dot_general(pb, dos, TN, preferred_element_type=F32)
        out.append((dq, dk, dv))
    return out


def swa_fwd(qkv, sinks, *, name, carried=None):
    bsz, s_len, width = qkv.shape
    ckv = SWA_KV_HEADS * HEAD_DIM
    nb = s_len // ATT_BLOCK
    kblk = D_MODEL // ckv

    def body(sink_ref, q_ref, kp_ref, kc_ref, vp_ref, vc_ref, o_ref, lse_ref, ob_ref, bias2):
        n = pl.program_id(1)
        _fill_bias(bias2)
        bias = bias2[jnp.minimum(n, 1)]
        kk = jnp.concatenate([kp_ref[0], kc_ref[0]], axis=0)
        vv = jnp.concatenate([vp_ref[0], vc_ref[0]], axis=0)
        top = lax.broadcasted_iota(jnp.int32, (2 * ATT_BLOCK, 1), 0) < ATT_BLOCK
        for hp0 in range(0, N_HEADS // 2, PAIRS_AT_ONCE):
            items, places = [], []
            for hp in range(hp0, hp0 + PAIRS_AT_ONCE):
                cols = slice(hp * LANES, (hp + 1) * LANES)
                kb, kh = _kv_place(2 * hp, SWA_KV_HEADS)
                kcols = slice(kb * LANES, (kb + 1) * LANES)
                sink_col = jnp.where(top, sink_ref[2 * hp], sink_ref[2 * hp + 1])
                items.append((q_ref[0, :, cols], kk[:, kcols], vv[:, kcols], bias, (kh, kh), sink_col))
                places.append((cols, (kh, kh)))
            for (pv, m, l), (cols, kh2) in zip(_pairs_fwd(items), places):
                o2 = _fold_heads(pv / l, kh2)
                o_ref[0, :, cols] = o2
                ob_ref[0, :, cols] = o2.astype(BF16)
                lse_ref[0, :, cols] = _fold_heads(m + jnp.log(l), kh2)

    prev = lambda n: jnp.maximum(n - 1, 0)
    out = pl.BlockSpec((1, ATT_BLOCK, D_MODEL), lambda b, n: (b, n, 0))
    sd = lambda dt: jax.ShapeDtypeStruct((bsz, s_len, D_MODEL), dt)
    return call_with_exchange(
        body, carried,
        out_shape=(sd(F32), sd(F32), sd(BF16)),
        grid=(bsz, nb),
        in_specs=[pl.BlockSpec(memory_space=pltpu.SMEM), out,
                  pl.BlockSpec((1, ATT_BLOCK, ckv), lambda b, n: (b, prev(n), kblk)),
                  pl.BlockSpec((1, ATT_BLOCK, ckv), lambda b, n: (b, n, kblk)),
                  pl.BlockSpec((1, ATT_BLOCK, ckv), lambda b, n: (b, prev(n), kblk + 1)),
                  pl.BlockSpec((1, ATT_BLOCK, ckv), lambda b, n: (b, n, kblk + 1))],
        out_specs=(out, out, out),
        scratch_shapes=[pltpu.VMEM((2, 2 * ATT_BLOCK, 2 * ATT_BLOCK), F32)],
        name=name,
        args=(sinks, qkv, qkv, qkv, qkv, qkv))


def swa_bwd(qkv, o, lse, do, *, name, carried=None):
    bsz, s_len, width = qkv.shape
    ckv = SWA_KV_HEADS * HEAD_DIM
    nb = s_len // ATT_BLOCK
    kblk = D_MODEL // ckv

    def body(q_ref, kp_ref, kc_ref, vp_ref, vc_ref, o_ref, lse_ref, do_ref, dq_ref, dk_ref, dv_ref, dkk, dvv, ck, cv,
             bias2):
        n = pl.program_id(1)

        @pl.when(n < nb)
        def _():
            _fill_bias(bias2)
            bias = bias2[jnp.minimum(n, 1)]
            kk = jnp.concatenate([kp_ref[0], kc_ref[0]], axis=0)
            vv = jnp.concatenate([vp_ref[0], vc_ref[0]], axis=0)
            dkk[...] = jnp.zeros_like(dkk)
            dvv[...] = jnp.zeros_like(dvv)
            for hp0 in range(0, N_HEADS // 2, PAIRS_AT_ONCE):
                items, places = [], []
                for hp in range(hp0, hp0 + PAIRS_AT_ONCE):
                    cols = slice(hp * LANES, (hp + 1) * LANES)
                    kb, kh = _kv_place(2 * hp, SWA_KV_HEADS)
                    kcols = slice(kb * LANES, (kb + 1) * LANES)
                    items.append((q_ref[0, :, cols], kk[:, kcols], vv[:, kcols], do_ref[0, :, cols], o_ref[0, :, cols],
                                  lse_ref[0, :, cols], bias, (kh, kh)))
                    places.append((cols, kcols))
                for (dq, dk, dv), (cols, kcols) in zip(_pairs_bwd(items), places):
                    dq_ref[0, :, cols] = dq
                    dkk[:, kcols] += dk
                    dvv[:, kcols] += dv

        @pl.when((n >= 1) & (n < nb))
        def _():
            dk_ref[0] = ck[...] + dkk[pl.ds(0, ATT_BLOCK), :]
            dv_ref[0] = cv[...] + dvv[pl.ds(0, ATT_BLOCK), :]

        @pl.when(n == nb)
        def _():
            dk_ref[0] = ck[...]
            dv_ref[0] = cv[...]

        @pl.when(n < nb)
        def _():
            ck[...] = dkk[pl.ds(ATT_BLOCK, ATT_BLOCK), :]
            cv[...] = dvv[pl.ds(ATT_BLOCK, ATT_BLOCK), :]

    clamp = lambda n: jnp.minimum(n, nb - 1)
    prev = lambda n: jnp.maximum(n - 1, 0)
    row = pl.BlockSpec((1, ATT_BLOCK, D_MODEL), lambda b, n: (b, clamp(n), 0))
    kv_out = pl.BlockSpec((1, ATT_BLOCK, ckv), lambda b, n: (b, prev(n), 0))
    return call_with_exchange(
        body, carried,
        out_shape=(jax.ShapeDtypeStruct((bsz, s_len, D_MODEL), F32), jax.ShapeDtypeStruct((bsz, s_len, ckv), F32),
                   jax.ShapeDtypeStruct((bsz, s_len, ckv), F32)),
        grid=(bsz, nb + 1),
        in_specs=[row,
                  pl.BlockSpec((1, ATT_BLOCK, ckv), lambda b, n: (b, prev(clamp(n)), kblk)),
                  pl.BlockSpec((1, ATT_BLOCK, ckv), lambda b, n: (b, clamp(n), kblk)),
                  pl.BlockSpec((1, ATT_BLOCK, ckv), lambda b, n: (b, prev(clamp(n)), kblk + 1)),
                  pl.BlockSpec((1, ATT_BLOCK, ckv), lambda b, n: (b, clamp(n), kblk + 1)),
                  row, row, row],
        out_specs=(row, kv_out, kv_out),
        scratch_shapes=[pltpu.VMEM((2 * ATT_BLOCK, ckv), F32), pltpu.VMEM((2 * ATT_BLOCK, ckv), F32),
                        pltpu.VMEM((ATT_BLOCK, ckv), F32), pltpu.VMEM((ATT_BLOCK, ckv), F32),
                        pltpu.VMEM((2, 2 * ATT_BLOCK, 2 * ATT_BLOCK), F32)],
        name=name,
        args=(qkv, qkv, qkv, qkv, qkv, o, lse, do))


DIL_PATTERNS = tuple((d, 2048 // d // ATT_BLOCK) for d in DILATIONS)
MHA = (0, 1)


def _dil_rows(idx, d, nb):
    j = idx // nb
    n = idx % nb
    base = j + n * (ATT_BLOCK * d)
    prev = jnp.maximum(base - ATT_BLOCK * d, j)
    if d == 1:
        return n, pl.ds(pl.multiple_of(base, ATT_BLOCK), ATT_BLOCK), pl.ds(pl.multiple_of(prev, ATT_BLOCK), ATT_BLOCK)
    return n, pl.ds(base, ATT_BLOCK, stride=d), pl.ds(prev, ATT_BLOCK, stride=d)


def dil_fwd(qkv, *, name, carried=None):
    bsz, s_len, _ = qkv.shape
    assert s_len == DIL_PATTERNS[0][0] * DIL_PATTERNS[0][1] * ATT_BLOCK
    npair = N_HEADS // 2

    def body(q_ref, k_ref, v_ref, y_ref, lse_ref, yb_ref, m_acc, l_acc, bias2, bias1):
        _fill_bias(bias2, bias1)
        for ci, (d, nb) in enumerate(DIL_PATTERNS):
            single = nb == 1

            def blocks(it, carry):
                items, places = [], []
                for u in range(PAIRS_AT_ONCE):
                    n, rows, prows = _dil_rows(it * PAIRS_AT_ONCE + u, d, nb)
                    kc = k_ref[rows, :].astype(BF16)
                    vc = v_ref[rows, :].astype(BF16)
                    if single:
                        kk, vv, bias = kc, vc, bias1[...]
                    else:
                        kk = jnp.concatenate([k_ref[prows, :].astype(BF16), kc], axis=0)
                        vv = jnp.concatenate([v_ref[prows, :].astype(BF16), vc], axis=0)
                        bias = bias2[jnp.minimum(n, 1)]
                    items.append((q_ref[rows, :].astype(BF16), kk, vv, bias, MHA, None))
                    places.append(rows)
                for (pv, m, l), rows in zip(_pairs_fwd(items), places):
                    o2, m2, l2 = _fold_heads(pv, MHA), _fold_heads(m, MHA), _fold_heads(l, MHA)
                    if ci == 0:
                        y_ref[rows, :] = o2
                        m_acc[rows, :] = m2
                        l_acc[rows, :] = l2
                    else:
                        m_old = m_acc[rows, :]
                        m_new = jnp.maximum(m_old, m2)
                        w_old = jnp.exp(m_old - m_new)
                        w_new = jnp.exp(m2 - m_new)
                        y_ref[rows, :] = y_ref[rows, :] * w_old + o2 * w_new
                        l_acc[rows, :] = l_acc[rows, :] * w_old + l2 * w_new
                        m_acc[rows, :] = m_new
                return carry

            lax.fori_loop(0, d * nb // PAIRS_AT_ONCE, blocks, 0)
        y = y_ref[...] / l_acc[...]
        y_ref[...] = y
        yb_ref[...] = y.astype(BF16)
        lse_ref[...] = m_acc[...] + jnp.log(l_acc[...])

    slab = lambda off: pl.BlockSpec((None, s_len, LANES), functools.partial(lambda o, b, h: (b, 0, o + h), off))
    sd = lambda dt: jax.ShapeDtypeStruct((bsz, s_len, D_MODEL), dt)
    return call_with_exchange(
        body, carried,
        out_shape=(sd(F32), sd(F32), sd(BF16)),
        grid=(bsz, npair),
        in_specs=[slab(0), slab(npair), slab(2 * npair)],
        out_specs=(slab(0), slab(0), slab(0)),
        scratch_shapes=[pltpu.VMEM((s_len, LANES), F32), pltpu.VMEM((s_len, LANES), F32),
                        pltpu.VMEM((2, 2 * ATT_BLOCK, 2 * ATT_BLOCK), F32), pltpu.VMEM((2 * ATT_BLOCK, ATT_BLOCK), F32)],
        name=name,
        args=(qkv, qkv, qkv))


def dil_bwd(qkv, y, lse, dy, *, name, carried=None):
    bsz, s_len, _ = qkv.shape
    npair = N_HEADS // 2

    def body(q_ref, k_ref, v_ref, y_ref, lse_ref, dy_ref, dq_ref, dk_ref, dv_ref, bias2, bias1):
        _fill_bias(bias2, bias1)
        dq_ref[...] = jnp.zeros_like(dq_ref)
        dk_ref[...] = jnp.zeros_like(dk_ref)
        dv_ref[...] = jnp.zeros_like(dv_ref)
        for d, nb in DIL_PATTERNS:
            single = nb == 1

            def blocks(it, carry):
                items, places = [], []
                for u in range(PAIRS_AT_ONCE):
                    n, rows, prows = _dil_rows(it * PAIRS_AT_ONCE + u, d, nb)
                    kc = k_ref[rows, :].astype(BF16)
                    vc = v_ref[rows, :].astype(BF16)
                    if single:
                        kk, vv, bias = kc, vc, bias1[...]
                    else:
                        kk = jnp.concatenate([k_ref[prows, :].astype(BF16), kc], axis=0)
                        vv = jnp.concatenate([v_ref[prows, :].astype(BF16), vc], axis=0)
                        bias = bias2[jnp.minimum(n, 1)]
                    items.append((q_ref[rows, :].astype(BF16), kk, vv, dy_ref[rows, :], y_ref[rows, :], lse_ref[rows, :], bias, MHA))
                    places.append((rows, prows))
                for (dq, dk, dv), (rows, prows) in zip(_pairs_bwd(items), places):
                    dq_ref[rows, :] += dq
                    if single:
                        dk_ref[rows, :] += dk
                        dv_ref[rows, :] += dv
                    else:
                        dk_ref[prows, :] += dk[:ATT_BLOCK]
                        dv_ref[prows, :] += dv[:ATT_BLOCK]
                        dk_ref[rows, :] += dk[ATT_BLOCK:]
                        dv_ref[rows, :] += dv[ATT_BLOCK:]
                return carry

            lax.fori_loop(0, d * nb // PAIRS_AT_ONCE, blocks, 0)

    slab = lambda off: pl.BlockSpec((None, s_len, LANES), functools.partial(lambda o, b, h: (b, 0, o + h), off))
    sd = jax.ShapeDtypeStruct((bsz, s_len, D_MODEL), F32)
    return call_with_exchange(
        body, carried,
        out_shape=(sd, sd, sd),
        grid=(bsz, npair),
        in_specs=[slab(0), slab(npair), slab(2 * npair), slab(0), slab(0), slab(0)],
        out_specs=(slab(0), slab(0), slab(0)),
        scratch_shapes=[pltpu.VMEM((2, 2 * ATT_BLOCK, 2 * ATT_BLOCK), F32), pltpu.VMEM((2 * ATT_BLOCK, ATT_BLOCK), F32)],
        name=name,
        args=(qkv, qkv, qkv, y, lse, dy))


def sink_grad(do, o, lse, sink_lanes, *, name):
    t_dim, d = do.shape
    tr = _pick(t_dim, (256, 128, 8))

    def body(do_ref, o_ref, l_ref, s_ref, out_ref):
        @pl.when(pl.program_id(0) == 0)
        def _():
            out_ref[...] = jnp.zeros_like(out_ref)

        out_ref[...] += jnp.sum(-jnp.exp(s_ref[...] - l_ref[...]) * do_ref[...] * o_ref[...], axis=0, keepdims=True)

    row = pl.BlockSpec((tr, d), lambda i: (i, 0))
    vec = pl.BlockSpec((1, d), lambda i: (0, 0))
    return pl.pallas_call(
        body,
        out_shape=jax.ShapeDtypeStruct((1, d), F32),
        grid=(t_dim // tr,),
        in_specs=[row, row, row, vec],
        out_specs=vec,
        compiler_params=_params(("arbitrary",)),
        name=name,
    )(do, o, lse, sink_lanes)


def adamw(w, g, m, v, *, name):
    rows, cols = w.shape
    tr = _pick(rows, (256, 128, 64, 32, 16, 8))

    def body(w_ref, g_ref, m_ref, v_ref, d_ref, nm_ref, nv_ref):
        gv = g_ref[...]
        nm = ADAM_B1 * m_ref[...] + (1.0 - ADAM_B1) * gv
        nv = ADAM_B2 * v_ref[...] + (1.0 - ADAM_B2) * (gv * gv)
        m_hat = nm / (1.0 - ADAM_B1 ** ADAM_STEP)
        v_hat = nv / (1.0 - ADAM_B2 ** ADAM_STEP)
        d_ref[...] = -ADAM_LR * (m_hat / (jnp.sqrt(v_hat) + ADAM_EPS) + ADAM_WD * w_ref[...])
        nm_ref[...] = nm
        nv_ref[...] = nv

    row = pl.BlockSpec((tr, cols), lambda i: (i, 0))
    return pl.pallas_call(
        body,
        out_shape=(jax.ShapeDtypeStruct((rows, cols), F32),) * 3,
        grid=(rows // tr,),
        in_specs=[row] * 4,
        out_specs=(row, row, row),
        compiler_params=_params(("parallel",)),
        name=name,
    )(w, g, m, v)


def _place():
    return lax.axis_index("x"), lax.axis_index("y"), lax.axis_index("c")


def _gather_copies(x_ref, out_ref, send_sems, recv_sems):
    x, y, c = _place()
    me, sibling = (x, y, c), (x, y, 1 - c)
    chips = [(1 - x, y), (x, 1 - y), (1 - x, 1 - y)]

    def slot(px, py, pc):
        return out_ref.at[4 * px + 2 * py + pc]

    def copy(k, block, to, src=None):
        return pltpu.make_async_remote_copy(
            src_ref=slot(*block) if src is None else src, dst_ref=slot(*block),
            send_sem=send_sems.at[k], recv_sem=recv_sems.at[k], device_id=to, device_id_type=MESH)

    first = [lambda: copy(0, me, sibling, src=x_ref)] + [functools.partial(copy, 1 + j, me, (*chip, c), src=x_ref)
                                                         for j, chip in enumerate(chips)]
    passed = [functools.partial(copy, 4 + j, (*chip, c), sibling) for j, chip in enumerate(chips)]
    landing = [functools.partial(copy, 1 + j, (*chip, c), me) for j, chip in enumerate(chips)]
    from_sibling = [lambda: copy(0, sibling, me)] + [functools.partial(copy, 4 + j, (*chip, 1 - c), me) for j, chip in enumerate(chips)]
    return slot(*me), first, passed, landing, from_sibling


def _gather_start(x_ref, out_ref, send_sems, recv_sems, local_sem):
    mine, first, _, _, _ = _gather_copies(x_ref, out_ref, send_sems, recv_sems)
    pltpu.make_async_copy(x_ref, mine, local_sem).start()
    for cp in first:
        cp().start()


def _gather_finish(x_ref, out_ref, send_sems, recv_sems, local_sem):
    mine, first, passed, landing, from_sibling = _gather_copies(x_ref, out_ref, send_sems, recv_sems)
    for cp, fwd in zip(landing, passed):
        cp().wait_recv()
        fwd().start()
    for cp in from_sibling:
        cp().wait_recv()
    for cp in first + passed:
        cp().wait_send()
    pltpu.make_async_copy(x_ref, mine, local_sem).wait()


def _a2a_copies(x_ref, out_ref, send_sems, recv_sems):
    x, y, c = _place()
    me = 4 * x + 2 * y + c
    copies = []
    for k in range(1, N_DEV):
        px = 1 - x if k & 4 else x
        py = 1 - y if k & 2 else y
        pc = 1 - c if k & 1 else c
        copies.append(pltpu.make_async_remote_copy(
            src_ref=x_ref.at[4 * px + 2 * py + pc], dst_ref=out_ref.at[me], send_sem=send_sems.at[k - 1],
            recv_sem=recv_sems.at[k - 1], device_id=(px, py, pc), device_id_type=MESH))
    return me, copies


def _a2a_start(x_ref, out_ref, send_sems, recv_sems, local_sem):
    me, copies = _a2a_copies(x_ref, out_ref, send_sems, recv_sems)
    pltpu.make_async_copy(x_ref.at[me], out_ref.at[me], local_sem).start()
    for cp in copies:
        cp.start()


def _a2a_finish(x_ref, out_ref, send_sems, recv_sems, local_sem):
    me, copies = _a2a_copies(x_ref, out_ref, send_sems, recv_sems)
    for cp in copies:
        cp.wait_recv()
    for cp in copies:
        cp.wait_send()
    pltpu.make_async_copy(x_ref.at[me], out_ref.at[me], local_sem).wait()


EXCHANGES = {"gather": (_gather_start, _gather_finish, lambda x: (N_DEV,) + x.shape),
             "a2a": (_a2a_start, _a2a_finish, lambda x: x.shape)}
EXCHANGE_SEMS = [pltpu.SemaphoreType.DMA((7,)), pltpu.SemaphoreType.DMA((7,)), pltpu.SemaphoreType.DMA(())]


def exchange(kind, x, *, name):
    start, finish, shape = EXCHANGES[kind]

    def body(x_ref, out_ref, *sems):
        start(x_ref, out_ref, *sems)
        finish(x_ref, out_ref, *sems)

    return pl.pallas_call(
        body,
        out_shape=jax.ShapeDtypeStruct(shape(x), x.dtype),
        in_specs=[pl.BlockSpec(memory_space=pl.ANY)],
        out_specs=pl.BlockSpec(memory_space=pl.ANY),
        scratch_shapes=EXCHANGE_SEMS,
        name=name,
    )(x)


def call_with_exchange(body, carried, *, out_shape, grid, in_specs, out_specs, scratch_shapes, name, args):
    sem = ("arbitrary",) * len(grid)
    carried = list(carried or ())
    if not carried:
        res = pl.pallas_call(body, out_shape=out_shape, grid=grid, in_specs=in_specs, out_specs=out_specs,
                             scratch_shapes=scratch_shapes, compiler_params=_params(sem), name=name)(*args)
        return res, []
    n_in, n_out, n_scr, n_x = len(in_specs), len(out_shape), len(scratch_shapes), len(carried)
    n_sems = len(EXCHANGE_SEMS)

    def wrapped(*refs):
        ins, x_refs = refs[:n_in], refs[n_in:n_in + n_x]
        outs = refs[n_in + n_x:n_in + n_x + n_out]
        out_refs = refs[n_in + n_x + n_out:n_in + 2 * n_x + n_out]
        rest = refs[n_in + 2 * n_x + n_out:]
        scratch, sems = rest[:n_scr], rest[n_scr:]
        ids = [pl.program_id(i) for i in range(len(grid))]
        is_first = functools.reduce(lambda a, b: a & b, [i == 0 for i in ids])
        is_last = functools.reduce(lambda a, b: a & b, [i == g - 1 for i, g in zip(ids, grid)])

        @pl.when(is_first)
        def _():
            for e, (kind, _) in enumerate(carried):
                EXCHANGES[kind][0](x_refs[e], out_refs[e], *sems[e * n_sems:(e + 1) * n_sems])

        body(*ins, *outs, *scratch)

        @pl.when(is_last)
        def _():
            for e, (kind, _) in enumerate(carried):
                EXCHANGES[kind][1](x_refs[e], out_refs[e], *sems[e * n_sems:(e + 1) * n_sems])

    any_spec = pl.BlockSpec(memory_space=pl.ANY)
    res = pl.pallas_call(
        wrapped,
        out_shape=tuple(out_shape) + tuple(jax.ShapeDtypeStruct(EXCHANGES[kind][2](x), x.dtype) for kind, x in carried),
        grid=grid,
        in_specs=list(in_specs) + [any_spec] * n_x,
        out_specs=tuple(out_specs) + (any_spec,) * n_x,
        scratch_shapes=list(scratch_shapes) + EXCHANGE_SEMS * n_x,
        compiler_params=_params(sem),
        name=name + "".join("_" + kind for kind, _ in carried),
    )(*args, *[x for _, x in carried])
    return res[:n_out], list(res[n_out:])


def sum_slots(x, *, name):
    _, rows, cols = x.shape
    tr = _pick(rows, (512, 256, 128, 64, 32, 16))

    def body(x_ref, o_ref):
        acc = x_ref[0].astype(F32)
        for k in range(1, N_DEV):
            acc = acc + x_ref[k].astype(F32)
        o_ref[...] = acc

    return pl.pallas_call(
        body,
        out_shape=jax.ShapeDtypeStruct((rows, cols), F32),
        grid=(rows // tr,),
        in_specs=[pl.BlockSpec((N_DEV, tr, cols), lambda i: (0, i, 0))],
        out_specs=pl.BlockSpec((tr, cols), lambda i: (i, 0)),
        compiler_params=_params(("parallel",)),
        name=name,
    )(x)


BIG = ("w_in", "w_branch", "w_out", "w_ffn_in", "w_ffn_out")
SMALL = ("conv_b", "w_rg", "b_rg", "w_ig", "b_ig", "lru_lambda", "sinks", "ln1_g", "ln1_b", "ln2_g", "ln2_b")
N_LRU_BLOCKS = D_MODEL // HEAD_DIM
SMALL_ROWS_TILE = 512


def _block_diag(w):
    z = jnp.zeros((N_LRU_BLOCKS // 2, HEAD_DIM, HEAD_DIM), w.dtype)
    top = jnp.concatenate([w[0::2], z], axis=2)
    bot = jnp.concatenate([z, w[1::2]], axis=2)
    return jnp.concatenate([top, bot], axis=1)


def _block_diag_grad(g):
    return jnp.stack([g[:, :HEAD_DIM, :HEAD_DIM], g[:, HEAD_DIM:, HEAD_DIM:]], axis=1).reshape(N_LRU_BLOCKS, HEAD_DIM, HEAD_DIM)


def layer_fwd(x, xb, p, bsz, own_late=None, next_w_in=None):
    t_dim = x.shape[0]
    s_len = t_dim // bsz
    w_f, w_qs, w_qd = p["w_in_f"], p["w_in_qs"], p["w_in_qd"]
    proj_f = matmul(xb, w_f, name="proj_f")
    qs = matmul(xb, w_qs, out_dtype=BF16, name="proj_qs").reshape(bsz, s_len, W_QS)
    qd = matmul(xb, w_qd, name="proj_qd").reshape(bsz, s_len, W_QD)
    proj_f3 = proj_f.reshape(bsz, s_len, W_F)
    wr_bd, wi_bd = _block_diag(p["w_rg"]), _block_diag(p["w_ig"])
    (y_a, h), got_in = lru_fwd(proj_f3, p["conv_w"], p["conv_b"], wr_bd, wi_bd, p["b_rg"], p["b_ig"], p["lru_lambda"],
                               name="lru_fwd", carried=[("gather", next_w_in)] if next_w_in is not None else [])
    (y_b, lse_b, y_bb), _ = swa_fwd(qs, p["sinks"], name="swa_fwd")
    (y_c, lse_c, y_cb), got_late = dil_fwd(qd, name="dil_fwd", carried=[("gather", t) for t in own_late or ()])
    if own_late is not None:
        p = {**p, **_late_weights(*got_late)}
    ys =[t.reshape(t_dim, D_MODEL) for t in (y_a, y_bb, y_cb)]
    br = [matmul(ys[n], p["w_branch"][n], out_dtype=BF16, name="branch") for n in range(3)]
    merged = merge_fwd(proj_f, br, name="merge_fwd")
    mix = matmul(merged, p["w_out"], name="w_out")
    x1, x1b, z1 = ln_fwd(x, mix, p["ln1_g"], p["ln1_b"], name="ln_fwd")
    h13 = matmul(x1b, p["w_ffn_in"], out_dtype=BF16, name="ffn_in")
    act = swiglu_fwd(h13, name="swiglu_fwd")
    ffn = matmul(act, p["w_ffn_out"], name="ffn_out")
    x2, x2b, z2 = ln_fwd(x1, ffn, p["ln2_g"], p["ln2_b"], name="ln_fwd")
    saved = dict(xb=xb, proj_f=proj_f, qs=qs, qd=qd, h=h, ys=ys, y_b=y_b, y_c=y_c, lse_b=lse_b, lse_c=lse_c, br=br, merged=merged,
                 z1=z1, x1b=x1b, h13=h13, act=act, z2=z2, wr_bd=wr_bd, wi_bd=wi_bd, p=p)
    return x2, x2b, saved, (got_in[0] if got_in else None)


def layer_bwd(dx2, s, bsz, exchange_own=False, above_w_in=None):
    p = s["p"]
    t_dim = dx2.shape[0]
    s_len = t_dim // bsz
    g = {}
    dz2, dz2b, g["ln2_g"], g["ln2_b"] = ln_bwd(dx2, s["z2"], p["ln2_g"], name="ln_bwd")
    dact = matmul(dz2b, p["w_ffn_out"], trans_b=True, out_dtype=BF16, name="d_act")
    dh13 = swiglu_bwd(dact, s["h13"], name="swiglu_bwd")
    g["w_ffn_out"] = matmul(s["act"], dz2b, trans_a=True, out_dtype=BF16, name="dw_ffn_out")
    g["w_ffn_in"] = matmul(s["x1b"], dh13, trans_a=True, out_dtype=BF16, name="dw_ffn_in")
    dx1 = matmul(dh13, p["w_ffn_in"], trans_b=True, add=dz2, add_scale=ALPHA, name="dx_ffn")
    dz1, dz1b, g["ln1_g"], g["ln1_b"] = ln_bwd(dx1, s["z1"], p["ln1_g"], name="ln_bwd")
    dmerged = matmul(dz1b, p["w_out"], trans_b=True, name="d_merged")
    g["w_out"] = matmul(s["merged"], dz1b, trans_a=True, out_dtype=BF16, name="dw_out")
    *dbr, dgates = merge_bwd(dmerged, s["proj_f"], s["br"], name="merge_bwd")
    dys = [matmul(dbr[n], p["w_branch"][n], trans_b=True, out_dtype=F32 if n == 2 else BF16, name="d_branch") for n in range(3)]
    g["w_branch"] = jnp.stack([matmul(s["ys"][n], dbr[n], trans_a=True, out_dtype=BF16, name="dw_branch") for n in range(3)])
    fi_slots, rows_slots = _late_slots(g) if exchange_own else (None, None)
    shape3 = (bsz, s_len, D_MODEL)
    (dlx, dlg, g["conv_w"], g["conv_b"], g["b_rg"], g["b_ig"], g["lru_lambda"], dwr, dwi), got_rows = lru_bwd(
        dys[0].reshape(shape3), s["proj_f"].reshape(bsz, s_len, W_F), s["h"], p["conv_w"], p["conv_b"], s["wr_bd"], s["wi_bd"],
        jnp.swapaxes(s["wr_bd"], 1, 2), jnp.swapaxes(s["wi_bd"], 1, 2), p["b_rg"], p["b_ig"], p["lru_lambda"], name="lru_bwd",
        carried=[("a2a", rows_slots)] if exchange_own else [])
    g["w_rg"], g["w_ig"] = _block_diag_grad(dwr), _block_diag_grad(dwi)
    dy_b3 = dys[1].reshape(shape3)
    dqs, got_fi = swa_bwd(s["qs"], s["y_b"], s["lse_b"], dy_b3, name="swa_bwd", carried=[("a2a", fi_slots)] if exchange_own else [])
    sink_lanes = jnp.repeat(p["sinks"], HEAD_DIM).reshape(1, D_MODEL)
    g["sinks"] = sink_grad(dys[1], s["y_b"].reshape(t_dim, D_MODEL), s["lse_b"].reshape(t_dim, D_MODEL), sink_lanes,
                           name="sink_grad").reshape(N_HEADS, HEAD_DIM).sum(axis=1)
    dqd, got_in = dil_bwd(s["qd"], s["y_c"], s["lse_c"], dys[2].reshape(shape3), name="dil_bwd",
                          carried=[("a2a", above_w_in)] if above_w_in is not None else [])
    flat = lambda t: t.reshape(t_dim, t.shape[-1])
    dproj_f = jnp.concatenate([flat(dlx), flat(dlg), dgates], axis=1)
    dproj_qs = jnp.concatenate([flat(t) for t in dqs], axis=1).astype(BF16)
    dproj_qd = jnp.concatenate([flat(t) for t in dqd], axis=1).astype(BF16)
    g["w_in_f"] = matmul(s["xb"], dproj_f, trans_a=True, out_dtype=BF16, name="dw_in_f")
    g["w_in_qs"] = matmul(s["xb"], dproj_qs, trans_a=True, out_dtype=BF16, name="dw_in_qs")
    g["w_in_qd"] = matmul(s["xb"], dproj_qd, trans_a=True, out_dtype=BF16, name="dw_in_qd")
    dx = matmul(dproj_f, p["w_in_f"], trans_b=True, add=dz1, add_scale=ALPHA, name="dx_f")
    dx = matmul(dproj_qs, p["w_in_qs"], trans_b=True, add=dx, name="dx_qs")
    dx = matmul(dproj_qd, p["w_in_qd"], trans_b=True, add=dx, name="dx_qd")
    g = {k: (v.reshape(p[k].shape) if k in p else v) for k, v in g.items()}
    return dx, g, dict(late=(got_fi[0], got_rows[0]) if exchange_own else None, w_in=got_in[0] if got_in else None)


def local_step(x, target, layer_params, layer_shards=None, first_w_in=None):
    bsz, s_len, d = x.shape
    t_dim = bsz * s_len
    xf = x.reshape(t_dim, d)
    xb = xf.astype(BF16)
    exchanging = layer_shards is not None
    saved, gathered = [], first_w_in
    for l in range(DEPTH):
        p = layer_params(l, gathered)
        xf, xb, s, gathered = layer_fwd(xf, xb, p, bsz, own_late=layer_shards[l][1:] if exchanging else None,
                                        next_w_in=layer_shards[l + 1][0] if exchanging and l + 1 < DEPTH else None)
        saved.append(s)
    dy, sq = loss_head(xf, target.reshape(t_dim, d), name="loss_head")
    loss = 0.5 * jnp.sum(sq) / d
    grads, received, w_in_slots = [None] * DEPTH, [[None] * 3 for _ in range(DEPTH)], None
    for l in reversed(range(DEPTH)):
        dy, grads[l], got = layer_bwd(dy, saved[l], bsz, exchange_own=exchanging, above_w_in=w_in_slots)
        if got["w_in"] is not None:
            received[l + 1][0] = got["w_in"]
        if exchanging:
            received[l][1:] = got["late"]
            w_in_slots = _w_in_slots(grads[l])
    return loss, dy.reshape(bsz, s_len, d), grads, received, w_in_slots


W_IN_SEGMENTS = (("w_in_f", 0, 0, 2 * D_MODEL), ("w_in_qs", 0, 2 * D_MODEL, W_QS), ("w_in_qd", 0, 2 * D_MODEL + W_QS, W_QD),
                 ("w_in_f", 2 * D_MODEL, 2 * D_MODEL + W_QS + W_QD, 3 * D_MODEL))
ROW_SHARDED = ("w_branch", "w_out", "w_ffn_out")


def _cols_of_shards(shards, lo, hi):
    width = shards[0].shape[-1]
    parts = []
    for k, sh in enumerate(shards):
        a, b = max(lo, k * width), min(hi, (k + 1) * width)
        if a < b:
            parts.append(sh[..., a - k * width:b - k * width])
    return parts[0] if len(parts) == 1 else jnp.concatenate(parts, axis=-1)


def _cols_of_w_in(pieces, lo, hi):
    parts = []
    for name, p0, l0, width in W_IN_SEGMENTS:
        a, b = max(lo, l0), min(hi, l0 + width)
        if a < b:
            parts.append(pieces[name][..., p0 + a - l0:p0 + b - l0])
    return parts[0] if len(parts) == 1 else jnp.concatenate(parts, axis=-1)


W_IN_COLS = W_F + W_QS + W_QD


def _layer_shards(w, l):
    rows = jnp.concatenate([w[k][l].reshape(-1, D_MODEL) for k in ROW_SHARDED]).astype(BF16)
    return w["w_in"][l].astype(BF16), w["w_ffn_in"][l].astype(BF16), rows


ROW_COUNTS = (3 * D_MODEL // N_DEV, D_MODEL // N_DEV, FF_HIDDEN // N_DEV)


def _w_in_weights(g_in):
    sh = [g_in[k] for k in range(N_DEV)]
    return dict(w_in_f=jnp.concatenate([_cols_of_shards(sh, 0, 2 * D_MODEL), _cols_of_shards(sh, W_IN_COLS - 3 * D_MODEL, W_IN_COLS)], axis=-1),
                w_in_qs=_cols_of_shards(sh, 2 * D_MODEL, 2 * D_MODEL + W_QS),
                w_in_qd=_cols_of_shards(sh, 2 * D_MODEL + W_QS, 2 * D_MODEL + W_QS + W_QD))


def _late_weights(g_fi, g_rows):
    p = dict(w_ffn_in=jnp.concatenate([g_fi[k] for k in range(N_DEV)], axis=-1))
    off = 0
    for k, n in zip(ROW_SHARDED, ROW_COUNTS):
        t = g_rows[:, off:off + n]
        if k == "w_branch":
            p[k] = jnp.transpose(t.reshape(N_DEV, 3, n // 3, D_MODEL), (1, 0, 2, 3)).reshape(3, -1, D_MODEL)
        else:
            p[k] = t.reshape(-1, D_MODEL)
        off += n
    return p


def _w_in_slots(g):
    shard = W_IN_COLS // N_DEV
    return jnp.stack([_cols_of_w_in(g, k * shard, (k + 1) * shard) for k in range(N_DEV)]).astype(BF16)


def _late_slots(g):
    shard = g["w_ffn_in"].shape[-1] // N_DEV
    s_fi = jnp.stack([g["w_ffn_in"][:, k * shard:(k + 1) * shard] for k in range(N_DEV)]).astype(BF16)
    rows = jnp.concatenate([jnp.transpose(g["w_branch"].reshape(3, N_DEV, -1, D_MODEL), (1, 0, 2, 3)).reshape(N_DEV, -1, D_MODEL),
                            g["w_out"].reshape(N_DEV, -1, D_MODEL), g["w_ffn_out"].reshape(N_DEV, -1, D_MODEL)], axis=1).astype(BF16)
    return s_fi, rows


def _pad_rows(flat, tile_rows):
    n = flat.shape[0]
    per = tile_rows * LANES
    total = -(-n // per) * per
    return jnp.pad(flat, (0, total - n)).reshape(-1, LANES)


def kernel(x, w_in, conv_w, conv_b, w_rg, b_rg, w_ig, b_ig, lru_lambda, sinks, w_branch, w_out, ln1_g, ln1_b, w_ffn_in, w_ffn_out, ln2_g, ln2_b, loss_target, m_w_in, m_conv_w, m_conv_b, m_w_rg, m_b_rg, m_w_ig, m_b_ig, m_lru_lambda, m_sinks, m_w_branch, m_w_out, m_ln1_g, m_ln1_b, m_w_ffn_in, m_w_ffn_out, m_ln2_g, m_ln2_b, v_w_in, v_conv_w, v_conv_b, v_w_rg, v_b_rg, v_w_ig, v_b_ig, v_lru_lambda, v_sinks, v_w_branch, v_w_out, v_ln1_g, v_ln1_b, v_w_ffn_in, v_w_ffn_out, v_ln2_g, v_ln2_b):
    w = dict(w_in=w_in, conv_w=conv_w, conv_b=conv_b, w_rg=w_rg, b_rg=b_rg, w_ig=w_ig, b_ig=b_ig, lru_lambda=lru_lambda, sinks=sinks,
             w_branch=w_branch, w_out=w_out, ln1_g=ln1_g, ln1_b=ln1_b, w_ffn_in=w_ffn_in, w_ffn_out=w_ffn_out, ln2_g=ln2_g, ln2_b=ln2_b)
    m = dict(w_in=m_w_in, conv_w=m_conv_w, conv_b=m_conv_b, w_rg=m_w_rg, b_rg=m_b_rg, w_ig=m_w_ig, b_ig=m_b_ig, lru_lambda=m_lru_lambda,
             sinks=m_sinks, w_branch=m_w_branch, w_out=m_w_out, ln1_g=m_ln1_g, ln1_b=m_ln1_b, w_ffn_in=m_w_ffn_in, w_ffn_out=m_w_ffn_out,
             ln2_g=m_ln2_g, ln2_b=m_ln2_b)
    v = dict(w_in=v_w_in, conv_w=v_conv_w, conv_b=v_conv_b, w_rg=v_w_rg, b_rg=v_b_rg, w_ig=v_w_ig, b_ig=v_b_ig, lru_lambda=v_lru_lambda,
             sinks=v_sinks, w_branch=v_w_branch, w_out=v_w_out, ln1_g=v_ln1_g, ln1_b=v_ln1_b, w_ffn_in=v_w_ffn_in, w_ffn_out=v_w_ffn_out,
             ln2_g=v_ln2_g, ln2_b=v_ln2_b)
    order = ["w_in", "conv_w", "conv_b", "w_rg", "b_rg", "w_ig", "b_ig", "lru_lambda", "sinks", "w_branch", "w_out", "ln1_g", "ln1_b",
             "w_ffn_in", "w_ffn_out", "ln2_g", "ln2_b"]
    me = 4 * lax.axis_index("x") + 2 * lax.axis_index("y") + lax.axis_index("c")

    names = ("w_in", "w_ffn_in", "w_rows")
    shards = [_layer_shards(w, l) for l in range(DEPTH)]
    first_w_in = exchange("gather", shards[0][0], name="gather_w_in")
    cw = exchange("gather", conv_w.reshape(-1, LANES), name="gather_conv_w")
    conv_w_full = jnp.moveaxis(cw.reshape(N_DEV, DEPTH, CONV_WIDTH, LANES), 0, 2).reshape(DEPTH, CONV_WIDTH, D_MODEL)

    def layer_params(l, gathered_w_in):
        return {**_w_in_weights(gathered_w_in), **{k: w[k][l] for k in SMALL}, "conv_w": conv_w_full[l]}

    loss_local, grad_x, grads, received, w_in_slots = local_step(x, loss_target, layer_params, shards, first_w_in)
    loss = lax.psum(loss_local, ("x", "y", "c"))
    received[0][0] = exchange("a2a", w_in_slots, name="exchange_g_w_in")

    sums = [[sum_slots(t, name=f"sum_g_{n}") for t, n in zip(received[l], names)] for l in range(DEPTH)]
    g_final = {"w_in": jnp.stack([sums[l][0] for l in range(DEPTH)]), "w_ffn_in": jnp.stack([sums[l][1] for l in range(DEPTH)])}
    off = 0
    for k, n in zip(ROW_SHARDED, ROW_COUNTS):
        g_final[k] = jnp.stack([sums[l][2][off:off + n] for l in range(DEPTH)]).reshape(w[k].shape)
        off += n
    grads = {k: jnp.stack([grads[l][k] for l in range(DEPTH)]) for k in list(SMALL) + ["conv_w"]}

    small_names = list(SMALL) + ["conv_w"]
    small_sizes = [grads[k].size for k in small_names]
    svec = _pad_rows(jnp.concatenate([grads[k].reshape(-1) for k in small_names]), SMALL_ROWS_TILE)
    ssum = sum_slots(exchange("gather", svec, name="gather_small_grads"), name="sum_small_grads")
    sflat, off = ssum.reshape(-1), 0
    for k, n in zip(small_names, small_sizes):
        g_final[k] = sflat[off:off + n].reshape(grads[k].shape)
        off += n
    g_final["conv_w"] = lax.dynamic_slice_in_dim(g_final["conv_w"], me * LANES, LANES, axis=2)

    delta, new_m, new_v = {}, {}, {}
    for k in list(BIG) + ["conv_w"]:
        cols = w[k].shape[-1]
        two_d = lambda t: t.reshape(-1, cols)
        d_, m_, v_ = adamw(two_d(w[k]), two_d(g_final[k]), two_d(m[k]), two_d(v[k]), name=f"adamw_{k}")
        delta[k], new_m[k], new_v[k] = d_.reshape(w[k].shape), m_.reshape(w[k].shape), v_.reshape(w[k].shape)
    pack_small = lambda dct: _pad_rows(jnp.concatenate([dct[k].reshape(-1) for k in SMALL]), SMALL_ROWS_TILE)
    d_, m_, v_ = adamw(pack_small(w), pack_small(g_final), pack_small(m), pack_small(v), name="adamw_small")
    off = 0
    for k in SMALL:
        n = w[k].size
        for dst, src in ((delta, d_), (new_m, m_), (new_v, v_)):
            dst[k] = src.reshape(-1)[off:off + n].reshape(w[k].shape)
        off += n
    return (loss, grad_x, *[g_final[k] for k in order], *[delta[k] for k in order], *[new_m[k] for k in order], *[new_v[k] for k in order])
```

```python
import functools
import math

import jax
import jax.numpy as jnp
from jax import lax
from jax.experimental import pallas as pl
from jax.experimental.pallas import tpu as pltpu

F32 = jnp.float32
BF16 = jnp.bfloat16

N_DEV = 8
DEPTH = 4
D_MODEL = 1024
HEAD_DIM = 64
LANES = 128
N_HEADS = D_MODEL // HEAD_DIM
SWA_KV_HEADS = 4
ATT_BLOCK = 128
DILATIONS = (1, 4, 16)
CONV_WIDTH = 4
LRU_C = 8.0
FF_HIDDEN = 2816
ALPHA = (2.0 * DEPTH) ** 0.25
LN_EPS = 1e-5
NEG_INF = -1e30
W_F = 5 * D_MODEL
W_QS = D_MODEL + 2 * SWA_KV_HEADS * HEAD_DIM
W_QD = 3 * D_MODEL

ADAM_LR = 0.001
ADAM_B1 = 0.9
ADAM_B2 = 0.999
ADAM_EPS = 1e-08
ADAM_WD = 0.01
ADAM_STEP = 10

VMEM_LIMIT = 56 * 1024 * 1024
MATMUL_BLOCK_BYTES = 40 * 1024 * 1024
MESH = pl.DeviceIdType.MESH


def _pick(n, cands):
    for c in cands:
        if n % c == 0:
            return c
    raise ValueError(f"no tile for {n} among {cands}")


def _params(sem):
    return pltpu.CompilerParams(dimension_semantics=sem, vmem_limit_bytes=VMEM_LIMIT)


def _tile(n, cap):
    best = None
    for t in range(LANES, cap + 1, LANES):
        if n % t == 0:
            best = t
    assert best is not None, (n, cap)
    return best


def matmul(a, b, *, name, trans_a=False, trans_b=False, out_dtype=F32, add=None, add_scale=1.0):
    if trans_a:
        k_dim, m_dim = a.shape
    else:
        m_dim, k_dim = a.shape
    n_dim = b.shape[0] if trans_b else b.shape[1]
    assert (b.shape[1] if trans_b else b.shape[0]) == k_dim
    tn = _tile(n_dim, 1408)
    tm, tk = _tile(m_dim, 1024), _tile(k_dim, 1408)
    for cand in (1024, 512, 256):
        ctm = _tile(m_dim, cand)
        blocks = 2 * (ctm * k_dim * a.dtype.itemsize + tn * k_dim * b.dtype.itemsize + ctm * tn * jnp.dtype(out_dtype).itemsize
                      + (ctm * tn * add.dtype.itemsize if add is not None else 0))
        if blocks <= MATMUL_BLOCK_BYTES:
            tm, tk = ctm, k_dim
            break
    nk = k_dim // tk
    dims = (((0 if trans_a else 1,), (1 if trans_b else 0,)), ((), ()))

    def body(*refs):
        if add is None:
            a_ref, b_ref, o_ref, acc_ref = refs
            add_ref = None
        else:
            a_ref, b_ref, add_ref, o_ref, acc_ref = refs
        k = pl.program_id(2)
        part = lax.dot_general(a_ref[...].astype(BF16), b_ref[...].astype(BF16), dims, preferred_element_type=F32)

        def finish(r):
            if add_ref is not None:
                r = r + add_scale * add_ref[...].astype(F32)
            o_ref[...] = r.astype(out_dtype)

        if nk == 1:
            finish(part)
        else:
            @pl.when(k == 0)
            def _():
                acc_ref[...] = part

            @pl.when((k > 0) & (k < nk - 1))
            def _():
                acc_ref[...] += part

            @pl.when(k == nk - 1)
            def _():
                finish(acc_ref[...] + part)

    a_spec = pl.BlockSpec((tk, tm), lambda i, j, k: (k, i)) if trans_a else pl.BlockSpec((tm, tk), lambda i, j, k: (i, k))
    b_spec = pl.BlockSpec((tn, tk), lambda i, j, k: (j, k)) if trans_b else pl.BlockSpec((tk, tn), lambda i, j, k: (k, j))
    in_specs = [a_spec, b_spec]
    args = [a, b]
    if add is not None:
        in_specs.append(pl.BlockSpec((tm, tn), lambda i, j, k: (i, j)))
        args.append(add)
    return pl.pallas_call(
        body,
        out_shape=jax.ShapeDtypeStruct((m_dim, n_dim), out_dtype),
        grid=(m_dim // tm, n_dim // tn, nk),
        in_specs=in_specs,
        out_specs=pl.BlockSpec((tm, tn), lambda i, j, k: (i, j)),
        scratch_shapes=[pltpu.VMEM((tm, tn) if nk > 1 else (8, LANES), F32)],
        compiler_params=_params(("parallel", "parallel", "arbitrary")),
        name=name,
    )(*args)


def ln_fwd(x, r, g, b, *, name):
    t_dim, d = x.shape
    tr = _pick(t_dim, (256, 128, 8))

    def body(x_ref, r_ref, g_ref, b_ref, y_ref, yb_ref, z_ref):
        z = ALPHA * x_ref[...] + r_ref[...]
        mu = jnp.mean(z, axis=-1, keepdims=True)
        zc = z - mu
        var = jnp.mean(zc * zc, axis=-1, keepdims=True)
        y = zc * lax.rsqrt(var + LN_EPS) * g_ref[...] + b_ref[...]
        y_ref[...] = y
        yb_ref[...] = y.astype(BF16)
        z_ref[...] = z

    row = pl.BlockSpec((tr, d), lambda i: (i, 0))
    vec = pl.BlockSpec((1, d), lambda i: (0, 0))
    return pl.pallas_call(
        body,
        out_shape=(jax.ShapeDtypeStruct((t_dim, d), F32), jax.ShapeDtypeStruct((t_dim, d), BF16), jax.ShapeDtypeStruct((t_dim, d), F32)),
        grid=(t_dim // tr,),
        in_specs=[row, row, vec, vec],
        out_specs=(row, row, row),
        compiler_params=_params(("parallel",)),
        name=name,
    )(x, r, g.reshape(1, d), b.reshape(1, d))


def ln_bwd(dy, z, g, *, name):
    t_dim, d = dy.shape
    tr = _pick(t_dim, (256, 128, 8))

    def body(dy_ref, z_ref, g_ref, dz_ref, dzb_ref, dg_ref, db_ref):
        @pl.when(pl.program_id(0) == 0)
        def _():
            dg_ref[...] = jnp.zeros_like(dg_ref)
            db_ref[...] = jnp.zeros_like(db_ref)

        z = z_ref[...]
        dyv = dy_ref[...]
        mu = jnp.mean(z, axis=-1, keepdims=True)
        zc = z - mu
        var = jnp.mean(zc * zc, axis=-1, keepdims=True)
        rstd = lax.rsqrt(var + LN_EPS)
        xhat = zc * rstd
        dxhat = dyv * g_ref[...]
        m1 = jnp.mean(dxhat, axis=-1, keepdims=True)
        m2 = jnp.mean(dxhat * xhat, axis=-1, keepdims=True)
        dz = rstd * (dxhat - m1 - xhat * m2)
        dz_ref[...] = dz
        dzb_ref[...] = dz.astype(BF16)
        dg_ref[...] += jnp.sum(dyv * xhat, axis=0, keepdims=True)
        db_ref[...] += jnp.sum(dyv, axis=0, keepdims=True)

    row = pl.BlockSpec((tr, d), lambda i: (i, 0))
    vec = pl.BlockSpec((1, d), lambda i: (0, 0))
    return pl.pallas_call(
        body,
        out_shape=(jax.ShapeDtypeStruct((t_dim, d), F32), jax.ShapeDtypeStruct((t_dim, d), BF16),
                   jax.ShapeDtypeStruct((1, d), F32), jax.ShapeDtypeStruct((1, d), F32)),
        grid=(t_dim // tr,),
        in_specs=[row, row, vec],
        out_specs=(row, row, vec, vec),
        compiler_params=_params(("arbitrary",)),
        name=name,
    )(dy, z, g.reshape(1, d))


def loss_head(y, target, *, name):
    t_dim, d = y.shape
    tr = _pick(t_dim, (256, 128, 8))

    def body(y_ref, t_ref, dy_ref, sq_ref):
        @pl.when(pl.program_id(0) == 0)
        def _():
            sq_ref[...] = jnp.zeros_like(sq_ref)

        diff = y_ref[...] - t_ref[...]
        dy_ref[...] = diff / d
        sq_ref[...] += jnp.sum(diff * diff, axis=0, keepdims=True)

    row = pl.BlockSpec((tr, d), lambda i: (i, 0))
    vec = pl.BlockSpec((1, d), lambda i: (0, 0))
    return pl.pallas_call(
        body,
        out_shape=(jax.ShapeDtypeStruct((t_dim, d), F32), jax.ShapeDtypeStruct((1, d), F32)),
        grid=(t_dim // tr,),
        in_specs=[row, row],
        out_specs=(row, vec),
        compiler_params=_params(("arbitrary",)),
        name=name,
    )(y, target)


def _sigmoid(x):
    return 0.5 * jnp.tanh(0.5 * x) + 0.5


def swiglu_fwd(h13, *, name):
    t_dim = h13.shape[0]
    f = h13.shape[1] // 2
    tr = _pick(t_dim, (256, 128, 8))

    def body(h1_ref, h3_ref, act_ref):
        h1 = h1_ref[...].astype(F32)
        act_ref[...] = (h1 * _sigmoid(h1) * h3_ref[...].astype(F32)).astype(BF16)

    return pl.pallas_call(
        body,
        out_shape=jax.ShapeDtypeStruct((t_dim, f), BF16),
        grid=(t_dim // tr,),
        in_specs=[pl.BlockSpec((tr, f), lambda i: (i, 0)), pl.BlockSpec((tr, f), lambda i: (i, 1))],
        out_specs=pl.BlockSpec((tr, f), lambda i: (i, 0)),
        compiler_params=_params(("parallel",)),
        name=name,
    )(h13, h13)


def swiglu_bwd(dact, h13, *, name):
    t_dim = h13.shape[0]
    f = h13.shape[1] // 2
    tr = _pick(t_dim, (256, 128, 8))

    def body(da_ref, h1_ref, h3_ref, dh_ref):
        h1 = h1_ref[...].astype(F32)
        da = da_ref[...].astype(F32)
        sg = _sigmoid(h1)
        dh_ref[:, :f] = (da * h3_ref[...].astype(F32) * sg * (1.0 + h1 * (1.0 - sg))).astype(BF16)
        dh_ref[:, f:] = (da * h1 * sg).astype(BF16)

    return pl.pallas_call(
        body,
        out_shape=jax.ShapeDtypeStruct((t_dim, 2 * f), BF16),
        grid=(t_dim // tr,),
        in_specs=[pl.BlockSpec((tr, f), lambda i: (i, 0)), pl.BlockSpec((tr, f), lambda i: (i, 0)),
                  pl.BlockSpec((tr, f), lambda i: (i, 1))],
        out_specs=pl.BlockSpec((tr, 2 * f), lambda i: (i, 0)),
        compiler_params=_params(("parallel",)),
        name=name,
    )(dact, h13, h13)


def merge_fwd(proj_f, br, *, name):
    t_dim, d = br[0].shape
    tr = _pick(t_dim, (256, 128, 8))

    def body(g0, g1, g2, b0, b1, b2, o_ref):
        o_ref[...] = (_sigmoid(g0[...]) * b0[...].astype(F32) + _sigmoid(g1[...]) * b1[...].astype(F32)
                      + _sigmoid(g2[...]) * b2[...].astype(F32)).astype(BF16)

    row = pl.BlockSpec((tr, d), lambda i: (i, 0))
    gate = [pl.BlockSpec((tr, d), functools.partial(lambda n, i: (i, 2 + n), n)) for n in range(3)]
    return pl.pallas_call(
        body,
        out_shape=jax.ShapeDtypeStruct((t_dim, d), BF16),
        grid=(t_dim // tr,),
        in_specs=gate + [row, row, row],
        out_specs=row,
        compiler_params=_params(("parallel",)),
        name=name,
    )(proj_f, proj_f, proj_f, *br)


def merge_bwd(dmerged, proj_f, br, *, name):
    t_dim, d = dmerged.shape
    tr = _pick(t_dim, (256, 128, 8))

    def body(dm_ref, g0, g1, g2, b0, b1, b2, d0, d1, d2, dg_ref):
        dm = dm_ref[...]
        for n, (g, b, o) in enumerate(((g0, b0, d0), (g1, b1, d1), (g2, b2, d2))):
            sg = _sigmoid(g[...])
            o[...] = (dm * sg).astype(BF16)
            dg_ref[:, n * d:(n + 1) * d] = (dm * b[...].astype(F32) * sg * (1.0 - sg)).astype(BF16)

    row = pl.BlockSpec((tr, d), lambda i: (i, 0))
    gate = [pl.BlockSpec((tr, d), functools.partial(lambda n, i: (i, 2 + n), n)) for n in range(3)]
    return pl.pallas_call(
        body,
        out_shape=(jax.ShapeDtypeStruct((t_dim, d), BF16),) * 3 + (jax.ShapeDtypeStruct((t_dim, 3 * d), BF16),),
        grid=(t_dim // tr,),
        in_specs=[row] + gate + [row, row, row],
        out_specs=(row, row, row, pl.BlockSpec((tr, 3 * d), lambda i: (i, 0))),
        compiler_params=_params(("parallel",)),
        name=name,
    )(dmerged, proj_f, proj_f, proj_f, *br)


GELU_C = math.sqrt(2.0 / math.pi)
PAD = 8
SCAN_TILES = 8


def _gelu(x):
    return 0.5 * x * (1.0 + jnp.tanh(GELU_C * (x + 0.044715 * x * x * x)))


def _gelu_grad(x):
    t = jnp.tanh(GELU_C * (x + 0.044715 * x * x * x))
    return 0.5 * (1.0 + t) + 0.5 * x * (1.0 - t * t) * GELU_C * (1.0 + 3.0 * 0.044715 * x * x)


def _neg_expm1(x, exp_x):
    series = -x * (1.0 + x * (0.5 + x * (1.0 / 6.0)))
    return jnp.where(x > -0.02, series, 1.0 - exp_x)


def _lru_gates(xv, cw_ref, cb_ref, wr_ref, wi_ref, br_ref, bi_ref, lam_ref, pad_ref, s_len):
    pad_ref[pl.ds(0, PAD), :] = jnp.zeros((PAD, LANES), F32)
    pad_ref[pl.ds(PAD, s_len), :] = xv
    xc = cb_ref[...] + jnp.zeros((s_len, LANES), F32)
    for j in range(CONV_WIDTH):
        xc = xc + pad_ref[pl.ds(PAD - (CONV_WIDTH - 1) + j, s_len), :] * cw_ref[pl.ds(j, 1), :]
    xcb = xc.astype(BF16)
    r = _sigmoid(jnp.dot(xcb, wr_ref[0].astype(BF16), preferred_element_type=F32) + br_ref[...])
    i = _sigmoid(jnp.dot(xcb, wi_ref[0].astype(BF16), preferred_element_type=F32) + bi_ref[...])
    nl = -lam_ref[...]
    sp = jnp.maximum(nl, 0.0) + jnp.log(1.0 + jnp.exp(-jnp.abs(nl)))
    log_a = -LRU_C * r * sp
    a = jnp.exp(log_a)
    mult = jnp.sqrt(_neg_expm1(2.0 * log_a, a * a))
    return xc, r, i, sp, a, mult


def _tile_scan(a, b, row, reverse):
    for s in (1, 2, 4):
        if reverse:
            a_sh = pltpu.roll(a, 8 - s, 0)
            b_sh = pltpu.roll(b, 8 - s, 0)
            m = row + s <= 7
        else:
            a_sh = pltpu.roll(a, s, 0)
            b_sh = pltpu.roll(b, s, 0)
            m = row >= s
        b = jnp.where(m, a * b_sh + b, b)
        a = jnp.where(m, a * a_sh, a)
    return a, b


def lru_fwd(proj_f, conv_w, conv_b, wr_bd, wi_bd, b_rg, b_ig, lam, *, name, carried=None):
    bsz, s_len, _ = proj_f.shape
    d = D_MODEL
    ncb = d // LANES
    n_tiles = s_len // 8

    def body(x_ref, g_ref, cw_ref, cb_ref, wr_ref, wi_ref, br_ref, bi_ref, lam_ref, y_ref, h_ref, pad_ref, a_s, b_s):
        xc, r, i, sp, a, mult = _lru_gates(x_ref[0], cw_ref, cb_ref, wr_ref, wi_ref, br_ref, bi_ref, lam_ref, pad_ref, s_len)
        a_s[...] = a
        b_s[...] = mult * (i * xc)
        row = lax.broadcasted_iota(jnp.int32, (8, LANES), 0)

        def tiles(t, carry):
            starts = [pl.multiple_of((t * SCAN_TILES + u) * 8, 8) for u in range(SCAN_TILES)]
            local = [_tile_scan(a_s[pl.ds(i0, 8), :], b_s[pl.ds(i0, 8), :], row, False) for i0 in starts]
            for i0, (ac, hl) in zip(starts, local):
                h = hl + ac * carry
                h_ref[0, pl.ds(i0, 8), :] = h
                carry = jnp.broadcast_to(h[7:8, :], (8, LANES))
            return carry

        lax.fori_loop(0, n_tiles // SCAN_TILES, tiles, jnp.zeros((8, LANES), F32))
        y_ref[0] = (h_ref[0] * _gelu(g_ref[0])).astype(BF16)

    slab = lambda off: pl.BlockSpec((1, s_len, LANES), functools.partial(lambda o, c, b: (b, 0, o + c), off))
    vec = pl.BlockSpec((1, LANES), lambda c, b: (0, c))
    mat = pl.BlockSpec((1, LANES, LANES), lambda c, b: (c, 0, 0))
    out = pl.BlockSpec((1, s_len, LANES), lambda c, b: (b, 0, c))
    return call_with_exchange(
        body, carried,
        out_shape=(jax.ShapeDtypeStruct((bsz, s_len, d), BF16), jax.ShapeDtypeStruct((bsz, s_len, d), F32)),
        grid=(ncb, bsz),
        in_specs=[slab(0), slab(ncb), pl.BlockSpec((CONV_WIDTH, LANES), lambda c, b: (0, c)), vec, mat, mat, vec, vec, vec],
        out_specs=(out, out),
        scratch_shapes=[pltpu.VMEM((s_len + 2 * PAD, LANES), F32), pltpu.VMEM((s_len, LANES), F32), pltpu.VMEM((s_len, LANES), F32)],
        name=name,
        args=(proj_f, proj_f, conv_w, conv_b.reshape(1, d), wr_bd, wi_bd, b_rg.reshape(1, d), b_ig.reshape(1, d), lam.reshape(1, d)))


def lru_bwd(dy, proj_f, h, conv_w, conv_b, wr_bd, wi_bd, wr_bd_t, wi_bd_t, b_rg, b_ig, lam, *, name, carried=None):
    bsz, s_len, _ = proj_f.shape
    d = D_MODEL
    ncb = d // LANES
    n_tiles = s_len // 8

    def body(dy_ref, x_ref, g_ref, h_ref, cw_ref, cb_ref, wr_ref, wi_ref, wrt_ref, wit_ref, br_ref, bi_ref, lam_ref,
             dx_ref, dg_ref, dcw_ref, dcb_ref, dbr_ref, dbi_ref, dlam_ref, dwr_ref, dwi_ref, pad_ref, a_s, b_s, l_s):
        @pl.when(pl.program_id(1) == 0)
        def _():
            for ref in (dcw_ref, dcb_ref, dbr_ref, dbi_ref, dlam_ref, dwr_ref, dwi_ref):
                ref[...] = jnp.zeros_like(ref)

        xc, r, i, sp, a, mult = _lru_gates(x_ref[0], cw_ref, cb_ref, wr_ref, wi_ref, br_ref, bi_ref, lam_ref, pad_ref, s_len)
        gate = g_ref[0]
        hv = h_ref[0]
        dyv = dy_ref[0].astype(F32)
        dg_ref[0] = (dyv * hv * _gelu_grad(gate)).astype(BF16)
        b_s[...] = dyv * _gelu(gate)
        l_s[pl.ds(0, s_len), :] = a
        l_s[pl.ds(s_len, PAD), :] = jnp.zeros((PAD, LANES), F32)
        a_s[...] = l_s[pl.ds(1, s_len), :]
        row = lax.broadcasted_iota(jnp.int32, (8, LANES), 0)

        def tiles(t, carry):
            starts = [pl.multiple_of((n_tiles - 1 - (t * SCAN_TILES + u)) * 8, 8) for u in range(SCAN_TILES)]
            local = [_tile_scan(a_s[pl.ds(i0, 8), :], b_s[pl.ds(i0, 8), :], row, True) for i0 in starts]
            for i0, (ac, ll) in zip(starts, local):
                lmb = ll + ac * carry
                b_s[pl.ds(i0, 8), :] = lmb
                carry = jnp.broadcast_to(lmb[0:1, :], (8, LANES))
            return carry

        lax.fori_loop(0, n_tiles // SCAN_TILES, tiles, jnp.zeros((8, LANES), F32))
        lmb = b_s[...]
        l_s[pl.ds(0, PAD), :] = jnp.zeros((PAD, LANES), F32)
        l_s[pl.ds(PAD, s_len), :] = hv
        h_prev = l_s[pl.ds(PAD - 1, s_len), :]
        da = lmb * h_prev
        dmult = lmb * (i * xc)
        di = lmb * mult * xc
        dxc = lmb * mult * i
        dlog_a = da * a - dmult * a * a / mult
        dr = -LRU_C * sp * dlog_a
        dsp = jnp.sum(-LRU_C * r * dlog_a, axis=0, keepdims=True)
        dlam_ref[...] += dsp * (-_sigmoid(-lam_ref[...]))
        dpr = dr * r * (1.0 - r)
        dpi = di * i * (1.0 - i)
        dprb = dpr.astype(BF16)
        dpib = dpi.astype(BF16)
        xcb = xc.astype(BF16)
        dbr_ref[...] += jnp.sum(dpr, axis=0, keepdims=True)
        dbi_ref[...] += jnp.sum(dpi, axis=0, keepdims=True)
        tn = (((0,), (0,)), ((), ()))
        dwr_ref[0] += lax.dot_general(xcb, dprb, tn, preferred_element_type=F32)
        dwi_ref[0] += lax.dot_general(xcb, dpib, tn, preferred_element_type=F32)
        dxc = (dxc + jnp.dot(dprb, wrt_ref[0].astype(BF16), preferred_element_type=F32)
               + jnp.dot(dpib, wit_ref[0].astype(BF16), preferred_element_type=F32))
        dcb_ref[...] += jnp.sum(dxc, axis=0, keepdims=True)
        for j in range(CONV_WIDTH):
            dcw_ref[pl.ds(j, 1), :] += jnp.sum(dxc * pad_ref[pl.ds(PAD - (CONV_WIDTH - 1) + j, s_len), :], axis=0, keepdims=True)
        l_s[pl.ds(0, s_len), :] = dxc
        l_s[pl.ds(s_len, PAD), :] = jnp.zeros((PAD, LANES), F32)
        dx = jnp.zeros((s_len, LANES), F32)
        for j in range(CONV_WIDTH):
            dx = dx + l_s[pl.ds(CONV_WIDTH - 1 - j, s_len), :] * cw_ref[pl.ds(j, 1), :]
        dx_ref[0] = dx.astype(BF16)

    slab = lambda off: pl.BlockSpec((1, s_len, LANES), functools.partial(lambda o, c, b: (b, 0, o + c), off))
    vec = pl.BlockSpec((1, LANES), lambda c, b: (0, c))
    mat = pl.BlockSpec((1, LANES, LANES), lambda c, b: (c, 0, 0))
    cw = pl.BlockSpec((CONV_WIDTH, LANES), lambda c, b: (0, c))
    out = pl.BlockSpec((1, s_len, LANES), lambda c, b: (b, 0, c))
    vshape = jax.ShapeDtypeStruct((1, d), F32)
    mshape = jax.ShapeDtypeStruct((ncb, LANES, LANES), F32)
    return call_with_exchange(
        body, carried,
        out_shape=(jax.ShapeDtypeStruct((bsz, s_len, d), BF16),) * 2
        + (jax.ShapeDtypeStruct((CONV_WIDTH, d), F32), vshape, vshape, vshape, vshape, mshape, mshape),
        grid=(ncb, bsz),
        in_specs=[out, slab(0), slab(ncb), out, cw, vec, mat, mat, mat, mat, vec, vec, vec],
        out_specs=(out, out, cw, vec, vec, vec, vec, mat, mat),
        scratch_shapes=[pltpu.VMEM((s_len + 2 * PAD, LANES), F32), pltpu.VMEM((s_len, LANES), F32), pltpu.VMEM((s_len, LANES), F32),
                        pltpu.VMEM((s_len + 2 * PAD, LANES), F32)],
        name=name,
        args=(dy, proj_f, proj_f, h, conv_w, conv_b.reshape(1, d), wr_bd, wi_bd, wr_bd_t, wi_bd_t,
              b_rg.reshape(1, d), b_ig.reshape(1, d), lam.reshape(1, d)))


def _kv_place(head, n_kv_heads):
    kv = head // (N_HEADS // n_kv_heads)
    return kv // 2, kv % 2


def _band_mask(n, single):
    if single:
        qi = lax.broadcasted_iota(jnp.int32, (ATT_BLOCK, ATT_BLOCK), 0)
        return qi >= lax.broadcasted_iota(jnp.int32, (ATT_BLOCK, ATT_BLOCK), 1)
    qi = lax.broadcasted_iota(jnp.int32, (ATT_BLOCK, 2 * ATT_BLOCK), 0)
    kj = lax.broadcasted_iota(jnp.int32, (ATT_BLOCK, 2 * ATT_BLOCK), 1)
    rel = qi + ATT_BLOCK - kj
    return (rel >= 0) & (rel <= ATT_BLOCK) & ((n > 0) | (kj >= ATT_BLOCK))


def _half_masks(dtype):
    lane = lax.broadcasted_iota(jnp.int32, (1, LANES), 1)
    return [(lane < HEAD_DIM).astype(dtype), (lane >= HEAD_DIM).astype(dtype)]


NT = (((1,), (1,)), ((), ()))
TN = (((0,), (0,)), ((), ()))


def _qkv_specs(dil, q_blk, k_blk, v_blk, ckv, clamp):
    qw = D_MODEL // LANES * LANES
    return [
        pl.BlockSpec((1, ATT_BLOCK, qw), lambda b, j, n: (b, clamp(n), j * (q_blk[1]) + q_blk[0])),
        pl.BlockSpec((1, ATT_BLOCK, ckv), lambda b, j, n: (b, jnp.maximum(clamp(n) - 1, 0), j * k_blk[1] + k_blk[0])),
        pl.BlockSpec((1, ATT_BLOCK, ckv), lambda b, j, n: (b, clamp(n), j * k_blk[1] + k_blk[0])),
        pl.BlockSpec((1, ATT_BLOCK, ckv), lambda b, j, n: (b, jnp.maximum(clamp(n) - 1, 0), j * v_blk[1] + v_blk[0])),
        pl.BlockSpec((1, ATT_BLOCK, ckv), lambda b, j, n: (b, clamp(n), j * v_blk[1] + v_blk[0])),
    ]


def attn_fwd(qkv, *, dil, n_kv_heads, sinks, name, emit_bf16=False):
    bsz, s_len, width = qkv.shape
    ckv = n_kv_heads * HEAD_DIM
    l_sub = s_len // dil
    nb = l_sub // ATT_BLOCK
    view = qkv.reshape(bsz, l_sub, dil * width)
    scale = HEAD_DIM ** -0.5
    q_blk = (0, width // D_MODEL)
    k_blk = (D_MODEL // ckv, width // ckv)
    v_blk = (D_MODEL // ckv + 1, width // ckv)
    assert (dil == 1 or width % D_MODEL == 0) and width % ckv == 0 and D_MODEL % ckv == 0

    single = nb == 1

    def body(*refs):
        refs = list(refs)
        sink_ref = refs.pop(0) if sinks is not None else None
        ob_ref = refs.pop() if emit_bf16 else None
        q_ref, kp_ref, kc_ref, vp_ref, vc_ref, o_ref, lse_ref = refs
        n = pl.program_id(2)
        mask = _band_mask(n, single)
        hm = _half_masks(BF16)
        hmf = _half_masks(F32)
        kk = kc_ref[0] if single else jnp.concatenate([kp_ref[0], kc_ref[0]], axis=0)
        vv = vc_ref[0] if single else jnp.concatenate([vp_ref[0], vc_ref[0]], axis=0)
        for hp in range(N_HEADS // 2):
            q2 = q_ref[0, :, hp * LANES:(hp + 1) * LANES]
            o2 = jnp.zeros((ATT_BLOCK, LANES), F32)
            l2 = jnp.zeros((ATT_BLOCK, LANES), F32)
            for a in range(2):
                kb, kh = _kv_place(2 * hp + a, n_kv_heads)
                k2 = kk[:, kb * LANES:(kb + 1) * LANES]
                v2 = vv[:, kb * LANES:(kb + 1) * LANES]
                if kh != a:
                    k2 = pltpu.roll(k2, HEAD_DIM, 1)
                    v2 = pltpu.roll(v2, HEAD_DIM, 1)
                s = lax.dot_general(q2 * hm[a], k2, NT, preferred_element_type=F32) * scale
                s = jnp.where(mask, s, NEG_INF)
                m = jnp.max(s, axis=-1, keepdims=True)
                if sink_ref is not None:
                    sk = sink_ref[2 * hp + a]
                    m = jnp.maximum(m, sk)
                p = jnp.exp(s - m)
                den = jnp.sum(p, axis=-1, keepdims=True)
                if sink_ref is not None:
                    den = den + jnp.exp(sk - m)
                o2 = o2 + jnp.dot(p.astype(BF16), v2 * hm[a], preferred_element_type=F32) / den
                l2 = l2 + (m + jnp.log(den)) * hmf[a]
            o_ref[0, :, hp * LANES:(hp + 1) * LANES] = o2
            lse_ref[0, :, hp * LANES:(hp + 1) * LANES] = l2
            if ob_ref is not None:
                ob_ref[0, :, hp * LANES:(hp + 1) * LANES] = o2.astype(BF16)

    in_specs = _qkv_specs(dil, q_blk, k_blk, v_blk, ckv, lambda n: n)
    args = [view] * 5
    if sinks is not None:
        in_specs = [pl.BlockSpec(memory_space=pltpu.SMEM)] + in_specs
        args = [sinks] + args
    out = pl.BlockSpec((1, ATT_BLOCK, D_MODEL), lambda b, j, n: (b, n, j))
    res = pl.pallas_call(
        body,
        out_shape=(jax.ShapeDtypeStruct((bsz, l_sub, dil * D_MODEL), F32),) * 2
        + ((jax.ShapeDtypeStruct((bsz, l_sub, dil * D_MODEL), BF16),) if emit_bf16 else ()),
        grid=(bsz, dil, nb),
        in_specs=in_specs,
        out_specs=(out,) * (3 if emit_bf16 else 2),
        compiler_params=_params(("parallel", "parallel", "arbitrary")),
        name=name,
    )(*args)
    return tuple(t.reshape(bsz, s_len, D_MODEL) for t in res)


def attn_bwd(qkv, o, lse, do, acc, *, dil, n_kv_heads, name):
    bsz, s_len, width = qkv.shape
    ckv = n_kv_heads * HEAD_DIM
    l_sub = s_len // dil
    nb = l_sub // ATT_BLOCK
    view = qkv.reshape(bsz, l_sub, dil * width)
    scale = HEAD_DIM ** -0.5
    q_blk = (0, width // D_MODEL)
    k_blk = (D_MODEL // ckv, width // ckv)
    v_blk = (D_MODEL // ckv + 1, width // ckv)
    single = nb == 1

    def body(*refs):
        if acc is None:
            q_ref, kp_ref, kc_ref, vp_ref, vc_ref, o_ref, lse_ref, do_ref, dq_ref, dk_ref, dv_ref, dkk, dvv, ck, cv = refs
            aq_ref = ak_ref = av_ref = None
        else:
            (q_ref, kp_ref, kc_ref, vp_ref, vc_ref, o_ref, lse_ref, do_ref, aq_ref, ak_ref, av_ref,
             dq_ref, dk_ref, dv_ref, dkk, dvv, ck, cv) = refs
        n = pl.program_id(2)

        @pl.when(n < nb)
        def _():
            mask = _band_mask(n, single)
            hm = _half_masks(BF16)
            hmf = _half_masks(F32)
            kk = kc_ref[0] if single else jnp.concatenate([kp_ref[0], kc_ref[0]], axis=0)
            vv = vc_ref[0] if single else jnp.concatenate([vp_ref[0], vc_ref[0]], axis=0)
            krows = pl.ds(ATT_BLOCK, ATT_BLOCK) if single else pl.ds(0, 2 * ATT_BLOCK)
            dkk[...] = jnp.zeros_like(dkk)
            dvv[...] = jnp.zeros_like(dvv)
            for hp in range(N_HEADS // 2):
                cols = slice(hp * LANES, (hp + 1) * LANES)
                q2 = q_ref[0, :, cols]
                do2f = do_ref[0, :, cols]
                do2 = do2f.astype(BF16)
                dd2 = do2f * o_ref[0, :, cols]
                l2 = lse_ref[0, :, cols]
                dq2 = jnp.zeros((ATT_BLOCK, LANES), F32)
                for a in range(2):
                    kb, kh = _kv_place(2 * hp + a, n_kv_heads)
                    kcols = slice(kb * LANES, (kb + 1) * LANES)
                    k2 = kk[:, kcols]
                    v2 = vv[:, kcols]
                    if kh != a:
                        k2 = pltpu.roll(k2, HEAD_DIM, 1)
                        v2 = pltpu.roll(v2, HEAD_DIM, 1)
                    qm = q2 * hm[a]
                    dom = do2 * hm[a]
                    dsum = jnp.sum(dd2 * hmf[a], axis=-1, keepdims=True)
                    lse_h = jnp.max(jnp.where(hmf[a] > 0.5, l2, NEG_INF), axis=-1, keepdims=True)
                    s = lax.dot_general(qm, k2, NT, preferred_element_type=F32) * scale
                    s = jnp.where(mask, s, NEG_INF)
                    p = jnp.exp(s - lse_h)
                    dp = lax.dot_general(dom, v2, NT, preferred_element_type=F32)
                    ds = (p * (dp - dsum) * scale).astype(BF16)
                    dq2 = dq2 + jnp.dot(ds, k2 * hm[a], preferred_element_type=F32)
                    dk_c = lax.dot_general(ds, qm, TN, preferred_element_type=F32)
                    dv_c = lax.dot_general(p.astype(BF16), dom, TN, preferred_element_type=F32)
                    if kh != a:
                        dk_c = pltpu.roll(dk_c, HEAD_DIM, 1)
                        dv_c = pltpu.roll(dv_c, HEAD_DIM, 1)
                    dkk[krows, kcols] += dk_c
                    dvv[krows, kcols] += dv_c
                if aq_ref is not None:
                    dq2 = dq2 + aq_ref[0, :, cols]
                dq_ref[0, :, cols] = dq2

        @pl.when((n >= 1) & (n < nb))
        def _():
            dk_ref[0] = ck[...] + dkk[pl.ds(0, ATT_BLOCK), :] + (0.0 if ak_ref is None else ak_ref[0])
            dv_ref[0] = cv[...] + dvv[pl.ds(0, ATT_BLOCK), :] + (0.0 if av_ref is None else av_ref[0])

        @pl.when(n == nb)
        def _():
            dk_ref[0] = ck[...] + (0.0 if ak_ref is None else ak_ref[0])
            dv_ref[0] = cv[...] + (0.0 if av_ref is None else av_ref[0])

        @pl.when(n < nb)
        def _():
            ck[...] = dkk[pl.ds(ATT_BLOCK, ATT_BLOCK), :]
            cv[...] = dvv[pl.ds(ATT_BLOCK, ATT_BLOCK), :]

    clamp = lambda n: jnp.minimum(n, nb - 1)
    prev = lambda n: jnp.maximum(n - 1, 0)
    row = pl.BlockSpec((1, ATT_BLOCK, D_MODEL), lambda b, j, n: (b, clamp(n), j))
    kv_out = pl.BlockSpec((1, ATT_BLOCK, ckv), lambda b, j, n: (b, prev(n), j))
    in_specs = _qkv_specs(dil, q_blk, k_blk, v_blk, ckv, clamp) + [row, row, row]
    rs = lambda t: t.reshape(bsz, l_sub, dil * t.shape[-1])
    args = [view] * 5 + [rs(o), rs(lse), rs(do)]
    if acc is not None:
        in_specs += [row, kv_out, kv_out]
        args += [rs(t) for t in acc]
    dq, dk, dv = pl.pallas_call(
        body,
        out_shape=(jax.ShapeDtypeStruct((bsz, l_sub, dil * D_MODEL), F32),
                   jax.ShapeDtypeStruct((bsz, l_sub, dil * ckv), F32), jax.ShapeDtypeStruct((bsz, l_sub, dil * ckv), F32)),
        grid=(bsz, dil, nb + 1),
        in_specs=in_specs,
        out_specs=(row, kv_out, kv_out),
        scratch_shapes=[pltpu.VMEM((2 * ATT_BLOCK, ckv), F32), pltpu.VMEM((2 * ATT_BLOCK, ckv), F32),
                        pltpu.VMEM((ATT_BLOCK, ckv), F32), pltpu.VMEM((ATT_BLOCK, ckv), F32)],
        compiler_params=_params(("parallel", "parallel", "arbitrary")),
        name=name,
    )(*args)
    return dq.reshape(bsz, s_len, D_MODEL), dk.reshape(bsz, s_len, ckv), dv.reshape(bsz, s_len, ckv)


def dil_combine(os_, lses, *, name):
    t_dim, d = os_[0].shape
    tr = _pick(t_dim, (256, 128, 8))

    def body(o0, o1, o2, l0, l1, l2, y_ref, lt_ref, yb_ref):
        la, lb, lc = l0[...], l1[...], l2[...]
        m = jnp.maximum(jnp.maximum(la, lb), lc)
        ea, eb, ec = jnp.exp(la - m), jnp.exp(lb - m), jnp.exp(lc - m)
        tot = ea + eb + ec
        y = (ea / tot) * o0[...] + (eb / tot) * o1[...] + (ec / tot) * o2[...]
        y_ref[...] = y
        yb_ref[...] = y.astype(BF16)
        lt_ref[...] = m + jnp.log(tot)

    row = pl.BlockSpec((tr, d), lambda i: (i, 0))
    return pl.pallas_call(
        body,
        out_shape=(jax.ShapeDtypeStruct((t_dim, d), F32),) * 2 + (jax.ShapeDtypeStruct((t_dim, d), BF16),),
        grid=(t_dim // tr,),
        in_specs=[row] * 6,
        out_specs=(row, row, row),
        compiler_params=_params(("parallel",)),
        name=name,
    )(*os_, *lses)


ATT_SCALE = HEAD_DIM ** -0.5


def _band_mask(n, single):
    nk = ATT_BLOCK if single else 2 * ATT_BLOCK
    qi = lax.broadcasted_iota(jnp.int32, (2 * ATT_BLOCK, nk), 0) % ATT_BLOCK
    kj = lax.broadcasted_iota(jnp.int32, (2 * ATT_BLOCK, nk), 1)
    if single:
        return qi >= kj
    rel = qi + ATT_BLOCK - kj
    return (rel >= 0) & (rel <= ATT_BLOCK) & ((n > 0) | (kj >= ATT_BLOCK))


def _lane_halves():
    lane = lax.broadcasted_iota(jnp.int32, (1, LANES), 1)
    return lane < HEAD_DIM


def _stack_heads(t2, kh):
    first = _lane_halves()
    parts = []
    for a in range(2):
        ta = jnp.where(first if a == 0 else ~first, t2, jnp.zeros_like(t2))
        if a != kh[a]:
            ta = pltpu.roll(ta, HEAD_DIM, 1)
        parts.append(ta)
    return jnp.concatenate(parts, axis=0)


def _fold_heads(t, kh):
    t0, t1 = t[:ATT_BLOCK], t[ATT_BLOCK:]
    if t.shape[1] == LANES:
        if kh[0] != 0:
            t0 = pltpu.roll(t0, HEAD_DIM, 1)
        if kh[1] != 1:
            t1 = pltpu.roll(t1, HEAD_DIM, 1)
    return jnp.where(_lane_halves(), t0, t1)


def _rows_of_heads(t2):
    return jnp.concatenate([t2[:, 0:1], t2[:, HEAD_DIM:HEAD_DIM + 1]], axis=0)


PAIRS_AT_ONCE = 4


def _fill_bias(bias2_ref, bias1_ref=None):
    for i in range(2):
        bias2_ref[i] = jnp.where(_band_mask(i, False), 0.0, NEG_INF)
    if bias1_ref is not None:
        bias1_ref[...] = jnp.where(_band_mask(0, True), 0.0, NEG_INF)


def _pairs_fwd(items):
    ss = [lax.dot_general(_stack_heads(q2 * ATT_SCALE, kh), kk, NT, preferred_element_type=F32) + bias
          for q2, kk, _, bias, kh, _ in items]
    ps, ms, ls = [], [], []
    for s, (_, _, _, _, _, sink_col) in zip(ss, items):
        m = jnp.max(s, axis=-1, keepdims=True)
        if sink_col is not None:
            m = jnp.maximum(m, sink_col)
        p = jnp.exp(s - m)
        l = jnp.sum(p, axis=-1, keepdims=True)
        if sink_col is not None:
            l = l + jnp.exp(sink_col - m)
        ps.append(p.astype(BF16))
        ms.append(m)
        ls.append(l)
    pvs = [jnp.dot(p, it[2], preferred_element_type=F32) for p, it in zip(ps, items)]
    return list(zip(pvs, ms, ls))


def _pairs_bwd(items):
    first = _lane_halves()
    pre = []
    for q2, kk, vv, do2, o2, lse2, bias, kh in items:
        dd = do2 * o2
        dsum = jnp.concatenate([jnp.sum(jnp.where(first, dd, 0.0), axis=-1, keepdims=True),
                                jnp.sum(jnp.where(first, 0.0, dd), axis=-1, keepdims=True)], axis=0)
        qs = _stack_heads(q2 * ATT_SCALE, kh)
        dos = _stack_heads(do2.astype(BF16), kh)
        s = lax.dot_general(qs, kk, NT, preferred_element_type=F32) + bias
        dp = lax.dot_general(dos, vv, NT, preferred_element_type=F32)
        pre.append((qs, dos, s, dp, dsum))
    mid = []
    for (qs, dos, s, dp, dsum), it in zip(pre, items):
        p = jnp.exp(s - _rows_of_heads(it[5]))
        mid.append((p.astype(BF16), (p * (dp - dsum)).astype(BF16)))
    out = []
    for (pb, ds), (qs, dos, _, _, dsum), it in zip(mid, pre, items):
        dq = _fold_heads(jnp.dot(ds, it[1], preferred_element_type=F32), it[7]) * ATT_SCALE
        dk = lax.dot_general(ds, qs, TN, preferred_element_type=F32)
        dv = lax.dot_general(pb, dos, TN, preferred_element_type=F32)
        out.append((dq, dk, dv, dsum))
    return out


def swa_fwd(qkv, sinks, *, name, carried=None):
    bsz, s_len, width = qkv.shape
    ckv = SWA_KV_HEADS * HEAD_DIM
    nb = s_len // ATT_BLOCK
    kblk = D_MODEL // ckv

    def body(sink_ref, q_ref, kp_ref, kc_ref, vp_ref, vc_ref, o_ref, lse_ref, ob_ref, bias2):
        n = pl.program_id(1)
        _fill_bias(bias2)
        bias = bias2[jnp.minimum(n, 1)]
        kk = jnp.concatenate([kp_ref[0], kc_ref[0]], axis=0)
        vv = jnp.concatenate([vp_ref[0], vc_ref[0]], axis=0)
        top = lax.broadcasted_iota(jnp.int32, (2 * ATT_BLOCK, 1), 0) < ATT_BLOCK
        for hp0 in range(0, N_HEADS // 2, PAIRS_AT_ONCE):
            items, places = [], []
            for hp in range(hp0, hp0 + PAIRS_AT_ONCE):
                cols = slice(hp * LANES, (hp + 1) * LANES)
                kb, kh = _kv_place(2 * hp, SWA_KV_HEADS)
                kcols = slice(kb * LANES, (kb + 1) * LANES)
                sink_col = jnp.where(top, sink_ref[2 * hp], sink_ref[2 * hp + 1])
                items.append((q_ref[0, :, cols], kk[:, kcols], vv[:, kcols], bias, (kh, kh), sink_col))
                places.append((cols, (kh, kh)))
            for (pv, m, l), (cols, kh2) in zip(_pairs_fwd(items), places):
                o2 = _fold_heads(pv / l, kh2)
                o_ref[0, :, cols] = o2
                ob_ref[0, :, cols] = o2.astype(BF16)
                lse_ref[0, :, cols] = _fold_heads(m + jnp.log(l), kh2)

    prev = lambda n: jnp.maximum(n - 1, 0)
    out = pl.BlockSpec((1, ATT_BLOCK, D_MODEL), lambda b, n: (b, n, 0))
    sd = lambda dt: jax.ShapeDtypeStruct((bsz, s_len, D_MODEL), dt)
    return call_with_exchange(
        body, carried,
        out_shape=(sd(F32), sd(F32), sd(BF16)),
        grid=(bsz, nb),
        in_specs=[pl.BlockSpec(memory_space=pltpu.SMEM), out,
                  pl.BlockSpec((1, ATT_BLOCK, ckv), lambda b, n: (b, prev(n), kblk)),
                  pl.BlockSpec((1, ATT_BLOCK, ckv), lambda b, n: (b, n, kblk)),
                  pl.BlockSpec((1, ATT_BLOCK, ckv), lambda b, n: (b, prev(n), kblk + 1)),
                  pl.BlockSpec((1, ATT_BLOCK, ckv), lambda b, n: (b, n, kblk + 1))],
        out_specs=(out, out, out),
        scratch_shapes=[pltpu.VMEM((2, 2 * ATT_BLOCK, 2 * ATT_BLOCK), F32)],
        name=name,
        args=(sinks, qkv, qkv, qkv, qkv, qkv))


def swa_bwd(qkv, sinks, o, lse, do, *, name, carried=None):
    bsz, s_len, width = qkv.shape
    ckv = SWA_KV_HEADS * HEAD_DIM
    nb = s_len // ATT_BLOCK
    kblk = D_MODEL // ckv

    def body(sink_ref, q_ref, kp_ref, kc_ref, vp_ref, vc_ref, o_ref, lse_ref, do_ref, dq_ref, dk_ref, dv_ref, dsink_ref,
             dkk, dvv, ck, cv, bias2):
        n = pl.program_id(1)

        @pl.when((n == 0) & (pl.program_id(0) == 0))
        def _():
            dsink_ref[...] = jnp.zeros_like(dsink_ref)

        @pl.when(n < nb)
        def _():
            top = lax.broadcasted_iota(jnp.int32, (2 * ATT_BLOCK, 1), 0) < ATT_BLOCK
            lane = lax.broadcasted_iota(jnp.int32, dsink_ref.shape, 1)
            first_row = lax.broadcasted_iota(jnp.int32, dsink_ref.shape, 0) == 0
            _fill_bias(bias2)
            bias = bias2[jnp.minimum(n, 1)]
            kk = jnp.concatenate([kp_ref[0], kc_ref[0]], axis=0)
            vv = jnp.concatenate([vp_ref[0], vc_ref[0]], axis=0)
            dkk[...] = jnp.zeros_like(dkk)
            dvv[...] = jnp.zeros_like(dvv)
            for hp0 in range(0, N_HEADS // 2, PAIRS_AT_ONCE):
                items, places = [], []
                for hp in range(hp0, hp0 + PAIRS_AT_ONCE):
                    cols = slice(hp * LANES, (hp + 1) * LANES)
                    kb, kh = _kv_place(2 * hp, SWA_KV_HEADS)
                    kcols = slice(kb * LANES, (kb + 1) * LANES)
                    items.append((q_ref[0, :, cols], kk[:, kcols], vv[:, kcols], do_ref[0, :, cols], o_ref[0, :, cols],
                                  lse_ref[0, :, cols], bias, (kh, kh)))
                    places.append((cols, kcols, hp))
                for (dq, dk, dv, dsum), (cols, kcols, hp) in zip(_pairs_bwd(items), places):
                    dq_ref[0, :, cols] = dq
                    dkk[:, kcols] += dk
                    dvv[:, kcols] += dv
                    sink_col = jnp.where(top, sink_ref[2 * hp], sink_ref[2 * hp + 1])
                    t = -jnp.exp(sink_col - _rows_of_heads(lse_ref[0, :, cols])) * dsum
                    d0 = jnp.sum(t[:ATT_BLOCK], axis=0, keepdims=True)
                    d1 = jnp.sum(t[ATT_BLOCK:], axis=0, keepdims=True)
                    dsink_ref[...] += jnp.where(first_row & (lane == 2 * hp), d0, 0.0) + jnp.where(first_row & (lane == 2 * hp + 1), d1, 0.0)

        @pl.when((n >= 1) & (n < nb))
        def _():
            dk_ref[0] = ck[...] + dkk[pl.ds(0, ATT_BLOCK), :]
            dv_ref[0] = cv[...] + dvv[pl.ds(0, ATT_BLOCK), :]

        @pl.when(n == nb)
        def _():
            dk_ref[0] = ck[...]
            dv_ref[0] = cv[...]

        @pl.when(n < nb)
        def _():
            ck[...] = dkk[pl.ds(ATT_BLOCK, ATT_BLOCK), :]
            cv[...] = dvv[pl.ds(ATT_BLOCK, ATT_BLOCK), :]

    clamp = lambda n: jnp.minimum(n, nb - 1)
    prev = lambda n: jnp.maximum(n - 1, 0)
    row = pl.BlockSpec((1, ATT_BLOCK, D_MODEL), lambda b, n: (b, clamp(n), 0))
    kv_out = pl.BlockSpec((1, ATT_BLOCK, ckv), lambda b, n: (b, prev(n), 0))
    return call_with_exchange(
        body, carried,
        out_shape=(jax.ShapeDtypeStruct((bsz, s_len, D_MODEL), F32), jax.ShapeDtypeStruct((bsz, s_len, ckv), F32),
                   jax.ShapeDtypeStruct((bsz, s_len, ckv), F32), jax.ShapeDtypeStruct((8, LANES), F32)),
        grid=(bsz, nb + 1),
        in_specs=[pl.BlockSpec(memory_space=pltpu.SMEM), row,
                  pl.BlockSpec((1, ATT_BLOCK, ckv), lambda b, n: (b, prev(clamp(n)), kblk)),
                  pl.BlockSpec((1, ATT_BLOCK, ckv), lambda b, n: (b, clamp(n), kblk)),
                  pl.BlockSpec((1, ATT_BLOCK, ckv), lambda b, n: (b, prev(clamp(n)), kblk + 1)),
                  pl.BlockSpec((1, ATT_BLOCK, ckv), lambda b, n: (b, clamp(n), kblk + 1)),
                  row, row, row],
        out_specs=(row, kv_out, kv_out, pl.BlockSpec((8, LANES), lambda b, n: (0, 0))),
        scratch_shapes=[pltpu.VMEM((2 * ATT_BLOCK, ckv), F32), pltpu.VMEM((2 * ATT_BLOCK, ckv), F32),
                        pltpu.VMEM((ATT_BLOCK, ckv), F32), pltpu.VMEM((ATT_BLOCK, ckv), F32),
                        pltpu.VMEM((2, 2 * ATT_BLOCK, 2 * ATT_BLOCK), F32)],
        name=name,
        args=(sinks, qkv, qkv, qkv, qkv, qkv, o, lse, do))


DIL_PATTERNS = tuple((d, 2048 // d // ATT_BLOCK) for d in DILATIONS)
MHA = (0, 1)


def _dil_rows(idx, d, nb):
    j = idx // nb
    n = idx % nb
    base = j + n * (ATT_BLOCK * d)
    prev = jnp.maximum(base - ATT_BLOCK * d, j)
    if d == 1:
        return n, pl.ds(pl.multiple_of(base, ATT_BLOCK), ATT_BLOCK), pl.ds(pl.multiple_of(prev, ATT_BLOCK), ATT_BLOCK)
    return n, pl.ds(base, ATT_BLOCK, stride=d), pl.ds(prev, ATT_BLOCK, stride=d)


def dil_fwd(qkv, *, name, carried=None):
    bsz, s_len, _ = qkv.shape
    assert s_len == DIL_PATTERNS[0][0] * DIL_PATTERNS[0][1] * ATT_BLOCK
    npair = N_HEADS // 2

    def body(q_ref, k_ref, v_ref, y_ref, lse_ref, yb_ref, m_acc, l_acc, bias2, bias1):
        _fill_bias(bias2, bias1)
        for ci, (d, nb) in enumerate(DIL_PATTERNS):
            single = nb == 1

            def blocks(it, carry):
                items, places = [], []
                for u in range(PAIRS_AT_ONCE):
                    n, rows, prows = _dil_rows(it * PAIRS_AT_ONCE + u, d, nb)
                    kc = k_ref[rows, :].astype(BF16)
                    vc = v_ref[rows, :].astype(BF16)
                    if single:
                        kk, vv, bias = kc, vc, bias1[...]
                    else:
                        kk = jnp.concatenate([k_ref[prows, :].astype(BF16), kc], axis=0)
                        vv = jnp.concatenate([v_ref[prows, :].astype(BF16), vc], axis=0)
                        bias = bias2[jnp.minimum(n, 1)]
                    items.append((q_ref[rows, :].astype(BF16), kk, vv, bias, MHA, None))
                    places.append(rows)
                for (pv, m, l), rows in zip(_pairs_fwd(items), places):
                    o2, m2, l2 = _fold_heads(pv, MHA), _fold_heads(m, MHA), _fold_heads(l, MHA)
                    if ci == 0:
                        y_ref[rows, :] = o2
                        m_acc[rows, :] = m2
                        l_acc[rows, :] = l2
                    else:
                        m_old = m_acc[rows, :]
                        m_new = jnp.maximum(m_old, m2)
                        w_old = jnp.exp(m_old - m_new)
                        w_new = jnp.exp(m2 - m_new)
                        y_ref[rows, :] = y_ref[rows, :] * w_old + o2 * w_new
                        l_acc[rows, :] = l_acc[rows, :] * w_old + l2 * w_new
                        m_acc[rows, :] = m_new
                return carry

            lax.fori_loop(0, d * nb // PAIRS_AT_ONCE, blocks, 0)
        y = y_ref[...] / l_acc[...]
        y_ref[...] = y
        yb_ref[...] = y.astype(BF16)
        lse_ref[...] = m_acc[...] + jnp.log(l_acc[...])

    slab = lambda off: pl.BlockSpec((None, s_len, LANES), functools.partial(lambda o, b, h: (b, 0, o + h), off))
    sd = lambda dt: jax.ShapeDtypeStruct((bsz, s_len, D_MODEL), dt)
    return call_with_exchange(
        body, carried,
        out_shape=(sd(F32), sd(F32), sd(BF16)),
        grid=(bsz, npair),
        in_specs=[slab(0), slab(npair), slab(2 * npair)],
        out_specs=(slab(0), slab(0), slab(0)),
        scratch_shapes=[pltpu.VMEM((s_len, LANES), F32), pltpu.VMEM((s_len, LANES), F32),
                        pltpu.VMEM((2, 2 * ATT_BLOCK, 2 * ATT_BLOCK), F32), pltpu.VMEM((2 * ATT_BLOCK, ATT_BLOCK), F32)],
        name=name,
        args=(qkv, qkv, qkv))


def dil_bwd(qkv, y, lse, dy, *, name, carried=None):
    bsz, s_len, _ = qkv.shape
    npair = N_HEADS // 2

    def body(q_ref, k_ref, v_ref, y_ref, lse_ref, dy_ref, dq_ref, dk_ref, dv_ref, bias2, bias1):
        _fill_bias(bias2, bias1)
        dq_ref[...] = jnp.zeros_like(dq_ref)
        dk_ref[...] = jnp.zeros_like(dk_ref)
        dv_ref[...] = jnp.zeros_like(dv_ref)
        for d, nb in DIL_PATTERNS:
            single = nb == 1

            def blocks(it, carry):
                items, places = [], []
                for u in range(PAIRS_AT_ONCE):
                    n, rows, prows = _dil_rows(it * PAIRS_AT_ONCE + u, d, nb)
                    kc = k_ref[rows, :].astype(BF16)
                    vc = v_ref[rows, :].astype(BF16)
                    if single:
                        kk, vv, bias = kc, vc, bias1[...]
                    else:
                        kk = jnp.concatenate([k_ref[prows, :].astype(BF16), kc], axis=0)
                        vv = jnp.concatenate([v_ref[prows, :].astype(BF16), vc], axis=0)
                        bias = bias2[jnp.minimum(n, 1)]
                    items.append((q_ref[rows, :].astype(BF16), kk, vv, dy_ref[rows, :], y_ref[rows, :], lse_ref[rows, :], bias, MHA))
                    places.append((rows, prows))
                for (dq, dk, dv, _), (rows, prows) in zip(_pairs_bwd(items), places):
                    dq_ref[rows, :] += dq
                    if single:
                        dk_ref[rows, :] += dk
                        dv_ref[rows, :] += dv
                    else:
                        dk_ref[prows, :] += dk[:ATT_BLOCK]
                        dv_ref[prows, :] += dv[:ATT_BLOCK]
                        dk_ref[rows, :] += dk[ATT_BLOCK:]
                        dv_ref[rows, :] += dv[ATT_BLOCK:]
                return carry

            lax.fori_loop(0, d * nb // PAIRS_AT_ONCE, blocks, 0)

    slab = lambda off: pl.BlockSpec((None, s_len, LANES), functools.partial(lambda o, b, h: (b, 0, o + h), off))
    sd = jax.ShapeDtypeStruct((bsz, s_len, D_MODEL), F32)
    return call_with_exchange(
        body, carried,
        out_shape=(sd, sd, sd),
        grid=(bsz, npair),
        in_specs=[slab(0), slab(npair), slab(2 * npair), slab(0), slab(0), slab(0)],
        out_specs=(slab(0), slab(0), slab(0)),
        scratch_shapes=[pltpu.VMEM((2, 2 * ATT_BLOCK, 2 * ATT_BLOCK), F32), pltpu.VMEM((2 * ATT_BLOCK, ATT_BLOCK), F32)],
        name=name,
        args=(qkv, qkv, qkv, y, lse, dy))


def sink_grad(do, o, lse, sink_lanes, *, name):
    t_dim, d = do.shape
    tr = _pick(t_dim, (256, 128, 8))

    def body(do_ref, o_ref, l_ref, s_ref, out_ref):
        @pl.when(pl.program_id(0) == 0)
        def _():
            out_ref[...] = jnp.zeros_like(out_ref)

        out_ref[...] += jnp.sum(-jnp.exp(s_ref[...] - l_ref[...]) * do_ref[...] * o_ref[...], axis=0, keepdims=True)

    row = pl.BlockSpec((tr, d), lambda i: (i, 0))
    vec = pl.BlockSpec((1, d), lambda i: (0, 0))
    return pl.pallas_call(
        body,
        out_shape=jax.ShapeDtypeStruct((1, d), F32),
        grid=(t_dim // tr,),
        in_specs=[row, row, row, vec],
        out_specs=vec,
        compiler_params=_params(("arbitrary",)),
        name=name,
    )(do, o, lse, sink_lanes)


def adamw(w, g, m, v, *, name):
    rows, cols = w.shape
    tr = _pick(rows, (256, 128, 64, 32, 16, 8))

    def body(w_ref, g_ref, m_ref, v_ref, d_ref, nm_ref, nv_ref):
        gv = g_ref[...]
        nm = ADAM_B1 * m_ref[...] + (1.0 - ADAM_B1) * gv
        nv = ADAM_B2 * v_ref[...] + (1.0 - ADAM_B2) * (gv * gv)
        m_hat = nm / (1.0 - ADAM_B1 ** ADAM_STEP)
        v_hat = nv / (1.0 - ADAM_B2 ** ADAM_STEP)
        d_ref[...] = -ADAM_LR * (m_hat / (jnp.sqrt(v_hat) + ADAM_EPS) + ADAM_WD * w_ref[...])
        nm_ref[...] = nm
        nv_ref[...] = nv

    row = pl.BlockSpec((tr, cols), lambda i: (i, 0))
    return pl.pallas_call(
        body,
        out_shape=(jax.ShapeDtypeStruct((rows, cols), F32),) * 3,
        grid=(rows // tr,),
        in_specs=[row] * 4,
        out_specs=(row, row, row),
        compiler_params=_params(("parallel",)),
        name=name,
    )(w, g, m, v)


def _place():
    return lax.axis_index("x"), lax.axis_index("y"), lax.axis_index("c")


def _gather_copies(x_ref, out_ref, send_sems, recv_sems):
    x, y, c = _place()
    me, sibling = (x, y, c), (x, y, 1 - c)
    chips = [(1 - x, y), (x, 1 - y), (1 - x, 1 - y)]

    def slot(px, py, pc):
        return out_ref.at[4 * px + 2 * py + pc]

    def copy(k, block, to, src=None):
        return pltpu.make_async_remote_copy(
            src_ref=slot(*block) if src is None else src, dst_ref=slot(*block),
            send_sem=send_sems.at[k], recv_sem=recv_sems.at[k], device_id=to, device_id_type=MESH)

    first = [lambda: copy(0, me, sibling, src=x_ref)] + [functools.partial(copy, 1 + j, me, (*chip, c), src=x_ref)
                                                         for j, chip in enumerate(chips)]
    passed = [functools.partial(copy, 4 + j, (*chip, c), sibling) for j, chip in enumerate(chips)]
    landing = [functools.partial(copy, 1 + j, (*chip, c), me) for j, chip in enumerate(chips)]
    from_sibling = [lambda: copy(0, sibling, me)] + [functools.partial(copy, 4 + j, (*chip, 1 - c), me) for j, chip in enumerate(chips)]
    return slot(*me), first, passed, landing, from_sibling


def _gather_start(x_ref, out_ref, send_sems, recv_sems, local_sem):
    mine, first, _, _, _ = _gather_copies(x_ref, out_ref, send_sems, recv_sems)
    pltpu.make_async_copy(x_ref, mine, local_sem).start()
    for cp in first:
        cp().start()


def _gather_finish(x_ref, out_ref, send_sems, recv_sems, local_sem):
    mine, first, passed, landing, from_sibling = _gather_copies(x_ref, out_ref, send_sems, recv_sems)
    for cp, fwd in zip(landing, passed):
        cp().wait_recv()
        fwd().start()
    for cp in from_sibling:
        cp().wait_recv()
    for cp in first + passed:
        cp().wait_send()
    pltpu.make_async_copy(x_ref, mine, local_sem).wait()


def _a2a_copies(x_ref, out_ref, send_sems, recv_sems):
    x, y, c = _place()
    me = 4 * x + 2 * y + c
    copies = []
    for k in range(1, N_DEV):
        px = 1 - x if k & 4 else x
        py = 1 - y if k & 2 else y
        pc = 1 - c if k & 1 else c
        copies.append(pltpu.make_async_remote_copy(
            src_ref=x_ref.at[4 * px + 2 * py + pc], dst_ref=out_ref.at[me], send_sem=send_sems.at[k - 1],
            recv_sem=recv_sems.at[k - 1], device_id=(px, py, pc), device_id_type=MESH))
    return me, copies


def _a2a_start(x_ref, out_ref, send_sems, recv_sems, local_sem):
    me, copies = _a2a_copies(x_ref, out_ref, send_sems, recv_sems)
    pltpu.make_async_copy(x_ref.at[me], out_ref.at[me], local_sem).start()
    for cp in copies:
        cp.start()


def _a2a_finish(x_ref, out_ref, send_sems, recv_sems, local_sem):
    me, copies = _a2a_copies(x_ref, out_ref, send_sems, recv_sems)
    for cp in copies:
        cp.wait_recv()
    for cp in copies:
        cp.wait_send()
    pltpu.make_async_copy(x_ref.at[me], out_ref.at[me], local_sem).wait()


EXCHANGES = {"gather": (_gather_start, _gather_finish, lambda x: (N_DEV,) + x.shape),
             "a2a": (_a2a_start, _a2a_finish, lambda x: x.shape)}
EXCHANGE_SEMS = [pltpu.SemaphoreType.DMA((7,)), pltpu.SemaphoreType.DMA((7,)), pltpu.SemaphoreType.DMA(())]


def exchange(kind, x, *, name):
    start, finish, shape = EXCHANGES[kind]

    def body(x_ref, out_ref, *sems):
        start(x_ref, out_ref, *sems)
        finish(x_ref, out_ref, *sems)

    return pl.pallas_call(
        body,
        out_shape=jax.ShapeDtypeStruct(shape(x), x.dtype),
        in_specs=[pl.BlockSpec(memory_space=pl.ANY)],
        out_specs=pl.BlockSpec(memory_space=pl.ANY),
        scratch_shapes=EXCHANGE_SEMS,
        name=name,
    )(x)


def call_with_exchange(body, carried, *, out_shape, grid, in_specs, out_specs, scratch_shapes, name, args):
    sem = ("arbitrary",) * len(grid)
    carried = list(carried or ())
    if not carried:
        res = pl.pallas_call(body, out_shape=out_shape, grid=grid, in_specs=in_specs, out_specs=out_specs,
                             scratch_shapes=scratch_shapes, compiler_params=_params(sem), name=name)(*args)
        return res, []
    n_in, n_out, n_scr, n_x = len(in_specs), len(out_shape), len(scratch_shapes), len(carried)
    n_sems = len(EXCHANGE_SEMS)

    def wrapped(*refs):
        ins, x_refs = refs[:n_in], refs[n_in:n_in + n_x]
        outs = refs[n_in + n_x:n_in + n_x + n_out]
        out_refs = refs[n_in + n_x + n_out:n_in + 2 * n_x + n_out]
        rest = refs[n_in + 2 * n_x + n_out:]
        scratch, sems = rest[:n_scr], rest[n_scr:]
        ids = [pl.program_id(i) for i in range(len(grid))]
        is_first = functools.reduce(lambda a, b: a & b, [i == 0 for i in ids])
        is_last = functools.reduce(lambda a, b: a & b, [i == g - 1 for i, g in zip(ids, grid)])

        @pl.when(is_first)
        def _():
            for e, (kind, _) in enumerate(carried):
                EXCHANGES[kind][0](x_refs[e], out_refs[e], *sems[e * n_sems:(e + 1) * n_sems])

        body(*ins, *outs, *scratch)

        @pl.when(is_last)
        def _():
            for e, (kind, _) in enumerate(carried):
                EXCHANGES[kind][1](x_refs[e], out_refs[e], *sems[e * n_sems:(e + 1) * n_sems])

    any_spec = pl.BlockSpec(memory_space=pl.ANY)
    res = pl.pallas_call(
        wrapped,
        out_shape=tuple(out_shape) + tuple(jax.ShapeDtypeStruct(EXCHANGES[kind][2](x), x.dtype) for kind, x in carried),
        grid=grid,
        in_specs=list(in_specs) + [any_spec] * n_x,
        out_specs=tuple(out_specs) + (any_spec,) * n_x,
        scratch_shapes=list(scratch_shapes) + EXCHANGE_SEMS * n_x,
        compiler_params=_params(sem),
        name=name + "".join("_" + kind for kind, _ in carried),
    )(*args, *[x for _, x in carried])
    return res[:n_out], list(res[n_out:])


def sum_slots(x, *, name):
    _, rows, cols = x.shape
    tr = _pick(rows, (512, 256, 128, 64, 32, 16))

    def body(x_ref, o_ref):
        acc = x_ref[0].astype(F32)
        for k in range(1, N_DEV):
            acc = acc + x_ref[k].astype(F32)
        o_ref[...] = acc

    return pl.pallas_call(
        body,
        out_shape=jax.ShapeDtypeStruct((rows, cols), F32),
        grid=(rows // tr,),
        in_specs=[pl.BlockSpec((N_DEV, tr, cols), lambda i: (0, i, 0))],
        out_specs=pl.BlockSpec((tr, cols), lambda i: (i, 0)),
        compiler_params=_params(("parallel",)),
        name=name,
    )(x)


BIG = ("w_in", "w_branch", "w_out", "w_ffn_in", "w_ffn_out")
SMALL = ("conv_b", "w_rg", "b_rg", "w_ig", "b_ig", "lru_lambda", "sinks", "ln1_g", "ln1_b", "ln2_g", "ln2_b")
N_LRU_BLOCKS = D_MODEL // HEAD_DIM
SMALL_ROWS_TILE = 512


def _block_diag(w):
    z = jnp.zeros((N_LRU_BLOCKS // 2, HEAD_DIM, HEAD_DIM), w.dtype)
    top = jnp.concatenate([w[0::2], z], axis=2)
    bot = jnp.concatenate([z, w[1::2]], axis=2)
    return jnp.concatenate([top, bot], axis=1)


def _block_diag_grad(g):
    return jnp.stack([g[:, :HEAD_DIM, :HEAD_DIM], g[:, HEAD_DIM:, HEAD_DIM:]], axis=1).reshape(N_LRU_BLOCKS, HEAD_DIM, HEAD_DIM)


def layer_fwd(x, xb, p, bsz, own_late=None, next_w_in=None):
    t_dim = x.shape[0]
    s_len = t_dim // bsz
    w_f, w_qs, w_qd = p["w_in_f"], p["w_in_qs"], p["w_in_qd"]
    proj_f = matmul(xb, w_f, name="proj_f")
    qs = matmul(xb, w_qs, out_dtype=BF16, name="proj_qs").reshape(bsz, s_len, W_QS)
    qd = matmul(xb, w_qd, name="proj_qd").reshape(bsz, s_len, W_QD)
    proj_f3 = proj_f.reshape(bsz, s_len, W_F)
    wr_bd, wi_bd = _block_diag(p["w_rg"]), _block_diag(p["w_ig"])
    (y_a, h), got_rows = lru_fwd(proj_f3, p["conv_w"], p["conv_b"], wr_bd, wi_bd, p["b_rg"], p["b_ig"], p["lru_lambda"],
                                 name="lru_fwd", carried=[("gather", own_late[1])] if own_late is not None else [])
    (y_b, lse_b, y_bb), got_fi = swa_fwd(qs, p["sinks"], name="swa_fwd", carried=[("gather", own_late[0])] if own_late is not None else [])
    (y_c, lse_c, y_cb), got_next = dil_fwd(qd, name="dil_fwd", carried=[("gather", next_w_in)] if next_w_in is not None else [])
    if own_late is not None:
        p = {**p, **_late_weights(got_fi[0], got_rows[0])}
    ys = [t.reshape(t_dim, D_MODEL) for t in (y_a, y_bb, y_cb)]
    br = [matmul(ys[n], p["w_branch"][n], out_dtype=BF16, name="branch") for n in range(3)]
    merged = merge_fwd(proj_f, br, name="merge_fwd")
    mix = matmul(merged, p["w_out"], name="w_out")
    x1, x1b, z1 = ln_fwd(x, mix, p["ln1_g"], p["ln1_b"], name="ln_fwd")
    h13 = matmul(x1b, p["w_ffn_in"], out_dtype=BF16, name="ffn_in")
    act = swiglu_fwd(h13, name="swiglu_fwd")
    ffn = matmul(act, p["w_ffn_out"], name="ffn_out")
    x2, x2b, z2 = ln_fwd(x1, ffn, p["ln2_g"], p["ln2_b"], name="ln_fwd")
    saved = dict(xb=xb, proj_f=proj_f, qs=qs, qd=qd, h=h, ys=ys, y_b=y_b, y_c=y_c, lse_b=lse_b, lse_c=lse_c, br=br, merged=merged,
                 z1=z1, x1b=x1b, h13=h13, act=act, z2=z2, wr_bd=wr_bd, wi_bd=wi_bd, p=p)
    return x2, x2b, saved, (got_next[0] if got_next else None)


def layer_bwd(dx2, s, bsz, exchange_own=False, above_w_in=None):
    p = s["p"]
    t_dim = dx2.shape[0]
    s_len = t_dim // bsz
    g = {}
    dz2, dz2b, g["ln2_g"], g["ln2_b"] = ln_bwd(dx2, s["z2"], p["ln2_g"], name="ln_bwd")
    dact = matmul(dz2b, p["w_ffn_out"], trans_b=True, out_dtype=BF16, name="d_act")
    dh13 = swiglu_bwd(dact, s["h13"], name="swiglu_bwd")
    g["w_ffn_out"] = matmul(s["act"], dz2b, trans_a=True, out_dtype=BF16, name="dw_ffn_out")
    g["w_ffn_in"] = matmul(s["x1b"], dh13, trans_a=True, out_dtype=BF16, name="dw_ffn_in")
    dx1 = matmul(dh13, p["w_ffn_in"], trans_b=True, add=dz2, add_scale=ALPHA, name="dx_ffn")
    dz1, dz1b, g["ln1_g"], g["ln1_b"] = ln_bwd(dx1, s["z1"], p["ln1_g"], name="ln_bwd")
    dmerged = matmul(dz1b, p["w_out"], trans_b=True, name="d_merged")
    g["w_out"] = matmul(s["merged"], dz1b, trans_a=True, out_dtype=BF16, name="dw_out")
    *dbr, dgates = merge_bwd(dmerged, s["proj_f"], s["br"], name="merge_bwd")
    dys = [matmul(dbr[n], p["w_branch"][n], trans_b=True, out_dtype=F32 if n == 2 else BF16, name="d_branch") for n in range(3)]
    g["w_branch"] = jnp.stack([matmul(s["ys"][n], dbr[n], trans_a=True, out_dtype=BF16, name="dw_branch") for n in range(3)])
    fi_slots, rows_slots = _late_slots(g) if exchange_own else (None, None)
    shape3 = (bsz, s_len, D_MODEL)
    (dlx, dlg, g["conv_w"], g["conv_b"], g["b_rg"], g["b_ig"], g["lru_lambda"], dwr, dwi), got_rows = lru_bwd(
        dys[0].reshape(shape3), s["proj_f"].reshape(bsz, s_len, W_F), s["h"], p["conv_w"], p["conv_b"], s["wr_bd"], s["wi_bd"],
        jnp.swapaxes(s["wr_bd"], 1, 2), jnp.swapaxes(s["wi_bd"], 1, 2), p["b_rg"], p["b_ig"], p["lru_lambda"], name="lru_bwd",
        carried=[("a2a", rows_slots)] if exchange_own else [])
    g["w_rg"], g["w_ig"] = _block_diag_grad(dwr), _block_diag_grad(dwi)
    dy_b3 = dys[1].reshape(shape3)
    (*dqs, dsinks), got_fi = swa_bwd(s["qs"], p["sinks"], s["y_b"], s["lse_b"], dy_b3, name="swa_bwd",
                                     carried=[("a2a", fi_slots)] if exchange_own else [])
    g["sinks"] = dsinks[0, :N_HEADS]
    dqd, got_in = dil_bwd(s["qd"], s["y_c"], s["lse_c"], dys[2].reshape(shape3), name="dil_bwd",
                          carried=[("a2a", above_w_in)] if above_w_in is not None else [])
    flat = lambda t: t.reshape(t_dim, t.shape[-1])
    dproj_f = jnp.concatenate([flat(dlx), flat(dlg), dgates], axis=1)
    dproj_qs = jnp.concatenate([flat(t) for t in dqs], axis=1).astype(BF16)
    dproj_qd = jnp.concatenate([flat(t) for t in dqd], axis=1).astype(BF16)
    g["w_in_f"] = matmul(s["xb"], dproj_f, trans_a=True, out_dtype=BF16, name="dw_in_f")
    g["w_in_qs"] = matmul(s["xb"], dproj_qs, trans_a=True, out_dtype=BF16, name="dw_in_qs")
    g["w_in_qd"] = matmul(s["xb"], dproj_qd, trans_a=True, out_dtype=BF16, name="dw_in_qd")
    dx = matmul(dproj_f, p["w_in_f"], trans_b=True, add=dz1, add_scale=ALPHA, name="dx_f")
    dx = matmul(dproj_qs, p["w_in_qs"], trans_b=True, add=dx, name="dx_qs")
    dx = matmul(dproj_qd, p["w_in_qd"], trans_b=True, add=dx, name="dx_qd")
    g = {k: (v.reshape(p[k].shape) if k in p else v) for k, v in g.items()}
    return dx, g, dict(late=(got_fi[0], got_rows[0]) if exchange_own else None, w_in=got_in[0] if got_in else None)


def local_step(x, target, layer_params, layer_shards=None, first_w_in=None):
    bsz, s_len, d = x.shape
    t_dim = bsz * s_len
    xf = x.reshape(t_dim, d)
    xb = xf.astype(BF16)
    exchanging = layer_shards is not None
    saved, gathered = [], first_w_in
    for l in range(DEPTH):
        p = layer_params(l, gathered)
        xf, xb, s, gathered = layer_fwd(xf, xb, p, bsz, own_late=layer_shards[l][1:] if exchanging else None,
                                        next_w_in=layer_shards[l + 1][0] if exchanging and l + 1 < DEPTH else None)
        saved.append(s)
    dy, sq = loss_head(xf, target.reshape(t_dim, d), name="loss_head")
    loss = 0.5 * jnp.sum(sq) / d
    grads, received, w_in_slots = [None] * DEPTH, [[None] * 3 for _ in range(DEPTH)], None
    for l in reversed(range(DEPTH)):
        dy, grads[l], got = layer_bwd(dy, saved[l], bsz, exchange_own=exchanging, above_w_in=w_in_slots)
        if got["w_in"] is not None:
            received[l + 1][0] = got["w_in"]
        if exchanging:
            received[l][1:] = got["late"]
            w_in_slots = _w_in_slots(grads[l])
    return loss, dy.reshape(bsz, s_len, d), grads, received, w_in_slots


W_IN_SEGMENTS = (("w_in_f", 0, 0, 2 * D_MODEL), ("w_in_qs", 0, 2 * D_MODEL, W_QS), ("w_in_qd", 0, 2 * D_MODEL + W_QS, W_QD),
                 ("w_in_f", 2 * D_MODEL, 2 * D_MODEL + W_QS + W_QD, 3 * D_MODEL))
ROW_SHARDED = ("w_branch", "w_out", "w_ffn_out")


def _cols_of_shards(shards, lo, hi):
    width = shards[0].shape[-1]
    parts = []
    for k, sh in enumerate(shards):
        a, b = max(lo, k * width), min(hi, (k + 1) * width)
        if a < b:
            parts.append(sh[..., a - k * width:b - k * width])
    return parts[0] if len(parts) == 1 else jnp.concatenate(parts, axis=-1)


def _cols_of_w_in(pieces, lo, hi):
    parts = []
    for name, p0, l0, width in W_IN_SEGMENTS:
        a, b = max(lo, l0), min(hi, l0 + width)
        if a < b:
            parts.append(pieces[name][..., p0 + a - l0:p0 + b - l0])
    return parts[0] if len(parts) == 1 else jnp.concatenate(parts, axis=-1)


W_IN_COLS = W_F + W_QS + W_QD


def _layer_shards(w, l):
    rows = jnp.concatenate([w[k][l].reshape(-1, D_MODEL) for k in ROW_SHARDED]).astype(BF16)
    return w["w_in"][l].astype(BF16), w["w_ffn_in"][l].astype(BF16), rows


ROW_COUNTS = (3 * D_MODEL // N_DEV, D_MODEL // N_DEV, FF_HIDDEN // N_DEV)


def _w_in_weights(g_in):
    sh = [g_in[k] for k in range(N_DEV)]
    return dict(w_in_f=jnp.concatenate([_cols_of_shards(sh, 0, 2 * D_MODEL), _cols_of_shards(sh, W_IN_COLS - 3 * D_MODEL, W_IN_COLS)], axis=-1),
                w_in_qs=_cols_of_shards(sh, 2 * D_MODEL, 2 * D_MODEL + W_QS),
                w_in_qd=_cols_of_shards(sh, 2 * D_MODEL + W_QS, 2 * D_MODEL + W_QS + W_QD))


def _late_weights(g_fi, g_rows):
    p = dict(w_ffn_in=jnp.concatenate([g_fi[k] for k in range(N_DEV)], axis=-1))
    off = 0
    for k, n in zip(ROW_SHARDED, ROW_COUNTS):
        t = g_rows[:, off:off + n]
        if k == "w_branch":
            p[k] = jnp.transpose(t.reshape(N_DEV, 3, n // 3, D_MODEL), (1, 0, 2, 3)).reshape(3, -1, D_MODEL)
        else:
            p[k] = t.reshape(-1, D_MODEL)
        off += n
    return p


def _w_in_slots(g):
    shard = W_IN_COLS // N_DEV
    return jnp.stack([_cols_of_w_in(g, k * shard, (k + 1) * shard) for k in range(N_DEV)]).astype(BF16)


def _late_slots(g):
    shard = g["w_ffn_in"].shape[-1] // N_DEV
    s_fi = jnp.stack([g["w_ffn_in"][:, k * shard:(k + 1) * shard] for k in range(N_DEV)]).astype(BF16)
    rows = jnp.concatenate([jnp.transpose(g["w_branch"].reshape(3, N_DEV, -1, D_MODEL), (1, 0, 2, 3)).reshape(N_DEV, -1, D_MODEL),
                            g["w_out"].reshape(N_DEV, -1, D_MODEL), g["w_ffn_out"].reshape(N_DEV, -1, D_MODEL)], axis=1).astype(BF16)
    return s_fi, rows


def _pad_rows(flat, tile_rows):
    n = flat.shape[0]
    per = tile_rows * LANES
    total = -(-n // per) * per
    return jnp.pad(flat, (0, total - n)).reshape(-1, LANES)


def kernel(x, w_in, conv_w, conv_b, w_rg, b_rg, w_ig, b_ig, lru_lambda, sinks, w_branch, w_out, ln1_g, ln1_b, w_ffn_in, w_ffn_out, ln2_g, ln2_b, loss_target, m_w_in, m_conv_w, m_conv_b, m_w_rg, m_b_rg, m_w_ig, m_b_ig, m_lru_lambda, m_sinks, m_w_branch, m_w_out, m_ln1_g, m_ln1_b, m_w_ffn_in, m_w_ffn_out, m_ln2_g, m_ln2_b, v_w_in, v_conv_w, v_conv_b, v_w_rg, v_b_rg, v_w_ig, v_b_ig, v_lru_lambda, v_sinks, v_w_branch, v_w_out, v_ln1_g, v_ln1_b, v_w_ffn_in, v_w_ffn_out, v_ln2_g, v_ln2_b):
    w = dict(w_in=w_in, conv_w=conv_w, conv_b=conv_b, w_rg=w_rg, b_rg=b_rg, w_ig=w_ig, b_ig=b_ig, lru_lambda=lru_lambda, sinks=sinks,
             w_branch=w_branch, w_out=w_out, ln1_g=ln1_g, ln1_b=ln1_b, w_ffn_in=w_ffn_in, w_ffn_out=w_ffn_out, ln2_g=ln2_g, ln2_b=ln2_b)
    m = dict(w_in=m_w_in, conv_w=m_conv_w, conv_b=m_conv_b, w_rg=m_w_rg, b_rg=m_b_rg, w_ig=m_w_ig, b_ig=m_b_ig, lru_lambda=m_lru_lambda,
             sinks=m_sinks, w_branch=m_w_branch, w_out=m_w_out, ln1_g=m_ln1_g, ln1_b=m_ln1_b, w_ffn_in=m_w_ffn_in, w_ffn_out=m_w_ffn_out,
             ln2_g=m_ln2_g, ln2_b=m_ln2_b)
    v = dict(w_in=v_w_in, conv_w=v_conv_w, conv_b=v_conv_b, w_rg=v_w_rg, b_rg=v_b_rg, w_ig=v_w_ig, b_ig=v_b_ig, lru_lambda=v_lru_lambda,
             sinks=v_sinks, w_branch=v_w_branch, w_out=v_w_out, ln1_g=v_ln1_g, ln1_b=v_ln1_b, w_ffn_in=v_w_ffn_in, w_ffn_out=v_w_ffn_out,
             ln2_g=v_ln2_g, ln2_b=v_ln2_b)
    order = ["w_in", "conv_w", "conv_b", "w_rg", "b_rg", "w_ig", "b_ig", "lru_lambda", "sinks", "w_branch", "w_out", "ln1_g", "ln1_b",
             "w_ffn_in", "w_ffn_out", "ln2_g", "ln2_b"]
    me = 4 * lax.axis_index("x") + 2 * lax.axis_index("y") + lax.axis_index("c")

    names = ("w_in", "w_ffn_in", "w_rows")
    shards = [_layer_shards(w, l) for l in range(DEPTH)]
    first_w_in = exchange("gather", shards[0][0], name="gather_w_in")
    cw = exchange("gather", conv_w.reshape(-1, LANES), name="gather_conv_w")
    conv_w_full = jnp.moveaxis(cw.reshape(N_DEV, DEPTH, CONV_WIDTH, LANES), 0, 2).reshape(DEPTH, CONV_WIDTH, D_MODEL)

    def layer_params(l, gathered_w_in):
        return {**_w_in_weights(gathered_w_in), **{k: w[k][l] for k in SMALL}, "conv_w": conv_w_full[l]}

    loss_local, grad_x, grads, received, w_in_slots = local_step(x, loss_target, layer_params, shards, first_w_in)
    loss = lax.psum(loss_local, ("x", "y", "c"))
    received[0][0] = exchange("a2a", w_in_slots, name="exchange_g_w_in")

    sums = [[sum_slots(t, name=f"sum_g_{n}") for t, n in zip(received[l], names)] for l in range(DEPTH)]
    g_final = {"w_in": jnp.stack([sums[l][0] for l in range(DEPTH)]), "w_ffn_in": jnp.stack([sums[l][1] for l in range(DEPTH)])}
    off = 0
    for k, n in zip(ROW_SHARDED, ROW_COUNTS):
        g_final[k] = jnp.stack([sums[l][2][off:off + n] for l in range(DEPTH)]).reshape(w[k].shape)
        off += n
    grads = {k: jnp.stack([grads[l][k] for l in range(DEPTH)]) for k in list(SMALL) + ["conv_w"]}

    small_names = list(SMALL) + ["conv_w"]
    small_sizes = [grads[k].size for k in small_names]
    svec = _pad_rows(jnp.concatenate([grads[k].reshape(-1) for k in small_names]), SMALL_ROWS_TILE)
    ssum = sum_slots(exchange("gather", svec, name="gather_small_grads"), name="sum_small_grads")
    sflat, off = ssum.reshape(-1), 0
    for k, n in zip(small_names, small_sizes):
        g_final[k] = sflat[off:off + n].reshape(grads[k].shape)
        off += n
    g_final["conv_w"] = lax.dynamic_slice_in_dim(g_final["conv_w"], me * LANES, LANES, axis=2)

    delta, new_m, new_v = {}, {}, {}
    for k in list(BIG) + ["conv_w"]:
        cols = w[k].shape[-1]
        two_d = lambda t: t.reshape(-1, cols)
        d_, m_, v_ = adamw(two_d(w[k]), two_d(g_final[k]), two_d(m[k]), two_d(v[k]), name=f"adamw_{k}")
        delta[k], new_m[k], new_v[k] = d_.reshape(w[k].shape), m_.reshape(w[k].shape), v_.reshape(w[k].shape)
    pack_small = lambda dct: _pad_rows(jnp.concatenate([dct[k].reshape(-1) for k in SMALL]), SMALL_ROWS_TILE)
    d_, m_, v_ = adamw(pack_small(w), pack_small(g_final), pack_small(m), pack_small(v), name="adamw_small")
    off = 0
    for k in SMALL:
        n = w[k].size
        for dst, src in ((delta, d_), (new_m, m_), (new_v, v_)):
            dst[k] = src.reshape(-1)[off:off + n].reshape(w[k].shape)
        off += n
    return (loss, grad_x, *[g_final[k] for k in order], *[delta[k] for k in order], *[new_m[k] for k in order], *[new_v[k] for k in order])
```

```python
import functools
import math

import jax
import jax.numpy as jnp
from jax import lax
from jax.experimental import pallas as pl
from jax.experimental.pallas import tpu as pltpu

F32 = jnp.float32
BF16 = jnp.bfloat16

N_DEV = 8
DEPTH = 4
D_MODEL = 1024
HEAD_DIM = 64
LANES = 128
N_HEADS = D_MODEL // HEAD_DIM
SWA_KV_HEADS = 4
ATT_BLOCK = 128
DILATIONS = (1, 4, 16)
CONV_WIDTH = 4
LRU_C = 8.0
FF_HIDDEN = 2816
ALPHA = (2.0 * DEPTH) ** 0.25
LN_EPS = 1e-5
NEG_INF = -1e30
W_F = 5 * D_MODEL
W_QS = D_MODEL + 2 * SWA_KV_HEADS * HEAD_DIM
W_QD = 3 * D_MODEL

ADAM_LR = 0.001
ADAM_B1 = 0.9
ADAM_B2 = 0.999
ADAM_EPS = 1e-08
ADAM_WD = 0.01
ADAM_STEP = 10

VMEM_LIMIT = 56 * 1024 * 1024
MATMUL_BLOCK_BYTES = 40 * 1024 * 1024
MESH = pl.DeviceIdType.MESH


def _pick(n, cands):
    for c in cands:
        if n % c == 0:
            return c
    raise ValueError(f"no tile for {n} among {cands}")


def _params(sem):
    return pltpu.CompilerParams(dimension_semantics=sem, vmem_limit_bytes=VMEM_LIMIT)


def _tile(n, cap):
    best = None
    for t in range(LANES, cap + 1, LANES):
        if n % t == 0:
            best = t
    assert best is not None, (n, cap)
    return best


def matmul(a, b, *, name, trans_a=False, trans_b=False, out_dtype=F32, add=None, add_scale=1.0):
    if trans_a:
        k_dim, m_dim = a.shape
    else:
        m_dim, k_dim = a.shape
    n_dim = b.shape[0] if trans_b else b.shape[1]
    assert (b.shape[1] if trans_b else b.shape[0]) == k_dim
    tn = _tile(n_dim, 1408)
    tm, tk = _tile(m_dim, 1024), _tile(k_dim, 1408)
    for cand in (1024, 512, 256):
        ctm = _tile(m_dim, cand)
        blocks = 2 * (ctm * k_dim * a.dtype.itemsize + tn * k_dim * b.dtype.itemsize + ctm * tn * jnp.dtype(out_dtype).itemsize
                      + (ctm * tn * add.dtype.itemsize if add is not None else 0))
        if blocks <= MATMUL_BLOCK_BYTES:
            tm, tk = ctm, k_dim
            break
    nk = k_dim // tk
    dims = (((0 if trans_a else 1,), (1 if trans_b else 0,)), ((), ()))

    def body(*refs):
        if add is None:
            a_ref, b_ref, o_ref, acc_ref = refs
            add_ref = None
        else:
            a_ref, b_ref, add_ref, o_ref, acc_ref = refs
        k = pl.program_id(2)
        part = lax.dot_general(a_ref[...].astype(BF16), b_ref[...].astype(BF16), dims, preferred_element_type=F32)

        def finish(r):
            if add_ref is not None:
                r = r + add_scale * add_ref[...].astype(F32)
            o_ref[...] = r.astype(out_dtype)

        if nk == 1:
            finish(part)
        else:
            @pl.when(k == 0)
            def _():
                acc_ref[...] = part

            @pl.when((k > 0) & (k < nk - 1))
            def _():
                acc_ref[...] += part

            @pl.when(k == nk - 1)
            def _():
                finish(acc_ref[...] + part)

    a_spec = pl.BlockSpec((tk, tm), lambda i, j, k: (k, i)) if trans_a else pl.BlockSpec((tm, tk), lambda i, j, k: (i, k))
    b_spec = pl.BlockSpec((tn, tk), lambda i, j, k: (j, k)) if trans_b else pl.BlockSpec((tk, tn), lambda i, j, k: (k, j))
    in_specs = [a_spec, b_spec]
    args = [a, b]
    if add is not None:
        in_specs.append(pl.BlockSpec((tm, tn), lambda i, j, k: (i, j)))
        args.append(add)
    return pl.pallas_call(
        body,
        out_shape=jax.ShapeDtypeStruct((m_dim, n_dim), out_dtype),
        grid=(m_dim // tm, n_dim // tn, nk),
        in_specs=in_specs,
        out_specs=pl.BlockSpec((tm, tn), lambda i, j, k: (i, j)),
        scratch_shapes=[pltpu.VMEM((tm, tn) if nk > 1 else (8, LANES), F32)],
        compiler_params=_params(("parallel", "parallel", "arbitrary")),
        name=name,
    )(*args)


def ln_fwd(x, r, g, b, *, name):
    t_dim, d = x.shape
    tr = _pick(t_dim, (256, 128, 8))

    def body(x_ref, r_ref, g_ref, b_ref, y_ref, yb_ref, z_ref):
        z = ALPHA * x_ref[...] + r_ref[...]
        mu = jnp.mean(z, axis=-1, keepdims=True)
        zc = z - mu
        var = jnp.mean(zc * zc, axis=-1, keepdims=True)
        y = zc * lax.rsqrt(var + LN_EPS) * g_ref[...] + b_ref[...]
        y_ref[...] = y
        yb_ref[...] = y.astype(BF16)
        z_ref[...] = z

    row = pl.BlockSpec((tr, d), lambda i: (i, 0))
    vec = pl.BlockSpec((1, d), lambda i: (0, 0))
    return pl.pallas_call(
        body,
        out_shape=(jax.ShapeDtypeStruct((t_dim, d), F32), jax.ShapeDtypeStruct((t_dim, d), BF16), jax.ShapeDtypeStruct((t_dim, d), F32)),
        grid=(t_dim // tr,),
        in_specs=[row, row, vec, vec],
        out_specs=(row, row, row),
        compiler_params=_params(("parallel",)),
        name=name,
    )(x, r, g.reshape(1, d), b.reshape(1, d))


def ln_bwd(dy, z, g, *, name):
    t_dim, d = dy.shape
    tr = _pick(t_dim, (256, 128, 8))

    def body(dy_ref, z_ref, g_ref, dz_ref, dzb_ref, dg_ref, db_ref):
        @pl.when(pl.program_id(0) == 0)
        def _():
            dg_ref[...] = jnp.zeros_like(dg_ref)
            db_ref[...] = jnp.zeros_like(db_ref)

        z = z_ref[...]
        dyv = dy_ref[...]
        mu = jnp.mean(z, axis=-1, keepdims=True)
        zc = z - mu
        var = jnp.mean(zc * zc, axis=-1, keepdims=True)
        rstd = lax.rsqrt(var + LN_EPS)
        xhat = zc * rstd
        dxhat = dyv * g_ref[...]
        m1 = jnp.mean(dxhat, axis=-1, keepdims=True)
        m2 = jnp.mean(dxhat * xhat, axis=-1, keepdims=True)
        dz = rstd * (dxhat - m1 - xhat * m2)
        dz_ref[...] = dz
        dzb_ref[...] = dz.astype(BF16)
        dg_ref[...] += jnp.sum(dyv * xhat, axis=0, keepdims=True)
        db_ref[...] += jnp.sum(dyv, axis=0, keepdims=True)

    row = pl.BlockSpec((tr, d), lambda i: (i, 0))
    vec = pl.BlockSpec((1, d), lambda i: (0, 0))
    return pl.pallas_call(
        body,
        out_shape=(jax.ShapeDtypeStruct((t_dim, d), F32), jax.ShapeDtypeStruct((t_dim, d), BF16),
                   jax.ShapeDtypeStruct((1, d), F32), jax.ShapeDtypeStruct((1, d), F32)),
        grid=(t_dim // tr,),
        in_specs=[row, row, vec],
        out_specs=(row, row, vec, vec),
        compiler_params=_params(("arbitrary",)),
        name=name,
    )(dy, z, g.reshape(1, d))


def loss_head(y, target, *, name):
    t_dim, d = y.shape
    tr = _pick(t_dim, (256, 128, 8))

    def body(y_ref, t_ref, dy_ref, sq_ref):
        @pl.when(pl.program_id(0) == 0)
        def _():
            sq_ref[...] = jnp.zeros_like(sq_ref)

        diff = y_ref[...] - t_ref[...]
        dy_ref[...] = diff / d
        sq_ref[...] += jnp.sum(diff * diff, axis=0, keepdims=True)

    row = pl.BlockSpec((tr, d), lambda i: (i, 0))
    vec = pl.BlockSpec((1, d), lambda i: (0, 0))
    return pl.pallas_call(
        body,
        out_shape=(jax.ShapeDtypeStruct((t_dim, d), F32), jax.ShapeDtypeStruct((1, d), F32)),
        grid=(t_dim // tr,),
        in_specs=[row, row],
        out_specs=(row, vec),
        compiler_params=_params(("arbitrary",)),
        name=name,
    )(y, target)


def _sigmoid(x):
    return 0.5 * jnp.tanh(0.5 * x) + 0.5


def swiglu_fwd(h13, *, name):
    t_dim = h13.shape[0]
    f = h13.shape[1] // 2
    tr = _pick(t_dim, (256, 128, 8))

    def body(h1_ref, h3_ref, act_ref):
        h1 = h1_ref[...].astype(F32)
        act_ref[...] = (h1 * _sigmoid(h1) * h3_ref[...].astype(F32)).astype(BF16)

    return pl.pallas_call(
        body,
        out_shape=jax.ShapeDtypeStruct((t_dim, f), BF16),
        grid=(t_dim // tr,),
        in_specs=[pl.BlockSpec((tr, f), lambda i: (i, 0)), pl.BlockSpec((tr, f), lambda i: (i, 1))],
        out_specs=pl.BlockSpec((tr, f), lambda i: (i, 0)),
        compiler_params=_params(("parallel",)),
        name=name,
    )(h13, h13)


def swiglu_bwd(dact, h13, *, name):
    t_dim = h13.shape[0]
    f = h13.shape[1] // 2
    tr = _pick(t_dim, (256, 128, 8))

    def body(da_ref, h1_ref, h3_ref, dh_ref):
        h1 = h1_ref[...].astype(F32)
        da = da_ref[...].astype(F32)
        sg = _sigmoid(h1)
        dh_ref[:, :f] = (da * h3_ref[...].astype(F32) * sg * (1.0 + h1 * (1.0 - sg))).astype(BF16)
        dh_ref[:, f:] = (da * h1 * sg).astype(BF16)

    return pl.pallas_call(
        body,
        out_shape=jax.ShapeDtypeStruct((t_dim, 2 * f), BF16),
        grid=(t_dim // tr,),
        in_specs=[pl.BlockSpec((tr, f), lambda i: (i, 0)), pl.BlockSpec((tr, f), lambda i: (i, 0)),
                  pl.BlockSpec((tr, f), lambda i: (i, 1))],
        out_specs=pl.BlockSpec((tr, 2 * f), lambda i: (i, 0)),
        compiler_params=_params(("parallel",)),
        name=name,
    )(dact, h13, h13)


def merge_fwd(proj_f, br, *, name):
    t_dim, d = br[0].shape
    tr = _pick(t_dim, (256, 128, 8))

    def body(g0, g1, g2, b0, b1, b2, o_ref):
        o_ref[...] = (_sigmoid(g0[...]) * b0[...].astype(F32) + _sigmoid(g1[...]) * b1[...].astype(F32)
                      + _sigmoid(g2[...]) * b2[...].astype(F32)).astype(BF16)

    row = pl.BlockSpec((tr, d), lambda i: (i, 0))
    gate = [pl.BlockSpec((tr, d), functools.partial(lambda n, i: (i, 2 + n), n)) for n in range(3)]
    return pl.pallas_call(
        body,
        out_shape=jax.ShapeDtypeStruct((t_dim, d), BF16),
        grid=(t_dim // tr,),
        in_specs=gate + [row, row, row],
        out_specs=row,
        compiler_params=_params(("parallel",)),
        name=name,
    )(proj_f, proj_f, proj_f, *br)


def merge_bwd(dmerged, proj_f, br, *, name):
    t_dim, d = dmerged.shape
    tr = _pick(t_dim, (256, 128, 8))

    def body(dm_ref, g0, g1, g2, b0, b1, b2, d0, d1, d2, dg_ref):
        dm = dm_ref[...]
        for n, (g, b, o) in enumerate(((g0, b0, d0), (g1, b1, d1), (g2, b2, d2))):
            sg = _sigmoid(g[...])
            o[...] = (dm * sg).astype(BF16)
            dg_ref[:, n * d:(n + 1) * d] = (dm * b[...].astype(F32) * sg * (1.0 - sg)).astype(BF16)

    row = pl.BlockSpec((tr, d), lambda i: (i, 0))
    gate = [pl.BlockSpec((tr, d), functools.partial(lambda n, i: (i, 2 + n), n)) for n in range(3)]
    return pl.pallas_call(
        body,
        out_shape=(jax.ShapeDtypeStruct((t_dim, d), BF16),) * 3 + (jax.ShapeDtypeStruct((t_dim, 3 * d), BF16),),
        grid=(t_dim // tr,),
        in_specs=[row] + gate + [row, row, row],
        out_specs=(row, row, row, pl.BlockSpec((tr, 3 * d), lambda i: (i, 0))),
        compiler_params=_params(("parallel",)),
        name=name,
    )(dmerged, proj_f, proj_f, proj_f, *br)


GELU_C = math.sqrt(2.0 / math.pi)
PAD = 8
SCAN_TILES = 8


def _gelu(x):
    return 0.5 * x * (1.0 + jnp.tanh(GELU_C * (x + 0.044715 * x * x * x)))


def _gelu_grad(x):
    t = jnp.tanh(GELU_C * (x + 0.044715 * x * x * x))
    return 0.5 * (1.0 + t) + 0.5 * x * (1.0 - t * t) * GELU_C * (1.0 + 3.0 * 0.044715 * x * x)


def _neg_expm1(x, exp_x):
    series = -x * (1.0 + x * (0.5 + x * (1.0 / 6.0)))
    return jnp.where(x > -0.02, series, 1.0 - exp_x)


def _lru_gates(xv, cw_ref, cb_ref, wr_ref, wi_ref, br_ref, bi_ref, lam_ref, pad_ref, s_len):
    pad_ref[pl.ds(0, PAD), :] = jnp.zeros((PAD, LANES), F32)
    pad_ref[pl.ds(PAD, s_len), :] = xv
    xc = cb_ref[...] + jnp.zeros((s_len, LANES), F32)
    for j in range(CONV_WIDTH):
        xc = xc + pad_ref[pl.ds(PAD - (CONV_WIDTH - 1) + j, s_len), :] * cw_ref[pl.ds(j, 1), :]
    xcb = xc.astype(BF16)
    r = _sigmoid(jnp.dot(xcb, wr_ref[0].astype(BF16), preferred_element_type=F32) + br_ref[...])
    i = _sigmoid(jnp.dot(xcb, wi_ref[0].astype(BF16), preferred_element_type=F32) + bi_ref[...])
    nl = -lam_ref[...]
    sp = jnp.maximum(nl, 0.0) + jnp.log(1.0 + jnp.exp(-jnp.abs(nl)))
    log_a = -LRU_C * r * sp
    a = jnp.exp(log_a)
    mult = jnp.sqrt(_neg_expm1(2.0 * log_a, a * a))
    return xc, r, i, sp, a, mult


def _tile_scan(a, b, row, reverse):
    for s in (1, 2, 4):
        if reverse:
            a_sh = pltpu.roll(a, 8 - s, 0)
            b_sh = pltpu.roll(b, 8 - s, 0)
            m = row + s <= 7
        else:
            a_sh = pltpu.roll(a, s, 0)
            b_sh = pltpu.roll(b, s, 0)
            m = row >= s
        b = jnp.where(m, a * b_sh + b, b)
        a = jnp.where(m, a * a_sh, a)
    return a, b


def lru_fwd(proj_f, conv_w, conv_b, wr_bd, wi_bd, b_rg, b_ig, lam, *, name, carried=None):
    bsz, s_len, _ = proj_f.shape
    d = D_MODEL
    ncb = d // LANES
    n_tiles = s_len // 8

    def body(x_ref, g_ref, cw_ref, cb_ref, wr_ref, wi_ref, br_ref, bi_ref, lam_ref, y_ref, h_ref, pad_ref, a_s, b_s):
        xc, r, i, sp, a, mult = _lru_gates(x_ref[0], cw_ref, cb_ref, wr_ref, wi_ref, br_ref, bi_ref, lam_ref, pad_ref, s_len)
        a_s[...] = a
        b_s[...] = mult * (i * xc)
        row = lax.broadcasted_iota(jnp.int32, (8, LANES), 0)

        def tiles(t, carry):
            starts = [pl.multiple_of((t * SCAN_TILES + u) * 8, 8) for u in range(SCAN_TILES)]
            local = [_tile_scan(a_s[pl.ds(i0, 8), :], b_s[pl.ds(i0, 8), :], row, False) for i0 in starts]
            for i0, (ac, hl) in zip(starts, local):
                h = hl + ac * carry
                h_ref[0, pl.ds(i0, 8), :] = h
                carry = jnp.broadcast_to(h[7:8, :], (8, LANES))
            return carry

        lax.fori_loop(0, n_tiles // SCAN_TILES, tiles, jnp.zeros((8, LANES), F32))
        y_ref[0] = (h_ref[0] * _gelu(g_ref[0])).astype(BF16)

    slab = lambda off: pl.BlockSpec((1, s_len, LANES), functools.partial(lambda o, c, b: (b, 0, o + c), off))
    vec = pl.BlockSpec((1, LANES), lambda c, b: (0, c))
    mat = pl.BlockSpec((1, LANES, LANES), lambda c, b: (c, 0, 0))
    out = pl.BlockSpec((1, s_len, LANES), lambda c, b: (b, 0, c))
    return call_with_exchange(
        body, carried,
        out_shape=(jax.ShapeDtypeStruct((bsz, s_len, d), BF16), jax.ShapeDtypeStruct((bsz, s_len, d), F32)),
        grid=(ncb, bsz),
        in_specs=[slab(0), slab(ncb), pl.BlockSpec((CONV_WIDTH, LANES), lambda c, b: (0, c)), vec, mat, mat, vec, vec, vec],
        out_specs=(out, out),
        scratch_shapes=[pltpu.VMEM((s_len + 2 * PAD, LANES), F32), pltpu.VMEM((s_len, LANES), F32), pltpu.VMEM((s_len, LANES), F32)],
        name=name,
        args=(proj_f, proj_f, conv_w, conv_b.reshape(1, d), wr_bd, wi_bd, b_rg.reshape(1, d), b_ig.reshape(1, d), lam.reshape(1, d)))


def lru_bwd(dy, proj_f, h, conv_w, conv_b, wr_bd, wi_bd, wr_bd_t, wi_bd_t, b_rg, b_ig, lam, *, name, carried=None):
    bsz, s_len, _ = proj_f.shape
    d = D_MODEL
    ncb = d // LANES
    n_tiles = s_len // 8

    def body(dy_ref, x_ref, g_ref, h_ref, cw_ref, cb_ref, wr_ref, wi_ref, wrt_ref, wit_ref, br_ref, bi_ref, lam_ref,
             dx_ref, dg_ref, dcw_ref, dcb_ref, dbr_ref, dbi_ref, dlam_ref, dwr_ref, dwi_ref, pad_ref, a_s, b_s, l_s):
        @pl.when(pl.program_id(1) == 0)
        def _():
            for ref in (dcw_ref, dcb_ref, dbr_ref, dbi_ref, dlam_ref, dwr_ref, dwi_ref):
                ref[...] = jnp.zeros_like(ref)

        xc, r, i, sp, a, mult = _lru_gates(x_ref[0], cw_ref, cb_ref, wr_ref, wi_ref, br_ref, bi_ref, lam_ref, pad_ref, s_len)
        gate = g_ref[0]
        hv = h_ref[0]
        dyv = dy_ref[0].astype(F32)
        dg_ref[0] = (dyv * hv * _gelu_grad(gate)).astype(BF16)
        b_s[...] = dyv * _gelu(gate)
        l_s[pl.ds(0, s_len), :] = a
        l_s[pl.ds(s_len, PAD), :] = jnp.zeros((PAD, LANES), F32)
        a_s[...] = l_s[pl.ds(1, s_len), :]
        row = lax.broadcasted_iota(jnp.int32, (8, LANES), 0)

        def tiles(t, carry):
            starts = [pl.multiple_of((n_tiles - 1 - (t * SCAN_TILES + u)) * 8, 8) for u in range(SCAN_TILES)]
            local = [_tile_scan(a_s[pl.ds(i0, 8), :], b_s[pl.ds(i0, 8), :], row, True) for i0 in starts]
            for i0, (ac, ll) in zip(starts, local):
                lmb = ll + ac * carry
                b_s[pl.ds(i0, 8), :] = lmb
                carry = jnp.broadcast_to(lmb[0:1, :], (8, LANES))
            return carry

        lax.fori_loop(0, n_tiles // SCAN_TILES, tiles, jnp.zeros((8, LANES), F32))
        lmb = b_s[...]
        l_s[pl.ds(0, PAD), :] = jnp.zeros((PAD, LANES), F32)
        l_s[pl.ds(PAD, s_len), :] = hv
        h_prev = l_s[pl.ds(PAD - 1, s_len), :]
        da = lmb * h_prev
        dmult = lmb * (i * xc)
        di = lmb * mult * xc
        dxc = lmb * mult * i
        dlog_a = da * a - dmult * a * a / mult
        dr = -LRU_C * sp * dlog_a
        dsp = jnp.sum(-LRU_C * r * dlog_a, axis=0, keepdims=True)
        dlam_ref[...] += dsp * (-_sigmoid(-lam_ref[...]))
        dpr = dr * r * (1.0 - r)
        dpi = di * i * (1.0 - i)
        dprb = dpr.astype(BF16)
        dpib = dpi.astype(BF16)
        xcb = xc.astype(BF16)
        dbr_ref[...] += jnp.sum(dpr, axis=0, keepdims=True)
        dbi_ref[...] += jnp.sum(dpi, axis=0, keepdims=True)
        tn = (((0,), (0,)), ((), ()))
        dwr_ref[0] += lax.dot_general(xcb, dprb, tn, preferred_element_type=F32)
        dwi_ref[0] += lax.dot_general(xcb, dpib, tn, preferred_element_type=F32)
        dxc = (dxc + jnp.dot(dprb, wrt_ref[0].astype(BF16), preferred_element_type=F32)
               + jnp.dot(dpib, wit_ref[0].astype(BF16), preferred_element_type=F32))
        dcb_ref[...] += jnp.sum(dxc, axis=0, keepdims=True)
        for j in range(CONV_WIDTH):
            dcw_ref[pl.ds(j, 1), :] += jnp.sum(dxc * pad_ref[pl.ds(PAD - (CONV_WIDTH - 1) + j, s_len), :], axis=0, keepdims=True)
        l_s[pl.ds(0, s_len), :] = dxc
        l_s[pl.ds(s_len, PAD), :] = jnp.zeros((PAD, LANES), F32)
        dx = jnp.zeros((s_len, LANES), F32)
        for j in range(CONV_WIDTH):
            dx = dx + l_s[pl.ds(CONV_WIDTH - 1 - j, s_len), :] * cw_ref[pl.ds(j, 1), :]
        dx_ref[0] = dx.astype(BF16)

    slab = lambda off: pl.BlockSpec((1, s_len, LANES), functools.partial(lambda o, c, b: (b, 0, o + c), off))
    vec = pl.BlockSpec((1, LANES), lambda c, b: (0, c))
    mat = pl.BlockSpec((1, LANES, LANES), lambda c, b: (c, 0, 0))
    cw = pl.BlockSpec((CONV_WIDTH, LANES), lambda c, b: (0, c))
    out = pl.BlockSpec((1, s_len, LANES), lambda c, b: (b, 0, c))
    vshape = jax.ShapeDtypeStruct((1, d), F32)
    mshape = jax.ShapeDtypeStruct((ncb, LANES, LANES), F32)
    return call_with_exchange(
        body, carried,
        out_shape=(jax.ShapeDtypeStruct((bsz, s_len, d), BF16),) * 2
        + (jax.ShapeDtypeStruct((CONV_WIDTH, d), F32), vshape, vshape, vshape, vshape, mshape, mshape),
        grid=(ncb, bsz),
        in_specs=[out, slab(0), slab(ncb), out, cw, vec, mat, mat, mat, mat, vec, vec, vec],
        out_specs=(out, out, cw, vec, vec, vec, vec, mat, mat),
        scratch_shapes=[pltpu.VMEM((s_len + 2 * PAD, LANES), F32), pltpu.VMEM((s_len, LANES), F32), pltpu.VMEM((s_len, LANES), F32),
                        pltpu.VMEM((s_len + 2 * PAD, LANES), F32)],
        name=name,
        args=(dy, proj_f, proj_f, h, conv_w, conv_b.reshape(1, d), wr_bd, wi_bd, wr_bd_t, wi_bd_t,
              b_rg.reshape(1, d), b_ig.reshape(1, d), lam.reshape(1, d)))


NT = (((1,), (1,)), ((), ()))
TN = (((0,), (0,)), ((), ()))
ATT_SCALE = HEAD_DIM ** -0.5


def _kv_place(head, n_kv_heads):
    kv = head // (N_HEADS // n_kv_heads)
    return kv // 2, kv % 2


def _band_mask(n, single):
    nk = ATT_BLOCK if single else 2 * ATT_BLOCK
    qi = lax.broadcasted_iota(jnp.int32, (2 * ATT_BLOCK, nk), 0) % ATT_BLOCK
    kj = lax.broadcasted_iota(jnp.int32, (2 * ATT_BLOCK, nk), 1)
    if single:
        return qi >= kj
    rel = qi + ATT_BLOCK - kj
    return (rel >= 0) & (rel <= ATT_BLOCK) & ((n > 0) | (kj >= ATT_BLOCK))


def _lane_halves():
    lane = lax.broadcasted_iota(jnp.int32, (1, LANES), 1)
    return lane < HEAD_DIM


def _stack_heads(t2, kh):
    first = _lane_halves()
    parts = []
    for a in range(2):
        ta = jnp.where(first if a == 0 else ~first, t2, jnp.zeros_like(t2))
        if a != kh[a]:
            ta = pltpu.roll(ta, HEAD_DIM, 1)
        parts.append(ta)
    return jnp.concatenate(parts, axis=0)


def _fold_heads(t, kh):
    t0, t1 = t[:ATT_BLOCK], t[ATT_BLOCK:]
    if t.shape[1] == LANES:
        if kh[0] != 0:
            t0 = pltpu.roll(t0, HEAD_DIM, 1)
        if kh[1] != 1:
            t1 = pltpu.roll(t1, HEAD_DIM, 1)
    return jnp.where(_lane_halves(), t0, t1)


def _rows_of_heads(t2):
    return jnp.concatenate([t2[:, 0:1], t2[:, HEAD_DIM:HEAD_DIM + 1]], axis=0)


PAIRS_AT_ONCE = 4


def _fill_bias(bias2_ref, bias1_ref=None):
    for i in range(2):
        bias2_ref[i] = jnp.where(_band_mask(i, False), 0.0, NEG_INF)
    if bias1_ref is not None:
        bias1_ref[...] = jnp.where(_band_mask(0, True), 0.0, NEG_INF)


def _pairs_fwd(items):
    ss = [lax.dot_general(_stack_heads(q2 * ATT_SCALE, kh), kk, NT, preferred_element_type=F32) + bias
          for q2, kk, _, bias, kh, _ in items]
    ps, ms, ls = [], [], []
    for s, (_, _, _, _, _, sink_col) in zip(ss, items):
        m = jnp.max(s, axis=-1, keepdims=True)
        if sink_col is not None:
            m = jnp.maximum(m, sink_col)
        p = jnp.exp(s - m)
        l = jnp.sum(p, axis=-1, keepdims=True)
        if sink_col is not None:
            l = l + jnp.exp(sink_col - m)
        ps.append(p.astype(BF16))
        ms.append(m)
        ls.append(l)
    pvs = [jnp.dot(p, it[2], preferred_element_type=F32) for p, it in zip(ps, items)]
    return list(zip(pvs, ms, ls))


def _pairs_bwd(items):
    first = _lane_halves()
    pre = []
    for q2, kk, vv, do2, o2, lse2, bias, kh in items:
        dd = do2 * o2
        dsum = jnp.concatenate([jnp.sum(jnp.where(first, dd, 0.0), axis=-1, keepdims=True),
                                jnp.sum(jnp.where(first, 0.0, dd), axis=-1, keepdims=True)], axis=0)
        qs = _stack_heads(q2 * ATT_SCALE, kh)
        dos = _stack_heads(do2.astype(BF16), kh)
        s = lax.dot_general(qs, kk, NT, preferred_element_type=F32) + bias
        dp = lax.dot_general(dos, vv, NT, preferred_element_type=F32)
        pre.append((qs, dos, s, dp, dsum))
    mid = []
    for (qs, dos, s, dp, dsum), it in zip(pre, items):
        p = jnp.exp(s - _rows_of_heads(it[5]))
        mid.append((p.astype(BF16), (p * (dp - dsum)).astype(BF16)))
    out = []
    for (pb, ds), (qs, dos, _, _, dsum), it in zip(mid, pre, items):
        dq = _fold_heads(jnp.dot(ds, it[1], preferred_element_type=F32), it[7]) * ATT_SCALE
        dk = lax.dot_general(ds, qs, TN, preferred_element_type=F32)
        dv = lax.dot_general(pb, dos, TN, preferred_element_type=F32)
        out.append((dq, dk, dv, dsum))
    return out


def swa_fwd(qkv, sinks, *, name, carried=None):
    bsz, s_len, width = qkv.shape
    ckv = SWA_KV_HEADS * HEAD_DIM
    nb = s_len // ATT_BLOCK
    kblk = D_MODEL // ckv

    def body(sink_ref, q_ref, kp_ref, kc_ref, vp_ref, vc_ref, o_ref, lse_ref, ob_ref, bias2):
        n = pl.program_id(1)
        _fill_bias(bias2)
        bias = bias2[jnp.minimum(n, 1)]
        kk = jnp.concatenate([kp_ref[0], kc_ref[0]], axis=0)
        vv = jnp.concatenate([vp_ref[0], vc_ref[0]], axis=0)
        top = lax.broadcasted_iota(jnp.int32, (2 * ATT_BLOCK, 1), 0) < ATT_BLOCK
        for hp0 in range(0, N_HEADS // 2, PAIRS_AT_ONCE):
            items, places = [], []
            for hp in range(hp0, hp0 + PAIRS_AT_ONCE):
                cols = slice(hp * LANES, (hp + 1) * LANES)
                kb, kh = _kv_place(2 * hp, SWA_KV_HEADS)
                kcols = slice(kb * LANES, (kb + 1) * LANES)
                sink_col = jnp.where(top, sink_ref[2 * hp], sink_ref[2 * hp + 1])
                items.append((q_ref[0, :, cols], kk[:, kcols], vv[:, kcols], bias, (kh, kh), sink_col))
                places.append((cols, (kh, kh)))
            for (pv, m, l), (cols, kh2) in zip(_pairs_fwd(items), places):
                o2 = _fold_heads(pv / l, kh2)
                o_ref[0, :, cols] = o2
                ob_ref[0, :, cols] = o2.astype(BF16)
                lse_ref[0, :, cols] = _fold_heads(m + jnp.log(l), kh2)

    prev = lambda n: jnp.maximum(n - 1, 0)
    out = pl.BlockSpec((1, ATT_BLOCK, D_MODEL), lambda b, n: (b, n, 0))
    sd = lambda dt: jax.ShapeDtypeStruct((bsz, s_len, D_MODEL), dt)
    return call_with_exchange(
        body, carried,
        out_shape=(sd(F32), sd(F32), sd(BF16)),
        grid=(bsz, nb),
        in_specs=[pl.BlockSpec(memory_space=pltpu.SMEM), out,
                  pl.BlockSpec((1, ATT_BLOCK, ckv), lambda b, n: (b, prev(n), kblk)),
                  pl.BlockSpec((1, ATT_BLOCK, ckv), lambda b, n: (b, n, kblk)),
                  pl.BlockSpec((1, ATT_BLOCK, ckv), lambda b, n: (b, prev(n), kblk + 1)),
                  pl.BlockSpec((1, ATT_BLOCK, ckv), lambda b, n: (b, n, kblk + 1))],
        out_specs=(out, out, out),
        scratch_shapes=[pltpu.VMEM((2, 2 * ATT_BLOCK, 2 * ATT_BLOCK), F32)],
        name=name,
        args=(sinks, qkv, qkv, qkv, qkv, qkv))


def swa_bwd(qkv, sinks, o, lse, do, *, name, carried=None):
    bsz, s_len, width = qkv.shape
    ckv = SWA_KV_HEADS * HEAD_DIM
    nb = s_len // ATT_BLOCK
    kblk = D_MODEL // ckv

    def body(sink_ref, q_ref, kp_ref, kc_ref, vp_ref, vc_ref, o_ref, lse_ref, do_ref, dq_ref, dk_ref, dv_ref, dsink_ref,
             dkk, dvv, ck, cv, bias2):
        n = pl.program_id(1)

        @pl.when((n == 0) & (pl.program_id(0) == 0))
        def _():
            dsink_ref[...] = jnp.zeros_like(dsink_ref)

        @pl.when(n < nb)
        def _():
            top = lax.broadcasted_iota(jnp.int32, (2 * ATT_BLOCK, 1), 0) < ATT_BLOCK
            lane = lax.broadcasted_iota(jnp.int32, dsink_ref.shape, 1)
            first_row = lax.broadcasted_iota(jnp.int32, dsink_ref.shape, 0) == 0
            _fill_bias(bias2)
            bias = bias2[jnp.minimum(n, 1)]
            kk = jnp.concatenate([kp_ref[0], kc_ref[0]], axis=0)
            vv = jnp.concatenate([vp_ref[0], vc_ref[0]], axis=0)
            dkk[...] = jnp.zeros_like(dkk)
            dvv[...] = jnp.zeros_like(dvv)
            for hp0 in range(0, N_HEADS // 2, PAIRS_AT_ONCE):
                items, places = [], []
                for hp in range(hp0, hp0 + PAIRS_AT_ONCE):
                    cols = slice(hp * LANES, (hp + 1) * LANES)
                    kb, kh = _kv_place(2 * hp, SWA_KV_HEADS)
                    kcols = slice(kb * LANES, (kb + 1) * LANES)
                    items.append((q_ref[0, :, cols], kk[:, kcols], vv[:, kcols], do_ref[0, :, cols], o_ref[0, :, cols],
                                  lse_ref[0, :, cols], bias, (kh, kh)))
                    places.append((cols, kcols, hp))
                for (dq, dk, dv, dsum), (cols, kcols, hp) in zip(_pairs_bwd(items), places):
                    dq_ref[0, :, cols] = dq
                    dkk[:, kcols] += dk
                    dvv[:, kcols] += dv
                    sink_col = jnp.where(top, sink_ref[2 * hp], sink_ref[2 * hp + 1])
                    t = -jnp.exp(sink_col - _rows_of_heads(lse_ref[0, :, cols])) * dsum
                    d0 = jnp.sum(t[:ATT_BLOCK], axis=0, keepdims=True)
                    d1 = jnp.sum(t[ATT_BLOCK:], axis=0, keepdims=True)
                    dsink_ref[...] += jnp.where(first_row & (lane == 2 * hp), d0, 0.0) + jnp.where(first_row & (lane == 2 * hp + 1), d1, 0.0)

        @pl.when((n >= 1) & (n < nb))
        def _():
            dk_ref[0] = ck[...] + dkk[pl.ds(0, ATT_BLOCK), :]
            dv_ref[0] = cv[...] + dvv[pl.ds(0, ATT_BLOCK), :]

        @pl.when(n == nb)
        def _():
            dk_ref[0] = ck[...]
            dv_ref[0] = cv[...]

        @pl.when(n < nb)
        def _():
            ck[...] = dkk[pl.ds(ATT_BLOCK, ATT_BLOCK), :]
            cv[...] = dvv[pl.ds(ATT_BLOCK, ATT_BLOCK), :]

    clamp = lambda n: jnp.minimum(n, nb - 1)
    prev = lambda n: jnp.maximum(n - 1, 0)
    row = pl.BlockSpec((1, ATT_BLOCK, D_MODEL), lambda b, n: (b, clamp(n), 0))
    kv_out = pl.BlockSpec((1, ATT_BLOCK, ckv), lambda b, n: (b, prev(n), 0))
    return call_with_exchange(
        body, carried,
        out_shape=(jax.ShapeDtypeStruct((bsz, s_len, D_MODEL), F32), jax.ShapeDtypeStruct((bsz, s_len, ckv), F32),
                   jax.ShapeDtypeStruct((bsz, s_len, ckv), F32), jax.ShapeDtypeStruct((8, LANES), F32)),
        grid=(bsz, nb + 1),
        in_specs=[pl.BlockSpec(memory_space=pltpu.SMEM), row,
                  pl.BlockSpec((1, ATT_BLOCK, ckv), lambda b, n: (b, prev(clamp(n)), kblk)),
                  pl.BlockSpec((1, ATT_BLOCK, ckv), lambda b, n: (b, clamp(n), kblk)),
                  pl.BlockSpec((1, ATT_BLOCK, ckv), lambda b, n: (b, prev(clamp(n)), kblk + 1)),
                  pl.BlockSpec((1, ATT_BLOCK, ckv), lambda b, n: (b, clamp(n), kblk + 1)),
                  row, row, row],
        out_specs=(row, kv_out, kv_out, pl.BlockSpec((8, LANES), lambda b, n: (0, 0))),
        scratch_shapes=[pltpu.VMEM((2 * ATT_BLOCK, ckv), F32), pltpu.VMEM((2 * ATT_BLOCK, ckv), F32),
                        pltpu.VMEM((ATT_BLOCK, ckv), F32), pltpu.VMEM((ATT_BLOCK, ckv), F32),
                        pltpu.VMEM((2, 2 * ATT_BLOCK, 2 * ATT_BLOCK), F32)],
        name=name,
        args=(sinks, qkv, qkv, qkv, qkv, qkv, o, lse, do))


DIL_PATTERNS = tuple((d, 2048 // d // ATT_BLOCK) for d in DILATIONS)
MHA = (0, 1)


def _dil_rows(idx, d, nb):
    j = idx // nb
    n = idx % nb
    base = j + n * (ATT_BLOCK * d)
    prev = jnp.maximum(base - ATT_BLOCK * d, j)
    if d == 1:
        return n, pl.ds(pl.multiple_of(base, ATT_BLOCK), ATT_BLOCK), pl.ds(pl.multiple_of(prev, ATT_BLOCK), ATT_BLOCK)
    return n, pl.ds(base, ATT_BLOCK, stride=d), pl.ds(prev, ATT_BLOCK, stride=d)


def dil_fwd(qkv, *, name, carried=None):
    bsz, s_len, _ = qkv.shape
    assert s_len == DIL_PATTERNS[0][0] * DIL_PATTERNS[0][1] * ATT_BLOCK
    npair = N_HEADS // 2

    def body(q_ref, k_ref, v_ref, y_ref, lse_ref, yb_ref, m_acc, l_acc, bias2, bias1):
        _fill_bias(bias2, bias1)
        for ci, (d, nb) in enumerate(DIL_PATTERNS):
            single = nb == 1

            def blocks(it, carry):
                items, places = [], []
                for u in range(PAIRS_AT_ONCE):
                    n, rows, prows = _dil_rows(it * PAIRS_AT_ONCE + u, d, nb)
                    kc = k_ref[rows, :].astype(BF16)
                    vc = v_ref[rows, :].astype(BF16)
                    if single:
                        kk, vv, bias = kc, vc, bias1[...]
                    else:
                        kk = jnp.concatenate([k_ref[prows, :].astype(BF16), kc], axis=0)
                        vv = jnp.concatenate([v_ref[prows, :].astype(BF16), vc], axis=0)
                        bias = bias2[jnp.minimum(n, 1)]
                    items.append((q_ref[rows, :].astype(BF16), kk, vv, bias, MHA, None))
                    places.append(rows)
                for (pv, m, l), rows in zip(_pairs_fwd(items), places):
                    o2, m2, l2 = _fold_heads(pv, MHA), _fold_heads(m, MHA), _fold_heads(l, MHA)
                    if ci == 0:
                        y_ref[rows, :] = o2
                        m_acc[rows, :] = m2
                        l_acc[rows, :] = l2
                    else:
                        m_old = m_acc[rows, :]
                        m_new = jnp.maximum(m_old, m2)
                        w_old = jnp.exp(m_old - m_new)
                        w_new = jnp.exp(m2 - m_new)
                        y_ref[rows, :] = y_ref[rows, :] * w_old + o2 * w_new
                        l_acc[rows, :] = l_acc[rows, :] * w_old + l2 * w_new
                        m_acc[rows, :] = m_new
                return carry

            lax.fori_loop(0, d * nb // PAIRS_AT_ONCE, blocks, 0)
        y = y_ref[...] / l_acc[...]
        y_ref[...] = y
        yb_ref[...] = y.astype(BF16)
        lse_ref[...] = m_acc[...] + jnp.log(l_acc[...])

    slab = lambda off: pl.BlockSpec((None, s_len, LANES), functools.partial(lambda o, b, h: (b, 0, o + h), off))
    sd = lambda dt: jax.ShapeDtypeStruct((bsz, s_len, D_MODEL), dt)
    return call_with_exchange(
        body, carried,
        out_shape=(sd(F32), sd(F32), sd(BF16)),
        grid=(bsz, npair),
        in_specs=[slab(0), slab(npair), slab(2 * npair)],
        out_specs=(slab(0), slab(0), slab(0)),
        scratch_shapes=[pltpu.VMEM((s_len, LANES), F32), pltpu.VMEM((s_len, LANES), F32),
                        pltpu.VMEM((2, 2 * ATT_BLOCK, 2 * ATT_BLOCK), F32), pltpu.VMEM((2 * ATT_BLOCK, ATT_BLOCK), F32)],
        name=name,
        args=(qkv, qkv, qkv))


def dil_bwd(qkv, y, lse, dy, *, name, carried=None):
    bsz, s_len, _ = qkv.shape
    npair = N_HEADS // 2

    def body(q_ref, k_ref, v_ref, y_ref, lse_ref, dy_ref, dq_ref, dk_ref, dv_ref, bias2, bias1):
        _fill_bias(bias2, bias1)
        dq_ref[...] = jnp.zeros_like(dq_ref)
        dk_ref[...] = jnp.zeros_like(dk_ref)
        dv_ref[...] = jnp.zeros_like(dv_ref)
        for d, nb in DIL_PATTERNS:
            single = nb == 1

            def blocks(it, carry):
                items, places = [], []
                for u in range(PAIRS_AT_ONCE):
                    n, rows, prows = _dil_rows(it * PAIRS_AT_ONCE + u, d, nb)
                    kc = k_ref[rows, :].astype(BF16)
                    vc = v_ref[rows, :].astype(BF16)
                    if single:
                        kk, vv, bias = kc, vc, bias1[...]
                    else:
                        kk = jnp.concatenate([k_ref[prows, :].astype(BF16), kc], axis=0)
                        vv = jnp.concatenate([v_ref[prows, :].astype(BF16), vc], axis=0)
                        bias = bias2[jnp.minimum(n, 1)]
                    items.append((q_ref[rows, :].astype(BF16), kk, vv, dy_ref[rows, :], y_ref[rows, :], lse_ref[rows, :], bias, MHA))
                    places.append((rows, prows))
                for (dq, dk, dv, _), (rows, prows) in zip(_pairs_bwd(items), places):
                    dq_ref[rows, :] += dq
                    if single:
                        dk_ref[rows, :] += dk
                        dv_ref[rows, :] += dv
                    else:
                        dk_ref[prows, :] += dk[:ATT_BLOCK]
                        dv_ref[prows, :] += dv[:ATT_BLOCK]
                        dk_ref[rows, :] += dk[ATT_BLOCK:]
                        dv_ref[rows, :] += dv[ATT_BLOCK:]
                return carry

            lax.fori_loop(0, d * nb // PAIRS_AT_ONCE, blocks, 0)

    slab = lambda off: pl.BlockSpec((None, s_len, LANES), functools.partial(lambda o, b, h: (b, 0, o + h), off))
    sd = jax.ShapeDtypeStruct((bsz, s_len, D_MODEL), F32)
    return call_with_exchange(
        body, carried,
        out_shape=(sd, sd, sd),
        grid=(bsz, npair),
        in_specs=[slab(0), slab(npair), slab(2 * npair), slab(0), slab(0), slab(0)],
        out_specs=(slab(0), slab(0), slab(0)),
        scratch_shapes=[pltpu.VMEM((2, 2 * ATT_BLOCK, 2 * ATT_BLOCK), F32), pltpu.VMEM((2 * ATT_BLOCK, ATT_BLOCK), F32)],
        name=name,
        args=(qkv, qkv, qkv, y, lse, dy))


def adamw(w, g, m, v, *, name):
    rows, cols = w.shape
    tr = _pick(rows, (256, 128, 64, 32, 16, 8))

    def body(w_ref, g_ref, m_ref, v_ref, d_ref, nm_ref, nv_ref):
        gv = g_ref[...]
        nm = ADAM_B1 * m_ref[...] + (1.0 - ADAM_B1) * gv
        nv = ADAM_B2 * v_ref[...] + (1.0 - ADAM_B2) * (gv * gv)
        m_hat = nm / (1.0 - ADAM_B1 ** ADAM_STEP)
        v_hat = nv / (1.0 - ADAM_B2 ** ADAM_STEP)
        d_ref[...] = -ADAM_LR * (m_hat / (jnp.sqrt(v_hat) + ADAM_EPS) + ADAM_WD * w_ref[...])
        nm_ref[...] = nm
        nv_ref[...] = nv

    row = pl.BlockSpec((tr, cols), lambda i: (i, 0))
    return pl.pallas_call(
        body,
        out_shape=(jax.ShapeDtypeStruct((rows, cols), F32),) * 3,
        grid=(rows // tr,),
        in_specs=[row] * 4,
        out_specs=(row, row, row),
        compiler_params=_params(("parallel",)),
        name=name,
    )(w, g, m, v)


def _place():
    return lax.axis_index("x"), lax.axis_index("y"), lax.axis_index("c")


def _gather_copies(x_ref, out_ref, send_sems, recv_sems):
    x, y, c = _place()
    me, sibling = (x, y, c), (x, y, 1 - c)
    chips = [(1 - x, y), (x, 1 - y), (1 - x, 1 - y)]

    def slot(px, py, pc):
        return out_ref.at[4 * px + 2 * py + pc]

    def copy(k, block, to, src=None):
        return pltpu.make_async_remote_copy(
            src_ref=slot(*block) if src is None else src, dst_ref=slot(*block),
            send_sem=send_sems.at[k], recv_sem=recv_sems.at[k], device_id=to, device_id_type=MESH)

    first = [lambda: copy(0, me, sibling, src=x_ref)] + [functools.partial(copy, 1 + j, me, (*chip, c), src=x_ref)
                                                         for j, chip in enumerate(chips)]
    passed = [functools.partial(copy, 4 + j, (*chip, c), sibling) for j, chip in enumerate(chips)]
    landing = [functools.partial(copy, 1 + j, (*chip, c), me) for j, chip in enumerate(chips)]
    from_sibling = [lambda: copy(0, sibling, me)] + [functools.partial(copy, 4 + j, (*chip, 1 - c), me) for j, chip in enumerate(chips)]
    return slot(*me), first, passed, landing, from_sibling


def _gather_start(x_ref, out_ref, send_sems, recv_sems, local_sem):
    mine, first, _, _, _ = _gather_copies(x_ref, out_ref, send_sems, recv_sems)
    pltpu.make_async_copy(x_ref, mine, local_sem).start()
    for cp in first:
        cp().start()


def _gather_finish(x_ref, out_ref, send_sems, recv_sems, local_sem):
    mine, first, passed, landing, from_sibling = _gather_copies(x_ref, out_ref, send_sems, recv_sems)
    for cp, fwd in zip(landing, passed):
        cp().wait_recv()
        fwd().start()
    for cp in from_sibling:
        cp().wait_recv()
    for cp in first + passed:
        cp().wait_send()
    pltpu.make_async_copy(x_ref, mine, local_sem).wait()


def _a2a_copies(x_ref, out_ref, send_sems, recv_sems):
    x, y, c = _place()
    me = 4 * x + 2 * y + c
    copies = []
    for k in range(1, N_DEV):
        px = 1 - x if k & 4 else x
        py = 1 - y if k & 2 else y
        pc = 1 - c if k & 1 else c
        copies.append(pltpu.make_async_remote_copy(
            src_ref=x_ref.at[4 * px + 2 * py + pc], dst_ref=out_ref.at[me], send_sem=send_sems.at[k - 1],
            recv_sem=recv_sems.at[k - 1], device_id=(px, py, pc), device_id_type=MESH))
    return me, copies


def _a2a_start(x_ref, out_ref, send_sems, recv_sems, local_sem):
    me, copies = _a2a_copies(x_ref, out_ref, send_sems, recv_sems)
    pltpu.make_async_copy(x_ref.at[me], out_ref.at[me], local_sem).start()
    for cp in copies:
        cp.start()


def _a2a_finish(x_ref, out_ref, send_sems, recv_sems, local_sem):
    me, copies = _a2a_copies(x_ref, out_ref, send_sems, recv_sems)
    for cp in copies:
        cp.wait_recv()
    for cp in copies:
        cp.wait_send()
    pltpu.make_async_copy(x_ref.at[me], out_ref.at[me], local_sem).wait()


EXCHANGES = {"gather": (_gather_start, _gather_finish, lambda x: (N_DEV,) + x.shape),
             "a2a": (_a2a_start, _a2a_finish, lambda x: x.shape)}
EXCHANGE_SEMS = [pltpu.SemaphoreType.DMA((7,)), pltpu.SemaphoreType.DMA((7,)), pltpu.SemaphoreType.DMA(())]


def exchange(kind, x, *, name):
    start, finish, shape = EXCHANGES[kind]

    def body(x_ref, out_ref, *sems):
        start(x_ref, out_ref, *sems)
        finish(x_ref, out_ref, *sems)

    return pl.pallas_call(
        body,
        out_shape=jax.ShapeDtypeStruct(shape(x), x.dtype),
        in_specs=[pl.BlockSpec(memory_space=pl.ANY)],
        out_specs=pl.BlockSpec(memory_space=pl.ANY),
        scratch_shapes=EXCHANGE_SEMS,
        name=name,
    )(x)


def call_with_exchange(body, carried, *, out_shape, grid, in_specs, out_specs, scratch_shapes, name, args):
    sem = ("arbitrary",) * len(grid)
    carried = list(carried or ())
    if not carried:
        res = pl.pallas_call(body, out_shape=out_shape, grid=grid, in_specs=in_specs, out_specs=out_specs,
                             scratch_shapes=scratch_shapes, compiler_params=_params(sem), name=name)(*args)
        return res, []
    n_in, n_out, n_scr, n_x = len(in_specs), len(out_shape), len(scratch_shapes), len(carried)
    n_sems = len(EXCHANGE_SEMS)

    def wrapped(*refs):
        ins, x_refs = refs[:n_in], refs[n_in:n_in + n_x]
        outs = refs[n_in + n_x:n_in + n_x + n_out]
        out_refs = refs[n_in + n_x + n_out:n_in + 2 * n_x + n_out]
        rest = refs[n_in + 2 * n_x + n_out:]
        scratch, sems = rest[:n_scr], rest[n_scr:]
        ids = [pl.program_id(i) for i in range(len(grid))]
        is_first = functools.reduce(lambda a, b: a & b, [i == 0 for i in ids])
        is_last = functools.reduce(lambda a, b: a & b, [i == g - 1 for i, g in zip(ids, grid)])

        @pl.when(is_first)
        def _():
            for e, (kind, _) in enumerate(carried):
                EXCHANGES[kind][0](x_refs[e], out_refs[e], *sems[e * n_sems:(e + 1) * n_sems])

        body(*ins, *outs, *scratch)

        @pl.when(is_last)
        def _():
            for e, (kind, _) in enumerate(carried):
                EXCHANGES[kind][1](x_refs[e], out_refs[e], *sems[e * n_sems:(e + 1) * n_sems])

    any_spec = pl.BlockSpec(memory_space=pl.ANY)
    res = pl.pallas_call(
        wrapped,
        out_shape=tuple(out_shape) + tuple(jax.ShapeDtypeStruct(EXCHANGES[kind][2](x), x.dtype) for kind, x in carried),
        grid=grid,
        in_specs=list(in_specs) + [any_spec] * n_x,
        out_specs=tuple(out_specs) + (any_spec,) * n_x,
        scratch_shapes=list(scratch_shapes) + EXCHANGE_SEMS * n_x,
        compiler_params=_params(sem),
        name=name + "".join("_" + kind for kind, _ in carried),
    )(*args, *[x for _, x in carried])
    return res[:n_out], list(res[n_out:])


def sum_slots(x, *, name):
    _, rows, cols = x.shape
    tr = _pick(rows, (512, 256, 128, 64, 32, 16))

    def body(x_ref, o_ref):
        acc = x_ref[0].astype(F32)
        for k in range(1, N_DEV):
            acc = acc + x_ref[k].astype(F32)
        o_ref[...] = acc

    return pl.pallas_call(
        body,
        out_shape=jax.ShapeDtypeStruct((rows, cols), F32),
        grid=(rows // tr,),
        in_specs=[pl.BlockSpec((N_DEV, tr, cols), lambda i: (0, i, 0))],
        out_specs=pl.BlockSpec((tr, cols), lambda i: (i, 0)),
        compiler_params=_params(("parallel",)),
        name=name,
    )(x)


BIG = ("w_in", "w_branch", "w_out", "w_ffn_in", "w_ffn_out")
SMALL = ("conv_b", "w_rg", "b_rg", "w_ig", "b_ig", "lru_lambda", "sinks", "ln1_g", "ln1_b", "ln2_g", "ln2_b")
N_LRU_BLOCKS = D_MODEL // HEAD_DIM
SMALL_ROWS_TILE = 512


def _block_diag(w):
    z = jnp.zeros((N_LRU_BLOCKS // 2, HEAD_DIM, HEAD_DIM), w.dtype)
    top = jnp.concatenate([w[0::2], z], axis=2)
    bot = jnp.concatenate([z, w[1::2]], axis=2)
    return jnp.concatenate([top, bot], axis=1)


def _block_diag_grad(g):
    return jnp.stack([g[:, :HEAD_DIM, :HEAD_DIM], g[:, HEAD_DIM:, HEAD_DIM:]], axis=1).reshape(N_LRU_BLOCKS, HEAD_DIM, HEAD_DIM)


def layer_fwd(x, xb, p, bsz, own_late=None, next_w_in=None):
    t_dim = x.shape[0]
    s_len = t_dim // bsz
    w_f, w_qs, w_qd = p["w_in_f"], p["w_in_qs"], p["w_in_qd"]
    proj_f = matmul(xb, w_f, name="proj_f")
    qs = matmul(xb, w_qs, out_dtype=BF16, name="proj_qs").reshape(bsz, s_len, W_QS)
    qd = matmul(xb, w_qd, name="proj_qd").reshape(bsz, s_len, W_QD)
    proj_f3 = proj_f.reshape(bsz, s_len, W_F)
    wr_bd, wi_bd = _block_diag(p["w_rg"]), _block_diag(p["w_ig"])
    (y_a, h), got_rows = lru_fwd(proj_f3, p["conv_w"], p["conv_b"], wr_bd, wi_bd, p["b_rg"], p["b_ig"], p["lru_lambda"],
                                 name="lru_fwd", carried=[("gather", own_late[1])] if own_late is not None else [])
    (y_b, lse_b, y_bb), got_fi = swa_fwd(qs, p["sinks"], name="swa_fwd", carried=[("gather", own_late[0])] if own_late is not None else [])
    (y_c, lse_c, y_cb), got_next = dil_fwd(qd, name="dil_fwd", carried=[("gather", next_w_in)] if next_w_in is not None else [])
    if own_late is not None:
        p = {**p, **_late_weights(got_fi[0], got_rows[0])}
    ys = [t.reshape(t_dim, D_MODEL) for t in (y_a, y_bb, y_cb)]
    br = [matmul(ys[n], p["w_branch"][n], out_dtype=BF16, name="branch") for n in range(3)]
    merged = merge_fwd(proj_f, br, name="merge_fwd")
    mix = matmul(merged, p["w_out"], name="w_out")
    x1, x1b, z1 = ln_fwd(x, mix, p["ln1_g"], p["ln1_b"], name="ln_fwd")
    h13 = matmul(x1b, p["w_ffn_in"], out_dtype=BF16, name="ffn_in")
    act = swiglu_fwd(h13, name="swiglu_fwd")
    ffn = matmul(act, p["w_ffn_out"], name="ffn_out")
    x2, x2b, z2 = ln_fwd(x1, ffn, p["ln2_g"], p["ln2_b"], name="ln_fwd")
    saved = dict(xb=xb, proj_f=proj_f, qs=qs, qd=qd, h=h, ys=ys, y_b=y_b, y_c=y_c, lse_b=lse_b, lse_c=lse_c, br=br, merged=merged,
                 z1=z1, x1b=x1b, h13=h13, act=act, z2=z2, wr_bd=wr_bd, wi_bd=wi_bd, p=p)
    return x2, x2b, saved, (got_next[0] if got_next else None)


def layer_bwd(dx2, s, bsz, exchange_own=False, above_w_in=None):
    p = s["p"]
    t_dim = dx2.shape[0]
    s_len = t_dim // bsz
    g = {}
    dz2, dz2b, g["ln2_g"], g["ln2_b"] = ln_bwd(dx2, s["z2"], p["ln2_g"], name="ln_bwd")
    dact = matmul(dz2b, p["w_ffn_out"], trans_b=True, out_dtype=BF16, name="d_act")
    dh13 = swiglu_bwd(dact, s["h13"], name="swiglu_bwd")
    g["w_ffn_out"] = matmul(s["act"], dz2b, trans_a=True, out_dtype=BF16, name="dw_ffn_out")
    g["w_ffn_in"] = matmul(s["x1b"], dh13, trans_a=True, out_dtype=BF16, name="dw_ffn_in")
    dx1 = matmul(dh13, p["w_ffn_in"], trans_b=True, add=dz2, add_scale=ALPHA, name="dx_ffn")
    dz1, dz1b, g["ln1_g"], g["ln1_b"] = ln_bwd(dx1, s["z1"], p["ln1_g"], name="ln_bwd")
    dmerged = matmul(dz1b, p["w_out"], trans_b=True, name="d_merged")
    g["w_out"] = matmul(s["merged"], dz1b, trans_a=True, out_dtype=BF16, name="dw_out")
    *dbr, dgates = merge_bwd(dmerged, s["proj_f"], s["br"], name="merge_bwd")
    dys = [matmul(dbr[n], p["w_branch"][n], trans_b=True, out_dtype=F32 if n == 2 else BF16, name="d_branch") for n in range(3)]
    g["w_branch"] = jnp.stack([matmul(s["ys"][n], dbr[n], trans_a=True, out_dtype=BF16, name="dw_branch") for n in range(3)])
    fi_slots, rows_slots = _late_slots(g) if exchange_own else (None, None)
    shape3 = (bsz, s_len, D_MODEL)
    (dlx, dlg, g["conv_w"], g["conv_b"], g["b_rg"], g["b_ig"], g["lru_lambda"], dwr, dwi), got_rows = lru_bwd(
        dys[0].reshape(shape3), s["proj_f"].reshape(bsz, s_len, W_F), s["h"], p["conv_w"], p["conv_b"], s["wr_bd"], s["wi_bd"],
        jnp.swapaxes(s["wr_bd"], 1, 2), jnp.swapaxes(s["wi_bd"], 1, 2), p["b_rg"], p["b_ig"], p["lru_lambda"], name="lru_bwd",
        carried=[("a2a", rows_slots)] if exchange_own else [])
    g["w_rg"], g["w_ig"] = _block_diag_grad(dwr), _block_diag_grad(dwi)
    dy_b3 = dys[1].reshape(shape3)
    (*dqs, dsinks), got_fi = swa_bwd(s["qs"], p["sinks"], s["y_b"], s["lse_b"], dy_b3, name="swa_bwd",
                                     carried=[("a2a", fi_slots)] if exchange_own else [])
    g["sinks"] = dsinks[0, :N_HEADS]
    dqd, got_in = dil_bwd(s["qd"], s["y_c"], s["lse_c"], dys[2].reshape(shape3), name="dil_bwd",
                          carried=[("a2a", above_w_in)] if above_w_in is not None else [])
    flat = lambda t: t.reshape(t_dim, t.shape[-1])
    dproj_f = jnp.concatenate([flat(dlx), flat(dlg), dgates], axis=1)
    dproj_qs = jnp.concatenate([flat(t) for t in dqs], axis=1).astype(BF16)
    dproj_qd = jnp.concatenate([flat(t) for t in dqd], axis=1).astype(BF16)
    g["w_in_f"] = matmul(s["xb"], dproj_f, trans_a=True, out_dtype=BF16, name="dw_in_f")
    g["w_in_qs"] = matmul(s["xb"], dproj_qs, trans_a=True, out_dtype=BF16, name="dw_in_qs")
    g["w_in_qd"] = matmul(s["xb"], dproj_qd, trans_a=True, out_dtype=BF16, name="dw_in_qd")
    dx = matmul(dproj_f, p["w_in_f"], trans_b=True, add=dz1, add_scale=ALPHA, name="dx_f")
    dx = matmul(dproj_qs, p["w_in_qs"], trans_b=True, add=dx, name="dx_qs")
    dx = matmul(dproj_qd, p["w_in_qd"], trans_b=True, add=dx, name="dx_qd")
    g = {k: (v.reshape(p[k].shape) if k in p else v) for k, v in g.items()}
    return dx, g, dict(late=(got_fi[0], got_rows[0]) if exchange_own else None, w_in=got_in[0] if got_in else None)


def local_step(x, target, layer_params, layer_shards=None, first_w_in=None):
    bsz, s_len, d = x.shape
    t_dim = bsz * s_len
    xf = x.reshape(t_dim, d)
    xb = xf.astype(BF16)
    exchanging = layer_shards is not None
    saved, gathered = [], first_w_in
    for l in range(DEPTH):
        p = layer_params(l, gathered)
        xf, xb, s, gathered = layer_fwd(xf, xb, p, bsz, own_late=layer_shards[l][1:] if exchanging else None,
                                        next_w_in=layer_shards[l + 1][0] if exchanging and l + 1 < DEPTH else None)
        saved.append(s)
    dy, sq = loss_head(xf, target.reshape(t_dim, d), name="loss_head")
    loss = 0.5 * jnp.sum(sq) / d
    grads, received, w_in_slots = [None] * DEPTH, [[None] * 3 for _ in range(DEPTH)], None
    for l in reversed(range(DEPTH)):
        dy, grads[l], got = layer_bwd(dy, saved[l], bsz, exchange_own=exchanging, above_w_in=w_in_slots)
        if got["w_in"] is not None:
            received[l + 1][0] = got["w_in"]
        if exchanging:
            received[l][1:] = got["late"]
            w_in_slots = _w_in_slots(grads[l])
    return loss, dy.reshape(bsz, s_len, d), grads, received, w_in_slots


W_IN_SEGMENTS = (("w_in_f", 0, 0, 2 * D_MODEL), ("w_in_qs", 0, 2 * D_MODEL, W_QS), ("w_in_qd", 0, 2 * D_MODEL + W_QS, W_QD),
                 ("w_in_f", 2 * D_MODEL, 2 * D_MODEL + W_QS + W_QD, 3 * D_MODEL))
ROW_SHARDED = ("w_branch", "w_out", "w_ffn_out")


def _cols_of_shards(shards, lo, hi):
    width = shards[0].shape[-1]
    parts = []
    for k, sh in enumerate(shards):
        a, b = max(lo, k * width), min(hi, (k + 1) * width)
        if a < b:
            parts.append(sh[..., a - k * width:b - k * width])
    return parts[0] if len(parts) == 1 else jnp.concatenate(parts, axis=-1)


def _cols_of_w_in(pieces, lo, hi):
    parts = []
    for name, p0, l0, width in W_IN_SEGMENTS:
        a, b = max(lo, l0), min(hi, l0 + width)
        if a < b:
            parts.append(pieces[name][..., p0 + a - l0:p0 + b - l0])
    return parts[0] if len(parts) == 1 else jnp.concatenate(parts, axis=-1)


W_IN_COLS = W_F + W_QS + W_QD


def _layer_shards(w):
    rows = jnp.concatenate([w[k].reshape(DEPTH, -1, D_MODEL) for k in ROW_SHARDED], axis=1).astype(BF16)
    w_in, w_fi = w["w_in"].astype(BF16), w["w_ffn_in"].astype(BF16)
    return [(w_in[l], w_fi[l], rows[l]) for l in range(DEPTH)]


ROW_COUNTS = (3 * D_MODEL // N_DEV, D_MODEL // N_DEV, FF_HIDDEN // N_DEV)


def _w_in_weights(g_in):
    sh = [g_in[k] for k in range(N_DEV)]
    return dict(w_in_f=jnp.concatenate([_cols_of_shards(sh, 0, 2 * D_MODEL), _cols_of_shards(sh, W_IN_COLS - 3 * D_MODEL, W_IN_COLS)], axis=-1),
                w_in_qs=_cols_of_shards(sh, 2 * D_MODEL, 2 * D_MODEL + W_QS),
                w_in_qd=_cols_of_shards(sh, 2 * D_MODEL + W_QS, 2 * D_MODEL + W_QS + W_QD))


def _late_weights(g_fi, g_rows):
    p = dict(w_ffn_in=jnp.concatenate([g_fi[k] for k in range(N_DEV)], axis=-1))
    off = 0
    for k, n in zip(ROW_SHARDED, ROW_COUNTS):
        t = g_rows[:, off:off + n]
        if k == "w_branch":
            p[k] = jnp.transpose(t.reshape(N_DEV, 3, n // 3, D_MODEL), (1, 0, 2, 3)).reshape(3, -1, D_MODEL)
        else:
            p[k] = t.reshape(-1, D_MODEL)
        off += n
    return p


def _w_in_slots(g):
    shard = W_IN_COLS // N_DEV
    return jnp.stack([_cols_of_w_in(g, k * shard, (k + 1) * shard) for k in range(N_DEV)]).astype(BF16)


def _late_slots(g):
    shard = g["w_ffn_in"].shape[-1] // N_DEV
    s_fi = jnp.stack([g["w_ffn_in"][:, k * shard:(k + 1) * shard] for k in range(N_DEV)]).astype(BF16)
    rows = jnp.concatenate([jnp.transpose(g["w_branch"].reshape(3, N_DEV, -1, D_MODEL), (1, 0, 2, 3)).reshape(N_DEV, -1, D_MODEL),
                            g["w_out"].reshape(N_DEV, -1, D_MODEL), g["w_ffn_out"].reshape(N_DEV, -1, D_MODEL)], axis=1).astype(BF16)
    return s_fi, rows


def _pad_rows(flat, tile_rows):
    n = flat.shape[0]
    per = tile_rows * LANES
    total = -(-n // per) * per
    return jnp.pad(flat, (0, total - n)).reshape(-1, LANES)


def kernel(x, w_in, conv_w, conv_b, w_rg, b_rg, w_ig, b_ig, lru_lambda, sinks, w_branch, w_out, ln1_g, ln1_b, w_ffn_in, w_ffn_out, ln2_g, ln2_b, loss_target, m_w_in, m_conv_w, m_conv_b, m_w_rg, m_b_rg, m_w_ig, m_b_ig, m_lru_lambda, m_sinks, m_w_branch, m_w_out, m_ln1_g, m_ln1_b, m_w_ffn_in, m_w_ffn_out, m_ln2_g, m_ln2_b, v_w_in, v_conv_w, v_conv_b, v_w_rg, v_b_rg, v_w_ig, v_b_ig, v_lru_lambda, v_sinks, v_w_branch, v_w_out, v_ln1_g, v_ln1_b, v_w_ffn_in, v_w_ffn_out, v_ln2_g, v_ln2_b):
    w = dict(w_in=w_in, conv_w=conv_w, conv_b=conv_b, w_rg=w_rg, b_rg=b_rg, w_ig=w_ig, b_ig=b_ig, lru_lambda=lru_lambda, sinks=sinks,
             w_branch=w_branch, w_out=w_out, ln1_g=ln1_g, ln1_b=ln1_b, w_ffn_in=w_ffn_in, w_ffn_out=w_ffn_out, ln2_g=ln2_g, ln2_b=ln2_b)
    m = dict(w_in=m_w_in, conv_w=m_conv_w, conv_b=m_conv_b, w_rg=m_w_rg, b_rg=m_b_rg, w_ig=m_w_ig, b_ig=m_b_ig, lru_lambda=m_lru_lambda,
             sinks=m_sinks, w_branch=m_w_branch, w_out=m_w_out, ln1_g=m_ln1_g, ln1_b=m_ln1_b, w_ffn_in=m_w_ffn_in, w_ffn_out=m_w_ffn_out,
             ln2_g=m_ln2_g, ln2_b=m_ln2_b)
    v = dict(w_in=v_w_in, conv_w=v_conv_w, conv_b=v_conv_b, w_rg=v_w_rg, b_rg=v_b_rg, w_ig=v_w_ig, b_ig=v_b_ig, lru_lambda=v_lru_lambda,
             sinks=v_sinks, w_branch=v_w_branch, w_out=v_w_out, ln1_g=v_ln1_g, ln1_b=v_ln1_b, w_ffn_in=v_w_ffn_in, w_ffn_out=v_w_ffn_out,
             ln2_g=v_ln2_g, ln2_b=v_ln2_b)
    order = ["w_in", "conv_w", "conv_b", "w_rg", "b_rg", "w_ig", "b_ig", "lru_lambda", "sinks", "w_branch", "w_out", "ln1_g", "ln1_b",
             "w_ffn_in", "w_ffn_out", "ln2_g", "ln2_b"]
    me = 4 * lax.axis_index("x") + 2 * lax.axis_index("y") + lax.axis_index("c")

    names = ("w_in", "w_ffn_in", "w_rows")
    shards = _layer_shards(w)
    first_w_in = exchange("gather", shards[0][0], name="gather_w_in")
    cw = exchange("gather", conv_w.reshape(-1, LANES), name="gather_conv_w")
    conv_w_full = jnp.moveaxis(cw.reshape(N_DEV, DEPTH, CONV_WIDTH, LANES), 0, 2).reshape(DEPTH, CONV_WIDTH, D_MODEL)

    def layer_params(l, gathered_w_in):
        return {**_w_in_weights(gathered_w_in), **{k: w[k][l] for k in SMALL}, "conv_w": conv_w_full[l]}

    loss_local, grad_x, grads, received, w_in_slots = local_step(x, loss_target, layer_params, shards, first_w_in)
    loss = lax.psum(loss_local, ("x", "y", "c"))
    received[0][0] = exchange("a2a", w_in_slots, name="exchange_g_w_in")

    sums = [[sum_slots(t, name=f"sum_g_{n}") for t, n in zip(received[l], names)] for l in range(DEPTH)]
    g_final = {"w_in": jnp.stack([sums[l][0] for l in range(DEPTH)]), "w_ffn_in": jnp.stack([sums[l][1] for l in range(DEPTH)])}
    off = 0
    for k, n in zip(ROW_SHARDED, ROW_COUNTS):
        g_final[k] = jnp.stack([sums[l][2][off:off + n] for l in range(DEPTH)]).reshape(w[k].shape)
        off += n
    grads = {k: jnp.stack([grads[l][k] for l in range(DEPTH)]) for k in list(SMALL) + ["conv_w"]}

    small_names = list(SMALL) + ["conv_w"]
    small_sizes = [grads[k].size for k in small_names]
    svec = _pad_rows(jnp.concatenate([grads[k].reshape(-1) for k in small_names]), SMALL_ROWS_TILE)
    ssum = sum_slots(exchange("gather", svec, name="gather_small_grads"), name="sum_small_grads")
    sflat, off = ssum.reshape(-1), 0
    for k, n in zip(small_names, small_sizes):
        g_final[k] = sflat[off:off + n].reshape(grads[k].shape)
        off += n
    g_final["conv_w"] = lax.dynamic_slice_in_dim(g_final["conv_w"], me * LANES, LANES, axis=2)

    delta, new_m, new_v = {}, {}, {}
    for k in list(BIG) + ["conv_w"]:
        cols = w[k].shape[-1]
        two_d = lambda t: t.reshape(-1, cols)
        d_, m_, v_ = adamw(two_d(w[k]), two_d(g_final[k]), two_d(m[k]), two_d(v[k]), name=f"adamw_{k}")
        delta[k], new_m[k], new_v[k] = d_.reshape(w[k].shape), m_.reshape(w[k].shape), v_.reshape(w[k].shape)
    pack_small = lambda dct: _pad_rows(jnp.concatenate([dct[k].reshape(-1) for k in SMALL]), SMALL_ROWS_TILE)
    d_, m_, v_ = adamw(pack_small(w), pack_small(g_final), pack_small(m), pack_small(v), name="adamw_small")
    off = 0
    for k in SMALL:
        n = w[k].size
        for dst, src in ((delta, d_), (new_m, m_), (new_v, v_)):
            dst[k] = src.reshape(-1)[off:off + n].reshape(w[k].shape)
        off += n
    return (loss, grad_x, *[g_final[k] for k in order], *[delta[k] for k in order], *[new_m[k] for k in order], *[new_v[k] for k in order])
```

```python
import functools
import math

import jax
import jax.numpy as jnp
from jax import lax
from jax.experimental import pallas as pl
from jax.experimental.pallas import tpu as pltpu

F32 = jnp.float32
BF16 = jnp.bfloat16

N_DEV = 8
DEPTH = 4
D_MODEL = 1024
HEAD_DIM = 64
LANES = 128
N_HEADS = D_MODEL // HEAD_DIM
SWA_KV_HEADS = 4
ATT_BLOCK = 128
DILATIONS = (1, 4, 16)
CONV_WIDTH = 4
LRU_C = 8.0
FF_HIDDEN = 2816
ALPHA = (2.0 * DEPTH) ** 0.25
LN_EPS = 1e-5
NEG_INF = -1e30
W_F = 5 * D_MODEL
W_QS = D_MODEL + 2 * SWA_KV_HEADS * HEAD_DIM
W_QD = 3 * D_MODEL

ADAM_LR = 0.001
ADAM_B1 = 0.9
ADAM_B2 = 0.999
ADAM_EPS = 1e-08
ADAM_WD = 0.01
ADAM_STEP = 10

VMEM_LIMIT = 56 * 1024 * 1024
MATMUL_BLOCK_BYTES = 40 * 1024 * 1024
MESH = pl.DeviceIdType.MESH


def _pick(n, cands):
    for c in cands:
        if n % c == 0:
            return c
    raise ValueError(f"no tile for {n} among {cands}")


def _params(sem):
    return pltpu.CompilerParams(dimension_semantics=sem, vmem_limit_bytes=VMEM_LIMIT)


def _tile(n, cap):
    best = None
    for t in range(LANES, cap + 1, LANES):
        if n % t == 0:
            best = t
    assert best is not None, (n, cap)
    return best


def matmul(a, b, *, name, trans_a=False, trans_b=False, out_dtype=F32, add=None, add_scale=1.0):
    if trans_a:
        k_dim, m_dim = a.shape
    else:
        m_dim, k_dim = a.shape
    n_dim = b.shape[0] if trans_b else b.shape[1]
    assert (b.shape[1] if trans_b else b.shape[0]) == k_dim
    tn = _tile(n_dim, 1408)
    tm, tk = _tile(m_dim, 1024), _tile(k_dim, 1408)
    for cand in (1024, 512, 256):
        ctm = _tile(m_dim, cand)
        blocks = 2 * (ctm * k_dim * a.dtype.itemsize + tn * k_dim * b.dtype.itemsize + ctm * tn * jnp.dtype(out_dtype).itemsize
                      + (ctm * tn * add.dtype.itemsize if add is not None else 0))
        if blocks <= MATMUL_BLOCK_BYTES:
            tm, tk = ctm, k_dim
            break
    nk = k_dim // tk
    dims = (((0 if trans_a else 1,), (1 if trans_b else 0,)), ((), ()))

    def body(*refs):
        if add is None:
            a_ref, b_ref, o_ref, acc_ref = refs
            add_ref = None
        else:
            a_ref, b_ref, add_ref, o_ref, acc_ref = refs
        k = pl.program_id(2)
        part = lax.dot_general(a_ref[...].astype(BF16), b_ref[...].astype(BF16), dims, preferred_element_type=F32)

        def finish(r):
            if add_ref is not None:
                r = r + add_scale * add_ref[...].astype(F32)
            o_ref[...] = r.astype(out_dtype)

        if nk == 1:
            finish(part)
        else:
            @pl.when(k == 0)
            def _():
                acc_ref[...] = part

            @pl.when((k > 0) & (k < nk - 1))
            def _():
                acc_ref[...] += part

            @pl.when(k == nk - 1)
            def _():
                finish(acc_ref[...] + part)

    a_spec = pl.BlockSpec((tk, tm), lambda i, j, k: (k, i)) if trans_a else pl.BlockSpec((tm, tk), lambda i, j, k: (i, k))
    b_spec = pl.BlockSpec((tn, tk), lambda i, j, k: (j, k)) if trans_b else pl.BlockSpec((tk, tn), lambda i, j, k: (k, j))
    in_specs = [a_spec, b_spec]
    args = [a, b]
    if add is not None:
        in_specs.append(pl.BlockSpec((tm, tn), lambda i, j, k: (i, j)))
        args.append(add)
    return pl.pallas_call(
        body,
        out_shape=jax.ShapeDtypeStruct((m_dim, n_dim), out_dtype),
        grid=(m_dim // tm, n_dim // tn, nk),
        in_specs=in_specs,
        out_specs=pl.BlockSpec((tm, tn), lambda i, j, k: (i, j)),
        scratch_shapes=[pltpu.VMEM((tm, tn) if nk > 1 else (8, LANES), F32)],
        compiler_params=_params(("parallel", "parallel", "arbitrary")),
        name=name,
    )(*args)


def ln_fwd(x, r, g, b, *, name):
    t_dim, d = x.shape
    tr = _pick(t_dim, (256, 128, 8))

    def body(x_ref, r_ref, g_ref, b_ref, y_ref, yb_ref, z_ref):
        z = ALPHA * x_ref[...] + r_ref[...]
        mu = jnp.mean(z, axis=-1, keepdims=True)
        zc = z - mu
        var = jnp.mean(zc * zc, axis=-1, keepdims=True)
        y = zc * lax.rsqrt(var + LN_EPS) * g_ref[...] + b_ref[...]
        y_ref[...] = y
        yb_ref[...] = y.astype(BF16)
        z_ref[...] = z

    row = pl.BlockSpec((tr, d), lambda i: (i, 0))
    vec = pl.BlockSpec((1, d), lambda i: (0, 0))
    return pl.pallas_call(
        body,
        out_shape=(jax.ShapeDtypeStruct((t_dim, d), F32), jax.ShapeDtypeStruct((t_dim, d), BF16), jax.ShapeDtypeStruct((t_dim, d), F32)),
        grid=(t_dim // tr,),
        in_specs=[row, row, vec, vec],
        out_specs=(row, row, row),
        compiler_params=_params(("parallel",)),
        name=name,
    )(x, r, g.reshape(1, d), b.reshape(1, d))


def ln_bwd(dy, z, g, *, name):
    t_dim, d = dy.shape
    tr = _pick(t_dim, (256, 128, 8))

    def body(dy_ref, z_ref, g_ref, dz_ref, dzb_ref, dg_ref, db_ref):
        @pl.when(pl.program_id(0) == 0)
        def _():
            dg_ref[...] = jnp.zeros_like(dg_ref)
            db_ref[...] = jnp.zeros_like(db_ref)

        z = z_ref[...]
        dyv = dy_ref[...]
        mu = jnp.mean(z, axis=-1, keepdims=True)
        zc = z - mu
        var = jnp.mean(zc * zc, axis=-1, keepdims=True)
        rstd = lax.rsqrt(var + LN_EPS)
        xhat = zc * rstd
        dxhat = dyv * g_ref[...]
        m1 = jnp.mean(dxhat, axis=-1, keepdims=True)
        m2 = jnp.mean(dxhat * xhat, axis=-1, keepdims=True)
        dz = rstd * (dxhat - m1 - xhat * m2)
        dz_ref[...] = dz
        dzb_ref[...] = dz.astype(BF16)
        dg_ref[...] += jnp.sum(dyv * xhat, axis=0, keepdims=True)
        db_ref[...] += jnp.sum(dyv, axis=0, keepdims=True)

    row = pl.BlockSpec((tr, d), lambda i: (i, 0))
    vec = pl.BlockSpec((1, d), lambda i: (0, 0))
    return pl.pallas_call(
        body,
        out_shape=(jax.ShapeDtypeStruct((t_dim, d), F32), jax.ShapeDtypeStruct((t_dim, d), BF16),
                   jax.ShapeDtypeStruct((1, d), F32), jax.ShapeDtypeStruct((1, d), F32)),
        grid=(t_dim // tr,),
        in_specs=[row, row, vec],
        out_specs=(row, row, vec, vec),
        compiler_params=_params(("arbitrary",)),
        name=name,
    )(dy, z, g.reshape(1, d))


def loss_head(y, target, *, name):
    t_dim, d = y.shape
    tr = _pick(t_dim, (256, 128, 8))

    def body(y_ref, t_ref, dy_ref, sq_ref):
        @pl.when(pl.program_id(0) == 0)
        def _():
            sq_ref[...] = jnp.zeros_like(sq_ref)

        diff = y_ref[...] - t_ref[...]
        dy_ref[...] = diff / d
        sq_ref[...] += jnp.sum(diff * diff, axis=0, keepdims=True)

    row = pl.BlockSpec((tr, d), lambda i: (i, 0))
    vec = pl.BlockSpec((1, d), lambda i: (0, 0))
    return pl.pallas_call(
        body,
        out_shape=(jax.ShapeDtypeStruct((t_dim, d), F32), jax.ShapeDtypeStruct((1, d), F32)),
        grid=(t_dim // tr,),
        in_specs=[row, row],
        out_specs=(row, vec),
        compiler_params=_params(("arbitrary",)),
        name=name,
    )(y, target)


def _sigmoid(x):
    return 0.5 * jnp.tanh(0.5 * x) + 0.5


def ffn_in_swiglu(x, w, *, name):
    t_dim, d = x.shape
    f = w.shape[1] // 2
    tm, tn = _tile(t_dim, 1024), _tile(f, 1408)
    nf = f // tn

    def body(x_ref, w1_ref, w3_ref, h1_ref, h3_ref, act_ref):
        xv = x_ref[...]
        h1 = jnp.dot(xv, w1_ref[...], preferred_element_type=F32)
        h3 = jnp.dot(xv, w3_ref[...], preferred_element_type=F32)
        h1_ref[...] = h1.astype(BF16)
        h3_ref[...] = h3.astype(BF16)
        act_ref[...] = (h1 * _sigmoid(h1) * h3).astype(BF16)

    out = pl.BlockSpec((tm, tn), lambda i, j: (i, j))
    return pl.pallas_call(
        body,
        out_shape=(jax.ShapeDtypeStruct((t_dim, f), BF16),) * 3,
        grid=(t_dim // tm, nf),
        in_specs=[pl.BlockSpec((tm, d), lambda i, j: (i, 0)), pl.BlockSpec((d, tn), lambda i, j: (0, j)),
                  pl.BlockSpec((d, tn), lambda i, j: (0, j + nf))],
        out_specs=(out, out, out),
        compiler_params=_params(("parallel", "parallel")),
        name=name,
    )(x, w, w)


def swiglu_bwd(dact, h1, h3, *, name):
    t_dim, f = h1.shape
    tr = _pick(t_dim, (256, 128, 8))

    def body(da_ref, h1_ref, h3_ref, dh_ref):
        h1 = h1_ref[...].astype(F32)
        da = da_ref[...].astype(F32)
        sg = _sigmoid(h1)
        dh_ref[:, :f] = (da * h3_ref[...].astype(F32) * sg * (1.0 + h1 * (1.0 - sg))).astype(BF16)
        dh_ref[:, f:] = (da * h1 * sg).astype(BF16)

    return pl.pallas_call(
        body,
        out_shape=jax.ShapeDtypeStruct((t_dim, 2 * f), BF16),
        grid=(t_dim // tr,),
        in_specs=[pl.BlockSpec((tr, f), lambda i: (i, 0))] * 3,
        out_specs=pl.BlockSpec((tr, 2 * f), lambda i: (i, 0)),
        compiler_params=_params(("parallel",)),
        name=name,
    )(dact, h1, h3)


def branch_merge(ys, w_branch, proj_f, *, name):
    t_dim, d = ys[0].shape
    tm = _tile(t_dim, 512)

    def body(y0, y1, y2, w_ref, g0, g1, g2, m_ref, b0, b1, b2):
        acc = None
        for n, (y, g, b) in enumerate(((y0, g0, b0), (y1, g1, b1), (y2, g2, b2))):
            br = jnp.dot(y[...], w_ref[n], preferred_element_type=F32)
            b[...] = br.astype(BF16)
            t = _sigmoid(g[...]) * br
            acc = t if acc is None else acc + t
        m_ref[...] = acc.astype(BF16)

    row = pl.BlockSpec((tm, d), lambda i: (i, 0))
    gate = [pl.BlockSpec((tm, d), functools.partial(lambda n, i: (i, 2 + n), n)) for n in range(3)]
    merged, *br = pl.pallas_call(
        body,
        out_shape=(jax.ShapeDtypeStruct((t_dim, d), BF16),) * 4,
        grid=(t_dim // tm,),
        in_specs=[row, row, row, pl.BlockSpec((3, d, d), lambda i: (0, 0, 0))] + gate,
        out_specs=(row, row, row, row),
        compiler_params=_params(("parallel",)),
        name=name,
    )(*ys, w_branch, proj_f, proj_f, proj_f)
    return merged, br


def merge_bwd(dmerged, proj_f, br, *, name):
    t_dim, d = dmerged.shape
    tr = _pick(t_dim, (256, 128, 8))

    def body(dm_ref, g0, g1, g2, b0, b1, b2, d0, d1, d2, dg_ref):
        dm = dm_ref[...]
        for n, (g, b, o) in enumerate(((g0, b0, d0), (g1, b1, d1), (g2, b2, d2))):
            sg = _sigmoid(g[...])
            o[...] = (dm * sg).astype(BF16)
            dg_ref[:, n * d:(n + 1) * d] = (dm * b[...].astype(F32) * sg * (1.0 - sg)).astype(BF16)

    row = pl.BlockSpec((tr, d), lambda i: (i, 0))
    gate = [pl.BlockSpec((tr, d), functools.partial(lambda n, i: (i, 2 + n), n)) for n in range(3)]
    return pl.pallas_call(
        body,
        out_shape=(jax.ShapeDtypeStruct((t_dim, d), BF16),) * 3 + (jax.ShapeDtypeStruct((t_dim, 3 * d), BF16),),
        grid=(t_dim // tr,),
        in_specs=[row] + gate + [row, row, row],
        out_specs=(row, row, row, pl.BlockSpec((tr, 3 * d), lambda i: (i, 0))),
        compiler_params=_params(("parallel",)),
        name=name,
    )(dmerged, proj_f, proj_f, proj_f, *br)


GELU_C = math.sqrt(2.0 / math.pi)
PAD = 8
SCAN_TILES = 8


def _gelu(x):
    return 0.5 * x * (1.0 + jnp.tanh(GELU_C * (x + 0.044715 * x * x * x)))


def _gelu_grad(x):
    t = jnp.tanh(GELU_C * (x + 0.044715 * x * x * x))
    return 0.5 * (1.0 + t) + 0.5 * x * (1.0 - t * t) * GELU_C * (1.0 + 3.0 * 0.044715 * x * x)


def _neg_expm1(x, exp_x):
    series = -x * (1.0 + x * (0.5 + x * (1.0 / 6.0)))
    return jnp.where(x > -0.02, series, 1.0 - exp_x)


def _lru_gates(xv, cw_ref, cb_ref, wr_ref, wi_ref, br_ref, bi_ref, lam_ref, pad_ref, s_len):
    pad_ref[pl.ds(0, PAD), :] = jnp.zeros((PAD, LANES), F32)
    pad_ref[pl.ds(PAD, s_len), :] = xv
    xc = cb_ref[...] + jnp.zeros((s_len, LANES), F32)
    for j in range(CONV_WIDTH):
        xc = xc + pad_ref[pl.ds(PAD - (CONV_WIDTH - 1) + j, s_len), :] * cw_ref[pl.ds(j, 1), :]
    xcb = xc.astype(BF16)
    r = _sigmoid(jnp.dot(xcb, wr_ref[0].astype(BF16), preferred_element_type=F32) + br_ref[...])
    i = _sigmoid(jnp.dot(xcb, wi_ref[0].astype(BF16), preferred_element_type=F32) + bi_ref[...])
    nl = -lam_ref[...]
    sp = jnp.maximum(nl, 0.0) + jnp.log(1.0 + jnp.exp(-jnp.abs(nl)))
    log_a = -LRU_C * r * sp
    a = jnp.exp(log_a)
    mult = jnp.sqrt(_neg_expm1(2.0 * log_a, a * a))
    return xc, r, i, sp, a, mult


def _tile_scan(a, b, row, reverse):
    for s in (1, 2, 4):
        if reverse:
            a_sh = pltpu.roll(a, 8 - s, 0)
            b_sh = pltpu.roll(b, 8 - s, 0)
            m = row + s <= 7
        else:
            a_sh = pltpu.roll(a, s, 0)
            b_sh = pltpu.roll(b, s, 0)
            m = row >= s
        b = jnp.where(m, a * b_sh + b, b)
        a = jnp.where(m, a * a_sh, a)
    return a, b


def lru_fwd(proj_f, conv_w, conv_b, wr_bd, wi_bd, b_rg, b_ig, lam, *, name, carried=None):
    bsz, s_len, _ = proj_f.shape
    d = D_MODEL
    ncb = d // LANES
    n_tiles = s_len // 8

    def body(x_ref, g_ref, cw_ref, cb_ref, wr_ref, wi_ref, br_ref, bi_ref, lam_ref, y_ref, h_ref, pad_ref, a_s, b_s):
        xc, r, i, sp, a, mult = _lru_gates(x_ref[0], cw_ref, cb_ref, wr_ref, wi_ref, br_ref, bi_ref, lam_ref, pad_ref, s_len)
        a_s[...] = a
        b_s[...] = mult * (i * xc)
        row = lax.broadcasted_iota(jnp.int32, (8, LANES), 0)

        def tiles(t, carry):
            starts = [pl.multiple_of((t * SCAN_TILES + u) * 8, 8) for u in range(SCAN_TILES)]
            local = [_tile_scan(a_s[pl.ds(i0, 8), :], b_s[pl.ds(i0, 8), :], row, False) for i0 in starts]
            for i0, (ac, hl) in zip(starts, local):
                h = hl + ac * carry
                h_ref[0, pl.ds(i0, 8), :] = h
                carry = jnp.broadcast_to(h[7:8, :], (8, LANES))
            return carry

        lax.fori_loop(0, n_tiles // SCAN_TILES, tiles, jnp.zeros((8, LANES), F32))
        y_ref[0] = (h_ref[0] * _gelu(g_ref[0])).astype(BF16)

    slab = lambda off: pl.BlockSpec((1, s_len, LANES), functools.partial(lambda o, c, b: (b, 0, o + c), off))
    vec = pl.BlockSpec((1, LANES), lambda c, b: (0, c))
    mat = pl.BlockSpec((1, LANES, LANES), lambda c, b: (c, 0, 0))
    out = pl.BlockSpec((1, s_len, LANES), lambda c, b: (b, 0, c))
    return call_with_exchange(
        body, carried,
        out_shape=(jax.ShapeDtypeStruct((bsz, s_len, d), BF16), jax.ShapeDtypeStruct((bsz, s_len, d), F32)),
        grid=(ncb, bsz),
        in_specs=[slab(0), slab(ncb), pl.BlockSpec((CONV_WIDTH, LANES), lambda c, b: (0, c)), vec, mat, mat, vec, vec, vec],
        out_specs=(out, out),
        scratch_shapes=[pltpu.VMEM((s_len + 2 * PAD, LANES), F32), pltpu.VMEM((s_len, LANES), F32), pltpu.VMEM((s_len, LANES), F32)],
        name=name,
        args=(proj_f, proj_f, conv_w, conv_b.reshape(1, d), wr_bd, wi_bd, b_rg.reshape(1, d), b_ig.reshape(1, d), lam.reshape(1, d)))


def lru_bwd(dy, proj_f, h, conv_w, conv_b, wr_bd, wi_bd, wr_bd_t, wi_bd_t, b_rg, b_ig, lam, *, name, carried=None):
    bsz, s_len, _ = proj_f.shape
    d = D_MODEL
    ncb = d // LANES
    n_tiles = s_len // 8

    def body(dy_ref, x_ref, g_ref, h_ref, cw_ref, cb_ref, wr_ref, wi_ref, wrt_ref, wit_ref, br_ref, bi_ref, lam_ref,
             dx_ref, dg_ref, dcw_ref, dcb_ref, dbr_ref, dbi_ref, dlam_ref, dwr_ref, dwi_ref, pad_ref, a_s, b_s, l_s):
        @pl.when(pl.program_id(1) == 0)
        def _():
            for ref in (dcw_ref, dcb_ref, dbr_ref, dbi_ref, dlam_ref, dwr_ref, dwi_ref):
                ref[...] = jnp.zeros_like(ref)

        xc, r, i, sp, a, mult = _lru_gates(x_ref[0], cw_ref, cb_ref, wr_ref, wi_ref, br_ref, bi_ref, lam_ref, pad_ref, s_len)
        gate = g_ref[0]
        hv = h_ref[0]
        dyv = dy_ref[0].astype(F32)
        dg_ref[0] = (dyv * hv * _gelu_grad(gate)).astype(BF16)
        b_s[...] = dyv * _gelu(gate)
        l_s[pl.ds(0, s_len), :] = a
        l_s[pl.ds(s_len, PAD), :] = jnp.zeros((PAD, LANES), F32)
        a_s[...] = l_s[pl.ds(1, s_len), :]
        row = lax.broadcasted_iota(jnp.int32, (8, LANES), 0)

        def tiles(t, carry):
            starts = [pl.multiple_of((n_tiles - 1 - (t * SCAN_TILES + u)) * 8, 8) for u in range(SCAN_TILES)]
            local = [_tile_scan(a_s[pl.ds(i0, 8), :], b_s[pl.ds(i0, 8), :], row, True) for i0 in starts]
            for i0, (ac, ll) in zip(starts, local):
                lmb = ll + ac * carry
                b_s[pl.ds(i0, 8), :] = lmb
                carry = jnp.broadcast_to(lmb[0:1, :], (8, LANES))
            return carry

        lax.fori_loop(0, n_tiles // SCAN_TILES, tiles, jnp.zeros((8, LANES), F32))
        lmb = b_s[...]
        l_s[pl.ds(0, PAD), :] = jnp.zeros((PAD, LANES), F32)
        l_s[pl.ds(PAD, s_len), :] = hv
        h_prev = l_s[pl.ds(PAD - 1, s_len), :]
        da = lmb * h_prev
        dmult = lmb * (i * xc)
        di = lmb * mult * xc
        dxc = lmb * mult * i
        dlog_a = da * a - dmult * a * a / mult
        dr = -LRU_C * sp * dlog_a
        dsp = jnp.sum(-LRU_C * r * dlog_a, axis=0, keepdims=True)
        dlam_ref[...] += dsp * (-_sigmoid(-lam_ref[...]))
        dpr = dr * r * (1.0 - r)
        dpi = di * i * (1.0 - i)
        dprb = dpr.astype(BF16)
        dpib = dpi.astype(BF16)
        xcb = xc.astype(BF16)
        dbr_ref[...] += jnp.sum(dpr, axis=0, keepdims=True)
        dbi_ref[...] += jnp.sum(dpi, axis=0, keepdims=True)
        tn = (((0,), (0,)), ((), ()))
        dwr_ref[0] += lax.dot_general(xcb, dprb, tn, preferred_element_type=F32)
        dwi_ref[0] += lax.dot_general(xcb, dpib, tn, preferred_element_type=F32)
        dxc = (dxc + jnp.dot(dprb, wrt_ref[0].astype(BF16), preferred_element_type=F32)
               + jnp.dot(dpib, wit_ref[0].astype(BF16), preferred_element_type=F32))
        dcb_ref[...] += jnp.sum(dxc, axis=0, keepdims=True)
        for j in range(CONV_WIDTH):
            dcw_ref[pl.ds(j, 1), :] += jnp.sum(dxc * pad_ref[pl.ds(PAD - (CONV_WIDTH - 1) + j, s_len), :], axis=0, keepdims=True)
        l_s[pl.ds(0, s_len), :] = dxc
        l_s[pl.ds(s_len, PAD), :] = jnp.zeros((PAD, LANES), F32)
        dx = jnp.zeros((s_len, LANES), F32)
        for j in range(CONV_WIDTH):
            dx = dx + l_s[pl.ds(CONV_WIDTH - 1 - j, s_len), :] * cw_ref[pl.ds(j, 1), :]
        dx_ref[0] = dx.astype(BF16)

    slab = lambda off: pl.BlockSpec((1, s_len, LANES), functools.partial(lambda o, c, b: (b, 0, o + c), off))
    vec = pl.BlockSpec((1, LANES), lambda c, b: (0, c))
    mat = pl.BlockSpec((1, LANES, LANES), lambda c, b: (c, 0, 0))
    cw = pl.BlockSpec((CONV_WIDTH, LANES), lambda c, b: (0, c))
    out = pl.BlockSpec((1, s_len, LANES), lambda c, b: (b, 0, c))
    vshape = jax.ShapeDtypeStruct((1, d), F32)
    mshape = jax.ShapeDtypeStruct((ncb, LANES, LANES), F32)
    return call_with_exchange(
        body, carried,
        out_shape=(jax.ShapeDtypeStruct((bsz, s_len, d), BF16),) * 2
        + (jax.ShapeDtypeStruct((CONV_WIDTH, d), F32), vshape, vshape, vshape, vshape, mshape, mshape),
        grid=(ncb, bsz),
        in_specs=[out, slab(0), slab(ncb), out, cw, vec, mat, mat, mat, mat, vec, vec, vec],
        out_specs=(out, out, cw, vec, vec, vec, vec, mat, mat),
        scratch_shapes=[pltpu.VMEM((s_len + 2 * PAD, LANES), F32), pltpu.VMEM((s_len, LANES), F32), pltpu.VMEM((s_len, LANES), F32),
                        pltpu.VMEM((s_len + 2 * PAD, LANES), F32)],
        name=name,
        args=(dy, proj_f, proj_f, h, conv_w, conv_b.reshape(1, d), wr_bd, wi_bd, wr_bd_t, wi_bd_t,
              b_rg.reshape(1, d), b_ig.reshape(1, d), lam.reshape(1, d)))


NT = (((1,), (1,)), ((), ()))
TN = (((0,), (0,)), ((), ()))
ATT_SCALE = HEAD_DIM ** -0.5


def _kv_place(head, n_kv_heads):
    kv = head // (N_HEADS // n_kv_heads)
    return kv // 2, kv % 2


def _band_mask(n, single):
    nk = ATT_BLOCK if single else 2 * ATT_BLOCK
    qi = lax.broadcasted_iota(jnp.int32, (2 * ATT_BLOCK, nk), 0) % ATT_BLOCK
    kj = lax.broadcasted_iota(jnp.int32, (2 * ATT_BLOCK, nk), 1)
    if single:
        return qi >= kj
    rel = qi + ATT_BLOCK - kj
    return (rel >= 0) & (rel <= ATT_BLOCK) & ((n > 0) | (kj >= ATT_BLOCK))


def _lane_halves():
    lane = lax.broadcasted_iota(jnp.int32, (1, LANES), 1)
    return lane < HEAD_DIM


def _stack_heads(t2, kh):
    first = _lane_halves()
    parts = []
    for a in range(2):
        ta = jnp.where(first if a == 0 else ~first, t2, jnp.zeros_like(t2))
        if a != kh[a]:
            ta = pltpu.roll(ta, HEAD_DIM, 1)
        parts.append(ta)
    return jnp.concatenate(parts, axis=0)


def _fold_heads(t, kh):
    t0, t1 = t[:ATT_BLOCK], t[ATT_BLOCK:]
    if t.shape[1] == LANES:
        if kh[0] != 0:
            t0 = pltpu.roll(t0, HEAD_DIM, 1)
        if kh[1] != 1:
            t1 = pltpu.roll(t1, HEAD_DIM, 1)
    return jnp.where(_lane_halves(), t0, t1)


def _rows_of_heads(t2):
    return jnp.concatenate([t2[:, 0:1], t2[:, HEAD_DIM:HEAD_DIM + 1]], axis=0)


PAIRS_AT_ONCE = 4


def _fill_bias(bias2_ref, bias1_ref=None):
    for i in range(2):
        bias2_ref[i] = jnp.where(_band_mask(i, False), 0.0, NEG_INF)
    if bias1_ref is not None:
        bias1_ref[...] = jnp.where(_band_mask(0, True), 0.0, NEG_INF)


def _pairs_fwd(items):
    ss = [lax.dot_general(_stack_heads(q2 * ATT_SCALE, kh), kk, NT, preferred_element_type=F32) + bias
          for q2, kk, _, bias, kh, _ in items]
    ps, ms, ls = [], [], []
    for s, (_, _, _, _, _, sink_col) in zip(ss, items):
        m = jnp.max(s, axis=-1, keepdims=True)
        if sink_col is not None:
            m = jnp.maximum(m, sink_col)
        p = jnp.exp(s - m)
        l = jnp.sum(p, axis=-1, keepdims=True)
        if sink_col is not None:
            l = l + jnp.exp(sink_col - m)
        ps.append(p.astype(BF16))
        ms.append(m)
        ls.append(l)
    pvs = [jnp.dot(p, it[2], preferred_element_type=F32) for p, it in zip(ps, items)]
    return list(zip(pvs, ms, ls))


def _pairs_bwd(items):
    first = _lane_halves()
    pre = []
    for q2, kk, vv, do2, o2, lse2, bias, kh in items:
        dd = do2 * o2
        dsum = jnp.concatenate([jnp.sum(jnp.where(first, dd, 0.0), axis=-1, keepdims=True),
                                jnp.sum(jnp.where(first, 0.0, dd), axis=-1, keepdims=True)], axis=0)
        qs = _stack_heads(q2 * ATT_SCALE, kh)
        dos = _stack_heads(do2.astype(BF16), kh)
        s = lax.dot_general(qs, kk, NT, preferred_element_type=F32) + bias
        dp = lax.dot_general(dos, vv, NT, preferred_element_type=F32)
        pre.append((qs, dos, s, dp, dsum))
    mid = []
    for (qs, dos, s, dp, dsum), it in zip(pre, items):
        p = jnp.exp(s - _rows_of_heads(it[5]))
        mid.append((p.astype(BF16), (p * (dp - dsum)).astype(BF16)))
    out = []
    for (pb, ds), (qs, dos, _, _, dsum), it in zip(mid, pre, items):
        dq = _fold_heads(jnp.dot(ds, it[1], preferred_element_type=F32), it[7]) * ATT_SCALE
        dk = lax.dot_general(ds, qs, TN, preferred_element_type=F32)
        dv = lax.dot_general(pb, dos, TN, preferred_element_type=F32)
        out.append((dq, dk, dv, dsum))
    return out


def swa_fwd(qkv, sinks, *, name, carried=None):
    bsz, s_len, width = qkv.shape
    ckv = SWA_KV_HEADS * HEAD_DIM
    nb = s_len // ATT_BLOCK
    kblk = D_MODEL // ckv

    def body(sink_ref, q_ref, kp_ref, kc_ref, vp_ref, vc_ref, o_ref, lse_ref, ob_ref, bias2):
        n = pl.program_id(1)
        _fill_bias(bias2)
        bias = bias2[jnp.minimum(n, 1)]
        kk = jnp.concatenate([kp_ref[0], kc_ref[0]], axis=0)
        vv = jnp.concatenate([vp_ref[0], vc_ref[0]], axis=0)
        top = lax.broadcasted_iota(jnp.int32, (2 * ATT_BLOCK, 1), 0) < ATT_BLOCK
        for hp0 in range(0, N_HEADS // 2, PAIRS_AT_ONCE):
            items, places = [], []
            for hp in range(hp0, hp0 + PAIRS_AT_ONCE):
                cols = slice(hp * LANES, (hp + 1) * LANES)
                kb, kh = _kv_place(2 * hp, SWA_KV_HEADS)
                kcols = slice(kb * LANES, (kb + 1) * LANES)
                sink_col = jnp.where(top, sink_ref[2 * hp], sink_ref[2 * hp + 1])
                items.append((q_ref[0, :, cols], kk[:, kcols], vv[:, kcols], bias, (kh, kh), sink_col))
                places.append((cols, (kh, kh)))
            for (pv, m, l), (cols, kh2) in zip(_pairs_fwd(items), places):
                o2 = _fold_heads(pv / l, kh2)
                o_ref[0, :, cols] = o2
                ob_ref[0, :, cols] = o2.astype(BF16)
                lse_ref[0, :, cols] = _fold_heads(m + jnp.log(l), kh2)

    prev = lambda n: jnp.maximum(n - 1, 0)
    out = pl.BlockSpec((1, ATT_BLOCK, D_MODEL), lambda b, n: (b, n, 0))
    sd = lambda dt: jax.ShapeDtypeStruct((bsz, s_len, D_MODEL), dt)
    return call_with_exchange(
        body, carried,
        out_shape=(sd(F32), sd(F32), sd(BF16)),
        grid=(bsz, nb),
        in_specs=[pl.BlockSpec(memory_space=pltpu.SMEM), out,
                  pl.BlockSpec((1, ATT_BLOCK, ckv), lambda b, n: (b, prev(n), kblk)),
                  pl.BlockSpec((1, ATT_BLOCK, ckv), lambda b, n: (b, n, kblk)),
                  pl.BlockSpec((1, ATT_BLOCK, ckv), lambda b, n: (b, prev(n), kblk + 1)),
                  pl.BlockSpec((1, ATT_BLOCK, ckv), lambda b, n: (b, n, kblk + 1))],
        out_specs=(out, out, out),
        scratch_shapes=[pltpu.VMEM((2, 2 * ATT_BLOCK, 2 * ATT_BLOCK), F32)],
        name=name,
        args=(sinks, qkv, qkv, qkv, qkv, qkv))


def swa_bwd(qkv, sinks, o, lse, do, *, name, carried=None):
    bsz, s_len, width = qkv.shape
    ckv = SWA_KV_HEADS * HEAD_DIM
    nb = s_len // ATT_BLOCK
    kblk = D_MODEL // ckv

    def body(sink_ref, q_ref, kp_ref, kc_ref, vp_ref, vc_ref, o_ref, lse_ref, do_ref, dq_ref, dk_ref, dv_ref, dsink_ref,
             dkk, dvv, ck, cv, bias2):
        n = pl.program_id(1)

        @pl.when((n == 0) & (pl.program_id(0) == 0))
        def _():
            dsink_ref[...] = jnp.zeros_like(dsink_ref)

        @pl.when(n < nb)
        def _():
            top = lax.broadcasted_iota(jnp.int32, (2 * ATT_BLOCK, 1), 0) < ATT_BLOCK
            lane = lax.broadcasted_iota(jnp.int32, dsink_ref.shape, 1)
            first_row = lax.broadcasted_iota(jnp.int32, dsink_ref.shape, 0) == 0
            _fill_bias(bias2)
            bias = bias2[jnp.minimum(n, 1)]
            kk = jnp.concatenate([kp_ref[0], kc_ref[0]], axis=0)
            vv = jnp.concatenate([vp_ref[0], vc_ref[0]], axis=0)
            dkk[...] = jnp.zeros_like(dkk)
            dvv[...] = jnp.zeros_like(dvv)
            for hp0 in range(0, N_HEADS // 2, PAIRS_AT_ONCE):
                items, places = [], []
                for hp in range(hp0, hp0 + PAIRS_AT_ONCE):
                    cols = slice(hp * LANES, (hp + 1) * LANES)
                    kb, kh = _kv_place(2 * hp, SWA_KV_HEADS)
                    kcols = slice(kb * LANES, (kb + 1) * LANES)
                    items.append((q_ref[0, :, cols], kk[:, kcols], vv[:, kcols], do_ref[0, :, cols], o_ref[0, :, cols],
                                  lse_ref[0, :, cols], bias, (kh, kh)))
                    places.append((cols, kcols, hp))
                for (dq, dk, dv, dsum), (cols, kcols, hp) in zip(_pairs_bwd(items), places):
                    dq_ref[0, :, cols] = dq
                    dkk[:, kcols] += dk
                    dvv[:, kcols] += dv
                    sink_col = jnp.where(top, sink_ref[2 * hp], sink_ref[2 * hp + 1])
                    t = -jnp.exp(sink_col - _rows_of_heads(lse_ref[0, :, cols])) * dsum
                    d0 = jnp.sum(t[:ATT_BLOCK], axis=0, keepdims=True)
                    d1 = jnp.sum(t[ATT_BLOCK:], axis=0, keepdims=True)
                    dsink_ref[...] += jnp.where(first_row & (lane == 2 * hp), d0, 0.0) + jnp.where(first_row & (lane == 2 * hp + 1), d1, 0.0)

        @pl.when((n >= 1) & (n < nb))
        def _():
            dk_ref[0] = ck[...] + dkk[pl.ds(0, ATT_BLOCK), :]
            dv_ref[0] = cv[...] + dvv[pl.ds(0, ATT_BLOCK), :]

        @pl.when(n == nb)
        def _():
            dk_ref[0] = ck[...]
            dv_ref[0] = cv[...]

        @pl.when(n < nb)
        def _():
            ck[...] = dkk[pl.ds(ATT_BLOCK, ATT_BLOCK), :]
            cv[...] = dvv[pl.ds(ATT_BLOCK, ATT_BLOCK), :]

    clamp = lambda n: jnp.minimum(n, nb - 1)
    prev = lambda n: jnp.maximum(n - 1, 0)
    row = pl.BlockSpec((1, ATT_BLOCK, D_MODEL), lambda b, n: (b, clamp(n), 0))
    kv_out = pl.BlockSpec((1, ATT_BLOCK, ckv), lambda b, n: (b, prev(n), 0))
    return call_with_exchange(
        body, carried,
        out_shape=(jax.ShapeDtypeStruct((bsz, s_len, D_MODEL), F32), jax.ShapeDtypeStruct((bsz, s_len, ckv), F32),
                   jax.ShapeDtypeStruct((bsz, s_len, ckv), F32), jax.ShapeDtypeStruct((8, LANES), F32)),
        grid=(bsz, nb + 1),
        in_specs=[pl.BlockSpec(memory_space=pltpu.SMEM), row,
                  pl.BlockSpec((1, ATT_BLOCK, ckv), lambda b, n: (b, prev(clamp(n)), kblk)),
                  pl.BlockSpec((1, ATT_BLOCK, ckv), lambda b, n: (b, clamp(n), kblk)),
                  pl.BlockSpec((1, ATT_BLOCK, ckv), lambda b, n: (b, prev(clamp(n)), kblk + 1)),
                  pl.BlockSpec((1, ATT_BLOCK, ckv), lambda b, n: (b, clamp(n), kblk + 1)),
                  row, row, row],
        out_specs=(row, kv_out, kv_out, pl.BlockSpec((8, LANES), lambda b, n: (0, 0))),
        scratch_shapes=[pltpu.VMEM((2 * ATT_BLOCK, ckv), F32), pltpu.VMEM((2 * ATT_BLOCK, ckv), F32),
                        pltpu.VMEM((ATT_BLOCK, ckv), F32), pltpu.VMEM((ATT_BLOCK, ckv), F32),
                        pltpu.VMEM((2, 2 * ATT_BLOCK, 2 * ATT_BLOCK), F32)],
        name=name,
        args=(sinks, qkv, qkv, qkv, qkv, qkv, o, lse, do))


DIL_PATTERNS = tuple((d, 2048 // d // ATT_BLOCK) for d in reversed(DILATIONS))
MHA = (0, 1)


def _dil_rows(idx, d, nb):
    j = idx // nb
    n = idx % nb
    base = j + n * (ATT_BLOCK * d)
    prev = jnp.maximum(base - ATT_BLOCK * d, j)
    if d == 1:
        return n, pl.ds(pl.multiple_of(base, ATT_BLOCK), ATT_BLOCK), pl.ds(pl.multiple_of(prev, ATT_BLOCK), ATT_BLOCK)
    return n, pl.ds(base, ATT_BLOCK, stride=d), pl.ds(prev, ATT_BLOCK, stride=d)


def dil_fwd(qkv, *, name, carried=None):
    bsz, s_len, _ = qkv.shape
    assert s_len == DIL_PATTERNS[0][0] * DIL_PATTERNS[0][1] * ATT_BLOCK
    npair = N_HEADS // 2

    def body(q_ref, k_ref, v_ref, y_ref, lse_ref, yb_ref, m_acc, l_acc, bias2, bias1):
        _fill_bias(bias2, bias1)
        for ci, (d, nb) in enumerate(DIL_PATTERNS):
            single = nb == 1

            def blocks(it, carry):
                items, places = [], []
                for u in range(PAIRS_AT_ONCE):
                    n, rows, prows = _dil_rows(it * PAIRS_AT_ONCE + u, d, nb)
                    kc = k_ref[rows, :].astype(BF16)
                    vc = v_ref[rows, :].astype(BF16)
                    if single:
                        kk, vv, bias = kc, vc, bias1[...]
                    else:
                        kk = jnp.concatenate([k_ref[prows, :].astype(BF16), kc], axis=0)
                        vv = jnp.concatenate([v_ref[prows, :].astype(BF16), vc], axis=0)
                        bias = bias2[jnp.minimum(n, 1)]
                    items.append((q_ref[rows, :].astype(BF16), kk, vv, bias, MHA, None))
                    places.append(rows)
                for (pv, m, l), rows in zip(_pairs_fwd(items), places):
                    o2, m2, l2 = _fold_heads(pv, MHA), _fold_heads(m, MHA), _fold_heads(l, MHA)
                    if ci == 0:
                        y_ref[rows, :] = o2
                        m_acc[rows, :] = m2
                        l_acc[rows, :] = l2
                    else:
                        m_old = m_acc[rows, :]
                        m_new = jnp.maximum(m_old, m2)
                        w_old = jnp.exp(m_old - m_new)
                        w_new = jnp.exp(m2 - m_new)
                        y_ref[rows, :] = y_ref[rows, :] * w_old + o2 * w_new
                        l_acc[rows, :] = l_acc[rows, :] * w_old + l2 * w_new
                        m_acc[rows, :] = m_new
                return carry

            lax.fori_loop(0, d * nb // PAIRS_AT_ONCE, blocks, 0)
        y = y_ref[...] / l_acc[...]
        y_ref[...] = y
        yb_ref[...] = y.astype(BF16)
        lse_ref[...] = m_acc[...] + jnp.log(l_acc[...])

    slab = lambda off: pl.BlockSpec((None, s_len, LANES), functools.partial(lambda o, b, h: (b, 0, o + h), off))
    sd = lambda dt: jax.ShapeDtypeStruct((bsz, s_len, D_MODEL), dt)
    return call_with_exchange(
        body, carried,
        out_shape=(sd(F32), sd(F32), sd(BF16)),
        grid=(bsz, npair),
        in_specs=[slab(0), slab(npair), slab(2 * npair)],
        out_specs=(slab(0), slab(0), slab(0)),
        scratch_shapes=[pltpu.VMEM((s_len, LANES), F32), pltpu.VMEM((s_len, LANES), F32),
                        pltpu.VMEM((2, 2 * ATT_BLOCK, 2 * ATT_BLOCK), F32), pltpu.VMEM((2 * ATT_BLOCK, ATT_BLOCK), F32)],
        name=name,
        args=(qkv, qkv, qkv))


def dil_bwd(qkv, y, lse, dy, *, name, carried=None):
    bsz, s_len, _ = qkv.shape
    npair = N_HEADS // 2

    def body(q_ref, k_ref, v_ref, y_ref, lse_ref, dy_ref, dq_ref, dk_ref, dv_ref, bias2, bias1):
        _fill_bias(bias2, bias1)
        assert DIL_PATTERNS[0][1] == 1
        for d, nb in DIL_PATTERNS:
            single = nb == 1

            def blocks(it, carry):
                items, places = [], []
                for u in range(PAIRS_AT_ONCE):
                    n, rows, prows = _dil_rows(it * PAIRS_AT_ONCE + u, d, nb)
                    kc = k_ref[rows, :].astype(BF16)
                    vc = v_ref[rows, :].astype(BF16)
                    if single:
                        kk, vv, bias = kc, vc, bias1[...]
                    else:
                        kk = jnp.concatenate([k_ref[prows, :].astype(BF16), kc], axis=0)
                        vv = jnp.concatenate([v_ref[prows, :].astype(BF16), vc], axis=0)
                        bias = bias2[jnp.minimum(n, 1)]
                    items.append((q_ref[rows, :].astype(BF16), kk, vv, dy_ref[rows, :], y_ref[rows, :], lse_ref[rows, :], bias, MHA))
                    places.append((rows, prows))
                for (dq, dk, dv, _), (rows, prows) in zip(_pairs_bwd(items), places):
                    if single:
                        dq_ref[rows, :] = dq
                        dk_ref[rows, :] = dk
                        dv_ref[rows, :] = dv
                    else:
                        dq_ref[rows, :] += dq
                        dk_ref[prows, :] += dk[:ATT_BLOCK]
                        dv_ref[prows, :] += dv[:ATT_BLOCK]
                        dk_ref[rows, :] += dk[ATT_BLOCK:]
                        dv_ref[rows, :] += dv[ATT_BLOCK:]
                return carry

            lax.fori_loop(0, d * nb // PAIRS_AT_ONCE, blocks, 0)

    slab = lambda off: pl.BlockSpec((None, s_len, LANES), functools.partial(lambda o, b, h: (b, 0, o + h), off))
    sd = jax.ShapeDtypeStruct((bsz, s_len, D_MODEL), F32)
    return call_with_exchange(
        body, carried,
        out_shape=(sd, sd, sd),
        grid=(bsz, npair),
        in_specs=[slab(0), slab(npair), slab(2 * npair), slab(0), slab(0), slab(0)],
        out_specs=(slab(0), slab(0), slab(0)),
        scratch_shapes=[pltpu.VMEM((2, 2 * ATT_BLOCK, 2 * ATT_BLOCK), F32), pltpu.VMEM((2 * ATT_BLOCK, ATT_BLOCK), F32)],
        name=name,
        args=(qkv, qkv, qkv, y, lse, dy))


def adamw(w, g, m, v, *, name):
    rows, cols = w.shape
    tr = _pick(rows, (256, 128, 64, 32, 16, 8))

    def body(w_ref, g_ref, m_ref, v_ref, d_ref, nm_ref, nv_ref):
        gv = g_ref[...]
        nm = ADAM_B1 * m_ref[...] + (1.0 - ADAM_B1) * gv
        nv = ADAM_B2 * v_ref[...] + (1.0 - ADAM_B2) * (gv * gv)
        m_hat = nm / (1.0 - ADAM_B1 ** ADAM_STEP)
        v_hat = nv / (1.0 - ADAM_B2 ** ADAM_STEP)
        d_ref[...] = -ADAM_LR * (m_hat / (jnp.sqrt(v_hat) + ADAM_EPS) + ADAM_WD * w_ref[...])
        nm_ref[...] = nm
        nv_ref[...] = nv

    row = pl.BlockSpec((tr, cols), lambda i: (i, 0))
    return pl.pallas_call(
        body,
        out_shape=(jax.ShapeDtypeStruct((rows, cols), F32),) * 3,
        grid=(rows // tr,),
        in_specs=[row] * 4,
        out_specs=(row, row, row),
        compiler_params=_params(("parallel",)),
        name=name,
    )(w, g, m, v)


def _place():
    return lax.axis_index("x"), lax.axis_index("y"), lax.axis_index("c")


def _gather_copies(x_ref, out_ref, send_sems, recv_sems):
    x, y, c = _place()
    me, sibling = (x, y, c), (x, y, 1 - c)
    chips = [(1 - x, y), (x, 1 - y), (1 - x, 1 - y)]

    def slot(px, py, pc):
        return out_ref.at[4 * px + 2 * py + pc]

    def copy(k, block, to, src=None):
        return pltpu.make_async_remote_copy(
            src_ref=slot(*block) if src is None else src, dst_ref=slot(*block),
            send_sem=send_sems.at[k], recv_sem=recv_sems.at[k], device_id=to, device_id_type=MESH)

    first = [lambda: copy(0, me, sibling, src=x_ref)] + [functools.partial(copy, 1 + j, me, (*chip, c), src=x_ref)
                                                         for j, chip in enumerate(chips)]
    passed = [functools.partial(copy, 4 + j, (*chip, c), sibling) for j, chip in enumerate(chips)]
    landing = [functools.partial(copy, 1 + j, (*chip, c), me) for j, chip in enumerate(chips)]
    from_sibling = [lambda: copy(0, sibling, me)] + [functools.partial(copy, 4 + j, (*chip, 1 - c), me) for j, chip in enumerate(chips)]
    return slot(*me), first, passed, landing, from_sibling


def _gather_start(x_ref, out_ref, send_sems, recv_sems, local_sem):
    mine, first, _, _, _ = _gather_copies(x_ref, out_ref, send_sems, recv_sems)
    pltpu.make_async_copy(x_ref, mine, local_sem).start()
    for cp in first:
        cp().start()


def _gather_finish(x_ref, out_ref, send_sems, recv_sems, local_sem):
    mine, first, passed, landing, from_sibling = _gather_copies(x_ref, out_ref, send_sems, recv_sems)
    for cp, fwd in zip(landing, passed):
        cp().wait_recv()
        fwd().start()
    for cp in from_sibling:
        cp().wait_recv()
    for cp in first + passed:
        cp().wait_send()
    pltpu.make_async_copy(x_ref, mine, local_sem).wait()


def _a2a_copies(x_ref, out_ref, send_sems, recv_sems):
    x, y, c = _place()
    me = 4 * x + 2 * y + c
    copies = []
    for k in range(1, N_DEV):
        px = 1 - x if k & 4 else x
        py = 1 - y if k & 2 else y
        pc = 1 - c if k & 1 else c
        copies.append(pltpu.make_async_remote_copy(
            src_ref=x_ref.at[4 * px + 2 * py + pc], dst_ref=out_ref.at[me], send_sem=send_sems.at[k - 1],
            recv_sem=recv_sems.at[k - 1], device_id=(px, py, pc), device_id_type=MESH))
    return me, copies


def _a2a_start(x_ref, out_ref, send_sems, recv_sems, local_sem):
    me, copies = _a2a_copies(x_ref, out_ref, send_sems, recv_sems)
    pltpu.make_async_copy(x_ref.at[me], out_ref.at[me], local_sem).start()
    for cp in copies:
        cp.start()


def _a2a_finish(x_ref, out_ref, send_sems, recv_sems, local_sem):
    me, copies = _a2a_copies(x_ref, out_ref, send_sems, recv_sems)
    for cp in copies:
        cp.wait_recv()
    for cp in copies:
        cp.wait_send()
    pltpu.make_async_copy(x_ref.at[me], out_ref.at[me], local_sem).wait()


EXCHANGES = {"gather": (_gather_start, _gather_finish, lambda x: (N_DEV,) + x.shape),
             "a2a": (_a2a_start, _a2a_finish, lambda x: x.shape)}
EXCHANGE_SEMS = [pltpu.SemaphoreType.DMA((7,)), pltpu.SemaphoreType.DMA((7,)), pltpu.SemaphoreType.DMA(())]


def exchange(kind, x, *, name):
    start, finish, shape = EXCHANGES[kind]

    def body(x_ref, out_ref, *sems):
        start(x_ref, out_ref, *sems)
        finish(x_ref, out_ref, *sems)

    return pl.pallas_call(
        body,
        out_shape=jax.ShapeDtypeStruct(shape(x), x.dtype),
        in_specs=[pl.BlockSpec(memory_space=pl.ANY)],
        out_specs=pl.BlockSpec(memory_space=pl.ANY),
        scratch_shapes=EXCHANGE_SEMS,
        name=name,
    )(x)


def call_with_exchange(body, carried, *, out_shape, grid, in_specs, out_specs, scratch_shapes, name, args):
    sem = ("arbitrary",) * len(grid)
    carried = list(carried or ())
    if not carried:
        res = pl.pallas_call(body, out_shape=out_shape, grid=grid, in_specs=in_specs, out_specs=out_specs,
                             scratch_shapes=scratch_shapes, compiler_params=_params(sem), name=name)(*args)
        return res, []
    n_in, n_out, n_scr, n_x = len(in_specs), len(out_shape), len(scratch_shapes), len(carried)
    n_sems = len(EXCHANGE_SEMS)

    def wrapped(*refs):
        ins, x_refs = refs[:n_in], refs[n_in:n_in + n_x]
        outs = refs[n_in + n_x:n_in + n_x + n_out]
        out_refs = refs[n_in + n_x + n_out:n_in + 2 * n_x + n_out]
        rest = refs[n_in + 2 * n_x + n_out:]
        scratch, sems = rest[:n_scr], rest[n_scr:]
        ids = [pl.program_id(i) for i in range(len(grid))]
        is_first = functools.reduce(lambda a, b: a & b, [i == 0 for i in ids])
        is_last = functools.reduce(lambda a, b: a & b, [i == g - 1 for i, g in zip(ids, grid)])

        @pl.when(is_first)
        def _():
            for e, (kind, _) in enumerate(carried):
                EXCHANGES[kind][0](x_refs[e], out_refs[e], *sems[e * n_sems:(e + 1) * n_sems])

        body(*ins, *outs, *scratch)

        @pl.when(is_last)
        def _():
            for e, (kind, _) in enumerate(carried):
                EXCHANGES[kind][1](x_refs[e], out_refs[e], *sems[e * n_sems:(e + 1) * n_sems])

    any_spec = pl.BlockSpec(memory_space=pl.ANY)
    res = pl.pallas_call(
        wrapped,
        out_shape=tuple(out_shape) + tuple(jax.ShapeDtypeStruct(EXCHANGES[kind][2](x), x.dtype) for kind, x in carried),
        grid=grid,
        in_specs=list(in_specs) + [any_spec] * n_x,
        out_specs=tuple(out_specs) + (any_spec,) * n_x,
        scratch_shapes=list(scratch_shapes) + EXCHANGE_SEMS * n_x,
        compiler_params=_params(sem),
        name=name + "".join("_" + kind for kind, _ in carried),
    )(*args, *[x for _, x in carried])
    return res[:n_out], list(res[n_out:])


def sum_slots(x, *, name):
    _, rows, cols = x.shape
    tr = _pick(rows, (512, 256, 128, 64, 32, 16))

    def body(x_ref, o_ref):
        acc = x_ref[0].astype(F32)
        for k in range(1, N_DEV):
            acc = acc + x_ref[k].astype(F32)
        o_ref[...] = acc

    return pl.pallas_call(
        body,
        out_shape=jax.ShapeDtypeStruct((rows, cols), F32),
        grid=(rows // tr,),
        in_specs=[pl.BlockSpec((N_DEV, tr, cols), lambda i: (0, i, 0))],
        out_specs=pl.BlockSpec((tr, cols), lambda i: (i, 0)),
        compiler_params=_params(("parallel",)),
        name=name,
    )(x)


BIG = ("w_in", "w_branch", "w_out", "w_ffn_in", "w_ffn_out")
SMALL = ("conv_b", "w_rg", "b_rg", "w_ig", "b_ig", "lru_lambda", "sinks", "ln1_g", "ln1_b", "ln2_g", "ln2_b")
N_LRU_BLOCKS = D_MODEL // HEAD_DIM
SMALL_ROWS_TILE = 512


def _block_diag(w):
    z = jnp.zeros((N_LRU_BLOCKS // 2, HEAD_DIM, HEAD_DIM), w.dtype)
    top = jnp.concatenate([w[0::2], z], axis=2)
    bot = jnp.concatenate([z, w[1::2]], axis=2)
    return jnp.concatenate([top, bot], axis=1)


def _block_diag_grad(g):
    return jnp.stack([g[:, :HEAD_DIM, :HEAD_DIM], g[:, HEAD_DIM:, HEAD_DIM:]], axis=1).reshape(N_LRU_BLOCKS, HEAD_DIM, HEAD_DIM)


def layer_fwd(x, xb, p, bsz, own_late=None, next_w_in=None):
    t_dim = x.shape[0]
    s_len = t_dim // bsz
    w_f, w_qs, w_qd = p["w_in_f"], p["w_in_qs"], p["w_in_qd"]
    proj_f = matmul(xb, w_f, name="proj_f")
    qs = matmul(xb, w_qs, out_dtype=BF16, name="proj_qs").reshape(bsz, s_len, W_QS)
    qd = matmul(xb, w_qd, name="proj_qd").reshape(bsz, s_len, W_QD)
    proj_f3 = proj_f.reshape(bsz, s_len, W_F)
    wr_bd, wi_bd = _block_diag(p["w_rg"]), _block_diag(p["w_ig"])
    (y_a, h), got_rows = lru_fwd(proj_f3, p["conv_w"], p["conv_b"], wr_bd, wi_bd, p["b_rg"], p["b_ig"], p["lru_lambda"],
                                 name="lru_fwd", carried=[("gather", own_late[1])] if own_late is not None else [])
    (y_b, lse_b, y_bb), got_fi = swa_fwd(qs, p["sinks"], name="swa_fwd", carried=[("gather", own_late[0])] if own_late is not None else [])
    (y_c, lse_c, y_cb), got_next = dil_fwd(qd, name="dil_fwd", carried=[("gather", next_w_in)] if next_w_in is not None else [])
    if own_late is not None:
        p = {**p, **_late_weights(got_fi[0], got_rows[0])}
    ys = [t.reshape(t_dim, D_MODEL) for t in (y_a, y_bb, y_cb)]
    merged, br = branch_merge(ys, p["w_branch"], proj_f, name="branch_merge")
    mix = matmul(merged, p["w_out"], name="w_out")
    x1, x1b, z1 = ln_fwd(x, mix, p["ln1_g"], p["ln1_b"], name="ln_fwd")
    h1, h3, act = ffn_in_swiglu(x1b, p["w_ffn_in"], name="ffn_in_swiglu")
    ffn = matmul(act, p["w_ffn_out"], name="ffn_out")
    x2, x2b, z2 = ln_fwd(x1, ffn, p["ln2_g"], p["ln2_b"], name="ln_fwd")
    saved = dict(xb=xb, proj_f=proj_f, qs=qs, qd=qd, h=h, ys=ys, y_b=y_b, y_c=y_c, lse_b=lse_b, lse_c=lse_c, br=br, merged=merged,
                 z1=z1, x1b=x1b, h1=h1, h3=h3, act=act, z2=z2, wr_bd=wr_bd, wi_bd=wi_bd, p=p)
    return x2, x2b, saved, (got_next[0] if got_next else None)


def layer_bwd(dx2, s, bsz, exchange_own=False, above_w_in=None):
    p = s["p"]
    t_dim = dx2.shape[0]
    s_len = t_dim // bsz
    g = {}
    dz2, dz2b, g["ln2_g"], g["ln2_b"] = ln_bwd(dx2, s["z2"], p["ln2_g"], name="ln_bwd")
    dact = matmul(dz2b, p["w_ffn_out"], trans_b=True, out_dtype=BF16, name="d_act")
    dh13 = swiglu_bwd(dact, s["h1"], s["h3"], name="swiglu_bwd")
    g["w_ffn_out"] = matmul(s["act"], dz2b, trans_a=True, out_dtype=BF16, name="dw_ffn_out")
    g["w_ffn_in"] = matmul(s["x1b"], dh13, trans_a=True, out_dtype=BF16, name="dw_ffn_in")
    dx1 = matmul(dh13, p["w_ffn_in"], trans_b=True, add=dz2, add_scale=ALPHA, name="dx_ffn")
    dz1, dz1b, g["ln1_g"], g["ln1_b"] = ln_bwd(dx1, s["z1"], p["ln1_g"], name="ln_bwd")
    dmerged = matmul(dz1b, p["w_out"], trans_b=True, name="d_merged")
    g["w_out"] = matmul(s["merged"], dz1b, trans_a=True, out_dtype=BF16, name="dw_out")
    *dbr, dgates = merge_bwd(dmerged, s["proj_f"], s["br"], name="merge_bwd")
    dys = [matmul(dbr[n], p["w_branch"][n], trans_b=True, out_dtype=F32 if n == 2 else BF16, name="d_branch") for n in range(3)]
    g["w_branch"] = jnp.stack([matmul(s["ys"][n], dbr[n], trans_a=True, out_dtype=BF16, name="dw_branch") for n in range(3)])
    fi_slots, rows_slots = _late_slots(g) if exchange_own else (None, None)
    shape3 = (bsz, s_len, D_MODEL)
    (dlx, dlg, g["conv_w"], g["conv_b"], g["b_rg"], g["b_ig"], g["lru_lambda"], dwr, dwi), got_rows = lru_bwd(
        dys[0].reshape(shape3), s["proj_f"].reshape(bsz, s_len, W_F), s["h"], p["conv_w"], p["conv_b"], s["wr_bd"], s["wi_bd"],
        jnp.swapaxes(s["wr_bd"], 1, 2), jnp.swapaxes(s["wi_bd"], 1, 2), p["b_rg"], p["b_ig"], p["lru_lambda"], name="lru_bwd",
        carried=[("a2a", rows_slots)] if exchange_own else [])
    g["w_rg"], g["w_ig"] = _block_diag_grad(dwr), _block_diag_grad(dwi)
    dy_b3 = dys[1].reshape(shape3)
    (*dqs, dsinks), got_fi = swa_bwd(s["qs"], p["sinks"], s["y_b"], s["lse_b"], dy_b3, name="swa_bwd",
                                     carried=[("a2a", fi_slots)] if exchange_own else [])
    g["sinks"] = dsinks[0, :N_HEADS]
    dqd, got_in = dil_bwd(s["qd"], s["y_c"], s["lse_c"], dys[2].reshape(shape3), name="dil_bwd",
                          carried=[("a2a", above_w_in)] if above_w_in is not None else [])
    flat = lambda t: t.reshape(t_dim, t.shape[-1])
    dproj_f = jnp.concatenate([flat(dlx), flat(dlg), dgates], axis=1)
    dproj_qs = jnp.concatenate([flat(t) for t in dqs], axis=1).astype(BF16)
    dproj_qd = jnp.concatenate([flat(t) for t in dqd], axis=1).astype(BF16)
    g["w_in_f"] = matmul(s["xb"], dproj_f, trans_a=True, out_dtype=BF16, name="dw_in_f")
    g["w_in_qs"] = matmul(s["xb"], dproj_qs, trans_a=True, out_dtype=BF16, name="dw_in_qs")
    g["w_in_qd"] = matmul(s["xb"], dproj_qd, trans_a=True, out_dtype=BF16, name="dw_in_qd")
    dx = matmul(dproj_f, p["w_in_f"], trans_b=True, add=dz1, add_scale=ALPHA, name="dx_f")
    dx = matmul(dproj_qs, p["w_in_qs"], trans_b=True, add=dx, name="dx_qs")
    dx = matmul(dproj_qd, p["w_in_qd"], trans_b=True, add=dx, name="dx_qd")
    g = {k: (v.reshape(p[k].shape) if k in p else v) for k, v in g.items()}
    return dx, g, dict(late=(got_fi[0], got_rows[0]) if exchange_own else None, w_in=got_in[0] if got_in else None)


def local_step(x, target, layer_params, layer_shards=None, first_w_in=None):
    bsz, s_len, d = x.shape
    t_dim = bsz * s_len
    xf = x.reshape(t_dim, d)
    xb = xf.astype(BF16)
    exchanging = layer_shards is not None
    saved, gathered = [], first_w_in
    for l in range(DEPTH):
        p = layer_params(l, gathered)
        xf, xb, s, gathered = layer_fwd(xf, xb, p, bsz, own_late=layer_shards[l][1:] if exchanging else None,
                                        next_w_in=layer_shards[l + 1][0] if exchanging and l + 1 < DEPTH else None)
        saved.append(s)
    dy, sq = loss_head(xf, target.reshape(t_dim, d), name="loss_head")
    loss = 0.5 * jnp.sum(sq) / d
    grads, received, w_in_slots = [None] * DEPTH, [[None] * 3 for _ in range(DEPTH)], None
    for l in reversed(range(DEPTH)):
        dy, grads[l], got = layer_bwd(dy, saved[l], bsz, exchange_own=exchanging, above_w_in=w_in_slots)
        if got["w_in"] is not None:
            received[l + 1][0] = got["w_in"]
        if exchanging:
            received[l][1:] = got["late"]
            w_in_slots = _w_in_slots(grads[l])
    return loss, dy.reshape(bsz, s_len, d), grads, received, w_in_slots


W_IN_SEGMENTS = (("w_in_f", 0, 0, 2 * D_MODEL), ("w_in_qs", 0, 2 * D_MODEL, W_QS), ("w_in_qd", 0, 2 * D_MODEL + W_QS, W_QD),
                 ("w_in_f", 2 * D_MODEL, 2 * D_MODEL + W_QS + W_QD, 3 * D_MODEL))
ROW_SHARDED = ("w_branch", "w_out", "w_ffn_out")


def _cols_of_shards(shards, lo, hi):
    width = shards[0].shape[-1]
    parts = []
    for k, sh in enumerate(shards):
        a, b = max(lo, k * width), min(hi, (k + 1) * width)
        if a < b:
            parts.append(sh[..., a - k * width:b - k * width])
    return parts[0] if len(parts) == 1 else jnp.concatenate(parts, axis=-1)


def _cols_of_w_in(pieces, lo, hi):
    parts = []
    for name, p0, l0, width in W_IN_SEGMENTS:
        a, b = max(lo, l0), min(hi, l0 + width)
        if a < b:
            parts.append(pieces[name][..., p0 + a - l0:p0 + b - l0])
    return parts[0] if len(parts) == 1 else jnp.concatenate(parts, axis=-1)


W_IN_COLS = W_F + W_QS + W_QD


def _layer_shards(w):
    rows = jnp.concatenate([w[k].reshape(DEPTH, -1, D_MODEL) for k in ROW_SHARDED], axis=1).astype(BF16)
    w_in, w_fi = w["w_in"].astype(BF16), w["w_ffn_in"].astype(BF16)
    return [(w_in[l], w_fi[l], rows[l]) for l in range(DEPTH)]


ROW_COUNTS = (3 * D_MODEL // N_DEV, D_MODEL // N_DEV, FF_HIDDEN // N_DEV)


def _w_in_weights(g_in):
    sh = [g_in[k] for k in range(N_DEV)]
    return dict(w_in_f=jnp.concatenate([_cols_of_shards(sh, 0, 2 * D_MODEL), _cols_of_shards(sh, W_IN_COLS - 3 * D_MODEL, W_IN_COLS)], axis=-1),
                w_in_qs=_cols_of_shards(sh, 2 * D_MODEL, 2 * D_MODEL + W_QS),
                w_in_qd=_cols_of_shards(sh, 2 * D_MODEL + W_QS, 2 * D_MODEL + W_QS + W_QD))


def _late_weights(g_fi, g_rows):
    p = dict(w_ffn_in=jnp.concatenate([g_fi[k] for k in range(N_DEV)], axis=-1))
    off = 0
    for k, n in zip(ROW_SHARDED, ROW_COUNTS):
        t = g_rows[:, off:off + n]
        if k == "w_branch":
            p[k] = jnp.transpose(t.reshape(N_DEV, 3, n // 3, D_MODEL), (1, 0, 2, 3)).reshape(3, -1, D_MODEL)
        else:
            p[k] = t.reshape(-1, D_MODEL)
        off += n
    return p


def _w_in_slots(g):
    shard = W_IN_COLS // N_DEV
    return jnp.stack([_cols_of_w_in(g, k * shard, (k + 1) * shard) for k in range(N_DEV)]).astype(BF16)


def _late_slots(g):
    shard = g["w_ffn_in"].shape[-1] // N_DEV
    s_fi = jnp.stack([g["w_ffn_in"][:, k * shard:(k + 1) * shard] for k in range(N_DEV)]).astype(BF16)
    rows = jnp.concatenate([jnp.transpose(g["w_branch"].reshape(3, N_DEV, -1, D_MODEL), (1, 0, 2, 3)).reshape(N_DEV, -1, D_MODEL),
                            g["w_out"].reshape(N_DEV, -1, D_MODEL), g["w_ffn_out"].reshape(N_DEV, -1, D_MODEL)], axis=1).astype(BF16)
    return s_fi, rows


def _pad_rows(flat, tile_rows):
    n = flat.shape[0]
    per = tile_rows * LANES
    total = -(-n // per) * per
    return jnp.pad(flat, (0, total - n)).reshape(-1, LANES)


def kernel(x, w_in, conv_w, conv_b, w_rg, b_rg, w_ig, b_ig, lru_lambda, sinks, w_branch, w_out, ln1_g, ln1_b, w_ffn_in, w_ffn_out, ln2_g, ln2_b, loss_target, m_w_in, m_conv_w, m_conv_b, m_w_rg, m_b_rg, m_w_ig, m_b_ig, m_lru_lambda, m_sinks, m_w_branch, m_w_out, m_ln1_g, m_ln1_b, m_w_ffn_in, m_w_ffn_out, m_ln2_g, m_ln2_b, v_w_in, v_conv_w, v_conv_b, v_w_rg, v_b_rg, v_w_ig, v_b_ig, v_lru_lambda, v_sinks, v_w_branch, v_w_out, v_ln1_g, v_ln1_b, v_w_ffn_in, v_w_ffn_out, v_ln2_g, v_ln2_b):
    w = dict(w_in=w_in, conv_w=conv_w, conv_b=conv_b, w_rg=w_rg, b_rg=b_rg, w_ig=w_ig, b_ig=b_ig, lru_lambda=lru_lambda, sinks=sinks,
             w_branch=w_branch, w_out=w_out, ln1_g=ln1_g, ln1_b=ln1_b, w_ffn_in=w_ffn_in, w_ffn_out=w_ffn_out, ln2_g=ln2_g, ln2_b=ln2_b)
    m = dict(w_in=m_w_in, conv_w=m_conv_w, conv_b=m_conv_b, w_rg=m_w_rg, b_rg=m_b_rg, w_ig=m_w_ig, b_ig=m_b_ig, lru_lambda=m_lru_lambda,
             sinks=m_sinks, w_branch=m_w_branch, w_out=m_w_out, ln1_g=m_ln1_g, ln1_b=m_ln1_b, w_ffn_in=m_w_ffn_in, w_ffn_out=m_w_ffn_out,
             ln2_g=m_ln2_g, ln2_b=m_ln2_b)
    v = dict(w_in=v_w_in, conv_w=v_conv_w, conv_b=v_conv_b, w_rg=v_w_rg, b_rg=v_b_rg, w_ig=v_w_ig, b_ig=v_b_ig, lru_lambda=v_lru_lambda,
             sinks=v_sinks, w_branch=v_w_branch, w_out=v_w_out, ln1_g=v_ln1_g, ln1_b=v_ln1_b, w_ffn_in=v_w_ffn_in, w_ffn_out=v_w_ffn_out,
             ln2_g=v_ln2_g, ln2_b=v_ln2_b)
    order = ["w_in", "conv_w", "conv_b", "w_rg", "b_rg", "w_ig", "b_ig", "lru_lambda", "sinks", "w_branch", "w_out", "ln1_g", "ln1_b",
             "w_ffn_in", "w_ffn_out", "ln2_g", "ln2_b"]
    me = 4 * lax.axis_index("x") + 2 * lax.axis_index("y") + lax.axis_index("c")

    names = ("w_in", "w_ffn_in", "w_rows")
    shards = _layer_shards(w)
    first_w_in = exchange("gather", shards[0][0], name="gather_w_in")
    cw = exchange("gather", conv_w.reshape(-1, LANES), name="gather_conv_w")
    conv_w_full = jnp.moveaxis(cw.reshape(N_DEV, DEPTH, CONV_WIDTH, LANES), 0, 2).reshape(DEPTH, CONV_WIDTH, D_MODEL)

    def layer_params(l, gathered_w_in):
        return {**_w_in_weights(gathered_w_in), **{k: w[k][l] for k in SMALL}, "conv_w": conv_w_full[l]}

    loss_local, grad_x, grads, received, w_in_slots = local_step(x, loss_target, layer_params, shards, first_w_in)
    loss = lax.psum(loss_local, ("x", "y", "c"))
    received[0][0] = exchange("a2a", w_in_slots, name="exchange_g_w_in")

    sums = [[sum_slots(t, name=f"sum_g_{n}") for t, n in zip(received[l], names)] for l in range(DEPTH)]
    g_final = {"w_in": jnp.stack([sums[l][0] for l in range(DEPTH)]), "w_ffn_in": jnp.stack([sums[l][1] for l in range(DEPTH)])}
    off = 0
    for k, n in zip(ROW_SHARDED, ROW_COUNTS):
        g_final[k] = jnp.stack([sums[l][2][off:off + n] for l in range(DEPTH)]).reshape(w[k].shape)
        off += n
    grads = {k: jnp.stack([grads[l][k] for l in range(DEPTH)]) for k in list(SMALL) + ["conv_w"]}

    small_names = list(SMALL) + ["conv_w"]
    small_sizes = [grads[k].size for k in small_names]
    svec = _pad_rows(jnp.concatenate([grads[k].reshape(-1) for k in small_names]), SMALL_ROWS_TILE)
    ssum = sum_slots(exchange("gather", svec, name="gather_small_grads"), name="sum_small_grads")
    sflat, off = ssum.reshape(-1), 0
    for k, n in zip(small_names, small_sizes):
        g_final[k] = sflat[off:off + n].reshape(grads[k].shape)
        off += n
    g_final["conv_w"] = lax.dynamic_slice_in_dim(g_final["conv_w"], me * LANES, LANES, axis=2)

    delta, new_m, new_v = {}, {}, {}
    for k in list(BIG) + ["conv_w"]:
        cols = w[k].shape[-1]
        two_d = lambda t: t.reshape(-1, cols)
        d_, m_, v_ = adamw(two_d(w[k]), two_d(g_final[k]), two_d(m[k]), two_d(v[k]), name=f"adamw_{k}")
        delta[k], new_m[k], new_v[k] = d_.reshape(w[k].shape), m_.reshape(w[k].shape), v_.reshape(w[k].shape)
    pack_small = lambda dct: _pad_rows(jnp.concatenate([dct[k].reshape(-1) for k in SMALL]), SMALL_ROWS_TILE)
    d_, m_, v_ = adamw(pack_small(w), pack_small(g_final), pack_small(m), pack_small(v), name="adamw_small")
    off = 0
    for k in SMALL:
        n = w[k].size
        for dst, src in ((delta, d_), (new_m, m_), (new_v, v_)):
            dst[k] = src.reshape(-1)[off:off + n].reshape(w[k].shape)
        off += n
    return (loss, grad_x, *[g_final[k] for k in order], *[delta[k] for k in order], *[new_m[k] for k in order], *[new_v[k] for k in order])
```

```python
import functools
import math

import jax
import jax.numpy as jnp
from jax import lax
from jax.experimental import pallas as pl
from jax.experimental.pallas import tpu as pltpu

F32 = jnp.float32
BF16 = jnp.bfloat16

N_DEV = 8
DEPTH = 4
D_MODEL = 1024
HEAD_DIM = 64
LANES = 128
N_HEADS = D_MODEL // HEAD_DIM
SWA_KV_HEADS = 4
ATT_BLOCK = 128
DILATIONS = (1, 4, 16)
CONV_WIDTH = 4
LRU_C = 8.0
FF_HIDDEN = 2816
ALPHA = (2.0 * DEPTH) ** 0.25
LN_EPS = 1e-5
NEG_INF = -1e30
W_F = 5 * D_MODEL
W_QS = D_MODEL + 2 * SWA_KV_HEADS * HEAD_DIM
W_QD = 3 * D_MODEL

ADAM_LR = 0.001
ADAM_B1 = 0.9
ADAM_B2 = 0.999
ADAM_EPS = 1e-08
ADAM_WD = 0.01
ADAM_STEP = 10

VMEM_LIMIT = 56 * 1024 * 1024
MATMUL_BLOCK_BYTES = 40 * 1024 * 1024
MESH = pl.DeviceIdType.MESH


def _pick(n, cands):
    for c in cands:
        if n % c == 0:
            return c
    raise ValueError(f"no tile for {n} among {cands}")


def _params(sem):
    return pltpu.CompilerParams(dimension_semantics=sem, vmem_limit_bytes=VMEM_LIMIT)


def _tile(n, cap):
    best = None
    for t in range(LANES, cap + 1, LANES):
        if n % t == 0:
            best = t
    assert best is not None, (n, cap)
    return best


def matmul(a, b, *, name, trans_a=False, trans_b=False, out_dtype=F32, add=None, add_scale=1.0):
    if trans_a:
        k_dim, m_dim = a.shape
    else:
        m_dim, k_dim = a.shape
    n_dim = b.shape[0] if trans_b else b.shape[1]
    assert (b.shape[1] if trans_b else b.shape[0]) == k_dim
    tn = _tile(n_dim, 1408)
    tm, tk = _tile(m_dim, 1024), _tile(k_dim, 1408)
    for cand in (1024, 512, 256):
        ctm = _tile(m_dim, cand)
        blocks = 2 * (ctm * k_dim * a.dtype.itemsize + tn * k_dim * b.dtype.itemsize + ctm * tn * jnp.dtype(out_dtype).itemsize
                      + (ctm * tn * add.dtype.itemsize if add is not None else 0))
        if blocks <= MATMUL_BLOCK_BYTES:
            tm, tk = ctm, k_dim
            break
    nk = k_dim // tk
    dims = (((0 if trans_a else 1,), (1 if trans_b else 0,)), ((), ()))

    def body(*refs):
        if add is None:
            a_ref, b_ref, o_ref, acc_ref = refs
            add_ref = None
        else:
            a_ref, b_ref, add_ref, o_ref, acc_ref = refs
        k = pl.program_id(2)
        part = lax.dot_general(a_ref[...].astype(BF16), b_ref[...].astype(BF16), dims, preferred_element_type=F32)

        def finish(r):
            if add_ref is not None:
                r = r + add_scale * add_ref[...].astype(F32)
            o_ref[...] = r.astype(out_dtype)

        if nk == 1:
            finish(part)
        else:
            @pl.when(k == 0)
            def _():
                acc_ref[...] = part

            @pl.when((k > 0) & (k < nk - 1))
            def _():
                acc_ref[...] += part

            @pl.when(k == nk - 1)
            def _():
                finish(acc_ref[...] + part)

    a_spec = pl.BlockSpec((tk, tm), lambda i, j, k: (k, i)) if trans_a else pl.BlockSpec((tm, tk), lambda i, j, k: (i, k))
    b_spec = pl.BlockSpec((tn, tk), lambda i, j, k: (j, k)) if trans_b else pl.BlockSpec((tk, tn), lambda i, j, k: (k, j))
    in_specs = [a_spec, b_spec]
    args = [a, b]
    if add is not None:
        in_specs.append(pl.BlockSpec((tm, tn), lambda i, j, k: (i, j)))
        args.append(add)
    return pl.pallas_call(
        body,
        out_shape=jax.ShapeDtypeStruct((m_dim, n_dim), out_dtype),
        grid=(m_dim // tm, n_dim // tn, nk),
        in_specs=in_specs,
        out_specs=pl.BlockSpec((tm, tn), lambda i, j, k: (i, j)),
        scratch_shapes=[pltpu.VMEM((tm, tn) if nk > 1 else (8, LANES), F32)],
        compiler_params=_params(("parallel", "parallel", "arbitrary")),
        name=name,
    )(*args)


def ln_fwd(x, r, g, b, *, name):
    t_dim, d = x.shape
    tr = _pick(t_dim, (256, 128, 8))

    def body(x_ref, r_ref, g_ref, b_ref, y_ref, yb_ref, z_ref):
        z = ALPHA * x_ref[...] + r_ref[...]
        mu = jnp.mean(z, axis=-1, keepdims=True)
        zc = z - mu
        var = jnp.mean(zc * zc, axis=-1, keepdims=True)
        y = zc * lax.rsqrt(var + LN_EPS) * g_ref[...] + b_ref[...]
        y_ref[...] = y
        yb_ref[...] = y.astype(BF16)
        z_ref[...] = z

    row = pl.BlockSpec((tr, d), lambda i: (i, 0))
    vec = pl.BlockSpec((1, d), lambda i: (0, 0))
    return pl.pallas_call(
        body,
        out_shape=(jax.ShapeDtypeStruct((t_dim, d), F32), jax.ShapeDtypeStruct((t_dim, d), BF16), jax.ShapeDtypeStruct((t_dim, d), F32)),
        grid=(t_dim // tr,),
        in_specs=[row, row, vec, vec],
        out_specs=(row, row, row),
        compiler_params=_params(("parallel",)),
        name=name,
    )(x, r, g.reshape(1, d), b.reshape(1, d))


def ln_bwd(dy, z, g, *, name):
    t_dim, d = dy.shape
    tr = _pick(t_dim, (256, 128, 8))

    def body(dy_ref, z_ref, g_ref, dz_ref, dzb_ref, dg_ref, db_ref):
        @pl.when(pl.program_id(0) == 0)
        def _():
            dg_ref[...] = jnp.zeros_like(dg_ref)
            db_ref[...] = jnp.zeros_like(db_ref)

        z = z_ref[...]
        dyv = dy_ref[...]
        mu = jnp.mean(z, axis=-1, keepdims=True)
        zc = z - mu
        var = jnp.mean(zc * zc, axis=-1, keepdims=True)
        rstd = lax.rsqrt(var + LN_EPS)
        xhat = zc * rstd
        dxhat = dyv * g_ref[...]
        m1 = jnp.mean(dxhat, axis=-1, keepdims=True)
        m2 = jnp.mean(dxhat * xhat, axis=-1, keepdims=True)
        dz = rstd * (dxhat - m1 - xhat * m2)
        dz_ref[...] = dz
        dzb_ref[...] = dz.astype(BF16)
        dg_ref[...] += jnp.sum(dyv * xhat, axis=0, keepdims=True)
        db_ref[...] += jnp.sum(dyv, axis=0, keepdims=True)

    row = pl.BlockSpec((tr, d), lambda i: (i, 0))
    vec = pl.BlockSpec((1, d), lambda i: (0, 0))
    return pl.pallas_call(
        body,
        out_shape=(jax.ShapeDtypeStruct((t_dim, d), F32), jax.ShapeDtypeStruct((t_dim, d), BF16),
                   jax.ShapeDtypeStruct((1, d), F32), jax.ShapeDtypeStruct((1, d), F32)),
        grid=(t_dim // tr,),
        in_specs=[row, row, vec],
        out_specs=(row, row, vec, vec),
        compiler_params=_params(("arbitrary",)),
        name=name,
    )(dy, z, g.reshape(1, d))


def loss_head(y, target, *, name):
    t_dim, d = y.shape
    tr = _pick(t_dim, (256, 128, 8))

    def body(y_ref, t_ref, dy_ref, sq_ref):
        @pl.when(pl.program_id(0) == 0)
        def _():
            sq_ref[...] = jnp.zeros_like(sq_ref)

        diff = y_ref[...] - t_ref[...]
        dy_ref[...] = diff / d
        sq_ref[...] += jnp.sum(diff * diff, axis=0, keepdims=True)

    row = pl.BlockSpec((tr, d), lambda i: (i, 0))
    vec = pl.BlockSpec((1, d), lambda i: (0, 0))
    return pl.pallas_call(
        body,
        out_shape=(jax.ShapeDtypeStruct((t_dim, d), F32), jax.ShapeDtypeStruct((1, d), F32)),
        grid=(t_dim // tr,),
        in_specs=[row, row],
        out_specs=(row, vec),
        compiler_params=_params(("arbitrary",)),
        name=name,
    )(y, target)


def _sigmoid(x):
    return 0.5 * jnp.tanh(0.5 * x) + 0.5


def ffn_in_swiglu(x, w, *, name):
    t_dim, d = x.shape
    f = w.shape[1] // 2
    tm, tn = _tile(t_dim, 1024), _tile(f, 1408)
    nf = f // tn

    def body(x_ref, w1_ref, w3_ref, h1_ref, h3_ref, act_ref):
        xv = x_ref[...]
        h1 = jnp.dot(xv, w1_ref[...], preferred_element_type=F32)
        h3 = jnp.dot(xv, w3_ref[...], preferred_element_type=F32)
        h1_ref[...] = h1.astype(BF16)
        h3_ref[...] = h3.astype(BF16)
        act_ref[...] = (h1 * _sigmoid(h1) * h3).astype(BF16)

    out = pl.BlockSpec((tm, tn), lambda i, j: (i, j))
    return pl.pallas_call(
        body,
        out_shape=(jax.ShapeDtypeStruct((t_dim, f), BF16),) * 3,
        grid=(t_dim // tm, nf),
        in_specs=[pl.BlockSpec((tm, d), lambda i, j: (i, 0)), pl.BlockSpec((d, tn), lambda i, j: (0, j)),
                  pl.BlockSpec((d, tn), lambda i, j: (0, j + nf))],
        out_specs=(out, out, out),
        compiler_params=_params(("parallel", "parallel")),
        name=name,
    )(x, w, w)


def swiglu_bwd(dz, w_ffn_out, h1, h3, *, name):
    t_dim, d = dz.shape
    f = h1.shape[1]
    tr = _tile(t_dim, 512)

    def body(dz_ref, w_ref, h1_ref, h3_ref, dh_ref):
        da = lax.dot_general(dz_ref[...], w_ref[...], NT, preferred_element_type=F32)
        h1 = h1_ref[...].astype(F32)
        sg = _sigmoid(h1)
        dh_ref[:, :f] = (da * h3_ref[...].astype(F32) * sg * (1.0 + h1 * (1.0 - sg))).astype(BF16)
        dh_ref[:, f:] = (da * h1 * sg).astype(BF16)

    wide = pl.BlockSpec((tr, f), lambda i: (i, 0))
    return pl.pallas_call(
        body,
        out_shape=jax.ShapeDtypeStruct((t_dim, 2 * f), BF16),
        grid=(t_dim // tr,),
        in_specs=[pl.BlockSpec((tr, d), lambda i: (i, 0)), pl.BlockSpec((f, d), lambda i: (0, 0)), wide, wide],
        out_specs=pl.BlockSpec((tr, 2 * f), lambda i: (i, 0)),
        compiler_params=_params(("parallel",)),
        name=name,
    )(dz, w_ffn_out, h1, h3)


def branch_merge(ys, w_branch, proj_f, *, name):
    t_dim, d = ys[0].shape
    tm = _tile(t_dim, 512)

    def body(y0, y1, y2, w_ref, g0, g1, g2, m_ref, b0, b1, b2):
        acc = None
        for n, (y, g, b) in enumerate(((y0, g0, b0), (y1, g1, b1), (y2, g2, b2))):
            br = jnp.dot(y[...], w_ref[n], preferred_element_type=F32)
            b[...] = br.astype(BF16)
            t = _sigmoid(g[...]) * br
            acc = t if acc is None else acc + t
        m_ref[...] = acc.astype(BF16)

    row = pl.BlockSpec((tm, d), lambda i: (i, 0))
    gate = [pl.BlockSpec((tm, d), functools.partial(lambda n, i: (i, 2 + n), n)) for n in range(3)]
    merged, *br = pl.pallas_call(
        body,
        out_shape=(jax.ShapeDtypeStruct((t_dim, d), BF16),) * 4,
        grid=(t_dim // tm,),
        in_specs=[row, row, row, pl.BlockSpec((3, d, d), lambda i: (0, 0, 0))] + gate,
        out_specs=(row, row, row, row),
        compiler_params=_params(("parallel",)),
        name=name,
    )(*ys, w_branch, proj_f, proj_f, proj_f)
    return merged, br


def merge_bwd(dz, w_out, proj_f, br, *, name):
    t_dim, d = dz.shape
    tr = _tile(t_dim, 512)

    def body(dz_ref, w_ref, g0, g1, g2, b0, b1, b2, d0, d1, d2, dg_ref):
        dm = lax.dot_general(dz_ref[...], w_ref[...], NT, preferred_element_type=F32)
        for n, (g, b, o) in enumerate(((g0, b0, d0), (g1, b1, d1), (g2, b2, d2))):
            sg = _sigmoid(g[...])
            o[...] = (dm * sg).astype(BF16)
            dg_ref[:, n * d:(n + 1) * d] = (dm * b[...].astype(F32) * sg * (1.0 - sg)).astype(BF16)

    row = pl.BlockSpec((tr, d), lambda i: (i, 0))
    gate = [pl.BlockSpec((tr, d), functools.partial(lambda n, i: (i, 2 + n), n)) for n in range(3)]
    return pl.pallas_call(
        body,
        out_shape=(jax.ShapeDtypeStruct((t_dim, d), BF16),) * 3 + (jax.ShapeDtypeStruct((t_dim, 3 * d), BF16),),
        grid=(t_dim // tr,),
        in_specs=[row, pl.BlockSpec((d, d), lambda i: (0, 0))] + gate + [row, row, row],
        out_specs=(row, row, row, pl.BlockSpec((tr, 3 * d), lambda i: (i, 0))),
        compiler_params=_params(("parallel",)),
        name=name,
    )(dz, w_out, proj_f, proj_f, proj_f, *br)


GELU_C = math.sqrt(2.0 / math.pi)
PAD = 8
SCAN_TILES = 8


def _gelu(x):
    return 0.5 * x * (1.0 + jnp.tanh(GELU_C * (x + 0.044715 * x * x * x)))


def _gelu_grad(x):
    t = jnp.tanh(GELU_C * (x + 0.044715 * x * x * x))
    return 0.5 * (1.0 + t) + 0.5 * x * (1.0 - t * t) * GELU_C * (1.0 + 3.0 * 0.044715 * x * x)


def _neg_expm1(x, exp_x):
    series = -x * (1.0 + x * (0.5 + x * (1.0 / 6.0)))
    return jnp.where(x > -0.02, series, 1.0 - exp_x)


def _lru_gates(xv, cw_ref, cb_ref, wr_ref, wi_ref, br_ref, bi_ref, lam_ref, pad_ref, s_len):
    pad_ref[pl.ds(0, PAD), :] = jnp.zeros((PAD, LANES), F32)
    pad_ref[pl.ds(PAD, s_len), :] = xv
    xc = cb_ref[...] + jnp.zeros((s_len, LANES), F32)
    for j in range(CONV_WIDTH):
        xc = xc + pad_ref[pl.ds(PAD - (CONV_WIDTH - 1) + j, s_len), :] * cw_ref[pl.ds(j, 1), :]
    xcb = xc.astype(BF16)
    r = _sigmoid(jnp.dot(xcb, wr_ref[0].astype(BF16), preferred_element_type=F32) + br_ref[...])
    i = _sigmoid(jnp.dot(xcb, wi_ref[0].astype(BF16), preferred_element_type=F32) + bi_ref[...])
    nl = -lam_ref[...]
    sp = jnp.maximum(nl, 0.0) + jnp.log(1.0 + jnp.exp(-jnp.abs(nl)))
    log_a = -LRU_C * r * sp
    a = jnp.exp(log_a)
    mult = jnp.sqrt(_neg_expm1(2.0 * log_a, a * a))
    return xc, r, i, sp, a, mult


def _tile_scan(a, b, row, reverse):
    for s in (1, 2, 4):
        if reverse:
            a_sh = pltpu.roll(a, 8 - s, 0)
            b_sh = pltpu.roll(b, 8 - s, 0)
            m = row + s <= 7
        else:
            a_sh = pltpu.roll(a, s, 0)
            b_sh = pltpu.roll(b, s, 0)
            m = row >= s
        b = jnp.where(m, a * b_sh + b, b)
        a = jnp.where(m, a * a_sh, a)
    return a, b


def lru_fwd(proj_f, conv_w, conv_b, wr_bd, wi_bd, b_rg, b_ig, lam, *, name, carried=None):
    bsz, s_len, _ = proj_f.shape
    d = D_MODEL
    ncb = d // LANES
    n_tiles = s_len // 8

    def body(x_ref, g_ref, cw_ref, cb_ref, wr_ref, wi_ref, br_ref, bi_ref, lam_ref, y_ref, h_ref, pad_ref, a_s, b_s):
        xc, r, i, sp, a, mult = _lru_gates(x_ref[0], cw_ref, cb_ref, wr_ref, wi_ref, br_ref, bi_ref, lam_ref, pad_ref, s_len)
        a_s[...] = a
        b_s[...] = mult * (i * xc)
        row = lax.broadcasted_iota(jnp.int32, (8, LANES), 0)

        def tiles(t, carry):
            starts = [pl.multiple_of((t * SCAN_TILES + u) * 8, 8) for u in range(SCAN_TILES)]
            local = [_tile_scan(a_s[pl.ds(i0, 8), :], b_s[pl.ds(i0, 8), :], row, False) for i0 in starts]
            for i0, (ac, hl) in zip(starts, local):
                h = hl + ac * carry
                h_ref[0, pl.ds(i0, 8), :] = h
                carry = jnp.broadcast_to(h[7:8, :], (8, LANES))
            return carry

        lax.fori_loop(0, n_tiles // SCAN_TILES, tiles, jnp.zeros((8, LANES), F32))
        y_ref[0] = (h_ref[0] * _gelu(g_ref[0])).astype(BF16)

    slab = lambda off: pl.BlockSpec((1, s_len, LANES), functools.partial(lambda o, c, b: (b, 0, o + c), off))
    vec = pl.BlockSpec((1, LANES), lambda c, b: (0, c))
    mat = pl.BlockSpec((1, LANES, LANES), lambda c, b: (c, 0, 0))
    out = pl.BlockSpec((1, s_len, LANES), lambda c, b: (b, 0, c))
    return call_with_exchange(
        body, carried,
        out_shape=(jax.ShapeDtypeStruct((bsz, s_len, d), BF16), jax.ShapeDtypeStruct((bsz, s_len, d), F32)),
        grid=(ncb, bsz),
        in_specs=[slab(0), slab(ncb), pl.BlockSpec((CONV_WIDTH, LANES), lambda c, b: (0, c)), vec, mat, mat, vec, vec, vec],
        out_specs=(out, out),
        scratch_shapes=[pltpu.VMEM((s_len + 2 * PAD, LANES), F32), pltpu.VMEM((s_len, LANES), F32), pltpu.VMEM((s_len, LANES), F32)],
        name=name,
        args=(proj_f, proj_f, conv_w, conv_b.reshape(1, d), wr_bd, wi_bd, b_rg.reshape(1, d), b_ig.reshape(1, d), lam.reshape(1, d)))


def lru_bwd(dy, proj_f, h, conv_w, conv_b, wr_bd, wi_bd, wr_bd_t, wi_bd_t, b_rg, b_ig, lam, *, name, carried=None):
    bsz, s_len, _ = proj_f.shape
    d = D_MODEL
    ncb = d // LANES
    n_tiles = s_len // 8

    def body(dy_ref, x_ref, g_ref, h_ref, cw_ref, cb_ref, wr_ref, wi_ref, wrt_ref, wit_ref, br_ref, bi_ref, lam_ref,
             dx_ref, dg_ref, dcw_ref, dcb_ref, dbr_ref, dbi_ref, dlam_ref, dwr_ref, dwi_ref, pad_ref, a_s, b_s, l_s):
        @pl.when(pl.program_id(1) == 0)
        def _():
            for ref in (dcw_ref, dcb_ref, dbr_ref, dbi_ref, dlam_ref, dwr_ref, dwi_ref):
                ref[...] = jnp.zeros_like(ref)

        xc, r, i, sp, a, mult = _lru_gates(x_ref[0], cw_ref, cb_ref, wr_ref, wi_ref, br_ref, bi_ref, lam_ref, pad_ref, s_len)
        gate = g_ref[0]
        hv = h_ref[0]
        dyv = dy_ref[0].astype(F32)
        dg_ref[0] = (dyv * hv * _gelu_grad(gate)).astype(BF16)
        b_s[...] = dyv * _gelu(gate)
        l_s[pl.ds(0, s_len), :] = a
        l_s[pl.ds(s_len, PAD), :] = jnp.zeros((PAD, LANES), F32)
        a_s[...] = l_s[pl.ds(1, s_len), :]
        row = lax.broadcasted_iota(jnp.int32, (8, LANES), 0)

        def tiles(t, carry):
            starts = [pl.multiple_of((n_tiles - 1 - (t * SCAN_TILES + u)) * 8, 8) for u in range(SCAN_TILES)]
            local = [_tile_scan(a_s[pl.ds(i0, 8), :], b_s[pl.ds(i0, 8), :], row, True) for i0 in starts]
            for i0, (ac, ll) in zip(starts, local):
                lmb = ll + ac * carry
                b_s[pl.ds(i0, 8), :] = lmb
                carry = jnp.broadcast_to(lmb[0:1, :], (8, LANES))
            return carry

        lax.fori_loop(0, n_tiles // SCAN_TILES, tiles, jnp.zeros((8, LANES), F32))
        lmb = b_s[...]
        l_s[pl.ds(0, PAD), :] = jnp.zeros((PAD, LANES), F32)
        l_s[pl.ds(PAD, s_len), :] = hv
        h_prev = l_s[pl.ds(PAD - 1, s_len), :]
        da = lmb * h_prev
        dmult = lmb * (i * xc)
        di = lmb * mult * xc
        dxc = lmb * mult * i
        dlog_a = da * a - dmult * a * a / mult
        dr = -LRU_C * sp * dlog_a
        dsp = jnp.sum(-LRU_C * r * dlog_a, axis=0, keepdims=True)
        dlam_ref[...] += dsp * (-_sigmoid(-lam_ref[...]))
        dpr = dr * r * (1.0 - r)
        dpi = di * i * (1.0 - i)
        dprb = dpr.astype(BF16)
        dpib = dpi.astype(BF16)
        xcb = xc.astype(BF16)
        dbr_ref[...] += jnp.sum(dpr, axis=0, keepdims=True)
        dbi_ref[...] += jnp.sum(dpi, axis=0, keepdims=True)
        tn = (((0,), (0,)), ((), ()))
        dwr_ref[0] += lax.dot_general(xcb, dprb, tn, preferred_element_type=F32)
        dwi_ref[0] += lax.dot_general(xcb, dpib, tn, preferred_element_type=F32)
        dxc = (dxc + jnp.dot(dprb, wrt_ref[0].astype(BF16), preferred_element_type=F32)
               + jnp.dot(dpib, wit_ref[0].astype(BF16), preferred_element_type=F32))
        dcb_ref[...] += jnp.sum(dxc, axis=0, keepdims=True)
        for j in range(CONV_WIDTH):
            dcw_ref[pl.ds(j, 1), :] += jnp.sum(dxc * pad_ref[pl.ds(PAD - (CONV_WIDTH - 1) + j, s_len), :], axis=0, keepdims=True)
        l_s[pl.ds(0, s_len), :] = dxc
        l_s[pl.ds(s_len, PAD), :] = jnp.zeros((PAD, LANES), F32)
        dx = jnp.zeros((s_len, LANES), F32)
        for j in range(CONV_WIDTH):
            dx = dx + l_s[pl.ds(CONV_WIDTH - 1 - j, s_len), :] * cw_ref[pl.ds(j, 1), :]
        dx_ref[0] = dx.astype(BF16)

    slab = lambda off: pl.BlockSpec((1, s_len, LANES), functools.partial(lambda o, c, b: (b, 0, o + c), off))
    vec = pl.BlockSpec((1, LANES), lambda c, b: (0, c))
    mat = pl.BlockSpec((1, LANES, LANES), lambda c, b: (c, 0, 0))
    cw = pl.BlockSpec((CONV_WIDTH, LANES), lambda c, b: (0, c))
    out = pl.BlockSpec((1, s_len, LANES), lambda c, b: (b, 0, c))
    vshape = jax.ShapeDtypeStruct((1, d), F32)
    mshape = jax.ShapeDtypeStruct((ncb, LANES, LANES), F32)
    return call_with_exchange(
        body, carried,
        out_shape=(jax.ShapeDtypeStruct((bsz, s_len, d), BF16),) * 2
        + (jax.ShapeDtypeStruct((CONV_WIDTH, d), F32), vshape, vshape, vshape, vshape, mshape, mshape),
        grid=(ncb, bsz),
        in_specs=[out, slab(0), slab(ncb), out, cw, vec, mat, mat, mat, mat, vec, vec, vec],
        out_specs=(out, out, cw, vec, vec, vec, vec, mat, mat),
        scratch_shapes=[pltpu.VMEM((s_len + 2 * PAD, LANES), F32), pltpu.VMEM((s_len, LANES), F32), pltpu.VMEM((s_len, LANES), F32),
                        pltpu.VMEM((s_len + 2 * PAD, LANES), F32)],
        name=name,
        args=(dy, proj_f, proj_f, h, conv_w, conv_b.reshape(1, d), wr_bd, wi_bd, wr_bd_t, wi_bd_t,
              b_rg.reshape(1, d), b_ig.reshape(1, d), lam.reshape(1, d)))


NT = (((1,), (1,)), ((), ()))
TN = (((0,), (0,)), ((), ()))
ATT_SCALE = HEAD_DIM ** -0.5


def _kv_place(head, n_kv_heads):
    kv = head // (N_HEADS // n_kv_heads)
    return kv // 2, kv % 2


def _band_mask(n, single):
    nk = ATT_BLOCK if single else 2 * ATT_BLOCK
    qi = lax.broadcasted_iota(jnp.int32, (2 * ATT_BLOCK, nk), 0) % ATT_BLOCK
    kj = lax.broadcasted_iota(jnp.int32, (2 * ATT_BLOCK, nk), 1)
    if single:
        return qi >= kj
    rel = qi + ATT_BLOCK - kj
    return (rel >= 0) & (rel <= ATT_BLOCK) & ((n > 0) | (kj >= ATT_BLOCK))


def _lane_halves():
    lane = lax.broadcasted_iota(jnp.int32, (1, LANES), 1)
    return lane < HEAD_DIM


def _stack_heads(t2, kh):
    first = _lane_halves()
    parts = []
    for a in range(2):
        ta = jnp.where(first if a == 0 else ~first, t2, jnp.zeros_like(t2))
        if a != kh[a]:
            ta = pltpu.roll(ta, HEAD_DIM, 1)
        parts.append(ta)
    return jnp.concatenate(parts, axis=0)


def _fold_heads(t, kh):
    t0, t1 = t[:ATT_BLOCK], t[ATT_BLOCK:]
    if t.shape[1] == LANES:
        if kh[0] != 0:
            t0 = pltpu.roll(t0, HEAD_DIM, 1)
        if kh[1] != 1:
            t1 = pltpu.roll(t1, HEAD_DIM, 1)
    return jnp.where(_lane_halves(), t0, t1)


def _rows_of_heads(t2):
    return jnp.concatenate([t2[:, 0:1], t2[:, HEAD_DIM:HEAD_DIM + 1]], axis=0)


PAIRS_AT_ONCE = 4


def _fill_bias(bias2_ref, bias1_ref=None):
    for i in range(2):
        bias2_ref[i] = jnp.where(_band_mask(i, False), 0.0, NEG_INF)
    if bias1_ref is not None:
        bias1_ref[...] = jnp.where(_band_mask(0, True), 0.0, NEG_INF)


def _pairs_fwd(items):
    ss = [lax.dot_general(_stack_heads(q2 * ATT_SCALE, kh), kk, NT, preferred_element_type=F32) + bias
          for q2, kk, _, bias, kh, _ in items]
    ps, ms, ls = [], [], []
    for s, (_, _, _, _, _, sink_col) in zip(ss, items):
        m = jnp.max(s, axis=-1, keepdims=True)
        if sink_col is not None:
            m = jnp.maximum(m, sink_col)
        p = jnp.exp(s - m)
        l = jnp.sum(p, axis=-1, keepdims=True)
        if sink_col is not None:
            l = l + jnp.exp(sink_col - m)
        ps.append(p.astype(BF16))
        ms.append(m)
        ls.append(l)
    pvs = [jnp.dot(p, it[2], preferred_element_type=F32) for p, it in zip(ps, items)]
    return list(zip(pvs, ms, ls))


def _pairs_bwd(items):
    first = _lane_halves()
    pre = []
    for q2, kk, vv, do2, o2, lse2, bias, kh in items:
        dd = do2 * o2
        dsum = jnp.concatenate([jnp.sum(jnp.where(first, dd, 0.0), axis=-1, keepdims=True),
                                jnp.sum(jnp.where(first, 0.0, dd), axis=-1, keepdims=True)], axis=0)
        qs = _stack_heads(q2 * ATT_SCALE, kh)
        dos = _stack_heads(do2.astype(BF16), kh)
        s = lax.dot_general(qs, kk, NT, preferred_element_type=F32) + bias
        dp = lax.dot_general(dos, vv, NT, preferred_element_type=F32)
        pre.append((qs, dos, s, dp, dsum))
    mid = []
    for (qs, dos, s, dp, dsum), it in zip(pre, items):
        p = jnp.exp(s - _rows_of_heads(it[5]))
        mid.append((p.astype(BF16), (p * (dp - dsum)).astype(BF16)))
    out = []
    for (pb, ds), (qs, dos, _, _, dsum), it in zip(mid, pre, items):
        dq = _fold_heads(jnp.dot(ds, it[1], preferred_element_type=F32), it[7]) * ATT_SCALE
        dk = lax.dot_general(ds, qs, TN, preferred_element_type=F32)
        dv = lax.dot_general(pb, dos, TN, preferred_element_type=F32)
        out.append((dq, dk, dv, dsum))
    return out


def swa_fwd(qkv, sinks, *, name, carried=None):
    bsz, s_len, width = qkv.shape
    ckv = SWA_KV_HEADS * HEAD_DIM
    nb = s_len // ATT_BLOCK
    kblk = D_MODEL // ckv

    def body(sink_ref, q_ref, kp_ref, kc_ref, vp_ref, vc_ref, o_ref, lse_ref, ob_ref, bias2):
        n = pl.program_id(1)
        _fill_bias(bias2)
        bias = bias2[jnp.minimum(n, 1)]
        kk = jnp.concatenate([kp_ref[0], kc_ref[0]], axis=0)
        vv = jnp.concatenate([vp_ref[0], vc_ref[0]], axis=0)
        top = lax.broadcasted_iota(jnp.int32, (2 * ATT_BLOCK, 1), 0) < ATT_BLOCK
        for hp0 in range(0, N_HEADS // 2, PAIRS_AT_ONCE):
            items, places = [], []
            for hp in range(hp0, hp0 + PAIRS_AT_ONCE):
                cols = slice(hp * LANES, (hp + 1) * LANES)
                kb, kh = _kv_place(2 * hp, SWA_KV_HEADS)
                kcols = slice(kb * LANES, (kb + 1) * LANES)
                sink_col = jnp.where(top, sink_ref[2 * hp], sink_ref[2 * hp + 1])
                items.append((q_ref[0, :, cols], kk[:, kcols], vv[:, kcols], bias, (kh, kh), sink_col))
                places.append((cols, (kh, kh)))
            for (pv, m, l), (cols, kh2) in zip(_pairs_fwd(items), places):
                o2 = _fold_heads(pv / l, kh2)
                o_ref[0, :, cols] = o2
                ob_ref[0, :, cols] = o2.astype(BF16)
                lse_ref[0, :, cols] = _fold_heads(m + jnp.log(l), kh2)

    prev = lambda n: jnp.maximum(n - 1, 0)
    out = pl.BlockSpec((1, ATT_BLOCK, D_MODEL), lambda b, n: (b, n, 0))
    sd = lambda dt: jax.ShapeDtypeStruct((bsz, s_len, D_MODEL), dt)
    return call_with_exchange(
        body, carried,
        out_shape=(sd(F32), sd(F32), sd(BF16)),
        grid=(bsz, nb),
        in_specs=[pl.BlockSpec(memory_space=pltpu.SMEM), out,
                  pl.BlockSpec((1, ATT_BLOCK, ckv), lambda b, n: (b, prev(n), kblk)),
                  pl.BlockSpec((1, ATT_BLOCK, ckv), lambda b, n: (b, n, kblk)),
                  pl.BlockSpec((1, ATT_BLOCK, ckv), lambda b, n: (b, prev(n), kblk + 1)),
                  pl.BlockSpec((1, ATT_BLOCK, ckv), lambda b, n: (b, n, kblk + 1))],
        out_specs=(out, out, out),
        scratch_shapes=[pltpu.VMEM((2, 2 * ATT_BLOCK, 2 * ATT_BLOCK), F32)],
        name=name,
        args=(sinks, qkv, qkv, qkv, qkv, qkv))


def swa_bwd(qkv, sinks, o, lse, do, *, name, carried=None):
    bsz, s_len, width = qkv.shape
    ckv = SWA_KV_HEADS * HEAD_DIM
    nb = s_len // ATT_BLOCK
    kblk = D_MODEL // ckv

    def body(sink_ref, q_ref, kp_ref, kc_ref, vp_ref, vc_ref, o_ref, lse_ref, do_ref, dq_ref, dk_ref, dv_ref, dsink_ref,
             dkk, dvv, ck, cv, bias2):
        n = pl.program_id(1)

        @pl.when((n == 0) & (pl.program_id(0) == 0))
        def _():
            dsink_ref[...] = jnp.zeros_like(dsink_ref)

        @pl.when(n < nb)
        def _():
            top = lax.broadcasted_iota(jnp.int32, (2 * ATT_BLOCK, 1), 0) < ATT_BLOCK
            lane = lax.broadcasted_iota(jnp.int32, dsink_ref.shape, 1)
            first_row = lax.broadcasted_iota(jnp.int32, dsink_ref.shape, 0) == 0
            _fill_bias(bias2)
            bias = bias2[jnp.minimum(n, 1)]
            kk = jnp.concatenate([kp_ref[0], kc_ref[0]], axis=0)
            vv = jnp.concatenate([vp_ref[0], vc_ref[0]], axis=0)
            dkk[...] = jnp.zeros_like(dkk)
            dvv[...] = jnp.zeros_like(dvv)
            for hp0 in range(0, N_HEADS // 2, PAIRS_AT_ONCE):
                items, places = [], []
                for hp in range(hp0, hp0 + PAIRS_AT_ONCE):
                    cols = slice(hp * LANES, (hp + 1) * LANES)
                    kb, kh = _kv_place(2 * hp, SWA_KV_HEADS)
                    kcols = slice(kb * LANES, (kb + 1) * LANES)
                    items.append((q_ref[0, :, cols], kk[:, kcols], vv[:, kcols], do_ref[0, :, cols], o_ref[0, :, cols],
                                  lse_ref[0, :, cols], bias, (kh, kh)))
                    places.append((cols, kcols, hp))
                for (dq, dk, dv, dsum), (cols, kcols, hp) in zip(_pairs_bwd(items), places):
                    dq_ref[0, :, cols] = dq
                    dkk[:, kcols] += dk
                    dvv[:, kcols] += dv
                    sink_col = jnp.where(top, sink_ref[2 * hp], sink_ref[2 * hp + 1])
                    t = -jnp.exp(sink_col - _rows_of_heads(lse_ref[0, :, cols])) * dsum
                    d0 = jnp.sum(t[:ATT_BLOCK], axis=0, keepdims=True)
                    d1 = jnp.sum(t[ATT_BLOCK:], axis=0, keepdims=True)
                    dsink_ref[...] += jnp.where(first_row & (lane == 2 * hp), d0, 0.0) + jnp.where(first_row & (lane == 2 * hp + 1), d1, 0.0)

        @pl.when((n >= 1) & (n < nb))
        def _():
            dk_ref[0] = ck[...] + dkk[pl.ds(0, ATT_BLOCK), :]
            dv_ref[0] = cv[...] + dvv[pl.ds(0, ATT_BLOCK), :]

        @pl.when(n == nb)
        def _():
            dk_ref[0] = ck[...]
            dv_ref[0] = cv[...]

        @pl.when(n < nb)
        def _():
            ck[...] = dkk[pl.ds(ATT_BLOCK, ATT_BLOCK), :]
            cv[...] = dvv[pl.ds(ATT_BLOCK, ATT_BLOCK), :]

    clamp = lambda n: jnp.minimum(n, nb - 1)
    prev = lambda n: jnp.maximum(n - 1, 0)
    row = pl.BlockSpec((1, ATT_BLOCK, D_MODEL), lambda b, n: (b, clamp(n), 0))
    kv_out = pl.BlockSpec((1, ATT_BLOCK, ckv), lambda b, n: (b, prev(n), 0))
    return call_with_exchange(
        body, carried,
        out_shape=(jax.ShapeDtypeStruct((bsz, s_len, D_MODEL), F32), jax.ShapeDtypeStruct((bsz, s_len, ckv), F32),
                   jax.ShapeDtypeStruct((bsz, s_len, ckv), F32), jax.ShapeDtypeStruct((8, LANES), F32)),
        grid=(bsz, nb + 1),
        in_specs=[pl.BlockSpec(memory_space=pltpu.SMEM), row,
                  pl.BlockSpec((1, ATT_BLOCK, ckv), lambda b, n: (b, prev(clamp(n)), kblk)),
                  pl.BlockSpec((1, ATT_BLOCK, ckv), lambda b, n: (b, clamp(n), kblk)),
                  pl.BlockSpec((1, ATT_BLOCK, ckv), lambda b, n: (b, prev(clamp(n)), kblk + 1)),
                  pl.BlockSpec((1, ATT_BLOCK, ckv), lambda b, n: (b, clamp(n), kblk + 1)),
                  row, row, row],
        out_specs=(row, kv_out, kv_out, pl.BlockSpec((8, LANES), lambda b, n: (0, 0))),
        scratch_shapes=[pltpu.VMEM((2 * ATT_BLOCK, ckv), F32), pltpu.VMEM((2 * ATT_BLOCK, ckv), F32),
                        pltpu.VMEM((ATT_BLOCK, ckv), F32), pltpu.VMEM((ATT_BLOCK, ckv), F32),
                        pltpu.VMEM((2, 2 * ATT_BLOCK, 2 * ATT_BLOCK), F32)],
        name=name,
        args=(sinks, qkv, qkv, qkv, qkv, qkv, o, lse, do))


DIL_PATTERNS = tuple((d, 2048 // d // ATT_BLOCK) for d in reversed(DILATIONS))
MHA = (0, 1)


def _dil_rows(idx, d, nb):
    j = idx // nb
    n = idx % nb
    base = j + n * (ATT_BLOCK * d)
    prev = jnp.maximum(base - ATT_BLOCK * d, j)
    if d == 1:
        return n, pl.ds(pl.multiple_of(base, ATT_BLOCK), ATT_BLOCK), pl.ds(pl.multiple_of(prev, ATT_BLOCK), ATT_BLOCK)
    return n, pl.ds(base, ATT_BLOCK, stride=d), pl.ds(prev, ATT_BLOCK, stride=d)


def dil_fwd(qkv, *, name, carried=None):
    bsz, s_len, _ = qkv.shape
    assert s_len == DIL_PATTERNS[0][0] * DIL_PATTERNS[0][1] * ATT_BLOCK
    npair = N_HEADS // 2

    def body(q_ref, k_ref, v_ref, y_ref, lse_ref, yb_ref, m_acc, l_acc, bias2, bias1):
        _fill_bias(bias2, bias1)
        for ci, (d, nb) in enumerate(DIL_PATTERNS):
            single = nb == 1

            def blocks(it, carry):
                items, places = [], []
                for u in range(PAIRS_AT_ONCE):
                    n, rows, prows = _dil_rows(it * PAIRS_AT_ONCE + u, d, nb)
                    kc = k_ref[rows, :].astype(BF16)
                    vc = v_ref[rows, :].astype(BF16)
                    if single:
                        kk, vv, bias = kc, vc, bias1[...]
                    else:
                        kk = jnp.concatenate([k_ref[prows, :].astype(BF16), kc], axis=0)
                        vv = jnp.concatenate([v_ref[prows, :].astype(BF16), vc], axis=0)
                        bias = bias2[jnp.minimum(n, 1)]
                    items.append((q_ref[rows, :].astype(BF16), kk, vv, bias, MHA, None))
                    places.append(rows)
                for (pv, m, l), rows in zip(_pairs_fwd(items), places):
                    o2, m2, l2 = _fold_heads(pv, MHA), _fold_heads(m, MHA), _fold_heads(l, MHA)
                    if ci == 0:
                        y_ref[rows, :] = o2
                        m_acc[rows, :] = m2
                        l_acc[rows, :] = l2
                    else:
                        m_old = m_acc[rows, :]
                        m_new = jnp.maximum(m_old, m2)
                        w_old = jnp.exp(m_old - m_new)
                        w_new = jnp.exp(m2 - m_new)
                        y_ref[rows, :] = y_ref[rows, :] * w_old + o2 * w_new
                        l_acc[rows, :] = l_acc[rows, :] * w_old + l2 * w_new
                        m_acc[rows, :] = m_new
                return carry

            lax.fori_loop(0, d * nb // PAIRS_AT_ONCE, blocks, 0)
        y = y_ref[...] / l_acc[...]
        y_ref[...] = y
        yb_ref[...] = y.astype(BF16)
        lse_ref[...] = m_acc[...] + jnp.log(l_acc[...])

    slab = lambda off: pl.BlockSpec((None, s_len, LANES), functools.partial(lambda o, b, h: (b, 0, o + h), off))
    sd = lambda dt: jax.ShapeDtypeStruct((bsz, s_len, D_MODEL), dt)
    return call_with_exchange(
        body, carried,
        out_shape=(sd(F32), sd(F32), sd(BF16)),
        grid=(bsz, npair),
        in_specs=[slab(0), slab(npair), slab(2 * npair)],
        out_specs=(slab(0), slab(0), slab(0)),
        scratch_shapes=[pltpu.VMEM((s_len, LANES), F32), pltpu.VMEM((s_len, LANES), F32),
                        pltpu.VMEM((2, 2 * ATT_BLOCK, 2 * ATT_BLOCK), F32), pltpu.VMEM((2 * ATT_BLOCK, ATT_BLOCK), F32)],
        name=name,
        args=(qkv, qkv, qkv))


def dil_bwd(qkv, y, lse, dy, *, name, carried=None):
    bsz, s_len, _ = qkv.shape
    npair = N_HEADS // 2

    def body(q_ref, k_ref, v_ref, y_ref, lse_ref, dy_ref, dq_ref, dk_ref, dv_ref, bias2, bias1):
        _fill_bias(bias2, bias1)
        assert DIL_PATTERNS[0][1] == 1
        for d, nb in DIL_PATTERNS:
            single = nb == 1

            def blocks(it, carry):
                items, places = [], []
                for u in range(PAIRS_AT_ONCE):
                    n, rows, prows = _dil_rows(it * PAIRS_AT_ONCE + u, d, nb)
                    kc = k_ref[rows, :].astype(BF16)
                    vc = v_ref[rows, :].astype(BF16)
                    if single:
                        kk, vv, bias = kc, vc, bias1[...]
                    else:
                        kk = jnp.concatenate([k_ref[prows, :].astype(BF16), kc], axis=0)
                        vv = jnp.concatenate([v_ref[prows, :].astype(BF16), vc], axis=0)
                        bias = bias2[jnp.minimum(n, 1)]
                    items.append((q_ref[rows, :].astype(BF16), kk, vv, dy_ref[rows, :], y_ref[rows, :], lse_ref[rows, :], bias, MHA))
                    places.append((rows, prows))
                for (dq, dk, dv, _), (rows, prows) in zip(_pairs_bwd(items), places):
                    if single:
                        dq_ref[rows, :] = dq
                        dk_ref[rows, :] = dk
                        dv_ref[rows, :] = dv
                    else:
                        dq_ref[rows, :] += dq
                        dk_ref[prows, :] += dk[:ATT_BLOCK]
                        dv_ref[prows, :] += dv[:ATT_BLOCK]
                        dk_ref[rows, :] += dk[ATT_BLOCK:]
                        dv_ref[rows, :] += dv[ATT_BLOCK:]
                return carry

            lax.fori_loop(0, d * nb // PAIRS_AT_ONCE, blocks, 0)

    slab = lambda off: pl.BlockSpec((None, s_len, LANES), functools.partial(lambda o, b, h: (b, 0, o + h), off))
    sd = jax.ShapeDtypeStruct((bsz, s_len, D_MODEL), F32)
    return call_with_exchange(
        body, carried,
        out_shape=(sd, sd, sd),
        grid=(bsz, npair),
        in_specs=[slab(0), slab(npair), slab(2 * npair), slab(0), slab(0), slab(0)],
        out_specs=(slab(0), slab(0), slab(0)),
        scratch_shapes=[pltpu.VMEM((2, 2 * ATT_BLOCK, 2 * ATT_BLOCK), F32), pltpu.VMEM((2 * ATT_BLOCK, ATT_BLOCK), F32)],
        name=name,
        args=(qkv, qkv, qkv, y, lse, dy))


def adamw(w, g, m, v, *, name):
    rows, cols = w.shape
    tr = _pick(rows, (256, 128, 64, 32, 16, 8))

    def body(w_ref, g_ref, m_ref, v_ref, d_ref, nm_ref, nv_ref):
        gv = g_ref[...]
        nm = ADAM_B1 * m_ref[...] + (1.0 - ADAM_B1) * gv
        nv = ADAM_B2 * v_ref[...] + (1.0 - ADAM_B2) * (gv * gv)
        m_hat = nm / (1.0 - ADAM_B1 ** ADAM_STEP)
        v_hat = nv / (1.0 - ADAM_B2 ** ADAM_STEP)
        d_ref[...] = -ADAM_LR * (m_hat / (jnp.sqrt(v_hat) + ADAM_EPS) + ADAM_WD * w_ref[...])
        nm_ref[...] = nm
        nv_ref[...] = nv

    row = pl.BlockSpec((tr, cols), lambda i: (i, 0))
    return pl.pallas_call(
        body,
        out_shape=(jax.ShapeDtypeStruct((rows, cols), F32),) * 3,
        grid=(rows // tr,),
        in_specs=[row] * 4,
        out_specs=(row, row, row),
        compiler_params=_params(("parallel",)),
        name=name,
    )(w, g, m, v)


def _place():
    return lax.axis_index("x"), lax.axis_index("y"), lax.axis_index("c")


def _gather_copies(x_ref, out_ref, send_sems, recv_sems):
    x, y, c = _place()
    me, sibling = (x, y, c), (x, y, 1 - c)
    chips = [(1 - x, y), (x, 1 - y), (1 - x, 1 - y)]

    def slot(px, py, pc):
        return out_ref.at[4 * px + 2 * py + pc]

    def copy(k, block, to, src=None):
        return pltpu.make_async_remote_copy(
            src_ref=slot(*block) if src is None else src, dst_ref=slot(*block),
            send_sem=send_sems.at[k], recv_sem=recv_sems.at[k], device_id=to, device_id_type=MESH)

    first = [lambda: copy(0, me, sibling, src=x_ref)] + [functools.partial(copy, 1 + j, me, (*chip, c), src=x_ref)
                                                         for j, chip in enumerate(chips)]
    passed = [functools.partial(copy, 4 + j, (*chip, c), sibling) for j, chip in enumerate(chips)]
    landing = [functools.partial(copy, 1 + j, (*chip, c), me) for j, chip in enumerate(chips)]
    from_sibling = [lambda: copy(0, sibling, me)] + [functools.partial(copy, 4 + j, (*chip, 1 - c), me) for j, chip in enumerate(chips)]
    return slot(*me), first, passed, landing, from_sibling


def _gather_start(x_ref, out_ref, send_sems, recv_sems, local_sem):
    mine, first, _, _, _ = _gather_copies(x_ref, out_ref, send_sems, recv_sems)
    pltpu.make_async_copy(x_ref, mine, local_sem).start()
    for cp in first:
        cp().start()


def _gather_finish(x_ref, out_ref, send_sems, recv_sems, local_sem):
    mine, first, passed, landing, from_sibling = _gather_copies(x_ref, out_ref, send_sems, recv_sems)
    for cp, fwd in zip(landing, passed):
        cp().wait_recv()
        fwd().start()
    for cp in from_sibling:
        cp().wait_recv()
    for cp in first + passed:
        cp().wait_send()
    pltpu.make_async_copy(x_ref, mine, local_sem).wait()


def _a2a_copies(x_ref, out_ref, send_sems, recv_sems):
    x, y, c = _place()
    me = 4 * x + 2 * y + c
    copies = []
    for k in range(1, N_DEV):
        px = 1 - x if k & 4 else x
        py = 1 - y if k & 2 else y
        pc = 1 - c if k & 1 else c
        copies.append(pltpu.make_async_remote_copy(
            src_ref=x_ref.at[4 * px + 2 * py + pc], dst_ref=out_ref.at[me], send_sem=send_sems.at[k - 1],
            recv_sem=recv_sems.at[k - 1], device_id=(px, py, pc), device_id_type=MESH))
    return me, copies


def _a2a_start(x_ref, out_ref, send_sems, recv_sems, local_sem):
    me, copies = _a2a_copies(x_ref, out_ref, send_sems, recv_sems)
    pltpu.make_async_copy(x_ref.at[me], out_ref.at[me], local_sem).start()
    for cp in copies:
        cp.start()


def _a2a_finish(x_ref, out_ref, send_sems, recv_sems, local_sem):
    me, copies = _a2a_copies(x_ref, out_ref, send_sems, recv_sems)
    for cp in copies:
        cp.wait_recv()
    for cp in copies:
        cp.wait_send()
    pltpu.make_async_copy(x_ref.at[me], out_ref.at[me], local_sem).wait()


EXCHANGES = {"gather": (_gather_start, _gather_finish, lambda x: (N_DEV,) + x.shape),
             "a2a": (_a2a_start, _a2a_finish, lambda x: x.shape)}
EXCHANGE_SEMS = [pltpu.SemaphoreType.DMA((7,)), pltpu.SemaphoreType.DMA((7,)), pltpu.SemaphoreType.DMA(())]


def exchange(kind, x, *, name):
    start, finish, shape = EXCHANGES[kind]

    def body(x_ref, out_ref, *sems):
        start(x_ref, out_ref, *sems)
        finish(x_ref, out_ref, *sems)

    return pl.pallas_call(
        body,
        out_shape=jax.ShapeDtypeStruct(shape(x), x.dtype),
        in_specs=[pl.BlockSpec(memory_space=pl.ANY)],
        out_specs=pl.BlockSpec(memory_space=pl.ANY),
        scratch_shapes=EXCHANGE_SEMS,
        name=name,
    )(x)


def call_with_exchange(body, carried, *, out_shape, grid, in_specs, out_specs, scratch_shapes, name, args):
    sem = ("arbitrary",) * len(grid)
    carried = list(carried or ())
    if not carried:
        res = pl.pallas_call(body, out_shape=out_shape, grid=grid, in_specs=in_specs, out_specs=out_specs,
                             scratch_shapes=scratch_shapes, compiler_params=_params(sem), name=name)(*args)
        return res, []
    n_in, n_out, n_scr, n_x = len(in_specs), len(out_shape), len(scratch_shapes), len(carried)
    n_sems = len(EXCHANGE_SEMS)

    def wrapped(*refs):
        ins, x_refs = refs[:n_in], refs[n_in:n_in + n_x]
        outs = refs[n_in + n_x:n_in + n_x + n_out]
        out_refs = refs[n_in + n_x + n_out:n_in + 2 * n_x + n_out]
        rest = refs[n_in + 2 * n_x + n_out:]
        scratch, sems = rest[:n_scr], rest[n_scr:]
        ids = [pl.program_id(i) for i in range(len(grid))]
        is_first = functools.reduce(lambda a, b: a & b, [i == 0 for i in ids])
        is_last = functools.reduce(lambda a, b: a & b, [i == g - 1 for i, g in zip(ids, grid)])

        @pl.when(is_first)
        def _():
            for e, (kind, _) in enumerate(carried):
                EXCHANGES[kind][0](x_refs[e], out_refs[e], *sems[e * n_sems:(e + 1) * n_sems])

        body(*ins, *outs, *scratch)

        @pl.when(is_last)
        def _():
            for e, (kind, _) in enumerate(carried):
                EXCHANGES[kind][1](x_refs[e], out_refs[e], *sems[e * n_sems:(e + 1) * n_sems])

    any_spec = pl.BlockSpec(memory_space=pl.ANY)
    res = pl.pallas_call(
        wrapped,
        out_shape=tuple(out_shape) + tuple(jax.ShapeDtypeStruct(EXCHANGES[kind][2](x), x.dtype) for kind, x in carried),
        grid=grid,
        in_specs=list(in_specs) + [any_spec] * n_x,
        out_specs=tuple(out_specs) + (any_spec,) * n_x,
        scratch_shapes=list(scratch_shapes) + EXCHANGE_SEMS * n_x,
        compiler_params=_params(sem),
        name=name + "".join("_" + kind for kind, _ in carried),
    )(*args, *[x for _, x in carried])
    return res[:n_out], list(res[n_out:])


def sum_slots(x, *, name):
    _, rows, cols = x.shape
    tr = _pick(rows, (512, 256, 128, 64, 32, 16))

    def body(x_ref, o_ref):
        acc = x_ref[0].astype(F32)
        for k in range(1, N_DEV):
            acc = acc + x_ref[k].astype(F32)
        o_ref[...] = acc

    return pl.pallas_call(
        body,
        out_shape=jax.ShapeDtypeStruct((rows, cols), F32),
        grid=(rows // tr,),
        in_specs=[pl.BlockSpec((N_DEV, tr, cols), lambda i: (0, i, 0))],
        out_specs=pl.BlockSpec((tr, cols), lambda i: (i, 0)),
        compiler_params=_params(("parallel",)),
        name=name,
    )(x)


BIG = ("w_in", "w_branch", "w_out", "w_ffn_in", "w_ffn_out")
SMALL = ("conv_b", "w_rg", "b_rg", "w_ig", "b_ig", "lru_lambda", "sinks", "ln1_g", "ln1_b", "ln2_g", "ln2_b")
N_LRU_BLOCKS = D_MODEL // HEAD_DIM
SMALL_ROWS_TILE = 512


def _block_diag(w):
    z = jnp.zeros((N_LRU_BLOCKS // 2, HEAD_DIM, HEAD_DIM), w.dtype)
    top = jnp.concatenate([w[0::2], z], axis=2)
    bot = jnp.concatenate([z, w[1::2]], axis=2)
    return jnp.concatenate([top, bot], axis=1)


def _block_diag_grad(g):
    return jnp.stack([g[:, :HEAD_DIM, :HEAD_DIM], g[:, HEAD_DIM:, HEAD_DIM:]], axis=1).reshape(N_LRU_BLOCKS, HEAD_DIM, HEAD_DIM)


def layer_fwd(x, xb, p, bsz, own_late=None, next_w_in=None):
    t_dim = x.shape[0]
    s_len = t_dim // bsz
    w_f, w_qs, w_qd = p["w_in_f"], p["w_in_qs"], p["w_in_qd"]
    proj_f = matmul(xb, w_f, name="proj_f")
    qs = matmul(xb, w_qs, out_dtype=BF16, name="proj_qs").reshape(bsz, s_len, W_QS)
    qd = matmul(xb, w_qd, name="proj_qd").reshape(bsz, s_len, W_QD)
    proj_f3 = proj_f.reshape(bsz, s_len, W_F)
    wr_bd, wi_bd = _block_diag(p["w_rg"]), _block_diag(p["w_ig"])
    (y_a, h), got_rows = lru_fwd(proj_f3, p["conv_w"], p["conv_b"], wr_bd, wi_bd, p["b_rg"], p["b_ig"], p["lru_lambda"],
                                 name="lru_fwd", carried=[("gather", own_late[1])] if own_late is not None else [])
    (y_b, lse_b, y_bb), got_fi = swa_fwd(qs, p["sinks"], name="swa_fwd", carried=[("gather", own_late[0])] if own_late is not None else [])
    (y_c, lse_c, y_cb), got_next = dil_fwd(qd, name="dil_fwd", carried=[("gather", next_w_in)] if next_w_in is not None else [])
    if own_late is not None:
        p = {**p, **_late_weights(got_fi[0], got_rows[0])}
    ys = [t.reshape(t_dim, D_MODEL) for t in (y_a, y_bb, y_cb)]
    merged, br = branch_merge(ys, p["w_branch"], proj_f, name="branch_merge")
    mix = matmul(merged, p["w_out"], name="w_out")
    x1, x1b, z1 = ln_fwd(x, mix, p["ln1_g"], p["ln1_b"], name="ln_fwd")
    h1, h3, act = ffn_in_swiglu(x1b, p["w_ffn_in"], name="ffn_in_swiglu")
    ffn = matmul(act, p["w_ffn_out"], name="ffn_out")
    x2, x2b, z2 = ln_fwd(x1, ffn, p["ln2_g"], p["ln2_b"], name="ln_fwd")
    saved = dict(xb=xb, proj_f=proj_f, qs=qs, qd=qd, h=h, ys=ys, y_b=y_b, y_c=y_c, lse_b=lse_b, lse_c=lse_c, br=br, merged=merged,
                 z1=z1, x1b=x1b, h1=h1, h3=h3, act=act, z2=z2, wr_bd=wr_bd, wi_bd=wi_bd, p=p)
    return x2, x2b, saved, (got_next[0] if got_next else None)


def layer_bwd(dx2, s, bsz, exchange_own=False, above_w_in=None):
    p = s["p"]
    t_dim = dx2.shape[0]
    s_len = t_dim // bsz
    g = {}
    dz2, dz2b, g["ln2_g"], g["ln2_b"] = ln_bwd(dx2, s["z2"], p["ln2_g"], name="ln_bwd")
    dh13 = swiglu_bwd(dz2b, p["w_ffn_out"], s["h1"], s["h3"], name="swiglu_bwd")
    g["w_ffn_out"] = matmul(s["act"], dz2b, trans_a=True, out_dtype=BF16, name="dw_ffn_out")
    g["w_ffn_in"] = matmul(s["x1b"], dh13, trans_a=True, out_dtype=BF16, name="dw_ffn_in")
    dx1 = matmul(dh13, p["w_ffn_in"], trans_b=True, add=dz2, add_scale=ALPHA, name="dx_ffn")
    dz1, dz1b, g["ln1_g"], g["ln1_b"] = ln_bwd(dx1, s["z1"], p["ln1_g"], name="ln_bwd")
    g["w_out"] = matmul(s["merged"], dz1b, trans_a=True, out_dtype=BF16, name="dw_out")
    *dbr, dgates = merge_bwd(dz1b, p["w_out"], s["proj_f"], s["br"], name="merge_bwd")
    dys = [matmul(dbr[n], p["w_branch"][n], trans_b=True, out_dtype=F32 if n == 2 else BF16, name="d_branch") for n in range(3)]
    g["w_branch"] = jnp.stack([matmul(s["ys"][n], dbr[n], trans_a=True, out_dtype=BF16, name="dw_branch") for n in range(3)])
    fi_slots, rows_slots = _late_slots(g) if exchange_own else (None, None)
    shape3 = (bsz, s_len, D_MODEL)
    (dlx, dlg, g["conv_w"], g["conv_b"], g["b_rg"], g["b_ig"], g["lru_lambda"], dwr, dwi), got_rows = lru_bwd(
        dys[0].reshape(shape3), s["proj_f"].reshape(bsz, s_len, W_F), s["h"], p["conv_w"], p["conv_b"], s["wr_bd"], s["wi_bd"],
        jnp.swapaxes(s["wr_bd"], 1, 2), jnp.swapaxes(s["wi_bd"], 1, 2), p["b_rg"], p["b_ig"], p["lru_lambda"], name="lru_bwd",
        carried=[("a2a", rows_slots)] if exchange_own else [])
    g["w_rg"], g["w_ig"] = _block_diag_grad(dwr), _block_diag_grad(dwi)
    dy_b3 = dys[1].reshape(shape3)
    (*dqs, dsinks), got_fi = swa_bwd(s["qs"], p["sinks"], s["y_b"], s["lse_b"], dy_b3, name="swa_bwd",
                                     carried=[("a2a", fi_slots)] if exchange_own else [])
    g["sinks"] = dsinks[0, :N_HEADS]
    dqd, got_in = dil_bwd(s["qd"], s["y_c"], s["lse_c"], dys[2].reshape(shape3), name="dil_bwd",
                          carried=[("a2a", above_w_in)] if above_w_in is not None else [])
    flat = lambda t: t.reshape(t_dim, t.shape[-1])
    dproj_f = jnp.concatenate([flat(dlx), flat(dlg), dgates], axis=1)
    dproj_qs = jnp.concatenate([flat(t) for t in dqs], axis=1).astype(BF16)
    dproj_qd = jnp.concatenate([flat(t) for t in dqd], axis=1).astype(BF16)
    g["w_in_f"] = matmul(s["xb"], dproj_f, trans_a=True, out_dtype=BF16, name="dw_in_f")
    g["w_in_qs"] = matmul(s["xb"], dproj_qs, trans_a=True, out_dtype=BF16, name="dw_in_qs")
    g["w_in_qd"] = matmul(s["xb"], dproj_qd, trans_a=True, out_dtype=BF16, name="dw_in_qd")
    dx = matmul(dproj_f, p["w_in_f"], trans_b=True, add=dz1, add_scale=ALPHA, name="dx_f")
    dx = matmul(dproj_qs, p["w_in_qs"], trans_b=True, add=dx, name="dx_qs")
    dx = matmul(dproj_qd, p["w_in_qd"], trans_b=True, add=dx, name="dx_qd")
    g = {k: (v.reshape(p[k].shape) if k in p else v) for k, v in g.items()}
    return dx, g, dict(late=(got_fi[0], got_rows[0]) if exchange_own else None, w_in=got_in[0] if got_in else None)


def local_step(x, target, layer_params, layer_shards=None, first_w_in=None):
    bsz, s_len, d = x.shape
    t_dim = bsz * s_len
    xf = x.reshape(t_dim, d)
    xb = xf.astype(BF16)
    exchanging = layer_shards is not None
    saved, gathered = [], first_w_in
    for l in range(DEPTH):
        p = layer_params(l, gathered)
        xf, xb, s, gathered = layer_fwd(xf, xb, p, bsz, own_late=layer_shards[l][1:] if exchanging else None,
                                        next_w_in=layer_shards[l + 1][0] if exchanging and l + 1 < DEPTH else None)
        saved.append(s)
    dy, sq = loss_head(xf, target.reshape(t_dim, d), name="loss_head")
    loss = 0.5 * jnp.sum(sq) / d
    grads, received, w_in_slots = [None] * DEPTH, [[None] * 3 for _ in range(DEPTH)], None
    for l in reversed(range(DEPTH)):
        dy, grads[l], got = layer_bwd(dy, saved[l], bsz, exchange_own=exchanging, above_w_in=w_in_slots)
        if got["w_in"] is not None:
            received[l + 1][0] = got["w_in"]
        if exchanging:
            received[l][1:] = got["late"]
            w_in_slots = _w_in_slots(grads[l])
    return loss, dy.reshape(bsz, s_len, d), grads, received, w_in_slots


W_IN_SEGMENTS = (("w_in_f", 0, 0, 2 * D_MODEL), ("w_in_qs", 0, 2 * D_MODEL, W_QS), ("w_in_qd", 0, 2 * D_MODEL + W_QS, W_QD),
                 ("w_in_f", 2 * D_MODEL, 2 * D_MODEL + W_QS + W_QD, 3 * D_MODEL))
ROW_SHARDED = ("w_branch", "w_out", "w_ffn_out")


def _cols_of_shards(shards, lo, hi):
    width = shards[0].shape[-1]
    parts = []
    for k, sh in enumerate(shards):
        a, b = max(lo, k * width), min(hi, (k + 1) * width)
        if a < b:
            parts.append(sh[..., a - k * width:b - k * width])
    return parts[0] if len(parts) == 1 else jnp.concatenate(parts, axis=-1)


def _cols_of_w_in(pieces, lo, hi):
    parts = []
    for name, p0, l0, width in W_IN_SEGMENTS:
        a, b = max(lo, l0), min(hi, l0 + width)
        if a < b:
            parts.append(pieces[name][..., p0 + a - l0:p0 + b - l0])
    return parts[0] if len(parts) == 1 else jnp.concatenate(parts, axis=-1)


W_IN_COLS = W_F + W_QS + W_QD


def _layer_shards(w):
    rows = jnp.concatenate([w[k].reshape(DEPTH, -1, D_MODEL) for k in ROW_SHARDED], axis=1).astype(BF16)
    w_in, w_fi = w["w_in"].astype(BF16), w["w_ffn_in"].astype(BF16)
    return [(w_in[l], w_fi[l], rows[l]) for l in range(DEPTH)]


ROW_COUNTS = (3 * D_MODEL // N_DEV, D_MODEL // N_DEV, FF_HIDDEN // N_DEV)


def _w_in_weights(g_in):
    sh = [g_in[k] for k in range(N_DEV)]
    return dict(w_in_f=jnp.concatenate([_cols_of_shards(sh, 0, 2 * D_MODEL), _cols_of_shards(sh, W_IN_COLS - 3 * D_MODEL, W_IN_COLS)], axis=-1),
                w_in_qs=_cols_of_shards(sh, 2 * D_MODEL, 2 * D_MODEL + W_QS),
                w_in_qd=_cols_of_shards(sh, 2 * D_MODEL + W_QS, 2 * D_MODEL + W_QS + W_QD))


def _late_weights(g_fi, g_rows):
    p = dict(w_ffn_in=jnp.concatenate([g_fi[k] for k in range(N_DEV)], axis=-1))
    off = 0
    for k, n in zip(ROW_SHARDED, ROW_COUNTS):
        t = g_rows[:, off:off + n]
        if k == "w_branch":
            p[k] = jnp.transpose(t.reshape(N_DEV, 3, n // 3, D_MODEL), (1, 0, 2, 3)).reshape(3, -1, D_MODEL)
        else:
            p[k] = t.reshape(-1, D_MODEL)
        off += n
    return p


def _w_in_slots(g):
    shard = W_IN_COLS // N_DEV
    return jnp.stack([_cols_of_w_in(g, k * shard, (k + 1) * shard) for k in range(N_DEV)]).astype(BF16)


def _late_slots(g):
    shard = g["w_ffn_in"].shape[-1] // N_DEV
    s_fi = jnp.stack([g["w_ffn_in"][:, k * shard:(k + 1) * shard] for k in range(N_DEV)]).astype(BF16)
    rows = jnp.concatenate([jnp.transpose(g["w_branch"].reshape(3, N_DEV, -1, D_MODEL), (1, 0, 2, 3)).reshape(N_DEV, -1, D_MODEL),
                            g["w_out"].reshape(N_DEV, -1, D_MODEL), g["w_ffn_out"].reshape(N_DEV, -1, D_MODEL)], axis=1).astype(BF16)
    return s_fi, rows


def _pad_rows(flat, tile_rows):
    n = flat.shape[0]
    per = tile_rows * LANES
    total = -(-n // per) * per
    return jnp.pad(flat, (0, total - n)).reshape(-1, LANES)


def kernel(x, w_in, conv_w, conv_b, w_rg, b_rg, w_ig, b_ig, lru_lambda, sinks, w_branch, w_out, ln1_g, ln1_b, w_ffn_in, w_ffn_out, ln2_g, ln2_b, loss_target, m_w_in, m_conv_w, m_conv_b, m_w_rg, m_b_rg, m_w_ig, m_b_ig, m_lru_lambda, m_sinks, m_w_branch, m_w_out, m_ln1_g, m_ln1_b, m_w_ffn_in, m_w_ffn_out, m_ln2_g, m_ln2_b, v_w_in, v_conv_w, v_conv_b, v_w_rg, v_b_rg, v_w_ig, v_b_ig, v_lru_lambda, v_sinks, v_w_branch, v_w_out, v_ln1_g, v_ln1_b, v_w_ffn_in, v_w_ffn_out, v_ln2_g, v_ln2_b):
    w = dict(w_in=w_in, conv_w=conv_w, conv_b=conv_b, w_rg=w_rg, b_rg=b_rg, w_ig=w_ig, b_ig=b_ig, lru_lambda=lru_lambda, sinks=sinks,
             w_branch=w_branch, w_out=w_out, ln1_g=ln1_g, ln1_b=ln1_b, w_ffn_in=w_ffn_in, w_ffn_out=w_ffn_out, ln2_g=ln2_g, ln2_b=ln2_b)
    m = dict(w_in=m_w_in, conv_w=m_conv_w, conv_b=m_conv_b, w_rg=m_w_rg, b_rg=m_b_rg, w_ig=m_w_ig, b_ig=m_b_ig, lru_lambda=m_lru_lambda,
             sinks=m_sinks, w_branch=m_w_branch, w_out=m_w_out, ln1_g=m_ln1_g, ln1_b=m_ln1_b, w_ffn_in=m_w_ffn_in, w_ffn_out=m_w_ffn_out,
             ln2_g=m_ln2_g, ln2_b=m_ln2_b)
    v = dict(w_in=v_w_in, conv_w=v_conv_w, conv_b=v_conv_b, w_rg=v_w_rg, b_rg=v_b_rg, w_ig=v_w_ig, b_ig=v_b_ig, lru_lambda=v_lru_lambda,
             sinks=v_sinks, w_branch=v_w_branch, w_out=v_w_out, ln1_g=v_ln1_g, ln1_b=v_ln1_b, w_ffn_in=v_w_ffn_in, w_ffn_out=v_w_ffn_out,
             ln2_g=v_ln2_g, ln2_b=v_ln2_b)
    order = ["w_in", "conv_w", "conv_b", "w_rg", "b_rg", "w_ig", "b_ig", "lru_lambda", "sinks", "w_branch", "w_out", "ln1_g", "ln1_b",
             "w_ffn_in", "w_ffn_out", "ln2_g", "ln2_b"]
    me = 4 * lax.axis_index("x") + 2 * lax.axis_index("y") + lax.axis_index("c")

    names = ("w_in", "w_ffn_in", "w_rows")
    shards = _layer_shards(w)
    first_w_in = exchange("gather", shards[0][0], name="gather_w_in")
    cw = exchange("gather", conv_w.reshape(-1, LANES), name="gather_conv_w")
    conv_w_full = jnp.moveaxis(cw.reshape(N_DEV, DEPTH, CONV_WIDTH, LANES), 0, 2).reshape(DEPTH, CONV_WIDTH, D_MODEL)

    def layer_params(l, gathered_w_in):
        return {**_w_in_weights(gathered_w_in), **{k: w[k][l] for k in SMALL}, "conv_w": conv_w_full[l]}

    loss_local, grad_x, grads, received, w_in_slots = local_step(x, loss_target, layer_params, shards, first_w_in)
    loss = lax.psum(loss_local, ("x", "y", "c"))
    received[0][0] = exchange("a2a", w_in_slots, name="exchange_g_w_in")

    sums = [[sum_slots(t, name=f"sum_g_{n}") for t, n in zip(received[l], names)] for l in range(DEPTH)]
    g_final = {"w_in": jnp.stack([sums[l][0] for l in range(DEPTH)]), "w_ffn_in": jnp.stack([sums[l][1] for l in range(DEPTH)])}
    off = 0
    for k, n in zip(ROW_SHARDED, ROW_COUNTS):
        g_final[k] = jnp.stack([sums[l][2][off:off + n] for l in range(DEPTH)]).reshape(w[k].shape)
        off += n
    grads = {k: jnp.stack([grads[l][k] for l in range(DEPTH)]) for k in list(SMALL) + ["conv_w"]}

    small_names = list(SMALL) + ["conv_w"]
    small_sizes = [grads[k].size for k in small_names]
    svec = _pad_rows(jnp.concatenate([grads[k].reshape(-1) for k in small_names]), SMALL_ROWS_TILE)
    ssum = sum_slots(exchange("gather", svec, name="gather_small_grads"), name="sum_small_grads")
    sflat, off = ssum.reshape(-1), 0
    for k, n in zip(small_names, small_sizes):
        g_final[k] = sflat[off:off + n].reshape(grads[k].shape)
        off += n
    g_final["conv_w"] = lax.dynamic_slice_in_dim(g_final["conv_w"], me * LANES, LANES, axis=2)

    delta, new_m, new_v = {}, {}, {}
    for k in list(BIG) + ["conv_w"]:
        cols = w[k].shape[-1]
        two_d = lambda t: t.reshape(-1, cols)
        d_, m_, v_ = adamw(two_d(w[k]), two_d(g_final[k]), two_d(m[k]), two_d(v[k]), name=f"adamw_{k}")
        delta[k], new_m[k], new_v[k] = d_.reshape(w[k].shape), m_.reshape(w[k].shape), v_.reshape(w[k].shape)
    pack_small = lambda dct: _pad_rows(jnp.concatenate([dct[k].reshape(-1) for k in SMALL]), SMALL_ROWS_TILE)
    d_, m_, v_ = adamw(pack_small(w), pack_small(g_final), pack_small(m), pack_small(v), name="adamw_small")
    off = 0
    for k in SMALL:
        n = w[k].size
        for dst, src in ((delta, d_), (new_m, m_), (new_v, v_)):
            dst[k] = src.reshape(-1)[off:off + n].reshape(w[k].shape)
        off += n
    return (loss, grad_x, *[g_final[k] for k in order], *[delta[k] for k in order], *[new_m[k] for k in order], *[new_v[k] for k in order])
```

```python
import functools
import math

import jax
import jax.numpy as jnp
from jax import lax
from jax.experimental import pallas as pl
from jax.experimental.pallas import tpu as pltpu

F32 = jnp.float32
BF16 = jnp.bfloat16

N_DEV = 8
DEPTH = 4
D_MODEL = 1024
HEAD_DIM = 64
LANES = 128
N_HEADS = D_MODEL // HEAD_DIM
SWA_KV_HEADS = 4
ATT_BLOCK = 128
DILATIONS = (1, 4, 16)
CONV_WIDTH = 4
LRU_C = 8.0
FF_HIDDEN = 2816
ALPHA = (2.0 * DEPTH) ** 0.25
LN_EPS = 1e-5
NEG_INF = -1e30
W_F = 5 * D_MODEL
W_QS = D_MODEL + 2 * SWA_KV_HEADS * HEAD_DIM
W_QD = 3 * D_MODEL

ADAM_LR = 0.001
ADAM_B1 = 0.9
ADAM_B2 = 0.999
ADAM_EPS = 1e-08
ADAM_WD = 0.01
ADAM_STEP = 10

VMEM_LIMIT = 56 * 1024 * 1024
MATMUL_BLOCK_BYTES = 40 * 1024 * 1024
MESH = pl.DeviceIdType.MESH


def _pick(n, cands):
    for c in cands:
        if n % c == 0:
            return c
    raise ValueError(f"no tile for {n} among {cands}")


def _params(sem):
    return pltpu.CompilerParams(dimension_semantics=sem, vmem_limit_bytes=VMEM_LIMIT)


def _tile(n, cap):
    best = None
    for t in range(LANES, cap + 1, LANES):
        if n % t == 0:
            best = t
    assert best is not None, (n, cap)
    return best


def matmul(a, b, *, name, trans_a=False, trans_b=False, out_dtype=F32, add=None, add_scale=1.0):
    if trans_a:
        k_dim, m_dim = a.shape
    else:
        m_dim, k_dim = a.shape
    n_dim = b.shape[0] if trans_b else b.shape[1]
    assert (b.shape[1] if trans_b else b.shape[0]) == k_dim
    tn = _tile(n_dim, 1408)
    tm, tk = _tile(m_dim, 1024), _tile(k_dim, 1408)
    for cand in (1024, 512, 256):
        ctm = _tile(m_dim, cand)
        blocks = 2 * (ctm * k_dim * a.dtype.itemsize + tn * k_dim * b.dtype.itemsize + ctm * tn * jnp.dtype(out_dtype).itemsize
                      + (ctm * tn * add.dtype.itemsize if add is not None else 0))
        if blocks <= MATMUL_BLOCK_BYTES:
            tm, tk = ctm, k_dim
            break
    nk = k_dim // tk
    dims = (((0 if trans_a else 1,), (1 if trans_b else 0,)), ((), ()))

    def body(*refs):
        if add is None:
            a_ref, b_ref, o_ref, acc_ref = refs
            add_ref = None
        else:
            a_ref, b_ref, add_ref, o_ref, acc_ref = refs
        k = pl.program_id(2)
        part = lax.dot_general(a_ref[...].astype(BF16), b_ref[...].astype(BF16), dims, preferred_element_type=F32)

        def finish(r):
            if add_ref is not None:
                r = r + add_scale * add_ref[...].astype(F32)
            o_ref[...] = r.astype(out_dtype)

        if nk == 1:
            finish(part)
        else:
            @pl.when(k == 0)
            def _():
                acc_ref[...] = part

            @pl.when((k > 0) & (k < nk - 1))
            def _():
                acc_ref[...] += part

            @pl.when(k == nk - 1)
            def _():
                finish(acc_ref[...] + part)

    a_spec = pl.BlockSpec((tk, tm), lambda i, j, k: (k, i)) if trans_a else pl.BlockSpec((tm, tk), lambda i, j, k: (i, k))
    b_spec = pl.BlockSpec((tn, tk), lambda i, j, k: (j, k)) if trans_b else pl.BlockSpec((tk, tn), lambda i, j, k: (k, j))
    in_specs = [a_spec, b_spec]
    args = [a, b]
    if add is not None:
        in_specs.append(pl.BlockSpec((tm, tn), lambda i, j, k: (i, j)))
        args.append(add)
    return pl.pallas_call(
        body,
        out_shape=jax.ShapeDtypeStruct((m_dim, n_dim), out_dtype),
        grid=(m_dim // tm, n_dim // tn, nk),
        in_specs=in_specs,
        out_specs=pl.BlockSpec((tm, tn), lambda i, j, k: (i, j)),
        scratch_shapes=[pltpu.VMEM((tm, tn) if nk > 1 else (8, LANES), F32)],
        compiler_params=_params(("parallel", "parallel", "arbitrary")),
        name=name,
    )(*args)


def ln_fwd(x, a, w, g, b, *, name):
    t_dim, d = x.shape
    k_dim = a.shape[1]
    tr = _tile(t_dim, 512)

    def body(x_ref, a_ref, w_ref, g_ref, b_ref, y_ref, yb_ref, z_ref):
        z = ALPHA * x_ref[...] + jnp.dot(a_ref[...], w_ref[...], preferred_element_type=F32)
        mu = jnp.mean(z, axis=-1, keepdims=True)
        zc = z - mu
        var = jnp.mean(zc * zc, axis=-1, keepdims=True)
        y = zc * lax.rsqrt(var + LN_EPS) * g_ref[...] + b_ref[...]
        y_ref[...] = y
        yb_ref[...] = y.astype(BF16)
        z_ref[...] = z

    row = pl.BlockSpec((tr, d), lambda i: (i, 0))
    vec = pl.BlockSpec((1, d), lambda i: (0, 0))
    return pl.pallas_call(
        body,
        out_shape=(jax.ShapeDtypeStruct((t_dim, d), F32), jax.ShapeDtypeStruct((t_dim, d), BF16), jax.ShapeDtypeStruct((t_dim, d), F32)),
        grid=(t_dim // tr,),
        in_specs=[row, pl.BlockSpec((tr, k_dim), lambda i: (i, 0)), pl.BlockSpec((k_dim, d), lambda i: (0, 0)), vec, vec],
        out_specs=(row, row, row),
        compiler_params=_params(("parallel",)),
        name=name,
    )(x, a, w, g.reshape(1, d), b.reshape(1, d))


def ln_bwd(dy, z, g, *, name):
    t_dim, d = dy.shape
    tr = _pick(t_dim, (256, 128, 8))

    def body(dy_ref, z_ref, g_ref, dz_ref, dzb_ref, dg_ref, db_ref):
        @pl.when(pl.program_id(0) == 0)
        def _():
            dg_ref[...] = jnp.zeros_like(dg_ref)
            db_ref[...] = jnp.zeros_like(db_ref)

        z = z_ref[...]
        dyv = dy_ref[...]
        mu = jnp.mean(z, axis=-1, keepdims=True)
        zc = z - mu
        var = jnp.mean(zc * zc, axis=-1, keepdims=True)
        rstd = lax.rsqrt(var + LN_EPS)
        xhat = zc * rstd
        dxhat = dyv * g_ref[...]
        m1 = jnp.mean(dxhat, axis=-1, keepdims=True)
        m2 = jnp.mean(dxhat * xhat, axis=-1, keepdims=True)
        dz = rstd * (dxhat - m1 - xhat * m2)
        dz_ref[...] = dz
        dzb_ref[...] = dz.astype(BF16)
        dg_ref[...] += jnp.sum(dyv * xhat, axis=0, keepdims=True)
        db_ref[...] += jnp.sum(dyv, axis=0, keepdims=True)

    row = pl.BlockSpec((tr, d), lambda i: (i, 0))
    vec = pl.BlockSpec((1, d), lambda i: (0, 0))
    return pl.pallas_call(
        body,
        out_shape=(jax.ShapeDtypeStruct((t_dim, d), F32), jax.ShapeDtypeStruct((t_dim, d), BF16),
                   jax.ShapeDtypeStruct((1, d), F32), jax.ShapeDtypeStruct((1, d), F32)),
        grid=(t_dim // tr,),
        in_specs=[row, row, vec],
        out_specs=(row, row, vec, vec),
        compiler_params=_params(("arbitrary",)),
        name=name,
    )(dy, z, g.reshape(1, d))


def loss_head(y, target, *, name):
    t_dim, d = y.shape
    tr = _pick(t_dim, (256, 128, 8))

    def body(y_ref, t_ref, dy_ref, sq_ref):
        @pl.when(pl.program_id(0) == 0)
        def _():
            sq_ref[...] = jnp.zeros_like(sq_ref)

        diff = y_ref[...] - t_ref[...]
        dy_ref[...] = diff / d
        sq_ref[...] += jnp.sum(diff * diff, axis=0, keepdims=True)

    row = pl.BlockSpec((tr, d), lambda i: (i, 0))
    vec = pl.BlockSpec((1, d), lambda i: (0, 0))
    return pl.pallas_call(
        body,
        out_shape=(jax.ShapeDtypeStruct((t_dim, d), F32), jax.ShapeDtypeStruct((1, d), F32)),
        grid=(t_dim // tr,),
        in_specs=[row, row],
        out_specs=(row, vec),
        compiler_params=_params(("arbitrary",)),
        name=name,
    )(y, target)


def _sigmoid(x):
    return 0.5 * jnp.tanh(0.5 * x) + 0.5


def ffn_in_swiglu(x, w, *, name):
    t_dim, d = x.shape
    f = w.shape[1] // 2
    tm, tn = _tile(t_dim, 1024), _tile(f, 1408)
    nf = f // tn

    def body(x_ref, w1_ref, w3_ref, h1_ref, h3_ref, act_ref):
        xv = x_ref[...]
        h1 = jnp.dot(xv, w1_ref[...], preferred_element_type=F32)
        h3 = jnp.dot(xv, w3_ref[...], preferred_element_type=F32)
        h1_ref[...] = h1.astype(BF16)
        h3_ref[...] = h3.astype(BF16)
        act_ref[...] = (h1 * _sigmoid(h1) * h3).astype(BF16)

    out = pl.BlockSpec((tm, tn), lambda i, j: (i, j))
    return pl.pallas_call(
        body,
        out_shape=(jax.ShapeDtypeStruct((t_dim, f), BF16),) * 3,
        grid=(t_dim // tm, nf),
        in_specs=[pl.BlockSpec((tm, d), lambda i, j: (i, 0)), pl.BlockSpec((d, tn), lambda i, j: (0, j)),
                  pl.BlockSpec((d, tn), lambda i, j: (0, j + nf))],
        out_specs=(out, out, out),
        compiler_params=_params(("parallel", "parallel")),
        name=name,
    )(x, w, w)


def swiglu_bwd(dz, w_ffn_out, h1, h3, *, name):
    t_dim, d = dz.shape
    f = h1.shape[1]
    tr = _tile(t_dim, 512)

    def body(dz_ref, w_ref, h1_ref, h3_ref, dh_ref):
        da = lax.dot_general(dz_ref[...], w_ref[...], NT, preferred_element_type=F32)
        h1 = h1_ref[...].astype(F32)
        sg = _sigmoid(h1)
        dh_ref[:, :f] = (da * h3_ref[...].astype(F32) * sg * (1.0 + h1 * (1.0 - sg))).astype(BF16)
        dh_ref[:, f:] = (da * h1 * sg).astype(BF16)

    wide = pl.BlockSpec((tr, f), lambda i: (i, 0))
    return pl.pallas_call(
        body,
        out_shape=jax.ShapeDtypeStruct((t_dim, 2 * f), BF16),
        grid=(t_dim // tr,),
        in_specs=[pl.BlockSpec((tr, d), lambda i: (i, 0)), pl.BlockSpec((f, d), lambda i: (0, 0)), wide, wide],
        out_specs=pl.BlockSpec((tr, 2 * f), lambda i: (i, 0)),
        compiler_params=_params(("parallel",)),
        name=name,
    )(dz, w_ffn_out, h1, h3)


def branch_merge(ys, w_branch, proj_f, *, name):
    t_dim, d = ys[0].shape
    tm = _tile(t_dim, 512)

    def body(y0, y1, y2, w_ref, g0, g1, g2, m_ref, b0, b1, b2):
        acc = None
        for n, (y, g, b) in enumerate(((y0, g0, b0), (y1, g1, b1), (y2, g2, b2))):
            br = jnp.dot(y[...], w_ref[n], preferred_element_type=F32)
            b[...] = br.astype(BF16)
            t = _sigmoid(g[...]) * br
            acc = t if acc is None else acc + t
        m_ref[...] = acc.astype(BF16)

    row = pl.BlockSpec((tm, d), lambda i: (i, 0))
    gate = [pl.BlockSpec((tm, d), functools.partial(lambda n, i: (i, 2 + n), n)) for n in range(3)]
    merged, *br = pl.pallas_call(
        body,
        out_shape=(jax.ShapeDtypeStruct((t_dim, d), BF16),) * 4,
        grid=(t_dim // tm,),
        in_specs=[row, row, row, pl.BlockSpec((3, d, d), lambda i: (0, 0, 0))] + gate,
        out_specs=(row, row, row, row),
        compiler_params=_params(("parallel",)),
        name=name,
    )(*ys, w_branch, proj_f, proj_f, proj_f)
    return merged, br


def merge_bwd(dz, w_out, proj_f, br, *, name):
    t_dim, d = dz.shape
    tr = _tile(t_dim, 512)

    def body(dz_ref, w_ref, g0, g1, g2, b0, b1, b2, d0, d1, d2, dg_ref):
        dm = lax.dot_general(dz_ref[...], w_ref[...], NT, preferred_element_type=F32)
        for n, (g, b, o) in enumerate(((g0, b0, d0), (g1, b1, d1), (g2, b2, d2))):
            sg = _sigmoid(g[...])
            o[...] = (dm * sg).astype(BF16)
            dg_ref[:, n * d:(n + 1) * d] = (dm * b[...].astype(F32) * sg * (1.0 - sg)).astype(BF16)

    row = pl.BlockSpec((tr, d), lambda i: (i, 0))
    gate = [pl.BlockSpec((tr, d), functools.partial(lambda n, i: (i, 2 + n), n)) for n in range(3)]
    return pl.pallas_call(
        body,
        out_shape=(jax.ShapeDtypeStruct((t_dim, d), BF16),) * 3 + (jax.ShapeDtypeStruct((t_dim, 3 * d), BF16),),
        grid=(t_dim // tr,),
        in_specs=[row, pl.BlockSpec((d, d), lambda i: (0, 0))] + gate + [row, row, row],
        out_specs=(row, row, row, pl.BlockSpec((tr, 3 * d), lambda i: (i, 0))),
        compiler_params=_params(("parallel",)),
        name=name,
    )(dz, w_out, proj_f, proj_f, proj_f, *br)


GELU_C = math.sqrt(2.0 / math.pi)
PAD = 8
SCAN_TILES = 8


def _gelu(x):
    return 0.5 * x * (1.0 + jnp.tanh(GELU_C * (x + 0.044715 * x * x * x)))


def _gelu_grad(x):
    t = jnp.tanh(GELU_C * (x + 0.044715 * x * x * x))
    return 0.5 * (1.0 + t) + 0.5 * x * (1.0 - t * t) * GELU_C * (1.0 + 3.0 * 0.044715 * x * x)


def _neg_expm1(x, exp_x):
    series = -x * (1.0 + x * (0.5 + x * (1.0 / 6.0)))
    return jnp.where(x > -0.02, series, 1.0 - exp_x)


def _lru_gates(xv, cw_ref, cb_ref, wr_ref, wi_ref, br_ref, bi_ref, lam_ref, pad_ref, s_len):
    pad_ref[pl.ds(0, PAD), :] = jnp.zeros((PAD, LANES), F32)
    pad_ref[pl.ds(PAD, s_len), :] = xv
    xc = cb_ref[...] + jnp.zeros((s_len, LANES), F32)
    for j in range(CONV_WIDTH):
        xc = xc + pad_ref[pl.ds(PAD - (CONV_WIDTH - 1) + j, s_len), :] * cw_ref[pl.ds(j, 1), :]
    xcb = xc.astype(BF16)
    r = _sigmoid(jnp.dot(xcb, wr_ref[0].astype(BF16), preferred_element_type=F32) + br_ref[...])
    i = _sigmoid(jnp.dot(xcb, wi_ref[0].astype(BF16), preferred_element_type=F32) + bi_ref[...])
    nl = -lam_ref[...]
    sp = jnp.maximum(nl, 0.0) + jnp.log(1.0 + jnp.exp(-jnp.abs(nl)))
    log_a = -LRU_C * r * sp
    a = jnp.exp(log_a)
    mult = jnp.sqrt(_neg_expm1(2.0 * log_a, a * a))
    return xc, r, i, sp, a, mult


def _tile_scan(a, b, row, reverse):
    for s in (1, 2, 4):
        if reverse:
            a_sh = pltpu.roll(a, 8 - s, 0)
            b_sh = pltpu.roll(b, 8 - s, 0)
            m = row + s <= 7
        else:
            a_sh = pltpu.roll(a, s, 0)
            b_sh = pltpu.roll(b, s, 0)
            m = row >= s
        b = jnp.where(m, a * b_sh + b, b)
        a = jnp.where(m, a * a_sh, a)
    return a, b


def lru_fwd(proj_f, conv_w, conv_b, wr_bd, wi_bd, b_rg, b_ig, lam, *, name, carried=None):
    bsz, s_len, _ = proj_f.shape
    d = D_MODEL
    ncb = d // LANES
    n_tiles = s_len // 8

    def body(x_ref, g_ref, cw_ref, cb_ref, wr_ref, wi_ref, br_ref, bi_ref, lam_ref, y_ref, h_ref, pad_ref, a_s, b_s):
        xc, r, i, sp, a, mult = _lru_gates(x_ref[0], cw_ref, cb_ref, wr_ref, wi_ref, br_ref, bi_ref, lam_ref, pad_ref, s_len)
        a_s[...] = a
        b_s[...] = mult * (i * xc)
        row = lax.broadcasted_iota(jnp.int32, (8, LANES), 0)

        def tiles(t, carry):
            starts = [pl.multiple_of((t * SCAN_TILES + u) * 8, 8) for u in range(SCAN_TILES)]
            local = [_tile_scan(a_s[pl.ds(i0, 8), :], b_s[pl.ds(i0, 8), :], row, False) for i0 in starts]
            for i0, (ac, hl) in zip(starts, local):
                h = hl + ac * carry
                h_ref[0, pl.ds(i0, 8), :] = h
                carry = jnp.broadcast_to(h[7:8, :], (8, LANES))
            return carry

        lax.fori_loop(0, n_tiles // SCAN_TILES, tiles, jnp.zeros((8, LANES), F32))
        y_ref[0] = (h_ref[0] * _gelu(g_ref[0])).astype(BF16)

    slab = lambda off: pl.BlockSpec((1, s_len, LANES), functools.partial(lambda o, c, b: (b, 0, o + c), off))
    vec = pl.BlockSpec((1, LANES), lambda c, b: (0, c))
    mat = pl.BlockSpec((1, LANES, LANES), lambda c, b: (c, 0, 0))
    out = pl.BlockSpec((1, s_len, LANES), lambda c, b: (b, 0, c))
    return call_with_exchange(
        body, carried,
        out_shape=(jax.ShapeDtypeStruct((bsz, s_len, d), BF16), jax.ShapeDtypeStruct((bsz, s_len, d), F32)),
        grid=(ncb, bsz),
        in_specs=[slab(0), slab(ncb), pl.BlockSpec((CONV_WIDTH, LANES), lambda c, b: (0, c)), vec, mat, mat, vec, vec, vec],
        out_specs=(out, out),
        scratch_shapes=[pltpu.VMEM((s_len + 2 * PAD, LANES), F32), pltpu.VMEM((s_len, LANES), F32), pltpu.VMEM((s_len, LANES), F32)],
        name=name,
        args=(proj_f, proj_f, conv_w, conv_b.reshape(1, d), wr_bd, wi_bd, b_rg.reshape(1, d), b_ig.reshape(1, d), lam.reshape(1, d)))


def lru_bwd(dy, proj_f, h, conv_w, conv_b, wr_bd, wi_bd, wr_bd_t, wi_bd_t, b_rg, b_ig, lam, *, name, carried=None):
    bsz, s_len, _ = proj_f.shape
    d = D_MODEL
    ncb = d // LANES
    n_tiles = s_len // 8

    def body(dy_ref, x_ref, g_ref, h_ref, cw_ref, cb_ref, wr_ref, wi_ref, wrt_ref, wit_ref, br_ref, bi_ref, lam_ref,
             dx_ref, dg_ref, dcw_ref, dcb_ref, dbr_ref, dbi_ref, dlam_ref, dwr_ref, dwi_ref, pad_ref, a_s, b_s, l_s):
        @pl.when(pl.program_id(1) == 0)
        def _():
            for ref in (dcw_ref, dcb_ref, dbr_ref, dbi_ref, dlam_ref, dwr_ref, dwi_ref):
                ref[...] = jnp.zeros_like(ref)

        xc, r, i, sp, a, mult = _lru_gates(x_ref[0], cw_ref, cb_ref, wr_ref, wi_ref, br_ref, bi_ref, lam_ref, pad_ref, s_len)
        gate = g_ref[0]
        hv = h_ref[0]
        dyv = dy_ref[0].astype(F32)
        dg_ref[0] = (dyv * hv * _gelu_grad(gate)).astype(BF16)
        b_s[...] = dyv * _gelu(gate)
        l_s[pl.ds(0, s_len), :] = a
        l_s[pl.ds(s_len, PAD), :] = jnp.zeros((PAD, LANES), F32)
        a_s[...] = l_s[pl.ds(1, s_len), :]
        row = lax.broadcasted_iota(jnp.int32, (8, LANES), 0)

        def tiles(t, carry):
            starts = [pl.multiple_of((n_tiles - 1 - (t * SCAN_TILES + u)) * 8, 8) for u in range(SCAN_TILES)]
            local = [_tile_scan(a_s[pl.ds(i0, 8), :], b_s[pl.ds(i0, 8), :], row, True) for i0 in starts]
            for i0, (ac, ll) in zip(starts, local):
                lmb = ll + ac * carry
                b_s[pl.ds(i0, 8), :] = lmb
                carry = jnp.broadcast_to(lmb[0:1, :], (8, LANES))
            return carry

        lax.fori_loop(0, n_tiles // SCAN_TILES, tiles, jnp.zeros((8, LANES), F32))
        lmb = b_s[...]
        l_s[pl.ds(0, PAD), :] = jnp.zeros((PAD, LANES), F32)
        l_s[pl.ds(PAD, s_len), :] = hv
        h_prev = l_s[pl.ds(PAD - 1, s_len), :]
        da = lmb * h_prev
        dmult = lmb * (i * xc)
        di = lmb * mult * xc
        dxc = lmb * mult * i
        dlog_a = da * a - dmult * a * a / mult
        dr = -LRU_C * sp * dlog_a
        dsp = jnp.sum(-LRU_C * r * dlog_a, axis=0, keepdims=True)
        dlam_ref[...] += dsp * (-_sigmoid(-lam_ref[...]))
        dpr = dr * r * (1.0 - r)
        dpi = di * i * (1.0 - i)
        dprb = dpr.astype(BF16)
        dpib = dpi.astype(BF16)
        xcb = xc.astype(BF16)
        dbr_ref[...] += jnp.sum(dpr, axis=0, keepdims=True)
        dbi_ref[...] += jnp.sum(dpi, axis=0, keepdims=True)
        tn = (((0,), (0,)), ((), ()))
        dwr_ref[0] += lax.dot_general(xcb, dprb, tn, preferred_element_type=F32)
        dwi_ref[0] += lax.dot_general(xcb, dpib, tn, preferred_element_type=F32)
        dxc = (dxc + jnp.dot(dprb, wrt_ref[0].astype(BF16), preferred_element_type=F32)
               + jnp.dot(dpib, wit_ref[0].astype(BF16), preferred_element_type=F32))
        dcb_ref[...] += jnp.sum(dxc, axis=0, keepdims=True)
        for j in range(CONV_WIDTH):
            dcw_ref[pl.ds(j, 1), :] += jnp.sum(dxc * pad_ref[pl.ds(PAD - (CONV_WIDTH - 1) + j, s_len), :], axis=0, keepdims=True)
        l_s[pl.ds(0, s_len), :] = dxc
        l_s[pl.ds(s_len, PAD), :] = jnp.zeros((PAD, LANES), F32)
        dx = jnp.zeros((s_len, LANES), F32)
        for j in range(CONV_WIDTH):
            dx = dx + l_s[pl.ds(CONV_WIDTH - 1 - j, s_len), :] * cw_ref[pl.ds(j, 1), :]
        dx_ref[0] = dx.astype(BF16)

    slab = lambda off: pl.BlockSpec((1, s_len, LANES), functools.partial(lambda o, c, b: (b, 0, o + c), off))
    vec = pl.BlockSpec((1, LANES), lambda c, b: (0, c))
    mat = pl.BlockSpec((1, LANES, LANES), lambda c, b: (c, 0, 0))
    cw = pl.BlockSpec((CONV_WIDTH, LANES), lambda c, b: (0, c))
    out = pl.BlockSpec((1, s_len, LANES), lambda c, b: (b, 0, c))
    vshape = jax.ShapeDtypeStruct((1, d), F32)
    mshape = jax.ShapeDtypeStruct((ncb, LANES, LANES), F32)
    return call_with_exchange(
        body, carried,
        out_shape=(jax.ShapeDtypeStruct((bsz, s_len, d), BF16),) * 2
        + (jax.ShapeDtypeStruct((CONV_WIDTH, d), F32), vshape, vshape, vshape, vshape, mshape, mshape),
        grid=(ncb, bsz),
        in_specs=[out, slab(0), slab(ncb), out, cw, vec, mat, mat, mat, mat, vec, vec, vec],
        out_specs=(out, out, cw, vec, vec, vec, vec, mat, mat),
        scratch_shapes=[pltpu.VMEM((s_len + 2 * PAD, LANES), F32), pltpu.VMEM((s_len, LANES), F32), pltpu.VMEM((s_len, LANES), F32),
                        pltpu.VMEM((s_len + 2 * PAD, LANES), F32)],
        name=name,
        args=(dy, proj_f, proj_f, h, conv_w, conv_b.reshape(1, d), wr_bd, wi_bd, wr_bd_t, wi_bd_t,
              b_rg.reshape(1, d), b_ig.reshape(1, d), lam.reshape(1, d)))


NT = (((1,), (1,)), ((), ()))
TN = (((0,), (0,)), ((), ()))
ATT_SCALE = HEAD_DIM ** -0.5


def _kv_place(head, n_kv_heads):
    kv = head // (N_HEADS // n_kv_heads)
    return kv // 2, kv % 2


def _band_mask(n, single):
    nk = ATT_BLOCK if single else 2 * ATT_BLOCK
    qi = lax.broadcasted_iota(jnp.int32, (2 * ATT_BLOCK, nk), 0) % ATT_BLOCK
    kj = lax.broadcasted_iota(jnp.int32, (2 * ATT_BLOCK, nk), 1)
    if single:
        return qi >= kj
    rel = qi + ATT_BLOCK - kj
    return (rel >= 0) & (rel <= ATT_BLOCK) & ((n > 0) | (kj >= ATT_BLOCK))


def _lane_halves():
    lane = lax.broadcasted_iota(jnp.int32, (1, LANES), 1)
    return lane < HEAD_DIM


def _stack_heads(t2, kh):
    first = _lane_halves()
    parts = []
    for a in range(2):
        ta = jnp.where(first if a == 0 else ~first, t2, jnp.zeros_like(t2))
        if a != kh[a]:
            ta = pltpu.roll(ta, HEAD_DIM, 1)
        parts.append(ta)
    return jnp.concatenate(parts, axis=0)


def _fold_heads(t, kh):
    t0, t1 = t[:ATT_BLOCK], t[ATT_BLOCK:]
    if t.shape[1] == LANES:
        if kh[0] != 0:
            t0 = pltpu.roll(t0, HEAD_DIM, 1)
        if kh[1] != 1:
            t1 = pltpu.roll(t1, HEAD_DIM, 1)
    return jnp.where(_lane_halves(), t0, t1)


def _rows_of_heads(t2):
    return jnp.concatenate([t2[:, 0:1], t2[:, HEAD_DIM:HEAD_DIM + 1]], axis=0)


PAIRS_AT_ONCE = 4


def _fill_bias(bias2_ref, bias1_ref=None):
    for i in range(2):
        bias2_ref[i] = jnp.where(_band_mask(i, False), 0.0, NEG_INF)
    if bias1_ref is not None:
        bias1_ref[...] = jnp.where(_band_mask(0, True), 0.0, NEG_INF)


def _pairs_fwd(items):
    ss = [lax.dot_general(_stack_heads(q2 * ATT_SCALE, kh), kk, NT, preferred_element_type=F32) + bias
          for q2, kk, _, bias, kh, _ in items]
    ps, ms, ls = [], [], []
    for s, (_, _, _, _, _, sink_col) in zip(ss, items):
        m = jnp.max(s, axis=-1, keepdims=True)
        if sink_col is not None:
            m = jnp.maximum(m, sink_col)
        p = jnp.exp(s - m)
        l = jnp.sum(p, axis=-1, keepdims=True)
        if sink_col is not None:
            l = l + jnp.exp(sink_col - m)
        ps.append(p.astype(BF16))
        ms.append(m)
        ls.append(l)
    pvs = [jnp.dot(p, it[2], preferred_element_type=F32) for p, it in zip(ps, items)]
    return list(zip(pvs, ms, ls))


def _pairs_bwd(items):
    first = _lane_halves()
    pre = []
    for q2, kk, vv, do2, o2, lse2, bias, kh in items:
        dd = do2 * o2
        dsum = jnp.concatenate([jnp.sum(jnp.where(first, dd, 0.0), axis=-1, keepdims=True),
                                jnp.sum(jnp.where(first, 0.0, dd), axis=-1, keepdims=True)], axis=0)
        qs = _stack_heads(q2 * ATT_SCALE, kh)
        dos = _stack_heads(do2.astype(BF16), kh)
        s = lax.dot_general(qs, kk, NT, preferred_element_type=F32) + bias
        dp = lax.dot_general(dos, vv, NT, preferred_element_type=F32)
        pre.append((qs, dos, s, dp, dsum))
    mid = []
    for (qs, dos, s, dp, dsum), it in zip(pre, items):
        p = jnp.exp(s - _rows_of_heads(it[5]))
        mid.append((p.astype(BF16), (p * (dp - dsum)).astype(BF16)))
    out = []
    for (pb, ds), (qs, dos, _, _, dsum), it in zip(mid, pre, items):
        dq = _fold_heads(jnp.dot(ds, it[1], preferred_element_type=F32), it[7]) * ATT_SCALE
        dk = lax.dot_general(ds, qs, TN, preferred_element_type=F32)
        dv = lax.dot_general(pb, dos, TN, preferred_element_type=F32)
        out.append((dq, dk, dv, dsum))
    return out


def swa_fwd(qkv, sinks, *, name, carried=None):
    bsz, s_len, width = qkv.shape
    ckv = SWA_KV_HEADS * HEAD_DIM
    nb = s_len // ATT_BLOCK
    kblk = D_MODEL // ckv

    def body(sink_ref, q_ref, kp_ref, kc_ref, vp_ref, vc_ref, o_ref, lse_ref, ob_ref, bias2):
        n = pl.program_id(1)
        _fill_bias(bias2)
        bias = bias2[jnp.minimum(n, 1)]
        kk = jnp.concatenate([kp_ref[0], kc_ref[0]], axis=0)
        vv = jnp.concatenate([vp_ref[0], vc_ref[0]], axis=0)
        top = lax.broadcasted_iota(jnp.int32, (2 * ATT_BLOCK, 1), 0) < ATT_BLOCK
        for hp0 in range(0, N_HEADS // 2, PAIRS_AT_ONCE):
            items, places = [], []
            for hp in range(hp0, hp0 + PAIRS_AT_ONCE):
                cols = slice(hp * LANES, (hp + 1) * LANES)
                kb, kh = _kv_place(2 * hp, SWA_KV_HEADS)
                kcols = slice(kb * LANES, (kb + 1) * LANES)
                sink_col = jnp.where(top, sink_ref[2 * hp], sink_ref[2 * hp + 1])
                items.append((q_ref[0, :, cols], kk[:, kcols], vv[:, kcols], bias, (kh, kh), sink_col))
                places.append((cols, (kh, kh)))
            for (pv, m, l), (cols, kh2) in zip(_pairs_fwd(items), places):
                o2 = _fold_heads(pv / l, kh2)
                o_ref[0, :, cols] = o2
                ob_ref[0, :, cols] = o2.astype(BF16)
                lse_ref[0, :, cols] = _fold_heads(m + jnp.log(l), kh2)

    prev = lambda n: jnp.maximum(n - 1, 0)
    out = pl.BlockSpec((1, ATT_BLOCK, D_MODEL), lambda b, n: (b, n, 0))
    sd = lambda dt: jax.ShapeDtypeStruct((bsz, s_len, D_MODEL), dt)
    return call_with_exchange(
        body, carried,
        out_shape=(sd(F32), sd(F32), sd(BF16)),
        grid=(bsz, nb),
        in_specs=[pl.BlockSpec(memory_space=pltpu.SMEM), out,
                  pl.BlockSpec((1, ATT_BLOCK, ckv), lambda b, n: (b, prev(n), kblk)),
                  pl.BlockSpec((1, ATT_BLOCK, ckv), lambda b, n: (b, n, kblk)),
                  pl.BlockSpec((1, ATT_BLOCK, ckv), lambda b, n: (b, prev(n), kblk + 1)),
                  pl.BlockSpec((1, ATT_BLOCK, ckv), lambda b, n: (b, n, kblk + 1))],
        out_specs=(out, out, out),
        scratch_shapes=[pltpu.VMEM((2, 2 * ATT_BLOCK, 2 * ATT_BLOCK), F32)],
        name=name,
        args=(sinks, qkv, qkv, qkv, qkv, qkv))


def swa_bwd(qkv, sinks, o, lse, do, *, name, carried=None):
    bsz, s_len, width = qkv.shape
    ckv = SWA_KV_HEADS * HEAD_DIM
    nb = s_len // ATT_BLOCK
    kblk = D_MODEL // ckv

    def body(sink_ref, q_ref, kp_ref, kc_ref, vp_ref, vc_ref, o_ref, lse_ref, do_ref, dq_ref, dk_ref, dv_ref, dsink_ref,
             dkk, dvv, ck, cv, bias2):
        n = pl.program_id(1)

        @pl.when((n == 0) & (pl.program_id(0) == 0))
        def _():
            dsink_ref[...] = jnp.zeros_like(dsink_ref)

        @pl.when(n < nb)
        def _():
            top = lax.broadcasted_iota(jnp.int32, (2 * ATT_BLOCK, 1), 0) < ATT_BLOCK
            lane = lax.broadcasted_iota(jnp.int32, dsink_ref.shape, 1)
            first_row = lax.broadcasted_iota(jnp.int32, dsink_ref.shape, 0) == 0
            _fill_bias(bias2)
            bias = bias2[jnp.minimum(n, 1)]
            kk = jnp.concatenate([kp_ref[0], kc_ref[0]], axis=0)
            vv = jnp.concatenate([vp_ref[0], vc_ref[0]], axis=0)
            dkk[...] = jnp.zeros_like(dkk)
            dvv[...] = jnp.zeros_like(dvv)
            for hp0 in range(0, N_HEADS // 2, PAIRS_AT_ONCE):
                items, places = [], []
                for hp in range(hp0, hp0 + PAIRS_AT_ONCE):
                    cols = slice(hp * LANES, (hp + 1) * LANES)
                    kb, kh = _kv_place(2 * hp, SWA_KV_HEADS)
                    kcols = slice(kb * LANES, (kb + 1) * LANES)
                    items.append((q_ref[0, :, cols], kk[:, kcols], vv[:, kcols], do_ref[0, :, cols], o_ref[0, :, cols],
                                  lse_ref[0, :, cols], bias, (kh, kh)))
                    places.append((cols, kcols, hp))
                for (dq, dk, dv, dsum), (cols, kcols, hp) in zip(_pairs_bwd(items), places):
                    dq_ref[0, :, cols] = dq
                    dkk[:, kcols] += dk
                    dvv[:, kcols] += dv
                    sink_col = jnp.where(top, sink_ref[2 * hp], sink_ref[2 * hp + 1])
                    t = -jnp.exp(sink_col - _rows_of_heads(lse_ref[0, :, cols])) * dsum
                    d0 = jnp.sum(t[:ATT_BLOCK], axis=0, keepdims=True)
                    d1 = jnp.sum(t[ATT_BLOCK:], axis=0, keepdims=True)
                    dsink_ref[...] += jnp.where(first_row & (lane == 2 * hp), d0, 0.0) + jnp.where(first_row & (lane == 2 * hp + 1), d1, 0.0)

        @pl.when((n >= 1) & (n < nb))
        def _():
            dk_ref[0] = ck[...] + dkk[pl.ds(0, ATT_BLOCK), :]
            dv_ref[0] = cv[...] + dvv[pl.ds(0, ATT_BLOCK), :]

        @pl.when(n == nb)
        def _():
            dk_ref[0] = ck[...]
            dv_ref[0] = cv[...]

        @pl.when(n < nb)
        def _():
            ck[...] = dkk[pl.ds(ATT_BLOCK, ATT_BLOCK), :]
            cv[...] = dvv[pl.ds(ATT_BLOCK, ATT_BLOCK), :]

    clamp = lambda n: jnp.minimum(n, nb - 1)
    prev = lambda n: jnp.maximum(n - 1, 0)
    row = pl.BlockSpec((1, ATT_BLOCK, D_MODEL), lambda b, n: (b, clamp(n), 0))
    kv_out = pl.BlockSpec((1, ATT_BLOCK, ckv), lambda b, n: (b, prev(n), 0))
    return call_with_exchange(
        body, carried,
        out_shape=(jax.ShapeDtypeStruct((bsz, s_len, D_MODEL), F32), jax.ShapeDtypeStruct((bsz, s_len, ckv), F32),
                   jax.ShapeDtypeStruct((bsz, s_len, ckv), F32), jax.ShapeDtypeStruct((8, LANES), F32)),
        grid=(bsz, nb + 1),
        in_specs=[pl.BlockSpec(memory_space=pltpu.SMEM), row,
                  pl.BlockSpec((1, ATT_BLOCK, ckv), lambda b, n: (b, prev(clamp(n)), kblk)),
                  pl.BlockSpec((1, ATT_BLOCK, ckv), lambda b, n: (b, clamp(n), kblk)),
                  pl.BlockSpec((1, ATT_BLOCK, ckv), lambda b, n: (b, prev(clamp(n)), kblk + 1)),
                  pl.BlockSpec((1, ATT_BLOCK, ckv), lambda b, n: (b, clamp(n), kblk + 1)),
                  row, row, row],
        out_specs=(row, kv_out, kv_out, pl.BlockSpec((8, LANES), lambda b, n: (0, 0))),
        scratch_shapes=[pltpu.VMEM((2 * ATT_BLOCK, ckv), F32), pltpu.VMEM((2 * ATT_BLOCK, ckv), F32),
                        pltpu.VMEM((ATT_BLOCK, ckv), F32), pltpu.VMEM((ATT_BLOCK, ckv), F32),
                        pltpu.VMEM((2, 2 * ATT_BLOCK, 2 * ATT_BLOCK), F32)],
        name=name,
        args=(sinks, qkv, qkv, qkv, qkv, qkv, o, lse, do))


DIL_PATTERNS = tuple((d, 2048 // d // ATT_BLOCK) for d in reversed(DILATIONS))
MHA = (0, 1)


def _dil_rows(idx, d, nb):
    j = idx // nb
    n = idx % nb
    base = j + n * (ATT_BLOCK * d)
    prev = jnp.maximum(base - ATT_BLOCK * d, j)
    if d == 1:
        return n, pl.ds(pl.multiple_of(base, ATT_BLOCK), ATT_BLOCK), pl.ds(pl.multiple_of(prev, ATT_BLOCK), ATT_BLOCK)
    return n, pl.ds(base, ATT_BLOCK, stride=d), pl.ds(prev, ATT_BLOCK, stride=d)


def dil_fwd(qkv, *, name, carried=None):
    bsz, s_len, _ = qkv.shape
    assert s_len == DIL_PATTERNS[0][0] * DIL_PATTERNS[0][1] * ATT_BLOCK
    npair = N_HEADS // 2

    def body(q_ref, k_ref, v_ref, y_ref, lse_ref, yb_ref, m_acc, l_acc, bias2, bias1):
        _fill_bias(bias2, bias1)
        for ci, (d, nb) in enumerate(DIL_PATTERNS):
            single = nb == 1

            def blocks(it, carry):
                items, places = [], []
                for u in range(PAIRS_AT_ONCE):
                    n, rows, prows = _dil_rows(it * PAIRS_AT_ONCE + u, d, nb)
                    kc = k_ref[rows, :].astype(BF16)
                    vc = v_ref[rows, :].astype(BF16)
                    if single:
                        kk, vv, bias = kc, vc, bias1[...]
                    else:
                        kk = jnp.concatenate([k_ref[prows, :].astype(BF16), kc], axis=0)
                        vv = jnp.concatenate([v_ref[prows, :].astype(BF16), vc], axis=0)
                        bias = bias2[jnp.minimum(n, 1)]
                    items.append((q_ref[rows, :].astype(BF16), kk, vv, bias, MHA, None))
                    places.append(rows)
                for (pv, m, l), rows in zip(_pairs_fwd(items), places):
                    o2, m2, l2 = _fold_heads(pv, MHA), _fold_heads(m, MHA), _fold_heads(l, MHA)
                    if ci == 0:
                        y_ref[rows, :] = o2
                        m_acc[rows, :] = m2
                        l_acc[rows, :] = l2
                    else:
                        m_old = m_acc[rows, :]
                        m_new = jnp.maximum(m_old, m2)
                        w_old = jnp.exp(m_old - m_new)
                        w_new = jnp.exp(m2 - m_new)
                        y_ref[rows, :] = y_ref[rows, :] * w_old + o2 * w_new
                        l_acc[rows, :] = l_acc[rows, :] * w_old + l2 * w_new
                        m_acc[rows, :] = m_new
                return carry

            lax.fori_loop(0, d * nb // PAIRS_AT_ONCE, blocks, 0)
        y = y_ref[...] / l_acc[...]
        y_ref[...] = y
        yb_ref[...] = y.astype(BF16)
        lse_ref[...] = m_acc[...] + jnp.log(l_acc[...])

    slab = lambda off: pl.BlockSpec((None, s_len, LANES), functools.partial(lambda o, b, h: (b, 0, o + h), off))
    sd = lambda dt: jax.ShapeDtypeStruct((bsz, s_len, D_MODEL), dt)
    return call_with_exchange(
        body, carried,
        out_shape=(sd(F32), sd(F32), sd(BF16)),
        grid=(bsz, npair),
        in_specs=[slab(0), slab(npair), slab(2 * npair)],
        out_specs=(slab(0), slab(0), slab(0)),
        scratch_shapes=[pltpu.VMEM((s_len, LANES), F32), pltpu.VMEM((s_len, LANES), F32),
                        pltpu.VMEM((2, 2 * ATT_BLOCK, 2 * ATT_BLOCK), F32), pltpu.VMEM((2 * ATT_BLOCK, ATT_BLOCK), F32)],
        name=name,
        args=(qkv, qkv, qkv))


def dil_bwd(qkv, y, lse, dy, *, name, carried=None):
    bsz, s_len, _ = qkv.shape
    npair = N_HEADS // 2

    def body(q_ref, k_ref, v_ref, y_ref, lse_ref, dy_ref, dq_ref, dk_ref, dv_ref, bias2, bias1):
        _fill_bias(bias2, bias1)
        assert DIL_PATTERNS[0][1] == 1
        for d, nb in DIL_PATTERNS:
            single = nb == 1

            def blocks(it, carry):
                items, places = [], []
                for u in range(PAIRS_AT_ONCE):
                    n, rows, prows = _dil_rows(it * PAIRS_AT_ONCE + u, d, nb)
                    kc = k_ref[rows, :].astype(BF16)
                    vc = v_ref[rows, :].astype(BF16)
                    if single:
                        kk, vv, bias = kc, vc, bias1[...]
                    else:
                        kk = jnp.concatenate([k_ref[prows, :].astype(BF16), kc], axis=0)
                        vv = jnp.concatenate([v_ref[prows, :].astype(BF16), vc], axis=0)
                        bias = bias2[jnp.minimum(n, 1)]
                    items.append((q_ref[rows, :].astype(BF16), kk, vv, dy_ref[rows, :], y_ref[rows, :], lse_ref[rows, :], bias, MHA))
                    places.append((rows, prows))
                for (dq, dk, dv, _), (rows, prows) in zip(_pairs_bwd(items), places):
                    if single:
                        dq_ref[rows, :] = dq
                        dk_ref[rows, :] = dk
                        dv_ref[rows, :] = dv
                    else:
                        dq_ref[rows, :] += dq
                        dk_ref[prows, :] += dk[:ATT_BLOCK]
                        dv_ref[prows, :] += dv[:ATT_BLOCK]
                        dk_ref[rows, :] += dk[ATT_BLOCK:]
                        dv_ref[rows, :] += dv[ATT_BLOCK:]
                return carry

            lax.fori_loop(0, d * nb // PAIRS_AT_ONCE, blocks, 0)

    slab = lambda off: pl.BlockSpec((None, s_len, LANES), functools.partial(lambda o, b, h: (b, 0, o + h), off))
    sd = jax.ShapeDtypeStruct((bsz, s_len, D_MODEL), F32)
    return call_with_exchange(
        body, carried,
        out_shape=(sd, sd, sd),
        grid=(bsz, npair),
        in_specs=[slab(0), slab(npair), slab(2 * npair), slab(0), slab(0), slab(0)],
        out_specs=(slab(0), slab(0), slab(0)),
        scratch_shapes=[pltpu.VMEM((2, 2 * ATT_BLOCK, 2 * ATT_BLOCK), F32), pltpu.VMEM((2 * ATT_BLOCK, ATT_BLOCK), F32)],
        name=name,
        args=(qkv, qkv, qkv, y, lse, dy))


def adamw(w, g, m, v, *, name):
    rows, cols = w.shape
    tr = _pick(rows, (256, 128, 64, 32, 16, 8))

    def body(w_ref, g_ref, m_ref, v_ref, d_ref, nm_ref, nv_ref):
        gv = g_ref[...]
        nm = ADAM_B1 * m_ref[...] + (1.0 - ADAM_B1) * gv
        nv = ADAM_B2 * v_ref[...] + (1.0 - ADAM_B2) * (gv * gv)
        m_hat = nm / (1.0 - ADAM_B1 ** ADAM_STEP)
        v_hat = nv / (1.0 - ADAM_B2 ** ADAM_STEP)
        d_ref[...] = -ADAM_LR * (m_hat / (jnp.sqrt(v_hat) + ADAM_EPS) + ADAM_WD * w_ref[...])
        nm_ref[...] = nm
        nv_ref[...] = nv

    row = pl.BlockSpec((tr, cols), lambda i: (i, 0))
    return pl.pallas_call(
        body,
        out_shape=(jax.ShapeDtypeStruct((rows, cols), F32),) * 3,
        grid=(rows // tr,),
        in_specs=[row] * 4,
        out_specs=(row, row, row),
        compiler_params=_params(("parallel",)),
        name=name,
    )(w, g, m, v)


def _place():
    return lax.axis_index("x"), lax.axis_index("y"), lax.axis_index("c")


def _gather_copies(x_ref, out_ref, send_sems, recv_sems):
    x, y, c = _place()
    me, sibling = (x, y, c), (x, y, 1 - c)
    chips = [(1 - x, y), (x, 1 - y), (1 - x, 1 - y)]

    def slot(px, py, pc):
        return out_ref.at[4 * px + 2 * py + pc]

    def copy(k, block, to, src=None):
        return pltpu.make_async_remote_copy(
            src_ref=slot(*block) if src is None else src, dst_ref=slot(*block),
            send_sem=send_sems.at[k], recv_sem=recv_sems.at[k], device_id=to, device_id_type=MESH)

    first = [lambda: copy(0, me, sibling, src=x_ref)] + [functools.partial(copy, 1 + j, me, (*chip, c), src=x_ref)
                                                         for j, chip in enumerate(chips)]
    passed = [functools.partial(copy, 4 + j, (*chip, c), sibling) for j, chip in enumerate(chips)]
    landing = [functools.partial(copy, 1 + j, (*chip, c), me) for j, chip in enumerate(chips)]
    from_sibling = [lambda: copy(0, sibling, me)] + [functools.partial(copy, 4 + j, (*chip, 1 - c), me) for j, chip in enumerate(chips)]
    return slot(*me), first, passed, landing, from_sibling


def _gather_start(x_ref, out_ref, send_sems, recv_sems, local_sem):
    mine, first, _, _, _ = _gather_copies(x_ref, out_ref, send_sems, recv_sems)
    pltpu.make_async_copy(x_ref, mine, local_sem).start()
    for cp in first:
        cp().start()


def _gather_finish(x_ref, out_ref, send_sems, recv_sems, local_sem):
    mine, first, passed, landing, from_sibling = _gather_copies(x_ref, out_ref, send_sems, recv_sems)
    for cp, fwd in zip(landing, passed):
        cp().wait_recv()
        fwd().start()
    for cp in from_sibling:
        cp().wait_recv()
    for cp in first + passed:
        cp().wait_send()
    pltpu.make_async_copy(x_ref, mine, local_sem).wait()


def _a2a_copies(x_ref, out_ref, send_sems, recv_sems):
    x, y, c = _place()
    me = 4 * x + 2 * y + c
    copies = []
    for k in range(1, N_DEV):
        px = 1 - x if k & 4 else x
        py = 1 - y if k & 2 else y
        pc = 1 - c if k & 1 else c
        copies.append(pltpu.make_async_remote_copy(
            src_ref=x_ref.at[4 * px + 2 * py + pc], dst_ref=out_ref.at[me], send_sem=send_sems.at[k - 1],
            recv_sem=recv_sems.at[k - 1], device_id=(px, py, pc), device_id_type=MESH))
    return me, copies


def _a2a_start(x_ref, out_ref, send_sems, recv_sems, local_sem):
    me, copies = _a2a_copies(x_ref, out_ref, send_sems, recv_sems)
    pltpu.make_async_copy(x_ref.at[me], out_ref.at[me], local_sem).start()
    for cp in copies:
        cp.start()


def _a2a_finish(x_ref, out_ref, send_sems, recv_sems, local_sem):
    me, copies = _a2a_copies(x_ref, out_ref, send_sems, recv_sems)
    for cp in copies:
        cp.wait_recv()
    for cp in copies:
        cp.wait_send()
    pltpu.make_async_copy(x_ref.at[me], out_ref.at[me], local_sem).wait()


EXCHANGES = {"gather": (_gather_start, _gather_finish, lambda x: (N_DEV,) + x.shape),
             "a2a": (_a2a_start, _a2a_finish, lambda x: x.shape)}
EXCHANGE_SEMS = [pltpu.SemaphoreType.DMA((7,)), pltpu.SemaphoreType.DMA((7,)), pltpu.SemaphoreType.DMA(())]


def exchange(kind, x, *, name):
    start, finish, shape = EXCHANGES[kind]

    def body(x_ref, out_ref, *sems):
        start(x_ref, out_ref, *sems)
        finish(x_ref, out_ref, *sems)

    return pl.pallas_call(
        body,
        out_shape=jax.ShapeDtypeStruct(shape(x), x.dtype),
        in_specs=[pl.BlockSpec(memory_space=pl.ANY)],
        out_specs=pl.BlockSpec(memory_space=pl.ANY),
        scratch_shapes=EXCHANGE_SEMS,
        name=name,
    )(x)


def call_with_exchange(body, carried, *, out_shape, grid, in_specs, out_specs, scratch_shapes, name, args):
    sem = ("arbitrary",) * len(grid)
    carried = list(carried or ())
    if not carried:
        res = pl.pallas_call(body, out_shape=out_shape, grid=grid, in_specs=in_specs, out_specs=out_specs,
                             scratch_shapes=scratch_shapes, compiler_params=_params(sem), name=name)(*args)
        return res, []
    n_in, n_out, n_scr, n_x = len(in_specs), len(out_shape), len(scratch_shapes), len(carried)
    n_sems = len(EXCHANGE_SEMS)

    def wrapped(*refs):
        ins, x_refs = refs[:n_in], refs[n_in:n_in + n_x]
        outs = refs[n_in + n_x:n_in + n_x + n_out]
        out_refs = refs[n_in + n_x + n_out:n_in + 2 * n_x + n_out]
        rest = refs[n_in + 2 * n_x + n_out:]
        scratch, sems = rest[:n_scr], rest[n_scr:]
        ids = [pl.program_id(i) for i in range(len(grid))]
        is_first = functools.reduce(lambda a, b: a & b, [i == 0 for i in ids])
        is_last = functools.reduce(lambda a, b: a & b, [i == g - 1 for i, g in zip(ids, grid)])

        @pl.when(is_first)
        def _():
            for e, (kind, _) in enumerate(carried):
                EXCHANGES[kind][0](x_refs[e], out_refs[e], *sems[e * n_sems:(e + 1) * n_sems])

        body(*ins, *outs, *scratch)

        @pl.when(is_last)
        def _():
            for e, (kind, _) in enumerate(carried):
                EXCHANGES[kind][1](x_refs[e], out_refs[e], *sems[e * n_sems:(e + 1) * n_sems])

    any_spec = pl.BlockSpec(memory_space=pl.ANY)
    res = pl.pallas_call(
        wrapped,
        out_shape=tuple(out_shape) + tuple(jax.ShapeDtypeStruct(EXCHANGES[kind][2](x), x.dtype) for kind, x in carried),
        grid=grid,
        in_specs=list(in_specs) + [any_spec] * n_x,
        out_specs=tuple(out_specs) + (any_spec,) * n_x,
        scratch_shapes=list(scratch_shapes) + EXCHANGE_SEMS * n_x,
        compiler_params=_params(sem),
        name=name + "".join("_" + kind for kind, _ in carried),
    )(*args, *[x for _, x in carried])
    return res[:n_out], list(res[n_out:])


def sum_slots(x, *, name):
    _, rows, cols = x.shape
    tr = _pick(rows, (512, 256, 128, 64, 32, 16))

    def body(x_ref, o_ref):
        acc = x_ref[0].astype(F32)
        for k in range(1, N_DEV):
            acc = acc + x_ref[k].astype(F32)
        o_ref[...] = acc

    return pl.pallas_call(
        body,
        out_shape=jax.ShapeDtypeStruct((rows, cols), F32),
        grid=(rows // tr,),
        in_specs=[pl.BlockSpec((N_DEV, tr, cols), lambda i: (0, i, 0))],
        out_specs=pl.BlockSpec((tr, cols), lambda i: (i, 0)),
        compiler_params=_params(("parallel",)),
        name=name,
    )(x)


BIG = ("w_in", "w_branch", "w_out", "w_ffn_in", "w_ffn_out")
SMALL = ("conv_b", "w_rg", "b_rg", "w_ig", "b_ig", "lru_lambda", "sinks", "ln1_g", "ln1_b", "ln2_g", "ln2_b")
N_LRU_BLOCKS = D_MODEL // HEAD_DIM
SMALL_ROWS_TILE = 512


def _block_diag(w):
    z = jnp.zeros((N_LRU_BLOCKS // 2, HEAD_DIM, HEAD_DIM), w.dtype)
    top = jnp.concatenate([w[0::2], z], axis=2)
    bot = jnp.concatenate([z, w[1::2]], axis=2)
    return jnp.concatenate([top, bot], axis=1)


def _block_diag_grad(g):
    return jnp.stack([g[:, :HEAD_DIM, :HEAD_DIM], g[:, HEAD_DIM:, HEAD_DIM:]], axis=1).reshape(N_LRU_BLOCKS, HEAD_DIM, HEAD_DIM)


def layer_fwd(x, xb, p, bsz, own_late=None, next_w_in=None):
    t_dim = x.shape[0]
    s_len = t_dim // bsz
    w_f, w_qs, w_qd = p["w_in_f"], p["w_in_qs"], p["w_in_qd"]
    proj_f = matmul(xb, w_f, name="proj_f")
    qs = matmul(xb, w_qs, out_dtype=BF16, name="proj_qs").reshape(bsz, s_len, W_QS)
    qd = matmul(xb, w_qd, name="proj_qd").reshape(bsz, s_len, W_QD)
    proj_f3 = proj_f.reshape(bsz, s_len, W_F)
    wr_bd, wi_bd = _block_diag(p["w_rg"]), _block_diag(p["w_ig"])
    (y_a, h), got_rows = lru_fwd(proj_f3, p["conv_w"], p["conv_b"], wr_bd, wi_bd, p["b_rg"], p["b_ig"], p["lru_lambda"],
                                 name="lru_fwd", carried=[("gather", own_late[1])] if own_late is not None else [])
    (y_b, lse_b, y_bb), got_fi = swa_fwd(qs, p["sinks"], name="swa_fwd", carried=[("gather", own_late[0])] if own_late is not None else [])
    (y_c, lse_c, y_cb), got_next = dil_fwd(qd, name="dil_fwd", carried=[("gather", next_w_in)] if next_w_in is not None else [])
    if own_late is not None:
        p = {**p, **_late_weights(got_fi[0], got_rows[0])}
    ys = [t.reshape(t_dim, D_MODEL) for t in (y_a, y_bb, y_cb)]
    merged, br = branch_merge(ys, p["w_branch"], proj_f, name="branch_merge")
    x1, x1b, z1 = ln_fwd(x, merged, p["w_out"], p["ln1_g"], p["ln1_b"], name="w_out_ln")
    h1, h3, act = ffn_in_swiglu(x1b, p["w_ffn_in"], name="ffn_in_swiglu")
    x2, x2b, z2 = ln_fwd(x1, act, p["w_ffn_out"], p["ln2_g"], p["ln2_b"], name="ffn_out_ln")
    saved = dict(xb=xb, proj_f=proj_f, qs=qs, qd=qd, h=h, ys=ys, y_b=y_b, y_c=y_c, lse_b=lse_b, lse_c=lse_c, br=br, merged=merged,
                 z1=z1, x1b=x1b, h1=h1, h3=h3, act=act, z2=z2, wr_bd=wr_bd, wi_bd=wi_bd, p=p)
    return x2, x2b, saved, (got_next[0] if got_next else None)


def layer_bwd(dx2, s, bsz, exchange_own=False, above_w_in=None):
    p = s["p"]
    t_dim = dx2.shape[0]
    s_len = t_dim // bsz
    g = {}
    dz2, dz2b, g["ln2_g"], g["ln2_b"] = ln_bwd(dx2, s["z2"], p["ln2_g"], name="ln_bwd")
    dh13 = swiglu_bwd(dz2b, p["w_ffn_out"], s["h1"], s["h3"], name="swiglu_bwd")
    g["w_ffn_out"] = matmul(s["act"], dz2b, trans_a=True, out_dtype=BF16, name="dw_ffn_out")
    g["w_ffn_in"] = matmul(s["x1b"], dh13, trans_a=True, out_dtype=BF16, name="dw_ffn_in")
    dx1 = matmul(dh13, p["w_ffn_in"], trans_b=True, add=dz2, add_scale=ALPHA, name="dx_ffn")
    dz1, dz1b, g["ln1_g"], g["ln1_b"] = ln_bwd(dx1, s["z1"], p["ln1_g"], name="ln_bwd")
    g["w_out"] = matmul(s["merged"], dz1b, trans_a=True, out_dtype=BF16, name="dw_out")
    *dbr, dgates = merge_bwd(dz1b, p["w_out"], s["proj_f"], s["br"], name="merge_bwd")
    dys = [matmul(dbr[n], p["w_branch"][n], trans_b=True, out_dtype=F32 if n == 2 else BF16, name="d_branch") for n in range(3)]
    g["w_branch"] = jnp.stack([matmul(s["ys"][n], dbr[n], trans_a=True, out_dtype=BF16, name="dw_branch") for n in range(3)])
    fi_slots, rows_slots = _late_slots(g) if exchange_own else (None, None)
    shape3 = (bsz, s_len, D_MODEL)
    (dlx, dlg, g["conv_w"], g["conv_b"], g["b_rg"], g["b_ig"], g["lru_lambda"], dwr, dwi), got_rows = lru_bwd(
        dys[0].reshape(shape3), s["proj_f"].reshape(bsz, s_len, W_F), s["h"], p["conv_w"], p["conv_b"], s["wr_bd"], s["wi_bd"],
        jnp.swapaxes(s["wr_bd"], 1, 2), jnp.swapaxes(s["wi_bd"], 1, 2), p["b_rg"], p["b_ig"], p["lru_lambda"], name="lru_bwd",
        carried=[("a2a", rows_slots)] if exchange_own else [])
    g["w_rg"], g["w_ig"] = _block_diag_grad(dwr), _block_diag_grad(dwi)
    dy_b3 = dys[1].reshape(shape3)
    (*dqs, dsinks), got_fi = swa_bwd(s["qs"], p["sinks"], s["y_b"], s["lse_b"], dy_b3, name="swa_bwd",
                                     carried=[("a2a", fi_slots)] if exchange_own else [])
    g["sinks"] = dsinks[0, :N_HEADS]
    dqd, got_in = dil_bwd(s["qd"], s["y_c"], s["lse_c"], dys[2].reshape(shape3), name="dil_bwd",
                          carried=[("a2a", above_w_in)] if above_w_in is not None else [])
    flat = lambda t: t.reshape(t_dim, t.shape[-1])
    dproj_f = jnp.concatenate([flat(dlx), flat(dlg), dgates], axis=1)
    dproj_qs = jnp.concatenate([flat(t) for t in dqs], axis=1).astype(BF16)
    dproj_qd = jnp.concatenate([flat(t) for t in dqd], axis=1).astype(BF16)
    g["w_in_f"] = matmul(s["xb"], dproj_f, trans_a=True, out_dtype=BF16, name="dw_in_f")
    g["w_in_qs"] = matmul(s["xb"], dproj_qs, trans_a=True, out_dtype=BF16, name="dw_in_qs")
    g["w_in_qd"] = matmul(s["xb"], dproj_qd, trans_a=True, out_dtype=BF16, name="dw_in_qd")
    dx = matmul(dproj_f, p["w_in_f"], trans_b=True, add=dz1, add_scale=ALPHA, name="dx_f")
    dx = matmul(dproj_qs, p["w_in_qs"], trans_b=True, add=dx, name="dx_qs")
    dx = matmul(dproj_qd, p["w_in_qd"], trans_b=True, add=dx, name="dx_qd")
    g = {k: (v.reshape(p[k].shape) if k in p else v) for k, v in g.items()}
    return dx, g, dict(late=(got_fi[0], got_rows[0]) if exchange_own else None, w_in=got_in[0] if got_in else None)


def local_step(x, target, layer_params, layer_shards=None, first_w_in=None):
    bsz, s_len, d = x.shape
    t_dim = bsz * s_len
    xf = x.reshape(t_dim, d)
    xb = xf.astype(BF16)
    exchanging = layer_shards is not None
    saved, gathered = [], first_w_in
    for l in range(DEPTH):
        p = layer_params(l, gathered)
        xf, xb, s, gathered = layer_fwd(xf, xb, p, bsz, own_late=layer_shards[l][1:] if exchanging else None,
                                        next_w_in=layer_shards[l + 1][0] if exchanging and l + 1 < DEPTH else None)
        saved.append(s)
    dy, sq = loss_head(xf, target.reshape(t_dim, d), name="loss_head")
    loss = 0.5 * jnp.sum(sq) / d
    grads, received, w_in_slots = [None] * DEPTH, [[None] * 3 for _ in range(DEPTH)], None
    for l in reversed(range(DEPTH)):
        dy, grads[l], got = layer_bwd(dy, saved[l], bsz, exchange_own=exchanging, above_w_in=w_in_slots)
        if got["w_in"] is not None:
            received[l + 1][0] = got["w_in"]
        if exchanging:
            received[l][1:] = got["late"]
            w_in_slots = _w_in_slots(grads[l])
    return loss, dy.reshape(bsz, s_len, d), grads, received, w_in_slots


W_IN_SEGMENTS = (("w_in_f", 0, 0, 2 * D_MODEL), ("w_in_qs", 0, 2 * D_MODEL, W_QS), ("w_in_qd", 0, 2 * D_MODEL + W_QS, W_QD),
                 ("w_in_f", 2 * D_MODEL, 2 * D_MODEL + W_QS + W_QD, 3 * D_MODEL))
ROW_SHARDED = ("w_branch", "w_out", "w_ffn_out")


def _cols_of_shards(shards, lo, hi):
    width = shards[0].shape[-1]
    parts = []
    for k, sh in enumerate(shards):
        a, b = max(lo, k * width), min(hi, (k + 1) * width)
        if a < b:
            parts.append(sh[..., a - k * width:b - k * width])
    return parts[0] if len(parts) == 1 else jnp.concatenate(parts, axis=-1)


def _cols_of_w_in(pieces, lo, hi):
    parts = []
    for name, p0, l0, width in W_IN_SEGMENTS:
        a, b = max(lo, l0), min(hi, l0 + width)
        if a < b:
            parts.append(pieces[name][..., p0 + a - l0:p0 + b - l0])
    return parts[0] if len(parts) == 1 else jnp.concatenate(parts, axis=-1)


W_IN_COLS = W_F + W_QS + W_QD


def _layer_shards(w):
    rows = jnp.concatenate([w[k].reshape(DEPTH, -1, D_MODEL) for k in ROW_SHARDED], axis=1).astype(BF16)
    w_in, w_fi = w["w_in"].astype(BF16), w["w_ffn_in"].astype(BF16)
    return [(w_in[l], w_fi[l], rows[l]) for l in range(DEPTH)]


ROW_COUNTS = (3 * D_MODEL // N_DEV, D_MODEL // N_DEV, FF_HIDDEN // N_DEV)


def _w_in_weights(g_in):
    sh = [g_in[k] for k in range(N_DEV)]
    return dict(w_in_f=jnp.concatenate([_cols_of_shards(sh, 0, 2 * D_MODEL), _cols_of_shards(sh, W_IN_COLS - 3 * D_MODEL, W_IN_COLS)], axis=-1),
                w_in_qs=_cols_of_shards(sh, 2 * D_MODEL, 2 * D_MODEL + W_QS),
                w_in_qd=_cols_of_shards(sh, 2 * D_MODEL + W_QS, 2 * D_MODEL + W_QS + W_QD))


def _late_weights(g_fi, g_rows):
    p = dict(w_ffn_in=jnp.concatenate([g_fi[k] for k in range(N_DEV)], axis=-1))
    off = 0
    for k, n in zip(ROW_SHARDED, ROW_COUNTS):
        t = g_rows[:, off:off + n]
        if k == "w_branch":
            p[k] = jnp.transpose(t.reshape(N_DEV, 3, n // 3, D_MODEL), (1, 0, 2, 3)).reshape(3, -1, D_MODEL)
        else:
            p[k] = t.reshape(-1, D_MODEL)
        off += n
    return p


def _w_in_slots(g):
    shard = W_IN_COLS // N_DEV
    return jnp.stack([_cols_of_w_in(g, k * shard, (k + 1) * shard) for k in range(N_DEV)]).astype(BF16)


def _late_slots(g):
    shard = g["w_ffn_in"].shape[-1] // N_DEV
    s_fi = jnp.stack([g["w_ffn_in"][:, k * shard:(k + 1) * shard] for k in range(N_DEV)]).astype(BF16)
    rows = jnp.concatenate([jnp.transpose(g["w_branch"].reshape(3, N_DEV, -1, D_MODEL), (1, 0, 2, 3)).reshape(N_DEV, -1, D_MODEL),
                            g["w_out"].reshape(N_DEV, -1, D_MODEL), g["w_ffn_out"].reshape(N_DEV, -1, D_MODEL)], axis=1).astype(BF16)
    return s_fi, rows


def _pad_rows(flat, tile_rows):
    n = flat.shape[0]
    per = tile_rows * LANES
    total = -(-n // per) * per
    return jnp.pad(flat, (0, total - n)).reshape(-1, LANES)


def kernel(x, w_in, conv_w, conv_b, w_rg, b_rg, w_ig, b_ig, lru_lambda, sinks, w_branch, w_out, ln1_g, ln1_b, w_ffn_in, w_ffn_out, ln2_g, ln2_b, loss_target, m_w_in, m_conv_w, m_conv_b, m_w_rg, m_b_rg, m_w_ig, m_b_ig, m_lru_lambda, m_sinks, m_w_branch, m_w_out, m_ln1_g, m_ln1_b, m_w_ffn_in, m_w_ffn_out, m_ln2_g, m_ln2_b, v_w_in, v_conv_w, v_conv_b, v_w_rg, v_b_rg, v_w_ig, v_b_ig, v_lru_lambda, v_sinks, v_w_branch, v_w_out, v_ln1_g, v_ln1_b, v_w_ffn_in, v_w_ffn_out, v_ln2_g, v_ln2_b):
    w = dict(w_in=w_in, conv_w=conv_w, conv_b=conv_b, w_rg=w_rg, b_rg=b_rg, w_ig=w_ig, b_ig=b_ig, lru_lambda=lru_lambda, sinks=sinks,
             w_branch=w_branch, w_out=w_out, ln1_g=ln1_g, ln1_b=ln1_b, w_ffn_in=w_ffn_in, w_ffn_out=w_ffn_out, ln2_g=ln2_g, ln2_b=ln2_b)
    m = dict(w_in=m_w_in, conv_w=m_conv_w, conv_b=m_conv_b, w_rg=m_w_rg, b_rg=m_b_rg, w_ig=m_w_ig, b_ig=m_b_ig, lru_lambda=m_lru_lambda,
             sinks=m_sinks, w_branch=m_w_branch, w_out=m_w_out, ln1_g=m_ln1_g, ln1_b=m_ln1_b, w_ffn_in=m_w_ffn_in, w_ffn_out=m_w_ffn_out,
             ln2_g=m_ln2_g, ln2_b=m_ln2_b)
    v = dict(w_in=v_w_in, conv_w=v_conv_w, conv_b=v_conv_b, w_rg=v_w_rg, b_rg=v_b_rg, w_ig=v_w_ig, b_ig=v_b_ig, lru_lambda=v_lru_lambda,
             sinks=v_sinks, w_branch=v_w_branch, w_out=v_w_out, ln1_g=v_ln1_g, ln1_b=v_ln1_b, w_ffn_in=v_w_ffn_in, w_ffn_out=v_w_ffn_out,
             ln2_g=v_ln2_g, ln2_b=v_ln2_b)
    order = ["w_in", "conv_w", "conv_b", "w_rg", "b_rg", "w_ig", "b_ig", "lru_lambda", "sinks", "w_branch", "w_out", "ln1_g", "ln1_b",
             "w_ffn_in", "w_ffn_out", "ln2_g", "ln2_b"]
    me = 4 * lax.axis_index("x") + 2 * lax.axis_index("y") + lax.axis_index("c")

    names = ("w_in", "w_ffn_in", "w_rows")
    shards = _layer_shards(w)
    first_w_in = exchange("gather", shards[0][0], name="gather_w_in")
    cw = exchange("gather", conv_w.reshape(-1, LANES), name="gather_conv_w")
    conv_w_full = jnp.moveaxis(cw.reshape(N_DEV, DEPTH, CONV_WIDTH, LANES), 0, 2).reshape(DEPTH, CONV_WIDTH, D_MODEL)

    def layer_params(l, gathered_w_in):
        return {**_w_in_weights(gathered_w_in), **{k: w[k][l] for k in SMALL}, "conv_w": conv_w_full[l]}

    loss_local, grad_x, grads, received, w_in_slots = local_step(x, loss_target, layer_params, shards, first_w_in)
    loss = lax.psum(loss_local, ("x", "y", "c"))
    received[0][0] = exchange("a2a", w_in_slots, name="exchange_g_w_in")

    sums = [[sum_slots(t, name=f"sum_g_{n}") for t, n in zip(received[l], names)] for l in range(DEPTH)]
    g_final = {"w_in": jnp.stack([sums[l][0] for l in range(DEPTH)]), "w_ffn_in": jnp.stack([sums[l][1] for l in range(DEPTH)])}
    off = 0
    for k, n in zip(ROW_SHARDED, ROW_COUNTS):
        g_final[k] = jnp.stack([sums[l][2][off:off + n] for l in range(DEPTH)]).reshape(w[k].shape)
        off += n
    grads = {k: jnp.stack([grads[l][k] for l in range(DEPTH)]) for k in list(SMALL) + ["conv_w"]}

    small_names = list(SMALL) + ["conv_w"]
    small_sizes = [grads[k].size for k in small_names]
    svec = _pad_rows(jnp.concatenate([grads[k].reshape(-1) for k in small_names]), SMALL_ROWS_TILE)
    ssum = sum_slots(exchange("gather", svec, name="gather_small_grads"), name="sum_small_grads")
    sflat, off = ssum.reshape(-1), 0
    for k, n in zip(small_names, small_sizes):
        g_final[k] = sflat[off:off + n].reshape(grads[k].shape)
        off += n
    g_final["conv_w"] = lax.dynamic_slice_in_dim(g_final["conv_w"], me * LANES, LANES, axis=2)

    delta, new_m, new_v = {}, {}, {}
    for k in list(BIG) + ["conv_w"]:
        cols = w[k].shape[-1]
        two_d = lambda t: t.reshape(-1, cols)
        d_, m_, v_ = adamw(two_d(w[k]), two_d(g_final[k]), two_d(m[k]), two_d(v[k]), name=f"adamw_{k}")
        delta[k], new_m[k], new_v[k] = d_.reshape(w[k].shape), m_.reshape(w[k].shape), v_.reshape(w[k].shape)
    pack_small = lambda dct: _pad_rows(jnp.concatenate([dct[k].reshape(-1) for k in SMALL]), SMALL_ROWS_TILE)
    d_, m_, v_ = adamw(pack_small(w), pack_small(g_final), pack_small(m), pack_small(v), name="adamw_small")
    off = 0
    for k in SMALL:
        n = w[k].size
        for dst, src in ((delta, d_), (new_m, m_), (new_v, v_)):
            dst[k] = src.reshape(-1)[off:off + n].reshape(w[k].shape)
        off += n
    return (loss, grad_x, *[g_final[k] for k in order], *[delta[k] for k in order], *[new_m[k] for k in order], *[new_v[k] for k in order])
```

```python
import functools
import math

import jax
import jax.numpy as jnp
from jax import lax
from jax.experimental import pallas as pl
from jax.experimental.pallas import tpu as pltpu

F32 = jnp.float32
BF16 = jnp.bfloat16

N_DEV = 8
DEPTH = 4
D_MODEL = 1024
HEAD_DIM = 64
LANES = 128
N_HEADS = D_MODEL // HEAD_DIM
SWA_KV_HEADS = 4
ATT_BLOCK = 128
DILATIONS = (1, 4, 16)
CONV_WIDTH = 4
LRU_C = 8.0
FF_HIDDEN = 2816
ALPHA = (2.0 * DEPTH) ** 0.25
LN_EPS = 1e-5
NEG_INF = -1e30
W_F = 5 * D_MODEL
W_QS = D_MODEL + 2 * SWA_KV_HEADS * HEAD_DIM
W_QD = 3 * D_MODEL

ADAM_LR = 0.001
ADAM_B1 = 0.9
ADAM_B2 = 0.999
ADAM_EPS = 1e-08
ADAM_WD = 0.01
ADAM_STEP = 10

VMEM_LIMIT = 56 * 1024 * 1024
MATMUL_BLOCK_BYTES = 40 * 1024 * 1024
MESH = pl.DeviceIdType.MESH


def _pick(n, cands):
    for c in cands:
        if n % c == 0:
            return c
    raise ValueError(f"no tile for {n} among {cands}")


def _params(sem):
    return pltpu.CompilerParams(dimension_semantics=sem, vmem_limit_bytes=VMEM_LIMIT)


def _tile(n, cap):
    best = None
    for t in range(LANES, cap + 1, LANES):
        if n % t == 0:
            best = t
    assert best is not None, (n, cap)
    return best


def matmul(a, b, *, name, trans_a=False, trans_b=False, out_dtype=F32, add=None, add_scale=1.0):
    if trans_a:
        k_dim, m_dim = a.shape
    else:
        m_dim, k_dim = a.shape
    n_dim = b.shape[0] if trans_b else b.shape[1]
    assert (b.shape[1] if trans_b else b.shape[0]) == k_dim
    tn = _tile(n_dim, 1408)
    tm, tk = _tile(m_dim, 1024), _tile(k_dim, 1408)
    for cand in (1024, 512, 256):
        ctm = _tile(m_dim, cand)
        blocks = 2 * (ctm * k_dim * a.dtype.itemsize + tn * k_dim * b.dtype.itemsize + ctm * tn * jnp.dtype(out_dtype).itemsize
                      + (ctm * tn * add.dtype.itemsize if add is not None else 0))
        if blocks <= MATMUL_BLOCK_BYTES:
            tm, tk = ctm, k_dim
            break
    nk = k_dim // tk
    dims = (((0 if trans_a else 1,), (1 if trans_b else 0,)), ((), ()))

    def body(*refs):
        if add is None:
            a_ref, b_ref, o_ref, acc_ref = refs
            add_ref = None
        else:
            a_ref, b_ref, add_ref, o_ref, acc_ref = refs
        k = pl.program_id(2)
        part = lax.dot_general(a_ref[...].astype(BF16), b_ref[...].astype(BF16), dims, preferred_element_type=F32)

        def finish(r):
            if add_ref is not None:
                r = r + add_scale * add_ref[...].astype(F32)
            o_ref[...] = r.astype(out_dtype)

        if nk == 1:
            finish(part)
        else:
            @pl.when(k == 0)
            def _():
                acc_ref[...] = part

            @pl.when((k > 0) & (k < nk - 1))
            def _():
                acc_ref[...] += part

            @pl.when(k == nk - 1)
            def _():
                finish(acc_ref[...] + part)

    a_spec = pl.BlockSpec((tk, tm), lambda i, j, k: (k, i)) if trans_a else pl.BlockSpec((tm, tk), lambda i, j, k: (i, k))
    b_spec = pl.BlockSpec((tn, tk), lambda i, j, k: (j, k)) if trans_b else pl.BlockSpec((tk, tn), lambda i, j, k: (k, j))
    in_specs = [a_spec, b_spec]
    args = [a, b]
    if add is not None:
        in_specs.append(pl.BlockSpec((tm, tn), lambda i, j, k: (i, j)))
        args.append(add)
    return pl.pallas_call(
        body,
        out_shape=jax.ShapeDtypeStruct((m_dim, n_dim), out_dtype),
        grid=(m_dim // tm, n_dim // tn, nk),
        in_specs=in_specs,
        out_specs=pl.BlockSpec((tm, tn), lambda i, j, k: (i, j)),
        scratch_shapes=[pltpu.VMEM((tm, tn) if nk > 1 else (8, LANES), F32)],
        compiler_params=_params(("parallel", "parallel", "arbitrary")),
        name=name,
    )(*args)


def ln_fwd(x, a, w, g, b, *, name):
    t_dim, d = x.shape
    k_dim = a.shape[1]
    tr = _tile(t_dim, 512)

    def body(x_ref, a_ref, w_ref, g_ref, b_ref, y_ref, yb_ref, z_ref):
        z = ALPHA * x_ref[...] + jnp.dot(a_ref[...], w_ref[...], preferred_element_type=F32)
        mu = jnp.mean(z, axis=-1, keepdims=True)
        zc = z - mu
        var = jnp.mean(zc * zc, axis=-1, keepdims=True)
        y = zc * lax.rsqrt(var + LN_EPS) * g_ref[...] + b_ref[...]
        y_ref[...] = y
        yb_ref[...] = y.astype(BF16)
        z_ref[...] = z

    row = pl.BlockSpec((tr, d), lambda i: (i, 0))
    vec = pl.BlockSpec((1, d), lambda i: (0, 0))
    return pl.pallas_call(
        body,
        out_shape=(jax.ShapeDtypeStruct((t_dim, d), F32), jax.ShapeDtypeStruct((t_dim, d), BF16), jax.ShapeDtypeStruct((t_dim, d), F32)),
        grid=(t_dim // tr,),
        in_specs=[row, pl.BlockSpec((tr, k_dim), lambda i: (i, 0)), pl.BlockSpec((k_dim, d), lambda i: (0, 0)), vec, vec],
        out_specs=(row, row, row),
        compiler_params=_params(("parallel",)),
        name=name,
    )(x, a, w, g.reshape(1, d), b.reshape(1, d))


def ln_bwd(dy, z, g, *, name, a=None, w=None, dy_scale=1.0):
    t_dim, d = dy.shape
    tr = _pick(t_dim, (256, 128, 8))

    def body(*refs):
        if a is None:
            dy_ref, z_ref, g_ref, dz_ref, dzb_ref, dg_ref, db_ref = refs
        else:
            dy_ref, a_ref, w_ref, z_ref, g_ref, dz_ref, dzb_ref, dg_ref, db_ref = refs

        @pl.when(pl.program_id(0) == 0)
        def _():
            dg_ref[...] = jnp.zeros_like(dg_ref)
            db_ref[...] = jnp.zeros_like(db_ref)

        z = z_ref[...]
        dyv = dy_scale * dy_ref[...]
        if a is not None:
            dyv = dyv + lax.dot_general(a_ref[...], w_ref[...], NT, preferred_element_type=F32)
        mu = jnp.mean(z, axis=-1, keepdims=True)
        zc = z - mu
        var = jnp.mean(zc * zc, axis=-1, keepdims=True)
        rstd = lax.rsqrt(var + LN_EPS)
        xhat = zc * rstd
        dxhat = dyv * g_ref[...]
        m1 = jnp.mean(dxhat, axis=-1, keepdims=True)
        m2 = jnp.mean(dxhat * xhat, axis=-1, keepdims=True)
        dz = rstd * (dxhat - m1 - xhat * m2)
        dz_ref[...] = dz
        dzb_ref[...] = dz.astype(BF16)
        dg_ref[...] += jnp.sum(dyv * xhat, axis=0, keepdims=True)
        db_ref[...] += jnp.sum(dyv, axis=0, keepdims=True)

    row = pl.BlockSpec((tr, d), lambda i: (i, 0))
    vec = pl.BlockSpec((1, d), lambda i: (0, 0))
    in_specs, args = [row], [dy]
    if a is not None:
        k_dim = a.shape[1]
        in_specs += [pl.BlockSpec((tr, k_dim), lambda i: (i, 0)), pl.BlockSpec((d, k_dim), lambda i: (0, 0))]
        args += [a, w]
    return pl.pallas_call(
        body,
        out_shape=(jax.ShapeDtypeStruct((t_dim, d), F32), jax.ShapeDtypeStruct((t_dim, d), BF16),
                   jax.ShapeDtypeStruct((1, d), F32), jax.ShapeDtypeStruct((1, d), F32)),
        grid=(t_dim // tr,),
        in_specs=in_specs + [row, vec],
        out_specs=(row, row, vec, vec),
        compiler_params=_params(("arbitrary",)),
        name=name,
    )(*args, z, g.reshape(1, d))


def loss_head(y, target, *, name):
    t_dim, d = y.shape
    tr = _pick(t_dim, (256, 128, 8))

    def body(y_ref, t_ref, dy_ref, sq_ref):
        @pl.when(pl.program_id(0) == 0)
        def _():
            sq_ref[...] = jnp.zeros_like(sq_ref)

        diff = y_ref[...] - t_ref[...]
        dy_ref[...] = diff / d
        sq_ref[...] += jnp.sum(diff * diff, axis=0, keepdims=True)

    row = pl.BlockSpec((tr, d), lambda i: (i, 0))
    vec = pl.BlockSpec((1, d), lambda i: (0, 0))
    return pl.pallas_call(
        body,
        out_shape=(jax.ShapeDtypeStruct((t_dim, d), F32), jax.ShapeDtypeStruct((1, d), F32)),
        grid=(t_dim // tr,),
        in_specs=[row, row],
        out_specs=(row, vec),
        compiler_params=_params(("arbitrary",)),
        name=name,
    )(y, target)


def _sigmoid(x):
    return 0.5 * jnp.tanh(0.5 * x) + 0.5


def ffn_in_swiglu(x, w, *, name):
    t_dim, d = x.shape
    f = w.shape[1] // 2
    tm, tn = _tile(t_dim, 1024), _tile(f, 1408)
    nf = f // tn

    def body(x_ref, w1_ref, w3_ref, h1_ref, h3_ref, act_ref):
        xv = x_ref[...]
        h1 = jnp.dot(xv, w1_ref[...], preferred_element_type=F32)
        h3 = jnp.dot(xv, w3_ref[...], preferred_element_type=F32)
        h1_ref[...] = h1.astype(BF16)
        h3_ref[...] = h3.astype(BF16)
        act_ref[...] = (h1 * _sigmoid(h1) * h3).astype(BF16)

    out = pl.BlockSpec((tm, tn), lambda i, j: (i, j))
    return pl.pallas_call(
        body,
        out_shape=(jax.ShapeDtypeStruct((t_dim, f), BF16),) * 3,
        grid=(t_dim // tm, nf),
        in_specs=[pl.BlockSpec((tm, d), lambda i, j: (i, 0)), pl.BlockSpec((d, tn), lambda i, j: (0, j)),
                  pl.BlockSpec((d, tn), lambda i, j: (0, j + nf))],
        out_specs=(out, out, out),
        compiler_params=_params(("parallel", "parallel")),
        name=name,
    )(x, w, w)


def swiglu_bwd(dz, w_ffn_out, h1, h3, *, name):
    t_dim, d = dz.shape
    f = h1.shape[1]
    tr = _tile(t_dim, 512)

    def body(dz_ref, w_ref, h1_ref, h3_ref, dh_ref):
        da = lax.dot_general(dz_ref[...], w_ref[...], NT, preferred_element_type=F32)
        h1 = h1_ref[...].astype(F32)
        sg = _sigmoid(h1)
        dh_ref[:, :f] = (da * h3_ref[...].astype(F32) * sg * (1.0 + h1 * (1.0 - sg))).astype(BF16)
        dh_ref[:, f:] = (da * h1 * sg).astype(BF16)

    wide = pl.BlockSpec((tr, f), lambda i: (i, 0))
    return pl.pallas_call(
        body,
        out_shape=jax.ShapeDtypeStruct((t_dim, 2 * f), BF16),
        grid=(t_dim // tr,),
        in_specs=[pl.BlockSpec((tr, d), lambda i: (i, 0)), pl.BlockSpec((f, d), lambda i: (0, 0)), wide, wide],
        out_specs=pl.BlockSpec((tr, 2 * f), lambda i: (i, 0)),
        compiler_params=_params(("parallel",)),
        name=name,
    )(dz, w_ffn_out, h1, h3)


def branch_merge(ys, w_branch, proj_f, *, name):
    t_dim, d = ys[0].shape
    tm = _tile(t_dim, 512)

    def body(y0, y1, y2, w_ref, g0, g1, g2, m_ref, b0, b1, b2):
        acc = None
        for n, (y, g, b) in enumerate(((y0, g0, b0), (y1, g1, b1), (y2, g2, b2))):
            br = jnp.dot(y[...], w_ref[n], preferred_element_type=F32)
            b[...] = br.astype(BF16)
            t = _sigmoid(g[...]) * br
            acc = t if acc is None else acc + t
        m_ref[...] = acc.astype(BF16)

    row = pl.BlockSpec((tm, d), lambda i: (i, 0))
    gate = [pl.BlockSpec((tm, d), functools.partial(lambda n, i: (i, 2 + n), n)) for n in range(3)]
    merged, *br = pl.pallas_call(
        body,
        out_shape=(jax.ShapeDtypeStruct((t_dim, d), BF16),) * 4,
        grid=(t_dim // tm,),
        in_specs=[row, row, row, pl.BlockSpec((3, d, d), lambda i: (0, 0, 0))] + gate,
        out_specs=(row, row, row, row),
        compiler_params=_params(("parallel",)),
        name=name,
    )(*ys, w_branch, proj_f, proj_f, proj_f)
    return merged, br


def merge_bwd(dz, w_out, proj_f, br, *, name):
    t_dim, d = dz.shape
    tr = _tile(t_dim, 512)

    def body(dz_ref, w_ref, g0, g1, g2, b0, b1, b2, d0, d1, d2, dg_ref):
        dm = lax.dot_general(dz_ref[...], w_ref[...], NT, preferred_element_type=F32)
        for n, (g, b, o) in enumerate(((g0, b0, d0), (g1, b1, d1), (g2, b2, d2))):
            sg = _sigmoid(g[...])
            o[...] = (dm * sg).astype(BF16)
            dg_ref[:, n * d:(n + 1) * d] = (dm * b[...].astype(F32) * sg * (1.0 - sg)).astype(BF16)

    row = pl.BlockSpec((tr, d), lambda i: (i, 0))
    gate = [pl.BlockSpec((tr, d), functools.partial(lambda n, i: (i, 2 + n), n)) for n in range(3)]
    return pl.pallas_call(
        body,
        out_shape=(jax.ShapeDtypeStruct((t_dim, d), BF16),) * 3 + (jax.ShapeDtypeStruct((t_dim, 3 * d), BF16),),
        grid=(t_dim // tr,),
        in_specs=[row, pl.BlockSpec((d, d), lambda i: (0, 0))] + gate + [row, row, row],
        out_specs=(row, row, row, pl.BlockSpec((tr, 3 * d), lambda i: (i, 0))),
        compiler_params=_params(("parallel",)),
        name=name,
    )(dz, w_out, proj_f, proj_f, proj_f, *br)


GELU_C = math.sqrt(2.0 / math.pi)
PAD = 8
SCAN_TILES = 8


def _gelu(x):
    return 0.5 * x * (1.0 + jnp.tanh(GELU_C * (x + 0.044715 * x * x * x)))


def _gelu_grad(x):
    t = jnp.tanh(GELU_C * (x + 0.044715 * x * x * x))
    return 0.5 * (1.0 + t) + 0.5 * x * (1.0 - t * t) * GELU_C * (1.0 + 3.0 * 0.044715 * x * x)


def _neg_expm1(x, exp_x):
    series = -x * (1.0 + x * (0.5 + x * (1.0 / 6.0)))
    return jnp.where(x > -0.02, series, 1.0 - exp_x)


def _lru_gates(xv, cw_ref, cb_ref, wr_ref, wi_ref, br_ref, bi_ref, lam_ref, pad_ref, s_len):
    pad_ref[pl.ds(0, PAD), :] = jnp.zeros((PAD, LANES), F32)
    pad_ref[pl.ds(PAD, s_len), :] = xv
    xc = cb_ref[...] + jnp.zeros((s_len, LANES), F32)
    for j in range(CONV_WIDTH):
        xc = xc + pad_ref[pl.ds(PAD - (CONV_WIDTH - 1) + j, s_len), :] * cw_ref[pl.ds(j, 1), :]
    xcb = xc.astype(BF16)
    r = _sigmoid(jnp.dot(xcb, wr_ref[0].astype(BF16), preferred_element_type=F32) + br_ref[...])
    i = _sigmoid(jnp.dot(xcb, wi_ref[0].astype(BF16), preferred_element_type=F32) + bi_ref[...])
    nl = -lam_ref[...]
    sp = jnp.maximum(nl, 0.0) + jnp.log(1.0 + jnp.exp(-jnp.abs(nl)))
    log_a = -LRU_C * r * sp
    a = jnp.exp(log_a)
    mult = jnp.sqrt(_neg_expm1(2.0 * log_a, a * a))
    return xc, r, i, sp, a, mult


def _tile_scan(a, b, row, reverse):
    for s in (1, 2, 4):
        if reverse:
            a_sh = pltpu.roll(a, 8 - s, 0)
            b_sh = pltpu.roll(b, 8 - s, 0)
            m = row + s <= 7
        else:
            a_sh = pltpu.roll(a, s, 0)
            b_sh = pltpu.roll(b, s, 0)
            m = row >= s
        b = jnp.where(m, a * b_sh + b, b)
        a = jnp.where(m, a * a_sh, a)
    return a, b


def lru_fwd(proj_f, conv_w, conv_b, wr_bd, wi_bd, b_rg, b_ig, lam, *, name, carried=None):
    bsz, s_len, _ = proj_f.shape
    d = D_MODEL
    ncb = d // LANES
    n_tiles = s_len // 8

    def body(x_ref, g_ref, cw_ref, cb_ref, wr_ref, wi_ref, br_ref, bi_ref, lam_ref, y_ref, h_ref, pad_ref, a_s, b_s):
        xc, r, i, sp, a, mult = _lru_gates(x_ref[0], cw_ref, cb_ref, wr_ref, wi_ref, br_ref, bi_ref, lam_ref, pad_ref, s_len)
        a_s[...] = a
        b_s[...] = mult * (i * xc)
        row = lax.broadcasted_iota(jnp.int32, (8, LANES), 0)

        def tiles(t, carry):
            starts = [pl.multiple_of((t * SCAN_TILES + u) * 8, 8) for u in range(SCAN_TILES)]
            local = [_tile_scan(a_s[pl.ds(i0, 8), :], b_s[pl.ds(i0, 8), :], row, False) for i0 in starts]
            for i0, (ac, hl) in zip(starts, local):
                h = hl + ac * carry
                h_ref[0, pl.ds(i0, 8), :] = h
                carry = jnp.broadcast_to(h[7:8, :], (8, LANES))
            return carry

        lax.fori_loop(0, n_tiles // SCAN_TILES, tiles, jnp.zeros((8, LANES), F32))
        y_ref[0] = (h_ref[0] * _gelu(g_ref[0])).astype(BF16)

    slab = lambda off: pl.BlockSpec((1, s_len, LANES), functools.partial(lambda o, c, b: (b, 0, o + c), off))
    vec = pl.BlockSpec((1, LANES), lambda c, b: (0, c))
    mat = pl.BlockSpec((1, LANES, LANES), lambda c, b: (c, 0, 0))
    out = pl.BlockSpec((1, s_len, LANES), lambda c, b: (b, 0, c))
    return call_with_exchange(
        body, carried,
        out_shape=(jax.ShapeDtypeStruct((bsz, s_len, d), BF16), jax.ShapeDtypeStruct((bsz, s_len, d), F32)),
        grid=(ncb, bsz),
        in_specs=[slab(0), slab(ncb), pl.BlockSpec((CONV_WIDTH, LANES), lambda c, b: (0, c)), vec, mat, mat, vec, vec, vec],
        out_specs=(out, out),
        scratch_shapes=[pltpu.VMEM((s_len + 2 * PAD, LANES), F32), pltpu.VMEM((s_len, LANES), F32), pltpu.VMEM((s_len, LANES), F32)],
        name=name,
        args=(proj_f, proj_f, conv_w, conv_b.reshape(1, d), wr_bd, wi_bd, b_rg.reshape(1, d), b_ig.reshape(1, d), lam.reshape(1, d)))


def lru_bwd(dy, proj_f, h, conv_w, conv_b, wr_bd, wi_bd, wr_bd_t, wi_bd_t, b_rg, b_ig, lam, *, name, carried=None):
    bsz, s_len, _ = proj_f.shape
    d = D_MODEL
    ncb = d // LANES
    n_tiles = s_len // 8

    def body(dy_ref, x_ref, g_ref, h_ref, cw_ref, cb_ref, wr_ref, wi_ref, wrt_ref, wit_ref, br_ref, bi_ref, lam_ref,
             dx_ref, dg_ref, dcw_ref, dcb_ref, dbr_ref, dbi_ref, dlam_ref, dwr_ref, dwi_ref, pad_ref, a_s, b_s, l_s):
        @pl.when(pl.program_id(1) == 0)
        def _():
            for ref in (dcw_ref, dcb_ref, dbr_ref, dbi_ref, dlam_ref, dwr_ref, dwi_ref):
                ref[...] = jnp.zeros_like(ref)

        xc, r, i, sp, a, mult = _lru_gates(x_ref[0], cw_ref, cb_ref, wr_ref, wi_ref, br_ref, bi_ref, lam_ref, pad_ref, s_len)
        gate = g_ref[0]
        hv = h_ref[0]
        dyv = dy_ref[0].astype(F32)
        dg_ref[0] = (dyv * hv * _gelu_grad(gate)).astype(BF16)
        b_s[...] = dyv * _gelu(gate)
        l_s[pl.ds(0, s_len), :] = a
        l_s[pl.ds(s_len, PAD), :] = jnp.zeros((PAD, LANES), F32)
        a_s[...] = l_s[pl.ds(1, s_len), :]
        row = lax.broadcasted_iota(jnp.int32, (8, LANES), 0)

        def tiles(t, carry):
            starts = [pl.multiple_of((n_tiles - 1 - (t * SCAN_TILES + u)) * 8, 8) for u in range(SCAN_TILES)]
            local = [_tile_scan(a_s[pl.ds(i0, 8), :], b_s[pl.ds(i0, 8), :], row, True) for i0 in starts]
            for i0, (ac, ll) in zip(starts, local):
                lmb = ll + ac * carry
                b_s[pl.ds(i0, 8), :] = lmb
                carry = jnp.broadcast_to(lmb[0:1, :], (8, LANES))
            return carry

        lax.fori_loop(0, n_tiles // SCAN_TILES, tiles, jnp.zeros((8, LANES), F32))
        lmb = b_s[...]
        l_s[pl.ds(0, PAD), :] = jnp.zeros((PAD, LANES), F32)
        l_s[pl.ds(PAD, s_len), :] = hv
        h_prev = l_s[pl.ds(PAD - 1, s_len), :]
        da = lmb * h_prev
        dmult = lmb * (i * xc)
        di = lmb * mult * xc
        dxc = lmb * mult * i
        dlog_a = da * a - dmult * a * a / mult
        dr = -LRU_C * sp * dlog_a
        dsp = jnp.sum(-LRU_C * r * dlog_a, axis=0, keepdims=True)
        dlam_ref[...] += dsp * (-_sigmoid(-lam_ref[...]))
        dpr = dr * r * (1.0 - r)
        dpi = di * i * (1.0 - i)
        dprb = dpr.astype(BF16)
        dpib = dpi.astype(BF16)
        xcb = xc.astype(BF16)
        dbr_ref[...] += jnp.sum(dpr, axis=0, keepdims=True)
        dbi_ref[...] += jnp.sum(dpi, axis=0, keepdims=True)
        tn = (((0,), (0,)), ((), ()))
        dwr_ref[0] += lax.dot_general(xcb, dprb, tn, preferred_element_type=F32)
        dwi_ref[0] += lax.dot_general(xcb, dpib, tn, preferred_element_type=F32)
        dxc = (dxc + jnp.dot(dprb, wrt_ref[0].astype(BF16), preferred_element_type=F32)
               + jnp.dot(dpib, wit_ref[0].astype(BF16), preferred_element_type=F32))
        dcb_ref[...] += jnp.sum(dxc, axis=0, keepdims=True)
        for j in range(CONV_WIDTH):
            dcw_ref[pl.ds(j, 1), :] += jnp.sum(dxc * pad_ref[pl.ds(PAD - (CONV_WIDTH - 1) + j, s_len), :], axis=0, keepdims=True)
        l_s[pl.ds(0, s_len), :] = dxc
        l_s[pl.ds(s_len, PAD), :] = jnp.zeros((PAD, LANES), F32)
        dx = jnp.zeros((s_len, LANES), F32)
        for j in range(CONV_WIDTH):
            dx = dx + l_s[pl.ds(CONV_WIDTH - 1 - j, s_len), :] * cw_ref[pl.ds(j, 1), :]
        dx_ref[0] = dx.astype(BF16)

    slab = lambda off: pl.BlockSpec((1, s_len, LANES), functools.partial(lambda o, c, b: (b, 0, o + c), off))
    vec = pl.BlockSpec((1, LANES), lambda c, b: (0, c))
    mat = pl.BlockSpec((1, LANES, LANES), lambda c, b: (c, 0, 0))
    cw = pl.BlockSpec((CONV_WIDTH, LANES), lambda c, b: (0, c))
    out = pl.BlockSpec((1, s_len, LANES), lambda c, b: (b, 0, c))
    vshape = jax.ShapeDtypeStruct((1, d), F32)
    mshape = jax.ShapeDtypeStruct((ncb, LANES, LANES), F32)
    return call_with_exchange(
        body, carried,
        out_shape=(jax.ShapeDtypeStruct((bsz, s_len, d), BF16),) * 2
        + (jax.ShapeDtypeStruct((CONV_WIDTH, d), F32), vshape, vshape, vshape, vshape, mshape, mshape),
        grid=(ncb, bsz),
        in_specs=[out, slab(0), slab(ncb), out, cw, vec, mat, mat, mat, mat, vec, vec, vec],
        out_specs=(out, out, cw, vec, vec, vec, vec, mat, mat),
        scratch_shapes=[pltpu.VMEM((s_len + 2 * PAD, LANES), F32), pltpu.VMEM((s_len, LANES), F32), pltpu.VMEM((s_len, LANES), F32),
                        pltpu.VMEM((s_len + 2 * PAD, LANES), F32)],
        name=name,
        args=(dy, proj_f, proj_f, h, conv_w, conv_b.reshape(1, d), wr_bd, wi_bd, wr_bd_t, wi_bd_t,
              b_rg.reshape(1, d), b_ig.reshape(1, d), lam.reshape(1, d)))


NT = (((1,), (1,)), ((), ()))
TN = (((0,), (0,)), ((), ()))
ATT_SCALE = HEAD_DIM ** -0.5


def _kv_place(head, n_kv_heads):
    kv = head // (N_HEADS // n_kv_heads)
    return kv // 2, kv % 2


def _band_mask(n, single):
    nk = ATT_BLOCK if single else 2 * ATT_BLOCK
    qi = lax.broadcasted_iota(jnp.int32, (2 * ATT_BLOCK, nk), 0) % ATT_BLOCK
    kj = lax.broadcasted_iota(jnp.int32, (2 * ATT_BLOCK, nk), 1)
    if single:
        return qi >= kj
    rel = qi + ATT_BLOCK - kj
    return (rel >= 0) & (rel <= ATT_BLOCK) & ((n > 0) | (kj >= ATT_BLOCK))


def _lane_halves():
    lane = lax.broadcasted_iota(jnp.int32, (1, LANES), 1)
    return lane < HEAD_DIM


def _stack_heads(t2, kh):
    first = _lane_halves()
    parts = []
    for a in range(2):
        ta = jnp.where(first if a == 0 else ~first, t2, jnp.zeros_like(t2))
        if a != kh[a]:
            ta = pltpu.roll(ta, HEAD_DIM, 1)
        parts.append(ta)
    return jnp.concatenate(parts, axis=0)


def _fold_heads(t, kh):
    t0, t1 = t[:ATT_BLOCK], t[ATT_BLOCK:]
    if t.shape[1] == LANES:
        if kh[0] != 0:
            t0 = pltpu.roll(t0, HEAD_DIM, 1)
        if kh[1] != 1:
            t1 = pltpu.roll(t1, HEAD_DIM, 1)
    return jnp.where(_lane_halves(), t0, t1)


def _rows_of_heads(t2):
    return jnp.concatenate([t2[:, 0:1], t2[:, HEAD_DIM:HEAD_DIM + 1]], axis=0)


PAIRS_AT_ONCE = 4


def _fill_bias(bias2_ref, bias1_ref=None):
    for i in range(2):
        bias2_ref[i] = jnp.where(_band_mask(i, False), 0.0, NEG_INF)
    if bias1_ref is not None:
        bias1_ref[...] = jnp.where(_band_mask(0, True), 0.0, NEG_INF)


def _pairs_fwd(items):
    ss = [lax.dot_general(_stack_heads(q2 * ATT_SCALE, kh), kk, NT, preferred_element_type=F32) + bias
          for q2, kk, _, bias, kh, _ in items]
    ps, ms, ls = [], [], []
    for s, (_, _, _, _, _, sink_col) in zip(ss, items):
        m = jnp.max(s, axis=-1, keepdims=True)
        if sink_col is not None:
            m = jnp.maximum(m, sink_col)
        p = jnp.exp(s - m)
        l = jnp.sum(p, axis=-1, keepdims=True)
        if sink_col is not None:
            l = l + jnp.exp(sink_col - m)
        ps.append(p.astype(BF16))
        ms.append(m)
        ls.append(l)
    pvs = [jnp.dot(p, it[2], preferred_element_type=F32) for p, it in zip(ps, items)]
    return list(zip(pvs, ms, ls))


def _pairs_bwd(items):
    first = _lane_halves()
    pre = []
    for q2, kk, vv, do2, o2, lse2, bias, kh in items:
        dd = do2 * o2
        dsum = jnp.concatenate([jnp.sum(jnp.where(first, dd, 0.0), axis=-1, keepdims=True),
                                jnp.sum(jnp.where(first, 0.0, dd), axis=-1, keepdims=True)], axis=0)
        qs = _stack_heads(q2 * ATT_SCALE, kh)
        dos = _stack_heads(do2.astype(BF16), kh)
        s = lax.dot_general(qs, kk, NT, preferred_element_type=F32) + bias
        dp = lax.dot_general(dos, vv, NT, preferred_element_type=F32)
        pre.append((qs, dos, s, dp, dsum))
    mid = []
    for (qs, dos, s, dp, dsum), it in zip(pre, items):
        p = jnp.exp(s - _rows_of_heads(it[5]))
        mid.append((p.astype(BF16), (p * (dp - dsum)).astype(BF16)))
    out = []
    for (pb, ds), (qs, dos, _, _, dsum), it in zip(mid, pre, items):
        dq = _fold_heads(jnp.dot(ds, it[1], preferred_element_type=F32), it[7]) * ATT_SCALE
        dk = lax.dot_general(ds, qs, TN, preferred_element_type=F32)
        dv = lax.dot_general(pb, dos, TN, preferred_element_type=F32)
        out.append((dq, dk, dv, dsum))
    return out


def swa_fwd(qkv, sinks, *, name, carried=None):
    bsz, s_len, width = qkv.shape
    ckv = SWA_KV_HEADS * HEAD_DIM
    nb = s_len // ATT_BLOCK
    kblk = D_MODEL // ckv

    def body(sink_ref, q_ref, kp_ref, kc_ref, vp_ref, vc_ref, o_ref, lse_ref, ob_ref, bias2):
        n = pl.program_id(1)
        _fill_bias(bias2)
        bias = bias2[jnp.minimum(n, 1)]
        kk = jnp.concatenate([kp_ref[0], kc_ref[0]], axis=0)
        vv = jnp.concatenate([vp_ref[0], vc_ref[0]], axis=0)
        top = lax.broadcasted_iota(jnp.int32, (2 * ATT_BLOCK, 1), 0) < ATT_BLOCK
        for hp0 in range(0, N_HEADS // 2, PAIRS_AT_ONCE):
            items, places = [], []
            for hp in range(hp0, hp0 + PAIRS_AT_ONCE):
                cols = slice(hp * LANES, (hp + 1) * LANES)
                kb, kh = _kv_place(2 * hp, SWA_KV_HEADS)
                kcols = slice(kb * LANES, (kb + 1) * LANES)
                sink_col = jnp.where(top, sink_ref[2 * hp], sink_ref[2 * hp + 1])
                items.append((q_ref[0, :, cols], kk[:, kcols], vv[:, kcols], bias, (kh, kh), sink_col))
                places.append((cols, (kh, kh)))
            for (pv, m, l), (cols, kh2) in zip(_pairs_fwd(items), places):
                o2 = _fold_heads(pv / l, kh2)
                o_ref[0, :, cols] = o2
                ob_ref[0, :, cols] = o2.astype(BF16)
                lse_ref[0, :, cols] = _fold_heads(m + jnp.log(l), kh2)

    prev = lambda n: jnp.maximum(n - 1, 0)
    out = pl.BlockSpec((1, ATT_BLOCK, D_MODEL), lambda b, n: (b, n, 0))
    sd = lambda dt: jax.ShapeDtypeStruct((bsz, s_len, D_MODEL), dt)
    return call_with_exchange(
        body, carried,
        out_shape=(sd(F32), sd(F32), sd(BF16)),
        grid=(bsz, nb),
        in_specs=[pl.BlockSpec(memory_space=pltpu.SMEM), out,
                  pl.BlockSpec((1, ATT_BLOCK, ckv), lambda b, n: (b, prev(n), kblk)),
                  pl.BlockSpec((1, ATT_BLOCK, ckv), lambda b, n: (b, n, kblk)),
                  pl.BlockSpec((1, ATT_BLOCK, ckv), lambda b, n: (b, prev(n), kblk + 1)),
                  pl.BlockSpec((1, ATT_BLOCK, ckv), lambda b, n: (b, n, kblk + 1))],
        out_specs=(out, out, out),
        scratch_shapes=[pltpu.VMEM((2, 2 * ATT_BLOCK, 2 * ATT_BLOCK), F32)],
        name=name,
        args=(sinks, qkv, qkv, qkv, qkv, qkv))


def swa_bwd(qkv, sinks, o, lse, do, *, name, carried=None):
    bsz, s_len, width = qkv.shape
    ckv = SWA_KV_HEADS * HEAD_DIM
    nb = s_len // ATT_BLOCK
    kblk = D_MODEL // ckv

    def body(sink_ref, q_ref, kp_ref, kc_ref, vp_ref, vc_ref, o_ref, lse_ref, do_ref, dq_ref, dk_ref, dv_ref, dsink_ref,
             dkk, dvv, ck, cv, bias2):
        n = pl.program_id(1)

        @pl.when((n == 0) & (pl.program_id(0) == 0))
        def _():
            dsink_ref[...] = jnp.zeros_like(dsink_ref)

        @pl.when(n < nb)
        def _():
            top = lax.broadcasted_iota(jnp.int32, (2 * ATT_BLOCK, 1), 0) < ATT_BLOCK
            lane = lax.broadcasted_iota(jnp.int32, dsink_ref.shape, 1)
            first_row = lax.broadcasted_iota(jnp.int32, dsink_ref.shape, 0) == 0
            _fill_bias(bias2)
            bias = bias2[jnp.minimum(n, 1)]
            kk = jnp.concatenate([kp_ref[0], kc_ref[0]], axis=0)
            vv = jnp.concatenate([vp_ref[0], vc_ref[0]], axis=0)
            dkk[...] = jnp.zeros_like(dkk)
            dvv[...] = jnp.zeros_like(dvv)
            for hp0 in range(0, N_HEADS // 2, PAIRS_AT_ONCE):
                items, places = [], []
                for hp in range(hp0, hp0 + PAIRS_AT_ONCE):
                    cols = slice(hp * LANES, (hp + 1) * LANES)
                    kb, kh = _kv_place(2 * hp, SWA_KV_HEADS)
                    kcols = slice(kb * LANES, (kb + 1) * LANES)
                    items.append((q_ref[0, :, cols], kk[:, kcols], vv[:, kcols], do_ref[0, :, cols], o_ref[0, :, cols],
                                  lse_ref[0, :, cols], bias, (kh, kh)))
                    places.append((cols, kcols, hp))
                for (dq, dk, dv, dsum), (cols, kcols, hp) in zip(_pairs_bwd(items), places):
                    dq_ref[0, :, cols] = dq
                    dkk[:, kcols] += dk
                    dvv[:, kcols] += dv
                    sink_col = jnp.where(top, sink_ref[2 * hp], sink_ref[2 * hp + 1])
                    t = -jnp.exp(sink_col - _rows_of_heads(lse_ref[0, :, cols])) * dsum
                    d0 = jnp.sum(t[:ATT_BLOCK], axis=0, keepdims=True)
                    d1 = jnp.sum(t[ATT_BLOCK:], axis=0, keepdims=True)
                    dsink_ref[...] += jnp.where(first_row & (lane == 2 * hp), d0, 0.0) + jnp.where(first_row & (lane == 2 * hp + 1), d1, 0.0)

        @pl.when((n >= 1) & (n < nb))
        def _():
            dk_ref[0] = ck[...] + dkk[pl.ds(0, ATT_BLOCK), :]
            dv_ref[0] = cv[...] + dvv[pl.ds(0, ATT_BLOCK), :]

        @pl.when(n == nb)
        def _():
            dk_ref[0] = ck[...]
            dv_ref[0] = cv[...]

        @pl.when(n < nb)
        def _():
            ck[...] = dkk[pl.ds(ATT_BLOCK, ATT_BLOCK), :]
            cv[...] = dvv[pl.ds(ATT_BLOCK, ATT_BLOCK), :]

    clamp = lambda n: jnp.minimum(n, nb - 1)
    prev = lambda n: jnp.maximum(n - 1, 0)
    row = pl.BlockSpec((1, ATT_BLOCK, D_MODEL), lambda b, n: (b, clamp(n), 0))
    kv_out = pl.BlockSpec((1, ATT_BLOCK, ckv), lambda b, n: (b, prev(n), 0))
    return call_with_exchange(
        body, carried,
        out_shape=(jax.ShapeDtypeStruct((bsz, s_len, D_MODEL), F32), jax.ShapeDtypeStruct((bsz, s_len, ckv), F32),
                   jax.ShapeDtypeStruct((bsz, s_len, ckv), F32), jax.ShapeDtypeStruct((8, LANES), F32)),
        grid=(bsz, nb + 1),
        in_specs=[pl.BlockSpec(memory_space=pltpu.SMEM), row,
                  pl.BlockSpec((1, ATT_BLOCK, ckv), lambda b, n: (b, prev(clamp(n)), kblk)),
                  pl.BlockSpec((1, ATT_BLOCK, ckv), lambda b, n: (b, clamp(n), kblk)),
                  pl.BlockSpec((1, ATT_BLOCK, ckv), lambda b, n: (b, prev(clamp(n)), kblk + 1)),
                  pl.BlockSpec((1, ATT_BLOCK, ckv), lambda b, n: (b, clamp(n), kblk + 1)),
                  row, row, row],
        out_specs=(row, kv_out, kv_out, pl.BlockSpec((8, LANES), lambda b, n: (0, 0))),
        scratch_shapes=[pltpu.VMEM((2 * ATT_BLOCK, ckv), F32), pltpu.VMEM((2 * ATT_BLOCK, ckv), F32),
                        pltpu.VMEM((ATT_BLOCK, ckv), F32), pltpu.VMEM((ATT_BLOCK, ckv), F32),
                        pltpu.VMEM((2, 2 * ATT_BLOCK, 2 * ATT_BLOCK), F32)],
        name=name,
        args=(sinks, qkv, qkv, qkv, qkv, qkv, o, lse, do))


DIL_PATTERNS = tuple((d, 2048 // d // ATT_BLOCK) for d in reversed(DILATIONS))
MHA = (0, 1)


def _dil_rows(idx, d, nb):
    j = idx // nb
    n = idx % nb
    base = j + n * (ATT_BLOCK * d)
    prev = jnp.maximum(base - ATT_BLOCK * d, j)
    if d == 1:
        return n, pl.ds(pl.multiple_of(base, ATT_BLOCK), ATT_BLOCK), pl.ds(pl.multiple_of(prev, ATT_BLOCK), ATT_BLOCK)
    return n, pl.ds(base, ATT_BLOCK, stride=d), pl.ds(prev, ATT_BLOCK, stride=d)


def dil_fwd(qkv, *, name, carried=None):
    bsz, s_len, _ = qkv.shape
    assert s_len == DIL_PATTERNS[0][0] * DIL_PATTERNS[0][1] * ATT_BLOCK
    npair = N_HEADS // 2

    def body(q_ref, k_ref, v_ref, y_ref, lse_ref, yb_ref, m_acc, l_acc, bias2, bias1):
        _fill_bias(bias2, bias1)
        for ci, (d, nb) in enumerate(DIL_PATTERNS):
            single = nb == 1

            def blocks(it, carry):
                items, places = [], []
                for u in range(PAIRS_AT_ONCE):
                    n, rows, prows = _dil_rows(it * PAIRS_AT_ONCE + u, d, nb)
                    kc = k_ref[rows, :].astype(BF16)
                    vc = v_ref[rows, :].astype(BF16)
                    if single:
                        kk, vv, bias = kc, vc, bias1[...]
                    else:
                        kk = jnp.concatenate([k_ref[prows, :].astype(BF16), kc], axis=0)
                        vv = jnp.concatenate([v_ref[prows, :].astype(BF16), vc], axis=0)
                        bias = bias2[jnp.minimum(n, 1)]
                    items.append((q_ref[rows, :].astype(BF16), kk, vv, bias, MHA, None))
                    places.append(rows)
                for (pv, m, l), rows in zip(_pairs_fwd(items), places):
                    o2, m2, l2 = _fold_heads(pv, MHA), _fold_heads(m, MHA), _fold_heads(l, MHA)
                    if ci == 0:
                        y_ref[rows, :] = o2
                        m_acc[rows, :] = m2
                        l_acc[rows, :] = l2
                    else:
                        m_old = m_acc[rows, :]
                        m_new = jnp.maximum(m_old, m2)
                        w_old = jnp.exp(m_old - m_new)
                        w_new = jnp.exp(m2 - m_new)
                        y_ref[rows, :] = y_ref[rows, :] * w_old + o2 * w_new
                        l_acc[rows, :] = l_acc[rows, :] * w_old + l2 * w_new
                        m_acc[rows, :] = m_new
                return carry

            lax.fori_loop(0, d * nb // PAIRS_AT_ONCE, blocks, 0)
        y = y_ref[...] / l_acc[...]
        y_ref[...] = y
        yb_ref[...] = y.astype(BF16)
        lse_ref[...] = m_acc[...] + jnp.log(l_acc[...])

    slab = lambda off: pl.BlockSpec((None, s_len, LANES), functools.partial(lambda o, b, h: (b, 0, o + h), off))
    sd = lambda dt: jax.ShapeDtypeStruct((bsz, s_len, D_MODEL), dt)
    return call_with_exchange(
        body, carried,
        out_shape=(sd(F32), sd(F32), sd(BF16)),
        grid=(bsz, npair),
        in_specs=[slab(0), slab(npair), slab(2 * npair)],
        out_specs=(slab(0), slab(0), slab(0)),
        scratch_shapes=[pltpu.VMEM((s_len, LANES), F32), pltpu.VMEM((s_len, LANES), F32),
                        pltpu.VMEM((2, 2 * ATT_BLOCK, 2 * ATT_BLOCK), F32), pltpu.VMEM((2 * ATT_BLOCK, ATT_BLOCK), F32)],
        name=name,
        args=(qkv, qkv, qkv))


def dil_bwd(qkv, y, lse, dy, *, name, carried=None):
    bsz, s_len, _ = qkv.shape
    npair = N_HEADS // 2

    def body(q_ref, k_ref, v_ref, y_ref, lse_ref, dy_ref, dq_ref, dk_ref, dv_ref, bias2, bias1):
        _fill_bias(bias2, bias1)
        assert DIL_PATTERNS[0][1] == 1
        for d, nb in DIL_PATTERNS:
            single = nb == 1

            def blocks(it, carry):
                items, places = [], []
                for u in range(PAIRS_AT_ONCE):
                    n, rows, prows = _dil_rows(it * PAIRS_AT_ONCE + u, d, nb)
                    kc = k_ref[rows, :].astype(BF16)
                    vc = v_ref[rows, :].astype(BF16)
                    if single:
                        kk, vv, bias = kc, vc, bias1[...]
                    else:
                        kk = jnp.concatenate([k_ref[prows, :].astype(BF16), kc], axis=0)
                        vv = jnp.concatenate([v_ref[prows, :].astype(BF16), vc], axis=0)
                        bias = bias2[jnp.minimum(n, 1)]
                    items.append((q_ref[rows, :].astype(BF16), kk, vv, dy_ref[rows, :], y_ref[rows, :], lse_ref[rows, :], bias, MHA))
                    places.append((rows, prows))
                for (dq, dk, dv, _), (rows, prows) in zip(_pairs_bwd(items), places):
                    if single:
                        dq_ref[rows, :] = dq
                        dk_ref[rows, :] = dk
                        dv_ref[rows, :] = dv
                    else:
                        dq_ref[rows, :] += dq
                        dk_ref[prows, :] += dk[:ATT_BLOCK]
                        dv_ref[prows, :] += dv[:ATT_BLOCK]
                        dk_ref[rows, :] += dk[ATT_BLOCK:]
                        dv_ref[rows, :] += dv[ATT_BLOCK:]
                return carry

            lax.fori_loop(0, d * nb // PAIRS_AT_ONCE, blocks, 0)

    slab = lambda off: pl.BlockSpec((None, s_len, LANES), functools.partial(lambda o, b, h: (b, 0, o + h), off))
    sd = jax.ShapeDtypeStruct((bsz, s_len, D_MODEL), F32)
    return call_with_exchange(
        body, carried,
        out_shape=(sd, sd, sd),
        grid=(bsz, npair),
        in_specs=[slab(0), slab(npair), slab(2 * npair), slab(0), slab(0), slab(0)],
        out_specs=(slab(0), slab(0), slab(0)),
        scratch_shapes=[pltpu.VMEM((2, 2 * ATT_BLOCK, 2 * ATT_BLOCK), F32), pltpu.VMEM((2 * ATT_BLOCK, ATT_BLOCK), F32)],
        name=name,
        args=(qkv, qkv, qkv, y, lse, dy))


def adamw(w, g, m, v, *, name):
    rows, cols = w.shape
    tr = _pick(rows, (256, 128, 64, 32, 16, 8))

    def body(w_ref, g_ref, m_ref, v_ref, d_ref, nm_ref, nv_ref):
        gv = g_ref[...]
        nm = ADAM_B1 * m_ref[...] + (1.0 - ADAM_B1) * gv
        nv = ADAM_B2 * v_ref[...] + (1.0 - ADAM_B2) * (gv * gv)
        m_hat = nm / (1.0 - ADAM_B1 ** ADAM_STEP)
        v_hat = nv / (1.0 - ADAM_B2 ** ADAM_STEP)
        d_ref[...] = -ADAM_LR * (m_hat / (jnp.sqrt(v_hat) + ADAM_EPS) + ADAM_WD * w_ref[...])
        nm_ref[...] = nm
        nv_ref[...] = nv

    row = pl.BlockSpec((tr, cols), lambda i: (i, 0))
    return pl.pallas_call(
        body,
        out_shape=(jax.ShapeDtypeStruct((rows, cols), F32),) * 3,
        grid=(rows // tr,),
        in_specs=[row] * 4,
        out_specs=(row, row, row),
        compiler_params=_params(("parallel",)),
        name=name,
    )(w, g, m, v)


def _place():
    return lax.axis_index("x"), lax.axis_index("y"), lax.axis_index("c")


def _gather_copies(x_ref, out_ref, send_sems, recv_sems):
    x, y, c = _place()
    me, sibling = (x, y, c), (x, y, 1 - c)
    chips = [(1 - x, y), (x, 1 - y), (1 - x, 1 - y)]

    def slot(px, py, pc):
        return out_ref.at[4 * px + 2 * py + pc]

    def copy(k, block, to, src=None):
        return pltpu.make_async_remote_copy(
            src_ref=slot(*block) if src is None else src, dst_ref=slot(*block),
            send_sem=send_sems.at[k], recv_sem=recv_sems.at[k], device_id=to, device_id_type=MESH)

    first = [lambda: copy(0, me, sibling, src=x_ref)] + [functools.partial(copy, 1 + j, me, (*chip, c), src=x_ref)
                                                         for j, chip in enumerate(chips)]
    passed = [functools.partial(copy, 4 + j, (*chip, c), sibling) for j, chip in enumerate(chips)]
    landing = [functools.partial(copy, 1 + j, (*chip, c), me) for j, chip in enumerate(chips)]
    from_sibling = [lambda: copy(0, sibling, me)] + [functools.partial(copy, 4 + j, (*chip, 1 - c), me) for j, chip in enumerate(chips)]
    return slot(*me), first, passed, landing, from_sibling


def _gather_start(x_ref, out_ref, send_sems, recv_sems, local_sem):
    mine, first, _, _, _ = _gather_copies(x_ref, out_ref, send_sems, recv_sems)
    pltpu.make_async_copy(x_ref, mine, local_sem).start()
    for cp in first:
        cp().start()


def _gather_finish(x_ref, out_ref, send_sems, recv_sems, local_sem):
    mine, first, passed, landing, from_sibling = _gather_copies(x_ref, out_ref, send_sems, recv_sems)
    for cp, fwd in zip(landing, passed):
        cp().wait_recv()
        fwd().start()
    for cp in from_sibling:
        cp().wait_recv()
    for cp in first + passed:
        cp().wait_send()
    pltpu.make_async_copy(x_ref, mine, local_sem).wait()


def _a2a_copies(x_ref, out_ref, send_sems, recv_sems):
    x, y, c = _place()
    me = 4 * x + 2 * y + c
    copies = []
    for k in range(1, N_DEV):
        px = 1 - x if k & 4 else x
        py = 1 - y if k & 2 else y
        pc = 1 - c if k & 1 else c
        copies.append(pltpu.make_async_remote_copy(
            src_ref=x_ref.at[4 * px + 2 * py + pc], dst_ref=out_ref.at[me], send_sem=send_sems.at[k - 1],
            recv_sem=recv_sems.at[k - 1], device_id=(px, py, pc), device_id_type=MESH))
    return me, copies


def _a2a_start(x_ref, out_ref, send_sems, recv_sems, local_sem):
    me, copies = _a2a_copies(x_ref, out_ref, send_sems, recv_sems)
    pltpu.make_async_copy(x_ref.at[me], out_ref.at[me], local_sem).start()
    for cp in copies:
        cp.start()


def _a2a_finish(x_ref, out_ref, send_sems, recv_sems, local_sem):
    me, copies = _a2a_copies(x_ref, out_ref, send_sems, recv_sems)
    for cp in copies:
        cp.wait_recv()
    for cp in copies:
        cp.wait_send()
    pltpu.make_async_copy(x_ref.at[me], out_ref.at[me], local_sem).wait()


EXCHANGES = {"gather": (_gather_start, _gather_finish, lambda x: (N_DEV,) + x.shape),
             "a2a": (_a2a_start, _a2a_finish, lambda x: x.shape)}
EXCHANGE_SEMS = [pltpu.SemaphoreType.DMA((7,)), pltpu.SemaphoreType.DMA((7,)), pltpu.SemaphoreType.DMA(())]


def exchange(kind, x, *, name):
    start, finish, shape = EXCHANGES[kind]

    def body(x_ref, out_ref, *sems):
        start(x_ref, out_ref, *sems)
        finish(x_ref, out_ref, *sems)

    return pl.pallas_call(
        body,
        out_shape=jax.ShapeDtypeStruct(shape(x), x.dtype),
        in_specs=[pl.BlockSpec(memory_space=pl.ANY)],
        out_specs=pl.BlockSpec(memory_space=pl.ANY),
        scratch_shapes=EXCHANGE_SEMS,
        name=name,
    )(x)


def call_with_exchange(body, carried, *, out_shape, grid, in_specs, out_specs, scratch_shapes, name, args):
    sem = ("arbitrary",) * len(grid)
    carried = list(carried or ())
    if not carried:
        res = pl.pallas_call(body, out_shape=out_shape, grid=grid, in_specs=in_specs, out_specs=out_specs,
                             scratch_shapes=scratch_shapes, compiler_params=_params(sem), name=name)(*args)
        return res, []
    n_in, n_out, n_scr, n_x = len(in_specs), len(out_shape), len(scratch_shapes), len(carried)
    n_sems = len(EXCHANGE_SEMS)

    def wrapped(*refs):
        ins, x_refs = refs[:n_in], refs[n_in:n_in + n_x]
        outs = refs[n_in + n_x:n_in + n_x + n_out]
        out_refs = refs[n_in + n_x + n_out:n_in + 2 * n_x + n_out]
        rest = refs[n_in + 2 * n_x + n_out:]
        scratch, sems = rest[:n_scr], rest[n_scr:]
        ids = [pl.program_id(i) for i in range(len(grid))]
        is_first = functools.reduce(lambda a, b: a & b, [i == 0 for i in ids])
        is_last = functools.reduce(lambda a, b: a & b, [i == g - 1 for i, g in zip(ids, grid)])

        @pl.when(is_first)
        def _():
            for e, (kind, _) in enumerate(carried):
                EXCHANGES[kind][0](x_refs[e], out_refs[e], *sems[e * n_sems:(e + 1) * n_sems])

        body(*ins, *outs, *scratch)

        @pl.when(is_last)
        def _():
            for e, (kind, _) in enumerate(carried):
                EXCHANGES[kind][1](x_refs[e], out_refs[e], *sems[e * n_sems:(e + 1) * n_sems])

    any_spec = pl.BlockSpec(memory_space=pl.ANY)
    res = pl.pallas_call(
        wrapped,
        out_shape=tuple(out_shape) + tuple(jax.ShapeDtypeStruct(EXCHANGES[kind][2](x), x.dtype) for kind, x in carried),
        grid=grid,
        in_specs=list(in_specs) + [any_spec] * n_x,
        out_specs=tuple(out_specs) + (any_spec,) * n_x,
        scratch_shapes=list(scratch_shapes) + EXCHANGE_SEMS * n_x,
        compiler_params=_params(sem),
        name=name + "".join("_" + kind for kind, _ in carried),
    )(*args, *[x for _, x in carried])
    return res[:n_out], list(res[n_out:])


def sum_slots(x, *, name):
    _, rows, cols = x.shape
    tr = _pick(rows, (512, 256, 128, 64, 32, 16))

    def body(x_ref, o_ref):
        acc = x_ref[0].astype(F32)
        for k in range(1, N_DEV):
            acc = acc + x_ref[k].astype(F32)
        o_ref[...] = acc

    return pl.pallas_call(
        body,
        out_shape=jax.ShapeDtypeStruct((rows, cols), F32),
        grid=(rows // tr,),
        in_specs=[pl.BlockSpec((N_DEV, tr, cols), lambda i: (0, i, 0))],
        out_specs=pl.BlockSpec((tr, cols), lambda i: (i, 0)),
        compiler_params=_params(("parallel",)),
        name=name,
    )(x)


BIG = ("w_in", "w_branch", "w_out", "w_ffn_in", "w_ffn_out")
SMALL = ("conv_b", "w_rg", "b_rg", "w_ig", "b_ig", "lru_lambda", "sinks", "ln1_g", "ln1_b", "ln2_g", "ln2_b")
N_LRU_BLOCKS = D_MODEL // HEAD_DIM
SMALL_ROWS_TILE = 512


def _block_diag(w):
    z = jnp.zeros((N_LRU_BLOCKS // 2, HEAD_DIM, HEAD_DIM), w.dtype)
    top = jnp.concatenate([w[0::2], z], axis=2)
    bot = jnp.concatenate([z, w[1::2]], axis=2)
    return jnp.concatenate([top, bot], axis=1)


def _block_diag_grad(g):
    return jnp.stack([g[:, :HEAD_DIM, :HEAD_DIM], g[:, HEAD_DIM:, HEAD_DIM:]], axis=1).reshape(N_LRU_BLOCKS, HEAD_DIM, HEAD_DIM)


def layer_fwd(x, xb, p, bsz, own_late=None, next_w_in=None):
    t_dim = x.shape[0]
    s_len = t_dim // bsz
    w_f, w_qs, w_qd = p["w_in_f"], p["w_in_qs"], p["w_in_qd"]
    proj_f = matmul(xb, w_f, name="proj_f")
    qs = matmul(xb, w_qs, out_dtype=BF16, name="proj_qs").reshape(bsz, s_len, W_QS)
    qd = matmul(xb, w_qd, name="proj_qd").reshape(bsz, s_len, W_QD)
    proj_f3 = proj_f.reshape(bsz, s_len, W_F)
    wr_bd, wi_bd = _block_diag(p["w_rg"]), _block_diag(p["w_ig"])
    (y_a, h), got_rows = lru_fwd(proj_f3, p["conv_w"], p["conv_b"], wr_bd, wi_bd, p["b_rg"], p["b_ig"], p["lru_lambda"],
                                 name="lru_fwd", carried=[("gather", own_late[1])] if own_late is not None else [])
    (y_b, lse_b, y_bb), got_fi = swa_fwd(qs, p["sinks"], name="swa_fwd", carried=[("gather", own_late[0])] if own_late is not None else [])
    (y_c, lse_c, y_cb), got_next = dil_fwd(qd, name="dil_fwd", carried=[("gather", next_w_in)] if next_w_in is not None else [])
    if own_late is not None:
        p = {**p, **_late_weights(got_fi[0], got_rows[0])}
    ys = [t.reshape(t_dim, D_MODEL) for t in (y_a, y_bb, y_cb)]
    merged, br = branch_merge(ys, p["w_branch"], proj_f, name="branch_merge")
    x1, x1b, z1 = ln_fwd(x, merged, p["w_out"], p["ln1_g"], p["ln1_b"], name="w_out_ln")
    h1, h3, act = ffn_in_swiglu(x1b, p["w_ffn_in"], name="ffn_in_swiglu")
    x2, x2b, z2 = ln_fwd(x1, act, p["w_ffn_out"], p["ln2_g"], p["ln2_b"], name="ffn_out_ln")
    saved = dict(xb=xb, proj_f=proj_f, qs=qs, qd=qd, h=h, ys=ys, y_b=y_b, y_c=y_c, lse_b=lse_b, lse_c=lse_c, br=br, merged=merged,
                 z1=z1, x1b=x1b, h1=h1, h3=h3, act=act, z2=z2, wr_bd=wr_bd, wi_bd=wi_bd, p=p)
    return x2, x2b, saved, (got_next[0] if got_next else None)


def layer_bwd(dx2, s, bsz, exchange_own=False, above_w_in=None, pending=None, defer_last=False):
    p = s["p"]
    t_dim = dx2.shape[0]
    s_len = t_dim // bsz
    g = {}
    if pending is None:
        dz2, dz2b, g["ln2_g"], g["ln2_b"] = ln_bwd(dx2, s["z2"], p["ln2_g"], name="ln2_bwd")
    else:
        dz2, dz2b, g["ln2_g"], g["ln2_b"] = ln_bwd(dx2, s["z2"], p["ln2_g"], a=pending[0], w=pending[1], name="dx_qd_ln2_bwd")
    dh13 = swiglu_bwd(dz2b, p["w_ffn_out"], s["h1"], s["h3"], name="swiglu_bwd")
    g["w_ffn_out"] = matmul(s["act"], dz2b, trans_a=True, out_dtype=BF16, name="dw_ffn_out")
    g["w_ffn_in"] = matmul(s["x1b"], dh13, trans_a=True, out_dtype=BF16, name="dw_ffn_in")
    dz1, dz1b, g["ln1_g"], g["ln1_b"] = ln_bwd(dz2, s["z1"], p["ln1_g"], a=dh13, w=p["w_ffn_in"], dy_scale=ALPHA, name="dx_ffn_ln1_bwd")
    g["w_out"] = matmul(s["merged"], dz1b, trans_a=True, out_dtype=BF16, name="dw_out")
    *dbr, dgates = merge_bwd(dz1b, p["w_out"], s["proj_f"], s["br"], name="merge_bwd")
    dys = [matmul(dbr[n], p["w_branch"][n], trans_b=True, out_dtype=F32 if n == 2 else BF16, name="d_branch") for n in range(3)]
    g["w_branch"] = jnp.stack([matmul(s["ys"][n], dbr[n], trans_a=True, out_dtype=BF16, name="dw_branch") for n in range(3)])
    fi_slots, rows_slots = _late_slots(g) if exchange_own else (None, None)
    shape3 = (bsz, s_len, D_MODEL)
    (dlx, dlg, g["conv_w"], g["conv_b"], g["b_rg"], g["b_ig"], g["lru_lambda"], dwr, dwi), got_rows = lru_bwd(
        dys[0].reshape(shape3), s["proj_f"].reshape(bsz, s_len, W_F), s["h"], p["conv_w"], p["conv_b"], s["wr_bd"], s["wi_bd"],
        jnp.swapaxes(s["wr_bd"], 1, 2), jnp.swapaxes(s["wi_bd"], 1, 2), p["b_rg"], p["b_ig"], p["lru_lambda"], name="lru_bwd",
        carried=[("a2a", rows_slots)] if exchange_own else [])
    g["w_rg"], g["w_ig"] = _block_diag_grad(dwr), _block_diag_grad(dwi)
    dy_b3 = dys[1].reshape(shape3)
    (*dqs, dsinks), got_fi = swa_bwd(s["qs"], p["sinks"], s["y_b"], s["lse_b"], dy_b3, name="swa_bwd",
                                     carried=[("a2a", fi_slots)] if exchange_own else [])
    g["sinks"] = dsinks[0, :N_HEADS]
    dqd, got_in = dil_bwd(s["qd"], s["y_c"], s["lse_c"], dys[2].reshape(shape3), name="dil_bwd",
                          carried=[("a2a", above_w_in)] if above_w_in is not None else [])
    flat = lambda t: t.reshape(t_dim, t.shape[-1])
    dproj_f = jnp.concatenate([flat(dlx), flat(dlg), dgates], axis=1)
    dproj_qs = jnp.concatenate([flat(t) for t in dqs], axis=1).astype(BF16)
    dproj_qd = jnp.concatenate([flat(t) for t in dqd], axis=1).astype(BF16)
    g["w_in_f"] = matmul(s["xb"], dproj_f, trans_a=True, out_dtype=BF16, name="dw_in_f")
    g["w_in_qs"] = matmul(s["xb"], dproj_qs, trans_a=True, out_dtype=BF16, name="dw_in_qs")
    g["w_in_qd"] = matmul(s["xb"], dproj_qd, trans_a=True, out_dtype=BF16, name="dw_in_qd")
    dx = matmul(dproj_f, p["w_in_f"], trans_b=True, add=dz1, add_scale=ALPHA, name="dx_f")
    dx = matmul(dproj_qs, p["w_in_qs"], trans_b=True, add=dx, name="dx_qs")
    left = (dproj_qd, p["w_in_qd"]) if defer_last else None
    if not defer_last:
        dx = matmul(dproj_qd, p["w_in_qd"], trans_b=True, add=dx, name="dx_qd")
    g = {k: (v.reshape(p[k].shape) if k in p else v) for k, v in g.items()}
    return dx, g, dict(late=(got_fi[0], got_rows[0]) if exchange_own else None, w_in=got_in[0] if got_in else None), left


def local_step(x, target, layer_params, layer_shards=None, first_w_in=None):
    bsz, s_len, d = x.shape
    t_dim = bsz * s_len
    xf = x.reshape(t_dim, d)
    xb = xf.astype(BF16)
    exchanging = layer_shards is not None
    saved, gathered = [], first_w_in
    for l in range(DEPTH):
        p = layer_params(l, gathered)
        xf, xb, s, gathered = layer_fwd(xf, xb, p, bsz, own_late=layer_shards[l][1:] if exchanging else None,
                                        next_w_in=layer_shards[l + 1][0] if exchanging and l + 1 < DEPTH else None)
        saved.append(s)
    dy, sq = loss_head(xf, target.reshape(t_dim, d), name="loss_head")
    loss = 0.5 * jnp.sum(sq) / d
    grads, received, w_in_slots, pending = [None] * DEPTH, [[None] * 3 for _ in range(DEPTH)], None, None
    for l in reversed(range(DEPTH)):
        dy, grads[l], got, pending = layer_bwd(dy, saved[l], bsz, exchange_own=exchanging, above_w_in=w_in_slots,
                                               pending=pending, defer_last=l > 0)
        if got["w_in"] is not None:
            received[l + 1][0] = got["w_in"]
        if exchanging:
            received[l][1:] = got["late"]
            w_in_slots = _w_in_slots(grads[l])
    return loss, dy.reshape(bsz, s_len, d), grads, received, w_in_slots


W_IN_SEGMENTS = (("w_in_f", 0, 0, 2 * D_MODEL), ("w_in_qs", 0, 2 * D_MODEL, W_QS), ("w_in_qd", 0, 2 * D_MODEL + W_QS, W_QD),
                 ("w_in_f", 2 * D_MODEL, 2 * D_MODEL + W_QS + W_QD, 3 * D_MODEL))
ROW_SHARDED = ("w_branch", "w_out", "w_ffn_out")


def _cols_of_shards(shards, lo, hi):
    width = shards[0].shape[-1]
    parts = []
    for k, sh in enumerate(shards):
        a, b = max(lo, k * width), min(hi, (k + 1) * width)
        if a < b:
            parts.append(sh[..., a - k * width:b - k * width])
    return parts[0] if len(parts) == 1 else jnp.concatenate(parts, axis=-1)


def _cols_of_w_in(pieces, lo, hi):
    parts = []
    for name, p0, l0, width in W_IN_SEGMENTS:
        a, b = max(lo, l0), min(hi, l0 + width)
        if a < b:
            parts.append(pieces[name][..., p0 + a - l0:p0 + b - l0])
    return parts[0] if len(parts) == 1 else jnp.concatenate(parts, axis=-1)


W_IN_COLS = W_F + W_QS + W_QD


def _layer_shards(w):
    rows = jnp.concatenate([w[k].reshape(DEPTH, -1, D_MODEL) for k in ROW_SHARDED], axis=1).astype(BF16)
    w_in, w_fi = w["w_in"].astype(BF16), w["w_ffn_in"].astype(BF16)
    return [(w_in[l], w_fi[l], rows[l]) for l in range(DEPTH)]


ROW_COUNTS = (3 * D_MODEL // N_DEV, D_MODEL // N_DEV, FF_HIDDEN // N_DEV)


def _w_in_weights(g_in):
    sh = [g_in[k] for k in range(N_DEV)]
    return dict(w_in_f=jnp.concatenate([_cols_of_shards(sh, 0, 2 * D_MODEL), _cols_of_shards(sh, W_IN_COLS - 3 * D_MODEL, W_IN_COLS)], axis=-1),
                w_in_qs=_cols_of_shards(sh, 2 * D_MODEL, 2 * D_MODEL + W_QS),
                w_in_qd=_cols_of_shards(sh, 2 * D_MODEL + W_QS, 2 * D_MODEL + W_QS + W_QD))


def _late_weights(g_fi, g_rows):
    p = dict(w_ffn_in=jnp.concatenate([g_fi[k] for k in range(N_DEV)], axis=-1))
    off = 0
    for k, n in zip(ROW_SHARDED, ROW_COUNTS):
        t = g_rows[:, off:off + n]
        if k == "w_branch":
            p[k] = jnp.transpose(t.reshape(N_DEV, 3, n // 3, D_MODEL), (1, 0, 2, 3)).reshape(3, -1, D_MODEL)
        else:
            p[k] = t.reshape(-1, D_MODEL)
        off += n
    return p


def _w_in_slots(g):
    shard = W_IN_COLS // N_DEV
    return jnp.stack([_cols_of_w_in(g, k * shard, (k + 1) * shard) for k in range(N_DEV)]).astype(BF16)


def _late_slots(g):
    shard = g["w_ffn_in"].shape[-1] // N_DEV
    s_fi = jnp.stack([g["w_ffn_in"][:, k * shard:(k + 1) * shard] for k in range(N_DEV)]).astype(BF16)
    rows = jnp.concatenate([jnp.transpose(g["w_branch"].reshape(3, N_DEV, -1, D_MODEL), (1, 0, 2, 3)).reshape(N_DEV, -1, D_MODEL),
                            g["w_out"].reshape(N_DEV, -1, D_MODEL), g["w_ffn_out"].reshape(N_DEV, -1, D_MODEL)], axis=1).astype(BF16)
    return s_fi, rows


def _pad_rows(flat, tile_rows):
    n = flat.shape[0]
    per = tile_rows * LANES
    total = -(-n // per) * per
    return jnp.pad(flat, (0, total - n)).reshape(-1, LANES)


def kernel(x, w_in, conv_w, conv_b, w_rg, b_rg, w_ig, b_ig, lru_lambda, sinks, w_branch, w_out, ln1_g, ln1_b, w_ffn_in, w_ffn_out, ln2_g, ln2_b, loss_target, m_w_in, m_conv_w, m_conv_b, m_w_rg, m_b_rg, m_w_ig, m_b_ig, m_lru_lambda, m_sinks, m_w_branch, m_w_out, m_ln1_g, m_ln1_b, m_w_ffn_in, m_w_ffn_out, m_ln2_g, m_ln2_b, v_w_in, v_conv_w, v_conv_b, v_w_rg, v_b_rg, v_w_ig, v_b_ig, v_lru_lambda, v_sinks, v_w_branch, v_w_out, v_ln1_g, v_ln1_b, v_w_ffn_in, v_w_ffn_out, v_ln2_g, v_ln2_b):
    w = dict(w_in=w_in, conv_w=conv_w, conv_b=conv_b, w_rg=w_rg, b_rg=b_rg, w_ig=w_ig, b_ig=b_ig, lru_lambda=lru_lambda, sinks=sinks,
             w_branch=w_branch, w_out=w_out, ln1_g=ln1_g, ln1_b=ln1_b, w_ffn_in=w_ffn_in, w_ffn_out=w_ffn_out, ln2_g=ln2_g, ln2_b=ln2_b)
    m = dict(w_in=m_w_in, conv_w=m_conv_w, conv_b=m_conv_b, w_rg=m_w_rg, b_rg=m_b_rg, w_ig=m_w_ig, b_ig=m_b_ig, lru_lambda=m_lru_lambda,
             sinks=m_sinks, w_branch=m_w_branch, w_out=m_w_out, ln1_g=m_ln1_g, ln1_b=m_ln1_b, w_ffn_in=m_w_ffn_in, w_ffn_out=m_w_ffn_out,
             ln2_g=m_ln2_g, ln2_b=m_ln2_b)
    v = dict(w_in=v_w_in, conv_w=v_conv_w, conv_b=v_conv_b, w_rg=v_w_rg, b_rg=v_b_rg, w_ig=v_w_ig, b_ig=v_b_ig, lru_lambda=v_lru_lambda,
             sinks=v_sinks, w_branch=v_w_branch, w_out=v_w_out, ln1_g=v_ln1_g, ln1_b=v_ln1_b, w_ffn_in=v_w_ffn_in, w_ffn_out=v_w_ffn_out,
             ln2_g=v_ln2_g, ln2_b=v_ln2_b)
    order = ["w_in", "conv_w", "conv_b", "w_rg", "b_rg", "w_ig", "b_ig", "lru_lambda", "sinks", "w_branch", "w_out", "ln1_g", "ln1_b",
             "w_ffn_in", "w_ffn_out", "ln2_g", "ln2_b"]
    me = 4 * lax.axis_index("x") + 2 * lax.axis_index("y") + lax.axis_index("c")

    names = ("w_in", "w_ffn_in", "w_rows")
    shards = _layer_shards(w)
    first_w_in = exchange("gather", shards[0][0], name="gather_w_in")
    cw = exchange("gather", conv_w.reshape(-1, LANES), name="gather_conv_w")
    conv_w_full = jnp.moveaxis(cw.reshape(N_DEV, DEPTH, CONV_WIDTH, LANES), 0, 2).reshape(DEPTH, CONV_WIDTH, D_MODEL)

    def layer_params(l, gathered_w_in):
        return {**_w_in_weights(gathered_w_in), **{k: w[k][l] for k in SMALL}, "conv_w": conv_w_full[l]}

    loss_local, grad_x, grads, received, w_in_slots = local_step(x, loss_target, layer_params, shards, first_w_in)
    loss = lax.psum(loss_local, ("x", "y", "c"))
    received[0][0] = exchange("a2a", w_in_slots, name="exchange_g_w_in")

    sums = [[sum_slots(t, name=f"sum_g_{n}") for t, n in zip(received[l], names)] for l in range(DEPTH)]
    g_final = {"w_in": jnp.stack([sums[l][0] for l in range(DEPTH)]), "w_ffn_in": jnp.stack([sums[l][1] for l in range(DEPTH)])}
    off = 0
    for k, n in zip(ROW_SHARDED, ROW_COUNTS):
        g_final[k] = jnp.stack([sums[l][2][off:off + n] for l in range(DEPTH)]).reshape(w[k].shape)
        off += n
    grads = {k: jnp.stack([grads[l][k] for l in range(DEPTH)]) for k in list(SMALL) + ["conv_w"]}

    small_names = list(SMALL) + ["conv_w"]
    small_sizes = [grads[k].size for k in small_names]
    svec = _pad_rows(jnp.concatenate([grads[k].reshape(-1) for k in small_names]), SMALL_ROWS_TILE)
    ssum = sum_slots(exchange("gather", svec, name="gather_small_grads"), name="sum_small_grads")
    sflat, off = ssum.reshape(-1), 0
    for k, n in zip(small_names, small_sizes):
        g_final[k] = sflat[off:off + n].reshape(grads[k].shape)
        off += n
    g_final["conv_w"] = lax.dynamic_slice_in_dim(g_final["conv_w"], me * LANES, LANES, axis=2)

    delta, new_m, new_v = {}, {}, {}
    for k in list(BIG) + ["conv_w"]:
        cols = w[k].shape[-1]
        two_d = lambda t: t.reshape(-1, cols)
        d_, m_, v_ = adamw(two_d(w[k]), two_d(g_final[k]), two_d(m[k]), two_d(v[k]), name=f"adamw_{k}")
        delta[k], new_m[k], new_v[k] = d_.reshape(w[k].shape), m_.reshape(w[k].shape), v_.reshape(w[k].shape)
    pack_small = lambda dct: _pad_rows(jnp.concatenate([dct[k].reshape(-1) for k in SMALL]), SMALL_ROWS_TILE)
    d_, m_, v_ = adamw(pack_small(w), pack_small(g_final), pack_small(m), pack_small(v), name="adamw_small")
    off = 0
    for k in SMALL:
        n = w[k].size
        for dst, src in ((delta, d_), (new_m, m_), (new_v, v_)):
            dst[k] = src.reshape(-1)[off:off + n].reshape(w[k].shape)
        off += n
    return (loss, grad_x, *[g_final[k] for k in order], *[delta[k] for k in order], *[new_m[k] for k in order], *[new_v[k] for k in order])
```

```python
import functools
import math

import jax
import jax.numpy as jnp
from jax import lax
from jax.experimental import pallas as pl
from jax.experimental.pallas import tpu as pltpu

F32 = jnp.float32
BF16 = jnp.bfloat16

N_DEV = 8
DEPTH = 4
D_MODEL = 1024
HEAD_DIM = 64
LANES = 128
N_HEADS = D_MODEL // HEAD_DIM
SWA_KV_HEADS = 4
ATT_BLOCK = 128
DILATIONS = (1, 4, 16)
CONV_WIDTH = 4
LRU_C = 8.0
FF_HIDDEN = 2816
ALPHA = (2.0 * DEPTH) ** 0.25
LN_EPS = 1e-5
NEG_INF = -1e30
W_F = 5 * D_MODEL
W_QS = D_MODEL + 2 * SWA_KV_HEADS * HEAD_DIM
W_QD = 3 * D_MODEL

ADAM_LR = 0.001
ADAM_B1 = 0.9
ADAM_B2 = 0.999
ADAM_EPS = 1e-08
ADAM_WD = 0.01
ADAM_STEP = 10

VMEM_LIMIT = 56 * 1024 * 1024
MATMUL_BLOCK_BYTES = 40 * 1024 * 1024
MESH = pl.DeviceIdType.MESH


def _pick(n, cands):
    for c in cands:
        if n % c == 0:
            return c
    raise ValueError(f"no tile for {n} among {cands}")


def _params(sem):
    return pltpu.CompilerParams(dimension_semantics=sem, vmem_limit_bytes=VMEM_LIMIT)


def _tile(n, cap):
    best = None
    for t in range(LANES, cap + 1, LANES):
        if n % t == 0:
            best = t
    assert best is not None, (n, cap)
    return best


def matmul(a, b, *, name, trans_a=False, trans_b=False, out_dtype=F32, add=None, add_scale=1.0):
    if trans_a:
        k_dim, m_dim = a.shape
    else:
        m_dim, k_dim = a.shape
    n_dim = b.shape[0] if trans_b else b.shape[1]
    assert (b.shape[1] if trans_b else b.shape[0]) == k_dim
    tn = _tile(n_dim, 1408)
    tm, tk = _tile(m_dim, 1024), _tile(k_dim, 1408)
    for cand in (1024, 512, 256):
        ctm = _tile(m_dim, cand)
        blocks = 2 * (ctm * k_dim * a.dtype.itemsize + tn * k_dim * b.dtype.itemsize + ctm * tn * jnp.dtype(out_dtype).itemsize
                      + (ctm * tn * add.dtype.itemsize if add is not None else 0))
        if blocks <= MATMUL_BLOCK_BYTES:
            tm, tk = ctm, k_dim
            break
    nk = k_dim // tk
    dims = (((0 if trans_a else 1,), (1 if trans_b else 0,)), ((), ()))

    def body(*refs):
        if add is None:
            a_ref, b_ref, o_ref, acc_ref = refs
            add_ref = None
        else:
            a_ref, b_ref, add_ref, o_ref, acc_ref = refs
        k = pl.program_id(2)
        part = lax.dot_general(a_ref[...].astype(BF16), b_ref[...].astype(BF16), dims, preferred_element_type=F32)

        def finish(r):
            if add_ref is not None:
                r = r + add_scale * add_ref[...].astype(F32)
            o_ref[...] = r.astype(out_dtype)

        if nk == 1:
            finish(part)
        else:
            @pl.when(k == 0)
            def _():
                acc_ref[...] = part

            @pl.when((k > 0) & (k < nk - 1))
            def _():
                acc_ref[...] += part

            @pl.when(k == nk - 1)
            def _():
                finish(acc_ref[...] + part)

    a_spec = pl.BlockSpec((tk, tm), lambda i, j, k: (k, i)) if trans_a else pl.BlockSpec((tm, tk), lambda i, j, k: (i, k))
    b_spec = pl.BlockSpec((tn, tk), lambda i, j, k: (j, k)) if trans_b else pl.BlockSpec((tk, tn), lambda i, j, k: (k, j))
    in_specs = [a_spec, b_spec]
    args = [a, b]
    if add is not None:
        in_specs.append(pl.BlockSpec((tm, tn), lambda i, j, k: (i, j)))
        args.append(add)
    return pl.pallas_call(
        body,
        out_shape=jax.ShapeDtypeStruct((m_dim, n_dim), out_dtype),
        grid=(m_dim // tm, n_dim // tn, nk),
        in_specs=in_specs,
        out_specs=pl.BlockSpec((tm, tn), lambda i, j, k: (i, j)),
        scratch_shapes=[pltpu.VMEM((tm, tn) if nk > 1 else (8, LANES), F32)],
        compiler_params=_params(("parallel", "parallel", "arbitrary")),
        name=name,
    )(*args)


def ln_fwd(x, a, w, g, b, *, name):
    t_dim, d = x.shape
    k_dim = a.shape[1]
    tr = _tile(t_dim, 512)

    def body(x_ref, a_ref, w_ref, g_ref, b_ref, y_ref, yb_ref, z_ref):
        z = ALPHA * x_ref[...] + jnp.dot(a_ref[...], w_ref[...], preferred_element_type=F32)
        mu = jnp.mean(z, axis=-1, keepdims=True)
        zc = z - mu
        var = jnp.mean(zc * zc, axis=-1, keepdims=True)
        y = zc * lax.rsqrt(var + LN_EPS) * g_ref[...] + b_ref[...]
        y_ref[...] = y
        yb_ref[...] = y.astype(BF16)
        z_ref[...] = z

    row = pl.BlockSpec((tr, d), lambda i: (i, 0))
    vec = pl.BlockSpec((1, d), lambda i: (0, 0))
    return pl.pallas_call(
        body,
        out_shape=(jax.ShapeDtypeStruct((t_dim, d), F32), jax.ShapeDtypeStruct((t_dim, d), BF16), jax.ShapeDtypeStruct((t_dim, d), F32)),
        grid=(t_dim // tr,),
        in_specs=[row, pl.BlockSpec((tr, k_dim), lambda i: (i, 0)), pl.BlockSpec((k_dim, d), lambda i: (0, 0)), vec, vec],
        out_specs=(row, row, row),
        compiler_params=_params(("parallel",)),
        name=name,
    )(x, a, w, g.reshape(1, d), b.reshape(1, d))


def ln_bwd(dy, z, g, *, name, a=None, w=None, dy_scale=1.0):
    t_dim, d = dy.shape
    tr = _pick(t_dim, (256, 128, 8))

    def body(*refs):
        if a is None:
            dy_ref, z_ref, g_ref, dz_ref, dzb_ref, dg_ref, db_ref = refs
        else:
            dy_ref, a_ref, w_ref, z_ref, g_ref, dz_ref, dzb_ref, dg_ref, db_ref = refs

        @pl.when(pl.program_id(0) == 0)
        def _():
            dg_ref[...] = jnp.zeros_like(dg_ref)
            db_ref[...] = jnp.zeros_like(db_ref)

        z = z_ref[...]
        dyv = dy_scale * dy_ref[...]
        if a is not None:
            dyv = dyv + lax.dot_general(a_ref[...], w_ref[...], NT, preferred_element_type=F32)
        mu = jnp.mean(z, axis=-1, keepdims=True)
        zc = z - mu
        var = jnp.mean(zc * zc, axis=-1, keepdims=True)
        rstd = lax.rsqrt(var + LN_EPS)
        xhat = zc * rstd
        dxhat = dyv * g_ref[...]
        m1 = jnp.mean(dxhat, axis=-1, keepdims=True)
        m2 = jnp.mean(dxhat * xhat, axis=-1, keepdims=True)
        dz = rstd * (dxhat - m1 - xhat * m2)
        dz_ref[...] = dz
        dzb_ref[...] = dz.astype(BF16)
        dg_ref[...] += jnp.sum(dyv * xhat, axis=0, keepdims=True)
        db_ref[...] += jnp.sum(dyv, axis=0, keepdims=True)

    row = pl.BlockSpec((tr, d), lambda i: (i, 0))
    vec = pl.BlockSpec((1, d), lambda i: (0, 0))
    in_specs, args = [row], [dy]
    if a is not None:
        k_dim = a.shape[1]
        in_specs += [pl.BlockSpec((tr, k_dim), lambda i: (i, 0)), pl.BlockSpec((d, k_dim), lambda i: (0, 0))]
        args += [a, w]
    return pl.pallas_call(
        body,
        out_shape=(jax.ShapeDtypeStruct((t_dim, d), F32), jax.ShapeDtypeStruct((t_dim, d), BF16),
                   jax.ShapeDtypeStruct((1, d), F32), jax.ShapeDtypeStruct((1, d), F32)),
        grid=(t_dim // tr,),
        in_specs=in_specs + [row, vec],
        out_specs=(row, row, vec, vec),
        compiler_params=_params(("arbitrary",)),
        name=name,
    )(*args, z, g.reshape(1, d))


def loss_head(y, target, *, name):
    t_dim, d = y.shape
    tr = _pick(t_dim, (256, 128, 8))

    def body(y_ref, t_ref, dy_ref, sq_ref):
        @pl.when(pl.program_id(0) == 0)
        def _():
            sq_ref[...] = jnp.zeros_like(sq_ref)

        diff = y_ref[...] - t_ref[...]
        dy_ref[...] = diff / d
        sq_ref[...] += jnp.sum(diff * diff, axis=0, keepdims=True)

    row = pl.BlockSpec((tr, d), lambda i: (i, 0))
    vec = pl.BlockSpec((1, d), lambda i: (0, 0))
    return pl.pallas_call(
        body,
        out_shape=(jax.ShapeDtypeStruct((t_dim, d), F32), jax.ShapeDtypeStruct((1, d), F32)),
        grid=(t_dim // tr,),
        in_specs=[row, row],
        out_specs=(row, vec),
        compiler_params=_params(("arbitrary",)),
        name=name,
    )(y, target)


def _sigmoid(x):
    return 0.5 * jnp.tanh(0.5 * x) + 0.5


def ffn_in_swiglu(x, w, *, name):
    t_dim, d = x.shape
    f = w.shape[1] // 2
    tm, tn = _tile(t_dim, 1024), _tile(f, 1408)
    nf = f // tn

    def body(x_ref, w1_ref, w3_ref, h1_ref, h3_ref, act_ref):
        xv = x_ref[...]
        h1 = jnp.dot(xv, w1_ref[...], preferred_element_type=F32)
        h3 = jnp.dot(xv, w3_ref[...], preferred_element_type=F32)
        h1_ref[...] = h1.astype(BF16)
        h3_ref[...] = h3.astype(BF16)
        act_ref[...] = (h1 * _sigmoid(h1) * h3).astype(BF16)

    out = pl.BlockSpec((tm, tn), lambda i, j: (i, j))
    return pl.pallas_call(
        body,
        out_shape=(jax.ShapeDtypeStruct((t_dim, f), BF16),) * 3,
        grid=(t_dim // tm, nf),
        in_specs=[pl.BlockSpec((tm, d), lambda i, j: (i, 0)), pl.BlockSpec((d, tn), lambda i, j: (0, j)),
                  pl.BlockSpec((d, tn), lambda i, j: (0, j + nf))],
        out_specs=(out, out, out),
        compiler_params=_params(("parallel", "parallel")),
        name=name,
    )(x, w, w)


def swiglu_bwd(dz, w_ffn_out, h1, h3, *, name):
    t_dim, d = dz.shape
    f = h1.shape[1]
    tr = _tile(t_dim, 512)

    def body(dz_ref, w_ref, h1_ref, h3_ref, dh_ref):
        da = lax.dot_general(dz_ref[...], w_ref[...], NT, preferred_element_type=F32)
        h1 = h1_ref[...].astype(F32)
        sg = _sigmoid(h1)
        dh_ref[:, :f] = (da * h3_ref[...].astype(F32) * sg * (1.0 + h1 * (1.0 - sg))).astype(BF16)
        dh_ref[:, f:] = (da * h1 * sg).astype(BF16)

    wide = pl.BlockSpec((tr, f), lambda i: (i, 0))
    return pl.pallas_call(
        body,
        out_shape=jax.ShapeDtypeStruct((t_dim, 2 * f), BF16),
        grid=(t_dim // tr,),
        in_specs=[pl.BlockSpec((tr, d), lambda i: (i, 0)), pl.BlockSpec((f, d), lambda i: (0, 0)), wide, wide],
        out_specs=pl.BlockSpec((tr, 2 * f), lambda i: (i, 0)),
        compiler_params=_params(("parallel",)),
        name=name,
    )(dz, w_ffn_out, h1, h3)


def branch_merge(ys, w_branch, proj_f, *, name):
    t_dim, d = ys[0].shape
    tm = _tile(t_dim, 512)

    def body(y0, y1, y2, w_ref, g0, g1, g2, m_ref, b0, b1, b2):
        acc = None
        for n, (y, g, b) in enumerate(((y0, g0, b0), (y1, g1, b1), (y2, g2, b2))):
            br = jnp.dot(y[...], w_ref[n], preferred_element_type=F32)
            b[...] = br.astype(BF16)
            t = _sigmoid(g[...]) * br
            acc = t if acc is None else acc + t
        m_ref[...] = acc.astype(BF16)

    row = pl.BlockSpec((tm, d), lambda i: (i, 0))
    gate = [pl.BlockSpec((tm, d), functools.partial(lambda n, i: (i, 2 + n), n)) for n in range(3)]
    merged, *br = pl.pallas_call(
        body,
        out_shape=(jax.ShapeDtypeStruct((t_dim, d), BF16),) * 4,
        grid=(t_dim // tm,),
        in_specs=[row, row, row, pl.BlockSpec((3, d, d), lambda i: (0, 0, 0))] + gate,
        out_specs=(row, row, row, row),
        compiler_params=_params(("parallel",)),
        name=name,
    )(*ys, w_branch, proj_f, proj_f, proj_f)
    return merged, br


def merge_bwd(dz, w_out, proj_f, br, *, name):
    t_dim, d = dz.shape
    tr = _tile(t_dim, 512)

    def body(dz_ref, w_ref, g0, g1, g2, b0, b1, b2, d0, d1, d2, dg_ref):
        dm = lax.dot_general(dz_ref[...], w_ref[...], NT, preferred_element_type=F32)
        for n, (g, b, o) in enumerate(((g0, b0, d0), (g1, b1, d1), (g2, b2, d2))):
            sg = _sigmoid(g[...])
            o[...] = (dm * sg).astype(BF16)
            dg_ref[:, n * d:(n + 1) * d] = (dm * b[...].astype(F32) * sg * (1.0 - sg)).astype(BF16)

    row = pl.BlockSpec((tr, d), lambda i: (i, 0))
    gate = [pl.BlockSpec((tr, d), functools.partial(lambda n, i: (i, 2 + n), n)) for n in range(3)]
    return pl.pallas_call(
        body,
        out_shape=(jax.ShapeDtypeStruct((t_dim, d), BF16),) * 3 + (jax.ShapeDtypeStruct((t_dim, 3 * d), BF16),),
        grid=(t_dim // tr,),
        in_specs=[row, pl.BlockSpec((d, d), lambda i: (0, 0))] + gate + [row, row, row],
        out_specs=(row, row, row, pl.BlockSpec((tr, 3 * d), lambda i: (i, 0))),
        compiler_params=_params(("parallel",)),
        name=name,
    )(dz, w_out, proj_f, proj_f, proj_f, *br)


GELU_C = math.sqrt(2.0 / math.pi)
PAD = 8
SCAN_TILES = 8


def _gelu(x):
    return 0.5 * x * (1.0 + jnp.tanh(GELU_C * (x + 0.044715 * x * x * x)))


def _gelu_grad(x):
    t = jnp.tanh(GELU_C * (x + 0.044715 * x * x * x))
    return 0.5 * (1.0 + t) + 0.5 * x * (1.0 - t * t) * GELU_C * (1.0 + 3.0 * 0.044715 * x * x)


def _neg_expm1(x, exp_x):
    series = -x * (1.0 + x * (0.5 + x * (1.0 / 6.0)))
    return jnp.where(x > -0.02, series, 1.0 - exp_x)


def _lru_gates(xv, cw_ref, cb_ref, wr_ref, wi_ref, br_ref, bi_ref, lam_ref, pad_ref, s_len):
    pad_ref[pl.ds(0, PAD), :] = jnp.zeros((PAD, LANES), F32)
    pad_ref[pl.ds(PAD, s_len), :] = xv
    xc = cb_ref[...] + jnp.zeros((s_len, LANES), F32)
    for j in range(CONV_WIDTH):
        xc = xc + pad_ref[pl.ds(PAD - (CONV_WIDTH - 1) + j, s_len), :] * cw_ref[pl.ds(j, 1), :]
    xcb = xc.astype(BF16)
    r = _sigmoid(jnp.dot(xcb, wr_ref[0].astype(BF16), preferred_element_type=F32) + br_ref[...])
    i = _sigmoid(jnp.dot(xcb, wi_ref[0].astype(BF16), preferred_element_type=F32) + bi_ref[...])
    nl = -lam_ref[...]
    sp = jnp.maximum(nl, 0.0) + jnp.log(1.0 + jnp.exp(-jnp.abs(nl)))
    log_a = -LRU_C * r * sp
    a = jnp.exp(log_a)
    mult = jnp.sqrt(_neg_expm1(2.0 * log_a, a * a))
    return xc, r, i, sp, a, mult


def _tile_scan(a, b, row, reverse):
    for s in (1, 2, 4):
        if reverse:
            a_sh = pltpu.roll(a, 8 - s, 0)
            b_sh = pltpu.roll(b, 8 - s, 0)
            m = row + s <= 7
        else:
            a_sh = pltpu.roll(a, s, 0)
            b_sh = pltpu.roll(b, s, 0)
            m = row >= s
        b = jnp.where(m, a * b_sh + b, b)
        a = jnp.where(m, a * a_sh, a)
    return a, b


def lru_fwd(proj_f, conv_w, conv_b, wr_bd, wi_bd, b_rg, b_ig, lam, *, name, carried=None):
    bsz, s_len, _ = proj_f.shape
    d = D_MODEL
    ncb = d // LANES
    n_tiles = s_len // 8

    def body(x_ref, g_ref, cw_ref, cb_ref, wr_ref, wi_ref, br_ref, bi_ref, lam_ref, y_ref, h_ref, pad_ref, a_s, b_s):
        xc, r, i, sp, a, mult = _lru_gates(x_ref[0], cw_ref, cb_ref, wr_ref, wi_ref, br_ref, bi_ref, lam_ref, pad_ref, s_len)
        a_s[...] = a
        b_s[...] = mult * (i * xc)
        row = lax.broadcasted_iota(jnp.int32, (8, LANES), 0)

        def tiles(t, carry):
            starts = [pl.multiple_of((t * SCAN_TILES + u) * 8, 8) for u in range(SCAN_TILES)]
            local = [_tile_scan(a_s[pl.ds(i0, 8), :], b_s[pl.ds(i0, 8), :], row, False) for i0 in starts]
            for i0, (ac, hl) in zip(starts, local):
                h = hl + ac * carry
                h_ref[0, pl.ds(i0, 8), :] = h
                carry = jnp.broadcast_to(h[7:8, :], (8, LANES))
            return carry

        lax.fori_loop(0, n_tiles // SCAN_TILES, tiles, jnp.zeros((8, LANES), F32))
        y_ref[0] = (h_ref[0] * _gelu(g_ref[0])).astype(BF16)

    slab = lambda off: pl.BlockSpec((1, s_len, LANES), functools.partial(lambda o, c, b: (b, 0, o + c), off))
    vec = pl.BlockSpec((1, LANES), lambda c, b: (0, c))
    mat = pl.BlockSpec((1, LANES, LANES), lambda c, b: (c, 0, 0))
    out = pl.BlockSpec((1, s_len, LANES), lambda c, b: (b, 0, c))
    return call_with_exchange(
        body, carried,
        out_shape=(jax.ShapeDtypeStruct((bsz, s_len, d), BF16), jax.ShapeDtypeStruct((bsz, s_len, d), F32)),
        grid=(ncb, bsz),
        in_specs=[slab(0), slab(ncb), pl.BlockSpec((CONV_WIDTH, LANES), lambda c, b: (0, c)), vec, mat, mat, vec, vec, vec],
        out_specs=(out, out),
        scratch_shapes=[pltpu.VMEM((s_len + 2 * PAD, LANES), F32), pltpu.VMEM((s_len, LANES), F32), pltpu.VMEM((s_len, LANES), F32)],
        name=name,
        args=(proj_f, proj_f, conv_w, conv_b.reshape(1, d), wr_bd, wi_bd, b_rg.reshape(1, d), b_ig.reshape(1, d), lam.reshape(1, d)))


def lru_bwd(dy, proj_f, h, conv_w, conv_b, wr_bd, wi_bd, wr_bd_t, wi_bd_t, b_rg, b_ig, lam, *, name, carried=None):
    bsz, s_len, _ = proj_f.shape
    d = D_MODEL
    ncb = d // LANES
    n_tiles = s_len // 8

    def body(dy_ref, x_ref, g_ref, h_ref, cw_ref, cb_ref, wr_ref, wi_ref, wrt_ref, wit_ref, br_ref, bi_ref, lam_ref,
             dx_ref, dg_ref, dcw_ref, dcb_ref, dbr_ref, dbi_ref, dlam_ref, dwr_ref, dwi_ref, pad_ref, a_s, b_s, l_s):
        @pl.when(pl.program_id(1) == 0)
        def _():
            for ref in (dcw_ref, dcb_ref, dbr_ref, dbi_ref, dlam_ref, dwr_ref, dwi_ref):
                ref[...] = jnp.zeros_like(ref)

        xc, r, i, sp, a, mult = _lru_gates(x_ref[0], cw_ref, cb_ref, wr_ref, wi_ref, br_ref, bi_ref, lam_ref, pad_ref, s_len)
        gate = g_ref[0]
        hv = h_ref[0]
        dyv = dy_ref[0].astype(F32)
        dg_ref[0] = (dyv * hv * _gelu_grad(gate)).astype(BF16)
        b_s[...] = dyv * _gelu(gate)
        l_s[pl.ds(0, s_len), :] = a
        l_s[pl.ds(s_len, PAD), :] = jnp.zeros((PAD, LANES), F32)
        a_s[...] = l_s[pl.ds(1, s_len), :]
        row = lax.broadcasted_iota(jnp.int32, (8, LANES), 0)

        def tiles(t, carry):
            starts = [pl.multiple_of((n_tiles - 1 - (t * SCAN_TILES + u)) * 8, 8) for u in range(SCAN_TILES)]
            local = [_tile_scan(a_s[pl.ds(i0, 8), :], b_s[pl.ds(i0, 8), :], row, True) for i0 in starts]
            for i0, (ac, ll) in zip(starts, local):
                lmb = ll + ac * carry
                b_s[pl.ds(i0, 8), :] = lmb
                carry = jnp.broadcast_to(lmb[0:1, :], (8, LANES))
            return carry

        lax.fori_loop(0, n_tiles // SCAN_TILES, tiles, jnp.zeros((8, LANES), F32))
        lmb = b_s[...]
        l_s[pl.ds(0, PAD), :] = jnp.zeros((PAD, LANES), F32)
        l_s[pl.ds(PAD, s_len), :] = hv
        h_prev = l_s[pl.ds(PAD - 1, s_len), :]
        da = lmb * h_prev
        dmult = lmb * (i * xc)
        di = lmb * mult * xc
        dxc = lmb * mult * i
        dlog_a = da * a - dmult * a * a / mult
        dr = -LRU_C * sp * dlog_a
        dsp = jnp.sum(-LRU_C * r * dlog_a, axis=0, keepdims=True)
        dlam_ref[...] += dsp * (-_sigmoid(-lam_ref[...]))
        dpr = dr * r * (1.0 - r)
        dpi = di * i * (1.0 - i)
        dprb = dpr.astype(BF16)
        dpib = dpi.astype(BF16)
        xcb = xc.astype(BF16)
        dbr_ref[...] += jnp.sum(dpr, axis=0, keepdims=True)
        dbi_ref[...] += jnp.sum(dpi, axis=0, keepdims=True)
        tn = (((0,), (0,)), ((), ()))
        dwr_ref[0] += lax.dot_general(xcb, dprb, tn, preferred_element_type=F32)
        dwi_ref[0] += lax.dot_general(xcb, dpib, tn, preferred_element_type=F32)
        dxc = (dxc + jnp.dot(dprb, wrt_ref[0].astype(BF16), preferred_element_type=F32)
               + jnp.dot(dpib, wit_ref[0].astype(BF16), preferred_element_type=F32))
        dcb_ref[...] += jnp.sum(dxc, axis=0, keepdims=True)
        for j in range(CONV_WIDTH):
            dcw_ref[pl.ds(j, 1), :] += jnp.sum(dxc * pad_ref[pl.ds(PAD - (CONV_WIDTH - 1) + j, s_len), :], axis=0, keepdims=True)
        l_s[pl.ds(0, s_len), :] = dxc
        l_s[pl.ds(s_len, PAD), :] = jnp.zeros((PAD, LANES), F32)
        dx = jnp.zeros((s_len, LANES), F32)
        for j in range(CONV_WIDTH):
            dx = dx + l_s[pl.ds(CONV_WIDTH - 1 - j, s_len), :] * cw_ref[pl.ds(j, 1), :]
        dx_ref[0] = dx.astype(BF16)

    slab = lambda off: pl.BlockSpec((1, s_len, LANES), functools.partial(lambda o, c, b: (b, 0, o + c), off))
    vec = pl.BlockSpec((1, LANES), lambda c, b: (0, c))
    mat = pl.BlockSpec((1, LANES, LANES), lambda c, b: (c, 0, 0))
    cw = pl.BlockSpec((CONV_WIDTH, LANES), lambda c, b: (0, c))
    out = pl.BlockSpec((1, s_len, LANES), lambda c, b: (b, 0, c))
    vshape = jax.ShapeDtypeStruct((1, d), F32)
    mshape = jax.ShapeDtypeStruct((ncb, LANES, LANES), F32)
    return call_with_exchange(
        body, carried,
        out_shape=(jax.ShapeDtypeStruct((bsz, s_len, d), BF16),) * 2
        + (jax.ShapeDtypeStruct((CONV_WIDTH, d), F32), vshape, vshape, vshape, vshape, mshape, mshape),
        grid=(ncb, bsz),
        in_specs=[out, slab(0), slab(ncb), out, cw, vec, mat, mat, mat, mat, vec, vec, vec],
        out_specs=(out, out, cw, vec, vec, vec, vec, mat, mat),
        scratch_shapes=[pltpu.VMEM((s_len + 2 * PAD, LANES), F32), pltpu.VMEM((s_len, LANES), F32), pltpu.VMEM((s_len, LANES), F32),
                        pltpu.VMEM((s_len + 2 * PAD, LANES), F32)],
        name=name,
        args=(dy, proj_f, proj_f, h, conv_w, conv_b.reshape(1, d), wr_bd, wi_bd, wr_bd_t, wi_bd_t,
              b_rg.reshape(1, d), b_ig.reshape(1, d), lam.reshape(1, d)))


NT = (((1,), (1,)), ((), ()))
TN = (((0,), (0,)), ((), ()))
ATT_SCALE = HEAD_DIM ** -0.5


def _kv_place(head, n_kv_heads):
    kv = head // (N_HEADS // n_kv_heads)
    return kv // 2, kv % 2


def _band_mask(n, single):
    nk = ATT_BLOCK if single else 2 * ATT_BLOCK
    qi = lax.broadcasted_iota(jnp.int32, (2 * ATT_BLOCK, nk), 0) % ATT_BLOCK
    kj = lax.broadcasted_iota(jnp.int32, (2 * ATT_BLOCK, nk), 1)
    if single:
        return qi >= kj
    rel = qi + ATT_BLOCK - kj
    return (rel >= 0) & (rel <= ATT_BLOCK) & ((n > 0) | (kj >= ATT_BLOCK))


def _lane_halves():
    lane = lax.broadcasted_iota(jnp.int32, (1, LANES), 1)
    return lane < HEAD_DIM


def _stack_heads(t2, kh):
    first = _lane_halves()
    parts = []
    for a in range(2):
        ta = jnp.where(first if a == 0 else ~first, t2, jnp.zeros_like(t2))
        if a != kh[a]:
            ta = pltpu.roll(ta, HEAD_DIM, 1)
        parts.append(ta)
    return jnp.concatenate(parts, axis=0)


def _fold_heads(t, kh):
    t0, t1 = t[:ATT_BLOCK], t[ATT_BLOCK:]
    if t.shape[1] == LANES:
        if kh[0] != 0:
            t0 = pltpu.roll(t0, HEAD_DIM, 1)
        if kh[1] != 1:
            t1 = pltpu.roll(t1, HEAD_DIM, 1)
    return jnp.where(_lane_halves(), t0, t1)


def _rows_of_heads(t2):
    return jnp.concatenate([t2[:, 0:1], t2[:, HEAD_DIM:HEAD_DIM + 1]], axis=0)


PAIRS_AT_ONCE = 4


def _fill_bias(bias2_ref, bias1_ref=None):
    for i in range(2):
        bias2_ref[i] = jnp.where(_band_mask(i, False), 0.0, NEG_INF)
    if bias1_ref is not None:
        bias1_ref[...] = jnp.where(_band_mask(0, True), 0.0, NEG_INF)


def _pairs_fwd(items):
    ss = [lax.dot_general(_stack_heads(q2 * ATT_SCALE, kh), kk, NT, preferred_element_type=F32) + bias
          for q2, kk, _, bias, kh, _ in items]
    ps, ms, ls = [], [], []
    for s, (_, _, _, _, _, sink_col) in zip(ss, items):
        m = jnp.max(s, axis=-1, keepdims=True)
        if sink_col is not None:
            m = jnp.maximum(m, sink_col)
        p = jnp.exp(s - m)
        l = jnp.sum(p, axis=-1, keepdims=True)
        if sink_col is not None:
            l = l + jnp.exp(sink_col - m)
        ps.append(p.astype(BF16))
        ms.append(m)
        ls.append(l)
    pvs = [jnp.dot(p, it[2], preferred_element_type=F32) for p, it in zip(ps, items)]
    return list(zip(pvs, ms, ls))


def _pairs_bwd(items):
    first = _lane_halves()
    pre = []
    for q2, kk, vv, do2, o2, lse2, bias, kh in items:
        dd = do2 * o2
        dsum = jnp.concatenate([jnp.sum(jnp.where(first, dd, 0.0), axis=-1, keepdims=True),
                                jnp.sum(jnp.where(first, 0.0, dd), axis=-1, keepdims=True)], axis=0)
        qs = _stack_heads(q2 * ATT_SCALE, kh)
        dos = _stack_heads(do2.astype(BF16), kh)
        s = lax.dot_general(qs, kk, NT, preferred_element_type=F32) + bias
        dp = lax.dot_general(dos, vv, NT, preferred_element_type=F32)
        pre.append((qs, dos, s, dp, dsum))
    mid = []
    for (qs, dos, s, dp, dsum), it in zip(pre, items):
        p = jnp.exp(s - _rows_of_heads(it[5]))
        mid.append((p.astype(BF16), (p * (dp - dsum)).astype(BF16)))
    out = []
    for (pb, ds), (qs, dos, _, _, dsum), it in zip(mid, pre, items):
        dq = _fold_heads(jnp.dot(ds, it[1], preferred_element_type=F32), it[7]) * ATT_SCALE
        dk = lax.dot_general(ds, qs, TN, preferred_element_type=F32)
        dv = lax.dot_general(pb, dos, TN, preferred_element_type=F32)
        out.append((dq, dk, dv, dsum))
    return out


def swa_fwd(qkv, sinks, *, name, carried=None):
    bsz, s_len, width = qkv.shape
    ckv = SWA_KV_HEADS * HEAD_DIM
    nb = s_len // ATT_BLOCK
    kblk = D_MODEL // ckv

    def body(sink_ref, q_ref, kp_ref, kc_ref, vp_ref, vc_ref, o_ref, lse_ref, ob_ref, bias2):
        n = pl.program_id(1)
        _fill_bias(bias2)
        bias = bias2[jnp.minimum(n, 1)]
        kk = jnp.concatenate([kp_ref[0], kc_ref[0]], axis=0)
        vv = jnp.concatenate([vp_ref[0], vc_ref[0]], axis=0)
        top = lax.broadcasted_iota(jnp.int32, (2 * ATT_BLOCK, 1), 0) < ATT_BLOCK
        for hp0 in range(0, N_HEADS // 2, PAIRS_AT_ONCE):
            items, places = [], []
            for hp in range(hp0, hp0 + PAIRS_AT_ONCE):
                cols = slice(hp * LANES, (hp + 1) * LANES)
                kb, kh = _kv_place(2 * hp, SWA_KV_HEADS)
                kcols = slice(kb * LANES, (kb + 1) * LANES)
                sink_col = jnp.where(top, sink_ref[2 * hp], sink_ref[2 * hp + 1])
                items.append((q_ref[0, :, cols], kk[:, kcols], vv[:, kcols], bias, (kh, kh), sink_col))
                places.append((cols, (kh, kh)))
            for (pv, m, l), (cols, kh2) in zip(_pairs_fwd(items), places):
                o2 = _fold_heads(pv / l, kh2)
                o_ref[0, :, cols] = o2
                ob_ref[0, :, cols] = o2.astype(BF16)
                lse_ref[0, :, cols] = _fold_heads(m + jnp.log(l), kh2)

    prev = lambda n: jnp.maximum(n - 1, 0)
    out = pl.BlockSpec((1, ATT_BLOCK, D_MODEL), lambda b, n: (b, n, 0))
    sd = lambda dt: jax.ShapeDtypeStruct((bsz, s_len, D_MODEL), dt)
    return call_with_exchange(
        body, carried,
        out_shape=(sd(F32), sd(F32), sd(BF16)),
        grid=(bsz, nb),
        in_specs=[pl.BlockSpec(memory_space=pltpu.SMEM), out,
                  pl.BlockSpec((1, ATT_BLOCK, ckv), lambda b, n: (b, prev(n), kblk)),
                  pl.BlockSpec((1, ATT_BLOCK, ckv), lambda b, n: (b, n, kblk)),
                  pl.BlockSpec((1, ATT_BLOCK, ckv), lambda b, n: (b, prev(n), kblk + 1)),
                  pl.BlockSpec((1, ATT_BLOCK, ckv), lambda b, n: (b, n, kblk + 1))],
        out_specs=(out, out, out),
        scratch_shapes=[pltpu.VMEM((2, 2 * ATT_BLOCK, 2 * ATT_BLOCK), F32)],
        name=name,
        args=(sinks, qkv, qkv, qkv, qkv, qkv))


def swa_bwd(qkv, sinks, o, lse, do, *, name, carried=None):
    bsz, s_len, width = qkv.shape
    ckv = SWA_KV_HEADS * HEAD_DIM
    nb = s_len // ATT_BLOCK
    kblk = D_MODEL // ckv

    def body(sink_ref, q_ref, kp_ref, kc_ref, vp_ref, vc_ref, o_ref, lse_ref, do_ref, dq_ref, dk_ref, dv_ref, dsink_ref,
             dkk, dvv, ck, cv, bias2):
        n = pl.program_id(1)

        @pl.when((n == 0) & (pl.program_id(0) == 0))
        def _():
            dsink_ref[...] = jnp.zeros_like(dsink_ref)

        @pl.when(n < nb)
        def _():
            top = lax.broadcasted_iota(jnp.int32, (2 * ATT_BLOCK, 1), 0) < ATT_BLOCK
            lane = lax.broadcasted_iota(jnp.int32, dsink_ref.shape, 1)
            first_row = lax.broadcasted_iota(jnp.int32, dsink_ref.shape, 0) == 0
            _fill_bias(bias2)
            bias = bias2[jnp.minimum(n, 1)]
            kk = jnp.concatenate([kp_ref[0], kc_ref[0]], axis=0)
            vv = jnp.concatenate([vp_ref[0], vc_ref[0]], axis=0)
            dkk[...] = jnp.zeros_like(dkk)
            dvv[...] = jnp.zeros_like(dvv)
            for hp0 in range(0, N_HEADS // 2, PAIRS_AT_ONCE):
                items, places = [], []
                for hp in range(hp0, hp0 + PAIRS_AT_ONCE):
                    cols = slice(hp * LANES, (hp + 1) * LANES)
                    kb, kh = _kv_place(2 * hp, SWA_KV_HEADS)
                    kcols = slice(kb * LANES, (kb + 1) * LANES)
                    items.append((q_ref[0, :, cols], kk[:, kcols], vv[:, kcols], do_ref[0, :, cols], o_ref[0, :, cols],
                                  lse_ref[0, :, cols], bias, (kh, kh)))
                    places.append((cols, kcols, hp))
                for (dq, dk, dv, dsum), (cols, kcols, hp) in zip(_pairs_bwd(items), places):
                    dq_ref[0, :, cols] = dq.astype(BF16)
                    dkk[:, kcols] += dk
                    dvv[:, kcols] += dv
                    sink_col = jnp.where(top, sink_ref[2 * hp], sink_ref[2 * hp + 1])
                    t = -jnp.exp(sink_col - _rows_of_heads(lse_ref[0, :, cols])) * dsum
                    d0 = jnp.sum(t[:ATT_BLOCK], axis=0, keepdims=True)
                    d1 = jnp.sum(t[ATT_BLOCK:], axis=0, keepdims=True)
                    dsink_ref[...] += jnp.where(first_row & (lane == 2 * hp), d0, 0.0) + jnp.where(first_row & (lane == 2 * hp + 1), d1, 0.0)

        @pl.when((n >= 1) & (n < nb))
        def _():
            dk_ref[0] = (ck[...] + dkk[pl.ds(0, ATT_BLOCK), :]).astype(BF16)
            dv_ref[0] = (cv[...] + dvv[pl.ds(0, ATT_BLOCK), :]).astype(BF16)

        @pl.when(n == nb)
        def _():
            dk_ref[0] = ck[...].astype(BF16)
            dv_ref[0] = cv[...].astype(BF16)

        @pl.when(n < nb)
        def _():
            ck[...] = dkk[pl.ds(ATT_BLOCK, ATT_BLOCK), :]
            cv[...] = dvv[pl.ds(ATT_BLOCK, ATT_BLOCK), :]

    clamp = lambda n: jnp.minimum(n, nb - 1)
    prev = lambda n: jnp.maximum(n - 1, 0)
    row = pl.BlockSpec((1, ATT_BLOCK, D_MODEL), lambda b, n: (b, clamp(n), 0))
    kv_out = pl.BlockSpec((1, ATT_BLOCK, ckv), lambda b, n: (b, prev(n), 0))
    return call_with_exchange(
        body, carried,
        out_shape=(jax.ShapeDtypeStruct((bsz, s_len, D_MODEL), BF16), jax.ShapeDtypeStruct((bsz, s_len, ckv), BF16),
                   jax.ShapeDtypeStruct((bsz, s_len, ckv), BF16), jax.ShapeDtypeStruct((8, LANES), F32)),
        grid=(bsz, nb + 1),
        in_specs=[pl.BlockSpec(memory_space=pltpu.SMEM), row,
                  pl.BlockSpec((1, ATT_BLOCK, ckv), lambda b, n: (b, prev(clamp(n)), kblk)),
                  pl.BlockSpec((1, ATT_BLOCK, ckv), lambda b, n: (b, clamp(n), kblk)),
                  pl.BlockSpec((1, ATT_BLOCK, ckv), lambda b, n: (b, prev(clamp(n)), kblk + 1)),
                  pl.BlockSpec((1, ATT_BLOCK, ckv), lambda b, n: (b, clamp(n), kblk + 1)),
                  row, row, row],
        out_specs=(row, kv_out, kv_out, pl.BlockSpec((8, LANES), lambda b, n: (0, 0))),
        scratch_shapes=[pltpu.VMEM((2 * ATT_BLOCK, ckv), F32), pltpu.VMEM((2 * ATT_BLOCK, ckv), F32),
                        pltpu.VMEM((ATT_BLOCK, ckv), F32), pltpu.VMEM((ATT_BLOCK, ckv), F32),
                        pltpu.VMEM((2, 2 * ATT_BLOCK, 2 * ATT_BLOCK), F32)],
        name=name,
        args=(sinks, qkv, qkv, qkv, qkv, qkv, o, lse, do))


DIL_PATTERNS = tuple((d, 2048 // d // ATT_BLOCK) for d in reversed(DILATIONS))
MHA = (0, 1)


def _dil_rows(idx, d, nb):
    j = idx // nb
    n = idx % nb
    base = j + n * (ATT_BLOCK * d)
    prev = jnp.maximum(base - ATT_BLOCK * d, j)
    if d == 1:
        return n, pl.ds(pl.multiple_of(base, ATT_BLOCK), ATT_BLOCK), pl.ds(pl.multiple_of(prev, ATT_BLOCK), ATT_BLOCK)
    return n, pl.ds(base, ATT_BLOCK, stride=d), pl.ds(prev, ATT_BLOCK, stride=d)


def dil_fwd(qkv, *, name, carried=None):
    bsz, s_len, _ = qkv.shape
    assert s_len == DIL_PATTERNS[0][0] * DIL_PATTERNS[0][1] * ATT_BLOCK
    npair = N_HEADS // 2

    def body(q_ref, k_ref, v_ref, y_ref, lse_ref, yb_ref, m_acc, l_acc, bias2, bias1):
        _fill_bias(bias2, bias1)
        for ci, (d, nb) in enumerate(DIL_PATTERNS):
            single = nb == 1

            def blocks(it, carry):
                items, places = [], []
                for u in range(PAIRS_AT_ONCE):
                    n, rows, prows = _dil_rows(it * PAIRS_AT_ONCE + u, d, nb)
                    kc = k_ref[rows, :].astype(BF16)
                    vc = v_ref[rows, :].astype(BF16)
                    if single:
                        kk, vv, bias = kc, vc, bias1[...]
                    else:
                        kk = jnp.concatenate([k_ref[prows, :].astype(BF16), kc], axis=0)
                        vv = jnp.concatenate([v_ref[prows, :].astype(BF16), vc], axis=0)
                        bias = bias2[jnp.minimum(n, 1)]
                    items.append((q_ref[rows, :].astype(BF16), kk, vv, bias, MHA, None))
                    places.append(rows)
                for (pv, m, l), rows in zip(_pairs_fwd(items), places):
                    o2, m2, l2 = _fold_heads(pv, MHA), _fold_heads(m, MHA), _fold_heads(l, MHA)
                    if ci == 0:
                        y_ref[rows, :] = o2
                        m_acc[rows, :] = m2
                        l_acc[rows, :] = l2
                    else:
                        m_old = m_acc[rows, :]
                        m_new = jnp.maximum(m_old, m2)
                        w_old = jnp.exp(m_old - m_new)
                        w_new = jnp.exp(m2 - m_new)
                        y_ref[rows, :] = y_ref[rows, :] * w_old + o2 * w_new
                        l_acc[rows, :] = l_acc[rows, :] * w_old + l2 * w_new
                        m_acc[rows, :] = m_new
                return carry

            lax.fori_loop(0, d * nb // PAIRS_AT_ONCE, blocks, 0)
        y = y_ref[...] / l_acc[...]
        y_ref[...] = y
        yb_ref[...] = y.astype(BF16)
        lse_ref[...] = m_acc[...] + jnp.log(l_acc[...])

    slab = lambda off: pl.BlockSpec((None, s_len, LANES), functools.partial(lambda o, b, h: (b, 0, o + h), off))
    sd = lambda dt: jax.ShapeDtypeStruct((bsz, s_len, D_MODEL), dt)
    return call_with_exchange(
        body, carried,
        out_shape=(sd(F32), sd(F32), sd(BF16)),
        grid=(bsz, npair),
        in_specs=[slab(0), slab(npair), slab(2 * npair)],
        out_specs=(slab(0), slab(0), slab(0)),
        scratch_shapes=[pltpu.VMEM((s_len, LANES), F32), pltpu.VMEM((s_len, LANES), F32),
                        pltpu.VMEM((2, 2 * ATT_BLOCK, 2 * ATT_BLOCK), F32), pltpu.VMEM((2 * ATT_BLOCK, ATT_BLOCK), F32)],
        name=name,
        args=(qkv, qkv, qkv))


def dil_bwd(qkv, y, lse, dy, *, name, carried=None):
    bsz, s_len, _ = qkv.shape
    npair = N_HEADS // 2

    def body(q_ref, k_ref, v_ref, y_ref, lse_ref, dy_ref, dq_out, dk_out, dv_out, bias2, bias1, dq_ref, dk_ref, dv_ref):
        _fill_bias(bias2, bias1)
        assert DIL_PATTERNS[0][1] == 1
        for d, nb in DIL_PATTERNS:
            single = nb == 1

            def blocks(it, carry):
                items, places = [], []
                for u in range(PAIRS_AT_ONCE):
                    n, rows, prows = _dil_rows(it * PAIRS_AT_ONCE + u, d, nb)
                    kc = k_ref[rows, :].astype(BF16)
                    vc = v_ref[rows, :].astype(BF16)
                    if single:
                        kk, vv, bias = kc, vc, bias1[...]
                    else:
                        kk = jnp.concatenate([k_ref[prows, :].astype(BF16), kc], axis=0)
                        vv = jnp.concatenate([v_ref[prows, :].astype(BF16), vc], axis=0)
                        bias = bias2[jnp.minimum(n, 1)]
                    items.append((q_ref[rows, :].astype(BF16), kk, vv, dy_ref[rows, :], y_ref[rows, :], lse_ref[rows, :], bias, MHA))
                    places.append((rows, prows))
                for (dq, dk, dv, _), (rows, prows) in zip(_pairs_bwd(items), places):
                    if single:
                        dq_ref[rows, :] = dq
                        dk_ref[rows, :] = dk
                        dv_ref[rows, :] = dv
                    else:
                        dq_ref[rows, :] += dq
                        dk_ref[prows, :] += dk[:ATT_BLOCK]
                        dv_ref[prows, :] += dv[:ATT_BLOCK]
                        dk_ref[rows, :] += dk[ATT_BLOCK:]
                        dv_ref[rows, :] += dv[ATT_BLOCK:]
                return carry

            lax.fori_loop(0, d * nb // PAIRS_AT_ONCE, blocks, 0)
        dq_out[...] = dq_ref[...].astype(BF16)
        dk_out[...] = dk_ref[...].astype(BF16)
        dv_out[...] = dv_ref[...].astype(BF16)

    slab = lambda off: pl.BlockSpec((None, s_len, LANES), functools.partial(lambda o, b, h: (b, 0, o + h), off))
    sd = jax.ShapeDtypeStruct((bsz, s_len, D_MODEL), BF16)
    return call_with_exchange(
        body, carried,
        out_shape=(sd, sd, sd),
        grid=(bsz, npair),
        in_specs=[slab(0), slab(npair), slab(2 * npair), slab(0), slab(0), slab(0)],
        out_specs=(slab(0), slab(0), slab(0)),
        scratch_shapes=[pltpu.VMEM((2, 2 * ATT_BLOCK, 2 * ATT_BLOCK), F32), pltpu.VMEM((2 * ATT_BLOCK, ATT_BLOCK), F32)]
        + [pltpu.VMEM((s_len, LANES), F32)] * 3,
        name=name,
        args=(qkv, qkv, qkv, y, lse, dy))


def adamw(w, g, m, v, *, name):
    rows, cols = w.shape
    tr = _pick(rows, (256, 128, 64, 32, 16, 8))

    def body(w_ref, g_ref, m_ref, v_ref, d_ref, nm_ref, nv_ref):
        gv = g_ref[...]
        nm = ADAM_B1 * m_ref[...] + (1.0 - ADAM_B1) * gv
        nv = ADAM_B2 * v_ref[...] + (1.0 - ADAM_B2) * (gv * gv)
        m_hat = nm / (1.0 - ADAM_B1 ** ADAM_STEP)
        v_hat = nv / (1.0 - ADAM_B2 ** ADAM_STEP)
        d_ref[...] = -ADAM_LR * (m_hat / (jnp.sqrt(v_hat) + ADAM_EPS) + ADAM_WD * w_ref[...])
        nm_ref[...] = nm
        nv_ref[...] = nv

    row = pl.BlockSpec((tr, cols), lambda i: (i, 0))
    return pl.pallas_call(
        body,
        out_shape=(jax.ShapeDtypeStruct((rows, cols), F32),) * 3,
        grid=(rows // tr,),
        in_specs=[row] * 4,
        out_specs=(row, row, row),
        compiler_params=_params(("parallel",)),
        name=name,
    )(w, g, m, v)


def _place():
    return lax.axis_index("x"), lax.axis_index("y"), lax.axis_index("c")


def _gather_copies(x_ref, out_ref, send_sems, recv_sems):
    x, y, c = _place()
    me, sibling = (x, y, c), (x, y, 1 - c)
    chips = [(1 - x, y), (x, 1 - y), (1 - x, 1 - y)]

    def slot(px, py, pc):
        return out_ref.at[4 * px + 2 * py + pc]

    def copy(k, block, to, src=None):
        return pltpu.make_async_remote_copy(
            src_ref=slot(*block) if src is None else src, dst_ref=slot(*block),
            send_sem=send_sems.at[k], recv_sem=recv_sems.at[k], device_id=to, device_id_type=MESH)

    first = [lambda: copy(0, me, sibling, src=x_ref)] + [functools.partial(copy, 1 + j, me, (*chip, c), src=x_ref)
                                                         for j, chip in enumerate(chips)]
    passed = [functools.partial(copy, 4 + j, (*chip, c), sibling) for j, chip in enumerate(chips)]
    landing = [functools.partial(copy, 1 + j, (*chip, c), me) for j, chip in enumerate(chips)]
    from_sibling = [lambda: copy(0, sibling, me)] + [functools.partial(copy, 4 + j, (*chip, 1 - c), me) for j, chip in enumerate(chips)]
    return slot(*me), first, passed, landing, from_sibling


def _gather_start(x_ref, out_ref, send_sems, recv_sems, local_sem):
    mine, first, _, _, _ = _gather_copies(x_ref, out_ref, send_sems, recv_sems)
    pltpu.make_async_copy(x_ref, mine, local_sem).start()
    for cp in first:
        cp().start()


def _gather_finish(x_ref, out_ref, send_sems, recv_sems, local_sem):
    mine, first, passed, landing, from_sibling = _gather_copies(x_ref, out_ref, send_sems, recv_sems)
    for cp, fwd in zip(landing, passed):
        cp().wait_recv()
        fwd().start()
    for cp in from_sibling:
        cp().wait_recv()
    for cp in first + passed:
        cp().wait_send()
    pltpu.make_async_copy(x_ref, mine, local_sem).wait()


def _a2a_copies(x_ref, out_ref, send_sems, recv_sems):
    x, y, c = _place()
    me = 4 * x + 2 * y + c
    copies = []
    for k in range(1, N_DEV):
        px = 1 - x if k & 4 else x
        py = 1 - y if k & 2 else y
        pc = 1 - c if k & 1 else c
        copies.append(pltpu.make_async_remote_copy(
            src_ref=x_ref.at[4 * px + 2 * py + pc], dst_ref=out_ref.at[me], send_sem=send_sems.at[k - 1],
            recv_sem=recv_sems.at[k - 1], device_id=(px, py, pc), device_id_type=MESH))
    return me, copies


def _a2a_start(x_ref, out_ref, send_sems, recv_sems, local_sem):
    me, copies = _a2a_copies(x_ref, out_ref, send_sems, recv_sems)
    pltpu.make_async_copy(x_ref.at[me], out_ref.at[me], local_sem).start()
    for cp in copies:
        cp.start()


def _a2a_finish(x_ref, out_ref, send_sems, recv_sems, local_sem):
    me, copies = _a2a_copies(x_ref, out_ref, send_sems, recv_sems)
    for cp in copies:
        cp.wait_recv()
    for cp in copies:
        cp.wait_send()
    pltpu.make_async_copy(x_ref.at[me], out_ref.at[me], local_sem).wait()


EXCHANGES = {"gather": (_gather_start, _gather_finish, lambda x: (N_DEV,) + x.shape),
             "a2a": (_a2a_start, _a2a_finish, lambda x: x.shape)}
EXCHANGE_SEMS = [pltpu.SemaphoreType.DMA((7,)), pltpu.SemaphoreType.DMA((7,)), pltpu.SemaphoreType.DMA(())]


def exchange(kind, x, *, name):
    start, finish, shape = EXCHANGES[kind]

    def body(x_ref, out_ref, *sems):
        start(x_ref, out_ref, *sems)
        finish(x_ref, out_ref, *sems)

    return pl.pallas_call(
        body,
        out_shape=jax.ShapeDtypeStruct(shape(x), x.dtype),
        in_specs=[pl.BlockSpec(memory_space=pl.ANY)],
        out_specs=pl.BlockSpec(memory_space=pl.ANY),
        scratch_shapes=EXCHANGE_SEMS,
        name=name,
    )(x)


def call_with_exchange(body, carried, *, out_shape, grid, in_specs, out_specs, scratch_shapes, name, args):
    sem = ("arbitrary",) * len(grid)
    carried = list(carried or ())
    if not carried:
        res = pl.pallas_call(body, out_shape=out_shape, grid=grid, in_specs=in_specs, out_specs=out_specs,
                             scratch_shapes=scratch_shapes, compiler_params=_params(sem), name=name)(*args)
        return res, []
    n_in, n_out, n_scr, n_x = len(in_specs), len(out_shape), len(scratch_shapes), len(carried)
    n_sems = len(EXCHANGE_SEMS)

    def wrapped(*refs):
        ins, x_refs = refs[:n_in], refs[n_in:n_in + n_x]
        outs = refs[n_in + n_x:n_in + n_x + n_out]
        out_refs = refs[n_in + n_x + n_out:n_in + 2 * n_x + n_out]
        rest = refs[n_in + 2 * n_x + n_out:]
        scratch, sems = rest[:n_scr], rest[n_scr:]
        ids = [pl.program_id(i) for i in range(len(grid))]
        is_first = functools.reduce(lambda a, b: a & b, [i == 0 for i in ids])
        is_last = functools.reduce(lambda a, b: a & b, [i == g - 1 for i, g in zip(ids, grid)])

        @pl.when(is_first)
        def _():
            for e, (kind, _) in enumerate(carried):
                EXCHANGES[kind][0](x_refs[e], out_refs[e], *sems[e * n_sems:(e + 1) * n_sems])

        body(*ins, *outs, *scratch)

        @pl.when(is_last)
        def _():
            for e, (kind, _) in enumerate(carried):
                EXCHANGES[kind][1](x_refs[e], out_refs[e], *sems[e * n_sems:(e + 1) * n_sems])

    any_spec = pl.BlockSpec(memory_space=pl.ANY)
    res = pl.pallas_call(
        wrapped,
        out_shape=tuple(out_shape) + tuple(jax.ShapeDtypeStruct(EXCHANGES[kind][2](x), x.dtype) for kind, x in carried),
        grid=grid,
        in_specs=list(in_specs) + [any_spec] * n_x,
        out_specs=tuple(out_specs) + (any_spec,) * n_x,
        scratch_shapes=list(scratch_shapes) + EXCHANGE_SEMS * n_x,
        compiler_params=_params(sem),
        name=name + "".join("_" + kind for kind, _ in carried),
    )(*args, *[x for _, x in carried])
    return res[:n_out], list(res[n_out:])


def sum_slots(x, *, name):
    _, rows, cols = x.shape
    tr = _pick(rows, (512, 256, 128, 64, 32, 16))

    def body(x_ref, o_ref):
        acc = x_ref[0].astype(F32)
        for k in range(1, N_DEV):
            acc = acc + x_ref[k].astype(F32)
        o_ref[...] = acc

    return pl.pallas_call(
        body,
        out_shape=jax.ShapeDtypeStruct((rows, cols), F32),
        grid=(rows // tr,),
        in_specs=[pl.BlockSpec((N_DEV, tr, cols), lambda i: (0, i, 0))],
        out_specs=pl.BlockSpec((tr, cols), lambda i: (i, 0)),
        compiler_params=_params(("parallel",)),
        name=name,
    )(x)


BIG = ("w_in", "w_branch", "w_out", "w_ffn_in", "w_ffn_out")
SMALL = ("conv_b", "w_rg", "b_rg", "w_ig", "b_ig", "lru_lambda", "sinks", "ln1_g", "ln1_b", "ln2_g", "ln2_b")
N_LRU_BLOCKS = D_MODEL // HEAD_DIM
SMALL_ROWS_TILE = 512


def _block_diag(w):
    z = jnp.zeros((N_LRU_BLOCKS // 2, HEAD_DIM, HEAD_DIM), w.dtype)
    top = jnp.concatenate([w[0::2], z], axis=2)
    bot = jnp.concatenate([z, w[1::2]], axis=2)
    return jnp.concatenate([top, bot], axis=1)


def _block_diag_grad(g):
    return jnp.stack([g[:, :HEAD_DIM, :HEAD_DIM], g[:, HEAD_DIM:, HEAD_DIM:]], axis=1).reshape(N_LRU_BLOCKS, HEAD_DIM, HEAD_DIM)


def layer_fwd(x, xb, p, bsz, own_late=None, next_w_in=None):
    t_dim = x.shape[0]
    s_len = t_dim // bsz
    w_f, w_qs, w_qd = p["w_in_f"], p["w_in_qs"], p["w_in_qd"]
    proj_f = matmul(xb, w_f, name="proj_f")
    qs = matmul(xb, w_qs, out_dtype=BF16, name="proj_qs").reshape(bsz, s_len, W_QS)
    qd = matmul(xb, w_qd, name="proj_qd").reshape(bsz, s_len, W_QD)
    proj_f3 = proj_f.reshape(bsz, s_len, W_F)
    wr_bd, wi_bd = _block_diag(p["w_rg"]), _block_diag(p["w_ig"])
    (y_a, h), got_rows = lru_fwd(proj_f3, p["conv_w"], p["conv_b"], wr_bd, wi_bd, p["b_rg"], p["b_ig"], p["lru_lambda"],
                                 name="lru_fwd", carried=[("gather", own_late[1])] if own_late is not None else [])
    (y_b, lse_b, y_bb), got_fi = swa_fwd(qs, p["sinks"], name="swa_fwd", carried=[("gather", own_late[0])] if own_late is not None else [])
    (y_c, lse_c, y_cb), got_next = dil_fwd(qd, name="dil_fwd", carried=[("gather", next_w_in)] if next_w_in is not None else [])
    if own_late is not None:
        p = {**p, **_late_weights(got_fi[0], got_rows[0])}
    ys = [t.reshape(t_dim, D_MODEL) for t in (y_a, y_bb, y_cb)]
    merged, br = branch_merge(ys, p["w_branch"], proj_f, name="branch_merge")
    x1, x1b, z1 = ln_fwd(x, merged, p["w_out"], p["ln1_g"], p["ln1_b"], name="w_out_ln")
    h1, h3, act = ffn_in_swiglu(x1b, p["w_ffn_in"], name="ffn_in_swiglu")
    x2, x2b, z2 = ln_fwd(x1, act, p["w_ffn_out"], p["ln2_g"], p["ln2_b"], name="ffn_out_ln")
    saved = dict(xb=xb, proj_f=proj_f, qs=qs, qd=qd, h=h, ys=ys, y_b=y_b, y_c=y_c, lse_b=lse_b, lse_c=lse_c, br=br, merged=merged,
                 z1=z1, x1b=x1b, h1=h1, h3=h3, act=act, z2=z2, wr_bd=wr_bd, wi_bd=wi_bd, p=p)
    return x2, x2b, saved, (got_next[0] if got_next else None)


def layer_bwd(dx2, s, bsz, exchange_own=False, above_w_in=None, pending=None, defer_last=False):
    p = s["p"]
    t_dim = dx2.shape[0]
    s_len = t_dim // bsz
    g = {}
    if pending is None:
        dz2, dz2b, g["ln2_g"], g["ln2_b"] = ln_bwd(dx2, s["z2"], p["ln2_g"], name="ln2_bwd")
    else:
        dz2, dz2b, g["ln2_g"], g["ln2_b"] = ln_bwd(dx2, s["z2"], p["ln2_g"], a=pending[0], w=pending[1], name="dx_qd_ln2_bwd")
    dh13 = swiglu_bwd(dz2b, p["w_ffn_out"], s["h1"], s["h3"], name="swiglu_bwd")
    g["w_ffn_out"] = matmul(s["act"], dz2b, trans_a=True, out_dtype=BF16, name="dw_ffn_out")
    g["w_ffn_in"] = matmul(s["x1b"], dh13, trans_a=True, out_dtype=BF16, name="dw_ffn_in")
    dz1, dz1b, g["ln1_g"], g["ln1_b"] = ln_bwd(dz2, s["z1"], p["ln1_g"], a=dh13, w=p["w_ffn_in"], dy_scale=ALPHA, name="dx_ffn_ln1_bwd")
    g["w_out"] = matmul(s["merged"], dz1b, trans_a=True, out_dtype=BF16, name="dw_out")
    *dbr, dgates = merge_bwd(dz1b, p["w_out"], s["proj_f"], s["br"], name="merge_bwd")
    dys = [matmul(dbr[n], p["w_branch"][n], trans_b=True, out_dtype=F32 if n == 2 else BF16, name="d_branch") for n in range(3)]
    g["w_branch"] = jnp.stack([matmul(s["ys"][n], dbr[n], trans_a=True, out_dtype=BF16, name="dw_branch") for n in range(3)])
    fi_slots, rows_slots = _late_slots(g) if exchange_own else (None, None)
    shape3 = (bsz, s_len, D_MODEL)
    (dlx, dlg, g["conv_w"], g["conv_b"], g["b_rg"], g["b_ig"], g["lru_lambda"], dwr, dwi), got_rows = lru_bwd(
        dys[0].reshape(shape3), s["proj_f"].reshape(bsz, s_len, W_F), s["h"], p["conv_w"], p["conv_b"], s["wr_bd"], s["wi_bd"],
        jnp.swapaxes(s["wr_bd"], 1, 2), jnp.swapaxes(s["wi_bd"], 1, 2), p["b_rg"], p["b_ig"], p["lru_lambda"], name="lru_bwd",
        carried=[("a2a", rows_slots)] if exchange_own else [])
    g["w_rg"], g["w_ig"] = _block_diag_grad(dwr), _block_diag_grad(dwi)
    dy_b3 = dys[1].reshape(shape3)
    (*dqs, dsinks), got_fi = swa_bwd(s["qs"], p["sinks"], s["y_b"], s["lse_b"], dy_b3, name="swa_bwd",
                                     carried=[("a2a", fi_slots)] if exchange_own else [])
    g["sinks"] = dsinks[0, :N_HEADS]
    dqd, got_in = dil_bwd(s["qd"], s["y_c"], s["lse_c"], dys[2].reshape(shape3), name="dil_bwd",
                          carried=[("a2a", above_w_in)] if above_w_in is not None else [])
    flat = lambda t: t.reshape(t_dim, t.shape[-1])
    dproj_f = jnp.concatenate([flat(dlx), flat(dlg), dgates], axis=1)
    dproj_qs = jnp.concatenate([flat(t) for t in dqs], axis=1)
    dproj_qd = jnp.concatenate([flat(t) for t in dqd], axis=1)
    g["w_in_f"] = matmul(s["xb"], dproj_f, trans_a=True, out_dtype=BF16, name="dw_in_f")
    g["w_in_qs"] = matmul(s["xb"], dproj_qs, trans_a=True, out_dtype=BF16, name="dw_in_qs")
    g["w_in_qd"] = matmul(s["xb"], dproj_qd, trans_a=True, out_dtype=BF16, name="dw_in_qd")
    dx = matmul(dproj_f, p["w_in_f"], trans_b=True, add=dz1, add_scale=ALPHA, name="dx_f")
    dx = matmul(dproj_qs, p["w_in_qs"], trans_b=True, add=dx, name="dx_qs")
    left = (dproj_qd, p["w_in_qd"]) if defer_last else None
    if not defer_last:
        dx = matmul(dproj_qd, p["w_in_qd"], trans_b=True, add=dx, name="dx_qd")
    g = {k: (v.reshape(p[k].shape) if k in p else v) for k, v in g.items()}
    return dx, g, dict(late=(got_fi[0], got_rows[0]) if exchange_own else None, w_in=got_in[0] if got_in else None), left


def local_step(x, target, layer_params, layer_shards=None, first_w_in=None):
    bsz, s_len, d = x.shape
    t_dim = bsz * s_len
    xf = x.reshape(t_dim, d)
    xb = xf.astype(BF16)
    exchanging = layer_shards is not None
    saved, gathered = [], first_w_in
    for l in range(DEPTH):
        p = layer_params(l, gathered)
        xf, xb, s, gathered = layer_fwd(xf, xb, p, bsz, own_late=layer_shards[l][1:] if exchanging else None,
                                        next_w_in=layer_shards[l + 1][0] if exchanging and l + 1 < DEPTH else None)
        saved.append(s)
    dy, sq = loss_head(xf, target.reshape(t_dim, d), name="loss_head")
    loss = 0.5 * jnp.sum(sq) / d
    grads, received, w_in_slots, pending = [None] * DEPTH, [[None] * 3 for _ in range(DEPTH)], None, None
    for l in reversed(range(DEPTH)):
        dy, grads[l], got, pending = layer_bwd(dy, saved[l], bsz, exchange_own=exchanging, above_w_in=w_in_slots,
                                               pending=pending, defer_last=l > 0)
        if got["w_in"] is not None:
            received[l + 1][0] = got["w_in"]
        if exchanging:
            received[l][1:] = got["late"]
            w_in_slots = _w_in_slots(grads[l])
    return loss, dy.reshape(bsz, s_len, d), grads, received, w_in_slots


W_IN_SEGMENTS = (("w_in_f", 0, 0, 2 * D_MODEL), ("w_in_qs", 0, 2 * D_MODEL, W_QS), ("w_in_qd", 0, 2 * D_MODEL + W_QS, W_QD),
                 ("w_in_f", 2 * D_MODEL, 2 * D_MODEL + W_QS + W_QD, 3 * D_MODEL))
ROW_SHARDED = ("w_branch", "w_out", "w_ffn_out")


def _cols_of_shards(shards, lo, hi):
    width = shards[0].shape[-1]
    parts = []
    for k, sh in enumerate(shards):
        a, b = max(lo, k * width), min(hi, (k + 1) * width)
        if a < b:
            parts.append(sh[..., a - k * width:b - k * width])
    return parts[0] if len(parts) == 1 else jnp.concatenate(parts, axis=-1)


def _cols_of_w_in(pieces, lo, hi):
    parts = []
    for name, p0, l0, width in W_IN_SEGMENTS:
        a, b = max(lo, l0), min(hi, l0 + width)
        if a < b:
            parts.append(pieces[name][..., p0 + a - l0:p0 + b - l0])
    return parts[0] if len(parts) == 1 else jnp.concatenate(parts, axis=-1)


W_IN_COLS = W_F + W_QS + W_QD


def _layer_shards(w):
    rows = jnp.concatenate([w[k].reshape(DEPTH, -1, D_MODEL) for k in ROW_SHARDED], axis=1).astype(BF16)
    w_in, w_fi = w["w_in"].astype(BF16), w["w_ffn_in"].astype(BF16)
    return [(w_in[l], w_fi[l], rows[l]) for l in range(DEPTH)]


ROW_COUNTS = (3 * D_MODEL // N_DEV, D_MODEL // N_DEV, FF_HIDDEN // N_DEV)


def _w_in_weights(g_in):
    sh = [g_in[k] for k in range(N_DEV)]
    return dict(w_in_f=jnp.concatenate([_cols_of_shards(sh, 0, 2 * D_MODEL), _cols_of_shards(sh, W_IN_COLS - 3 * D_MODEL, W_IN_COLS)], axis=-1),
                w_in_qs=_cols_of_shards(sh, 2 * D_MODEL, 2 * D_MODEL + W_QS),
                w_in_qd=_cols_of_shards(sh, 2 * D_MODEL + W_QS, 2 * D_MODEL + W_QS + W_QD))


def _late_weights(g_fi, g_rows):
    p = dict(w_ffn_in=jnp.concatenate([g_fi[k] for k in range(N_DEV)], axis=-1))
    off = 0
    for k, n in zip(ROW_SHARDED, ROW_COUNTS):
        t = g_rows[:, off:off + n]
        if k == "w_branch":
            p[k] = jnp.transpose(t.reshape(N_DEV, 3, n // 3, D_MODEL), (1, 0, 2, 3)).reshape(3, -1, D_MODEL)
        else:
            p[k] = t.reshape(-1, D_MODEL)
        off += n
    return p


def _w_in_slots(g):
    shard = W_IN_COLS // N_DEV
    return jnp.stack([_cols_of_w_in(g, k * shard, (k + 1) * shard) for k in range(N_DEV)]).astype(BF16)


def _late_slots(g):
    shard = g["w_ffn_in"].shape[-1] // N_DEV
    s_fi = jnp.stack([g["w_ffn_in"][:, k * shard:(k + 1) * shard] for k in range(N_DEV)]).astype(BF16)
    rows = jnp.concatenate([jnp.transpose(g["w_branch"].reshape(3, N_DEV, -1, D_MODEL), (1, 0, 2, 3)).reshape(N_DEV, -1, D_MODEL),
                            g["w_out"].reshape(N_DEV, -1, D_MODEL), g["w_ffn_out"].reshape(N_DEV, -1, D_MODEL)], axis=1).astype(BF16)
    return s_fi, rows


def _pad_rows(flat, tile_rows):
    n = flat.shape[0]
    per = tile_rows * LANES
    total = -(-n // per) * per
    return jnp.pad(flat, (0, total - n)).reshape(-1, LANES)


def kernel(x, w_in, conv_w, conv_b, w_rg, b_rg, w_ig, b_ig, lru_lambda, sinks, w_branch, w_out, ln1_g, ln1_b, w_ffn_in, w_ffn_out, ln2_g, ln2_b, loss_target, m_w_in, m_conv_w, m_conv_b, m_w_rg, m_b_rg, m_w_ig, m_b_ig, m_lru_lambda, m_sinks, m_w_branch, m_w_out, m_ln1_g, m_ln1_b, m_w_ffn_in, m_w_ffn_out, m_ln2_g, m_ln2_b, v_w_in, v_conv_w, v_conv_b, v_w_rg, v_b_rg, v_w_ig, v_b_ig, v_lru_lambda, v_sinks, v_w_branch, v_w_out, v_ln1_g, v_ln1_b, v_w_ffn_in, v_w_ffn_out, v_ln2_g, v_ln2_b):
    w = dict(w_in=w_in, conv_w=conv_w, conv_b=conv_b, w_rg=w_rg, b_rg=b_rg, w_ig=w_ig, b_ig=b_ig, lru_lambda=lru_lambda, sinks=sinks,
             w_branch=w_branch, w_out=w_out, ln1_g=ln1_g, ln1_b=ln1_b, w_ffn_in=w_ffn_in, w_ffn_out=w_ffn_out, ln2_g=ln2_g, ln2_b=ln2_b)
    m = dict(w_in=m_w_in, conv_w=m_conv_w, conv_b=m_conv_b, w_rg=m_w_rg, b_rg=m_b_rg, w_ig=m_w_ig, b_ig=m_b_ig, lru_lambda=m_lru_lambda,
             sinks=m_sinks, w_branch=m_w_branch, w_out=m_w_out, ln1_g=m_ln1_g, ln1_b=m_ln1_b, w_ffn_in=m_w_ffn_in, w_ffn_out=m_w_ffn_out,
             ln2_g=m_ln2_g, ln2_b=m_ln2_b)
    v = dict(w_in=v_w_in, conv_w=v_conv_w, conv_b=v_conv_b, w_rg=v_w_rg, b_rg=v_b_rg, w_ig=v_w_ig, b_ig=v_b_ig, lru_lambda=v_lru_lambda,
             sinks=v_sinks, w_branch=v_w_branch, w_out=v_w_out, ln1_g=v_ln1_g, ln1_b=v_ln1_b, w_ffn_in=v_w_ffn_in, w_ffn_out=v_w_ffn_out,
             ln2_g=v_ln2_g, ln2_b=v_ln2_b)
    order = ["w_in", "conv_w", "conv_b", "w_rg", "b_rg", "w_ig", "b_ig", "lru_lambda", "sinks", "w_branch", "w_out", "ln1_g", "ln1_b",
             "w_ffn_in", "w_ffn_out", "ln2_g", "ln2_b"]
    me = 4 * lax.axis_index("x") + 2 * lax.axis_index("y") + lax.axis_index("c")

    names = ("w_in", "w_ffn_in", "w_rows")
    shards = _layer_shards(w)
    first_w_in = exchange("gather", shards[0][0], name="gather_w_in")
    cw = exchange("gather", conv_w.reshape(-1, LANES), name="gather_conv_w")
    conv_w_full = jnp.moveaxis(cw.reshape(N_DEV, DEPTH, CONV_WIDTH, LANES), 0, 2).reshape(DEPTH, CONV_WIDTH, D_MODEL)

    def layer_params(l, gathered_w_in):
        return {**_w_in_weights(gathered_w_in), **{k: w[k][l] for k in SMALL}, "conv_w": conv_w_full[l]}

    loss_local, grad_x, grads, received, w_in_slots = local_step(x, loss_target, layer_params, shards, first_w_in)
    loss = lax.psum(loss_local, ("x", "y", "c"))
    received[0][0] = exchange("a2a", w_in_slots, name="exchange_g_w_in")

    sums = [[sum_slots(t, name=f"sum_g_{n}") for t, n in zip(received[l], names)] for l in range(DEPTH)]
    g_final = {"w_in": jnp.stack([sums[l][0] for l in range(DEPTH)]), "w_ffn_in": jnp.stack([sums[l][1] for l in range(DEPTH)])}
    off = 0
    for k, n in zip(ROW_SHARDED, ROW_COUNTS):
        g_final[k] = jnp.stack([sums[l][2][off:off + n] for l in range(DEPTH)]).reshape(w[k].shape)
        off += n
    grads = {k: jnp.stack([grads[l][k] for l in range(DEPTH)]) for k in list(SMALL) + ["conv_w"]}

    small_names = list(SMALL) + ["conv_w"]
    small_sizes = [grads[k].size for k in small_names]
    svec = _pad_rows(jnp.concatenate([grads[k].reshape(-1) for k in small_names]), SMALL_ROWS_TILE)
    ssum = sum_slots(exchange("gather", svec, name="gather_small_grads"), name="sum_small_grads")
    sflat, off = ssum.reshape(-1), 0
    for k, n in zip(small_names, small_sizes):
        g_final[k] = sflat[off:off + n].reshape(grads[k].shape)
        off += n
    g_final["conv_w"] = lax.dynamic_slice_in_dim(g_final["conv_w"], me * LANES, LANES, axis=2)

    delta, new_m, new_v = {}, {}, {}
    for k in list(BIG) + ["conv_w"]:
        cols = w[k].shape[-1]
        two_d = lambda t: t.reshape(-1, cols)
        d_, m_, v_ = adamw(two_d(w[k]), two_d(g_final[k]), two_d(m[k]), two_d(v[k]), name=f"adamw_{k}")
        delta[k], new_m[k], new_v[k] = d_.reshape(w[k].shape), m_.reshape(w[k].shape), v_.reshape(w[k].shape)
    pack_small = lambda dct: _pad_rows(jnp.concatenate([dct[k].reshape(-1) for k in SMALL]), SMALL_ROWS_TILE)
    d_, m_, v_ = adamw(pack_small(w), pack_small(g_final), pack_small(m), pack_small(v), name="adamw_small")
    off = 0
    for k in SMALL:
        n = w[k].size
        for dst, src in ((delta, d_), (new_m, m_), (new_v, v_)):
            dst[k] = src.reshape(-1)[off:off + n].reshape(w[k].shape)
        off += n
    return (loss, grad_x, *[g_final[k] for k in order], *[delta[k] for k in order], *[new_m[k] for k in order], *[new_v[k] for k in order])
```

```python
import functools
import math

import jax
import jax.numpy as jnp
from jax import lax
from jax.experimental import pallas as pl
from jax.experimental.pallas import tpu as pltpu

F32 = jnp.float32
BF16 = jnp.bfloat16

N_DEV = 8
DEPTH = 4
D_MODEL = 1024
HEAD_DIM = 64
LANES = 128
N_HEADS = D_MODEL // HEAD_DIM
SWA_KV_HEADS = 4
ATT_BLOCK = 128
DILATIONS = (1, 4, 16)
CONV_WIDTH = 4
LRU_C = 8.0
FF_HIDDEN = 2816
ALPHA = (2.0 * DEPTH) ** 0.25
LN_EPS = 1e-5
NEG_INF = -1e30
W_F = 5 * D_MODEL
W_QS = D_MODEL + 2 * SWA_KV_HEADS * HEAD_DIM
W_QD = 3 * D_MODEL

ADAM_LR = 0.001
ADAM_B1 = 0.9
ADAM_B2 = 0.999
ADAM_EPS = 1e-08
ADAM_WD = 0.01
ADAM_STEP = 10

VMEM_LIMIT = 56 * 1024 * 1024
MATMUL_BLOCK_BYTES = 40 * 1024 * 1024
MESH = pl.DeviceIdType.MESH


def _pick(n, cands):
    for c in cands:
        if n % c == 0:
            return c
    raise ValueError(f"no tile for {n} among {cands}")


def _params(sem):
    return pltpu.CompilerParams(dimension_semantics=sem, vmem_limit_bytes=VMEM_LIMIT)


def _tile(n, cap):
    best = None
    for t in range(LANES, cap + 1, LANES):
        if n % t == 0:
            best = t
    assert best is not None, (n, cap)
    return best


def matmul(a, b, *, name, trans_a=False, trans_b=False, out_dtype=F32, add=None, add_scale=1.0):
    if trans_a:
        k_dim, m_dim = a.shape
    else:
        m_dim, k_dim = a.shape
    n_dim = b.shape[0] if trans_b else b.shape[1]
    assert (b.shape[1] if trans_b else b.shape[0]) == k_dim
    tn = _tile(n_dim, 1408)
    tm, tk = _tile(m_dim, 1024), _tile(k_dim, 1408)
    for cand in (1024, 512, 256):
        ctm = _tile(m_dim, cand)
        blocks = 2 * (ctm * k_dim * a.dtype.itemsize + tn * k_dim * b.dtype.itemsize + ctm * tn * jnp.dtype(out_dtype).itemsize
                      + (ctm * tn * add.dtype.itemsize if add is not None else 0))
        if blocks <= MATMUL_BLOCK_BYTES:
            tm, tk = ctm, k_dim
            break
    nk = k_dim // tk
    dims = (((0 if trans_a else 1,), (1 if trans_b else 0,)), ((), ()))

    def body(*refs):
        if add is None:
            a_ref, b_ref, o_ref, acc_ref = refs
            add_ref = None
        else:
            a_ref, b_ref, add_ref, o_ref, acc_ref = refs
        k = pl.program_id(2)
        part = lax.dot_general(a_ref[...].astype(BF16), b_ref[...].astype(BF16), dims, preferred_element_type=F32)

        def finish(r):
            if add_ref is not None:
                r = r + add_scale * add_ref[...].astype(F32)
            o_ref[...] = r.astype(out_dtype)

        if nk == 1:
            finish(part)
        else:
            @pl.when(k == 0)
            def _():
                acc_ref[...] = part

            @pl.when((k > 0) & (k < nk - 1))
            def _():
                acc_ref[...] += part

            @pl.when(k == nk - 1)
            def _():
                finish(acc_ref[...] + part)

    a_spec = pl.BlockSpec((tk, tm), lambda i, j, k: (k, i)) if trans_a else pl.BlockSpec((tm, tk), lambda i, j, k: (i, k))
    b_spec = pl.BlockSpec((tn, tk), lambda i, j, k: (j, k)) if trans_b else pl.BlockSpec((tk, tn), lambda i, j, k: (k, j))
    in_specs = [a_spec, b_spec]
    args = [a, b]
    if add is not None:
        in_specs.append(pl.BlockSpec((tm, tn), lambda i, j, k: (i, j)))
        args.append(add)
    return pl.pallas_call(
        body,
        out_shape=jax.ShapeDtypeStruct((m_dim, n_dim), out_dtype),
        grid=(m_dim // tm, n_dim // tn, nk),
        in_specs=in_specs,
        out_specs=pl.BlockSpec((tm, tn), lambda i, j, k: (i, j)),
        scratch_shapes=[pltpu.VMEM((tm, tn) if nk > 1 else (8, LANES), F32)],
        compiler_params=_params(("parallel", "parallel", "arbitrary")),
        name=name,
    )(*args)


def ln_fwd(x, a, w, g, b, *, name):
    t_dim, d = x.shape
    k_dim = a.shape[1]
    tr = _tile(t_dim, 512)

    def body(x_ref, a_ref, w_ref, g_ref, b_ref, y_ref, yb_ref, z_ref):
        z = ALPHA * x_ref[...] + jnp.dot(a_ref[...], w_ref[...], preferred_element_type=F32)
        mu = jnp.mean(z, axis=-1, keepdims=True)
        zc = z - mu
        var = jnp.mean(zc * zc, axis=-1, keepdims=True)
        y = zc * lax.rsqrt(var + LN_EPS) * g_ref[...] + b_ref[...]
        y_ref[...] = y
        yb_ref[...] = y.astype(BF16)
        z_ref[...] = z

    row = pl.BlockSpec((tr, d), lambda i: (i, 0))
    vec = pl.BlockSpec((1, d), lambda i: (0, 0))
    return pl.pallas_call(
        body,
        out_shape=(jax.ShapeDtypeStruct((t_dim, d), F32), jax.ShapeDtypeStruct((t_dim, d), BF16), jax.ShapeDtypeStruct((t_dim, d), F32)),
        grid=(t_dim // tr,),
        in_specs=[row, pl.BlockSpec((tr, k_dim), lambda i: (i, 0)), pl.BlockSpec((k_dim, d), lambda i: (0, 0)), vec, vec],
        out_specs=(row, row, row),
        compiler_params=_params(("parallel",)),
        name=name,
    )(x, a, w, g.reshape(1, d), b.reshape(1, d))


def ln_bwd(dy, z, g, *, name, a=None, w=None, dy_scale=1.0):
    t_dim, d = dy.shape
    tr = _pick(t_dim, (256, 128, 8))

    def body(*refs):
        if a is None:
            dy_ref, z_ref, g_ref, dz_ref, dzb_ref, dg_ref, db_ref = refs
        else:
            dy_ref, a_ref, w_ref, z_ref, g_ref, dz_ref, dzb_ref, dg_ref, db_ref = refs

        @pl.when(pl.program_id(0) == 0)
        def _():
            dg_ref[...] = jnp.zeros_like(dg_ref)
            db_ref[...] = jnp.zeros_like(db_ref)

        z = z_ref[...]
        dyv = dy_scale * dy_ref[...]
        if a is not None:
            dyv = dyv + lax.dot_general(a_ref[...], w_ref[...], NT, preferred_element_type=F32)
        mu = jnp.mean(z, axis=-1, keepdims=True)
        zc = z - mu
        var = jnp.mean(zc * zc, axis=-1, keepdims=True)
        rstd = lax.rsqrt(var + LN_EPS)
        xhat = zc * rstd
        dxhat = dyv * g_ref[...]
        m1 = jnp.mean(dxhat, axis=-1, keepdims=True)
        m2 = jnp.mean(dxhat * xhat, axis=-1, keepdims=True)
        dz = rstd * (dxhat - m1 - xhat * m2)
        dz_ref[...] = dz
        dzb_ref[...] = dz.astype(BF16)
        dg_ref[...] += jnp.sum(dyv * xhat, axis=0, keepdims=True)
        db_ref[...] += jnp.sum(dyv, axis=0, keepdims=True)

    row = pl.BlockSpec((tr, d), lambda i: (i, 0))
    vec = pl.BlockSpec((1, d), lambda i: (0, 0))
    in_specs, args = [row], [dy]
    if a is not None:
        k_dim = a.shape[1]
        in_specs += [pl.BlockSpec((tr, k_dim), lambda i: (i, 0)), pl.BlockSpec((d, k_dim), lambda i: (0, 0))]
        args += [a, w]
    return pl.pallas_call(
        body,
        out_shape=(jax.ShapeDtypeStruct((t_dim, d), F32), jax.ShapeDtypeStruct((t_dim, d), BF16),
                   jax.ShapeDtypeStruct((1, d), F32), jax.ShapeDtypeStruct((1, d), F32)),
        grid=(t_dim // tr,),
        in_specs=in_specs + [row, vec],
        out_specs=(row, row, vec, vec),
        compiler_params=_params(("arbitrary",)),
        name=name,
    )(*args, z, g.reshape(1, d))


def loss_head(y, target, *, name):
    t_dim, d = y.shape
    tr = _pick(t_dim, (256, 128, 8))

    def body(y_ref, t_ref, dy_ref, sq_ref):
        @pl.when(pl.program_id(0) == 0)
        def _():
            sq_ref[...] = jnp.zeros_like(sq_ref)

        diff = y_ref[...] - t_ref[...]
        dy_ref[...] = diff / d
        sq_ref[...] += jnp.sum(diff * diff, axis=0, keepdims=True)

    row = pl.BlockSpec((tr, d), lambda i: (i, 0))
    vec = pl.BlockSpec((1, d), lambda i: (0, 0))
    return pl.pallas_call(
        body,
        out_shape=(jax.ShapeDtypeStruct((t_dim, d), F32), jax.ShapeDtypeStruct((1, d), F32)),
        grid=(t_dim // tr,),
        in_specs=[row, row],
        out_specs=(row, vec),
        compiler_params=_params(("arbitrary",)),
        name=name,
    )(y, target)


def _sigmoid(x):
    return 0.5 * jnp.tanh(0.5 * x) + 0.5


def ffn_in_swiglu(x, w, *, name):
    t_dim, d = x.shape
    f = w.shape[1] // 2
    tm, tn = _tile(t_dim, 1024), _tile(f, 1408)
    nf = f // tn

    def body(x_ref, w1_ref, w3_ref, h1_ref, h3_ref, act_ref):
        xv = x_ref[...]
        h1 = jnp.dot(xv, w1_ref[...], preferred_element_type=F32)
        h3 = jnp.dot(xv, w3_ref[...], preferred_element_type=F32)
        h1_ref[...] = h1.astype(BF16)
        h3_ref[...] = h3.astype(BF16)
        act_ref[...] = (h1 * _sigmoid(h1) * h3).astype(BF16)

    out = pl.BlockSpec((tm, tn), lambda i, j: (i, j))
    return pl.pallas_call(
        body,
        out_shape=(jax.ShapeDtypeStruct((t_dim, f), BF16),) * 3,
        grid=(t_dim // tm, nf),
        in_specs=[pl.BlockSpec((tm, d), lambda i, j: (i, 0)), pl.BlockSpec((d, tn), lambda i, j: (0, j)),
                  pl.BlockSpec((d, tn), lambda i, j: (0, j + nf))],
        out_specs=(out, out, out),
        compiler_params=_params(("parallel", "parallel")),
        name=name,
    )(x, w, w)


def swiglu_bwd(dz, w_ffn_out, h1, h3, *, name):
    t_dim, d = dz.shape
    f = h1.shape[1]
    tr = _tile(t_dim, 512)

    def body(dz_ref, w_ref, h1_ref, h3_ref, dh_ref):
        da = lax.dot_general(dz_ref[...], w_ref[...], NT, preferred_element_type=F32)
        h1 = h1_ref[...].astype(F32)
        sg = _sigmoid(h1)
        dh_ref[:, :f] = (da * h3_ref[...].astype(F32) * sg * (1.0 + h1 * (1.0 - sg))).astype(BF16)
        dh_ref[:, f:] = (da * h1 * sg).astype(BF16)

    wide = pl.BlockSpec((tr, f), lambda i: (i, 0))
    return pl.pallas_call(
        body,
        out_shape=jax.ShapeDtypeStruct((t_dim, 2 * f), BF16),
        grid=(t_dim // tr,),
        in_specs=[pl.BlockSpec((tr, d), lambda i: (i, 0)), pl.BlockSpec((f, d), lambda i: (0, 0)), wide, wide],
        out_specs=pl.BlockSpec((tr, 2 * f), lambda i: (i, 0)),
        compiler_params=_params(("parallel",)),
        name=name,
    )(dz, w_ffn_out, h1, h3)


def branch_merge(ys, w_branch, proj_f, *, name):
    t_dim, d = ys[0].shape
    tm = _tile(t_dim, 512)

    def body(y0, y1, y2, w_ref, g0, g1, g2, m_ref, b0, b1, b2):
        acc = None
        for n, (y, g, b) in enumerate(((y0, g0, b0), (y1, g1, b1), (y2, g2, b2))):
            br = jnp.dot(y[...], w_ref[n], preferred_element_type=F32)
            b[...] = br.astype(BF16)
            t = _sigmoid(g[...].astype(F32)) * br
            acc = t if acc is None else acc + t
        m_ref[...] = acc.astype(BF16)

    row = pl.BlockSpec((tm, d), lambda i: (i, 0))
    gate = [pl.BlockSpec((tm, d), functools.partial(lambda n, i: (i, 2 + n), n)) for n in range(3)]
    merged, *br = pl.pallas_call(
        body,
        out_shape=(jax.ShapeDtypeStruct((t_dim, d), BF16),) * 4,
        grid=(t_dim // tm,),
        in_specs=[row, row, row, pl.BlockSpec((3, d, d), lambda i: (0, 0, 0))] + gate,
        out_specs=(row, row, row, row),
        compiler_params=_params(("parallel",)),
        name=name,
    )(*ys, w_branch, proj_f, proj_f, proj_f)
    return merged, br


def merge_bwd(dz, w_out, proj_f, br, *, name):
    t_dim, d = dz.shape
    tr = _tile(t_dim, 512)

    def body(dz_ref, w_ref, g0, g1, g2, b0, b1, b2, d0, d1, d2, dg_ref):
        dm = lax.dot_general(dz_ref[...], w_ref[...], NT, preferred_element_type=F32)
        for n, (g, b, o) in enumerate(((g0, b0, d0), (g1, b1, d1), (g2, b2, d2))):
            sg = _sigmoid(g[...].astype(F32))
            o[...] = (dm * sg).astype(BF16)
            dg_ref[:, n * d:(n + 1) * d] = (dm * b[...].astype(F32) * sg * (1.0 - sg)).astype(BF16)

    row = pl.BlockSpec((tr, d), lambda i: (i, 0))
    gate = [pl.BlockSpec((tr, d), functools.partial(lambda n, i: (i, 2 + n), n)) for n in range(3)]
    return pl.pallas_call(
        body,
        out_shape=(jax.ShapeDtypeStruct((t_dim, d), BF16),) * 3 + (jax.ShapeDtypeStruct((t_dim, 3 * d), BF16),),
        grid=(t_dim // tr,),
        in_specs=[row, pl.BlockSpec((d, d), lambda i: (0, 0))] + gate + [row, row, row],
        out_specs=(row, row, row, pl.BlockSpec((tr, 3 * d), lambda i: (i, 0))),
        compiler_params=_params(("parallel",)),
        name=name,
    )(dz, w_out, proj_f, proj_f, proj_f, *br)


GELU_C = math.sqrt(2.0 / math.pi)
PAD = 8
SCAN_TILES = 8


def _gelu(x):
    return 0.5 * x * (1.0 + jnp.tanh(GELU_C * (x + 0.044715 * x * x * x)))


def _gelu_grad(x):
    t = jnp.tanh(GELU_C * (x + 0.044715 * x * x * x))
    return 0.5 * (1.0 + t) + 0.5 * x * (1.0 - t * t) * GELU_C * (1.0 + 3.0 * 0.044715 * x * x)


def _neg_expm1(x, exp_x):
    series = -x * (1.0 + x * (0.5 + x * (1.0 / 6.0)))
    return jnp.where(x > -0.02, series, 1.0 - exp_x)


def _lru_gates(xv, cw_ref, cb_ref, wr_ref, wi_ref, br_ref, bi_ref, lam_ref, pad_ref, s_len):
    pad_ref[pl.ds(0, PAD), :] = jnp.zeros((PAD, LANES), F32)
    pad_ref[pl.ds(PAD, s_len), :] = xv
    xc = cb_ref[...] + jnp.zeros((s_len, LANES), F32)
    for j in range(CONV_WIDTH):
        xc = xc + pad_ref[pl.ds(PAD - (CONV_WIDTH - 1) + j, s_len), :] * cw_ref[pl.ds(j, 1), :]
    xcb = xc.astype(BF16)
    r = _sigmoid(jnp.dot(xcb, wr_ref[0].astype(BF16), preferred_element_type=F32) + br_ref[...])
    i = _sigmoid(jnp.dot(xcb, wi_ref[0].astype(BF16), preferred_element_type=F32) + bi_ref[...])
    nl = -lam_ref[...]
    sp = jnp.maximum(nl, 0.0) + jnp.log(1.0 + jnp.exp(-jnp.abs(nl)))
    log_a = -LRU_C * r * sp
    a = jnp.exp(log_a)
    mult = jnp.sqrt(_neg_expm1(2.0 * log_a, a * a))
    return xc, r, i, sp, a, mult


def _tile_scan(a, b, row, reverse):
    for s in (1, 2, 4):
        if reverse:
            a_sh = pltpu.roll(a, 8 - s, 0)
            b_sh = pltpu.roll(b, 8 - s, 0)
            m = row + s <= 7
        else:
            a_sh = pltpu.roll(a, s, 0)
            b_sh = pltpu.roll(b, s, 0)
            m = row >= s
        b = jnp.where(m, a * b_sh + b, b)
        a = jnp.where(m, a * a_sh, a)
    return a, b


def lru_fwd(proj_f, conv_w, conv_b, wr_bd, wi_bd, b_rg, b_ig, lam, *, name, carried=None):
    bsz, s_len, _ = proj_f.shape
    d = D_MODEL
    ncb = d // LANES
    n_tiles = s_len // 8

    def body(x_ref, g_ref, cw_ref, cb_ref, wr_ref, wi_ref, br_ref, bi_ref, lam_ref, y_ref, h_ref, pad_ref, a_s, b_s):
        xc, r, i, sp, a, mult = _lru_gates(x_ref[0].astype(F32), cw_ref, cb_ref, wr_ref, wi_ref, br_ref, bi_ref, lam_ref, pad_ref, s_len)
        a_s[...] = a
        b_s[...] = mult * (i * xc)
        row = lax.broadcasted_iota(jnp.int32, (8, LANES), 0)

        def tiles(t, carry):
            starts = [pl.multiple_of((t * SCAN_TILES + u) * 8, 8) for u in range(SCAN_TILES)]
            local = [_tile_scan(a_s[pl.ds(i0, 8), :], b_s[pl.ds(i0, 8), :], row, False) for i0 in starts]
            for i0, (ac, hl) in zip(starts, local):
                h = hl + ac * carry
                h_ref[0, pl.ds(i0, 8), :] = h
                carry = jnp.broadcast_to(h[7:8, :], (8, LANES))
            return carry

        lax.fori_loop(0, n_tiles // SCAN_TILES, tiles, jnp.zeros((8, LANES), F32))
        y_ref[0] = (h_ref[0] * _gelu(g_ref[0].astype(F32))).astype(BF16)

    slab = lambda off: pl.BlockSpec((1, s_len, LANES), functools.partial(lambda o, c, b: (b, 0, o + c), off))
    vec = pl.BlockSpec((1, LANES), lambda c, b: (0, c))
    mat = pl.BlockSpec((1, LANES, LANES), lambda c, b: (c, 0, 0))
    out = pl.BlockSpec((1, s_len, LANES), lambda c, b: (b, 0, c))
    return call_with_exchange(
        body, carried,
        out_shape=(jax.ShapeDtypeStruct((bsz, s_len, d), BF16), jax.ShapeDtypeStruct((bsz, s_len, d), F32)),
        grid=(ncb, bsz),
        in_specs=[slab(0), slab(ncb), pl.BlockSpec((CONV_WIDTH, LANES), lambda c, b: (0, c)), vec, mat, mat, vec, vec, vec],
        out_specs=(out, out),
        scratch_shapes=[pltpu.VMEM((s_len + 2 * PAD, LANES), F32), pltpu.VMEM((s_len, LANES), F32), pltpu.VMEM((s_len, LANES), F32)],
        name=name,
        args=(proj_f, proj_f, conv_w, conv_b.reshape(1, d), wr_bd, wi_bd, b_rg.reshape(1, d), b_ig.reshape(1, d), lam.reshape(1, d)))


def lru_bwd(dy, proj_f, h, conv_w, conv_b, wr_bd, wi_bd, wr_bd_t, wi_bd_t, b_rg, b_ig, lam, *, name, carried=None):
    bsz, s_len, _ = proj_f.shape
    d = D_MODEL
    ncb = d // LANES
    n_tiles = s_len // 8

    def body(dy_ref, x_ref, g_ref, h_ref, cw_ref, cb_ref, wr_ref, wi_ref, wrt_ref, wit_ref, br_ref, bi_ref, lam_ref,
             dx_ref, dg_ref, dcw_ref, dcb_ref, dbr_ref, dbi_ref, dlam_ref, dwr_ref, dwi_ref, pad_ref, a_s, b_s, l_s):
        @pl.when(pl.program_id(1) == 0)
        def _():
            for ref in (dcw_ref, dcb_ref, dbr_ref, dbi_ref, dlam_ref, dwr_ref, dwi_ref):
                ref[...] = jnp.zeros_like(ref)

        xc, r, i, sp, a, mult = _lru_gates(x_ref[0].astype(F32), cw_ref, cb_ref, wr_ref, wi_ref, br_ref, bi_ref, lam_ref, pad_ref, s_len)
        gate = g_ref[0].astype(F32)
        hv = h_ref[0]
        dyv = dy_ref[0].astype(F32)
        dg_ref[0] = (dyv * hv * _gelu_grad(gate)).astype(BF16)
        b_s[...] = dyv * _gelu(gate)
        l_s[pl.ds(0, s_len), :] = a
        l_s[pl.ds(s_len, PAD), :] = jnp.zeros((PAD, LANES), F32)
        a_s[...] = l_s[pl.ds(1, s_len), :]
        row = lax.broadcasted_iota(jnp.int32, (8, LANES), 0)

        def tiles(t, carry):
            starts = [pl.multiple_of((n_tiles - 1 - (t * SCAN_TILES + u)) * 8, 8) for u in range(SCAN_TILES)]
            local = [_tile_scan(a_s[pl.ds(i0, 8), :], b_s[pl.ds(i0, 8), :], row, True) for i0 in starts]
            for i0, (ac, ll) in zip(starts, local):
                lmb = ll + ac * carry
                b_s[pl.ds(i0, 8), :] = lmb
                carry = jnp.broadcast_to(lmb[0:1, :], (8, LANES))
            return carry

        lax.fori_loop(0, n_tiles // SCAN_TILES, tiles, jnp.zeros((8, LANES), F32))
        lmb = b_s[...]
        l_s[pl.ds(0, PAD), :] = jnp.zeros((PAD, LANES), F32)
        l_s[pl.ds(PAD, s_len), :] = hv
        h_prev = l_s[pl.ds(PAD - 1, s_len), :]
        da = lmb * h_prev
        dmult = lmb * (i * xc)
        di = lmb * mult * xc
        dxc = lmb * mult * i
        dlog_a = da * a - dmult * a * a / mult
        dr = -LRU_C * sp * dlog_a
        dsp = jnp.sum(-LRU_C * r * dlog_a, axis=0, keepdims=True)
        dlam_ref[...] += dsp * (-_sigmoid(-lam_ref[...]))
        dpr = dr * r * (1.0 - r)
        dpi = di * i * (1.0 - i)
        dprb = dpr.astype(BF16)
        dpib = dpi.astype(BF16)
        xcb = xc.astype(BF16)
        dbr_ref[...] += jnp.sum(dpr, axis=0, keepdims=True)
        dbi_ref[...] += jnp.sum(dpi, axis=0, keepdims=True)
        tn = (((0,), (0,)), ((), ()))
        dwr_ref[0] += lax.dot_general(xcb, dprb, tn, preferred_element_type=F32)
        dwi_ref[0] += lax.dot_general(xcb, dpib, tn, preferred_element_type=F32)
        dxc = (dxc + jnp.dot(dprb, wrt_ref[0].astype(BF16), preferred_element_type=F32)
               + jnp.dot(dpib, wit_ref[0].astype(BF16), preferred_element_type=F32))
        dcb_ref[...] += jnp.sum(dxc, axis=0, keepdims=True)
        for j in range(CONV_WIDTH):
            dcw_ref[pl.ds(j, 1), :] += jnp.sum(dxc * pad_ref[pl.ds(PAD - (CONV_WIDTH - 1) + j, s_len), :], axis=0, keepdims=True)
        l_s[pl.ds(0, s_len), :] = dxc
        l_s[pl.ds(s_len, PAD), :] = jnp.zeros((PAD, LANES), F32)
        dx = jnp.zeros((s_len, LANES), F32)
        for j in range(CONV_WIDTH):
            dx = dx + l_s[pl.ds(CONV_WIDTH - 1 - j, s_len), :] * cw_ref[pl.ds(j, 1), :]
        dx_ref[0] = dx.astype(BF16)

    slab = lambda off: pl.BlockSpec((1, s_len, LANES), functools.partial(lambda o, c, b: (b, 0, o + c), off))
    vec = pl.BlockSpec((1, LANES), lambda c, b: (0, c))
    mat = pl.BlockSpec((1, LANES, LANES), lambda c, b: (c, 0, 0))
    cw = pl.BlockSpec((CONV_WIDTH, LANES), lambda c, b: (0, c))
    out = pl.BlockSpec((1, s_len, LANES), lambda c, b: (b, 0, c))
    vshape = jax.ShapeDtypeStruct((1, d), F32)
    mshape = jax.ShapeDtypeStruct((ncb, LANES, LANES), F32)
    return call_with_exchange(
        body, carried,
        out_shape=(jax.ShapeDtypeStruct((bsz, s_len, d), BF16),) * 2
        + (jax.ShapeDtypeStruct((CONV_WIDTH, d), F32), vshape, vshape, vshape, vshape, mshape, mshape),
        grid=(ncb, bsz),
        in_specs=[out, slab(0), slab(ncb), out, cw, vec, mat, mat, mat, mat, vec, vec, vec],
        out_specs=(out, out, cw, vec, vec, vec, vec, mat, mat),
        scratch_shapes=[pltpu.VMEM((s_len + 2 * PAD, LANES), F32), pltpu.VMEM((s_len, LANES), F32), pltpu.VMEM((s_len, LANES), F32),
                        pltpu.VMEM((s_len + 2 * PAD, LANES), F32)],
        name=name,
        args=(dy, proj_f, proj_f, h, conv_w, conv_b.reshape(1, d), wr_bd, wi_bd, wr_bd_t, wi_bd_t,
              b_rg.reshape(1, d), b_ig.reshape(1, d), lam.reshape(1, d)))


NT = (((1,), (1,)), ((), ()))
TN = (((0,), (0,)), ((), ()))
ATT_SCALE = HEAD_DIM ** -0.5


def _kv_place(head, n_kv_heads):
    kv = head // (N_HEADS // n_kv_heads)
    return kv // 2, kv % 2


def _band_mask(n, single):
    nk = ATT_BLOCK if single else 2 * ATT_BLOCK
    qi = lax.broadcasted_iota(jnp.int32, (2 * ATT_BLOCK, nk), 0) % ATT_BLOCK
    kj = lax.broadcasted_iota(jnp.int32, (2 * ATT_BLOCK, nk), 1)
    if single:
        return qi >= kj
    rel = qi + ATT_BLOCK - kj
    return (rel >= 0) & (rel <= ATT_BLOCK) & ((n > 0) | (kj >= ATT_BLOCK))


def _lane_halves():
    lane = lax.broadcasted_iota(jnp.int32, (1, LANES), 1)
    return lane < HEAD_DIM


def _stack_heads(t2, kh):
    first = _lane_halves()
    parts = []
    for a in range(2):
        ta = jnp.where(first if a == 0 else ~first, t2, jnp.zeros_like(t2))
        if a != kh[a]:
            ta = pltpu.roll(ta, HEAD_DIM, 1)
        parts.append(ta)
    return jnp.concatenate(parts, axis=0)


def _fold_heads(t, kh):
    t0, t1 = t[:ATT_BLOCK], t[ATT_BLOCK:]
    if t.shape[1] == LANES:
        if kh[0] != 0:
            t0 = pltpu.roll(t0, HEAD_DIM, 1)
        if kh[1] != 1:
            t1 = pltpu.roll(t1, HEAD_DIM, 1)
    return jnp.where(_lane_halves(), t0, t1)


def _rows_of_heads(t2):
    return jnp.concatenate([t2[:, 0:1], t2[:, HEAD_DIM:HEAD_DIM + 1]], axis=0)


PAIRS_AT_ONCE = 4


def _fill_bias(bias2_ref, bias1_ref=None):
    for i in range(2):
        bias2_ref[i] = jnp.where(_band_mask(i, False), 0.0, NEG_INF)
    if bias1_ref is not None:
        bias1_ref[...] = jnp.where(_band_mask(0, True), 0.0, NEG_INF)


def _pairs_fwd(items):
    ss = [lax.dot_general(_stack_heads(q2 * ATT_SCALE, kh), kk, NT, preferred_element_type=F32) + bias
          for q2, kk, _, bias, kh, _ in items]
    ps, ms, ls = [], [], []
    for s, (_, _, _, _, _, sink_col) in zip(ss, items):
        m = jnp.max(s, axis=-1, keepdims=True)
        if sink_col is not None:
            m = jnp.maximum(m, sink_col)
        p = jnp.exp(s - m)
        l = jnp.sum(p, axis=-1, keepdims=True)
        if sink_col is not None:
            l = l + jnp.exp(sink_col - m)
        ps.append(p.astype(BF16))
        ms.append(m)
        ls.append(l)
    pvs = [jnp.dot(p, it[2], preferred_element_type=F32) for p, it in zip(ps, items)]
    return list(zip(pvs, ms, ls))


def _pairs_bwd(items):
    first = _lane_halves()
    pre = []
    for q2, kk, vv, do2, o2, lse2, bias, kh in items:
        dd = do2 * o2
        dsum = jnp.concatenate([jnp.sum(jnp.where(first, dd, 0.0), axis=-1, keepdims=True),
                                jnp.sum(jnp.where(first, 0.0, dd), axis=-1, keepdims=True)], axis=0)
        qs = _stack_heads(q2 * ATT_SCALE, kh)
        dos = _stack_heads(do2.astype(BF16), kh)
        s = lax.dot_general(qs, kk, NT, preferred_element_type=F32) + bias
        dp = lax.dot_general(dos, vv, NT, preferred_element_type=F32)
        pre.append((qs, dos, s, dp, dsum))
    mid = []
    for (qs, dos, s, dp, dsum), it in zip(pre, items):
        p = jnp.exp(s - _rows_of_heads(it[5]))
        mid.append((p.astype(BF16), (p * (dp - dsum)).astype(BF16)))
    out = []
    for (pb, ds), (qs, dos, _, _, dsum), it in zip(mid, pre, items):
        dq = _fold_heads(jnp.dot(ds, it[1], preferred_element_type=F32), it[7]) * ATT_SCALE
        dk = lax.dot_general(ds, qs, TN, preferred_element_type=F32)
        dv = lax.dot_general(pb, dos, TN, preferred_element_type=F32)
        out.append((dq, dk, dv, dsum))
    return out


def swa_fwd(qkv, sinks, *, name, carried=None):
    bsz, s_len, width = qkv.shape
    ckv = SWA_KV_HEADS * HEAD_DIM
    nb = s_len // ATT_BLOCK
    kblk = D_MODEL // ckv

    def body(sink_ref, q_ref, kp_ref, kc_ref, vp_ref, vc_ref, o_ref, lse_ref, ob_ref, bias2):
        n = pl.program_id(1)
        _fill_bias(bias2)
        bias = bias2[jnp.minimum(n, 1)]
        kk = jnp.concatenate([kp_ref[0], kc_ref[0]], axis=0)
        vv = jnp.concatenate([vp_ref[0], vc_ref[0]], axis=0)
        top = lax.broadcasted_iota(jnp.int32, (2 * ATT_BLOCK, 1), 0) < ATT_BLOCK
        for hp0 in range(0, N_HEADS // 2, PAIRS_AT_ONCE):
            items, places = [], []
            for hp in range(hp0, hp0 + PAIRS_AT_ONCE):
                cols = slice(hp * LANES, (hp + 1) * LANES)
                kb, kh = _kv_place(2 * hp, SWA_KV_HEADS)
                kcols = slice(kb * LANES, (kb + 1) * LANES)
                sink_col = jnp.where(top, sink_ref[2 * hp], sink_ref[2 * hp + 1])
                items.append((q_ref[0, :, cols], kk[:, kcols], vv[:, kcols], bias, (kh, kh), sink_col))
                places.append((cols, (kh, kh)))
            for (pv, m, l), (cols, kh2) in zip(_pairs_fwd(items), places):
                o2 = _fold_heads(pv / l, kh2)
                o_ref[0, :, cols] = o2
                ob_ref[0, :, cols] = o2.astype(BF16)
                lse_ref[0, :, cols] = _fold_heads(m + jnp.log(l), kh2)

    prev = lambda n: jnp.maximum(n - 1, 0)
    out = pl.BlockSpec((1, ATT_BLOCK, D_MODEL), lambda b, n: (b, n, 0))
    sd = lambda dt: jax.ShapeDtypeStruct((bsz, s_len, D_MODEL), dt)
    return call_with_exchange(
        body, carried,
        out_shape=(sd(F32), sd(F32), sd(BF16)),
        grid=(bsz, nb),
        in_specs=[pl.BlockSpec(memory_space=pltpu.SMEM), out,
                  pl.BlockSpec((1, ATT_BLOCK, ckv), lambda b, n: (b, prev(n), kblk)),
                  pl.BlockSpec((1, ATT_BLOCK, ckv), lambda b, n: (b, n, kblk)),
                  pl.BlockSpec((1, ATT_BLOCK, ckv), lambda b, n: (b, prev(n), kblk + 1)),
                  pl.BlockSpec((1, ATT_BLOCK, ckv), lambda b, n: (b, n, kblk + 1))],
        out_specs=(out, out, out),
        scratch_shapes=[pltpu.VMEM((2, 2 * ATT_BLOCK, 2 * ATT_BLOCK), F32)],
        name=name,
        args=(sinks, qkv, qkv, qkv, qkv, qkv))


def swa_bwd(qkv, sinks, o, lse, do, *, name, carried=None):
    bsz, s_len, width = qkv.shape
    ckv = SWA_KV_HEADS * HEAD_DIM
    nb = s_len // ATT_BLOCK
    kblk = D_MODEL // ckv

    def body(sink_ref, q_ref, kp_ref, kc_ref, vp_ref, vc_ref, o_ref, lse_ref, do_ref, dq_ref, dk_ref, dv_ref, dsink_ref,
             dkk, dvv, ck, cv, bias2):
        n = pl.program_id(1)

        @pl.when((n == 0) & (pl.program_id(0) == 0))
        def _():
            dsink_ref[...] = jnp.zeros_like(dsink_ref)

        @pl.when(n < nb)
        def _():
            top = lax.broadcasted_iota(jnp.int32, (2 * ATT_BLOCK, 1), 0) < ATT_BLOCK
            lane = lax.broadcasted_iota(jnp.int32, dsink_ref.shape, 1)
            first_row = lax.broadcasted_iota(jnp.int32, dsink_ref.shape, 0) == 0
            _fill_bias(bias2)
            bias = bias2[jnp.minimum(n, 1)]
            kk = jnp.concatenate([kp_ref[0], kc_ref[0]], axis=0)
            vv = jnp.concatenate([vp_ref[0], vc_ref[0]], axis=0)
            dkk[...] = jnp.zeros_like(dkk)
            dvv[...] = jnp.zeros_like(dvv)
            for hp0 in range(0, N_HEADS // 2, PAIRS_AT_ONCE):
                items, places = [], []
                for hp in range(hp0, hp0 + PAIRS_AT_ONCE):
                    cols = slice(hp * LANES, (hp + 1) * LANES)
                    kb, kh = _kv_place(2 * hp, SWA_KV_HEADS)
                    kcols = slice(kb * LANES, (kb + 1) * LANES)
                    items.append((q_ref[0, :, cols], kk[:, kcols], vv[:, kcols], do_ref[0, :, cols], o_ref[0, :, cols],
                                  lse_ref[0, :, cols], bias, (kh, kh)))
                    places.append((cols, kcols, hp))
                for (dq, dk, dv, dsum), (cols, kcols, hp) in zip(_pairs_bwd(items), places):
                    dq_ref[0, :, cols] = dq.astype(BF16)
                    dkk[:, kcols] += dk
                    dvv[:, kcols] += dv
                    sink_col = jnp.where(top, sink_ref[2 * hp], sink_ref[2 * hp + 1])
                    t = -jnp.exp(sink_col - _rows_of_heads(lse_ref[0, :, cols])) * dsum
                    d0 = jnp.sum(t[:ATT_BLOCK], axis=0, keepdims=True)
                    d1 = jnp.sum(t[ATT_BLOCK:], axis=0, keepdims=True)
                    dsink_ref[...] += jnp.where(first_row & (lane == 2 * hp), d0, 0.0) + jnp.where(first_row & (lane == 2 * hp + 1), d1, 0.0)

        @pl.when((n >= 1) & (n < nb))
        def _():
            dk_ref[0] = (ck[...] + dkk[pl.ds(0, ATT_BLOCK), :]).astype(BF16)
            dv_ref[0] = (cv[...] + dvv[pl.ds(0, ATT_BLOCK), :]).astype(BF16)

        @pl.when(n == nb)
        def _():
            dk_ref[0] = ck[...].astype(BF16)
            dv_ref[0] = cv[...].astype(BF16)

        @pl.when(n < nb)
        def _():
            ck[...] = dkk[pl.ds(ATT_BLOCK, ATT_BLOCK), :]
            cv[...] = dvv[pl.ds(ATT_BLOCK, ATT_BLOCK), :]

    clamp = lambda n: jnp.minimum(n, nb - 1)
    prev = lambda n: jnp.maximum(n - 1, 0)
    row = pl.BlockSpec((1, ATT_BLOCK, D_MODEL), lambda b, n: (b, clamp(n), 0))
    kv_out = pl.BlockSpec((1, ATT_BLOCK, ckv), lambda b, n: (b, prev(n), 0))
    return call_with_exchange(
        body, carried,
        out_shape=(jax.ShapeDtypeStruct((bsz, s_len, D_MODEL), BF16), jax.ShapeDtypeStruct((bsz, s_len, ckv), BF16),
                   jax.ShapeDtypeStruct((bsz, s_len, ckv), BF16), jax.ShapeDtypeStruct((8, LANES), F32)),
        grid=(bsz, nb + 1),
        in_specs=[pl.BlockSpec(memory_space=pltpu.SMEM), row,
                  pl.BlockSpec((1, ATT_BLOCK, ckv), lambda b, n: (b, prev(clamp(n)), kblk)),
                  pl.BlockSpec((1, ATT_BLOCK, ckv), lambda b, n: (b, clamp(n), kblk)),
                  pl.BlockSpec((1, ATT_BLOCK, ckv), lambda b, n: (b, prev(clamp(n)), kblk + 1)),
                  pl.BlockSpec((1, ATT_BLOCK, ckv), lambda b, n: (b, clamp(n), kblk + 1)),
                  row, row, row],
        out_specs=(row, kv_out, kv_out, pl.BlockSpec((8, LANES), lambda b, n: (0, 0))),
        scratch_shapes=[pltpu.VMEM((2 * ATT_BLOCK, ckv), F32), pltpu.VMEM((2 * ATT_BLOCK, ckv), F32),
                        pltpu.VMEM((ATT_BLOCK, ckv), F32), pltpu.VMEM((ATT_BLOCK, ckv), F32),
                        pltpu.VMEM((2, 2 * ATT_BLOCK, 2 * ATT_BLOCK), F32)],
        name=name,
        args=(sinks, qkv, qkv, qkv, qkv, qkv, o, lse, do))


DIL_PATTERNS = tuple((d, 2048 // d // ATT_BLOCK) for d in reversed(DILATIONS))
MHA = (0, 1)


def _dil_rows(idx, d, nb):
    j = idx // nb
    n = idx % nb
    base = j + n * (ATT_BLOCK * d)
    prev = jnp.maximum(base - ATT_BLOCK * d, j)
    if d == 1:
        return n, pl.ds(pl.multiple_of(base, ATT_BLOCK), ATT_BLOCK), pl.ds(pl.multiple_of(prev, ATT_BLOCK), ATT_BLOCK)
    return n, pl.ds(base, ATT_BLOCK, stride=d), pl.ds(prev, ATT_BLOCK, stride=d)


def dil_fwd(qkv, *, name, carried=None):
    bsz, s_len, _ = qkv.shape
    assert s_len == DIL_PATTERNS[0][0] * DIL_PATTERNS[0][1] * ATT_BLOCK
    npair = N_HEADS // 2

    def body(q_ref, k_ref, v_ref, y_ref, lse_ref, yb_ref, m_acc, l_acc, bias2, bias1):
        _fill_bias(bias2, bias1)
        for ci, (d, nb) in enumerate(DIL_PATTERNS):
            single = nb == 1

            def blocks(it, carry):
                items, places = [], []
                for u in range(PAIRS_AT_ONCE):
                    n, rows, prows = _dil_rows(it * PAIRS_AT_ONCE + u, d, nb)
                    kc = k_ref[rows, :].astype(BF16)
                    vc = v_ref[rows, :].astype(BF16)
                    if single:
                        kk, vv, bias = kc, vc, bias1[...]
                    else:
                        kk = jnp.concatenate([k_ref[prows, :].astype(BF16), kc], axis=0)
                        vv = jnp.concatenate([v_ref[prows, :].astype(BF16), vc], axis=0)
                        bias = bias2[jnp.minimum(n, 1)]
                    items.append((q_ref[rows, :].astype(BF16), kk, vv, bias, MHA, None))
                    places.append(rows)
                for (pv, m, l), rows in zip(_pairs_fwd(items), places):
                    o2, m2, l2 = _fold_heads(pv, MHA), _fold_heads(m, MHA), _fold_heads(l, MHA)
                    if ci == 0:
                        y_ref[rows, :] = o2
                        m_acc[rows, :] = m2
                        l_acc[rows, :] = l2
                    else:
                        m_old = m_acc[rows, :]
                        m_new = jnp.maximum(m_old, m2)
                        w_old = jnp.exp(m_old - m_new)
                        w_new = jnp.exp(m2 - m_new)
                        y_ref[rows, :] = y_ref[rows, :] * w_old + o2 * w_new
                        l_acc[rows, :] = l_acc[rows, :] * w_old + l2 * w_new
                        m_acc[rows, :] = m_new
                return carry

            lax.fori_loop(0, d * nb // PAIRS_AT_ONCE, blocks, 0)
        y = y_ref[...] / l_acc[...]
        y_ref[...] = y
        yb_ref[...] = y.astype(BF16)
        lse_ref[...] = m_acc[...] + jnp.log(l_acc[...])

    slab = lambda off: pl.BlockSpec((None, s_len, LANES), functools.partial(lambda o, b, h: (b, 0, o + h), off))
    sd = lambda dt: jax.ShapeDtypeStruct((bsz, s_len, D_MODEL), dt)
    return call_with_exchange(
        body, carried,
        out_shape=(sd(F32), sd(F32), sd(BF16)),
        grid=(bsz, npair),
        in_specs=[slab(0), slab(npair), slab(2 * npair)],
        out_specs=(slab(0), slab(0), slab(0)),
        scratch_shapes=[pltpu.VMEM((s_len, LANES), F32), pltpu.VMEM((s_len, LANES), F32),
                        pltpu.VMEM((2, 2 * ATT_BLOCK, 2 * ATT_BLOCK), F32), pltpu.VMEM((2 * ATT_BLOCK, ATT_BLOCK), F32)],
        name=name,
        args=(qkv, qkv, qkv))


def dil_bwd(qkv, y, lse, dy, *, name, carried=None):
    bsz, s_len, _ = qkv.shape
    npair = N_HEADS // 2

    def body(q_ref, k_ref, v_ref, y_ref, lse_ref, dy_ref, dq_out, dk_out, dv_out, bias2, bias1, dq_ref, dk_ref, dv_ref):
        _fill_bias(bias2, bias1)
        assert DIL_PATTERNS[0][1] == 1
        for d, nb in DIL_PATTERNS:
            single = nb == 1

            def blocks(it, carry):
                items, places = [], []
                for u in range(PAIRS_AT_ONCE):
                    n, rows, prows = _dil_rows(it * PAIRS_AT_ONCE + u, d, nb)
                    kc = k_ref[rows, :].astype(BF16)
                    vc = v_ref[rows, :].astype(BF16)
                    if single:
                        kk, vv, bias = kc, vc, bias1[...]
                    else:
                        kk = jnp.concatenate([k_ref[prows, :].astype(BF16), kc], axis=0)
                        vv = jnp.concatenate([v_ref[prows, :].astype(BF16), vc], axis=0)
                        bias = bias2[jnp.minimum(n, 1)]
                    items.append((q_ref[rows, :].astype(BF16), kk, vv, dy_ref[rows, :], y_ref[rows, :], lse_ref[rows, :], bias, MHA))
                    places.append((rows, prows))
                for (dq, dk, dv, _), (rows, prows) in zip(_pairs_bwd(items), places):
                    if single:
                        dq_ref[rows, :] = dq
                        dk_ref[rows, :] = dk
                        dv_ref[rows, :] = dv
                    else:
                        dq_ref[rows, :] += dq
                        dk_ref[prows, :] += dk[:ATT_BLOCK]
                        dv_ref[prows, :] += dv[:ATT_BLOCK]
                        dk_ref[rows, :] += dk[ATT_BLOCK:]
                        dv_ref[rows, :] += dv[ATT_BLOCK:]
                return carry

            lax.fori_loop(0, d * nb // PAIRS_AT_ONCE, blocks, 0)
        dq_out[...] = dq_ref[...].astype(BF16)
        dk_out[...] = dk_ref[...].astype(BF16)
        dv_out[...] = dv_ref[...].astype(BF16)

    slab = lambda off: pl.BlockSpec((None, s_len, LANES), functools.partial(lambda o, b, h: (b, 0, o + h), off))
    sd = jax.ShapeDtypeStruct((bsz, s_len, D_MODEL), BF16)
    return call_with_exchange(
        body, carried,
        out_shape=(sd, sd, sd),
        grid=(bsz, npair),
        in_specs=[slab(0), slab(npair), slab(2 * npair), slab(0), slab(0), slab(0)],
        out_specs=(slab(0), slab(0), slab(0)),
        scratch_shapes=[pltpu.VMEM((2, 2 * ATT_BLOCK, 2 * ATT_BLOCK), F32), pltpu.VMEM((2 * ATT_BLOCK, ATT_BLOCK), F32)]
        + [pltpu.VMEM((s_len, LANES), F32)] * 3,
        name=name,
        args=(qkv, qkv, qkv, y, lse, dy))


def adamw(w, g, m, v, *, name):
    rows, cols = w.shape
    tr = _pick(rows, (256, 128, 64, 32, 16, 8))

    def body(w_ref, g_ref, m_ref, v_ref, d_ref, nm_ref, nv_ref):
        gv = g_ref[...]
        nm = ADAM_B1 * m_ref[...] + (1.0 - ADAM_B1) * gv
        nv = ADAM_B2 * v_ref[...] + (1.0 - ADAM_B2) * (gv * gv)
        m_hat = nm / (1.0 - ADAM_B1 ** ADAM_STEP)
        v_hat = nv / (1.0 - ADAM_B2 ** ADAM_STEP)
        d_ref[...] = -ADAM_LR * (m_hat / (jnp.sqrt(v_hat) + ADAM_EPS) + ADAM_WD * w_ref[...])
        nm_ref[...] = nm
        nv_ref[...] = nv

    row = pl.BlockSpec((tr, cols), lambda i: (i, 0))
    return pl.pallas_call(
        body,
        out_shape=(jax.ShapeDtypeStruct((rows, cols), F32),) * 3,
        grid=(rows // tr,),
        in_specs=[row] * 4,
        out_specs=(row, row, row),
        compiler_params=_params(("parallel",)),
        name=name,
    )(w, g, m, v)


def _place():
    return lax.axis_index("x"), lax.axis_index("y"), lax.axis_index("c")


def _gather_copies(x_ref, out_ref, send_sems, recv_sems):
    x, y, c = _place()
    me, sibling = (x, y, c), (x, y, 1 - c)
    chips = [(1 - x, y), (x, 1 - y), (1 - x, 1 - y)]

    def slot(px, py, pc):
        return out_ref.at[4 * px + 2 * py + pc]

    def copy(k, block, to, src=None):
        return pltpu.make_async_remote_copy(
            src_ref=slot(*block) if src is None else src, dst_ref=slot(*block),
            send_sem=send_sems.at[k], recv_sem=recv_sems.at[k], device_id=to, device_id_type=MESH)

    first = [lambda: copy(0, me, sibling, src=x_ref)] + [functools.partial(copy, 1 + j, me, (*chip, c), src=x_ref)
                                                         for j, chip in enumerate(chips)]
    passed = [functools.partial(copy, 4 + j, (*chip, c), sibling) for j, chip in enumerate(chips)]
    landing = [functools.partial(copy, 1 + j, (*chip, c), me) for j, chip in enumerate(chips)]
    from_sibling = [lambda: copy(0, sibling, me)] + [functools.partial(copy, 4 + j, (*chip, 1 - c), me) for j, chip in enumerate(chips)]
    return slot(*me), first, passed, landing, from_sibling


def _gather_start(x_ref, out_ref, send_sems, recv_sems, local_sem):
    mine, first, _, _, _ = _gather_copies(x_ref, out_ref, send_sems, recv_sems)
    pltpu.make_async_copy(x_ref, mine, local_sem).start()
    for cp in first:
        cp().start()


def _gather_finish(x_ref, out_ref, send_sems, recv_sems, local_sem):
    mine, first, passed, landing, from_sibling = _gather_copies(x_ref, out_ref, send_sems, recv_sems)
    for cp, fwd in zip(landing, passed):
        cp().wait_recv()
        fwd().start()
    for cp in from_sibling:
        cp().wait_recv()
    for cp in first + passed:
        cp().wait_send()
    pltpu.make_async_copy(x_ref, mine, local_sem).wait()


def _a2a_copies(x_ref, out_ref, send_sems, recv_sems):
    x, y, c = _place()
    me = 4 * x + 2 * y + c
    copies = []
    for k in range(1, N_DEV):
        px = 1 - x if k & 4 else x
        py = 1 - y if k & 2 else y
        pc = 1 - c if k & 1 else c
        copies.append(pltpu.make_async_remote_copy(
            src_ref=x_ref.at[4 * px + 2 * py + pc], dst_ref=out_ref.at[me], send_sem=send_sems.at[k - 1],
            recv_sem=recv_sems.at[k - 1], device_id=(px, py, pc), device_id_type=MESH))
    return me, copies


def _a2a_start(x_ref, out_ref, send_sems, recv_sems, local_sem):
    me, copies = _a2a_copies(x_ref, out_ref, send_sems, recv_sems)
    pltpu.make_async_copy(x_ref.at[me], out_ref.at[me], local_sem).start()
    for cp in copies:
        cp.start()


def _a2a_finish(x_ref, out_ref, send_sems, recv_sems, local_sem):
    me, copies = _a2a_copies(x_ref, out_ref, send_sems, recv_sems)
    for cp in copies:
        cp.wait_recv()
    for cp in copies:
        cp.wait_send()
    pltpu.make_async_copy(x_ref.at[me], out_ref.at[me], local_sem).wait()


EXCHANGES = {"gather": (_gather_start, _gather_finish, lambda x: (N_DEV,) + x.shape),
             "a2a": (_a2a_start, _a2a_finish, lambda x: x.shape)}
EXCHANGE_SEMS = [pltpu.SemaphoreType.DMA((7,)), pltpu.SemaphoreType.DMA((7,)), pltpu.SemaphoreType.DMA(())]


def exchange(kind, x, *, name):
    start, finish, shape = EXCHANGES[kind]

    def body(x_ref, out_ref, *sems):
        start(x_ref, out_ref, *sems)
        finish(x_ref, out_ref, *sems)

    return pl.pallas_call(
        body,
        out_shape=jax.ShapeDtypeStruct(shape(x), x.dtype),
        in_specs=[pl.BlockSpec(memory_space=pl.ANY)],
        out_specs=pl.BlockSpec(memory_space=pl.ANY),
        scratch_shapes=EXCHANGE_SEMS,
        name=name,
    )(x)


def call_with_exchange(body, carried, *, out_shape, grid, in_specs, out_specs, scratch_shapes, name, args):
    sem = ("arbitrary",) * len(grid)
    carried = list(carried or ())
    if not carried:
        res = pl.pallas_call(body, out_shape=out_shape, grid=grid, in_specs=in_specs, out_specs=out_specs,
                             scratch_shapes=scratch_shapes, compiler_params=_params(sem), name=name)(*args)
        return res, []
    n_in, n_out, n_scr, n_x = len(in_specs), len(out_shape), len(scratch_shapes), len(carried)
    n_sems = len(EXCHANGE_SEMS)

    def wrapped(*refs):
        ins, x_refs = refs[:n_in], refs[n_in:n_in + n_x]
        outs = refs[n_in + n_x:n_in + n_x + n_out]
        out_refs = refs[n_in + n_x + n_out:n_in + 2 * n_x + n_out]
        rest = refs[n_in + 2 * n_x + n_out:]
        scratch, sems = rest[:n_scr], rest[n_scr:]
        ids = [pl.program_id(i) for i in range(len(grid))]
        is_first = functools.reduce(lambda a, b: a & b, [i == 0 for i in ids])
        is_last = functools.reduce(lambda a, b: a & b, [i == g - 1 for i, g in zip(ids, grid)])

        @pl.when(is_first)
        def _():
            for e, (kind, _) in enumerate(carried):
                EXCHANGES[kind][0](x_refs[e], out_refs[e], *sems[e * n_sems:(e + 1) * n_sems])

        body(*ins, *outs, *scratch)

        @pl.when(is_last)
        def _():
            for e, (kind, _) in enumerate(carried):
                EXCHANGES[kind][1](x_refs[e], out_refs[e], *sems[e * n_sems:(e + 1) * n_sems])

    any_spec = pl.BlockSpec(memory_space=pl.ANY)
    res = pl.pallas_call(
        wrapped,
        out_shape=tuple(out_shape) + tuple(jax.ShapeDtypeStruct(EXCHANGES[kind][2](x), x.dtype) for kind, x in carried),
        grid=grid,
        in_specs=list(in_specs) + [any_spec] * n_x,
        out_specs=tuple(out_specs) + (any_spec,) * n_x,
        scratch_shapes=list(scratch_shapes) + EXCHANGE_SEMS * n_x,
        compiler_params=_params(sem),
        name=name + "".join("_" + kind for kind, _ in carried),
    )(*args, *[x for _, x in carried])
    return res[:n_out], list(res[n_out:])


def sum_slots(x, *, name):
    _, rows, cols = x.shape
    tr = _pick(rows, (512, 256, 128, 64, 32, 16))

    def body(x_ref, o_ref):
        acc = x_ref[0].astype(F32)
        for k in range(1, N_DEV):
            acc = acc + x_ref[k].astype(F32)
        o_ref[...] = acc

    return pl.pallas_call(
        body,
        out_shape=jax.ShapeDtypeStruct((rows, cols), F32),
        grid=(rows // tr,),
        in_specs=[pl.BlockSpec((N_DEV, tr, cols), lambda i: (0, i, 0))],
        out_specs=pl.BlockSpec((tr, cols), lambda i: (i, 0)),
        compiler_params=_params(("parallel",)),
        name=name,
    )(x)


BIG = ("w_in", "w_branch", "w_out", "w_ffn_in", "w_ffn_out")
SMALL = ("conv_b", "w_rg", "b_rg", "w_ig", "b_ig", "lru_lambda", "sinks", "ln1_g", "ln1_b", "ln2_g", "ln2_b")
N_LRU_BLOCKS = D_MODEL // HEAD_DIM
SMALL_ROWS_TILE = 512


def _block_diag(w):
    z = jnp.zeros((N_LRU_BLOCKS // 2, HEAD_DIM, HEAD_DIM), w.dtype)
    top = jnp.concatenate([w[0::2], z], axis=2)
    bot = jnp.concatenate([z, w[1::2]], axis=2)
    return jnp.concatenate([top, bot], axis=1)


def _block_diag_grad(g):
    return jnp.stack([g[:, :HEAD_DIM, :HEAD_DIM], g[:, HEAD_DIM:, HEAD_DIM:]], axis=1).reshape(N_LRU_BLOCKS, HEAD_DIM, HEAD_DIM)


def layer_fwd(x, xb, p, bsz, own_late=None, next_w_in=None):
    t_dim = x.shape[0]
    s_len = t_dim // bsz
    w_f, w_qs, w_qd = p["w_in_f"], p["w_in_qs"], p["w_in_qd"]
    proj_f = matmul(xb, w_f, out_dtype=BF16, name="proj_f")
    qs = matmul(xb, w_qs, out_dtype=BF16, name="proj_qs").reshape(bsz, s_len, W_QS)
    qd = matmul(xb, w_qd, name="proj_qd").reshape(bsz, s_len, W_QD)
    proj_f3 = proj_f.reshape(bsz, s_len, W_F)
    wr_bd, wi_bd = _block_diag(p["w_rg"]), _block_diag(p["w_ig"])
    (y_a, h), got_rows = lru_fwd(proj_f3, p["conv_w"], p["conv_b"], wr_bd, wi_bd, p["b_rg"], p["b_ig"], p["lru_lambda"],
                                 name="lru_fwd", carried=[("gather", own_late[1])] if own_late is not None else [])
    (y_b, lse_b, y_bb), got_fi = swa_fwd(qs, p["sinks"], name="swa_fwd", carried=[("gather", own_late[0])] if own_late is not None else [])
    (y_c, lse_c, y_cb), got_next = dil_fwd(qd, name="dil_fwd", carried=[("gather", next_w_in)] if next_w_in is not None else [])
    if own_late is not None:
        p = {**p, **_late_weights(got_fi[0], got_rows[0])}
    ys = [t.reshape(t_dim, D_MODEL) for t in (y_a, y_bb, y_cb)]
    merged, br = branch_merge(ys, p["w_branch"], proj_f, name="branch_merge")
    x1, x1b, z1 = ln_fwd(x, merged, p["w_out"], p["ln1_g"], p["ln1_b"], name="w_out_ln")
    h1, h3, act = ffn_in_swiglu(x1b, p["w_ffn_in"], name="ffn_in_swiglu")
    x2, x2b, z2 = ln_fwd(x1, act, p["w_ffn_out"], p["ln2_g"], p["ln2_b"], name="ffn_out_ln")
    saved = dict(xb=xb, proj_f=proj_f, qs=qs, qd=qd, h=h, ys=ys, y_b=y_b, y_c=y_c, lse_b=lse_b, lse_c=lse_c, br=br, merged=merged,
                 z1=z1, x1b=x1b, h1=h1, h3=h3, act=act, z2=z2, wr_bd=wr_bd, wi_bd=wi_bd, p=p)
    return x2, x2b, saved, (got_next[0] if got_next else None)


def layer_bwd(dx2, s, bsz, exchange_own=False, above_w_in=None, pending=None, defer_last=False):
    p = s["p"]
    t_dim = dx2.shape[0]
    s_len = t_dim // bsz
    g = {}
    if pending is None:
        dz2, dz2b, g["ln2_g"], g["ln2_b"] = ln_bwd(dx2, s["z2"], p["ln2_g"], name="ln2_bwd")
    else:
        dz2, dz2b, g["ln2_g"], g["ln2_b"] = ln_bwd(dx2, s["z2"], p["ln2_g"], a=pending[0], w=pending[1], name="dx_qd_ln2_bwd")
    dh13 = swiglu_bwd(dz2b, p["w_ffn_out"], s["h1"], s["h3"], name="swiglu_bwd")
    g["w_ffn_out"] = matmul(s["act"], dz2b, trans_a=True, out_dtype=BF16, name="dw_ffn_out")
    g["w_ffn_in"] = matmul(s["x1b"], dh13, trans_a=True, out_dtype=BF16, name="dw_ffn_in")
    dz1, dz1b, g["ln1_g"], g["ln1_b"] = ln_bwd(dz2, s["z1"], p["ln1_g"], a=dh13, w=p["w_ffn_in"], dy_scale=ALPHA, name="dx_ffn_ln1_bwd")
    g["w_out"] = matmul(s["merged"], dz1b, trans_a=True, out_dtype=BF16, name="dw_out")
    *dbr, dgates = merge_bwd(dz1b, p["w_out"], s["proj_f"], s["br"], name="merge_bwd")
    dys = [matmul(dbr[n], p["w_branch"][n], trans_b=True, out_dtype=F32 if n == 2 else BF16, name="d_branch") for n in range(3)]
    g["w_branch"] = jnp.stack([matmul(s["ys"][n], dbr[n], trans_a=True, out_dtype=BF16, name="dw_branch") for n in range(3)])
    fi_slots, rows_slots = _late_slots(g) if exchange_own else (None, None)
    shape3 = (bsz, s_len, D_MODEL)
    (dlx, dlg, g["conv_w"], g["conv_b"], g["b_rg"], g["b_ig"], g["lru_lambda"], dwr, dwi), got_rows = lru_bwd(
        dys[0].reshape(shape3), s["proj_f"].reshape(bsz, s_len, W_F), s["h"], p["conv_w"], p["conv_b"], s["wr_bd"], s["wi_bd"],
        jnp.swapaxes(s["wr_bd"], 1, 2), jnp.swapaxes(s["wi_bd"], 1, 2), p["b_rg"], p["b_ig"], p["lru_lambda"], name="lru_bwd",
        carried=[("a2a", rows_slots)] if exchange_own else [])
    g["w_rg"], g["w_ig"] = _block_diag_grad(dwr), _block_diag_grad(dwi)
    dy_b3 = dys[1].reshape(shape3)
    (*dqs, dsinks), got_fi = swa_bwd(s["qs"], p["sinks"], s["y_b"], s["lse_b"], dy_b3, name="swa_bwd",
                                     carried=[("a2a", fi_slots)] if exchange_own else [])
    g["sinks"] = dsinks[0, :N_HEADS]
    dqd, got_in = dil_bwd(s["qd"], s["y_c"], s["lse_c"], dys[2].reshape(shape3), name="dil_bwd",
                          carried=[("a2a", above_w_in)] if above_w_in is not None else [])
    flat = lambda t: t.reshape(t_dim, t.shape[-1])
    dproj_f = jnp.concatenate([flat(dlx), flat(dlg), dgates], axis=1)
    dproj_qs = jnp.concatenate([flat(t) for t in dqs], axis=1)
    dproj_qd = jnp.concatenate([flat(t) for t in dqd], axis=1)
    g["w_in_f"] = matmul(s["xb"], dproj_f, trans_a=True, out_dtype=BF16, name="dw_in_f")
    g["w_in_qs"] = matmul(s["xb"], dproj_qs, trans_a=True, out_dtype=BF16, name="dw_in_qs")
    g["w_in_qd"] = matmul(s["xb"], dproj_qd, trans_a=True, out_dtype=BF16, name="dw_in_qd")
    dx = matmul(dproj_f, p["w_in_f"], trans_b=True, add=dz1, add_scale=ALPHA, name="dx_f")
    dx = matmul(dproj_qs, p["w_in_qs"], trans_b=True, add=dx, name="dx_qs")
    left = (dproj_qd, p["w_in_qd"]) if defer_last else None
    if not defer_last:
        dx = matmul(dproj_qd, p["w_in_qd"], trans_b=True, add=dx, name="dx_qd")
    g = {k: (v.reshape(p[k].shape) if k in p else v) for k, v in g.items()}
    return dx, g, dict(late=(got_fi[0], got_rows[0]) if exchange_own else None, w_in=got_in[0] if got_in else None), left


def local_step(x, target, layer_params, layer_shards=None, first_w_in=None):
    bsz, s_len, d = x.shape
    t_dim = bsz * s_len
    xf = x.reshape(t_dim, d)
    xb = xf.astype(BF16)
    exchanging = layer_shards is not None
    saved, gathered = [], first_w_in
    for l in range(DEPTH):
        p = layer_params(l, gathered)
        xf, xb, s, gathered = layer_fwd(xf, xb, p, bsz, own_late=layer_shards[l][1:] if exchanging else None,
                                        next_w_in=layer_shards[l + 1][0] if exchanging and l + 1 < DEPTH else None)
        saved.append(s)
    dy, sq = loss_head(xf, target.reshape(t_dim, d), name="loss_head")
    loss = 0.5 * jnp.sum(sq) / d
    grads, received, w_in_slots, pending = [None] * DEPTH, [[None] * 3 for _ in range(DEPTH)], None, None
    for l in reversed(range(DEPTH)):
        dy, grads[l], got, pending = layer_bwd(dy, saved[l], bsz, exchange_own=exchanging, above_w_in=w_in_slots,
                                               pending=pending, defer_last=l > 0)
        if got["w_in"] is not None:
            received[l + 1][0] = got["w_in"]
        if exchanging:
            received[l][1:] = got["late"]
            w_in_slots = _w_in_slots(grads[l])
    return loss, dy.reshape(bsz, s_len, d), grads, received, w_in_slots


W_IN_SEGMENTS = (("w_in_f", 0, 0, 2 * D_MODEL), ("w_in_qs", 0, 2 * D_MODEL, W_QS), ("w_in_qd", 0, 2 * D_MODEL + W_QS, W_QD),
                 ("w_in_f", 2 * D_MODEL, 2 * D_MODEL + W_QS + W_QD, 3 * D_MODEL))
ROW_SHARDED = ("w_branch", "w_out", "w_ffn_out")


def _cols_of_shards(shards, lo, hi):
    width = shards[0].shape[-1]
    parts = []
    for k, sh in enumerate(shards):
        a, b = max(lo, k * width), min(hi, (k + 1) * width)
        if a < b:
            parts.append(sh[..., a - k * width:b - k * width])
    return parts[0] if len(parts) == 1 else jnp.concatenate(parts, axis=-1)


def _cols_of_w_in(pieces, lo, hi):
    parts = []
    for name, p0, l0, width in W_IN_SEGMENTS:
        a, b = max(lo, l0), min(hi, l0 + width)
        if a < b:
            parts.append(pieces[name][..., p0 + a - l0:p0 + b - l0])
    return parts[0] if len(parts) == 1 else jnp.concatenate(parts, axis=-1)


W_IN_COLS = W_F + W_QS + W_QD


def _layer_shards(w):
    rows = jnp.concatenate([w[k].reshape(DEPTH, -1, D_MODEL) for k in ROW_SHARDED], axis=1).astype(BF16)
    w_in, w_fi = w["w_in"].astype(BF16), w["w_ffn_in"].astype(BF16)
    return [(w_in[l], w_fi[l], rows[l]) for l in range(DEPTH)]


ROW_COUNTS = (3 * D_MODEL // N_DEV, D_MODEL // N_DEV, FF_HIDDEN // N_DEV)


def _w_in_weights(g_in):
    sh = [g_in[k] for k in range(N_DEV)]
    return dict(w_in_f=jnp.concatenate([_cols_of_shards(sh, 0, 2 * D_MODEL), _cols_of_shards(sh, W_IN_COLS - 3 * D_MODEL, W_IN_COLS)], axis=-1),
                w_in_qs=_cols_of_shards(sh, 2 * D_MODEL, 2 * D_MODEL + W_QS),
                w_in_qd=_cols_of_shards(sh, 2 * D_MODEL + W_QS, 2 * D_MODEL + W_QS + W_QD))


def _late_weights(g_fi, g_rows):
    p = dict(w_ffn_in=jnp.concatenate([g_fi[k] for k in range(N_DEV)], axis=-1))
    off = 0
    for k, n in zip(ROW_SHARDED, ROW_COUNTS):
        t = g_rows[:, off:off + n]
        if k == "w_branch":
            p[k] = jnp.transpose(t.reshape(N_DEV, 3, n // 3, D_MODEL), (1, 0, 2, 3)).reshape(3, -1, D_MODEL)
        else:
            p[k] = t.reshape(-1, D_MODEL)
        off += n
    return p


def _w_in_slots(g):
    shard = W_IN_COLS // N_DEV
    return jnp.stack([_cols_of_w_in(g, k * shard, (k + 1) * shard) for k in range(N_DEV)]).astype(BF16)


def _late_slots(g):
    shard = g["w_ffn_in"].shape[-1] // N_DEV
    s_fi = jnp.stack([g["w_ffn_in"][:, k * shard:(k + 1) * shard] for k in range(N_DEV)]).astype(BF16)
    rows = jnp.concatenate([jnp.transpose(g["w_branch"].reshape(3, N_DEV, -1, D_MODEL), (1, 0, 2, 3)).reshape(N_DEV, -1, D_MODEL),
                            g["w_out"].reshape(N_DEV, -1, D_MODEL), g["w_ffn_out"].reshape(N_DEV, -1, D_MODEL)], axis=1).astype(BF16)
    return s_fi, rows


def _pad_rows(flat, tile_rows):
    n = flat.shape[0]
    per = tile_rows * LANES
    total = -(-n // per) * per
    return jnp.pad(flat, (0, total - n)).reshape(-1, LANES)


def kernel(x, w_in, conv_w, conv_b, w_rg, b_rg, w_ig, b_ig, lru_lambda, sinks, w_branch, w_out, ln1_g, ln1_b, w_ffn_in, w_ffn_out, ln2_g, ln2_b, loss_target, m_w_in, m_conv_w, m_conv_b, m_w_rg, m_b_rg, m_w_ig, m_b_ig, m_lru_lambda, m_sinks, m_w_branch, m_w_out, m_ln1_g, m_ln1_b, m_w_ffn_in, m_w_ffn_out, m_ln2_g, m_ln2_b, v_w_in, v_conv_w, v_conv_b, v_w_rg, v_b_rg, v_w_ig, v_b_ig, v_lru_lambda, v_sinks, v_w_branch, v_w_out, v_ln1_g, v_ln1_b, v_w_ffn_in, v_w_ffn_out, v_ln2_g, v_ln2_b):
    w = dict(w_in=w_in, conv_w=conv_w, conv_b=conv_b, w_rg=w_rg, b_rg=b_rg, w_ig=w_ig, b_ig=b_ig, lru_lambda=lru_lambda, sinks=sinks,
             w_branch=w_branch, w_out=w_out, ln1_g=ln1_g, ln1_b=ln1_b, w_ffn_in=w_ffn_in, w_ffn_out=w_ffn_out, ln2_g=ln2_g, ln2_b=ln2_b)
    m = dict(w_in=m_w_in, conv_w=m_conv_w, conv_b=m_conv_b, w_rg=m_w_rg, b_rg=m_b_rg, w_ig=m_w_ig, b_ig=m_b_ig, lru_lambda=m_lru_lambda,
             sinks=m_sinks, w_branch=m_w_branch, w_out=m_w_out, ln1_g=m_ln1_g, ln1_b=m_ln1_b, w_ffn_in=m_w_ffn_in, w_ffn_out=m_w_ffn_out,
             ln2_g=m_ln2_g, ln2_b=m_ln2_b)
    v = dict(w_in=v_w_in, conv_w=v_conv_w, conv_b=v_conv_b, w_rg=v_w_rg, b_rg=v_b_rg, w_ig=v_w_ig, b_ig=v_b_ig, lru_lambda=v_lru_lambda,
             sinks=v_sinks, w_branch=v_w_branch, w_out=v_w_out, ln1_g=v_ln1_g, ln1_b=v_ln1_b, w_ffn_in=v_w_ffn_in, w_ffn_out=v_w_ffn_out,
             ln2_g=v_ln2_g, ln2_b=v_ln2_b)
    order = ["w_in", "conv_w", "conv_b", "w_rg", "b_rg", "w_ig", "b_ig", "lru_lambda", "sinks", "w_branch", "w_out", "ln1_g", "ln1_b",
             "w_ffn_in", "w_ffn_out", "ln2_g", "ln2_b"]
    me = 4 * lax.axis_index("x") + 2 * lax.axis_index("y") + lax.axis_index("c")

    names = ("w_in", "w_ffn_in", "w_rows")
    shards = _layer_shards(w)
    first_w_in = exchange("gather", shards[0][0], name="gather_w_in")
    cw = exchange("gather", conv_w.reshape(-1, LANES), name="gather_conv_w")
    conv_w_full = jnp.moveaxis(cw.reshape(N_DEV, DEPTH, CONV_WIDTH, LANES), 0, 2).reshape(DEPTH, CONV_WIDTH, D_MODEL)

    def layer_params(l, gathered_w_in):
        return {**_w_in_weights(gathered_w_in), **{k: w[k][l] for k in SMALL}, "conv_w": conv_w_full[l]}

    loss_local, grad_x, grads, received, w_in_slots = local_step(x, loss_target, layer_params, shards, first_w_in)
    loss = lax.psum(loss_local, ("x", "y", "c"))
    received[0][0] = exchange("a2a", w_in_slots, name="exchange_g_w_in")

    sums = [[sum_slots(t, name=f"sum_g_{n}") for t, n in zip(received[l], names)] for l in range(DEPTH)]
    g_final = {"w_in": jnp.stack([sums[l][0] for l in range(DEPTH)]), "w_ffn_in": jnp.stack([sums[l][1] for l in range(DEPTH)])}
    off = 0
    for k, n in zip(ROW_SHARDED, ROW_COUNTS):
        g_final[k] = jnp.stack([sums[l][2][off:off + n] for l in range(DEPTH)]).reshape(w[k].shape)
        off += n
    grads = {k: jnp.stack([grads[l][k] for l in range(DEPTH)]) for k in list(SMALL) + ["conv_w"]}

    small_names = list(SMALL) + ["conv_w"]
    small_sizes = [grads[k].size for k in small_names]
    svec = _pad_rows(jnp.concatenate([grads[k].reshape(-1) for k in small_names]), SMALL_ROWS_TILE)
    ssum = sum_slots(exchange("gather", svec, name="gather_small_grads"), name="sum_small_grads")
    sflat, off = ssum.reshape(-1), 0
    for k, n in zip(small_names, small_sizes):
        g_final[k] = sflat[off:off + n].reshape(grads[k].shape)
        off += n
    g_final["conv_w"] = lax.dynamic_slice_in_dim(g_final["conv_w"], me * LANES, LANES, axis=2)

    delta, new_m, new_v = {}, {}, {}
    for k in list(BIG) + ["conv_w"]:
        cols = w[k].shape[-1]
        two_d = lambda t: t.reshape(-1, cols)
        d_, m_, v_ = adamw(two_d(w[k]), two_d(g_final[k]), two_d(m[k]), two_d(v[k]), name=f"adamw_{k}")
        delta[k], new_m[k], new_v[k] = d_.reshape(w[k].shape), m_.reshape(w[k].shape), v_.reshape(w[k].shape)
    pack_small = lambda dct: _pad_rows(jnp.concatenate([dct[k].reshape(-1) for k in SMALL]), SMALL_ROWS_TILE)
    d_, m_, v_ = adamw(pack_small(w), pack_small(g_final), pack_small(m), pack_small(v), name="adamw_small")
    off = 0
    for k in SMALL:
        n = w[k].size
        for dst, src in ((delta, d_), (new_m, m_), (new_v, v_)):
            dst[k] = src.reshape(-1)[off:off + n].reshape(w[k].shape)
        off += n
    return (loss, grad_x, *[g_final[k] for k in order], *[delta[k] for k in order], *[new_m[k] for k in order], *[new_v[k] for k in order])
```

```python
import functools
import math

import jax
import jax.numpy as jnp
from jax import lax
from jax.experimental import pallas as pl
from jax.experimental.pallas import tpu as pltpu

F32 = jnp.float32
BF16 = jnp.bfloat16

N_DEV = 8
DEPTH = 4
D_MODEL = 1024
HEAD_DIM = 64
LANES = 128
N_HEADS = D_MODEL // HEAD_DIM
SWA_KV_HEADS = 4
ATT_BLOCK = 128
DILATIONS = (1, 4, 16)
CONV_WIDTH = 4
LRU_C = 8.0
FF_HIDDEN = 2816
ALPHA = (2.0 * DEPTH) ** 0.25
LN_EPS = 1e-5
NEG_INF = -1e30
W_F = 5 * D_MODEL
W_QS = D_MODEL + 2 * SWA_KV_HEADS * HEAD_DIM
W_QD = 3 * D_MODEL

ADAM_LR = 0.001
ADAM_B1 = 0.9
ADAM_B2 = 0.999
ADAM_EPS = 1e-08
ADAM_WD = 0.01
ADAM_STEP = 10

VMEM_LIMIT = 56 * 1024 * 1024
MATMUL_BLOCK_BYTES = 40 * 1024 * 1024
MESH = pl.DeviceIdType.MESH


def _pick(n, cands):
    for c in cands:
        if n % c == 0:
            return c
    raise ValueError(f"no tile for {n} among {cands}")


def _params(sem):
    return pltpu.CompilerParams(dimension_semantics=sem, vmem_limit_bytes=VMEM_LIMIT)


def _tile(n, cap):
    best = None
    for t in range(LANES, cap + 1, LANES):
        if n % t == 0:
            best = t
    assert best is not None, (n, cap)
    return best


def matmul(a, b, *, name, trans_a=False, trans_b=False, out_dtype=F32, add=None, add_scale=1.0, carried=None):
    if trans_a:
        k_dim, m_dim = a.shape
    else:
        m_dim, k_dim = a.shape
    n_dim = b.shape[0] if trans_b else b.shape[1]
    assert (b.shape[1] if trans_b else b.shape[0]) == k_dim
    tn = _tile(n_dim, 1408)
    tm, tk = _tile(m_dim, 1024), _tile(k_dim, 1408)
    for cand in (1024, 512, 256):
        ctm = _tile(m_dim, cand)
        blocks = 2 * (ctm * k_dim * a.dtype.itemsize + tn * k_dim * b.dtype.itemsize + ctm * tn * jnp.dtype(out_dtype).itemsize
                      + (ctm * tn * add.dtype.itemsize if add is not None else 0))
        if blocks <= MATMUL_BLOCK_BYTES:
            tm, tk = ctm, k_dim
            break
    nk = k_dim // tk
    dims = (((0 if trans_a else 1,), (1 if trans_b else 0,)), ((), ()))

    def body(*refs):
        if add is None:
            a_ref, b_ref, o_ref, acc_ref = refs
            add_ref = None
        else:
            a_ref, b_ref, add_ref, o_ref, acc_ref = refs
        k = pl.program_id(2)
        part = lax.dot_general(a_ref[...].astype(BF16), b_ref[...].astype(BF16), dims, preferred_element_type=F32)

        def finish(r):
            if add_ref is not None:
                r = r + add_scale * add_ref[...].astype(F32)
            o_ref[...] = r.astype(out_dtype)

        if nk == 1:
            finish(part)
        else:
            @pl.when(k == 0)
            def _():
                acc_ref[...] = part

            @pl.when((k > 0) & (k < nk - 1))
            def _():
                acc_ref[...] += part

            @pl.when(k == nk - 1)
            def _():
                finish(acc_ref[...] + part)

    a_spec = pl.BlockSpec((tk, tm), lambda i, j, k: (k, i)) if trans_a else pl.BlockSpec((tm, tk), lambda i, j, k: (i, k))
    b_spec = pl.BlockSpec((tn, tk), lambda i, j, k: (j, k)) if trans_b else pl.BlockSpec((tk, tn), lambda i, j, k: (k, j))
    in_specs = [a_spec, b_spec]
    args = [a, b]
    if add is not None:
        in_specs.append(pl.BlockSpec((tm, tn), lambda i, j, k: (i, j)))
        args.append(add)
    if carried is not None:
        (out,), got = call_with_exchange(
            body, [carried], out_shape=(jax.ShapeDtypeStruct((m_dim, n_dim), out_dtype),), grid=(m_dim // tm, n_dim // tn, nk),
            in_specs=in_specs, out_specs=(pl.BlockSpec((tm, tn), lambda i, j, k: (i, j)),),
            scratch_shapes=[pltpu.VMEM((tm, tn) if nk > 1 else (8, LANES), F32)], name=name, args=args)
        return out, got[0]
    return pl.pallas_call(
        body,
        out_shape=jax.ShapeDtypeStruct((m_dim, n_dim), out_dtype),
        grid=(m_dim // tm, n_dim // tn, nk),
        in_specs=in_specs,
        out_specs=pl.BlockSpec((tm, tn), lambda i, j, k: (i, j)),
        scratch_shapes=[pltpu.VMEM((tm, tn) if nk > 1 else (8, LANES), F32)],
        compiler_params=_params(("parallel", "parallel", "arbitrary")),
        name=name,
    )(*args)


def ln_fwd(x, a, w, g, b, *, name):
    t_dim, d = x.shape
    k_dim = a.shape[1]
    tr = _tile(t_dim, 512)

    def body(x_ref, a_ref, w_ref, g_ref, b_ref, y_ref, yb_ref, z_ref):
        z = ALPHA * x_ref[...] + jnp.dot(a_ref[...], w_ref[...], preferred_element_type=F32)
        mu = jnp.mean(z, axis=-1, keepdims=True)
        zc = z - mu
        var = jnp.mean(zc * zc, axis=-1, keepdims=True)
        y = zc * lax.rsqrt(var + LN_EPS) * g_ref[...] + b_ref[...]
        y_ref[...] = y
        yb_ref[...] = y.astype(BF16)
        z_ref[...] = z

    row = pl.BlockSpec((tr, d), lambda i: (i, 0))
    vec = pl.BlockSpec((1, d), lambda i: (0, 0))
    return pl.pallas_call(
        body,
        out_shape=(jax.ShapeDtypeStruct((t_dim, d), F32), jax.ShapeDtypeStruct((t_dim, d), BF16), jax.ShapeDtypeStruct((t_dim, d), F32)),
        grid=(t_dim // tr,),
        in_specs=[row, pl.BlockSpec((tr, k_dim), lambda i: (i, 0)), pl.BlockSpec((k_dim, d), lambda i: (0, 0)), vec, vec],
        out_specs=(row, row, row),
        compiler_params=_params(("parallel",)),
        name=name,
    )(x, a, w, g.reshape(1, d), b.reshape(1, d))


def ln_bwd(dy, z, g, *, name, a=None, w=None, dy_scale=1.0):
    t_dim, d = dy.shape
    tr = _pick(t_dim, (256, 128, 8))

    def body(*refs):
        if a is None:
            dy_ref, z_ref, g_ref, dz_ref, dzb_ref, dg_ref, db_ref = refs
        else:
            dy_ref, a_ref, w_ref, z_ref, g_ref, dz_ref, dzb_ref, dg_ref, db_ref = refs

        @pl.when(pl.program_id(0) == 0)
        def _():
            dg_ref[...] = jnp.zeros_like(dg_ref)
            db_ref[...] = jnp.zeros_like(db_ref)

        z = z_ref[...]
        dyv = dy_scale * dy_ref[...]
        if a is not None:
            dyv = dyv + lax.dot_general(a_ref[...], w_ref[...], NT, preferred_element_type=F32)
        mu = jnp.mean(z, axis=-1, keepdims=True)
        zc = z - mu
        var = jnp.mean(zc * zc, axis=-1, keepdims=True)
        rstd = lax.rsqrt(var + LN_EPS)
        xhat = zc * rstd
        dxhat = dyv * g_ref[...]
        m1 = jnp.mean(dxhat, axis=-1, keepdims=True)
        m2 = jnp.mean(dxhat * xhat, axis=-1, keepdims=True)
        dz = rstd * (dxhat - m1 - xhat * m2)
        dz_ref[...] = dz
        dzb_ref[...] = dz.astype(BF16)
        dg_ref[...] += jnp.sum(dyv * xhat, axis=0, keepdims=True)
        db_ref[...] += jnp.sum(dyv, axis=0, keepdims=True)

    row = pl.BlockSpec((tr, d), lambda i: (i, 0))
    vec = pl.BlockSpec((1, d), lambda i: (0, 0))
    in_specs, args = [row], [dy]
    if a is not None:
        k_dim = a.shape[1]
        in_specs += [pl.BlockSpec((tr, k_dim), lambda i: (i, 0)), pl.BlockSpec((d, k_dim), lambda i: (0, 0))]
        args += [a, w]
    return pl.pallas_call(
        body,
        out_shape=(jax.ShapeDtypeStruct((t_dim, d), F32), jax.ShapeDtypeStruct((t_dim, d), BF16),
                   jax.ShapeDtypeStruct((1, d), F32), jax.ShapeDtypeStruct((1, d), F32)),
        grid=(t_dim // tr,),
        in_specs=in_specs + [row, vec],
        out_specs=(row, row, vec, vec),
        compiler_params=_params(("arbitrary",)),
        name=name,
    )(*args, z, g.reshape(1, d))


def loss_head(y, target, *, name):
    t_dim, d = y.shape
    tr = _pick(t_dim, (256, 128, 8))

    def body(y_ref, t_ref, dy_ref, sq_ref):
        @pl.when(pl.program_id(0) == 0)
        def _():
            sq_ref[...] = jnp.zeros_like(sq_ref)

        diff = y_ref[...] - t_ref[...]
        dy_ref[...] = diff / d
        sq_ref[...] += jnp.sum(diff * diff, axis=0, keepdims=True)

    row = pl.BlockSpec((tr, d), lambda i: (i, 0))
    vec = pl.BlockSpec((1, d), lambda i: (0, 0))
    return pl.pallas_call(
        body,
        out_shape=(jax.ShapeDtypeStruct((t_dim, d), F32), jax.ShapeDtypeStruct((1, d), F32)),
        grid=(t_dim // tr,),
        in_specs=[row, row],
        out_specs=(row, vec),
        compiler_params=_params(("arbitrary",)),
        name=name,
    )(y, target)


def _sigmoid(x):
    return 0.5 * jnp.tanh(0.5 * x) + 0.5


def ffn_in_swiglu(x, w, *, name):
    t_dim, d = x.shape
    f = w.shape[1] // 2
    tm, tn = _tile(t_dim, 1024), _tile(f, 1408)
    nf = f // tn

    def body(x_ref, w1_ref, w3_ref, h1_ref, h3_ref, act_ref):
        xv = x_ref[...]
        h1 = jnp.dot(xv, w1_ref[...], preferred_element_type=F32)
        h3 = jnp.dot(xv, w3_ref[...], preferred_element_type=F32)
        h1_ref[...] = h1.astype(BF16)
        h3_ref[...] = h3.astype(BF16)
        act_ref[...] = (h1 * _sigmoid(h1) * h3).astype(BF16)

    out = pl.BlockSpec((tm, tn), lambda i, j: (i, j))
    return pl.pallas_call(
        body,
        out_shape=(jax.ShapeDtypeStruct((t_dim, f), BF16),) * 3,
        grid=(t_dim // tm, nf),
        in_specs=[pl.BlockSpec((tm, d), lambda i, j: (i, 0)), pl.BlockSpec((d, tn), lambda i, j: (0, j)),
                  pl.BlockSpec((d, tn), lambda i, j: (0, j + nf))],
        out_specs=(out, out, out),
        compiler_params=_params(("parallel", "parallel")),
        name=name,
    )(x, w, w)


def swiglu_bwd(dz, w_ffn_out, h1, h3, *, name):
    t_dim, d = dz.shape
    f = h1.shape[1]
    tr = _tile(t_dim, 512)

    def body(dz_ref, w_ref, h1_ref, h3_ref, dh_ref):
        da = lax.dot_general(dz_ref[...], w_ref[...], NT, preferred_element_type=F32)
        h1 = h1_ref[...].astype(F32)
        sg = _sigmoid(h1)
        dh_ref[:, :f] = (da * h3_ref[...].astype(F32) * sg * (1.0 + h1 * (1.0 - sg))).astype(BF16)
        dh_ref[:, f:] = (da * h1 * sg).astype(BF16)

    wide = pl.BlockSpec((tr, f), lambda i: (i, 0))
    return pl.pallas_call(
        body,
        out_shape=jax.ShapeDtypeStruct((t_dim, 2 * f), BF16),
        grid=(t_dim // tr,),
        in_specs=[pl.BlockSpec((tr, d), lambda i: (i, 0)), pl.BlockSpec((f, d), lambda i: (0, 0)), wide, wide],
        out_specs=pl.BlockSpec((tr, 2 * f), lambda i: (i, 0)),
        compiler_params=_params(("parallel",)),
        name=name,
    )(dz, w_ffn_out, h1, h3)


def branch_merge(ys, w_branch, proj_f, *, name):
    t_dim, d = ys[0].shape
    tm = _tile(t_dim, 512)

    def body(y0, y1, y2, w_ref, g0, g1, g2, m_ref, b0, b1, b2):
        acc = None
        for n, (y, g, b) in enumerate(((y0, g0, b0), (y1, g1, b1), (y2, g2, b2))):
            br = jnp.dot(y[...], w_ref[n], preferred_element_type=F32)
            b[...] = br.astype(BF16)
            t = _sigmoid(g[...].astype(F32)) * br
            acc = t if acc is None else acc + t
        m_ref[...] = acc.astype(BF16)

    row = pl.BlockSpec((tm, d), lambda i: (i, 0))
    gate = [pl.BlockSpec((tm, d), functools.partial(lambda n, i: (i, 2 + n), n)) for n in range(3)]
    merged, *br = pl.pallas_call(
        body,
        out_shape=(jax.ShapeDtypeStruct((t_dim, d), BF16),) * 4,
        grid=(t_dim // tm,),
        in_specs=[row, row, row, pl.BlockSpec((3, d, d), lambda i: (0, 0, 0))] + gate,
        out_specs=(row, row, row, row),
        compiler_params=_params(("parallel",)),
        name=name,
    )(*ys, w_branch, proj_f, proj_f, proj_f)
    return merged, br


def merge_bwd(dz, w_out, proj_f, br, *, name):
    t_dim, d = dz.shape
    tr = _tile(t_dim, 512)

    def body(dz_ref, w_ref, g0, g1, g2, b0, b1, b2, d0, d1, d2, dg_ref):
        dm = lax.dot_general(dz_ref[...], w_ref[...], NT, preferred_element_type=F32)
        for n, (g, b, o) in enumerate(((g0, b0, d0), (g1, b1, d1), (g2, b2, d2))):
            sg = _sigmoid(g[...].astype(F32))
            o[...] = (dm * sg).astype(BF16)
            dg_ref[:, n * d:(n + 1) * d] = (dm * b[...].astype(F32) * sg * (1.0 - sg)).astype(BF16)

    row = pl.BlockSpec((tr, d), lambda i: (i, 0))
    gate = [pl.BlockSpec((tr, d), functools.partial(lambda n, i: (i, 2 + n), n)) for n in range(3)]
    return pl.pallas_call(
        body,
        out_shape=(jax.ShapeDtypeStruct((t_dim, d), BF16),) * 3 + (jax.ShapeDtypeStruct((t_dim, 3 * d), BF16),),
        grid=(t_dim // tr,),
        in_specs=[row, pl.BlockSpec((d, d), lambda i: (0, 0))] + gate + [row, row, row],
        out_specs=(row, row, row, pl.BlockSpec((tr, 3 * d), lambda i: (i, 0))),
        compiler_params=_params(("parallel",)),
        name=name,
    )(dz, w_out, proj_f, proj_f, proj_f, *br)


GELU_C = math.sqrt(2.0 / math.pi)
PAD = 8
SCAN_TILES = 8


def _gelu(x):
    return 0.5 * x * (1.0 + jnp.tanh(GELU_C * (x + 0.044715 * x * x * x)))


def _gelu_grad(x):
    t = jnp.tanh(GELU_C * (x + 0.044715 * x * x * x))
    return 0.5 * (1.0 + t) + 0.5 * x * (1.0 - t * t) * GELU_C * (1.0 + 3.0 * 0.044715 * x * x)


def _neg_expm1(x, exp_x):
    series = -x * (1.0 + x * (0.5 + x * (1.0 / 6.0)))
    return jnp.where(x > -0.02, series, 1.0 - exp_x)


def _lru_gates(xv, cw_ref, cb_ref, wr_ref, wi_ref, br_ref, bi_ref, lam_ref, pad_ref, s_len):
    pad_ref[pl.ds(0, PAD), :] = jnp.zeros((PAD, LANES), F32)
    pad_ref[pl.ds(PAD, s_len), :] = xv
    xc = cb_ref[...] + jnp.zeros((s_len, LANES), F32)
    for j in range(CONV_WIDTH):
        xc = xc + pad_ref[pl.ds(PAD - (CONV_WIDTH - 1) + j, s_len), :] * cw_ref[pl.ds(j, 1), :]
    xcb = xc.astype(BF16)
    r = _sigmoid(jnp.dot(xcb, wr_ref[0].astype(BF16), preferred_element_type=F32) + br_ref[...])
    i = _sigmoid(jnp.dot(xcb, wi_ref[0].astype(BF16), preferred_element_type=F32) + bi_ref[...])
    nl = -lam_ref[...]
    sp = jnp.maximum(nl, 0.0) + jnp.log(1.0 + jnp.exp(-jnp.abs(nl)))
    log_a = -LRU_C * r * sp
    a = jnp.exp(log_a)
    mult = jnp.sqrt(_neg_expm1(2.0 * log_a, a * a))
    return xc, r, i, sp, a, mult


def _tile_scan(a, b, row, reverse):
    for s in (1, 2, 4):
        if reverse:
            a_sh = pltpu.roll(a, 8 - s, 0)
            b_sh = pltpu.roll(b, 8 - s, 0)
            m = row + s <= 7
        else:
            a_sh = pltpu.roll(a, s, 0)
            b_sh = pltpu.roll(b, s, 0)
            m = row >= s
        b = jnp.where(m, a * b_sh + b, b)
        a = jnp.where(m, a * a_sh, a)
    return a, b


def lru_fwd(proj_f, conv_w, conv_b, wr_bd, wi_bd, b_rg, b_ig, lam, *, name, carried=None):
    bsz, s_len, _ = proj_f.shape
    d = D_MODEL
    ncb = d // LANES
    n_tiles = s_len // 8

    def body(x_ref, g_ref, cw_ref, cb_ref, wr_ref, wi_ref, br_ref, bi_ref, lam_ref, y_ref, h_ref, pad_ref, a_s, b_s):
        xc, r, i, sp, a, mult = _lru_gates(x_ref[0].astype(F32), cw_ref, cb_ref, wr_ref, wi_ref, br_ref, bi_ref, lam_ref, pad_ref, s_len)
        a_s[...] = a
        b_s[...] = mult * (i * xc)
        row = lax.broadcasted_iota(jnp.int32, (8, LANES), 0)

        def tiles(t, carry):
            starts = [pl.multiple_of((t * SCAN_TILES + u) * 8, 8) for u in range(SCAN_TILES)]
            local = [_tile_scan(a_s[pl.ds(i0, 8), :], b_s[pl.ds(i0, 8), :], row, False) for i0 in starts]
            for i0, (ac, hl) in zip(starts, local):
                h = hl + ac * carry
                h_ref[0, pl.ds(i0, 8), :] = h
                carry = jnp.broadcast_to(h[7:8, :], (8, LANES))
            return carry

        lax.fori_loop(0, n_tiles // SCAN_TILES, tiles, jnp.zeros((8, LANES), F32))
        y_ref[0] = (h_ref[0] * _gelu(g_ref[0].astype(F32))).astype(BF16)

    slab = lambda off: pl.BlockSpec((1, s_len, LANES), functools.partial(lambda o, c, b: (b, 0, o + c), off))
    vec = pl.BlockSpec((1, LANES), lambda c, b: (0, c))
    mat = pl.BlockSpec((1, LANES, LANES), lambda c, b: (c, 0, 0))
    out = pl.BlockSpec((1, s_len, LANES), lambda c, b: (b, 0, c))
    return call_with_exchange(
        body, carried,
        out_shape=(jax.ShapeDtypeStruct((bsz, s_len, d), BF16), jax.ShapeDtypeStruct((bsz, s_len, d), F32)),
        grid=(ncb, bsz),
        in_specs=[slab(0), slab(ncb), pl.BlockSpec((CONV_WIDTH, LANES), lambda c, b: (0, c)), vec, mat, mat, vec, vec, vec],
        out_specs=(out, out),
        scratch_shapes=[pltpu.VMEM((s_len + 2 * PAD, LANES), F32), pltpu.VMEM((s_len, LANES), F32), pltpu.VMEM((s_len, LANES), F32)],
        name=name,
        args=(proj_f, proj_f, conv_w, conv_b.reshape(1, d), wr_bd, wi_bd, b_rg.reshape(1, d), b_ig.reshape(1, d), lam.reshape(1, d)))


def lru_bwd(dy, proj_f, h, conv_w, conv_b, wr_bd, wi_bd, wr_bd_t, wi_bd_t, b_rg, b_ig, lam, *, name, carried=None):
    bsz, s_len, _ = proj_f.shape
    d = D_MODEL
    ncb = d // LANES
    n_tiles = s_len // 8

    def body(dy_ref, x_ref, g_ref, h_ref, cw_ref, cb_ref, wr_ref, wi_ref, wrt_ref, wit_ref, br_ref, bi_ref, lam_ref,
             dx_ref, dg_ref, dcw_ref, dcb_ref, dbr_ref, dbi_ref, dlam_ref, dwr_ref, dwi_ref, pad_ref, a_s, b_s, l_s):
        @pl.when(pl.program_id(1) == 0)
        def _():
            for ref in (dcw_ref, dcb_ref, dbr_ref, dbi_ref, dlam_ref, dwr_ref, dwi_ref):
                ref[...] = jnp.zeros_like(ref)

        xc, r, i, sp, a, mult = _lru_gates(x_ref[0].astype(F32), cw_ref, cb_ref, wr_ref, wi_ref, br_ref, bi_ref, lam_ref, pad_ref, s_len)
        gate = g_ref[0].astype(F32)
        hv = h_ref[0]
        dyv = dy_ref[0].astype(F32)
        dg_ref[0] = (dyv * hv * _gelu_grad(gate)).astype(BF16)
        b_s[...] = dyv * _gelu(gate)
        l_s[pl.ds(0, s_len), :] = a
        l_s[pl.ds(s_len, PAD), :] = jnp.zeros((PAD, LANES), F32)
        a_s[...] = l_s[pl.ds(1, s_len), :]
        row = lax.broadcasted_iota(jnp.int32, (8, LANES), 0)

        def tiles(t, carry):
            starts = [pl.multiple_of((n_tiles - 1 - (t * SCAN_TILES + u)) * 8, 8) for u in range(SCAN_TILES)]
            local = [_tile_scan(a_s[pl.ds(i0, 8), :], b_s[pl.ds(i0, 8), :], row, True) for i0 in starts]
            for i0, (ac, ll) in zip(starts, local):
                lmb = ll + ac * carry
                b_s[pl.ds(i0, 8), :] = lmb
                carry = jnp.broadcast_to(lmb[0:1, :], (8, LANES))
            return carry

        lax.fori_loop(0, n_tiles // SCAN_TILES, tiles, jnp.zeros((8, LANES), F32))
        lmb = b_s[...]
        l_s[pl.ds(0, PAD), :] = jnp.zeros((PAD, LANES), F32)
        l_s[pl.ds(PAD, s_len), :] = hv
        h_prev = l_s[pl.ds(PAD - 1, s_len), :]
        da = lmb * h_prev
        dmult = lmb * (i * xc)
        di = lmb * mult * xc
        dxc = lmb * mult * i
        dlog_a = da * a - dmult * a * a / mult
        dr = -LRU_C * sp * dlog_a
        dsp = jnp.sum(-LRU_C * r * dlog_a, axis=0, keepdims=True)
        dlam_ref[...] += dsp * (-_sigmoid(-lam_ref[...]))
        dpr = dr * r * (1.0 - r)
        dpi = di * i * (1.0 - i)
        dprb = dpr.astype(BF16)
        dpib = dpi.astype(BF16)
        xcb = xc.astype(BF16)
        dbr_ref[...] += jnp.sum(dpr, axis=0, keepdims=True)
        dbi_ref[...] += jnp.sum(dpi, axis=0, keepdims=True)
        tn = (((0,), (0,)), ((), ()))
        dwr_ref[0] += lax.dot_general(xcb, dprb, tn, preferred_element_type=F32)
        dwi_ref[0] += lax.dot_general(xcb, dpib, tn, preferred_element_type=F32)
        dxc = (dxc + jnp.dot(dprb, wrt_ref[0].astype(BF16), preferred_element_type=F32)
               + jnp.dot(dpib, wit_ref[0].astype(BF16), preferred_element_type=F32))
        dcb_ref[...] += jnp.sum(dxc, axis=0, keepdims=True)
        for j in range(CONV_WIDTH):
            dcw_ref[pl.ds(j, 1), :] += jnp.sum(dxc * pad_ref[pl.ds(PAD - (CONV_WIDTH - 1) + j, s_len), :], axis=0, keepdims=True)
        l_s[pl.ds(0, s_len), :] = dxc
        l_s[pl.ds(s_len, PAD), :] = jnp.zeros((PAD, LANES), F32)
        dx = jnp.zeros((s_len, LANES), F32)
        for j in range(CONV_WIDTH):
            dx = dx + l_s[pl.ds(CONV_WIDTH - 1 - j, s_len), :] * cw_ref[pl.ds(j, 1), :]
        dx_ref[0] = dx.astype(BF16)

    slab = lambda off: pl.BlockSpec((1, s_len, LANES), functools.partial(lambda o, c, b: (b, 0, o + c), off))
    vec = pl.BlockSpec((1, LANES), lambda c, b: (0, c))
    mat = pl.BlockSpec((1, LANES, LANES), lambda c, b: (c, 0, 0))
    cw = pl.BlockSpec((CONV_WIDTH, LANES), lambda c, b: (0, c))
    out = pl.BlockSpec((1, s_len, LANES), lambda c, b: (b, 0, c))
    vshape = jax.ShapeDtypeStruct((1, d), F32)
    mshape = jax.ShapeDtypeStruct((ncb, LANES, LANES), F32)
    return call_with_exchange(
        body, carried,
        out_shape=(jax.ShapeDtypeStruct((bsz, s_len, d), BF16),) * 2
        + (jax.ShapeDtypeStruct((CONV_WIDTH, d), F32), vshape, vshape, vshape, vshape, mshape, mshape),
        grid=(ncb, bsz),
        in_specs=[out, slab(0), slab(ncb), out, cw, vec, mat, mat, mat, mat, vec, vec, vec],
        out_specs=(out, out, cw, vec, vec, vec, vec, mat, mat),
        scratch_shapes=[pltpu.VMEM((s_len + 2 * PAD, LANES), F32), pltpu.VMEM((s_len, LANES), F32), pltpu.VMEM((s_len, LANES), F32),
                        pltpu.VMEM((s_len + 2 * PAD, LANES), F32)],
        name=name,
        args=(dy, proj_f, proj_f, h, conv_w, conv_b.reshape(1, d), wr_bd, wi_bd, wr_bd_t, wi_bd_t,
              b_rg.reshape(1, d), b_ig.reshape(1, d), lam.reshape(1, d)))


NT = (((1,), (1,)), ((), ()))
TN = (((0,), (0,)), ((), ()))
ATT_SCALE = HEAD_DIM ** -0.5


def _kv_place(head, n_kv_heads):
    kv = head // (N_HEADS // n_kv_heads)
    return kv // 2, kv % 2


def _band_mask(n, single):
    nk = ATT_BLOCK if single else 2 * ATT_BLOCK
    qi = lax.broadcasted_iota(jnp.int32, (2 * ATT_BLOCK, nk), 0) % ATT_BLOCK
    kj = lax.broadcasted_iota(jnp.int32, (2 * ATT_BLOCK, nk), 1)
    if single:
        return qi >= kj
    rel = qi + ATT_BLOCK - kj
    return (rel >= 0) & (rel <= ATT_BLOCK) & ((n > 0) | (kj >= ATT_BLOCK))


def _lane_halves():
    lane = lax.broadcasted_iota(jnp.int32, (1, LANES), 1)
    return lane < HEAD_DIM


def _stack_heads(t2, kh):
    first = _lane_halves()
    parts = []
    for a in range(2):
        ta = jnp.where(first if a == 0 else ~first, t2, jnp.zeros_like(t2))
        if a != kh[a]:
            ta = pltpu.roll(ta, HEAD_DIM, 1)
        parts.append(ta)
    return jnp.concatenate(parts, axis=0)


def _fold_heads(t, kh):
    t0, t1 = t[:ATT_BLOCK], t[ATT_BLOCK:]
    if t.shape[1] == LANES:
        if kh[0] != 0:
            t0 = pltpu.roll(t0, HEAD_DIM, 1)
        if kh[1] != 1:
            t1 = pltpu.roll(t1, HEAD_DIM, 1)
    return jnp.where(_lane_halves(), t0, t1)


def _rows_of_heads(t2):
    return jnp.concatenate([t2[:, 0:1], t2[:, HEAD_DIM:HEAD_DIM + 1]], axis=0)


PAIRS_AT_ONCE = 4


def _fill_bias(bias2_ref, bias1_ref=None):
    for i in range(2):
        bias2_ref[i] = jnp.where(_band_mask(i, False), 0.0, NEG_INF)
    if bias1_ref is not None:
        bias1_ref[...] = jnp.where(_band_mask(0, True), 0.0, NEG_INF)


def _pairs_fwd(items):
    ss = [lax.dot_general(_stack_heads(q2 * ATT_SCALE, kh), kk, NT, preferred_element_type=F32) + bias
          for q2, kk, _, bias, kh, _ in items]
    ps, ms, ls = [], [], []
    for s, (_, _, _, _, _, sink_col) in zip(ss, items):
        m = jnp.max(s, axis=-1, keepdims=True)
        if sink_col is not None:
            m = jnp.maximum(m, sink_col)
        p = jnp.exp(s - m)
        l = jnp.sum(p, axis=-1, keepdims=True)
        if sink_col is not None:
            l = l + jnp.exp(sink_col - m)
        ps.append(p.astype(BF16))
        ms.append(m)
        ls.append(l)
    pvs = [jnp.dot(p, it[2], preferred_element_type=F32) for p, it in zip(ps, items)]
    return list(zip(pvs, ms, ls))


def _pairs_bwd(items):
    first = _lane_halves()
    pre = []
    for q2, kk, vv, do2, o2, lse2, bias, kh in items:
        dd = do2 * o2
        dsum = jnp.concatenate([jnp.sum(jnp.where(first, dd, 0.0), axis=-1, keepdims=True),
                                jnp.sum(jnp.where(first, 0.0, dd), axis=-1, keepdims=True)], axis=0)
        qs = _stack_heads(q2 * ATT_SCALE, kh)
        dos = _stack_heads(do2.astype(BF16), kh)
        s = lax.dot_general(qs, kk, NT, preferred_element_type=F32) + bias
        dp = lax.dot_general(dos, vv, NT, preferred_element_type=F32)
        pre.append((qs, dos, s, dp, dsum))
    mid = []
    for (qs, dos, s, dp, dsum), it in zip(pre, items):
        p = jnp.exp(s - _rows_of_heads(it[5]))
        mid.append((p.astype(BF16), (p * (dp - dsum)).astype(BF16)))
    out = []
    for (pb, ds), (qs, dos, _, _, dsum), it in zip(mid, pre, items):
        dq = _fold_heads(jnp.dot(ds, it[1], preferred_element_type=F32), it[7]) * ATT_SCALE
        dk = lax.dot_general(ds, qs, TN, preferred_element_type=F32)
        dv = lax.dot_general(pb, dos, TN, preferred_element_type=F32)
        out.append((dq, dk, dv, dsum))
    return out


def swa_fwd(qkv, sinks, *, name, carried=None):
    bsz, s_len, width = qkv.shape
    ckv = SWA_KV_HEADS * HEAD_DIM
    nb = s_len // ATT_BLOCK
    kblk = D_MODEL // ckv

    def body(sink_ref, q_ref, kp_ref, kc_ref, vp_ref, vc_ref, o_ref, lse_ref, ob_ref, bias2):
        n = pl.program_id(1)
        _fill_bias(bias2)
        bias = bias2[jnp.minimum(n, 1)]
        kk = jnp.concatenate([kp_ref[0], kc_ref[0]], axis=0)
        vv = jnp.concatenate([vp_ref[0], vc_ref[0]], axis=0)
        top = lax.broadcasted_iota(jnp.int32, (2 * ATT_BLOCK, 1), 0) < ATT_BLOCK
        for hp0 in range(0, N_HEADS // 2, PAIRS_AT_ONCE):
            items, places = [], []
            for hp in range(hp0, hp0 + PAIRS_AT_ONCE):
                cols = slice(hp * LANES, (hp + 1) * LANES)
                kb, kh = _kv_place(2 * hp, SWA_KV_HEADS)
                kcols = slice(kb * LANES, (kb + 1) * LANES)
                sink_col = jnp.where(top, sink_ref[2 * hp], sink_ref[2 * hp + 1])
                items.append((q_ref[0, :, cols], kk[:, kcols], vv[:, kcols], bias, (kh, kh), sink_col))
                places.append((cols, (kh, kh)))
            for (pv, m, l), (cols, kh2) in zip(_pairs_fwd(items), places):
                o2 = _fold_heads(pv / l, kh2)
                o_ref[0, :, cols] = o2
                ob_ref[0, :, cols] = o2.astype(BF16)
                lse_ref[0, :, cols] = _fold_heads(m + jnp.log(l), kh2)

    prev = lambda n: jnp.maximum(n - 1, 0)
    out = pl.BlockSpec((1, ATT_BLOCK, D_MODEL), lambda b, n: (b, n, 0))
    sd = lambda dt: jax.ShapeDtypeStruct((bsz, s_len, D_MODEL), dt)
    return call_with_exchange(
        body, carried,
        out_shape=(sd(F32), sd(F32), sd(BF16)),
        grid=(bsz, nb),
        in_specs=[pl.BlockSpec(memory_space=pltpu.SMEM), out,
                  pl.BlockSpec((1, ATT_BLOCK, ckv), lambda b, n: (b, prev(n), kblk)),
                  pl.BlockSpec((1, ATT_BLOCK, ckv), lambda b, n: (b, n, kblk)),
                  pl.BlockSpec((1, ATT_BLOCK, ckv), lambda b, n: (b, prev(n), kblk + 1)),
                  pl.BlockSpec((1, ATT_BLOCK, ckv), lambda b, n: (b, n, kblk + 1))],
        out_specs=(out, out, out),
        scratch_shapes=[pltpu.VMEM((2, 2 * ATT_BLOCK, 2 * ATT_BLOCK), F32)],
        name=name,
        args=(sinks, qkv, qkv, qkv, qkv, qkv))


def swa_bwd(qkv, sinks, o, lse, do, *, name, carried=None):
    bsz, s_len, width = qkv.shape
    ckv = SWA_KV_HEADS * HEAD_DIM
    nb = s_len // ATT_BLOCK
    kblk = D_MODEL // ckv

    def body(sink_ref, q_ref, kp_ref, kc_ref, vp_ref, vc_ref, o_ref, lse_ref, do_ref, dq_ref, dk_ref, dv_ref, dsink_ref,
             dkk, dvv, ck, cv, bias2):
        n = pl.program_id(1)

        @pl.when((n == 0) & (pl.program_id(0) == 0))
        def _():
            dsink_ref[...] = jnp.zeros_like(dsink_ref)

        @pl.when(n < nb)
        def _():
            top = lax.broadcasted_iota(jnp.int32, (2 * ATT_BLOCK, 1), 0) < ATT_BLOCK
            lane = lax.broadcasted_iota(jnp.int32, dsink_ref.shape, 1)
            first_row = lax.broadcasted_iota(jnp.int32, dsink_ref.shape, 0) == 0
            _fill_bias(bias2)
            bias = bias2[jnp.minimum(n, 1)]
            kk = jnp.concatenate([kp_ref[0], kc_ref[0]], axis=0)
            vv = jnp.concatenate([vp_ref[0], vc_ref[0]], axis=0)
            dkk[...] = jnp.zeros_like(dkk)
            dvv[...] = jnp.zeros_like(dvv)
            for hp0 in range(0, N_HEADS // 2, PAIRS_AT_ONCE):
                items, places = [], []
                for hp in range(hp0, hp0 + PAIRS_AT_ONCE):
                    cols = slice(hp * LANES, (hp + 1) * LANES)
                    kb, kh = _kv_place(2 * hp, SWA_KV_HEADS)
                    kcols = slice(kb * LANES, (kb + 1) * LANES)
                    items.append((q_ref[0, :, cols], kk[:, kcols], vv[:, kcols], do_ref[0, :, cols], o_ref[0, :, cols],
                                  lse_ref[0, :, cols], bias, (kh, kh)))
                    places.append((cols, kcols, hp))
                for (dq, dk, dv, dsum), (cols, kcols, hp) in zip(_pairs_bwd(items), places):
                    dq_ref[0, :, cols] = dq.astype(BF16)
                    dkk[:, kcols] += dk
                    dvv[:, kcols] += dv
                    sink_col = jnp.where(top, sink_ref[2 * hp], sink_ref[2 * hp + 1])
                    t = -jnp.exp(sink_col - _rows_of_heads(lse_ref[0, :, cols])) * dsum
                    d0 = jnp.sum(t[:ATT_BLOCK], axis=0, keepdims=True)
                    d1 = jnp.sum(t[ATT_BLOCK:], axis=0, keepdims=True)
                    dsink_ref[...] += jnp.where(first_row & (lane == 2 * hp), d0, 0.0) + jnp.where(first_row & (lane == 2 * hp + 1), d1, 0.0)

        @pl.when((n >= 1) & (n < nb))
        def _():
            dk_ref[0] = (ck[...] + dkk[pl.ds(0, ATT_BLOCK), :]).astype(BF16)
            dv_ref[0] = (cv[...] + dvv[pl.ds(0, ATT_BLOCK), :]).astype(BF16)

        @pl.when(n == nb)
        def _():
            dk_ref[0] = ck[...].astype(BF16)
            dv_ref[0] = cv[...].astype(BF16)

        @pl.when(n < nb)
        def _():
            ck[...] = dkk[pl.ds(ATT_BLOCK, ATT_BLOCK), :]
            cv[...] = dvv[pl.ds(ATT_BLOCK, ATT_BLOCK), :]

    clamp = lambda n: jnp.minimum(n, nb - 1)
    prev = lambda n: jnp.maximum(n - 1, 0)
    row = pl.BlockSpec((1, ATT_BLOCK, D_MODEL), lambda b, n: (b, clamp(n), 0))
    kv_out = pl.BlockSpec((1, ATT_BLOCK, ckv), lambda b, n: (b, prev(n), 0))
    return call_with_exchange(
        body, carried,
        out_shape=(jax.ShapeDtypeStruct((bsz, s_len, D_MODEL), BF16), jax.ShapeDtypeStruct((bsz, s_len, ckv), BF16),
                   jax.ShapeDtypeStruct((bsz, s_len, ckv), BF16), jax.ShapeDtypeStruct((8, LANES), F32)),
        grid=(bsz, nb + 1),
        in_specs=[pl.BlockSpec(memory_space=pltpu.SMEM), row,
                  pl.BlockSpec((1, ATT_BLOCK, ckv), lambda b, n: (b, prev(clamp(n)), kblk)),
                  pl.BlockSpec((1, ATT_BLOCK, ckv), lambda b, n: (b, clamp(n), kblk)),
                  pl.BlockSpec((1, ATT_BLOCK, ckv), lambda b, n: (b, prev(clamp(n)), kblk + 1)),
                  pl.BlockSpec((1, ATT_BLOCK, ckv), lambda b, n: (b, clamp(n), kblk + 1)),
                  row, row, row],
        out_specs=(row, kv_out, kv_out, pl.BlockSpec((8, LANES), lambda b, n: (0, 0))),
        scratch_shapes=[pltpu.VMEM((2 * ATT_BLOCK, ckv), F32), pltpu.VMEM((2 * ATT_BLOCK, ckv), F32),
                        pltpu.VMEM((ATT_BLOCK, ckv), F32), pltpu.VMEM((ATT_BLOCK, ckv), F32),
                        pltpu.VMEM((2, 2 * ATT_BLOCK, 2 * ATT_BLOCK), F32)],
        name=name,
        args=(sinks, qkv, qkv, qkv, qkv, qkv, o, lse, do))


DIL_PATTERNS = tuple((d, 2048 // d // ATT_BLOCK) for d in reversed(DILATIONS))
MHA = (0, 1)


def _dil_rows(idx, d, nb):
    j = idx // nb
    n = idx % nb
    base = j + n * (ATT_BLOCK * d)
    prev = jnp.maximum(base - ATT_BLOCK * d, j)
    if d == 1:
        return n, pl.ds(pl.multiple_of(base, ATT_BLOCK), ATT_BLOCK), pl.ds(pl.multiple_of(prev, ATT_BLOCK), ATT_BLOCK)
    return n, pl.ds(base, ATT_BLOCK, stride=d), pl.ds(prev, ATT_BLOCK, stride=d)


def dil_fwd(qkv, *, name, carried=None):
    bsz, s_len, _ = qkv.shape
    assert s_len == DIL_PATTERNS[0][0] * DIL_PATTERNS[0][1] * ATT_BLOCK
    npair = N_HEADS // 2

    def body(q_ref, k_ref, v_ref, y_ref, lse_ref, yb_ref, m_acc, l_acc, bias2, bias1):
        _fill_bias(bias2, bias1)
        for ci, (d, nb) in enumerate(DIL_PATTERNS):
            single = nb == 1

            def blocks(it, carry):
                items, places = [], []
                for u in range(PAIRS_AT_ONCE):
                    n, rows, prows = _dil_rows(it * PAIRS_AT_ONCE + u, d, nb)
                    kc = k_ref[rows, :].astype(BF16)
                    vc = v_ref[rows, :].astype(BF16)
                    if single:
                        kk, vv, bias = kc, vc, bias1[...]
                    else:
                        kk = jnp.concatenate([k_ref[prows, :].astype(BF16), kc], axis=0)
                        vv = jnp.concatenate([v_ref[prows, :].astype(BF16), vc], axis=0)
                        bias = bias2[jnp.minimum(n, 1)]
                    items.append((q_ref[rows, :].astype(BF16), kk, vv, bias, MHA, None))
                    places.append(rows)
                for (pv, m, l), rows in zip(_pairs_fwd(items), places):
                    o2, m2, l2 = _fold_heads(pv, MHA), _fold_heads(m, MHA), _fold_heads(l, MHA)
                    if ci == 0:
                        y_ref[rows, :] = o2
                        m_acc[rows, :] = m2
                        l_acc[rows, :] = l2
                    else:
                        m_old = m_acc[rows, :]
                        m_new = jnp.maximum(m_old, m2)
                        w_old = jnp.exp(m_old - m_new)
                        w_new = jnp.exp(m2 - m_new)
                        y_ref[rows, :] = y_ref[rows, :] * w_old + o2 * w_new
                        l_acc[rows, :] = l_acc[rows, :] * w_old + l2 * w_new
                        m_acc[rows, :] = m_new
                return carry

            lax.fori_loop(0, d * nb // PAIRS_AT_ONCE, blocks, 0)
        y = y_ref[...] / l_acc[...]
        y_ref[...] = y
        yb_ref[...] = y.astype(BF16)
        lse_ref[...] = m_acc[...] + jnp.log(l_acc[...])

    slab = lambda off: pl.BlockSpec((None, s_len, LANES), functools.partial(lambda o, b, h: (b, 0, o + h), off))
    sd = lambda dt: jax.ShapeDtypeStruct((bsz, s_len, D_MODEL), dt)
    return call_with_exchange(
        body, carried,
        out_shape=(sd(F32), sd(F32), sd(BF16)),
        grid=(bsz, npair),
        in_specs=[slab(0), slab(npair), slab(2 * npair)],
        out_specs=(slab(0), slab(0), slab(0)),
        scratch_shapes=[pltpu.VMEM((s_len, LANES), F32), pltpu.VMEM((s_len, LANES), F32),
                        pltpu.VMEM((2, 2 * ATT_BLOCK, 2 * ATT_BLOCK), F32), pltpu.VMEM((2 * ATT_BLOCK, ATT_BLOCK), F32)],
        name=name,
        args=(qkv, qkv, qkv))


def dil_bwd(qkv, y, lse, dy, *, name, carried=None):
    bsz, s_len, _ = qkv.shape
    npair = N_HEADS // 2

    def body(q_ref, k_ref, v_ref, y_ref, lse_ref, dy_ref, dq_out, dk_out, dv_out, bias2, bias1, dq_ref, dk_ref, dv_ref):
        _fill_bias(bias2, bias1)
        assert DIL_PATTERNS[0][1] == 1
        for d, nb in DIL_PATTERNS:
            single = nb == 1

            def blocks(it, carry):
                items, places = [], []
                for u in range(PAIRS_AT_ONCE):
                    n, rows, prows = _dil_rows(it * PAIRS_AT_ONCE + u, d, nb)
                    kc = k_ref[rows, :].astype(BF16)
                    vc = v_ref[rows, :].astype(BF16)
                    if single:
                        kk, vv, bias = kc, vc, bias1[...]
                    else:
                        kk = jnp.concatenate([k_ref[prows, :].astype(BF16), kc], axis=0)
                        vv = jnp.concatenate([v_ref[prows, :].astype(BF16), vc], axis=0)
                        bias = bias2[jnp.minimum(n, 1)]
                    items.append((q_ref[rows, :].astype(BF16), kk, vv, dy_ref[rows, :], y_ref[rows, :], lse_ref[rows, :], bias, MHA))
                    places.append((rows, prows))
                for (dq, dk, dv, _), (rows, prows) in zip(_pairs_bwd(items), places):
                    if single:
                        dq_ref[rows, :] = dq
                        dk_ref[rows, :] = dk
                        dv_ref[rows, :] = dv
                    else:
                        dq_ref[rows, :] += dq
                        dk_ref[prows, :] += dk[:ATT_BLOCK]
                        dv_ref[prows, :] += dv[:ATT_BLOCK]
                        dk_ref[rows, :] += dk[ATT_BLOCK:]
                        dv_ref[rows, :] += dv[ATT_BLOCK:]
                return carry

            lax.fori_loop(0, d * nb // PAIRS_AT_ONCE, blocks, 0)
        dq_out[...] = dq_ref[...].astype(BF16)
        dk_out[...] = dk_ref[...].astype(BF16)
        dv_out[...] = dv_ref[...].astype(BF16)

    slab = lambda off: pl.BlockSpec((None, s_len, LANES), functools.partial(lambda o, b, h: (b, 0, o + h), off))
    sd = jax.ShapeDtypeStruct((bsz, s_len, D_MODEL), BF16)
    return call_with_exchange(
        body, carried,
        out_shape=(sd, sd, sd),
        grid=(bsz, npair),
        in_specs=[slab(0), slab(npair), slab(2 * npair), slab(0), slab(0), slab(0)],
        out_specs=(slab(0), slab(0), slab(0)),
        scratch_shapes=[pltpu.VMEM((2, 2 * ATT_BLOCK, 2 * ATT_BLOCK), F32), pltpu.VMEM((2 * ATT_BLOCK, ATT_BLOCK), F32)]
        + [pltpu.VMEM((s_len, LANES), F32)] * 3,
        name=name,
        args=(qkv, qkv, qkv, y, lse, dy))


def adamw(w, g, m, v, *, name):
    rows, cols = w.shape
    tr = _pick(rows, (256, 128, 64, 32, 16, 8))

    def body(w_ref, g_ref, m_ref, v_ref, d_ref, nm_ref, nv_ref):
        gv = g_ref[...]
        nm = ADAM_B1 * m_ref[...] + (1.0 - ADAM_B1) * gv
        nv = ADAM_B2 * v_ref[...] + (1.0 - ADAM_B2) * (gv * gv)
        m_hat = nm / (1.0 - ADAM_B1 ** ADAM_STEP)
        v_hat = nv / (1.0 - ADAM_B2 ** ADAM_STEP)
        d_ref[...] = -ADAM_LR * (m_hat / (jnp.sqrt(v_hat) + ADAM_EPS) + ADAM_WD * w_ref[...])
        nm_ref[...] = nm
        nv_ref[...] = nv

    row = pl.BlockSpec((tr, cols), lambda i: (i, 0))
    return pl.pallas_call(
        body,
        out_shape=(jax.ShapeDtypeStruct((rows, cols), F32),) * 3,
        grid=(rows // tr,),
        in_specs=[row] * 4,
        out_specs=(row, row, row),
        compiler_params=_params(("parallel",)),
        name=name,
    )(w, g, m, v)


def _place():
    return lax.axis_index("x"), lax.axis_index("y"), lax.axis_index("c")


def _gather_copies(x_ref, out_ref, send_sems, recv_sems):
    x, y, c = _place()
    me, sibling = (x, y, c), (x, y, 1 - c)
    chips = [(1 - x, y), (x, 1 - y), (1 - x, 1 - y)]

    def slot(px, py, pc):
        return out_ref.at[4 * px + 2 * py + pc]

    def copy(k, block, to, src=None):
        return pltpu.make_async_remote_copy(
            src_ref=slot(*block) if src is None else src, dst_ref=slot(*block),
            send_sem=send_sems.at[k], recv_sem=recv_sems.at[k], device_id=to, device_id_type=MESH)

    first = [lambda: copy(0, me, sibling, src=x_ref)] + [functools.partial(copy, 1 + j, me, (*chip, c), src=x_ref)
                                                         for j, chip in enumerate(chips)]
    passed = [functools.partial(copy, 4 + j, (*chip, c), sibling) for j, chip in enumerate(chips)]
    landing = [functools.partial(copy, 1 + j, (*chip, c), me) for j, chip in enumerate(chips)]
    from_sibling = [lambda: copy(0, sibling, me)] + [functools.partial(copy, 4 + j, (*chip, 1 - c), me) for j, chip in enumerate(chips)]
    return slot(*me), first, passed, landing, from_sibling


def _gather_start(x_ref, out_ref, send_sems, recv_sems, local_sem):
    mine, first, _, _, _ = _gather_copies(x_ref, out_ref, send_sems, recv_sems)
    pltpu.make_async_copy(x_ref, mine, local_sem).start()
    for cp in first:
        cp().start()


def _gather_finish(x_ref, out_ref, send_sems, recv_sems, local_sem):
    mine, first, passed, landing, from_sibling = _gather_copies(x_ref, out_ref, send_sems, recv_sems)
    for cp, fwd in zip(landing, passed):
        cp().wait_recv()
        fwd().start()
    for cp in from_sibling:
        cp().wait_recv()
    for cp in first + passed:
        cp().wait_send()
    pltpu.make_async_copy(x_ref, mine, local_sem).wait()


def _a2a_copies(x_ref, out_ref, send_sems, recv_sems):
    x, y, c = _place()
    me = 4 * x + 2 * y + c
    copies = []
    for k in range(1, N_DEV):
        px = 1 - x if k & 4 else x
        py = 1 - y if k & 2 else y
        pc = 1 - c if k & 1 else c
        copies.append(pltpu.make_async_remote_copy(
            src_ref=x_ref.at[4 * px + 2 * py + pc], dst_ref=out_ref.at[me], send_sem=send_sems.at[k - 1],
            recv_sem=recv_sems.at[k - 1], device_id=(px, py, pc), device_id_type=MESH))
    return me, copies


def _a2a_start(x_ref, out_ref, send_sems, recv_sems, local_sem):
    me, copies = _a2a_copies(x_ref, out_ref, send_sems, recv_sems)
    pltpu.make_async_copy(x_ref.at[me], out_ref.at[me], local_sem).start()
    for cp in copies:
        cp.start()


def _a2a_finish(x_ref, out_ref, send_sems, recv_sems, local_sem):
    me, copies = _a2a_copies(x_ref, out_ref, send_sems, recv_sems)
    for cp in copies:
        cp.wait_recv()
    for cp in copies:
        cp.wait_send()
    pltpu.make_async_copy(x_ref.at[me], out_ref.at[me], local_sem).wait()


EXCHANGES = {"gather": (_gather_start, _gather_finish, lambda x: (N_DEV,) + x.shape),
             "a2a": (_a2a_start, _a2a_finish, lambda x: x.shape)}
EXCHANGE_SEMS = [pltpu.SemaphoreType.DMA((7,)), pltpu.SemaphoreType.DMA((7,)), pltpu.SemaphoreType.DMA(())]


def exchange(kind, x, *, name):
    start, finish, shape = EXCHANGES[kind]

    def body(x_ref, out_ref, *sems):
        start(x_ref, out_ref, *sems)
        finish(x_ref, out_ref, *sems)

    return pl.pallas_call(
        body,
        out_shape=jax.ShapeDtypeStruct(shape(x), x.dtype),
        in_specs=[pl.BlockSpec(memory_space=pl.ANY)],
        out_specs=pl.BlockSpec(memory_space=pl.ANY),
        scratch_shapes=EXCHANGE_SEMS,
        name=name,
    )(x)


def call_with_exchange(body, carried, *, out_shape, grid, in_specs, out_specs, scratch_shapes, name, args):
    sem = ("arbitrary",) * len(grid)
    carried = list(carried or ())
    if not carried:
        res = pl.pallas_call(body, out_shape=out_shape, grid=grid, in_specs=in_specs, out_specs=out_specs,
                             scratch_shapes=scratch_shapes, compiler_params=_params(sem), name=name)(*args)
        return res, []
    n_in, n_out, n_scr, n_x = len(in_specs), len(out_shape), len(scratch_shapes), len(carried)
    n_sems = len(EXCHANGE_SEMS)

    def wrapped(*refs):
        ins, x_refs = refs[:n_in], refs[n_in:n_in + n_x]
        outs = refs[n_in + n_x:n_in + n_x + n_out]
        out_refs = refs[n_in + n_x + n_out:n_in + 2 * n_x + n_out]
        rest = refs[n_in + 2 * n_x + n_out:]
        scratch, sems = rest[:n_scr], rest[n_scr:]
        ids = [pl.program_id(i) for i in range(len(grid))]
        is_first = functools.reduce(lambda a, b: a & b, [i == 0 for i in ids])
        is_last = functools.reduce(lambda a, b: a & b, [i == g - 1 for i, g in zip(ids, grid)])

        @pl.when(is_first)
        def _():
            for e, (kind, _) in enumerate(carried):
                EXCHANGES[kind][0](x_refs[e], out_refs[e], *sems[e * n_sems:(e + 1) * n_sems])

        body(*ins, *outs, *scratch)

        @pl.when(is_last)
        def _():
            for e, (kind, _) in enumerate(carried):
                EXCHANGES[kind][1](x_refs[e], out_refs[e], *sems[e * n_sems:(e + 1) * n_sems])

    any_spec = pl.BlockSpec(memory_space=pl.ANY)
    res = pl.pallas_call(
        wrapped,
        out_shape=tuple(out_shape) + tuple(jax.ShapeDtypeStruct(EXCHANGES[kind][2](x), x.dtype) for kind, x in carried),
        grid=grid,
        in_specs=list(in_specs) + [any_spec] * n_x,
        out_specs=tuple(out_specs) + (any_spec,) * n_x,
        scratch_shapes=list(scratch_shapes) + EXCHANGE_SEMS * n_x,
        compiler_params=_params(sem),
        name=name + "".join("_" + kind for kind, _ in carried),
    )(*args, *[x for _, x in carried])
    return res[:n_out], list(res[n_out:])


def sum_slots(x, *, name):
    _, rows, cols = x.shape
    tr = _pick(rows, (512, 256, 128, 64, 32, 16))

    def body(x_ref, o_ref):
        acc = x_ref[0].astype(F32)
        for k in range(1, N_DEV):
            acc = acc + x_ref[k].astype(F32)
        o_ref[...] = acc

    return pl.pallas_call(
        body,
        out_shape=jax.ShapeDtypeStruct((rows, cols), F32),
        grid=(rows // tr,),
        in_specs=[pl.BlockSpec((N_DEV, tr, cols), lambda i: (0, i, 0))],
        out_specs=pl.BlockSpec((tr, cols), lambda i: (i, 0)),
        compiler_params=_params(("parallel",)),
        name=name,
    )(x)


BIG = ("w_in", "w_branch", "w_out", "w_ffn_in", "w_ffn_out")
SMALL = ("conv_b", "w_rg", "b_rg", "w_ig", "b_ig", "lru_lambda", "sinks", "ln1_g", "ln1_b", "ln2_g", "ln2_b")
N_LRU_BLOCKS = D_MODEL // HEAD_DIM
SMALL_ROWS_TILE = 512


def _block_diag(w):
    z = jnp.zeros((N_LRU_BLOCKS // 2, HEAD_DIM, HEAD_DIM), w.dtype)
    top = jnp.concatenate([w[0::2], z], axis=2)
    bot = jnp.concatenate([z, w[1::2]], axis=2)
    return jnp.concatenate([top, bot], axis=1)


def _block_diag_grad(g):
    return jnp.stack([g[:, :HEAD_DIM, :HEAD_DIM], g[:, HEAD_DIM:, HEAD_DIM:]], axis=1).reshape(N_LRU_BLOCKS, HEAD_DIM, HEAD_DIM)


def layer_fwd(x, xb, p, bsz, own_late=None, next_w_in=None):
    t_dim = x.shape[0]
    s_len = t_dim // bsz
    w_f, w_qs, w_qd = p["w_in_f"], p["w_in_qs"], p["w_in_qd"]
    proj_f = matmul(xb, w_f, out_dtype=BF16, name="proj_f")
    qs = matmul(xb, w_qs, out_dtype=BF16, name="proj_qs").reshape(bsz, s_len, W_QS)
    qd = matmul(xb, w_qd, name="proj_qd").reshape(bsz, s_len, W_QD)
    proj_f3 = proj_f.reshape(bsz, s_len, W_F)
    wr_bd, wi_bd = _block_diag(p["w_rg"]), _block_diag(p["w_ig"])
    (y_a, h), got_rows = lru_fwd(proj_f3, p["conv_w"], p["conv_b"], wr_bd, wi_bd, p["b_rg"], p["b_ig"], p["lru_lambda"],
                                 name="lru_fwd", carried=[("gather", own_late[1])] if own_late is not None else [])
    (y_b, lse_b, y_bb), got_fi = swa_fwd(qs, p["sinks"], name="swa_fwd", carried=[("gather", own_late[0])] if own_late is not None else [])
    (y_c, lse_c, y_cb), got_next = dil_fwd(qd, name="dil_fwd", carried=[("gather", next_w_in)] if next_w_in is not None else [])
    if own_late is not None:
        p = {**p, **_late_weights(got_fi[0], got_rows[0])}
    ys = [t.reshape(t_dim, D_MODEL) for t in (y_a, y_bb, y_cb)]
    merged, br = branch_merge(ys, p["w_branch"], proj_f, name="branch_merge")
    x1, x1b, z1 = ln_fwd(x, merged, p["w_out"], p["ln1_g"], p["ln1_b"], name="w_out_ln")
    h1, h3, act = ffn_in_swiglu(x1b, p["w_ffn_in"], name="ffn_in_swiglu")
    x2, x2b, z2 = ln_fwd(x1, act, p["w_ffn_out"], p["ln2_g"], p["ln2_b"], name="ffn_out_ln")
    saved = dict(xb=xb, proj_f=proj_f, qs=qs, qd=qd, h=h, ys=ys, y_b=y_b, y_c=y_c, lse_b=lse_b, lse_c=lse_c, br=br, merged=merged,
                 z1=z1, x1b=x1b, h1=h1, h3=h3, act=act, z2=z2, wr_bd=wr_bd, wi_bd=wi_bd, p=p)
    return x2, x2b, saved, (got_next[0] if got_next else None)


def layer_bwd(dx2, s, bsz, exchange_own=False, above_w_in=None, pending=None, defer_last=False):
    p = s["p"]
    t_dim = dx2.shape[0]
    s_len = t_dim // bsz
    g = {}
    if pending is None:
        dz2, dz2b, g["ln2_g"], g["ln2_b"] = ln_bwd(dx2, s["z2"], p["ln2_g"], name="ln2_bwd")
    else:
        dz2, dz2b, g["ln2_g"], g["ln2_b"] = ln_bwd(dx2, s["z2"], p["ln2_g"], a=pending[0], w=pending[1], name="dx_qd_ln2_bwd")
    dh13 = swiglu_bwd(dz2b, p["w_ffn_out"], s["h1"], s["h3"], name="swiglu_bwd")
    g["w_ffn_out"] = matmul(s["act"], dz2b, trans_a=True, out_dtype=BF16, name="dw_ffn_out")
    g["w_ffn_in"] = matmul(s["x1b"], dh13, trans_a=True, out_dtype=BF16, name="dw_ffn_in")
    dz1, dz1b, g["ln1_g"], g["ln1_b"] = ln_bwd(dz2, s["z1"], p["ln1_g"], a=dh13, w=p["w_ffn_in"], dy_scale=ALPHA, name="dx_ffn_ln1_bwd")
    g["w_out"] = matmul(s["merged"], dz1b, trans_a=True, out_dtype=BF16, name="dw_out")
    *dbr, dgates = merge_bwd(dz1b, p["w_out"], s["proj_f"], s["br"], name="merge_bwd")
    dys = [matmul(dbr[n], p["w_branch"][n], trans_b=True, out_dtype=F32 if n == 2 else BF16, name="d_branch") for n in range(3)]
    g["w_branch"] = jnp.stack([matmul(s["ys"][n], dbr[n], trans_a=True, out_dtype=BF16, name="dw_branch") for n in range(3)])
    fi_slots, rows_slots = _late_slots(g) if exchange_own else (None, None)
    shape3 = (bsz, s_len, D_MODEL)
    (dlx, dlg, g["conv_w"], g["conv_b"], g["b_rg"], g["b_ig"], g["lru_lambda"], dwr, dwi), got_rows = lru_bwd(
        dys[0].reshape(shape3), s["proj_f"].reshape(bsz, s_len, W_F), s["h"], p["conv_w"], p["conv_b"], s["wr_bd"], s["wi_bd"],
        jnp.swapaxes(s["wr_bd"], 1, 2), jnp.swapaxes(s["wi_bd"], 1, 2), p["b_rg"], p["b_ig"], p["lru_lambda"], name="lru_bwd",
        carried=[("a2a", rows_slots)] if exchange_own else [])
    g["w_rg"], g["w_ig"] = _block_diag_grad(dwr), _block_diag_grad(dwi)
    dy_b3 = dys[1].reshape(shape3)
    (*dqs, dsinks), got_fi = swa_bwd(s["qs"], p["sinks"], s["y_b"], s["lse_b"], dy_b3, name="swa_bwd",
                                     carried=[("a2a", fi_slots)] if exchange_own else [])
    g["sinks"] = dsinks[0, :N_HEADS]
    dqd, got_in = dil_bwd(s["qd"], s["y_c"], s["lse_c"], dys[2].reshape(shape3), name="dil_bwd",
                          carried=[("a2a", above_w_in)] if above_w_in is not None else [])
    flat = lambda t: t.reshape(t_dim, t.shape[-1])
    dproj_f = jnp.concatenate([flat(dlx), flat(dlg), dgates], axis=1)
    dproj_qs = jnp.concatenate([flat(t) for t in dqs], axis=1)
    dproj_qd = jnp.concatenate([flat(t) for t in dqd], axis=1)
    g["w_in_f"] = matmul(s["xb"], dproj_f, trans_a=True, out_dtype=BF16, name="dw_in_f")
    g["w_in_qs"] = matmul(s["xb"], dproj_qs, trans_a=True, out_dtype=BF16, name="dw_in_qs")
    g["w_in_qd"] = matmul(s["xb"], dproj_qd, trans_a=True, out_dtype=BF16, name="dw_in_qd")
    own_w_in = None
    if exchange_own and not defer_last:
        dx, own_w_in = matmul(dproj_f, p["w_in_f"], trans_b=True, add=dz1, add_scale=ALPHA, name="dx_f",
                              carried=("a2a", _w_in_slots(g)))
    else:
        dx = matmul(dproj_f, p["w_in_f"], trans_b=True, add=dz1, add_scale=ALPHA, name="dx_f")
    dx = matmul(dproj_qs, p["w_in_qs"], trans_b=True, add=dx, name="dx_qs")
    left = (dproj_qd, p["w_in_qd"]) if defer_last else None
    if not defer_last:
        dx = matmul(dproj_qd, p["w_in_qd"], trans_b=True, add=dx, name="dx_qd")
    g = {k: (v.reshape(p[k].shape) if k in p else v) for k, v in g.items()}
    return dx, g, dict(late=(got_fi[0], got_rows[0]) if exchange_own else None, w_in=got_in[0] if got_in else None,
                       own_w_in=own_w_in), left


def local_step(x, target, layer_params, layer_shards=None, first_w_in=None):
    bsz, s_len, d = x.shape
    t_dim = bsz * s_len
    xf = x.reshape(t_dim, d)
    xb = xf.astype(BF16)
    exchanging = layer_shards is not None
    saved, gathered = [], first_w_in
    for l in range(DEPTH):
        p = layer_params(l, gathered)
        xf, xb, s, gathered = layer_fwd(xf, xb, p, bsz, own_late=layer_shards[l][1:] if exchanging else None,
                                        next_w_in=layer_shards[l + 1][0] if exchanging and l + 1 < DEPTH else None)
        saved.append(s)
    dy, sq = loss_head(xf, target.reshape(t_dim, d), name="loss_head")
    loss = 0.5 * jnp.sum(sq) / d
    grads, received, w_in_slots, pending = [None] * DEPTH, [[None] * 3 for _ in range(DEPTH)], None, None
    for l in reversed(range(DEPTH)):
        dy, grads[l], got, pending = layer_bwd(dy, saved[l], bsz, exchange_own=exchanging, above_w_in=w_in_slots,
                                               pending=pending, defer_last=l > 0)
        if got["w_in"] is not None:
            received[l + 1][0] = got["w_in"]
        if exchanging:
            received[l][1:] = got["late"]
            received[l][0] = got["own_w_in"]
            w_in_slots = _w_in_slots(grads[l]) if l > 0 else None
    return loss, dy.reshape(bsz, s_len, d), grads, received


W_IN_SEGMENTS = (("w_in_f", 0, 0, 2 * D_MODEL), ("w_in_qs", 0, 2 * D_MODEL, W_QS), ("w_in_qd", 0, 2 * D_MODEL + W_QS, W_QD),
                 ("w_in_f", 2 * D_MODEL, 2 * D_MODEL + W_QS + W_QD, 3 * D_MODEL))
ROW_SHARDED = ("w_branch", "w_out", "w_ffn_out")


def _cols_of_shards(shards, lo, hi):
    width = shards[0].shape[-1]
    parts = []
    for k, sh in enumerate(shards):
        a, b = max(lo, k * width), min(hi, (k + 1) * width)
        if a < b:
            parts.append(sh[..., a - k * width:b - k * width])
    return parts[0] if len(parts) == 1 else jnp.concatenate(parts, axis=-1)


def _cols_of_w_in(pieces, lo, hi):
    parts = []
    for name, p0, l0, width in W_IN_SEGMENTS:
        a, b = max(lo, l0), min(hi, l0 + width)
        if a < b:
            parts.append(pieces[name][..., p0 + a - l0:p0 + b - l0])
    return parts[0] if len(parts) == 1 else jnp.concatenate(parts, axis=-1)


W_IN_COLS = W_F + W_QS + W_QD


def _layer_shards(w):
    rows = jnp.concatenate([w[k].reshape(DEPTH, -1, D_MODEL) for k in ROW_SHARDED], axis=1).astype(BF16)
    w_in, w_fi = w["w_in"].astype(BF16), w["w_ffn_in"].astype(BF16)
    return [(w_in[l], w_fi[l], rows[l]) for l in range(DEPTH)]


ROW_COUNTS = (3 * D_MODEL // N_DEV, D_MODEL // N_DEV, FF_HIDDEN // N_DEV)


def _w_in_weights(g_in):
    sh = [g_in[k] for k in range(N_DEV)]
    return dict(w_in_f=jnp.concatenate([_cols_of_shards(sh, 0, 2 * D_MODEL), _cols_of_shards(sh, W_IN_COLS - 3 * D_MODEL, W_IN_COLS)], axis=-1),
                w_in_qs=_cols_of_shards(sh, 2 * D_MODEL, 2 * D_MODEL + W_QS),
                w_in_qd=_cols_of_shards(sh, 2 * D_MODEL + W_QS, 2 * D_MODEL + W_QS + W_QD))


def _late_weights(g_fi, g_rows):
    p = dict(w_ffn_in=jnp.concatenate([g_fi[k] for k in range(N_DEV)], axis=-1))
    off = 0
    for k, n in zip(ROW_SHARDED, ROW_COUNTS):
        t = g_rows[:, off:off + n]
        if k == "w_branch":
            p[k] = jnp.transpose(t.reshape(N_DEV, 3, n // 3, D_MODEL), (1, 0, 2, 3)).reshape(3, -1, D_MODEL)
        else:
            p[k] = t.reshape(-1, D_MODEL)
        off += n
    return p


def _w_in_slots(g):
    shard = W_IN_COLS // N_DEV
    return jnp.stack([_cols_of_w_in(g, k * shard, (k + 1) * shard) for k in range(N_DEV)]).astype(BF16)


def _late_slots(g):
    shard = g["w_ffn_in"].shape[-1] // N_DEV
    s_fi = jnp.stack([g["w_ffn_in"][:, k * shard:(k + 1) * shard] for k in range(N_DEV)]).astype(BF16)
    rows = jnp.concatenate([jnp.transpose(g["w_branch"].reshape(3, N_DEV, -1, D_MODEL), (1, 0, 2, 3)).reshape(N_DEV, -1, D_MODEL),
                            g["w_out"].reshape(N_DEV, -1, D_MODEL), g["w_ffn_out"].reshape(N_DEV, -1, D_MODEL)], axis=1).astype(BF16)
    return s_fi, rows


def _pad_rows(flat, tile_rows):
    n = flat.shape[0]
    per = tile_rows * LANES
    total = -(-n // per) * per
    return jnp.pad(flat, (0, total - n)).reshape(-1, LANES)


def kernel(x, w_in, conv_w, conv_b, w_rg, b_rg, w_ig, b_ig, lru_lambda, sinks, w_branch, w_out, ln1_g, ln1_b, w_ffn_in, w_ffn_out, ln2_g, ln2_b, loss_target, m_w_in, m_conv_w, m_conv_b, m_w_rg, m_b_rg, m_w_ig, m_b_ig, m_lru_lambda, m_sinks, m_w_branch, m_w_out, m_ln1_g, m_ln1_b, m_w_ffn_in, m_w_ffn_out, m_ln2_g, m_ln2_b, v_w_in, v_conv_w, v_conv_b, v_w_rg, v_b_rg, v_w_ig, v_b_ig, v_lru_lambda, v_sinks, v_w_branch, v_w_out, v_ln1_g, v_ln1_b, v_w_ffn_in, v_w_ffn_out, v_ln2_g, v_ln2_b):
    w = dict(w_in=w_in, conv_w=conv_w, conv_b=conv_b, w_rg=w_rg, b_rg=b_rg, w_ig=w_ig, b_ig=b_ig, lru_lambda=lru_lambda, sinks=sinks,
             w_branch=w_branch, w_out=w_out, ln1_g=ln1_g, ln1_b=ln1_b, w_ffn_in=w_ffn_in, w_ffn_out=w_ffn_out, ln2_g=ln2_g, ln2_b=ln2_b)
    m = dict(w_in=m_w_in, conv_w=m_conv_w, conv_b=m_conv_b, w_rg=m_w_rg, b_rg=m_b_rg, w_ig=m_w_ig, b_ig=m_b_ig, lru_lambda=m_lru_lambda,
             sinks=m_sinks, w_branch=m_w_branch, w_out=m_w_out, ln1_g=m_ln1_g, ln1_b=m_ln1_b, w_ffn_in=m_w_ffn_in, w_ffn_out=m_w_ffn_out,
             ln2_g=m_ln2_g, ln2_b=m_ln2_b)
    v = dict(w_in=v_w_in, conv_w=v_conv_w, conv_b=v_conv_b, w_rg=v_w_rg, b_rg=v_b_rg, w_ig=v_w_ig, b_ig=v_b_ig, lru_lambda=v_lru_lambda,
             sinks=v_sinks, w_branch=v_w_branch, w_out=v_w_out, ln1_g=v_ln1_g, ln1_b=v_ln1_b, w_ffn_in=v_w_ffn_in, w_ffn_out=v_w_ffn_out,
             ln2_g=v_ln2_g, ln2_b=v_ln2_b)
    order = ["w_in", "conv_w", "conv_b", "w_rg", "b_rg", "w_ig", "b_ig", "lru_lambda", "sinks", "w_branch", "w_out", "ln1_g", "ln1_b",
             "w_ffn_in", "w_ffn_out", "ln2_g", "ln2_b"]
    me = 4 * lax.axis_index("x") + 2 * lax.axis_index("y") + lax.axis_index("c")

    names = ("w_in", "w_ffn_in", "w_rows")
    shards = _layer_shards(w)
    first_w_in = exchange("gather", shards[0][0], name="gather_w_in")
    cw = exchange("gather", conv_w.reshape(-1, LANES), name="gather_conv_w")
    conv_w_full = jnp.moveaxis(cw.reshape(N_DEV, DEPTH, CONV_WIDTH, LANES), 0, 2).reshape(DEPTH, CONV_WIDTH, D_MODEL)

    def layer_params(l, gathered_w_in):
        return {**_w_in_weights(gathered_w_in), **{k: w[k][l] for k in SMALL}, "conv_w": conv_w_full[l]}

    loss_local, grad_x, grads, received = local_step(x, loss_target, layer_params, shards, first_w_in)
    loss = lax.psum(loss_local, ("x", "y", "c"))

    sums = [[sum_slots(t, name=f"sum_g_{n}") for t, n in zip(received[l], names)] for l in range(DEPTH)]
    g_final = {"w_in": jnp.stack([sums[l][0] for l in range(DEPTH)]), "w_ffn_in": jnp.stack([sums[l][1] for l in range(DEPTH)])}
    off = 0
    for k, n in zip(ROW_SHARDED, ROW_COUNTS):
        g_final[k] = jnp.stack([sums[l][2][off:off + n] for l in range(DEPTH)]).reshape(w[k].shape)
        off += n
    grads = {k: jnp.stack([grads[l][k] for l in range(DEPTH)]) for k in list(SMALL) + ["conv_w"]}

    small_names = list(SMALL) + ["conv_w"]
    small_sizes = [grads[k].size for k in small_names]
    svec = _pad_rows(jnp.concatenate([grads[k].reshape(-1) for k in small_names]), SMALL_ROWS_TILE)
    ssum = sum_slots(exchange("gather", svec, name="gather_small_grads"), name="sum_small_grads")
    sflat, off = ssum.reshape(-1), 0
    for k, n in zip(small_names, small_sizes):
        g_final[k] = sflat[off:off + n].reshape(grads[k].shape)
        off += n
    g_final["conv_w"] = lax.dynamic_slice_in_dim(g_final["conv_w"], me * LANES, LANES, axis=2)

    delta, new_m, new_v = {}, {}, {}
    for k in list(BIG) + ["conv_w"]:
        cols = w[k].shape[-1]
        two_d = lambda t: t.reshape(-1, cols)
        d_, m_, v_ = adamw(two_d(w[k]), two_d(g_final[k]), two_d(m[k]), two_d(v[k]), name=f"adamw_{k}")
        delta[k], new_m[k], new_v[k] = d_.reshape(w[k].shape), m_.reshape(w[k].shape), v_.reshape(w[k].shape)
    pack_small = lambda dct: _pad_rows(jnp.concatenate([dct[k].reshape(-1) for k in SMALL]), SMALL_ROWS_TILE)
    d_, m_, v_ = adamw(pack_small(w), pack_small(g_final), pack_small(m), pack_small(v), name="adamw_small")
    off = 0
    for k in SMALL:
        n = w[k].size
        for dst, src in ((delta, d_), (new_m, m_), (new_v, v_)):
            dst[k] = src.reshape(-1)[off:off + n].reshape(w[k].shape)
        off += n
    return (loss, grad_x, *[g_final[k] for k in order], *[delta[k] for k in order], *[new_m[k] for k in order], *[new_v[k] for k in order])
```
